```python
import math
import jax, jax.numpy as jnp
from jax import lax
import numpy as np

D_MODEL = 1024
BATCH = 8
SEQ = 2048
DEPTH = 1
DEC_BATCH = 128
DEC_SEQ = 4
PAST_LEN = 16384
PAGE_SIZE = 128

D_MIX = D_MODEL
D_ATTN = D_MIX // 2
D_SSM = D_MIX - D_ATTN
HEAD_DIM = 64
N_HEADS = D_ATTN // HEAD_DIM
N_KV_HEADS = 2
GQA_GROUP = N_HEADS // N_KV_HEADS
WINDOW = 128
SSM_GROUP = 16
N_SSM_GROUPS = D_SSM // SSM_GROUP
SSM_STATE = 64
N_EXPERTS = 32
TOP_K = 4
D_FF = D_MODEL
SWIGLU_LIMIT = 7.0
SWIGLU_ALPHA = 1.702
MOE_BLOCK = 128
RMS_EPS = 1e-6
NEG_INF = -1e30
D_KV = N_KV_HEADS * HEAD_DIM
D_IN_PROJ = D_ATTN + 2 * D_KV + D_SSM

kernel_name = 'hymba_swa_sink_s5_moe_step'


def _rmsnorm(x, g):
    x32 = x.astype(jnp.float32)
    y = x32 * lax.rsqrt(jnp.mean(x32 * x32, axis=-1, keepdims=True) + RMS_EPS)
    return (y * g.astype(jnp.float32)).astype(x.dtype)


def _project(xn, w_in, q_norm_g, k_norm_g):
    b_, s_ = xn.shape[:2]
    h = xn @ w_in
    q, k, v, u = jnp.split(h, [D_ATTN, D_ATTN + D_KV, D_ATTN + 2 * D_KV], axis=-1)
    q = _rmsnorm(q.reshape(b_, s_, N_KV_HEADS, GQA_GROUP, HEAD_DIM), q_norm_g)
    k = _rmsnorm(k.reshape(b_, s_, N_KV_HEADS, HEAD_DIM), k_norm_g)
    v = v.reshape(b_, s_, N_KV_HEADS, HEAD_DIM)
    return q, k, v, u


def _sink_attention(q, k, v, mask, sinks):
    s = jnp.einsum('...qkgd,...jkd->...kgqj', q.astype(jnp.float32), k.astype(jnp.float32)) * (HEAD_DIM ** -0.5)
    s = jnp.where(mask, s, NEG_INF)
    sink = jnp.broadcast_to(sinks.astype(jnp.float32).reshape(N_KV_HEADS, GQA_GROUP, 1, 1), s.shape[:-1] + (1,))
    p = jax.nn.softmax(jnp.concatenate([s, sink], axis=-1), axis=-1)[..., :-1]
    return jnp.einsum('...kgqj,...jkd->...qkgd', p.astype(v.dtype), v)


def _band_attention(q, k, v, sinks):
    b_, s_ = q.shape[:2]
    nb = s_ // WINDOW
    qb = q.reshape(b_, nb, WINDOW, N_KV_HEADS, GQA_GROUP, HEAD_DIM)

    def with_prev(t):
        t = t.reshape(b_, nb, WINDOW, N_KV_HEADS, HEAD_DIM)
        prev = jnp.concatenate([jnp.zeros_like(t[:, :1]), t[:, :-1]], axis=1)
        return jnp.concatenate([prev, t], axis=2)

    kband, vband = with_prev(k), with_prev(v)
    blk = jnp.arange(nb)[:, None, None]
    qpos = blk * WINDOW + jnp.arange(WINDOW)[None, :, None]
    kpos = (blk - 1) * WINDOW + jnp.arange(2 * WINDOW)[None, None, :]
    mask = (kpos <= qpos) & (kpos > qpos - WINDOW) & (kpos >= 0)
    o = _sink_attention(qb, kband, vband, mask[:, None, None], sinks)
    return o.reshape(b_, s_, D_ATTN)


def _window_cache_attention(q, k, v, cache_k, cache_v, sinks):
    b_, n_new = q.shape[:2]
    n_buf = cache_k.shape[1]
    k_all = jnp.concatenate([cache_k.astype(k.dtype), k], axis=1)
    v_all = jnp.concatenate([cache_v.astype(v.dtype), v], axis=1)
    qpos = n_buf + jnp.arange(n_new)[:, None]
    kpos = jnp.arange(n_buf + n_new)[None, :]
    mask = (kpos <= qpos) & (kpos > qpos - WINDOW)
    o = _sink_attention(q, k_all, v_all, mask, sinks)
    return o.reshape(b_, n_new, D_ATTN), k_all[:, n_new:], v_all[:, n_new:]


def _s5(u, h0_re, h0_im, lam_re, lam_im, b_re, b_im, c_re, c_im, d_skip, log_dt):
    b_, s_ = u.shape[:2]
    u32 = u.astype(jnp.float32).reshape(b_, s_, N_SSM_GROUPS, SSM_GROUP)
    dt = jnp.exp(log_dt.astype(jnp.float32))[:, None]
    l_re = jnp.minimum(lam_re.astype(jnp.float32), -1e-4)
    l_im = lam_im.astype(jnp.float32)
    mag = jnp.exp(l_re * dt)
    a_re = mag * jnp.cos(l_im * dt)
    a_im = mag * jnp.sin(l_im * dt)
    den = l_re * l_re + l_im * l_im
    n_re = a_re - 1.0
    z_re = (n_re * l_re + a_im * l_im) / den
    z_im = (a_im * l_re - n_re * l_im) / den
    br, bi = b_re.astype(jnp.float32), b_im.astype(jnp.float32)
    bb_re = z_re[..., None] * br - z_im[..., None] * bi
    bb_im = z_re[..., None] * bi + z_im[..., None] * br
    bu_re = jnp.einsum('bsgh,gph->bsgp', u32, bb_re)
    bu_im = jnp.einsum('bsgh,gph->bsgp', u32, bb_im)
    h0r, h0i = h0_re.astype(jnp.float32), h0_im.astype(jnp.float32)
    bu_re = bu_re.at[:, 0].add(a_re * h0r - a_im * h0i)
    bu_im = bu_im.at[:, 0].add(a_re * h0i + a_im * h0r)
    ar = jnp.broadcast_to(a_re, bu_re.shape)
    ai = jnp.broadcast_to(a_im, bu_im.shape)

    def combine(e1, e2):
        a1r, a1i, b1r, b1i = e1
        a2r, a2i, b2r, b2i = e2
        return (a2r * a1r - a2i * a1i, a2r * a1i + a2i * a1r,
                a2r * b1r - a2i * b1i + b2r, a2r * b1i + a2i * b1r + b2i)

    _, _, h_re, h_im = lax.associative_scan(combine, (ar, ai, bu_re, bu_im), axis=1)
    y = (jnp.einsum('bsgp,ghp->bsgh', h_re, c_re.astype(jnp.float32))
         - jnp.einsum('bsgp,ghp->bsgh', h_im, c_im.astype(jnp.float32))
         + d_skip.astype(jnp.float32) * u32)
    return y.reshape(b_, s_, D_SSM), h_re[:, -1], h_im[:, -1]


def _glu(y, w_glu, b_glu):
    z = jax.nn.gelu(y) @ w_glu.astype(jnp.float32) + b_glu.astype(jnp.float32)
    a, g = jnp.split(z, 2, axis=-1)
    return a * jax.nn.sigmoid(g)


def _clamped_swiglu(h):
    x_glu = jnp.minimum(h[..., ::2], SWIGLU_LIMIT)
    x_lin = jnp.clip(h[..., 1::2], -SWIGLU_LIMIT, SWIGLU_LIMIT)
    return x_glu * jax.nn.sigmoid(SWIGLU_ALPHA * x_glu) * (x_lin + 1.0)


def _moe(x2d, w_router, b_router, w_gate_up, b_gate_up, w_down, b_down):
    t_ = x2d.shape[0]
    n_assign = t_ * TOP_K
    logits = x2d.astype(jnp.float32) @ w_router.astype(jnp.float32) + b_router.astype(jnp.float32)
    top_v, top_i = lax.top_k(logits, TOP_K)
    gates = jax.nn.softmax(top_v, axis=-1)
    flat_e = top_i.reshape(-1)
    flat_tok = jnp.arange(n_assign, dtype=jnp.int32) // TOP_K
    flat_w = gates.reshape(-1)
    order = jnp.argsort(flat_e)
    sorted_e = flat_e[order]
    counts = jnp.bincount(flat_e, length=N_EXPERTS)
    padded = ((counts + MOE_BLOCK - 1) // MOE_BLOCK) * MOE_BLOCK
    pend = jnp.cumsum(padded)
    pstart = pend - padded
    ustart = jnp.cumsum(counts) - counts
    dest = pstart[sorted_e] + (jnp.arange(n_assign) - ustart[sorted_e])
    n_rows = ((n_assign + MOE_BLOCK - 1) // MOE_BLOCK + N_EXPERTS) * MOE_BLOCK
    row_tok = jnp.zeros((n_rows,), jnp.int32).at[dest].set(flat_tok[order])
    row_w = jnp.zeros((n_rows,), jnp.float32).at[dest].set(flat_w[order])
    nb = n_rows // MOE_BLOCK
    block_start = jnp.arange(nb) * MOE_BLOCK
    block_e = jnp.minimum(jnp.sum(pend[None, :] <= block_start[:, None], axis=1), N_EXPERTS - 1)
    x_rows = x2d[row_tok].reshape(nb, MOE_BLOCK, x2d.shape[-1])

    def expert_block(args):
        xb, e = args
        h = xb @ w_gate_up[e] + b_gate_up[e]
        return _clamped_swiglu(h) @ w_down[e] + b_down[e]

    out = lax.map(expert_block, (x_rows, block_e)).reshape(n_rows, -1)
    y = jax.ops.segment_sum(out.astype(jnp.float32) * row_w[:, None], row_tok, num_segments=t_)
    return y.astype(x2d.dtype)


def _layer(x, lp, cache_k, cache_v, h0_re, h0_im):
    xn = _rmsnorm(x, lp['norm_mix_g'])
    q, k, v, u = _project(xn, lp['w_in'], lp['q_norm_g'], lp['k_norm_g'])
    if cache_k is None:
        attn = _band_attention(q, k, v, lp['attn_sinks'])
        new_k, new_v = k[:, -WINDOW:], v[:, -WINDOW:]
    else:
        attn, new_k, new_v = _window_cache_attention(q, k, v, cache_k, cache_v, lp['attn_sinks'])
    y_ssm, h_re, h_im = _s5(u, h0_re, h0_im, lp['ssm_lambda_re'], lp['ssm_lambda_im'], lp['ssm_b_re'],
                            lp['ssm_b_im'], lp['ssm_c_re'], lp['ssm_c_im'], lp['ssm_d'], lp['ssm_log_dt'])
    ssm_o = _glu(y_ssm, lp['w_glu'], lp['b_glu']).astype(x.dtype)
    merged = jnp.concatenate([_rmsnorm(attn.astype(x.dtype), lp['attn_out_norm_g']),
                              _rmsnorm(ssm_o, lp['ssm_out_norm_g'])], axis=-1)
    x = x + merged @ lp['w_out']
    xn2 = _rmsnorm(x, lp['norm_ffn_g'])
    y_ff = _moe(xn2.reshape(-1, D_MODEL), lp['w_router'], lp['b_router'], lp['w_gate_up'],
                lp['b_gate_up'], lp['w_down'], lp['b_down'])
    x = x + y_ff.reshape(x.shape)
    return x, new_k, new_v, h_re, h_im


def setup_inputs(seed: int = 0) -> dict:
    key = jax.random.key(seed)
    ks = jax.random.split(key, 40)
    f32 = jnp.float32
    win_rows = min(WINDOW, PAST_LEN)

    def nrm(k, shape, scale):
        return jax.random.normal(k, shape, f32) * scale

    def gain(k, width):
        return 1.0 + nrm(k, (DEPTH, width), 0.02)

    lam_im = jnp.broadcast_to(math.pi * jnp.arange(SSM_STATE, dtype=f32), (DEPTH, N_SSM_GROUPS, SSM_STATE))
    return {
        'x_prompt': nrm(ks[0], (BATCH, SEQ, D_MODEL), 1.0),
        'x_sample': nrm(ks[1], (DEC_BATCH, DEC_SEQ, D_MODEL), 1.0),
        'cache_k': nrm(ks[2], (DEPTH, DEC_BATCH, win_rows, N_KV_HEADS, HEAD_DIM), 1.0),
        'cache_v': nrm(ks[3], (DEPTH, DEC_BATCH, win_rows, N_KV_HEADS, HEAD_DIM), 1.0),
        'state_ssm_re': nrm(ks[4], (DEPTH, DEC_BATCH, N_SSM_GROUPS, SSM_STATE), 0.1),
        'state_ssm_im': nrm(ks[5], (DEPTH, DEC_BATCH, N_SSM_GROUPS, SSM_STATE), 0.1),
        'norm_mix_g': gain(ks[6], D_MODEL),
        'w_in': nrm(ks[7], (DEPTH, D_MODEL, D_IN_PROJ), D_MODEL ** -0.5),
        'q_norm_g': gain(ks[8], HEAD_DIM),
        'k_norm_g': gain(ks[9], HEAD_DIM),
        'attn_sinks': nrm(ks[10], (DEPTH, N_HEADS), 0.5),
        'ssm_lambda_re': -0.5 + nrm(ks[11], (DEPTH, N_SSM_GROUPS, SSM_STATE), 0.01),
        'ssm_lambda_im': lam_im + nrm(ks[12], (DEPTH, N_SSM_GROUPS, SSM_STATE), 0.01),
        'ssm_b_re': nrm(ks[13], (DEPTH, N_SSM_GROUPS, SSM_STATE, SSM_GROUP), (2 * SSM_GROUP) ** -0.5),
        'ssm_b_im': nrm(ks[14], (DEPTH, N_SSM_GROUPS, SSM_STATE, SSM_GROUP), (2 * SSM_GROUP) ** -0.5),
        'ssm_c_re': nrm(ks[15], (DEPTH, N_SSM_GROUPS, SSM_GROUP, SSM_STATE), (2 * SSM_STATE) ** -0.5),
        'ssm_c_im': nrm(ks[16], (DEPTH, N_SSM_GROUPS, SSM_GROUP, SSM_STATE), (2 * SSM_STATE) ** -0.5),
        'ssm_d': nrm(ks[17], (DEPTH, N_SSM_GROUPS, SSM_GROUP), 1.0),
        'ssm_log_dt': jax.random.uniform(ks[18], (DEPTH, N_SSM_GROUPS), f32, math.log(1e-3), math.log(1e-1)),
        'w_glu': nrm(ks[19], (DEPTH, D_SSM, 2 * D_SSM), D_SSM ** -0.5),
        'b_glu': nrm(ks[20], (DEPTH, 2 * D_SSM), 0.01),
        'attn_out_norm_g': gain(ks[21], D_ATTN),
        'ssm_out_norm_g': gain(ks[22], D_SSM),
        'w_out': nrm(ks[23], (DEPTH, D_MIX, D_MODEL), D_MIX ** -0.5),
        'norm_ffn_g': gain(ks[24], D_MODEL),
        'w_router': nrm(ks[25], (DEPTH, D_MODEL, N_EXPERTS), D_MODEL ** -0.5),
        'b_router': nrm(ks[26], (DEPTH, N_EXPERTS), 0.01),
        'w_gate_up': nrm(ks[27], (DEPTH, N_EXPERTS, D_MODEL, 2 * D_FF), D_MODEL ** -0.5),
        'b_gate_up': nrm(ks[28], (DEPTH, N_EXPERTS, 2 * D_FF), 0.01),
        'w_down': nrm(ks[29], (DEPTH, N_EXPERTS, D_FF, D_MODEL), D_FF ** -0.5),
        'b_down': nrm(ks[30], (DEPTH, N_EXPERTS, D_MODEL), 0.01),
    }


def reference(x_prompt, x_sample, cache_k, cache_v, state_ssm_re, state_ssm_im, norm_mix_g, w_in,
              q_norm_g, k_norm_g, attn_sinks, ssm_lambda_re, ssm_lambda_im, ssm_b_re, ssm_b_im,
              ssm_c_re, ssm_c_im, ssm_d, ssm_log_dt, w_glu, b_glu, attn_out_norm_g, ssm_out_norm_g,
              w_out, norm_ffn_g, w_router, b_router, w_gate_up, b_gate_up, w_down, b_down):
    xp, xs = x_prompt, x_sample
    kp_l, vp_l, rp_l, ip_l, ks_l, vs_l, rs_l, is_l = [], [], [], [], [], [], [], []
    h0 = jnp.zeros((xp.shape[0], N_SSM_GROUPS, SSM_STATE), jnp.float32)
    for layer in range(DEPTH):
        lp = dict(norm_mix_g=norm_mix_g[layer], w_in=w_in[layer], q_norm_g=q_norm_g[layer],
                  k_norm_g=k_norm_g[layer], attn_sinks=attn_sinks[layer],
                  ssm_lambda_re=ssm_lambda_re[layer], ssm_lambda_im=ssm_lambda_im[layer],
                  ssm_b_re=ssm_b_re[layer], ssm_b_im=ssm_b_im[layer], ssm_c_re=ssm_c_re[layer],
                  ssm_c_im=ssm_c_im[layer], ssm_d=ssm_d[layer], ssm_log_dt=ssm_log_dt[layer],
                  w_glu=w_glu[layer], b_glu=b_glu[layer], attn_out_norm_g=attn_out_norm_g[layer],
                  ssm_out_norm_g=ssm_out_norm_g[layer], w_out=w_out[layer], norm_ffn_g=norm_ffn_g[layer],
                  w_router=w_router[layer], b_router=b_router[layer], w_gate_up=w_gate_up[layer],
                  b_gate_up=b_gate_up[layer], w_down=w_down[layer], b_down=b_down[layer])
        xp, kp, vp, rp, ip = _layer(xp, lp, None, None, h0, h0)
        xs, ksn, vsn, rsn, isn = _layer(xs, lp, cache_k[layer], cache_v[layer],
                                        state_ssm_re[layer], state_ssm_im[layer])
        kp_l.append(kp); vp_l.append(vp); rp_l.append(rp); ip_l.append(ip)
        ks_l.append(ksn); vs_l.append(vsn); rs_l.append(rsn); is_l.append(isn)
    new_k_prompt, new_v_prompt = jnp.stack(kp_l), jnp.stack(vp_l)
    new_ssm_re_prompt, new_ssm_im_prompt = jnp.stack(rp_l), jnp.stack(ip_l)
    new_k_sample, new_v_sample = jnp.stack(ks_l), jnp.stack(vs_l)
    new_ssm_re_sample, new_ssm_im_sample = jnp.stack(rs_l), jnp.stack(is_l)
    return (xp, xs, new_k_prompt, new_v_prompt, new_ssm_re_prompt, new_ssm_im_prompt,
            new_k_sample, new_v_sample, new_ssm_re_sample, new_ssm_im_sample)
```

```python
import functools
import math

import jax
import jax.numpy as jnp
from jax import lax
from jax.experimental import pallas as pl
from jax.experimental.pallas import tpu as pltpu

F32 = jnp.float32
BF16 = jnp.bfloat16

D_MODEL = 1024
D_ATTN = 512
D_SSM = 512
HEAD_DIM = 64
N_HEADS = 8
N_KV_HEADS = 2
D_KV = N_KV_HEADS * HEAD_DIM
WINDOW = 128
SSM_GROUP = 16
N_SSM_GROUPS = 32
SSM_STATE = 64
D_STATE = N_SSM_GROUPS * SSM_STATE
N_EXPERTS = 32
TOP_K = 4
D_FF = 1024
SWIGLU_LIMIT = 7.0
SWIGLU_ALPHA = 1.702
RMS_EPS = 1e-6
NEG_INF = -1e30
D_IN_PROJ = D_ATTN + 2 * D_KV + D_SSM
D_QK = D_ATTN + D_KV

SUBLANES = 8
VMEM_LIMIT = 56 * 1024 * 1024

ROW_TILE = 512
MOE_TILE = 256
S5_CHUNK = 128
S5_PAD = 8
S5_COLS = 512

_Q_PERM = (0, 4, 1, 5, 2, 6, 3, 7)


def _cparams(*sem):
    return pltpu.CompilerParams(dimension_semantics=sem, vmem_limit_bytes=VMEM_LIMIT)


def _dot(a, b):
    return jnp.dot(a, b, preferred_element_type=F32)


def _rms(x):
    return x * lax.rsqrt(jnp.mean(x * x, axis=-1, keepdims=True) + RMS_EPS)


def _in_proj_kernel(x_ref, g_ref, w_ref, qkg_ref, p_ref, q_ref, k_ref, v_ref, u_ref):
    xn = _rms(x_ref[...]) * g_ref[...]
    h = _dot(xn.astype(BF16), w_ref[...])
    qk = h[:, :D_QK]
    sq = (qk * qk).astype(BF16)
    p = p_ref[...]
    ms = jnp.concatenate(
        [_dot(sq[:, 0:256], p), _dot(sq[:, 256:512], p), _dot(sq[:, 512:640], p[:128, :128])],
        axis=-1)
    qkn = qk * lax.rsqrt(ms + RMS_EPS) * qkg_ref[...]
    q_ref[...] = qkn[:, :D_ATTN].astype(BF16)
    k_ref[...] = qkn[:, D_ATTN:]
    v_ref[...] = h[:, D_QK:D_QK + D_KV]
    u_ref[...] = h[:, D_QK + D_KV:]


def _in_proj(x2d, g, w, qkg, pmat):
    t = x2d.shape[0]
    tm = min(ROW_TILE, t)
    row = lambda i: (i, 0)
    fix = lambda i: (0, 0)
    return pl.pallas_call(
        _in_proj_kernel,
        grid=(t // tm,),
        in_specs=[pl.BlockSpec((tm, D_MODEL), row), pl.BlockSpec((1, D_MODEL), fix),
                  pl.BlockSpec((D_MODEL, D_IN_PROJ), fix), pl.BlockSpec((1, D_QK), fix),
                  pl.BlockSpec((256, 256), fix)],
        out_specs=[pl.BlockSpec((tm, D_ATTN), row), pl.BlockSpec((tm, D_KV), row),
                   pl.BlockSpec((tm, D_KV), row), pl.BlockSpec((tm, D_SSM), row)],
        out_shape=[jax.ShapeDtypeStruct((t, D_ATTN), BF16), jax.ShapeDtypeStruct((t, D_KV), F32),
                   jax.ShapeDtypeStruct((t, D_KV), F32), jax.ShapeDtypeStruct((t, D_SSM), F32)],
        compiler_params=_cparams("parallel"),
        name="in_proj",
    )(x2d, g, w, qkg, pmat)


def _softmax_pv(s_blocks, v_blocks, sink):
    m = sink
    for s in s_blocks:
        m = jnp.maximum(m, jnp.max(s, axis=-1, keepdims=True))
    den = jnp.exp(sink - m)
    acc = None
    for s, v in zip(s_blocks, v_blocks):
        p = jnp.exp(s - m)
        den = den + jnp.sum(p, axis=-1, keepdims=True)
        pv = _dot(p.astype(BF16), v)
        acc = pv if acc is None else acc + pv
    return acc / den


def _band_attn_kernel(sink_ref, q_ref, kp_ref, kc_ref, vp_ref, vc_ref, g_ref, o_ref):
    i = pl.program_id(1)
    q = q_ref[...]
    kb = jnp.concatenate([kp_ref[...], kc_ref[...]], axis=0).astype(BF16)
    vb = jnp.concatenate([vp_ref[...], vc_ref[...]], axis=0).astype(BF16)
    row = lax.broadcasted_iota(jnp.int32, (WINDOW, 2 * WINDOW), 0)
    col = lax.broadcasted_iota(jnp.int32, (WINDOW, 2 * WINDOW), 1)
    mask = (col > row) & (col <= row + WINDOW) & ((col >= WINDOW) | (i > 0))
    lane = lax.broadcasted_iota(jnp.int32, (WINDOW, 128), 1)
    low = lane < HEAD_DIM
    zero = jnp.zeros((), BF16)
    outs = []
    for pair in range(N_HEADS // 2):
        qp = q[:, 128 * pair:128 * (pair + 1)]
        halves = []
        for par in range(2):
            qm = jnp.where(low if par == 0 else ~low, qp, zero)
            s = lax.dot_general(qm, kb, (((1,), (1,)), ((), ())), preferred_element_type=F32)
            s = jnp.where(mask, s * (HEAD_DIM ** -0.5), NEG_INF)
            halves.append(_softmax_pv([s], [vb], sink_ref[_Q_PERM[2 * pair + par]]))
        outs.append(jnp.where(low, halves[0], halves[1]))
    o = jnp.concatenate(outs, axis=-1)
    o_ref[...] = (_rms(o) * g_ref[...]).astype(BF16)


def _band_attention(sinks, q, k, v, g):
    b, s, _ = q.shape
    nb = s // WINDOW
    cur = lambda bi, i: (bi, i, 0)
    prev = lambda bi, i: (bi, jnp.maximum(i - 1, 0), 0)
    kv_spec = lambda im: pl.BlockSpec((None, WINDOW, D_KV), im)
    return pl.pallas_call(
        _band_attn_kernel,
        grid=(b, nb),
        in_specs=[pl.BlockSpec(memory_space=pltpu.SMEM),
                  pl.BlockSpec((None, WINDOW, D_ATTN), cur),
                  kv_spec(prev), kv_spec(cur), kv_spec(prev), kv_spec(cur),
                  pl.BlockSpec((1, D_ATTN), lambda bi, i: (0, 0))],
        out_specs=pl.BlockSpec((None, WINDOW, D_ATTN), cur),
        out_shape=jax.ShapeDtypeStruct((b, s, D_ATTN), BF16),
        compiler_params=_cparams("parallel", "parallel"),
        name="band_attn",
    )(sinks, q, k, k, v, v, g)


_PAIR_ROWS = 8
_CACHE_BB = 8


def _cache_attn_kernel(sink_ref, q_ref, kn_ref, vn_ref, ck_ref, cv_ref, g_ref, o_ref, *, n_new):
    n_buf = ck_ref.shape[1]
    rows_blk = _CACHE_BB * n_new
    knew = kn_ref[...].astype(BF16)
    vnew = vn_ref[...].astype(BF16)
    lane = lax.broadcasted_iota(jnp.int32, (_PAIR_ROWS, 128), 1)
    low = lane < HEAD_DIM
    zero = jnp.zeros((), BF16)
    n_stack = N_HEADS * _PAIR_ROWS
    r = lax.broadcasted_iota(jnp.int32, (n_stack, 1), 0) % _PAIR_ROWS
    r_seq, r_tok = r // n_new, r % n_new
    colc = lax.broadcasted_iota(jnp.int32, (n_stack, 2 * n_buf), 1)
    c_seq, c_pos = colc // n_buf, colc % n_buf
    mask_c = (c_seq == r_seq) & (c_pos + WINDOW > r_tok + n_buf)
    coln = lax.broadcasted_iota(jnp.int32, (n_stack, rows_blk), 1)
    sink_col = jnp.concatenate(
        [jnp.full((_PAIR_ROWS, 1), sink_ref[_Q_PERM[h]], F32) for h in range(N_HEADS)], axis=0)
    for sp in range(_CACHE_BB // 2):
        q = q_ref[_PAIR_ROWS * sp:_PAIR_ROWS * (sp + 1), :]
        pieces = []
        for pair in range(N_HEADS // 2):
            qp = q[:, 128 * pair:128 * (pair + 1)]
            pieces.append(jnp.where(low, qp, zero))
            pieces.append(jnp.where(low, zero, qp))
        qs = jnp.concatenate(pieces, axis=0)
        kc = jnp.concatenate([ck_ref[2 * sp], ck_ref[2 * sp + 1]], axis=0).astype(BF16)
        vc = jnp.concatenate([cv_ref[2 * sp], cv_ref[2 * sp + 1]], axis=0).astype(BF16)
        nt = (((1,), (1,)), ((), ()))
        s_c = lax.dot_general(qs, kc, nt, preferred_element_type=F32) * (HEAD_DIM ** -0.5)
        s_n = lax.dot_general(qs, knew, nt, preferred_element_type=F32) * (HEAD_DIM ** -0.5)
        s_c = jnp.where(mask_c, s_c, NEG_INF)
        n_seq, n_tok = coln // n_new - 2 * sp, coln % n_new
        mask_n = (n_seq == r_seq) & (n_tok <= r_tok)
        s_n = jnp.where(mask_n, s_n, NEG_INF)
        o = _softmax_pv([s_c, s_n], [vc, vnew], sink_col)
        outs = [jnp.where(low, o[16 * pair:16 * pair + 8], o[16 * pair + 8:16 * pair + 16])
                for pair in range(N_HEADS // 2)]
        oo = jnp.concatenate(outs, axis=-1)
        o_ref[_PAIR_ROWS * sp:_PAIR_ROWS * (sp + 1), :] = (_rms(oo) * g_ref[...]).astype(BF16)


def _cache_attention(sinks, q, k, v, cache_k, cache_v, g, n_new):
    t = q.shape[0]
    nb, n_buf, _ = cache_k.shape
    assert n_new * 2 == _PAIR_ROWS and nb % _CACHE_BB == 0 and n_buf == WINDOW
    rows = _CACHE_BB * n_new
    row = lambda i: (i, 0)
    cache_spec = pl.BlockSpec((_CACHE_BB, n_buf, D_KV), lambda i: (i, 0, 0))
    return pl.pallas_call(
        functools.partial(_cache_attn_kernel, n_new=n_new),
        grid=(nb // _CACHE_BB,),
        in_specs=[pl.BlockSpec(memory_space=pltpu.SMEM),
                  pl.BlockSpec((rows, D_ATTN), row), pl.BlockSpec((rows, D_KV), row),
                  pl.BlockSpec((rows, D_KV), row), cache_spec, cache_spec,
                  pl.BlockSpec((1, D_ATTN), lambda i: (0, 0))],
        out_specs=pl.BlockSpec((rows, D_ATTN), row),
        out_shape=jax.ShapeDtypeStruct((t, D_ATTN), BF16),
        compiler_params=_cparams("parallel"),
        name="cache_attn",
    )(sinks, q, k, v, cache_k, cache_v, g)


def _s5_prep_kernel(lre_ref, lim_ref, ldt_ref, bre_ref, bim_ref, are_ref, aim_ref, bbre_ref, bbim_ref):
    dt = jnp.exp(ldt_ref[...])
    l_re = jnp.minimum(lre_ref[...], -1e-4)
    l_im = lim_ref[...]
    mag = jnp.exp(l_re * dt)
    a_re = mag * jnp.cos(l_im * dt)
    a_im = mag * jnp.sin(l_im * dt)
    den = l_re * l_re + l_im * l_im
    n_re = a_re - 1.0
    z_re = (n_re * l_re + a_im * l_im) / den
    z_im = (a_im * l_re - n_re * l_im) / den
    are_ref[...] = a_re
    aim_ref[...] = a_im
    br, bi = bre_ref[...], bim_ref[...]
    zr, zi = z_re[:, None, :], z_im[:, None, :]
    bbre_ref[...] = zr * br - zi * bi
    bbim_ref[...] = zr * bi + zi * br


def _s5_prep(lam_re, lam_im, log_dt, b_re_t, b_im_t):
    g, p = lam_re.shape
    sd = jax.ShapeDtypeStruct
    return pl.pallas_call(
        _s5_prep_kernel,
        out_shape=[sd((g, p), F32), sd((g, p), F32), sd(b_re_t.shape, F32), sd(b_re_t.shape, F32)],
        name="s5_prep",
    )(lam_re, lam_im, log_dt, b_re_t, b_im_t)


def _s5_kernel(u_ref, h0r_ref, h0i_ref, ar_ref, ai_ref, wb_ref, wc_ref, d_ref, wglu_ref, bglu_ref,
               g_ref, o_ref, hr_ref, hi_ref, bu_ref, hs_ref, *, tt, stride):
    j = pl.program_id(1)
    rows = SUBLANES * tt
    n_tiles = D_STATE // 128

    @pl.when(j == 0)
    def _():
        hs_ref[:, :D_STATE] = h0r_ref[...]
        hs_ref[:, D_STATE:] = h0i_ref[...]

    u = u_ref[...].reshape(rows, D_SSM)
    ub = u.astype(BF16)

    def put(c, val):
        if stride == tt:
            bu_ref[c, 0:rows, :] = val
        else:
            for b in range(SUBLANES):
                bu_ref[c, b * stride:b * stride + tt, :] = val[b * tt:(b + 1) * tt]

    def get(c):
        if stride == tt:
            return bu_ref[c, 0:rows, :]
        return jnp.concatenate([bu_ref[c, b * stride:b * stride + tt, :] for b in range(SUBLANES)], axis=0)

    for n in range(2 * D_STATE // 256):
        band = (n % (D_STATE // 256)) // 2
        res = _dot(ub[:, 128 * band:128 * (band + 1)],
                   wb_ref[128 * band:128 * (band + 1), 256 * n:256 * (n + 1)])
        put(2 * n, res[:, :128])
        put(2 * n + 1, res[:, 128:])

    tiles_per_pass = S5_COLS // 128
    for c0 in range(0, n_tiles, tiles_per_pass):
        tiles = range(c0, c0 + tiles_per_pass)
        a_r = [jnp.broadcast_to(ar_ref[:, 128 * c:128 * (c + 1)], (SUBLANES, 128)) for c in tiles]
        a_i = [jnp.broadcast_to(ai_ref[:, 128 * c:128 * (c + 1)], (SUBLANES, 128)) for c in tiles]

        def step(t, carry, tiles=tiles, a_r=a_r, a_i=a_i):
            at_t = pl.ds(t, SUBLANES, stride=stride)
            out = []
            for k, c in enumerate(tiles):
                h_r, h_i = carry[2 * k], carry[2 * k + 1]
                n_r = a_r[k] * h_r - a_i[k] * h_i + bu_ref[c, at_t, :]
                n_i = a_r[k] * h_i + a_i[k] * h_r + bu_ref[n_tiles + c, at_t, :]
                bu_ref[c, at_t, :] = n_r
                bu_ref[n_tiles + c, at_t, :] = n_i
                out += [n_r, n_i]
            return tuple(out)

        init = []
        for c in tiles:
            init += [hs_ref[:, 128 * c:128 * (c + 1)], hs_ref[:, D_STATE + 128 * c:D_STATE + 128 * (c + 1)]]
        fin = lax.fori_loop(0, tt, step, tuple(init), unroll=min(tt, 8))
        for k, c in enumerate(tiles):
            hs_ref[:, 128 * c:128 * (c + 1)] = fin[2 * k]
            hs_ref[:, D_STATE + 128 * c:D_STATE + 128 * (c + 1)] = fin[2 * k + 1]

    def h_cols(first_tile):
        return jnp.concatenate([get(first_tile + k) for k in range(4)], axis=-1).astype(BF16)

    ys = []
    for m in range(D_SSM // 128):
        y = _dot(h_cols(4 * m), wc_ref[512 * m:512 * (m + 1), 128 * m:128 * (m + 1)])
        y = y + _dot(h_cols(n_tiles + 4 * m),
                     wc_ref[D_STATE + 512 * m:D_STATE + 512 * (m + 1), 128 * m:128 * (m + 1)])
        ys.append(y)
    y = jnp.concatenate(ys, axis=-1) + d_ref[...] * u
    z = _dot(jax.nn.gelu(y).astype(BF16), wglu_ref[...]) + bglu_ref[...]
    s = z[:, :D_SSM] * jax.nn.sigmoid(z[:, D_SSM:])
    o_ref[...] = (_rms(s) * g_ref[...]).astype(BF16).reshape(o_ref.shape)

    @pl.when(j == pl.num_programs(1) - 1)
    def _():
        hr_ref[...] = hs_ref[:, :D_STATE]
        hi_ref[...] = hs_ref[:, D_STATE:]


def _s5(u, h0r, h0i, a_re, a_im, wb, wc, d, wglu, bglu, g, *, tt, stride, time_chunked):
    nbg = h0r.shape[0] // SUBLANES
    if time_chunked:
        nchunks = u.shape[1] // tt
        u_spec = pl.BlockSpec((SUBLANES, tt, D_SSM), lambda gi, j: (gi, j, 0))
    else:
        nchunks = 1
        u_spec = pl.BlockSpec((None, SUBLANES * tt, D_SSM), lambda gi, j: (gi, 0, 0))
    fix = lambda gi, j: (0, 0)
    st_spec = pl.BlockSpec((SUBLANES, D_STATE), lambda gi, j: (gi, 0))
    sd = jax.ShapeDtypeStruct
    return pl.pallas_call(
        functools.partial(_s5_kernel, tt=tt, stride=stride),
        grid=(nbg, nchunks),
        in_specs=[u_spec, st_spec, st_spec,
                  pl.BlockSpec((1, D_STATE), fix), pl.BlockSpec((1, D_STATE), fix),
                  pl.BlockSpec((D_SSM, 2 * D_STATE), fix), pl.BlockSpec((2 * D_STATE, D_SSM), fix),
                  pl.BlockSpec((1, D_SSM), fix), pl.BlockSpec((D_SSM, 2 * D_SSM), fix),
                  pl.BlockSpec((1, 2 * D_SSM), fix), pl.BlockSpec((1, D_SSM), fix)],
        out_specs=[u_spec, st_spec, st_spec],
        out_shape=[sd(u.shape, BF16), sd(h0r.shape, F32), sd(h0r.shape, F32)],
        scratch_shapes=[pltpu.VMEM((2 * D_STATE // 128, SUBLANES * stride, 128), F32),
                        pltpu.VMEM((SUBLANES, 2 * D_STATE), F32)],
        compiler_params=_cparams("parallel", "arbitrary"),
        name="s5",
    )(u, h0r, h0i, a_re, a_im, wb, wc, d, wglu, bglu, g)


def _out_proj_kernel(a_ref, s_ref, x_ref, wa_ref, ws_ref, g_ref, wr_ref, br_ref, x1_ref, xn_ref, lg_ref):
    x1 = x_ref[...] + _dot(a_ref[...], wa_ref[...]) + _dot(s_ref[...], ws_ref[...])
    x1_ref[...] = x1
    xn = (_rms(x1) * g_ref[...]).astype(BF16)
    xn_ref[...] = xn
    lg_ref[...] = _dot(xn, wr_ref[...]) + br_ref[...]


def _out_proj(attn_n, ssm_n, x2d, wa, ws, g, wr, br):
    t = x2d.shape[0]
    tm = min(ROW_TILE, t)
    row = lambda i: (i, 0)
    fix = lambda i: (0, 0)
    sd = jax.ShapeDtypeStruct
    return pl.pallas_call(
        _out_proj_kernel,
        grid=(t // tm,),
        in_specs=[pl.BlockSpec((tm, D_ATTN), row), pl.BlockSpec((tm, D_SSM), row),
                  pl.BlockSpec((tm, D_MODEL), row),
                  pl.BlockSpec((D_ATTN, D_MODEL), fix), pl.BlockSpec((D_SSM, D_MODEL), fix),
                  pl.BlockSpec((1, D_MODEL), fix), pl.BlockSpec((D_MODEL, 128), fix),
                  pl.BlockSpec((1, 128), fix)],
        out_specs=[pl.BlockSpec((tm, D_MODEL), row), pl.BlockSpec((tm, D_MODEL), row),
                   pl.BlockSpec((tm, 128), row)],
        out_shape=[sd((t, D_MODEL), F32), sd((t, D_MODEL), BF16), sd((t, 128), F32)],
        compiler_params=_cparams("parallel"),
        name="out_proj",
    )(attn_n, ssm_n, x2d, wa, ws, g, wr, br)


def _moe_kernel(be_ref, nu_ref, x_ref, wg_ref, wl_ref, bg_ref, bl_ref, wd_ref, bd_ref, rw_ref, o_ref):
    i = pl.program_id(0)

    @pl.when(i < nu_ref[0])
    def _():
        x = x_ref[...]
        glu = jnp.minimum(_dot(x, wg_ref[...]) + bg_ref[...], SWIGLU_LIMIT)
        lin = jnp.clip(_dot(x, wl_ref[...]) + bl_ref[...], -SWIGLU_LIMIT, SWIGLU_LIMIT)
        act = glu * jax.nn.sigmoid(SWIGLU_ALPHA * glu) * (lin + 1.0)
        out = _dot(act.astype(BF16), wd_ref[...]) + bd_ref[...]
        o_ref[...] = out * rw_ref[...]

    @pl.when(i >= nu_ref[0])
    def _():
        o_ref[...] = jnp.zeros_like(o_ref)


def _moe(block_e, n_used, x_rows, wg, wl, bg, bl, wd, bd, row_w):
    n_rows = x_rows.shape[0]
    nblk = n_rows // MOE_TILE
    row = lambda i, be, nu: (jnp.minimum(i, nu[0] - 1), 0)
    wsel = lambda i, be, nu: (be[i], 0, 0)
    grid_spec = pltpu.PrefetchScalarGridSpec(
        num_scalar_prefetch=2,
        grid=(nblk,),
        in_specs=[pl.BlockSpec((MOE_TILE, D_MODEL), row),
                  pl.BlockSpec((None, D_MODEL, D_FF), wsel), pl.BlockSpec((None, D_MODEL, D_FF), wsel),
                  pl.BlockSpec((None, 1, D_FF), wsel), pl.BlockSpec((None, 1, D_FF), wsel),
                  pl.BlockSpec((None, D_FF, D_MODEL), wsel), pl.BlockSpec((None, 1, D_MODEL), wsel),
                  pl.BlockSpec((MOE_TILE, 1), row)],
        out_specs=pl.BlockSpec((MOE_TILE, D_MODEL), lambda i, be, nu: (i, 0)),
    )
    return pl.pallas_call(
        _moe_kernel,
        grid_spec=grid_spec,
        out_shape=jax.ShapeDtypeStruct((n_rows, D_MODEL), F32),
        compiler_params=_cparams("arbitrary"),
        name="moe",
    )(block_e, n_used, x_rows, wg, wl, bg, bl, wd, bd, row_w)


def _route(logits):
    t = logits.shape[0]
    n_assign = t * TOP_K
    top_v, top_i = lax.top_k(logits, TOP_K)
    gates = jax.nn.softmax(top_v, axis=-1)
    flat_e = top_i.reshape(-1)
    onehot = (flat_e[:, None] == jnp.arange(N_EXPERTS, dtype=flat_e.dtype)[None, :]).astype(jnp.int32)
    csum = jnp.cumsum(onehot, axis=0)
    counts = csum[-1]
    rank = jnp.take_along_axis(csum, flat_e[:, None], axis=1)[:, 0] - 1
    padded = ((counts + MOE_TILE - 1) // MOE_TILE) * MOE_TILE
    pend = jnp.cumsum(padded)
    pstart = pend - padded
    dest = (pstart[flat_e] + rank).astype(jnp.int32)
    nblk = (n_assign + MOE_TILE - 1) // MOE_TILE + N_EXPERTS
    n_rows = nblk * MOE_TILE
    row_tok = jnp.zeros((n_rows,), jnp.int32).at[dest].set(jnp.arange(n_assign, dtype=jnp.int32) // TOP_K)
    row_w = jnp.zeros((n_rows,), F32).at[dest].set(gates.reshape(-1))
    n_used = (pend[-1] // MOE_TILE).astype(jnp.int32)
    block_start = jnp.arange(nblk, dtype=jnp.int32) * MOE_TILE
    block_e = jnp.sum(pend[None, :] <= jnp.minimum(block_start, pend[-1] - 1)[:, None], axis=1)
    block_e = jnp.minimum(block_e, N_EXPERTS - 1).astype(jnp.int32)
    return row_tok, row_w, dest.reshape(t, TOP_K), block_e, n_used.reshape(1)


def kernel(x_prompt, x_sample, cache_k, cache_v, state_ssm_re, state_ssm_im, norm_mix_g, w_in, q_norm_g, k_norm_g, attn_sinks, ssm_lambda_re, ssm_lambda_im, ssm_b_re, ssm_b_im, ssm_c_re, ssm_c_im, ssm_d, ssm_log_dt, w_glu, b_glu, attn_out_norm_g, ssm_out_norm_g, w_out, norm_ffn_g, w_router, b_router, w_gate_up, b_gate_up, w_down, b_down):
    depth = w_in.shape[0]
    assert depth == 1
    bp, sp, _ = x_prompt.shape
    bs, ss, _ = x_sample.shape
    tp, ts = bp * sp, bs * ss
    assert bp == SUBLANES and sp % S5_CHUNK == 0 and bs % SUBLANES == 0

    perm = jnp.asarray(_Q_PERM)
    w_in0 = w_in[0]
    wq = w_in0[:, :D_ATTN].reshape(D_MODEL, N_HEADS, HEAD_DIM)[:, perm].reshape(D_MODEL, D_ATTN)
    w_in_b = jnp.concatenate([wq, w_in0[:, D_ATTN:]], axis=1).astype(BF16)
    qkg = jnp.concatenate([jnp.tile(q_norm_g[0], N_HEADS), jnp.tile(k_norm_g[0], N_KV_HEADS)])[None]
    pmat = jnp.kron(jnp.eye(256 // HEAD_DIM, dtype=F32),
                    jnp.full((HEAD_DIM, HEAD_DIM), 1.0 / HEAD_DIM, F32)).astype(BF16)
    g_mix = norm_mix_g[0][None]
    sinks = attn_sinks[0]
    g_attn = attn_out_norm_g[0].reshape(N_HEADS, HEAD_DIM)[perm].reshape(1, D_ATTN)
    w_out0 = w_out[0]
    w_out_a = w_out0[:D_ATTN].reshape(N_HEADS, HEAD_DIM, D_MODEL)[perm].reshape(D_ATTN, D_MODEL).astype(BF16)
    w_out_s = w_out0[D_ATTN:].astype(BF16)
    g_ssm = ssm_out_norm_g[0][None]
    g_ffn = norm_ffn_g[0][None]
    w_r = jnp.pad(w_router[0], ((0, 0), (0, 128 - N_EXPERTS))).astype(BF16)
    b_r = jnp.pad(b_router[0], (0, 128 - N_EXPERTS), constant_values=NEG_INF)[None]

    a_re, a_im, bb_re, bb_im = _s5_prep(
        ssm_lambda_re[0], ssm_lambda_im[0], ssm_log_dt[0][:, None],
        jnp.swapaxes(ssm_b_re[0], 1, 2), jnp.swapaxes(ssm_b_im[0], 1, 2))
    eye_g = jnp.eye(N_SSM_GROUPS, dtype=F32)
    bd_b = lambda bb: jnp.einsum("ghp,gk->ghkp", bb, eye_g).reshape(D_SSM, D_STATE)
    wb = jnp.concatenate([bd_b(bb_re), bd_b(bb_im)], axis=1).astype(BF16)
    bd_c = lambda c: jnp.einsum("ghp,gk->gpkh", c, eye_g).reshape(D_STATE, D_SSM)
    wc = jnp.concatenate([bd_c(ssm_c_re[0]), -bd_c(ssm_c_im[0])], axis=0).astype(BF16)
    a_re, a_im = a_re.reshape(1, D_STATE), a_im.reshape(1, D_STATE)
    d_skip = ssm_d[0].reshape(1, D_SSM)
    w_glu_b = w_glu[0].astype(BF16)
    b_glu0 = b_glu[0][None]

    wgu = w_gate_up[0]
    w_g = wgu[:, :, 0::2].astype(BF16)
    w_l = wgu[:, :, 1::2].astype(BF16)
    b_g = b_gate_up[0][:, None, 0::2]
    b_l = b_gate_up[0][:, None, 1::2]
    w_d = w_down[0].astype(BF16)
    b_d = b_down[0][:, None, :]

    xp2 = x_prompt.reshape(tp, D_MODEL)
    xs2 = x_sample.reshape(ts, D_MODEL)
    qp, kp, vp, up = _in_proj(xp2, g_mix, w_in_b, qkg, pmat)
    qs, ks, vs, us = _in_proj(xs2, g_mix, w_in_b, qkg, pmat)

    kp3, vp3 = kp.reshape(bp, sp, D_KV), vp.reshape(bp, sp, D_KV)
    attn_p = _band_attention(sinks, qp.reshape(bp, sp, D_ATTN), kp3, vp3, g_attn).reshape(tp, D_ATTN)
    ck = cache_k[0].reshape(bs, -1, D_KV)
    cv = cache_v[0].reshape(bs, -1, D_KV)
    attn_s = _cache_attention(sinks, qs, ks, vs, ck, cv, g_attn, ss)

    zeros_p = jnp.zeros((bp, D_STATE), F32)
    s5_args = (a_re, a_im, wb, wc, d_skip, w_glu_b, b_glu0, g_ssm)
    ssm_p, hr_p, hi_p = _s5(up.reshape(bp, sp, D_SSM), zeros_p, zeros_p, *s5_args,
                            tt=S5_CHUNK, stride=S5_CHUNK + S5_PAD, time_chunked=True)
    ssm_s, hr_s, hi_s = _s5(us.reshape(bs // SUBLANES, SUBLANES * ss, D_SSM),
                            state_ssm_re[0].reshape(bs, D_STATE), state_ssm_im[0].reshape(bs, D_STATE),
                            *s5_args, tt=ss, stride=ss, time_chunked=False)

    x1p, xnp_, lgp = _out_proj(attn_p, ssm_p.reshape(tp, D_SSM), xp2, w_out_a, w_out_s, g_ffn, w_r, b_r)
    x1s, xns, lgs = _out_proj(attn_s, ssm_s.reshape(ts, D_SSM), xs2, w_out_a, w_out_s, g_ffn, w_r, b_r)

    xn_all = jnp.concatenate([xnp_, xns], axis=0)
    logits = jnp.concatenate([lgp, lgs], axis=0)[:, :N_EXPERTS]
    row_tok, row_w, pos, block_e, n_used = _route(logits)
    out_rows = _moe(block_e, n_used, xn_all[row_tok], w_g, w_l, b_g, b_l, w_d, b_d, row_w[:, None])
    y_ff = jnp.sum(out_rows[pos], axis=1)
    yp = (x1p + y_ff[:tp]).reshape(bp, sp, D_MODEL)
    ys = (x1s + y_ff[tp:]).reshape(bs, ss, D_MODEL)

    kv5 = lambda a, b_: a.reshape(b_, -1, N_KV_HEADS, HEAD_DIM)
    new_kp = kv5(kp3[:, -WINDOW:], bp)[None]
    new_vp = kv5(vp3[:, -WINDOW:], bp)[None]
    ks3, vs3 = ks.reshape(bs, ss, D_KV), vs.reshape(bs, ss, D_KV)
    new_ks = kv5(jnp.concatenate([ck, ks3], axis=1)[:, ss:], bs)[None]
    new_vs = kv5(jnp.concatenate([cv, vs3], axis=1)[:, ss:], bs)[None]
    st = lambda h, b_: h.reshape(1, b_, N_SSM_GROUPS, SSM_STATE)
    return (yp, ys, new_kp, new_vp, st(hr_p, bp), st(hi_p, bp),
            new_ks, new_vs, st(hr_s, bs), st(hi_s, bs))
```

```python
import functools
import math

import jax
import jax.numpy as jnp
from jax import lax
from jax.experimental import pallas as pl
from jax.experimental.pallas import tpu as pltpu

F32 = jnp.float32
BF16 = jnp.bfloat16

D_MODEL = 1024
D_ATTN = 512
D_SSM = 512
HEAD_DIM = 64
N_HEADS = 8
N_KV_HEADS = 2
D_KV = N_KV_HEADS * HEAD_DIM
WINDOW = 128
SSM_GROUP = 16
N_SSM_GROUPS = 32
SSM_STATE = 64
D_STATE = N_SSM_GROUPS * SSM_STATE
N_EXPERTS = 32
TOP_K = 4
D_FF = 1024
SWIGLU_LIMIT = 7.0
SWIGLU_ALPHA = 1.702
RMS_EPS = 1e-6
NEG_INF = -1e30
D_IN_PROJ = D_ATTN + 2 * D_KV + D_SSM
D_QK = D_ATTN + D_KV

SUBLANES = 8
VMEM_LIMIT = 56 * 1024 * 1024

ROW_TILE = 512
MOE_TILE = 256
COMBINE_TILE = 128
S5_CHUNK = 128
S5_PAD = 8
S5_COLS = 512

_Q_PERM = (0, 4, 1, 5, 2, 6, 3, 7)


def _cparams(*sem):
    return pltpu.CompilerParams(dimension_semantics=sem, vmem_limit_bytes=VMEM_LIMIT)


def _dot(a, b):
    return jnp.dot(a, b, preferred_element_type=F32)


def _rms(x):
    return x * lax.rsqrt(jnp.mean(x * x, axis=-1, keepdims=True) + RMS_EPS)


def _in_proj_kernel(x_ref, g_ref, w_ref, qkg_ref, p_ref, q_ref, k_ref, v_ref, u_ref):
    xn = _rms(x_ref[...]) * g_ref[...]
    h = _dot(xn.astype(BF16), w_ref[...])
    qk = h[:, :D_QK]
    sq = (qk * qk).astype(BF16)
    p = p_ref[...]
    ms = jnp.concatenate(
        [_dot(sq[:, 0:256], p), _dot(sq[:, 256:512], p), _dot(sq[:, 512:640], p[:128, :128])],
        axis=-1)
    qkn = qk * lax.rsqrt(ms + RMS_EPS) * qkg_ref[...]
    q_ref[...] = qkn[:, :D_ATTN].astype(BF16)
    k_ref[...] = qkn[:, D_ATTN:]
    v_ref[...] = h[:, D_QK:D_QK + D_KV]
    u_ref[...] = h[:, D_QK + D_KV:]


def _in_proj(x2d, g, w, qkg, pmat):
    t = x2d.shape[0]
    tm = min(ROW_TILE, t)
    row = lambda i: (i, 0)
    fix = lambda i: (0, 0)
    return pl.pallas_call(
        _in_proj_kernel,
        grid=(t // tm,),
        in_specs=[pl.BlockSpec((tm, D_MODEL), row), pl.BlockSpec((1, D_MODEL), fix),
                  pl.BlockSpec((D_MODEL, D_IN_PROJ), fix), pl.BlockSpec((1, D_QK), fix),
                  pl.BlockSpec((256, 256), fix)],
        out_specs=[pl.BlockSpec((tm, D_ATTN), row), pl.BlockSpec((tm, D_KV), row),
                   pl.BlockSpec((tm, D_KV), row), pl.BlockSpec((tm, D_SSM), row)],
        out_shape=[jax.ShapeDtypeStruct((t, D_ATTN), BF16), jax.ShapeDtypeStruct((t, D_KV), F32),
                   jax.ShapeDtypeStruct((t, D_KV), F32), jax.ShapeDtypeStruct((t, D_SSM), F32)],
        compiler_params=_cparams("parallel"),
        name="in_proj",
    )(x2d, g, w, qkg, pmat)


def _softmax_pv(s_blocks, v_blocks, sink):
    m = sink
    for s in s_blocks:
        m = jnp.maximum(m, jnp.max(s, axis=-1, keepdims=True))
    den = jnp.exp(sink - m)
    acc = None
    for s, v in zip(s_blocks, v_blocks):
        p = jnp.exp(s - m)
        den = den + jnp.sum(p, axis=-1, keepdims=True)
        pv = _dot(p.astype(BF16), v)
        acc = pv if acc is None else acc + pv
    return acc / den


def _band_attn_kernel(sink_ref, q_ref, kp_ref, kc_ref, vp_ref, vc_ref, g_ref, o_ref):
    i = pl.program_id(1)
    q = q_ref[...]
    kb = jnp.concatenate([kp_ref[...], kc_ref[...]], axis=0).astype(BF16)
    vb = jnp.concatenate([vp_ref[...], vc_ref[...]], axis=0).astype(BF16)
    row = lax.broadcasted_iota(jnp.int32, (WINDOW, 2 * WINDOW), 0)
    col = lax.broadcasted_iota(jnp.int32, (WINDOW, 2 * WINDOW), 1)
    mask = (col > row) & (col <= row + WINDOW) & ((col >= WINDOW) | (i > 0))
    lane = lax.broadcasted_iota(jnp.int32, (WINDOW, 128), 1)
    low = lane < HEAD_DIM
    zero = jnp.zeros((), BF16)
    outs = []
    for pair in range(N_HEADS // 2):
        qp = q[:, 128 * pair:128 * (pair + 1)]
        halves = []
        for par in range(2):
            qm = jnp.where(low if par == 0 else ~low, qp, zero)
            s = lax.dot_general(qm, kb, (((1,), (1,)), ((), ())), preferred_element_type=F32)
            s = jnp.where(mask, s * (HEAD_DIM ** -0.5), NEG_INF)
            halves.append(_softmax_pv([s], [vb], sink_ref[_Q_PERM[2 * pair + par]]))
        outs.append(jnp.where(low, halves[0], halves[1]))
    o = jnp.concatenate(outs, axis=-1)
    o_ref[...] = (_rms(o) * g_ref[...]).astype(BF16)


def _band_attention(sinks, q, k, v, g):
    b, s, _ = q.shape
    nb = s // WINDOW
    cur = lambda bi, i: (bi, i, 0)
    prev = lambda bi, i: (bi, jnp.maximum(i - 1, 0), 0)
    kv_spec = lambda im: pl.BlockSpec((None, WINDOW, D_KV), im)
    return pl.pallas_call(
        _band_attn_kernel,
        grid=(b, nb),
        in_specs=[pl.BlockSpec(memory_space=pltpu.SMEM),
                  pl.BlockSpec((None, WINDOW, D_ATTN), cur),
                  kv_spec(prev), kv_spec(cur), kv_spec(prev), kv_spec(cur),
                  pl.BlockSpec((1, D_ATTN), lambda bi, i: (0, 0))],
        out_specs=pl.BlockSpec((None, WINDOW, D_ATTN), cur),
        out_shape=jax.ShapeDtypeStruct((b, s, D_ATTN), BF16),
        compiler_params=_cparams("parallel", "parallel"),
        name="band_attn",
    )(sinks, q, k, k, v, v, g)


_PAIR_ROWS = 8
_CACHE_BB = 8


def _cache_attn_kernel(sink_ref, q_ref, kn_ref, vn_ref, ck_ref, cv_ref, g_ref, o_ref, *, n_new):
    n_buf = ck_ref.shape[1]
    rows_blk = _CACHE_BB * n_new
    knew = kn_ref[...].astype(BF16)
    vnew = vn_ref[...].astype(BF16)
    lane = lax.broadcasted_iota(jnp.int32, (_PAIR_ROWS, 128), 1)
    low = lane < HEAD_DIM
    zero = jnp.zeros((), BF16)
    n_stack = N_HEADS * _PAIR_ROWS
    r = lax.broadcasted_iota(jnp.int32, (n_stack, 1), 0) % _PAIR_ROWS
    r_seq, r_tok = r // n_new, r % n_new
    colc = lax.broadcasted_iota(jnp.int32, (n_stack, 2 * n_buf), 1)
    c_seq, c_pos = colc // n_buf, colc % n_buf
    mask_c = (c_seq == r_seq) & (c_pos + WINDOW > r_tok + n_buf)
    coln = lax.broadcasted_iota(jnp.int32, (n_stack, rows_blk), 1)
    sink_col = jnp.concatenate(
        [jnp.full((_PAIR_ROWS, 1), sink_ref[_Q_PERM[h]], F32) for h in range(N_HEADS)], axis=0)
    for sp in range(_CACHE_BB // 2):
        q = q_ref[_PAIR_ROWS * sp:_PAIR_ROWS * (sp + 1), :]
        pieces = []
        for pair in range(N_HEADS // 2):
            qp = q[:, 128 * pair:128 * (pair + 1)]
            pieces.append(jnp.where(low, qp, zero))
            pieces.append(jnp.where(low, zero, qp))
        qs = jnp.concatenate(pieces, axis=0)
        kc = jnp.concatenate([ck_ref[2 * sp], ck_ref[2 * sp + 1]], axis=0).astype(BF16)
        vc = jnp.concatenate([cv_ref[2 * sp], cv_ref[2 * sp + 1]], axis=0).astype(BF16)
        nt = (((1,), (1,)), ((), ()))
        s_c = lax.dot_general(qs, kc, nt, preferred_element_type=F32) * (HEAD_DIM ** -0.5)
        s_n = lax.dot_general(qs, knew, nt, preferred_element_type=F32) * (HEAD_DIM ** -0.5)
        s_c = jnp.where(mask_c, s_c, NEG_INF)
        n_seq, n_tok = coln // n_new - 2 * sp, coln % n_new
        mask_n = (n_seq == r_seq) & (n_tok <= r_tok)
        s_n = jnp.where(mask_n, s_n, NEG_INF)
        o = _softmax_pv([s_c, s_n], [vc, vnew], sink_col)
        outs = [jnp.where(low, o[16 * pair:16 * pair + 8], o[16 * pair + 8:16 * pair + 16])
                for pair in range(N_HEADS // 2)]
        oo = jnp.concatenate(outs, axis=-1)
        o_ref[_PAIR_ROWS * sp:_PAIR_ROWS * (sp + 1), :] = (_rms(oo) * g_ref[...]).astype(BF16)


def _cache_attention(sinks, q, k, v, cache_k, cache_v, g, n_new):
    t = q.shape[0]
    nb, n_buf, _ = cache_k.shape
    assert n_new * 2 == _PAIR_ROWS and nb % _CACHE_BB == 0 and n_buf == WINDOW
    rows = _CACHE_BB * n_new
    row = lambda i: (i, 0)
    cache_spec = pl.BlockSpec((_CACHE_BB, n_buf, D_KV), lambda i: (i, 0, 0))
    return pl.pallas_call(
        functools.partial(_cache_attn_kernel, n_new=n_new),
        grid=(nb // _CACHE_BB,),
        in_specs=[pl.BlockSpec(memory_space=pltpu.SMEM),
                  pl.BlockSpec((rows, D_ATTN), row), pl.BlockSpec((rows, D_KV), row),
                  pl.BlockSpec((rows, D_KV), row), cache_spec, cache_spec,
                  pl.BlockSpec((1, D_ATTN), lambda i: (0, 0))],
        out_specs=pl.BlockSpec((rows, D_ATTN), row),
        out_shape=jax.ShapeDtypeStruct((t, D_ATTN), BF16),
        compiler_params=_cparams("parallel"),
        name="cache_attn",
    )(sinks, q, k, v, cache_k, cache_v, g)


def _s5_prep_kernel(lre_ref, lim_ref, ldt_ref, bre_ref, bim_ref, are_ref, aim_ref, bbre_ref, bbim_ref):
    dt = jnp.exp(ldt_ref[...])
    l_re = jnp.minimum(lre_ref[...], -1e-4)
    l_im = lim_ref[...]
    mag = jnp.exp(l_re * dt)
    a_re = mag * jnp.cos(l_im * dt)
    a_im = mag * jnp.sin(l_im * dt)
    den = l_re * l_re + l_im * l_im
    n_re = a_re - 1.0
    z_re = (n_re * l_re + a_im * l_im) / den
    z_im = (a_im * l_re - n_re * l_im) / den
    are_ref[...] = a_re
    aim_ref[...] = a_im
    br, bi = bre_ref[...], bim_ref[...]
    zr, zi = z_re[:, None, :], z_im[:, None, :]
    bbre_ref[...] = zr * br - zi * bi
    bbim_ref[...] = zr * bi + zi * br


def _s5_prep(lam_re, lam_im, log_dt, b_re_t, b_im_t):
    g, p = lam_re.shape
    sd = jax.ShapeDtypeStruct
    return pl.pallas_call(
        _s5_prep_kernel,
        out_shape=[sd((g, p), F32), sd((g, p), F32), sd(b_re_t.shape, F32), sd(b_re_t.shape, F32)],
        name="s5_prep",
    )(lam_re, lam_im, log_dt, b_re_t, b_im_t)


def _s5_kernel(u_ref, h0r_ref, h0i_ref, ar_ref, ai_ref, wb_ref, wc_ref, d_ref, wglu_ref, bglu_ref,
               g_ref, o_ref, hr_ref, hi_ref, bu_ref, hs_ref, *, tt, stride):
    j = pl.program_id(1)
    rows = SUBLANES * tt
    n_tiles = D_STATE // 128

    @pl.when(j == 0)
    def _():
        hs_ref[:, :D_STATE] = h0r_ref[...]
        hs_ref[:, D_STATE:] = h0i_ref[...]

    u = u_ref[...].reshape(rows, D_SSM)
    ub = u.astype(BF16)

    def put(c, val):
        if stride == tt:
            bu_ref[c, 0:rows, :] = val
        else:
            for b in range(SUBLANES):
                bu_ref[c, b * stride:b * stride + tt, :] = val[b * tt:(b + 1) * tt]

    def get(c):
        if stride == tt:
            return bu_ref[c, 0:rows, :]
        return jnp.concatenate([bu_ref[c, b * stride:b * stride + tt, :] for b in range(SUBLANES)], axis=0)

    for n in range(2 * D_STATE // 256):
        band = (n % (D_STATE // 256)) // 2
        res = _dot(ub[:, 128 * band:128 * (band + 1)],
                   wb_ref[128 * band:128 * (band + 1), 256 * n:256 * (n + 1)])
        put(2 * n, res[:, :128])
        put(2 * n + 1, res[:, 128:])

    tiles_per_pass = S5_COLS // 128
    for c0 in range(0, n_tiles, tiles_per_pass):
        tiles = range(c0, c0 + tiles_per_pass)
        a_r = [jnp.broadcast_to(ar_ref[:, 128 * c:128 * (c + 1)], (SUBLANES, 128)) for c in tiles]
        a_i = [jnp.broadcast_to(ai_ref[:, 128 * c:128 * (c + 1)], (SUBLANES, 128)) for c in tiles]

        def step(t, carry, tiles=tiles, a_r=a_r, a_i=a_i):
            at_t = pl.ds(t, SUBLANES, stride=stride)
            out = []
            for k, c in enumerate(tiles):
                h_r, h_i = carry[2 * k], carry[2 * k + 1]
                n_r = a_r[k] * h_r - a_i[k] * h_i + bu_ref[c, at_t, :]
                n_i = a_r[k] * h_i + a_i[k] * h_r + bu_ref[n_tiles + c, at_t, :]
                bu_ref[c, at_t, :] = n_r
                bu_ref[n_tiles + c, at_t, :] = n_i
                out += [n_r, n_i]
            return tuple(out)

        init = []
        for c in tiles:
            init += [hs_ref[:, 128 * c:128 * (c + 1)], hs_ref[:, D_STATE + 128 * c:D_STATE + 128 * (c + 1)]]
        fin = lax.fori_loop(0, tt, step, tuple(init), unroll=min(tt, 8))
        for k, c in enumerate(tiles):
            hs_ref[:, 128 * c:128 * (c + 1)] = fin[2 * k]
            hs_ref[:, D_STATE + 128 * c:D_STATE + 128 * (c + 1)] = fin[2 * k + 1]

    def h_cols(first_tile):
        return jnp.concatenate([get(first_tile + k) for k in range(4)], axis=-1).astype(BF16)

    ys = []
    for m in range(D_SSM // 128):
        y = _dot(h_cols(4 * m), wc_ref[512 * m:512 * (m + 1), 128 * m:128 * (m + 1)])
        y = y + _dot(h_cols(n_tiles + 4 * m),
                     wc_ref[D_STATE + 512 * m:D_STATE + 512 * (m + 1), 128 * m:128 * (m + 1)])
        ys.append(y)
    y = jnp.concatenate(ys, axis=-1) + d_ref[...] * u
    z = _dot(jax.nn.gelu(y).astype(BF16), wglu_ref[...]) + bglu_ref[...]
    s = z[:, :D_SSM] * jax.nn.sigmoid(z[:, D_SSM:])
    o_ref[...] = (_rms(s) * g_ref[...]).astype(BF16).reshape(o_ref.shape)

    @pl.when(j == pl.num_programs(1) - 1)
    def _():
        hr_ref[...] = hs_ref[:, :D_STATE]
        hi_ref[...] = hs_ref[:, D_STATE:]


def _s5(u, h0r, h0i, a_re, a_im, wb, wc, d, wglu, bglu, g, *, tt, stride, time_chunked):
    nbg = h0r.shape[0] // SUBLANES
    if time_chunked:
        nchunks = u.shape[1] // tt
        u_spec = pl.BlockSpec((SUBLANES, tt, D_SSM), lambda gi, j: (gi, j, 0))
    else:
        nchunks = 1
        u_spec = pl.BlockSpec((None, SUBLANES * tt, D_SSM), lambda gi, j: (gi, 0, 0))
    fix = lambda gi, j: (0, 0)
    st_spec = pl.BlockSpec((SUBLANES, D_STATE), lambda gi, j: (gi, 0))
    sd = jax.ShapeDtypeStruct
    return pl.pallas_call(
        functools.partial(_s5_kernel, tt=tt, stride=stride),
        grid=(nbg, nchunks),
        in_specs=[u_spec, st_spec, st_spec,
                  pl.BlockSpec((1, D_STATE), fix), pl.BlockSpec((1, D_STATE), fix),
                  pl.BlockSpec((D_SSM, 2 * D_STATE), fix), pl.BlockSpec((2 * D_STATE, D_SSM), fix),
                  pl.BlockSpec((1, D_SSM), fix), pl.BlockSpec((D_SSM, 2 * D_SSM), fix),
                  pl.BlockSpec((1, 2 * D_SSM), fix), pl.BlockSpec((1, D_SSM), fix)],
        out_specs=[u_spec, st_spec, st_spec],
        out_shape=[sd(u.shape, BF16), sd(h0r.shape, F32), sd(h0r.shape, F32)],
        scratch_shapes=[pltpu.VMEM((2 * D_STATE // 128, SUBLANES * stride, 128), F32),
                        pltpu.VMEM((SUBLANES, 2 * D_STATE), F32)],
        compiler_params=_cparams("parallel", "arbitrary"),
        name="s5",
    )(u, h0r, h0i, a_re, a_im, wb, wc, d, wglu, bglu, g)


def _out_proj_kernel(a_ref, s_ref, x_ref, wa_ref, ws_ref, g_ref, wr_ref, br_ref, x1_ref, xn_ref, lg_ref):
    x1 = x_ref[...] + _dot(a_ref[...], wa_ref[...]) + _dot(s_ref[...], ws_ref[...])
    x1_ref[...] = x1
    xn = (_rms(x1) * g_ref[...]).astype(BF16)
    xn_ref[...] = xn
    lg_ref[...] = _dot(xn, wr_ref[...]) + br_ref[...]


def _out_proj(attn_n, ssm_n, x2d, wa, ws, g, wr, br):
    t = x2d.shape[0]
    tm = min(ROW_TILE, t)
    row = lambda i: (i, 0)
    fix = lambda i: (0, 0)
    sd = jax.ShapeDtypeStruct
    return pl.pallas_call(
        _out_proj_kernel,
        grid=(t // tm,),
        in_specs=[pl.BlockSpec((tm, D_ATTN), row), pl.BlockSpec((tm, D_SSM), row),
                  pl.BlockSpec((tm, D_MODEL), row),
                  pl.BlockSpec((D_ATTN, D_MODEL), fix), pl.BlockSpec((D_SSM, D_MODEL), fix),
                  pl.BlockSpec((1, D_MODEL), fix), pl.BlockSpec((D_MODEL, 128), fix),
                  pl.BlockSpec((1, 128), fix)],
        out_specs=[pl.BlockSpec((tm, D_MODEL), row), pl.BlockSpec((tm, D_MODEL), row),
                   pl.BlockSpec((tm, 128), row)],
        out_shape=[sd((t, D_MODEL), F32), sd((t, D_MODEL), BF16), sd((t, 128), F32)],
        compiler_params=_cparams("parallel"),
        name="out_proj",
    )(attn_n, ssm_n, x2d, wa, ws, g, wr, br)


def _moe_kernel(be_ref, nu_ref, x_ref, wgu_ref, bg_ref, bl_ref, wd_ref, bd_ref, perm_ref, o_ref,
                wg_s, wl_s, wd_s):
    i = pl.program_id(0)
    used = i < nu_ref[0]
    new_expert = (i == 0) | (be_ref[i] != be_ref[jnp.maximum(i - 1, 0)])

    @pl.when(used & new_expert)
    def _():
        for c in range(2 * D_FF // 256):
            r = _dot(wgu_ref[:, 256 * c:256 * (c + 1)].astype(BF16), perm_ref[...])
            wg_s[:, 128 * c:128 * (c + 1)] = r[:, :128].astype(BF16)
            wl_s[:, 128 * c:128 * (c + 1)] = r[:, 128:].astype(BF16)
        wd_s[...] = wd_ref[...].astype(BF16)

    @pl.when(used)
    def _():
        x = x_ref[...]
        glu = jnp.minimum(_dot(x, wg_s[...]) + bg_ref[...], SWIGLU_LIMIT)
        lin = jnp.clip(_dot(x, wl_s[...]) + bl_ref[...], -SWIGLU_LIMIT, SWIGLU_LIMIT)
        act = glu * jax.nn.sigmoid(SWIGLU_ALPHA * glu) * (lin + 1.0)
        o_ref[...] = _dot(act.astype(BF16), wd_s[...]) + bd_ref[...]

    @pl.when(jnp.logical_not(used))
    def _():
        o_ref[...] = jnp.zeros_like(o_ref)


def _moe(block_e, n_used, x_rows, wgu, bg, bl, wd, bd, perm):
    n_rows = x_rows.shape[0]
    nblk = n_rows // MOE_TILE
    row = lambda i, be, nu: (jnp.minimum(i, nu[0] - 1), 0)
    wsel = lambda i, be, nu: (be[i], 0, 0)
    grid_spec = pltpu.PrefetchScalarGridSpec(
        num_scalar_prefetch=2,
        grid=(nblk,),
        in_specs=[pl.BlockSpec((MOE_TILE, D_MODEL), row),
                  pl.BlockSpec((None, D_MODEL, 2 * D_FF), wsel),
                  pl.BlockSpec((None, 1, D_FF), wsel), pl.BlockSpec((None, 1, D_FF), wsel),
                  pl.BlockSpec((None, D_FF, D_MODEL), wsel), pl.BlockSpec((None, 1, D_MODEL), wsel),
                  pl.BlockSpec((256, 256), lambda i, be, nu: (0, 0))],
        out_specs=pl.BlockSpec((MOE_TILE, D_MODEL), lambda i, be, nu: (i, 0)),
        scratch_shapes=[pltpu.VMEM((D_MODEL, D_FF), BF16), pltpu.VMEM((D_MODEL, D_FF), BF16),
                        pltpu.VMEM((D_FF, D_MODEL), BF16)],
    )
    return pl.pallas_call(
        _moe_kernel,
        grid_spec=grid_spec,
        out_shape=jax.ShapeDtypeStruct((n_rows, D_MODEL), F32),
        compiler_params=_cparams("arbitrary"),
        name="moe",
    )(block_e, n_used, x_rows, wgu, bg, bl, wd, bd, perm)


def _combine_kernel(pos_ref, rows_hbm, x1_ref, gate_ref, o_ref, buf, sem):
    i = pl.program_id(0)
    n = pl.num_programs(0)

    def row_copy(tile, slot, t, k):
        r = pos_ref[(tile * COMBINE_TILE + t) * TOP_K + k]
        return pltpu.make_async_copy(rows_hbm.at[pl.ds(r, 1), :], buf.at[slot, k, pl.ds(t, 1), :],
                                     sem.at[slot])

    def fetch(tile, slot):
        def body(t, c):
            for k in range(TOP_K):
                row_copy(tile, slot, t, k).start()
            return c
        lax.fori_loop(0, COMBINE_TILE, body, 0, unroll=4)

    @pl.when(i == 0)
    def _():
        fetch(0, 0)

    @pl.when(i + 1 < n)
    def _():
        fetch(i + 1, (i + 1) % 2)

    slot = i % 2
    pltpu.make_async_copy(buf.at[slot], buf.at[slot], sem.at[slot]).wait()
    g = gate_ref[...]
    y = x1_ref[...]
    for k in range(TOP_K):
        y = y + g[:, k:k + 1] * buf[slot, k]
    o_ref[...] = y


def _combine(pos_flat, rows, x1, gates):
    t = x1.shape[0]
    grid_spec = pltpu.PrefetchScalarGridSpec(
        num_scalar_prefetch=1,
        grid=(t // COMBINE_TILE,),
        in_specs=[pl.BlockSpec(memory_space=pl.ANY),
                  pl.BlockSpec((COMBINE_TILE, D_MODEL), lambda i, p: (i, 0)),
                  pl.BlockSpec((COMBINE_TILE, TOP_K), lambda i, p: (i, 0))],
        out_specs=pl.BlockSpec((COMBINE_TILE, D_MODEL), lambda i, p: (i, 0)),
        scratch_shapes=[pltpu.VMEM((2, TOP_K, COMBINE_TILE, D_MODEL), F32),
                        pltpu.SemaphoreType.DMA((2,))],
    )
    return pl.pallas_call(
        _combine_kernel,
        grid_spec=grid_spec,
        out_shape=jax.ShapeDtypeStruct((t, D_MODEL), F32),
        compiler_params=_cparams("arbitrary"),
        name="combine",
    )(pos_flat, rows, x1, gates)


def _route(logits):
    t = logits.shape[0]
    n_assign = t * TOP_K
    top_v, top_i = lax.top_k(logits, TOP_K)
    gates = jax.nn.softmax(top_v, axis=-1)
    flat_e = top_i.reshape(-1)
    onehot = (flat_e[:, None] == jnp.arange(N_EXPERTS, dtype=flat_e.dtype)[None, :]).astype(jnp.int32)
    csum = jnp.cumsum(onehot, axis=0)
    counts = csum[-1]
    rank = jnp.take_along_axis(csum, flat_e[:, None], axis=1)[:, 0] - 1
    padded = ((counts + MOE_TILE - 1) // MOE_TILE) * MOE_TILE
    pend = jnp.cumsum(padded)
    pstart = pend - padded
    dest = (pstart[flat_e] + rank).astype(jnp.int32)
    nblk = (n_assign + MOE_TILE - 1) // MOE_TILE + N_EXPERTS
    n_rows = nblk * MOE_TILE
    row_tok = jnp.zeros((n_rows,), jnp.int32).at[dest].set(jnp.arange(n_assign, dtype=jnp.int32) // TOP_K)
    n_used = (pend[-1] // MOE_TILE).astype(jnp.int32)
    block_start = jnp.arange(nblk, dtype=jnp.int32) * MOE_TILE
    block_e = jnp.sum(pend[None, :] <= jnp.minimum(block_start, pend[-1] - 1)[:, None], axis=1)
    block_e = jnp.minimum(block_e, N_EXPERTS - 1).astype(jnp.int32)
    return row_tok, gates, dest, block_e, n_used.reshape(1)


def kernel(x_prompt, x_sample, cache_k, cache_v, state_ssm_re, state_ssm_im, norm_mix_g, w_in, q_norm_g, k_norm_g, attn_sinks, ssm_lambda_re, ssm_lambda_im, ssm_b_re, ssm_b_im, ssm_c_re, ssm_c_im, ssm_d, ssm_log_dt, w_glu, b_glu, attn_out_norm_g, ssm_out_norm_g, w_out, norm_ffn_g, w_router, b_router, w_gate_up, b_gate_up, w_down, b_down):
    depth = w_in.shape[0]
    assert depth == 1
    bp, sp, _ = x_prompt.shape
    bs, ss, _ = x_sample.shape
    tp, ts = bp * sp, bs * ss
    assert bp == SUBLANES and sp % S5_CHUNK == 0 and bs % SUBLANES == 0

    perm = jnp.asarray(_Q_PERM)
    w_in0 = w_in[0]
    wq = w_in0[:, :D_ATTN].reshape(D_MODEL, N_HEADS, HEAD_DIM)[:, perm].reshape(D_MODEL, D_ATTN)
    w_in_b = jnp.concatenate([wq, w_in0[:, D_ATTN:]], axis=1).astype(BF16)
    qkg = jnp.concatenate([jnp.tile(q_norm_g[0], N_HEADS), jnp.tile(k_norm_g[0], N_KV_HEADS)])[None]
    pmat = jnp.kron(jnp.eye(256 // HEAD_DIM, dtype=F32),
                    jnp.full((HEAD_DIM, HEAD_DIM), 1.0 / HEAD_DIM, F32)).astype(BF16)
    g_mix = norm_mix_g[0][None]
    sinks = attn_sinks[0]
    g_attn = attn_out_norm_g[0].reshape(N_HEADS, HEAD_DIM)[perm].reshape(1, D_ATTN)
    w_out0 = w_out[0]
    w_out_a = w_out0[:D_ATTN].reshape(N_HEADS, HEAD_DIM, D_MODEL)[perm].reshape(D_ATTN, D_MODEL).astype(BF16)
    w_out_s = w_out0[D_ATTN:].astype(BF16)
    g_ssm = ssm_out_norm_g[0][None]
    g_ffn = norm_ffn_g[0][None]
    w_r = jnp.pad(w_router[0], ((0, 0), (0, 128 - N_EXPERTS))).astype(BF16)
    b_r = jnp.pad(b_router[0], (0, 128 - N_EXPERTS), constant_values=NEG_INF)[None]

    a_re, a_im, bb_re, bb_im = _s5_prep(
        ssm_lambda_re[0], ssm_lambda_im[0], ssm_log_dt[0][:, None],
        jnp.swapaxes(ssm_b_re[0], 1, 2), jnp.swapaxes(ssm_b_im[0], 1, 2))
    eye_g = jnp.eye(N_SSM_GROUPS, dtype=F32)
    bd_b = lambda bb: jnp.einsum("ghp,gk->ghkp", bb, eye_g).reshape(D_SSM, D_STATE)
    wb = jnp.concatenate([bd_b(bb_re), bd_b(bb_im)], axis=1).astype(BF16)
    bd_c = lambda c: jnp.einsum("ghp,gk->gpkh", c, eye_g).reshape(D_STATE, D_SSM)
    wc = jnp.concatenate([bd_c(ssm_c_re[0]), -bd_c(ssm_c_im[0])], axis=0).astype(BF16)
    a_re, a_im = a_re.reshape(1, D_STATE), a_im.reshape(1, D_STATE)
    d_skip = ssm_d[0].reshape(1, D_SSM)
    w_glu_b = w_glu[0].astype(BF16)
    b_glu0 = b_glu[0][None]

    b_g = b_gate_up[0][:, None, 0::2]
    b_l = b_gate_up[0][:, None, 1::2]
    b_d = b_down[0][:, None, :]
    idx = jnp.arange(256)
    deint = (idx[None, :] == jnp.where(idx % 2 == 0, idx // 2, 128 + idx // 2)[:, None]).astype(BF16)

    xp2 = x_prompt.reshape(tp, D_MODEL)
    xs2 = x_sample.reshape(ts, D_MODEL)
    qp, kp, vp, up = _in_proj(xp2, g_mix, w_in_b, qkg, pmat)
    qs, ks, vs, us = _in_proj(xs2, g_mix, w_in_b, qkg, pmat)

    kp3, vp3 = kp.reshape(bp, sp, D_KV), vp.reshape(bp, sp, D_KV)
    attn_p = _band_attention(sinks, qp.reshape(bp, sp, D_ATTN), kp3, vp3, g_attn).reshape(tp, D_ATTN)
    ck = cache_k[0].reshape(bs, -1, D_KV)
    cv = cache_v[0].reshape(bs, -1, D_KV)
    attn_s = _cache_attention(sinks, qs, ks, vs, ck, cv, g_attn, ss)

    zeros_p = jnp.zeros((bp, D_STATE), F32)
    s5_args = (a_re, a_im, wb, wc, d_skip, w_glu_b, b_glu0, g_ssm)
    ssm_p, hr_p, hi_p = _s5(up.reshape(bp, sp, D_SSM), zeros_p, zeros_p, *s5_args,
                            tt=S5_CHUNK, stride=S5_CHUNK + S5_PAD, time_chunked=True)
    ssm_s, hr_s, hi_s = _s5(us.reshape(bs // SUBLANES, SUBLANES * ss, D_SSM),
                            state_ssm_re[0].reshape(bs, D_STATE), state_ssm_im[0].reshape(bs, D_STATE),
                            *s5_args, tt=ss, stride=ss, time_chunked=False)

    x1p, xnp_, lgp = _out_proj(attn_p, ssm_p.reshape(tp, D_SSM), xp2, w_out_a, w_out_s, g_ffn, w_r, b_r)
    x1s, xns, lgs = _out_proj(attn_s, ssm_s.reshape(ts, D_SSM), xs2, w_out_a, w_out_s, g_ffn, w_r, b_r)

    xn_all = jnp.concatenate([xnp_, xns], axis=0)
    logits = jnp.concatenate([lgp, lgs], axis=0)[:, :N_EXPERTS]
    row_tok, gates, pos, block_e, n_used = _route(logits)
    out_rows = _moe(block_e, n_used, xn_all[row_tok], w_gate_up[0], b_g, b_l, w_down[0], b_d, deint)
    yp = _combine(pos[:tp * TOP_K], out_rows, x1p, gates[:tp]).reshape(bp, sp, D_MODEL)
    ys = _combine(pos[tp * TOP_K:], out_rows, x1s, gates[tp:]).reshape(bs, ss, D_MODEL)

    kv5 = lambda a, b_: a.reshape(b_, -1, N_KV_HEADS, HEAD_DIM)
    new_kp = kv5(kp3[:, -WINDOW:], bp)[None]
    new_vp = kv5(vp3[:, -WINDOW:], bp)[None]
    ks3, vs3 = ks.reshape(bs, ss, D_KV), vs.reshape(bs, ss, D_KV)
    new_ks = kv5(jnp.concatenate([ck, ks3], axis=1)[:, ss:], bs)[None]
    new_vs = kv5(jnp.concatenate([cv, vs3], axis=1)[:, ss:], bs)[None]
    st = lambda h, b_: h.reshape(1, b_, N_SSM_GROUPS, SSM_STATE)
    return (yp, ys, new_kp, new_vp, st(hr_p, bp), st(hi_p, bp),
            new_ks, new_vs, st(hr_s, bs), st(hi_s, bs))
```

```python
import functools
import math

import jax
import jax.numpy as jnp
from jax import lax
from jax.experimental import pallas as pl
from jax.experimental.pallas import tpu as pltpu

F32 = jnp.float32
BF16 = jnp.bfloat16

D_MODEL = 1024
D_ATTN = 512
D_SSM = 512
HEAD_DIM = 64
N_HEADS = 8
N_KV_HEADS = 2
D_KV = N_KV_HEADS * HEAD_DIM
WINDOW = 128
SSM_GROUP = 16
N_SSM_GROUPS = 32
SSM_STATE = 64
D_STATE = N_SSM_GROUPS * SSM_STATE
N_EXPERTS = 32
TOP_K = 4
D_FF = 1024
SWIGLU_LIMIT = 7.0
SWIGLU_ALPHA = 1.702
RMS_EPS = 1e-6
NEG_INF = -1e30
D_IN_PROJ = D_ATTN + 2 * D_KV + D_SSM
D_QK = D_ATTN + D_KV

SUBLANES = 8
VMEM_LIMIT = 56 * 1024 * 1024

ROW_TILE = 512
MOE_TILE = 256
COMBINE_TILE = 128
S5_CHUNK = 128
S5_PAD = 8
S5_COLS = 512

_Q_PERM = (0, 4, 1, 5, 2, 6, 3, 7)


def _cparams(*sem):
    return pltpu.CompilerParams(dimension_semantics=sem, vmem_limit_bytes=VMEM_LIMIT)


def _dot(a, b):
    return jnp.dot(a, b, preferred_element_type=F32)


def _rms(x):
    return x * lax.rsqrt(jnp.mean(x * x, axis=-1, keepdims=True) + RMS_EPS)


def _in_proj_kernel(x_ref, g_ref, w_ref, qkg_ref, p_ref, q_ref, k_ref, v_ref, u_ref):
    xn = _rms(x_ref[...]) * g_ref[...]
    h = _dot(xn.astype(BF16), w_ref[...])
    qk = h[:, :D_QK]
    sq = (qk * qk).astype(BF16)
    p = p_ref[...]
    ms = jnp.concatenate(
        [_dot(sq[:, 0:256], p), _dot(sq[:, 256:512], p), _dot(sq[:, 512:640], p[:128, :128])],
        axis=-1)
    qkn = qk * lax.rsqrt(ms + RMS_EPS) * qkg_ref[...]
    q_ref[...] = qkn[:, :D_ATTN].astype(BF16)
    k_ref[...] = qkn[:, D_ATTN:]
    v_ref[...] = h[:, D_QK:D_QK + D_KV]
    u_ref[...] = h[:, D_QK + D_KV:]


def _in_proj(x2d, g, w, qkg, pmat):
    t = x2d.shape[0]
    tm = min(ROW_TILE, t)
    row = lambda i: (i, 0)
    fix = lambda i: (0, 0)
    return pl.pallas_call(
        _in_proj_kernel,
        grid=(t // tm,),
        in_specs=[pl.BlockSpec((tm, D_MODEL), row), pl.BlockSpec((1, D_MODEL), fix),
                  pl.BlockSpec((D_MODEL, D_IN_PROJ), fix), pl.BlockSpec((1, D_QK), fix),
                  pl.BlockSpec((256, 256), fix)],
        out_specs=[pl.BlockSpec((tm, D_ATTN), row), pl.BlockSpec((tm, D_KV), row),
                   pl.BlockSpec((tm, D_KV), row), pl.BlockSpec((tm, D_SSM), row)],
        out_shape=[jax.ShapeDtypeStruct((t, D_ATTN), BF16), jax.ShapeDtypeStruct((t, D_KV), F32),
                   jax.ShapeDtypeStruct((t, D_KV), F32), jax.ShapeDtypeStruct((t, D_SSM), F32)],
        compiler_params=_cparams("parallel"),
        name="in_proj",
    )(x2d, g, w, qkg, pmat)


def _softmax_pv(s_blocks, v_blocks, sink):
    m = sink
    for s in s_blocks:
        m = jnp.maximum(m, jnp.max(s, axis=-1, keepdims=True))
    den = jnp.exp(sink - m)
    acc = None
    for s, v in zip(s_blocks, v_blocks):
        p = jnp.exp(s - m)
        den = den + jnp.sum(p, axis=-1, keepdims=True)
        pv = _dot(p.astype(BF16), v)
        acc = pv if acc is None else acc + pv
    return acc / den


def _band_attn_kernel(sink_ref, q_ref, kp_ref, kc_ref, vp_ref, vc_ref, g_ref, o_ref):
    i = pl.program_id(1)
    q = q_ref[...]
    kb = jnp.concatenate([kp_ref[...], kc_ref[...]], axis=0).astype(BF16)
    vb = jnp.concatenate([vp_ref[...], vc_ref[...]], axis=0).astype(BF16)
    row = lax.broadcasted_iota(jnp.int32, (WINDOW, 2 * WINDOW), 0)
    col = lax.broadcasted_iota(jnp.int32, (WINDOW, 2 * WINDOW), 1)
    mask = (col > row) & (col <= row + WINDOW) & ((col >= WINDOW) | (i > 0))
    lane = lax.broadcasted_iota(jnp.int32, (WINDOW, 128), 1)
    low = lane < HEAD_DIM
    zero = jnp.zeros((), BF16)
    outs = []
    for pair in range(N_HEADS // 2):
        qp = q[:, 128 * pair:128 * (pair + 1)]
        halves = []
        for par in range(2):
            qm = jnp.where(low if par == 0 else ~low, qp, zero)
            s = lax.dot_general(qm, kb, (((1,), (1,)), ((), ())), preferred_element_type=F32)
            s = jnp.where(mask, s * (HEAD_DIM ** -0.5), NEG_INF)
            halves.append(_softmax_pv([s], [vb], sink_ref[_Q_PERM[2 * pair + par]]))
        outs.append(jnp.where(low, halves[0], halves[1]))
    o = jnp.concatenate(outs, axis=-1)
    o_ref[...] = (_rms(o) * g_ref[...]).astype(BF16)


def _band_attention(sinks, q, k, v, g):
    b, s, _ = q.shape
    nb = s // WINDOW
    cur = lambda bi, i: (bi, i, 0)
    prev = lambda bi, i: (bi, jnp.maximum(i - 1, 0), 0)
    kv_spec = lambda im: pl.BlockSpec((None, WINDOW, D_KV), im)
    return pl.pallas_call(
        _band_attn_kernel,
        grid=(b, nb),
        in_specs=[pl.BlockSpec(memory_space=pltpu.SMEM),
                  pl.BlockSpec((None, WINDOW, D_ATTN), cur),
                  kv_spec(prev), kv_spec(cur), kv_spec(prev), kv_spec(cur),
                  pl.BlockSpec((1, D_ATTN), lambda bi, i: (0, 0))],
        out_specs=pl.BlockSpec((None, WINDOW, D_ATTN), cur),
        out_shape=jax.ShapeDtypeStruct((b, s, D_ATTN), BF16),
        compiler_params=_cparams("parallel", "parallel"),
        name="band_attn",
    )(sinks, q, k, k, v, v, g)


_PAIR_ROWS = 8
_CACHE_BB = 8


def _cache_attn_kernel(sink_ref, q_ref, kn_ref, vn_ref, ck_ref, cv_ref, g_ref, o_ref, *, n_new):
    n_buf = ck_ref.shape[1]
    rows_blk = _CACHE_BB * n_new
    knew = kn_ref[...].astype(BF16)
    vnew = vn_ref[...].astype(BF16)
    lane = lax.broadcasted_iota(jnp.int32, (_PAIR_ROWS, 128), 1)
    low = lane < HEAD_DIM
    zero = jnp.zeros((), BF16)
    n_stack = N_HEADS * _PAIR_ROWS
    r = lax.broadcasted_iota(jnp.int32, (n_stack, 1), 0) % _PAIR_ROWS
    r_seq, r_tok = r // n_new, r % n_new
    colc = lax.broadcasted_iota(jnp.int32, (n_stack, 2 * n_buf), 1)
    c_seq, c_pos = colc // n_buf, colc % n_buf
    mask_c = (c_seq == r_seq) & (c_pos + WINDOW > r_tok + n_buf)
    coln = lax.broadcasted_iota(jnp.int32, (n_stack, rows_blk), 1)
    sink_col = jnp.concatenate(
        [jnp.full((_PAIR_ROWS, 1), sink_ref[_Q_PERM[h]], F32) for h in range(N_HEADS)], axis=0)
    for sp in range(_CACHE_BB // 2):
        q = q_ref[_PAIR_ROWS * sp:_PAIR_ROWS * (sp + 1), :]
        pieces = []
        for pair in range(N_HEADS // 2):
            qp = q[:, 128 * pair:128 * (pair + 1)]
            pieces.append(jnp.where(low, qp, zero))
            pieces.append(jnp.where(low, zero, qp))
        qs = jnp.concatenate(pieces, axis=0)
        kc = jnp.concatenate([ck_ref[2 * sp], ck_ref[2 * sp + 1]], axis=0).astype(BF16)
        vc = jnp.concatenate([cv_ref[2 * sp], cv_ref[2 * sp + 1]], axis=0).astype(BF16)
        nt = (((1,), (1,)), ((), ()))
        s_c = lax.dot_general(qs, kc, nt, preferred_element_type=F32) * (HEAD_DIM ** -0.5)
        s_n = lax.dot_general(qs, knew, nt, preferred_element_type=F32) * (HEAD_DIM ** -0.5)
        s_c = jnp.where(mask_c, s_c, NEG_INF)
        n_seq, n_tok = coln // n_new - 2 * sp, coln % n_new
        mask_n = (n_seq == r_seq) & (n_tok <= r_tok)
        s_n = jnp.where(mask_n, s_n, NEG_INF)
        o = _softmax_pv([s_c, s_n], [vc, vnew], sink_col)
        outs = [jnp.where(low, o[16 * pair:16 * pair + 8], o[16 * pair + 8:16 * pair + 16])
                for pair in range(N_HEADS // 2)]
        oo = jnp.concatenate(outs, axis=-1)
        o_ref[_PAIR_ROWS * sp:_PAIR_ROWS * (sp + 1), :] = (_rms(oo) * g_ref[...]).astype(BF16)


def _cache_attention(sinks, q, k, v, cache_k, cache_v, g, n_new):
    t = q.shape[0]
    nb, n_buf, _ = cache_k.shape
    assert n_new * 2 == _PAIR_ROWS and nb % _CACHE_BB == 0 and n_buf == WINDOW
    rows = _CACHE_BB * n_new
    row = lambda i: (i, 0)
    cache_spec = pl.BlockSpec((_CACHE_BB, n_buf, D_KV), lambda i: (i, 0, 0))
    return pl.pallas_call(
        functools.partial(_cache_attn_kernel, n_new=n_new),
        grid=(nb // _CACHE_BB,),
        in_specs=[pl.BlockSpec(memory_space=pltpu.SMEM),
                  pl.BlockSpec((rows, D_ATTN), row), pl.BlockSpec((rows, D_KV), row),
                  pl.BlockSpec((rows, D_KV), row), cache_spec, cache_spec,
                  pl.BlockSpec((1, D_ATTN), lambda i: (0, 0))],
        out_specs=pl.BlockSpec((rows, D_ATTN), row),
        out_shape=jax.ShapeDtypeStruct((t, D_ATTN), BF16),
        compiler_params=_cparams("parallel"),
        name="cache_attn",
    )(sinks, q, k, v, cache_k, cache_v, g)


def _s5_prep_kernel(lre_ref, lim_ref, ldt_ref, bre_ref, bim_ref, are_ref, aim_ref, bbre_ref, bbim_ref):
    dt = jnp.exp(ldt_ref[...])
    l_re = jnp.minimum(lre_ref[...], -1e-4)
    l_im = lim_ref[...]
    mag = jnp.exp(l_re * dt)
    a_re = mag * jnp.cos(l_im * dt)
    a_im = mag * jnp.sin(l_im * dt)
    den = l_re * l_re + l_im * l_im
    n_re = a_re - 1.0
    z_re = (n_re * l_re + a_im * l_im) / den
    z_im = (a_im * l_re - n_re * l_im) / den
    are_ref[...] = a_re
    aim_ref[...] = a_im
    br, bi = bre_ref[...], bim_ref[...]
    zr, zi = z_re[:, None, :], z_im[:, None, :]
    bbre_ref[...] = zr * br - zi * bi
    bbim_ref[...] = zr * bi + zi * br


def _s5_prep(lam_re, lam_im, log_dt, b_re_t, b_im_t):
    g, p = lam_re.shape
    sd = jax.ShapeDtypeStruct
    return pl.pallas_call(
        _s5_prep_kernel,
        out_shape=[sd((g, p), F32), sd((g, p), F32), sd(b_re_t.shape, F32), sd(b_re_t.shape, F32)],
        name="s5_prep",
    )(lam_re, lam_im, log_dt, b_re_t, b_im_t)


def _s5_kernel(u_ref, h0r_ref, h0i_ref, ar_ref, ai_ref, wb_ref, wc_ref, d_ref, wglu_ref, bglu_ref,
               g_ref, o_ref, hr_ref, hi_ref, bu_ref, hs_ref, *, tt, stride):
    j = pl.program_id(1)
    rows = SUBLANES * tt
    n_tiles = D_STATE // 128

    @pl.when(j == 0)
    def _():
        hs_ref[:, :D_STATE] = h0r_ref[...]
        hs_ref[:, D_STATE:] = h0i_ref[...]

    u = u_ref[...].reshape(rows, D_SSM)
    ub = u.astype(BF16)

    def put(c, val):
        if stride == tt:
            bu_ref[c, 0:rows, :] = val
        else:
            for b in range(SUBLANES):
                bu_ref[c, b * stride:b * stride + tt, :] = val[b * tt:(b + 1) * tt]

    def get(c):
        if stride == tt:
            return bu_ref[c, 0:rows, :]
        return jnp.concatenate([bu_ref[c, b * stride:b * stride + tt, :] for b in range(SUBLANES)], axis=0)

    for n in range(2 * D_STATE // 256):
        band = (n % (D_STATE // 256)) // 2
        res = _dot(ub[:, 128 * band:128 * (band + 1)],
                   wb_ref[128 * band:128 * (band + 1), 256 * n:256 * (n + 1)])
        put(2 * n, res[:, :128])
        put(2 * n + 1, res[:, 128:])

    tiles_per_pass = S5_COLS // 128
    for c0 in range(0, n_tiles, tiles_per_pass):
        tiles = range(c0, c0 + tiles_per_pass)
        a_r = [jnp.broadcast_to(ar_ref[:, 128 * c:128 * (c + 1)], (SUBLANES, 128)) for c in tiles]
        a_i = [jnp.broadcast_to(ai_ref[:, 128 * c:128 * (c + 1)], (SUBLANES, 128)) for c in tiles]

        def step(t, carry, tiles=tiles, a_r=a_r, a_i=a_i):
            at_t = pl.ds(t, SUBLANES, stride=stride)
            out = []
            for k, c in enumerate(tiles):
                h_r, h_i = carry[2 * k], carry[2 * k + 1]
                n_r = a_r[k] * h_r - a_i[k] * h_i + bu_ref[c, at_t, :]
                n_i = a_r[k] * h_i + a_i[k] * h_r + bu_ref[n_tiles + c, at_t, :]
                bu_ref[c, at_t, :] = n_r
                bu_ref[n_tiles + c, at_t, :] = n_i
                out += [n_r, n_i]
            return tuple(out)

        init = []
        for c in tiles:
            init += [hs_ref[:, 128 * c:128 * (c + 1)], hs_ref[:, D_STATE + 128 * c:D_STATE + 128 * (c + 1)]]
        fin = lax.fori_loop(0, tt, step, tuple(init), unroll=min(tt, 8))
        for k, c in enumerate(tiles):
            hs_ref[:, 128 * c:128 * (c + 1)] = fin[2 * k]
            hs_ref[:, D_STATE + 128 * c:D_STATE + 128 * (c + 1)] = fin[2 * k + 1]

    def h_cols(first_tile):
        return jnp.concatenate([get(first_tile + k) for k in range(4)], axis=-1).astype(BF16)

    ys = []
    for m in range(D_SSM // 128):
        y = _dot(h_cols(4 * m), wc_ref[512 * m:512 * (m + 1), 128 * m:128 * (m + 1)])
        y = y + _dot(h_cols(n_tiles + 4 * m),
                     wc_ref[D_STATE + 512 * m:D_STATE + 512 * (m + 1), 128 * m:128 * (m + 1)])
        ys.append(y)
    y = jnp.concatenate(ys, axis=-1) + d_ref[...] * u
    z = _dot(jax.nn.gelu(y).astype(BF16), wglu_ref[...]) + bglu_ref[...]
    s = z[:, :D_SSM] * jax.nn.sigmoid(z[:, D_SSM:])
    o_ref[...] = (_rms(s) * g_ref[...]).astype(BF16).reshape(o_ref.shape)

    @pl.when(j == pl.num_programs(1) - 1)
    def _():
        hr_ref[...] = hs_ref[:, :D_STATE]
        hi_ref[...] = hs_ref[:, D_STATE:]


def _s5(u, h0r, h0i, a_re, a_im, wb, wc, d, wglu, bglu, g, *, tt, stride, time_chunked):
    nbg = h0r.shape[0] // SUBLANES
    if time_chunked:
        nchunks = u.shape[1] // tt
        u_spec = pl.BlockSpec((SUBLANES, tt, D_SSM), lambda gi, j: (gi, j, 0))
    else:
        nchunks = 1
        u_spec = pl.BlockSpec((None, SUBLANES * tt, D_SSM), lambda gi, j: (gi, 0, 0))
    fix = lambda gi, j: (0, 0)
    st_spec = pl.BlockSpec((SUBLANES, D_STATE), lambda gi, j: (gi, 0))
    sd = jax.ShapeDtypeStruct
    return pl.pallas_call(
        functools.partial(_s5_kernel, tt=tt, stride=stride),
        grid=(nbg, nchunks),
        in_specs=[u_spec, st_spec, st_spec,
                  pl.BlockSpec((1, D_STATE), fix), pl.BlockSpec((1, D_STATE), fix),
                  pl.BlockSpec((D_SSM, 2 * D_STATE), fix), pl.BlockSpec((2 * D_STATE, D_SSM), fix),
                  pl.BlockSpec((1, D_SSM), fix), pl.BlockSpec((D_SSM, 2 * D_SSM), fix),
                  pl.BlockSpec((1, 2 * D_SSM), fix), pl.BlockSpec((1, D_SSM), fix)],
        out_specs=[u_spec, st_spec, st_spec],
        out_shape=[sd(u.shape, BF16), sd(h0r.shape, F32), sd(h0r.shape, F32)],
        scratch_shapes=[pltpu.VMEM((2 * D_STATE // 128, SUBLANES * stride, 128), F32),
                        pltpu.VMEM((SUBLANES, 2 * D_STATE), F32)],
        compiler_params=_cparams("parallel", "arbitrary"),
        name="s5",
    )(u, h0r, h0i, a_re, a_im, wb, wc, d, wglu, bglu, g)


def _out_proj_kernel(a_ref, s_ref, x_ref, wa_ref, ws_ref, g_ref, wr_ref, br_ref, x1_ref, xn_ref, lg_ref):
    x1 = x_ref[...] + _dot(a_ref[...], wa_ref[...]) + _dot(s_ref[...], ws_ref[...])
    x1_ref[...] = x1
    xn = (_rms(x1) * g_ref[...]).astype(BF16)
    xn_ref[...] = xn
    lg_ref[...] = _dot(xn, wr_ref[...]) + br_ref[...]


def _out_proj(attn_n, ssm_n, x2d, wa, ws, g, wr, br):
    t = x2d.shape[0]
    tm = min(ROW_TILE, t)
    row = lambda i: (i, 0)
    fix = lambda i: (0, 0)
    sd = jax.ShapeDtypeStruct
    return pl.pallas_call(
        _out_proj_kernel,
        grid=(t // tm,),
        in_specs=[pl.BlockSpec((tm, D_ATTN), row), pl.BlockSpec((tm, D_SSM), row),
                  pl.BlockSpec((tm, D_MODEL), row),
                  pl.BlockSpec((D_ATTN, D_MODEL), fix), pl.BlockSpec((D_SSM, D_MODEL), fix),
                  pl.BlockSpec((1, D_MODEL), fix), pl.BlockSpec((D_MODEL, 128), fix),
                  pl.BlockSpec((1, 128), fix)],
        out_specs=[pl.BlockSpec((tm, D_MODEL), row), pl.BlockSpec((tm, D_MODEL), row),
                   pl.BlockSpec((tm, 128), row)],
        out_shape=[sd((t, D_MODEL), F32), sd((t, D_MODEL), BF16), sd((t, 128), F32)],
        compiler_params=_cparams("parallel"),
        name="out_proj",
    )(attn_n, ssm_n, x2d, wa, ws, g, wr, br)


def _moe_kernel(be_ref, nu_ref, x_ref, wgu_ref, bg_ref, bl_ref, wd_ref, bd_ref, perm_ref, o_ref,
                wg_s, wl_s, wd_s):
    i = pl.program_id(0)
    used = i < nu_ref[0]
    new_expert = (i == 0) | (be_ref[i] != be_ref[jnp.maximum(i - 1, 0)])

    @pl.when(used & new_expert)
    def _():
        for c in range(2 * D_FF // 256):
            r = _dot(wgu_ref[:, 256 * c:256 * (c + 1)].astype(BF16), perm_ref[...])
            wg_s[:, 128 * c:128 * (c + 1)] = r[:, :128].astype(BF16)
            wl_s[:, 128 * c:128 * (c + 1)] = r[:, 128:].astype(BF16)
        wd_s[...] = wd_ref[...].astype(BF16)

    @pl.when(used)
    def _():
        x = x_ref[...]
        glu = jnp.minimum(_dot(x, wg_s[...]) + bg_ref[...], SWIGLU_LIMIT)
        lin = jnp.clip(_dot(x, wl_s[...]) + bl_ref[...], -SWIGLU_LIMIT, SWIGLU_LIMIT)
        act = glu * jax.nn.sigmoid(SWIGLU_ALPHA * glu) * (lin + 1.0)
        o_ref[...] = _dot(act.astype(BF16), wd_s[...]) + bd_ref[...]

    @pl.when(jnp.logical_not(used))
    def _():
        o_ref[...] = jnp.zeros_like(o_ref)


def _moe(block_e, n_used, x_rows, wgu, bg, bl, wd, bd, perm):
    n_rows = x_rows.shape[0]
    nblk = n_rows // MOE_TILE
    row = lambda i, be, nu: (jnp.minimum(i, nu[0] - 1), 0)
    wsel = lambda i, be, nu: (be[i], 0, 0)
    grid_spec = pltpu.PrefetchScalarGridSpec(
        num_scalar_prefetch=2,
        grid=(nblk,),
        in_specs=[pl.BlockSpec((MOE_TILE, D_MODEL), row),
                  pl.BlockSpec((None, D_MODEL, 2 * D_FF), wsel),
                  pl.BlockSpec((None, 1, D_FF), wsel), pl.BlockSpec((None, 1, D_FF), wsel),
                  pl.BlockSpec((None, D_FF, D_MODEL), wsel), pl.BlockSpec((None, 1, D_MODEL), wsel),
                  pl.BlockSpec((256, 256), lambda i, be, nu: (0, 0))],
        out_specs=pl.BlockSpec((MOE_TILE, D_MODEL), lambda i, be, nu: (i, 0)),
        scratch_shapes=[pltpu.VMEM((D_MODEL, D_FF), BF16), pltpu.VMEM((D_MODEL, D_FF), BF16),
                        pltpu.VMEM((D_FF, D_MODEL), BF16)],
    )
    return pl.pallas_call(
        _moe_kernel,
        grid_spec=grid_spec,
        out_shape=jax.ShapeDtypeStruct((n_rows, D_MODEL), F32),
        compiler_params=_cparams("arbitrary"),
        name="moe",
    )(block_e, n_used, x_rows, wgu, bg, bl, wd, bd, perm)


def _combine_kernel(pos_ref, rows_hbm, x1_ref, gate_ref, o_ref, buf, sem):
    i = pl.program_id(0)
    n = pl.num_programs(0)

    def row_copy(tile, slot, t, k):
        r = pos_ref[(tile * COMBINE_TILE + t) * TOP_K + k]
        return pltpu.make_async_copy(rows_hbm.at[pl.ds(r, 1), :], buf.at[slot, k, pl.ds(t, 1), :],
                                     sem.at[slot])

    def fetch(tile, slot):
        def body(t, c):
            for k in range(TOP_K):
                row_copy(tile, slot, t, k).start()
            return c
        lax.fori_loop(0, COMBINE_TILE, body, 0, unroll=4)

    @pl.when(i == 0)
    def _():
        fetch(0, 0)

    @pl.when(i + 1 < n)
    def _():
        fetch(i + 1, (i + 1) % 2)

    slot = i % 2
    pltpu.make_async_copy(buf.at[slot], buf.at[slot], sem.at[slot]).wait()
    g = gate_ref[...]
    y = x1_ref[...]
    for k in range(TOP_K):
        y = y + g[:, k:k + 1] * buf[slot, k]
    o_ref[...] = y


def _combine(pos_flat, rows, x1, gates):
    t = x1.shape[0]
    grid_spec = pltpu.PrefetchScalarGridSpec(
        num_scalar_prefetch=1,
        grid=(t // COMBINE_TILE,),
        in_specs=[pl.BlockSpec(memory_space=pl.ANY),
                  pl.BlockSpec((COMBINE_TILE, D_MODEL), lambda i, p: (i, 0)),
                  pl.BlockSpec((COMBINE_TILE, TOP_K), lambda i, p: (i, 0))],
        out_specs=pl.BlockSpec((COMBINE_TILE, D_MODEL), lambda i, p: (i, 0)),
        scratch_shapes=[pltpu.VMEM((2, TOP_K, COMBINE_TILE, D_MODEL), F32),
                        pltpu.SemaphoreType.DMA((2,))],
    )
    return pl.pallas_call(
        _combine_kernel,
        grid_spec=grid_spec,
        out_shape=jax.ShapeDtypeStruct((t, D_MODEL), F32),
        compiler_params=_cparams("arbitrary"),
        name="combine",
    )(pos_flat, rows, x1, gates)


def _lane_pick(cols):
    lane = lax.broadcasted_iota(jnp.int32, (cols[0].shape[0], len(cols)), 1)
    out = cols[-1]
    for k in range(len(cols) - 2, -1, -1):
        out = jnp.where(lane == k, cols[k], out)
    return out


def _topk_kernel(lg_ref, tri_ref, idx_ref, gate_ref, rank_ref, cnt_ref, carry):
    @pl.when(pl.program_id(0) == 0)
    def _():
        carry[...] = jnp.zeros_like(carry)

    l = lg_ref[...]
    lane = lax.broadcasted_iota(jnp.int32, l.shape, 1)
    vals, idxs, sels = [], [], []
    for _ in range(TOP_K):
        m = jnp.max(l, axis=-1, keepdims=True)
        idx = jnp.min(jnp.where(l == m, lane, l.shape[1]), axis=-1, keepdims=True)
        sel = lane == idx
        l = jnp.where(sel, -jnp.inf, l)
        vals.append(m)
        idxs.append(idx)
        sels.append(sel)
    exps = [jnp.exp(v - vals[0]) for v in vals]
    den = exps[0] + exps[1] + exps[2] + exps[3]
    onehot = jnp.where(sels[0] | sels[1] | sels[2] | sels[3], 1.0, 0.0)
    before = _dot(tri_ref[...], onehot.astype(BF16)) + carry[...]
    ranks = [jnp.sum(jnp.where(s, before, 0.0), axis=-1, keepdims=True) for s in sels]
    carry[...] = carry[...] + jnp.sum(onehot, axis=0, keepdims=True)
    idx_ref[...] = _lane_pick(idxs)
    gate_ref[...] = _lane_pick([e / den for e in exps])
    rank_ref[...] = _lane_pick(ranks).astype(jnp.int32)
    cnt_ref[...] = carry[...]


def _dest_kernel(idx_ref, rank_ref, start_ref, pos_ref):
    lane = lax.broadcasted_iota(jnp.int32, (idx_ref.shape[0], 128), 1)
    start = start_ref[...]
    idx = idx_ref[...]
    cols = [jnp.sum(jnp.where(lane == idx[:, k:k + 1], start, 0.0), axis=-1, keepdims=True)
            for k in range(TOP_K)]
    pos_ref[...] = _lane_pick(cols).astype(jnp.int32) + rank_ref[...]


def _route(logits):
    t = logits.shape[0]
    n_assign = t * TOP_K
    tr = ROW_TILE
    row = lambda i: (i, 0)
    fix = lambda i: (0, 0)
    sd = jax.ShapeDtypeStruct
    k_spec = pl.BlockSpec((tr, TOP_K), row)
    tri = (jnp.arange(tr)[:, None] > jnp.arange(tr)[None, :]).astype(BF16)
    top_i, gates, rank, counts = pl.pallas_call(
        _topk_kernel,
        grid=(t // tr,),
        in_specs=[pl.BlockSpec((tr, 128), row), pl.BlockSpec((tr, tr), fix)],
        out_specs=[k_spec, k_spec, k_spec, pl.BlockSpec((1, 128), fix)],
        out_shape=[sd((t, TOP_K), jnp.int32), sd((t, TOP_K), F32), sd((t, TOP_K), jnp.int32),
                   sd((1, 128), F32)],
        scratch_shapes=[pltpu.VMEM((1, 128), F32)],
        compiler_params=_cparams("arbitrary"),
        name="topk",
    )(logits, tri)
    counts = counts[0, :N_EXPERTS].astype(jnp.int32)
    padded = ((counts + MOE_TILE - 1) // MOE_TILE) * MOE_TILE
    pend = jnp.cumsum(padded)
    pstart = jnp.pad(pend - padded, (0, 128 - N_EXPERTS)).astype(F32)[None]
    dest = pl.pallas_call(
        _dest_kernel,
        grid=(t // tr,),
        in_specs=[k_spec, k_spec, pl.BlockSpec((1, 128), fix)],
        out_specs=k_spec,
        out_shape=sd((t, TOP_K), jnp.int32),
        compiler_params=_cparams("parallel"),
        name="dest",
    )(top_i, rank, pstart).reshape(-1)
    nblk = (n_assign + MOE_TILE - 1) // MOE_TILE + N_EXPERTS
    n_rows = nblk * MOE_TILE
    row_tok = jnp.zeros((n_rows,), jnp.int32).at[dest].set(jnp.arange(n_assign, dtype=jnp.int32) // TOP_K)
    n_used = (pend[-1] // MOE_TILE).astype(jnp.int32)
    block_start = jnp.arange(nblk, dtype=jnp.int32) * MOE_TILE
    block_e = jnp.sum(pend[None, :] <= jnp.minimum(block_start, pend[-1] - 1)[:, None], axis=1)
    block_e = jnp.minimum(block_e, N_EXPERTS - 1).astype(jnp.int32)
    return row_tok, gates, dest, block_e, n_used.reshape(1)


def kernel(x_prompt, x_sample, cache_k, cache_v, state_ssm_re, state_ssm_im, norm_mix_g, w_in, q_norm_g, k_norm_g, attn_sinks, ssm_lambda_re, ssm_lambda_im, ssm_b_re, ssm_b_im, ssm_c_re, ssm_c_im, ssm_d, ssm_log_dt, w_glu, b_glu, attn_out_norm_g, ssm_out_norm_g, w_out, norm_ffn_g, w_router, b_router, w_gate_up, b_gate_up, w_down, b_down):
    depth = w_in.shape[0]
    assert depth == 1
    bp, sp, _ = x_prompt.shape
    bs, ss, _ = x_sample.shape
    tp, ts = bp * sp, bs * ss
    assert bp == SUBLANES and sp % S5_CHUNK == 0 and bs % SUBLANES == 0

    perm = jnp.asarray(_Q_PERM)
    w_in0 = w_in[0]
    wq = w_in0[:, :D_ATTN].reshape(D_MODEL, N_HEADS, HEAD_DIM)[:, perm].reshape(D_MODEL, D_ATTN)
    w_in_b = jnp.concatenate([wq, w_in0[:, D_ATTN:]], axis=1).astype(BF16)
    qkg = jnp.concatenate([jnp.tile(q_norm_g[0], N_HEADS), jnp.tile(k_norm_g[0], N_KV_HEADS)])[None]
    pmat = jnp.kron(jnp.eye(256 // HEAD_DIM, dtype=F32),
                    jnp.full((HEAD_DIM, HEAD_DIM), 1.0 / HEAD_DIM, F32)).astype(BF16)
    g_mix = norm_mix_g[0][None]
    sinks = attn_sinks[0]
    g_attn = attn_out_norm_g[0].reshape(N_HEADS, HEAD_DIM)[perm].reshape(1, D_ATTN)
    w_out0 = w_out[0]
    w_out_a = w_out0[:D_ATTN].reshape(N_HEADS, HEAD_DIM, D_MODEL)[perm].reshape(D_ATTN, D_MODEL).astype(BF16)
    w_out_s = w_out0[D_ATTN:].astype(BF16)
    g_ssm = ssm_out_norm_g[0][None]
    g_ffn = norm_ffn_g[0][None]
    w_r = jnp.pad(w_router[0], ((0, 0), (0, 128 - N_EXPERTS))).astype(BF16)
    b_r = jnp.pad(b_router[0], (0, 128 - N_EXPERTS), constant_values=NEG_INF)[None]

    a_re, a_im, bb_re, bb_im = _s5_prep(
        ssm_lambda_re[0], ssm_lambda_im[0], ssm_log_dt[0][:, None],
        jnp.swapaxes(ssm_b_re[0], 1, 2), jnp.swapaxes(ssm_b_im[0], 1, 2))
    eye_g = jnp.eye(N_SSM_GROUPS, dtype=F32)
    bd_b = lambda bb: jnp.einsum("ghp,gk->ghkp", bb, eye_g).reshape(D_SSM, D_STATE)
    wb = jnp.concatenate([bd_b(bb_re), bd_b(bb_im)], axis=1).astype(BF16)
    bd_c = lambda c: jnp.einsum("ghp,gk->gpkh", c, eye_g).reshape(D_STATE, D_SSM)
    wc = jnp.concatenate([bd_c(ssm_c_re[0]), -bd_c(ssm_c_im[0])], axis=0).astype(BF16)
    a_re, a_im = a_re.reshape(1, D_STATE), a_im.reshape(1, D_STATE)
    d_skip = ssm_d[0].reshape(1, D_SSM)
    w_glu_b = w_glu[0].astype(BF16)
    b_glu0 = b_glu[0][None]

    b_g = b_gate_up[0][:, None, 0::2]
    b_l = b_gate_up[0][:, None, 1::2]
    b_d = b_down[0][:, None, :]
    idx = jnp.arange(256)
    deint = (idx[None, :] == jnp.where(idx % 2 == 0, idx // 2, 128 + idx // 2)[:, None]).astype(BF16)

    xp2 = x_prompt.reshape(tp, D_MODEL)
    xs2 = x_sample.reshape(ts, D_MODEL)
    qp, kp, vp, up = _in_proj(xp2, g_mix, w_in_b, qkg, pmat)
    qs, ks, vs, us = _in_proj(xs2, g_mix, w_in_b, qkg, pmat)

    kp3, vp3 = kp.reshape(bp, sp, D_KV), vp.reshape(bp, sp, D_KV)
    attn_p = _band_attention(sinks, qp.reshape(bp, sp, D_ATTN), kp3, vp3, g_attn).reshape(tp, D_ATTN)
    ck = cache_k[0].reshape(bs, -1, D_KV)
    cv = cache_v[0].reshape(bs, -1, D_KV)
    attn_s = _cache_attention(sinks, qs, ks, vs, ck, cv, g_attn, ss)

    zeros_p = jnp.zeros((bp, D_STATE), F32)
    s5_args = (a_re, a_im, wb, wc, d_skip, w_glu_b, b_glu0, g_ssm)
    ssm_p, hr_p, hi_p = _s5(up.reshape(bp, sp, D_SSM), zeros_p, zeros_p, *s5_args,
                            tt=S5_CHUNK, stride=S5_CHUNK + S5_PAD, time_chunked=True)
    ssm_s, hr_s, hi_s = _s5(us.reshape(bs // SUBLANES, SUBLANES * ss, D_SSM),
                            state_ssm_re[0].reshape(bs, D_STATE), state_ssm_im[0].reshape(bs, D_STATE),
                            *s5_args, tt=ss, stride=ss, time_chunked=False)

    x1p, xnp_, lgp = _out_proj(attn_p, ssm_p.reshape(tp, D_SSM), xp2, w_out_a, w_out_s, g_ffn, w_r, b_r)
    x1s, xns, lgs = _out_proj(attn_s, ssm_s.reshape(ts, D_SSM), xs2, w_out_a, w_out_s, g_ffn, w_r, b_r)

    xn_all = jnp.concatenate([xnp_, xns], axis=0)
    logits = jnp.concatenate([lgp, lgs], axis=0)
    row_tok, gates, pos, block_e, n_used = _route(logits)
    out_rows = _moe(block_e, n_used, xn_all[row_tok], w_gate_up[0], b_g, b_l, w_down[0], b_d, deint)
    yp = _combine(pos[:tp * TOP_K], out_rows, x1p, gates[:tp]).reshape(bp, sp, D_MODEL)
    ys = _combine(pos[tp * TOP_K:], out_rows, x1s, gates[tp:]).reshape(bs, ss, D_MODEL)

    kv5 = lambda a, b_: a.reshape(b_, -1, N_KV_HEADS, HEAD_DIM)
    new_kp = kv5(kp3[:, -WINDOW:], bp)[None]
    new_vp = kv5(vp3[:, -WINDOW:], bp)[None]
    ks3, vs3 = ks.reshape(bs, ss, D_KV), vs.reshape(bs, ss, D_KV)
    new_ks = kv5(jnp.concatenate([ck, ks3], axis=1)[:, ss:], bs)[None]
    new_vs = kv5(jnp.concatenate([cv, vs3], axis=1)[:, ss:], bs)[None]
    st = lambda h, b_: h.reshape(1, b_, N_SSM_GROUPS, SSM_STATE)
    return (yp, ys, new_kp, new_vp, st(hr_p, bp), st(hi_p, bp),
            new_ks, new_vs, st(hr_s, bs), st(hi_s, bs))
```

```python
import functools
import math

import jax
import jax.numpy as jnp
from jax import lax
from jax.experimental import pallas as pl
from jax.experimental.pallas import tpu as pltpu

F32 = jnp.float32
BF16 = jnp.bfloat16

D_MODEL = 1024
D_ATTN = 512
D_SSM = 512
HEAD_DIM = 64
N_HEADS = 8
N_KV_HEADS = 2
D_KV = N_KV_HEADS * HEAD_DIM
WINDOW = 128
SSM_GROUP = 16
N_SSM_GROUPS = 32
SSM_STATE = 64
D_STATE = N_SSM_GROUPS * SSM_STATE
N_EXPERTS = 32
TOP_K = 4
D_FF = 1024
SWIGLU_LIMIT = 7.0
SWIGLU_ALPHA = 1.702
RMS_EPS = 1e-6
NEG_INF = -1e30
D_IN_PROJ = D_ATTN + 2 * D_KV + D_SSM
D_QK = D_ATTN + D_KV

SUBLANES = 8
VMEM_LIMIT = 56 * 1024 * 1024

ROW_TILE = 512
MOE_TILE = 256
COMBINE_TILE = 128
S5_CHUNK = 128
S5_PAD = 8
S5_COLS = 512

_Q_PERM = (0, 4, 1, 5, 2, 6, 3, 7)


def _cparams(*sem):
    return pltpu.CompilerParams(dimension_semantics=sem, vmem_limit_bytes=VMEM_LIMIT)


def _dot(a, b):
    return jnp.dot(a, b, preferred_element_type=F32)


def _rms(x):
    return x * lax.rsqrt(jnp.mean(x * x, axis=-1, keepdims=True) + RMS_EPS)


def _in_proj_kernel(x_ref, g_ref, w_ref, qkg_ref, p_ref, q_ref, k_ref, v_ref, u_ref):
    xn = _rms(x_ref[...]) * g_ref[...]
    h = _dot(xn.astype(BF16), w_ref[...])
    qk = h[:, :D_QK]
    sq = (qk * qk).astype(BF16)
    p = p_ref[...]
    ms = jnp.concatenate(
        [_dot(sq[:, 0:256], p), _dot(sq[:, 256:512], p), _dot(sq[:, 512:640], p[:128, :128])],
        axis=-1)
    qkn = qk * lax.rsqrt(ms + RMS_EPS) * qkg_ref[...]
    q_ref[...] = qkn[:, :D_ATTN].astype(BF16)
    k_ref[...] = qkn[:, D_ATTN:]
    v_ref[...] = h[:, D_QK:D_QK + D_KV]
    u_ref[...] = h[:, D_QK + D_KV:]


def _in_proj(x2d, g, w, qkg, pmat):
    t = x2d.shape[0]
    tm = min(ROW_TILE, t)
    row = lambda i: (i, 0)
    fix = lambda i: (0, 0)
    return pl.pallas_call(
        _in_proj_kernel,
        grid=(t // tm,),
        in_specs=[pl.BlockSpec((tm, D_MODEL), row), pl.BlockSpec((1, D_MODEL), fix),
                  pl.BlockSpec((D_MODEL, D_IN_PROJ), fix), pl.BlockSpec((1, D_QK), fix),
                  pl.BlockSpec((256, 256), fix)],
        out_specs=[pl.BlockSpec((tm, D_ATTN), row), pl.BlockSpec((tm, D_KV), row),
                   pl.BlockSpec((tm, D_KV), row), pl.BlockSpec((tm, D_SSM), row)],
        out_shape=[jax.ShapeDtypeStruct((t, D_ATTN), BF16), jax.ShapeDtypeStruct((t, D_KV), F32),
                   jax.ShapeDtypeStruct((t, D_KV), F32), jax.ShapeDtypeStruct((t, D_SSM), F32)],
        compiler_params=_cparams("parallel"),
        name="in_proj",
    )(x2d, g, w, qkg, pmat)


def _softmax_pv(s_blocks, v_blocks, sink):
    m = sink
    for s in s_blocks:
        m = jnp.maximum(m, jnp.max(s, axis=-1, keepdims=True))
    den = jnp.exp(sink - m)
    acc = None
    for s, v in zip(s_blocks, v_blocks):
        p = jnp.exp(s - m)
        den = den + jnp.sum(p, axis=-1, keepdims=True)
        pv = _dot(p.astype(BF16), v)
        acc = pv if acc is None else acc + pv
    return acc / den


def _band_attn_kernel(sink_ref, q_ref, kp_ref, kc_ref, vp_ref, vc_ref, g_ref, o_ref):
    i = pl.program_id(1)
    q = q_ref[...]
    kb = jnp.concatenate([kp_ref[...], kc_ref[...]], axis=0).astype(BF16)
    vb = jnp.concatenate([vp_ref[...], vc_ref[...]], axis=0).astype(BF16)
    row = lax.broadcasted_iota(jnp.int32, (WINDOW, 2 * WINDOW), 0)
    col = lax.broadcasted_iota(jnp.int32, (WINDOW, 2 * WINDOW), 1)
    mask = (col > row) & (col <= row + WINDOW) & ((col >= WINDOW) | (i > 0))
    lane = lax.broadcasted_iota(jnp.int32, (WINDOW, 128), 1)
    low = lane < HEAD_DIM
    zero = jnp.zeros((), BF16)
    outs = []
    for pair in range(N_HEADS // 2):
        qp = q[:, 128 * pair:128 * (pair + 1)]
        halves = []
        for par in range(2):
            qm = jnp.where(low if par == 0 else ~low, qp, zero)
            s = lax.dot_general(qm, kb, (((1,), (1,)), ((), ())), preferred_element_type=F32)
            s = jnp.where(mask, s * (HEAD_DIM ** -0.5), NEG_INF)
            halves.append(_softmax_pv([s], [vb], sink_ref[_Q_PERM[2 * pair + par]]))
        outs.append(jnp.where(low, halves[0], halves[1]))
    o = jnp.concatenate(outs, axis=-1)
    o_ref[...] = (_rms(o) * g_ref[...]).astype(BF16)


def _band_attention(sinks, q, k, v, g):
    b, s, _ = q.shape
    nb = s // WINDOW
    cur = lambda bi, i: (bi, i, 0)
    prev = lambda bi, i: (bi, jnp.maximum(i - 1, 0), 0)
    kv_spec = lambda im: pl.BlockSpec((None, WINDOW, D_KV), im)
    return pl.pallas_call(
        _band_attn_kernel,
        grid=(b, nb),
        in_specs=[pl.BlockSpec(memory_space=pltpu.SMEM),
                  pl.BlockSpec((None, WINDOW, D_ATTN), cur),
                  kv_spec(prev), kv_spec(cur), kv_spec(prev), kv_spec(cur),
                  pl.BlockSpec((1, D_ATTN), lambda bi, i: (0, 0))],
        out_specs=pl.BlockSpec((None, WINDOW, D_ATTN), cur),
        out_shape=jax.ShapeDtypeStruct((b, s, D_ATTN), BF16),
        compiler_params=_cparams("parallel", "parallel"),
        name="band_attn",
    )(sinks, q, k, k, v, v, g)


_PAIR_ROWS = 8
_CACHE_BB = 8


def _cache_attn_kernel(sink_ref, q_ref, kn_ref, vn_ref, ck_ref, cv_ref, g_ref, o_ref, *, n_new):
    n_buf = ck_ref.shape[1]
    rows_blk = _CACHE_BB * n_new
    knew = kn_ref[...].astype(BF16)
    vnew = vn_ref[...].astype(BF16)
    lane = lax.broadcasted_iota(jnp.int32, (_PAIR_ROWS, 128), 1)
    low = lane < HEAD_DIM
    zero = jnp.zeros((), BF16)
    n_stack = N_HEADS * _PAIR_ROWS
    r = lax.broadcasted_iota(jnp.int32, (n_stack, 1), 0) % _PAIR_ROWS
    r_seq, r_tok = r // n_new, r % n_new
    colc = lax.broadcasted_iota(jnp.int32, (n_stack, 2 * n_buf), 1)
    c_seq, c_pos = colc // n_buf, colc % n_buf
    mask_c = (c_seq == r_seq) & (c_pos + WINDOW > r_tok + n_buf)
    coln = lax.broadcasted_iota(jnp.int32, (n_stack, rows_blk), 1)
    sink_col = jnp.concatenate(
        [jnp.full((_PAIR_ROWS, 1), sink_ref[_Q_PERM[h]], F32) for h in range(N_HEADS)], axis=0)
    for sp in range(_CACHE_BB // 2):
        q = q_ref[_PAIR_ROWS * sp:_PAIR_ROWS * (sp + 1), :]
        pieces = []
        for pair in range(N_HEADS // 2):
            qp = q[:, 128 * pair:128 * (pair + 1)]
            pieces.append(jnp.where(low, qp, zero))
            pieces.append(jnp.where(low, zero, qp))
        qs = jnp.concatenate(pieces, axis=0)
        kc = jnp.concatenate([ck_ref[2 * sp], ck_ref[2 * sp + 1]], axis=0).astype(BF16)
        vc = jnp.concatenate([cv_ref[2 * sp], cv_ref[2 * sp + 1]], axis=0).astype(BF16)
        nt = (((1,), (1,)), ((), ()))
        s_c = lax.dot_general(qs, kc, nt, preferred_element_type=F32) * (HEAD_DIM ** -0.5)
        s_n = lax.dot_general(qs, knew, nt, preferred_element_type=F32) * (HEAD_DIM ** -0.5)
        s_c = jnp.where(mask_c, s_c, NEG_INF)
        n_seq, n_tok = coln // n_new - 2 * sp, coln % n_new
        mask_n = (n_seq == r_seq) & (n_tok <= r_tok)
        s_n = jnp.where(mask_n, s_n, NEG_INF)
        o = _softmax_pv([s_c, s_n], [vc, vnew], sink_col)
        outs = [jnp.where(low, o[16 * pair:16 * pair + 8], o[16 * pair + 8:16 * pair + 16])
                for pair in range(N_HEADS // 2)]
        oo = jnp.concatenate(outs, axis=-1)
        o_ref[_PAIR_ROWS * sp:_PAIR_ROWS * (sp + 1), :] = (_rms(oo) * g_ref[...]).astype(BF16)


def _cache_attention(sinks, q, k, v, cache_k, cache_v, g, n_new):
    t = q.shape[0]
    nb, n_buf, _ = cache_k.shape
    assert n_new * 2 == _PAIR_ROWS and nb % _CACHE_BB == 0 and n_buf == WINDOW
    rows = _CACHE_BB * n_new
    row = lambda i: (i, 0)
    cache_spec = pl.BlockSpec((_CACHE_BB, n_buf, D_KV), lambda i: (i, 0, 0))
    return pl.pallas_call(
        functools.partial(_cache_attn_kernel, n_new=n_new),
        grid=(nb // _CACHE_BB,),
        in_specs=[pl.BlockSpec(memory_space=pltpu.SMEM),
                  pl.BlockSpec((rows, D_ATTN), row), pl.BlockSpec((rows, D_KV), row),
                  pl.BlockSpec((rows, D_KV), row), cache_spec, cache_spec,
                  pl.BlockSpec((1, D_ATTN), lambda i: (0, 0))],
        out_specs=pl.BlockSpec((rows, D_ATTN), row),
        out_shape=jax.ShapeDtypeStruct((t, D_ATTN), BF16),
        compiler_params=_cparams("parallel"),
        name="cache_attn",
    )(sinks, q, k, v, cache_k, cache_v, g)


def _s5_prep_kernel(lre_ref, lim_ref, ldt_ref, bre_ref, bim_ref, are_ref, aim_ref, bbre_ref, bbim_ref):
    dt = jnp.exp(ldt_ref[...])
    l_re = jnp.minimum(lre_ref[...], -1e-4)
    l_im = lim_ref[...]
    mag = jnp.exp(l_re * dt)
    a_re = mag * jnp.cos(l_im * dt)
    a_im = mag * jnp.sin(l_im * dt)
    den = l_re * l_re + l_im * l_im
    n_re = a_re - 1.0
    z_re = (n_re * l_re + a_im * l_im) / den
    z_im = (a_im * l_re - n_re * l_im) / den
    are_ref[...] = a_re
    aim_ref[...] = a_im
    br, bi = bre_ref[...], bim_ref[...]
    zr, zi = z_re[:, None, :], z_im[:, None, :]
    bbre_ref[...] = zr * br - zi * bi
    bbim_ref[...] = zr * bi + zi * br


def _s5_prep(lam_re, lam_im, log_dt, b_re_t, b_im_t):
    g, p = lam_re.shape
    sd = jax.ShapeDtypeStruct
    return pl.pallas_call(
        _s5_prep_kernel,
        out_shape=[sd((g, p), F32), sd((g, p), F32), sd(b_re_t.shape, F32), sd(b_re_t.shape, F32)],
        name="s5_prep",
    )(lam_re, lam_im, log_dt, b_re_t, b_im_t)


def _s5_kernel(u_ref, h0r_ref, h0i_ref, ar_ref, ai_ref, wb_ref, wc_ref, d_ref, wglu_ref, bglu_ref,
               g_ref, o_ref, hr_ref, hi_ref, bu_ref, hs_ref, *, tt, stride):
    j = pl.program_id(1)
    rows = SUBLANES * tt
    n_tiles = D_STATE // 128

    @pl.when(j == 0)
    def _():
        hs_ref[:, :D_STATE] = h0r_ref[...]
        hs_ref[:, D_STATE:] = h0i_ref[...]

    u = u_ref[...].reshape(rows, D_SSM)
    ub = u.astype(BF16)

    def put(c, val):
        if stride == tt:
            bu_ref[c, 0:rows, :] = val
        else:
            for b in range(SUBLANES):
                bu_ref[c, b * stride:b * stride + tt, :] = val[b * tt:(b + 1) * tt]

    def get(c):
        if stride == tt:
            return bu_ref[c, 0:rows, :]
        return jnp.concatenate([bu_ref[c, b * stride:b * stride + tt, :] for b in range(SUBLANES)], axis=0)

    for n in range(2 * D_STATE // 256):
        band = (n % (D_STATE // 256)) // 2
        res = _dot(ub[:, 128 * band:128 * (band + 1)],
                   wb_ref[128 * band:128 * (band + 1), 256 * n:256 * (n + 1)])
        put(2 * n, res[:, :128])
        put(2 * n + 1, res[:, 128:])

    tiles_per_pass = S5_COLS // 128
    for c0 in range(0, n_tiles, tiles_per_pass):
        tiles = range(c0, c0 + tiles_per_pass)
        a_r = [jnp.broadcast_to(ar_ref[:, 128 * c:128 * (c + 1)], (SUBLANES, 128)) for c in tiles]
        a_i = [jnp.broadcast_to(ai_ref[:, 128 * c:128 * (c + 1)], (SUBLANES, 128)) for c in tiles]

        def step(t, carry, tiles=tiles, a_r=a_r, a_i=a_i):
            at_t = pl.ds(t, SUBLANES, stride=stride)
            out = []
            for k, c in enumerate(tiles):
                h_r, h_i = carry[2 * k], carry[2 * k + 1]
                n_r = a_r[k] * h_r - a_i[k] * h_i + bu_ref[c, at_t, :]
                n_i = a_r[k] * h_i + a_i[k] * h_r + bu_ref[n_tiles + c, at_t, :]
                bu_ref[c, at_t, :] = n_r
                bu_ref[n_tiles + c, at_t, :] = n_i
                out += [n_r, n_i]
            return tuple(out)

        init = []
        for c in tiles:
            init += [hs_ref[:, 128 * c:128 * (c + 1)], hs_ref[:, D_STATE + 128 * c:D_STATE + 128 * (c + 1)]]
        fin = lax.fori_loop(0, tt, step, tuple(init), unroll=min(tt, 8))
        for k, c in enumerate(tiles):
            hs_ref[:, 128 * c:128 * (c + 1)] = fin[2 * k]
            hs_ref[:, D_STATE + 128 * c:D_STATE + 128 * (c + 1)] = fin[2 * k + 1]

    def h_cols(first_tile):
        return jnp.concatenate([get(first_tile + k) for k in range(4)], axis=-1).astype(BF16)

    ys = []
    for m in range(D_SSM // 128):
        y = _dot(h_cols(4 * m), wc_ref[512 * m:512 * (m + 1), 128 * m:128 * (m + 1)])
        y = y + _dot(h_cols(n_tiles + 4 * m),
                     wc_ref[D_STATE + 512 * m:D_STATE + 512 * (m + 1), 128 * m:128 * (m + 1)])
        ys.append(y)
    y = jnp.concatenate(ys, axis=-1) + d_ref[...] * u
    z = _dot(jax.nn.gelu(y).astype(BF16), wglu_ref[...]) + bglu_ref[...]
    s = z[:, :D_SSM] * jax.nn.sigmoid(z[:, D_SSM:])
    o_ref[...] = (_rms(s) * g_ref[...]).astype(BF16).reshape(o_ref.shape)

    @pl.when(j == pl.num_programs(1) - 1)
    def _():
        hr_ref[...] = hs_ref[:, :D_STATE]
        hi_ref[...] = hs_ref[:, D_STATE:]


def _s5(u, h0r, h0i, a_re, a_im, wb, wc, d, wglu, bglu, g, *, tt, stride, time_chunked):
    nbg = h0r.shape[0] // SUBLANES
    if time_chunked:
        nchunks = u.shape[1] // tt
        u_spec = pl.BlockSpec((SUBLANES, tt, D_SSM), lambda gi, j: (gi, j, 0))
    else:
        nchunks = 1
        u_spec = pl.BlockSpec((None, SUBLANES * tt, D_SSM), lambda gi, j: (gi, 0, 0))
    fix = lambda gi, j: (0, 0)
    st_spec = pl.BlockSpec((SUBLANES, D_STATE), lambda gi, j: (gi, 0))
    sd = jax.ShapeDtypeStruct
    return pl.pallas_call(
        functools.partial(_s5_kernel, tt=tt, stride=stride),
        grid=(nbg, nchunks),
        in_specs=[u_spec, st_spec, st_spec,
                  pl.BlockSpec((1, D_STATE), fix), pl.BlockSpec((1, D_STATE), fix),
                  pl.BlockSpec((D_SSM, 2 * D_STATE), fix), pl.BlockSpec((2 * D_STATE, D_SSM), fix),
                  pl.BlockSpec((1, D_SSM), fix), pl.BlockSpec((D_SSM, 2 * D_SSM), fix),
                  pl.BlockSpec((1, 2 * D_SSM), fix), pl.BlockSpec((1, D_SSM), fix)],
        out_specs=[u_spec, st_spec, st_spec],
        out_shape=[sd(u.shape, BF16), sd(h0r.shape, F32), sd(h0r.shape, F32)],
        scratch_shapes=[pltpu.VMEM((2 * D_STATE // 128, SUBLANES * stride, 128), F32),
                        pltpu.VMEM((SUBLANES, 2 * D_STATE), F32)],
        compiler_params=_cparams("parallel", "arbitrary"),
        name="s5",
    )(u, h0r, h0i, a_re, a_im, wb, wc, d, wglu, bglu, g)


def _to_row_tiles(ref, val):
    rows = val.shape[0]
    for c in range(D_MODEL // 128):
        ref[pl.ds(c, rows, stride=SUBLANES), :] = val[:, 128 * c:128 * (c + 1)]


def _from_row_tiles(ref, rows, lead=()):
    return jnp.concatenate(
        [ref[(*lead, pl.ds(c, rows, stride=SUBLANES), slice(None))] for c in range(D_MODEL // 128)], axis=-1)


def _out_proj_kernel(a_ref, s_ref, x_ref, wa_ref, ws_ref, g_ref, wr_ref, br_ref, x1_ref, xt_ref, lg_ref):
    x1 = x_ref[...] + _dot(a_ref[...], wa_ref[...]) + _dot(s_ref[...], ws_ref[...])
    x1_ref[...] = x1
    xn = _rms(x1) * g_ref[...]
    _to_row_tiles(xt_ref, xn)
    lg_ref[...] = _dot(xn.astype(BF16), wr_ref[...]) + br_ref[...]


def _out_proj(attn_n, ssm_n, x2d, wa, ws, g, wr, br):
    t = x2d.shape[0]
    tm = min(ROW_TILE, t)
    row = lambda i: (i, 0)
    fix = lambda i: (0, 0)
    sd = jax.ShapeDtypeStruct
    return pl.pallas_call(
        _out_proj_kernel,
        grid=(t // tm,),
        in_specs=[pl.BlockSpec((tm, D_ATTN), row), pl.BlockSpec((tm, D_SSM), row),
                  pl.BlockSpec((tm, D_MODEL), row),
                  pl.BlockSpec((D_ATTN, D_MODEL), fix), pl.BlockSpec((D_SSM, D_MODEL), fix),
                  pl.BlockSpec((1, D_MODEL), fix), pl.BlockSpec((D_MODEL, 128), fix),
                  pl.BlockSpec((1, 128), fix)],
        out_specs=[pl.BlockSpec((tm, D_MODEL), row), pl.BlockSpec((tm * SUBLANES, 128), row),
                   pl.BlockSpec((tm, 128), row)],
        out_shape=[sd((t, D_MODEL), F32), sd((t * SUBLANES, 128), F32), sd((t, 128), F32)],
        compiler_params=_cparams("parallel"),
        name="out_proj",
    )(attn_n, ssm_n, x2d, wa, ws, g, wr, br)


def _tile_rows(ref, r):
    return ref.at[pl.ds(pl.multiple_of(r * SUBLANES, SUBLANES), SUBLANES), :]


def _dispatch_kernel(pos_ref, pend_ref, padded_ref, nu_ref, xa_ref, xb_ref, rows_ref, zbuf, sem, zsem, *,
                     tiles_a, nblk):
    i = pl.program_id(0)
    block_rows = MOE_TILE * SUBLANES

    def zero_block(start_row):
        return pltpu.make_async_copy(
            zbuf, rows_ref.at[pl.ds(pl.multiple_of(start_row * SUBLANES, SUBLANES), block_rows), :], zsem)

    @pl.when(i == 0)
    def _():
        zbuf[...] = jnp.zeros_like(zbuf)
        for e in range(N_EXPERTS):
            @pl.when(padded_ref[e] > 0)
            def _(e=e):
                zero_block(pend_ref[e] - MOE_TILE).start()

        def tail_start(j, c):
            zero_block(j * MOE_TILE).start()
            return c

        def tail_wait(j, c):
            zero_block(j * MOE_TILE).wait()
            return c

        lax.fori_loop(nu_ref[0], nblk, tail_start, 0)
        for e in range(N_EXPERTS):
            @pl.when(padded_ref[e] > 0)
            def _(e=e):
                zero_block(pend_ref[e] - MOE_TILE).wait()
        lax.fori_loop(nu_ref[0], nblk, tail_wait, 0)

    def scatter(x_ref):
        def body(t, c):
            for k in range(TOP_K):
                r = pos_ref[(i * COMBINE_TILE + t) * TOP_K + k]
                pltpu.make_async_copy(_tile_rows(x_ref, t), _tile_rows(rows_ref, r), sem).start()
            return c

        lax.fori_loop(0, COMBINE_TILE, body, 0, unroll=4)
        for _ in range(TOP_K):
            pltpu.make_async_copy(x_ref, x_ref, sem).wait()

    @pl.when(i < tiles_a)
    def _():
        scatter(xa_ref)

    @pl.when(i >= tiles_a)
    def _():
        scatter(xb_ref)


def _dispatch(pos_flat, pend, padded, n_used, xa, xb, n_rows):
    tiles_a = xa.shape[0] // (COMBINE_TILE * SUBLANES)
    tiles_b = xb.shape[0] // (COMBINE_TILE * SUBLANES)
    nblk = n_rows // MOE_TILE
    tile = (COMBINE_TILE * SUBLANES, 128)
    grid_spec = pltpu.PrefetchScalarGridSpec(
        num_scalar_prefetch=4, grid=(tiles_a + tiles_b,),
        in_specs=[pl.BlockSpec(tile, lambda i, *_: (jnp.minimum(i, tiles_a - 1), 0)),
                  pl.BlockSpec(tile, lambda i, *_: (jnp.maximum(i - tiles_a, 0), 0))],
        out_specs=pl.BlockSpec(memory_space=pl.ANY),
        scratch_shapes=[pltpu.VMEM((MOE_TILE * SUBLANES, 128), F32), pltpu.SemaphoreType.DMA(()),
                        pltpu.SemaphoreType.DMA(())])
    return pl.pallas_call(
        functools.partial(_dispatch_kernel, tiles_a=tiles_a, nblk=nblk), grid_spec=grid_spec,
        out_shape=jax.ShapeDtypeStruct((n_rows * SUBLANES, 128), F32),
        compiler_params=_cparams("arbitrary"), name="dispatch",
    )(pos_flat, pend, padded, n_used, xa, xb)


def _moe_kernel(be_ref, nu_ref, x_ref, wgu_ref, bg_ref, bl_ref, wd_ref, bd_ref, perm_ref, o_ref,
                wg_s, wl_s, wd_s):
    i = pl.program_id(0)
    used = i < nu_ref[0]
    new_expert = (i == 0) | (be_ref[i] != be_ref[jnp.maximum(i - 1, 0)])

    @pl.when(used & new_expert)
    def _():
        for c in range(2 * D_FF // 256):
            r = _dot(wgu_ref[:, 256 * c:256 * (c + 1)].astype(BF16), perm_ref[...])
            wg_s[:, 128 * c:128 * (c + 1)] = r[:, :128].astype(BF16)
            wl_s[:, 128 * c:128 * (c + 1)] = r[:, 128:].astype(BF16)
        wd_s[...] = wd_ref[...].astype(BF16)

    @pl.when(used)
    def _():
        x = _from_row_tiles(x_ref, MOE_TILE).astype(BF16)
        glu = jnp.minimum(_dot(x, wg_s[...]) + bg_ref[...], SWIGLU_LIMIT)
        lin = jnp.clip(_dot(x, wl_s[...]) + bl_ref[...], -SWIGLU_LIMIT, SWIGLU_LIMIT)
        act = glu * jax.nn.sigmoid(SWIGLU_ALPHA * glu) * (lin + 1.0)
        _to_row_tiles(o_ref, _dot(act.astype(BF16), wd_s[...]) + bd_ref[...])

    @pl.when(jnp.logical_not(used))
    def _():
        o_ref[...] = jnp.zeros_like(o_ref)


def _moe(block_e, n_used, x_rows, wgu, bg, bl, wd, bd, perm):
    nblk = x_rows.shape[0] // (MOE_TILE * SUBLANES)
    row = lambda i, be, nu: (jnp.minimum(i, nu[0] - 1), 0)
    wsel = lambda i, be, nu: (be[i], 0, 0)
    grid_spec = pltpu.PrefetchScalarGridSpec(
        num_scalar_prefetch=2,
        grid=(nblk,),
        in_specs=[pl.BlockSpec((MOE_TILE * SUBLANES, 128), row),
                  pl.BlockSpec((None, D_MODEL, 2 * D_FF), wsel),
                  pl.BlockSpec((None, 1, D_FF), wsel), pl.BlockSpec((None, 1, D_FF), wsel),
                  pl.BlockSpec((None, D_FF, D_MODEL), wsel), pl.BlockSpec((None, 1, D_MODEL), wsel),
                  pl.BlockSpec((256, 256), lambda i, be, nu: (0, 0))],
        out_specs=pl.BlockSpec((MOE_TILE * SUBLANES, 128), lambda i, be, nu: (i, 0)),
        scratch_shapes=[pltpu.VMEM((D_MODEL, D_FF), BF16), pltpu.VMEM((D_MODEL, D_FF), BF16),
                        pltpu.VMEM((D_FF, D_MODEL), BF16)],
    )
    return pl.pallas_call(
        _moe_kernel,
        grid_spec=grid_spec,
        out_shape=jax.ShapeDtypeStruct(x_rows.shape, F32),
        compiler_params=_cparams("arbitrary"),
        name="moe",
    )(block_e, n_used, x_rows, wgu, bg, bl, wd, bd, perm)


def _combine_kernel(pos_ref, rows_hbm, x1_ref, gate_ref, o_ref, buf, sem):
    i = pl.program_id(0)
    n = pl.num_programs(0)

    def row_copy(tile, slot, t, k):
        r = pos_ref[(tile * COMBINE_TILE + t) * TOP_K + k]
        return pltpu.make_async_copy(_tile_rows(rows_hbm, r), _tile_rows(buf.at[slot, k], t), sem.at[slot])

    def fetch(tile, slot):
        def body(t, c):
            for k in range(TOP_K):
                row_copy(tile, slot, t, k).start()
            return c
        lax.fori_loop(0, COMBINE_TILE, body, 0, unroll=4)

    @pl.when(i == 0)
    def _():
        fetch(0, 0)

    @pl.when(i + 1 < n)
    def _():
        fetch(i + 1, (i + 1) % 2)

    slot = i % 2
    pltpu.make_async_copy(buf.at[slot], buf.at[slot], sem.at[slot]).wait()
    g = gate_ref[...]
    y = x1_ref[...]
    for k in range(TOP_K):
        y = y + g[:, k:k + 1] * _from_row_tiles(buf, COMBINE_TILE, lead=(slot, k))
    o_ref[...] = y


def _combine(pos_flat, rows, x1, gates):
    t = x1.shape[0]
    grid_spec = pltpu.PrefetchScalarGridSpec(
        num_scalar_prefetch=1,
        grid=(t // COMBINE_TILE,),
        in_specs=[pl.BlockSpec(memory_space=pl.ANY),
                  pl.BlockSpec((COMBINE_TILE, D_MODEL), lambda i, p: (i, 0)),
                  pl.BlockSpec((COMBINE_TILE, TOP_K), lambda i, p: (i, 0))],
        out_specs=pl.BlockSpec((COMBINE_TILE, D_MODEL), lambda i, p: (i, 0)),
        scratch_shapes=[pltpu.VMEM((2, TOP_K, COMBINE_TILE * SUBLANES, 128), F32),
                        pltpu.SemaphoreType.DMA((2,))],
    )
    return pl.pallas_call(
        _combine_kernel,
        grid_spec=grid_spec,
        out_shape=jax.ShapeDtypeStruct((t, D_MODEL), F32),
        compiler_params=_cparams("arbitrary"),
        name="combine",
    )(pos_flat, rows, x1, gates)


def _lane_pick(cols):
    lane = lax.broadcasted_iota(jnp.int32, (cols[0].shape[0], len(cols)), 1)
    out = cols[-1]
    for k in range(len(cols) - 2, -1, -1):
        out = jnp.where(lane == k, cols[k], out)
    return out


def _topk_kernel(lg_ref, tri_ref, idx_ref, gate_ref, rank_ref, cnt_ref, carry):
    @pl.when(pl.program_id(0) == 0)
    def _():
        carry[...] = jnp.zeros_like(carry)

    l = lg_ref[...]
    lane = lax.broadcasted_iota(jnp.int32, l.shape, 1)
    vals, idxs, sels = [], [], []
    for _ in range(TOP_K):
        m = jnp.max(l, axis=-1, keepdims=True)
        idx = jnp.min(jnp.where(l == m, lane, l.shape[1]), axis=-1, keepdims=True)
        sel = lane == idx
        l = jnp.where(sel, -jnp.inf, l)
        vals.append(m)
        idxs.append(idx)
        sels.append(sel)
    exps = [jnp.exp(v - vals[0]) for v in vals]
    den = exps[0] + exps[1] + exps[2] + exps[3]
    onehot = jnp.where(sels[0] | sels[1] | sels[2] | sels[3], 1.0, 0.0)
    before = _dot(tri_ref[...], onehot.astype(BF16)) + carry[...]
    ranks = [jnp.sum(jnp.where(s, before, 0.0), axis=-1, keepdims=True) for s in sels]
    carry[...] = carry[...] + jnp.sum(onehot, axis=0, keepdims=True)
    idx_ref[...] = _lane_pick(idxs)
    gate_ref[...] = _lane_pick([e / den for e in exps])
    rank_ref[...] = _lane_pick(ranks).astype(jnp.int32)
    cnt_ref[...] = carry[...]


def _dest_kernel(idx_ref, rank_ref, start_ref, pos_ref):
    lane = lax.broadcasted_iota(jnp.int32, (idx_ref.shape[0], 128), 1)
    start = start_ref[...]
    idx = idx_ref[...]
    cols = [jnp.sum(jnp.where(lane == idx[:, k:k + 1], start, 0.0), axis=-1, keepdims=True)
            for k in range(TOP_K)]
    pos_ref[...] = _lane_pick(cols).astype(jnp.int32) + rank_ref[...]


def _route(logits):
    t = logits.shape[0]
    n_assign = t * TOP_K
    tr = ROW_TILE
    row = lambda i: (i, 0)
    fix = lambda i: (0, 0)
    sd = jax.ShapeDtypeStruct
    k_spec = pl.BlockSpec((tr, TOP_K), row)
    tri = (jnp.arange(tr)[:, None] > jnp.arange(tr)[None, :]).astype(BF16)
    top_i, gates, rank, counts = pl.pallas_call(
        _topk_kernel,
        grid=(t // tr,),
        in_specs=[pl.BlockSpec((tr, 128), row), pl.BlockSpec((tr, tr), fix)],
        out_specs=[k_spec, k_spec, k_spec, pl.BlockSpec((1, 128), fix)],
        out_shape=[sd((t, TOP_K), jnp.int32), sd((t, TOP_K), F32), sd((t, TOP_K), jnp.int32),
                   sd((1, 128), F32)],
        scratch_shapes=[pltpu.VMEM((1, 128), F32)],
        compiler_params=_cparams("arbitrary"),
        name="topk",
    )(logits, tri)
    counts = counts[0, :N_EXPERTS].astype(jnp.int32)
    padded = ((counts + MOE_TILE - 1) // MOE_TILE) * MOE_TILE
    pend = jnp.cumsum(padded)
    pstart = jnp.pad(pend - padded, (0, 128 - N_EXPERTS)).astype(F32)[None]
    dest = pl.pallas_call(
        _dest_kernel,
        grid=(t // tr,),
        in_specs=[k_spec, k_spec, pl.BlockSpec((1, 128), fix)],
        out_specs=k_spec,
        out_shape=sd((t, TOP_K), jnp.int32),
        compiler_params=_cparams("parallel"),
        name="dest",
    )(top_i, rank, pstart).reshape(-1)
    nblk = (n_assign + MOE_TILE - 1) // MOE_TILE + N_EXPERTS
    n_rows = nblk * MOE_TILE
    n_used = (pend[-1] // MOE_TILE).astype(jnp.int32)
    block_start = jnp.arange(nblk, dtype=jnp.int32) * MOE_TILE
    block_e = jnp.sum(pend[None, :] <= jnp.minimum(block_start, pend[-1] - 1)[:, None], axis=1)
    block_e = jnp.minimum(block_e, N_EXPERTS - 1).astype(jnp.int32)
    return gates, dest, block_e, n_used.reshape(1), pend, padded, n_rows


def kernel(x_prompt, x_sample, cache_k, cache_v, state_ssm_re, state_ssm_im, norm_mix_g, w_in, q_norm_g, k_norm_g, attn_sinks, ssm_lambda_re, ssm_lambda_im, ssm_b_re, ssm_b_im, ssm_c_re, ssm_c_im, ssm_d, ssm_log_dt, w_glu, b_glu, attn_out_norm_g, ssm_out_norm_g, w_out, norm_ffn_g, w_router, b_router, w_gate_up, b_gate_up, w_down, b_down):
    depth = w_in.shape[0]
    assert depth == 1
    bp, sp, _ = x_prompt.shape
    bs, ss, _ = x_sample.shape
    tp, ts = bp * sp, bs * ss
    assert bp == SUBLANES and sp % S5_CHUNK == 0 and bs % SUBLANES == 0

    perm = jnp.asarray(_Q_PERM)
    w_in0 = w_in[0]
    wq = w_in0[:, :D_ATTN].reshape(D_MODEL, N_HEADS, HEAD_DIM)[:, perm].reshape(D_MODEL, D_ATTN)
    w_in_b = jnp.concatenate([wq, w_in0[:, D_ATTN:]], axis=1).astype(BF16)
    qkg = jnp.concatenate([jnp.tile(q_norm_g[0], N_HEADS), jnp.tile(k_norm_g[0], N_KV_HEADS)])[None]
    pmat = jnp.kron(jnp.eye(256 // HEAD_DIM, dtype=F32),
                    jnp.full((HEAD_DIM, HEAD_DIM), 1.0 / HEAD_DIM, F32)).astype(BF16)
    g_mix = norm_mix_g[0][None]
    sinks = attn_sinks[0]
    g_attn = attn_out_norm_g[0].reshape(N_HEADS, HEAD_DIM)[perm].reshape(1, D_ATTN)
    w_out0 = w_out[0]
    w_out_a = w_out0[:D_ATTN].reshape(N_HEADS, HEAD_DIM, D_MODEL)[perm].reshape(D_ATTN, D_MODEL).astype(BF16)
    w_out_s = w_out0[D_ATTN:].astype(BF16)
    g_ssm = ssm_out_norm_g[0][None]
    g_ffn = norm_ffn_g[0][None]
    w_r = jnp.pad(w_router[0], ((0, 0), (0, 128 - N_EXPERTS))).astype(BF16)
    b_r = jnp.pad(b_router[0], (0, 128 - N_EXPERTS), constant_values=NEG_INF)[None]

    a_re, a_im, bb_re, bb_im = _s5_prep(
        ssm_lambda_re[0], ssm_lambda_im[0], ssm_log_dt[0][:, None],
        jnp.swapaxes(ssm_b_re[0], 1, 2), jnp.swapaxes(ssm_b_im[0], 1, 2))
    eye_g = jnp.eye(N_SSM_GROUPS, dtype=F32)
    bd_b = lambda bb: jnp.einsum("ghp,gk->ghkp", bb, eye_g).reshape(D_SSM, D_STATE)
    wb = jnp.concatenate([bd_b(bb_re), bd_b(bb_im)], axis=1).astype(BF16)
    bd_c = lambda c: jnp.einsum("ghp,gk->gpkh", c, eye_g).reshape(D_STATE, D_SSM)
    wc = jnp.concatenate([bd_c(ssm_c_re[0]), -bd_c(ssm_c_im[0])], axis=0).astype(BF16)
    a_re, a_im = a_re.reshape(1, D_STATE), a_im.reshape(1, D_STATE)
    d_skip = ssm_d[0].reshape(1, D_SSM)
    w_glu_b = w_glu[0].astype(BF16)
    b_glu0 = b_glu[0][None]

    b_g = b_gate_up[0][:, None, 0::2]
    b_l = b_gate_up[0][:, None, 1::2]
    b_d = b_down[0][:, None, :]
    idx = jnp.arange(256)
    deint = (idx[None, :] == jnp.where(idx % 2 == 0, idx // 2, 128 + idx // 2)[:, None]).astype(BF16)

    xp2 = x_prompt.reshape(tp, D_MODEL)
    xs2 = x_sample.reshape(ts, D_MODEL)
    qp, kp, vp, up = _in_proj(xp2, g_mix, w_in_b, qkg, pmat)
    qs, ks, vs, us = _in_proj(xs2, g_mix, w_in_b, qkg, pmat)

    kp3, vp3 = kp.reshape(bp, sp, D_KV), vp.reshape(bp, sp, D_KV)
    attn_p = _band_attention(sinks, qp.reshape(bp, sp, D_ATTN), kp3, vp3, g_attn).reshape(tp, D_ATTN)
    ck = cache_k[0].reshape(bs, -1, D_KV)
    cv = cache_v[0].reshape(bs, -1, D_KV)
    attn_s = _cache_attention(sinks, qs, ks, vs, ck, cv, g_attn, ss)

    zeros_p = jnp.zeros((bp, D_STATE), F32)
    s5_args = (a_re, a_im, wb, wc, d_skip, w_glu_b, b_glu0, g_ssm)
    ssm_p, hr_p, hi_p = _s5(up.reshape(bp, sp, D_SSM), zeros_p, zeros_p, *s5_args,
                            tt=S5_CHUNK, stride=S5_CHUNK + S5_PAD, time_chunked=True)
    ssm_s, hr_s, hi_s = _s5(us.reshape(bs // SUBLANES, SUBLANES * ss, D_SSM),
                            state_ssm_re[0].reshape(bs, D_STATE), state_ssm_im[0].reshape(bs, D_STATE),
                            *s5_args, tt=ss, stride=ss, time_chunked=False)

    x1p, xtp, lgp = _out_proj(attn_p, ssm_p.reshape(tp, D_SSM), xp2, w_out_a, w_out_s, g_ffn, w_r, b_r)
    x1s, xts, lgs = _out_proj(attn_s, ssm_s.reshape(ts, D_SSM), xs2, w_out_a, w_out_s, g_ffn, w_r, b_r)

    logits = jnp.concatenate([lgp, lgs], axis=0)
    gates, pos, block_e, n_used, pend, padded, n_rows = _route(logits)
    pos_p, pos_s = pos[:tp * TOP_K], pos[tp * TOP_K:]
    x_rows = _dispatch(pos, pend, padded, n_used, xtp, xts, n_rows)
    out_rows = _moe(block_e, n_used, x_rows, w_gate_up[0], b_g, b_l, w_down[0], b_d, deint)
    yp = _combine(pos_p, out_rows, x1p, gates[:tp]).reshape(bp, sp, D_MODEL)
    ys = _combine(pos_s, out_rows, x1s, gates[tp:]).reshape(bs, ss, D_MODEL)

    kv5 = lambda a, b_: a.reshape(b_, -1, N_KV_HEADS, HEAD_DIM)
    new_kp = kv5(kp3[:, -WINDOW:], bp)[None]
    new_vp = kv5(vp3[:, -WINDOW:], bp)[None]
    ks3, vs3 = ks.reshape(bs, ss, D_KV), vs.reshape(bs, ss, D_KV)
    new_ks = kv5(jnp.concatenate([ck, ks3], axis=1)[:, ss:], bs)[None]
    new_vs = kv5(jnp.concatenate([cv, vs3], axis=1)[:, ss:], bs)[None]
    st = lambda h, b_: h.reshape(1, b_, N_SSM_GROUPS, SSM_STATE)
    return (yp, ys, new_kp, new_vp, st(hr_p, bp), st(hi_p, bp),
            new_ks, new_vs, st(hr_s, bs), st(hi_s, bs))
```

```python
import functools
import math

import jax
import jax.numpy as jnp
from jax import lax
from jax.experimental import pallas as pl
from jax.experimental.pallas import tpu as pltpu

F32 = jnp.float32
BF16 = jnp.bfloat16

D_MODEL = 1024
D_ATTN = 512
D_SSM = 512
HEAD_DIM = 64
N_HEADS = 8
N_KV_HEADS = 2
D_KV = N_KV_HEADS * HEAD_DIM
WINDOW = 128
SSM_GROUP = 16
N_SSM_GROUPS = 32
SSM_STATE = 64
D_STATE = N_SSM_GROUPS * SSM_STATE
N_EXPERTS = 32
TOP_K = 4
D_FF = 1024
SWIGLU_LIMIT = 7.0
SWIGLU_ALPHA = 1.702
RMS_EPS = 1e-6
NEG_INF = -1e30
D_IN_PROJ = D_ATTN + 2 * D_KV + D_SSM
D_QK = D_ATTN + D_KV

SUBLANES = 8
VMEM_LIMIT = 56 * 1024 * 1024

ROW_TILE = 512
MOE_TILE = 512
COMBINE_TILE = 128
S5_CHUNK = 128
S5_PAD = 8
S5_COLS = 512

_Q_PERM = (0, 4, 1, 5, 2, 6, 3, 7)


def _cparams(*sem):
    return pltpu.CompilerParams(dimension_semantics=sem, vmem_limit_bytes=VMEM_LIMIT)


def _dot(a, b):
    return jnp.dot(a, b, preferred_element_type=F32)


def _rms(x):
    return x * lax.rsqrt(jnp.mean(x * x, axis=-1, keepdims=True) + RMS_EPS)


def _in_proj_kernel(x_ref, g_ref, w_ref, qkg_ref, p_ref, q_ref, k_ref, v_ref, u_ref):
    xn = _rms(x_ref[...]) * g_ref[...]
    h = _dot(xn.astype(BF16), w_ref[...])
    qk = h[:, :D_QK]
    sq = (qk * qk).astype(BF16)
    p = p_ref[...]
    ms = jnp.concatenate(
        [_dot(sq[:, 0:256], p), _dot(sq[:, 256:512], p), _dot(sq[:, 512:640], p[:128, :128])],
        axis=-1)
    qkn = qk * lax.rsqrt(ms + RMS_EPS) * qkg_ref[...]
    q_ref[...] = qkn[:, :D_ATTN].astype(BF16)
    k_ref[...] = qkn[:, D_ATTN:]
    v_ref[...] = h[:, D_QK:D_QK + D_KV]
    u_ref[...] = h[:, D_QK + D_KV:]


def _in_proj(x2d, g, w, qkg, pmat):
    t = x2d.shape[0]
    tm = min(ROW_TILE, t)
    row = lambda i: (i, 0)
    fix = lambda i: (0, 0)
    return pl.pallas_call(
        _in_proj_kernel,
        grid=(t // tm,),
        in_specs=[pl.BlockSpec((tm, D_MODEL), row), pl.BlockSpec((1, D_MODEL), fix),
                  pl.BlockSpec((D_MODEL, D_IN_PROJ), fix), pl.BlockSpec((1, D_QK), fix),
                  pl.BlockSpec((256, 256), fix)],
        out_specs=[pl.BlockSpec((tm, D_ATTN), row), pl.BlockSpec((tm, D_KV), row),
                   pl.BlockSpec((tm, D_KV), row), pl.BlockSpec((tm, D_SSM), row)],
        out_shape=[jax.ShapeDtypeStruct((t, D_ATTN), BF16), jax.ShapeDtypeStruct((t, D_KV), F32),
                   jax.ShapeDtypeStruct((t, D_KV), F32), jax.ShapeDtypeStruct((t, D_SSM), F32)],
        compiler_params=_cparams("parallel"),
        name="in_proj",
    )(x2d, g, w, qkg, pmat)


def _softmax_pv(s_blocks, v_blocks, sink):
    m = sink
    for s in s_blocks:
        m = jnp.maximum(m, jnp.max(s, axis=-1, keepdims=True))
    den = jnp.exp(sink - m)
    acc = None
    for s, v in zip(s_blocks, v_blocks):
        p = jnp.exp(s - m)
        den = den + jnp.sum(p, axis=-1, keepdims=True)
        pv = _dot(p.astype(BF16), v)
        acc = pv if acc is None else acc + pv
    return acc / den


def _band_attn_kernel(sink_ref, q_ref, kp_ref, kc_ref, vp_ref, vc_ref, g_ref, o_ref):
    i = pl.program_id(1)
    q = q_ref[...]
    kb = jnp.concatenate([kp_ref[...], kc_ref[...]], axis=0).astype(BF16)
    vb = jnp.concatenate([vp_ref[...], vc_ref[...]], axis=0).astype(BF16)
    row = lax.broadcasted_iota(jnp.int32, (WINDOW, 2 * WINDOW), 0)
    col = lax.broadcasted_iota(jnp.int32, (WINDOW, 2 * WINDOW), 1)
    mask = (col > row) & (col <= row + WINDOW) & ((col >= WINDOW) | (i > 0))
    lane = lax.broadcasted_iota(jnp.int32, (WINDOW, 128), 1)
    low = lane < HEAD_DIM
    zero = jnp.zeros((), BF16)
    outs = []
    for pair in range(N_HEADS // 2):
        qp = q[:, 128 * pair:128 * (pair + 1)]
        halves = []
        for par in range(2):
            qm = jnp.where(low if par == 0 else ~low, qp, zero)
            s = lax.dot_general(qm, kb, (((1,), (1,)), ((), ())), preferred_element_type=F32)
            s = jnp.where(mask, s * (HEAD_DIM ** -0.5), NEG_INF)
            halves.append(_softmax_pv([s], [vb], sink_ref[_Q_PERM[2 * pair + par]]))
        outs.append(jnp.where(low, halves[0], halves[1]))
    o = jnp.concatenate(outs, axis=-1)
    o_ref[...] = (_rms(o) * g_ref[...]).astype(BF16)


def _band_attention(sinks, q, k, v, g):
    b, s, _ = q.shape
    nb = s // WINDOW
    cur = lambda bi, i: (bi, i, 0)
    prev = lambda bi, i: (bi, jnp.maximum(i - 1, 0), 0)
    kv_spec = lambda im: pl.BlockSpec((None, WINDOW, D_KV), im)
    return pl.pallas_call(
        _band_attn_kernel,
        grid=(b, nb),
        in_specs=[pl.BlockSpec(memory_space=pltpu.SMEM),
                  pl.BlockSpec((None, WINDOW, D_ATTN), cur),
                  kv_spec(prev), kv_spec(cur), kv_spec(prev), kv_spec(cur),
                  pl.BlockSpec((1, D_ATTN), lambda bi, i: (0, 0))],
        out_specs=pl.BlockSpec((None, WINDOW, D_ATTN), cur),
        out_shape=jax.ShapeDtypeStruct((b, s, D_ATTN), BF16),
        compiler_params=_cparams("parallel", "parallel"),
        name="band_attn",
    )(sinks, q, k, k, v, v, g)


_PAIR_ROWS = 8
_CACHE_BB = 8


def _cache_attn_kernel(sink_ref, q_ref, kn_ref, vn_ref, ck_ref, cv_ref, g_ref, o_ref, *, n_new):
    n_buf = ck_ref.shape[1]
    rows_blk = _CACHE_BB * n_new
    knew = kn_ref[...].astype(BF16)
    vnew = vn_ref[...].astype(BF16)
    lane = lax.broadcasted_iota(jnp.int32, (_PAIR_ROWS, 128), 1)
    low = lane < HEAD_DIM
    zero = jnp.zeros((), BF16)
    n_stack = N_HEADS * _PAIR_ROWS
    r = lax.broadcasted_iota(jnp.int32, (n_stack, 1), 0) % _PAIR_ROWS
    r_seq, r_tok = r // n_new, r % n_new
    colc = lax.broadcasted_iota(jnp.int32, (n_stack, 2 * n_buf), 1)
    c_seq, c_pos = colc // n_buf, colc % n_buf
    mask_c = (c_seq == r_seq) & (c_pos + WINDOW > r_tok + n_buf)
    coln = lax.broadcasted_iota(jnp.int32, (n_stack, rows_blk), 1)
    sink_col = jnp.concatenate(
        [jnp.full((_PAIR_ROWS, 1), sink_ref[_Q_PERM[h]], F32) for h in range(N_HEADS)], axis=0)
    for sp in range(_CACHE_BB // 2):
        q = q_ref[_PAIR_ROWS * sp:_PAIR_ROWS * (sp + 1), :]
        pieces = []
        for pair in range(N_HEADS // 2):
            qp = q[:, 128 * pair:128 * (pair + 1)]
            pieces.append(jnp.where(low, qp, zero))
            pieces.append(jnp.where(low, zero, qp))
        qs = jnp.concatenate(pieces, axis=0)
        kc = jnp.concatenate([ck_ref[2 * sp], ck_ref[2 * sp + 1]], axis=0).astype(BF16)
        vc = jnp.concatenate([cv_ref[2 * sp], cv_ref[2 * sp + 1]], axis=0).astype(BF16)
        nt = (((1,), (1,)), ((), ()))
        s_c = lax.dot_general(qs, kc, nt, preferred_element_type=F32) * (HEAD_DIM ** -0.5)
        s_n = lax.dot_general(qs, knew, nt, preferred_element_type=F32) * (HEAD_DIM ** -0.5)
        s_c = jnp.where(mask_c, s_c, NEG_INF)
        n_seq, n_tok = coln // n_new - 2 * sp, coln % n_new
        mask_n = (n_seq == r_seq) & (n_tok <= r_tok)
        s_n = jnp.where(mask_n, s_n, NEG_INF)
        o = _softmax_pv([s_c, s_n], [vc, vnew], sink_col)
        outs = [jnp.where(low, o[16 * pair:16 * pair + 8], o[16 * pair + 8:16 * pair + 16])
                for pair in range(N_HEADS // 2)]
        oo = jnp.concatenate(outs, axis=-1)
        o_ref[_PAIR_ROWS * sp:_PAIR_ROWS * (sp + 1), :] = (_rms(oo) * g_ref[...]).astype(BF16)


def _cache_attention(sinks, q, k, v, cache_k, cache_v, g, n_new):
    t = q.shape[0]
    nb, n_buf, _ = cache_k.shape
    assert n_new * 2 == _PAIR_ROWS and nb % _CACHE_BB == 0 and n_buf == WINDOW
    rows = _CACHE_BB * n_new
    row = lambda i: (i, 0)
    cache_spec = pl.BlockSpec((_CACHE_BB, n_buf, D_KV), lambda i: (i, 0, 0))
    return pl.pallas_call(
        functools.partial(_cache_attn_kernel, n_new=n_new),
        grid=(nb // _CACHE_BB,),
        in_specs=[pl.BlockSpec(memory_space=pltpu.SMEM),
                  pl.BlockSpec((rows, D_ATTN), row), pl.BlockSpec((rows, D_KV), row),
                  pl.BlockSpec((rows, D_KV), row), cache_spec, cache_spec,
                  pl.BlockSpec((1, D_ATTN), lambda i: (0, 0))],
        out_specs=pl.BlockSpec((rows, D_ATTN), row),
        out_shape=jax.ShapeDtypeStruct((t, D_ATTN), BF16),
        compiler_params=_cparams("parallel"),
        name="cache_attn",
    )(sinks, q, k, v, cache_k, cache_v, g)


def _s5_prep_kernel(lre_ref, lim_ref, ldt_ref, bre_ref, bim_ref, are_ref, aim_ref, bbre_ref, bbim_ref):
    dt = jnp.exp(ldt_ref[...])
    l_re = jnp.minimum(lre_ref[...], -1e-4)
    l_im = lim_ref[...]
    mag = jnp.exp(l_re * dt)
    a_re = mag * jnp.cos(l_im * dt)
    a_im = mag * jnp.sin(l_im * dt)
    den = l_re * l_re + l_im * l_im
    n_re = a_re - 1.0
    z_re = (n_re * l_re + a_im * l_im) / den
    z_im = (a_im * l_re - n_re * l_im) / den
    are_ref[...] = a_re
    aim_ref[...] = a_im
    br, bi = bre_ref[...], bim_ref[...]
    zr, zi = z_re[:, None, :], z_im[:, None, :]
    bbre_ref[...] = zr * br - zi * bi
    bbim_ref[...] = zr * bi + zi * br


def _s5_prep(lam_re, lam_im, log_dt, b_re_t, b_im_t):
    g, p = lam_re.shape
    sd = jax.ShapeDtypeStruct
    return pl.pallas_call(
        _s5_prep_kernel,
        out_shape=[sd((g, p), F32), sd((g, p), F32), sd(b_re_t.shape, F32), sd(b_re_t.shape, F32)],
        name="s5_prep",
    )(lam_re, lam_im, log_dt, b_re_t, b_im_t)


def _s5_kernel(u_ref, h0r_ref, h0i_ref, ar_ref, ai_ref, wb_ref, wc_ref, d_ref, wglu_ref, bglu_ref,
               g_ref, o_ref, hr_ref, hi_ref, bu_ref, hs_ref, *, tt, stride):
    j = pl.program_id(1)
    rows = SUBLANES * tt
    n_tiles = D_STATE // 128

    @pl.when(j == 0)
    def _():
        hs_ref[:, :D_STATE] = h0r_ref[...]
        hs_ref[:, D_STATE:] = h0i_ref[...]

    u = u_ref[...].reshape(rows, D_SSM)
    ub = u.astype(BF16)

    def put(c, val):
        if stride == tt:
            bu_ref[c, 0:rows, :] = val
        else:
            for b in range(SUBLANES):
                bu_ref[c, b * stride:b * stride + tt, :] = val[b * tt:(b + 1) * tt]

    def get(c):
        if stride == tt:
            return bu_ref[c, 0:rows, :]
        return jnp.concatenate([bu_ref[c, b * stride:b * stride + tt, :] for b in range(SUBLANES)], axis=0)

    for n in range(2 * D_STATE // 256):
        band = (n % (D_STATE // 256)) // 2
        res = _dot(ub[:, 128 * band:128 * (band + 1)],
                   wb_ref[128 * band:128 * (band + 1), 256 * n:256 * (n + 1)])
        put(2 * n, res[:, :128])
        put(2 * n + 1, res[:, 128:])

    tiles_per_pass = S5_COLS // 128
    for c0 in range(0, n_tiles, tiles_per_pass):
        tiles = range(c0, c0 + tiles_per_pass)
        a_r = [jnp.broadcast_to(ar_ref[:, 128 * c:128 * (c + 1)], (SUBLANES, 128)) for c in tiles]
        a_i = [jnp.broadcast_to(ai_ref[:, 128 * c:128 * (c + 1)], (SUBLANES, 128)) for c in tiles]

        def step(t, carry, tiles=tiles, a_r=a_r, a_i=a_i):
            at_t = pl.ds(t, SUBLANES, stride=stride)
            out = []
            for k, c in enumerate(tiles):
                h_r, h_i = carry[2 * k], carry[2 * k + 1]
                n_r = a_r[k] * h_r - a_i[k] * h_i + bu_ref[c, at_t, :]
                n_i = a_r[k] * h_i + a_i[k] * h_r + bu_ref[n_tiles + c, at_t, :]
                bu_ref[c, at_t, :] = n_r
                bu_ref[n_tiles + c, at_t, :] = n_i
                out += [n_r, n_i]
            return tuple(out)

        init = []
        for c in tiles:
            init += [hs_ref[:, 128 * c:128 * (c + 1)], hs_ref[:, D_STATE + 128 * c:D_STATE + 128 * (c + 1)]]
        fin = lax.fori_loop(0, tt, step, tuple(init), unroll=min(tt, 8))
        for k, c in enumerate(tiles):
            hs_ref[:, 128 * c:128 * (c + 1)] = fin[2 * k]
            hs_ref[:, D_STATE + 128 * c:D_STATE + 128 * (c + 1)] = fin[2 * k + 1]

    def h_cols(first_tile):
        return jnp.concatenate([get(first_tile + k) for k in range(4)], axis=-1).astype(BF16)

    ys = []
    for m in range(D_SSM // 128):
        y = _dot(h_cols(4 * m), wc_ref[512 * m:512 * (m + 1), 128 * m:128 * (m + 1)])
        y = y + _dot(h_cols(n_tiles + 4 * m),
                     wc_ref[D_STATE + 512 * m:D_STATE + 512 * (m + 1), 128 * m:128 * (m + 1)])
        ys.append(y)
    y = jnp.concatenate(ys, axis=-1) + d_ref[...] * u
    z = _dot(jax.nn.gelu(y).astype(BF16), wglu_ref[...]) + bglu_ref[...]
    s = z[:, :D_SSM] * jax.nn.sigmoid(z[:, D_SSM:])
    o_ref[...] = (_rms(s) * g_ref[...]).astype(BF16).reshape(o_ref.shape)

    @pl.when(j == pl.num_programs(1) - 1)
    def _():
        hr_ref[...] = hs_ref[:, :D_STATE]
        hi_ref[...] = hs_ref[:, D_STATE:]


def _s5(u, h0r, h0i, a_re, a_im, wb, wc, d, wglu, bglu, g, *, tt, stride, time_chunked):
    nbg = h0r.shape[0] // SUBLANES
    if time_chunked:
        nchunks = u.shape[1] // tt
        u_spec = pl.BlockSpec((SUBLANES, tt, D_SSM), lambda gi, j: (gi, j, 0))
    else:
        nchunks = 1
        u_spec = pl.BlockSpec((None, SUBLANES * tt, D_SSM), lambda gi, j: (gi, 0, 0))
    fix = lambda gi, j: (0, 0)
    st_spec = pl.BlockSpec((SUBLANES, D_STATE), lambda gi, j: (gi, 0))
    sd = jax.ShapeDtypeStruct
    return pl.pallas_call(
        functools.partial(_s5_kernel, tt=tt, stride=stride),
        grid=(nbg, nchunks),
        in_specs=[u_spec, st_spec, st_spec,
                  pl.BlockSpec((1, D_STATE), fix), pl.BlockSpec((1, D_STATE), fix),
                  pl.BlockSpec((D_SSM, 2 * D_STATE), fix), pl.BlockSpec((2 * D_STATE, D_SSM), fix),
                  pl.BlockSpec((1, D_SSM), fix), pl.BlockSpec((D_SSM, 2 * D_SSM), fix),
                  pl.BlockSpec((1, 2 * D_SSM), fix), pl.BlockSpec((1, D_SSM), fix)],
        out_specs=[u_spec, st_spec, st_spec],
        out_shape=[sd(u.shape, BF16), sd(h0r.shape, F32), sd(h0r.shape, F32)],
        scratch_shapes=[pltpu.VMEM((2 * D_STATE // 128, SUBLANES * stride, 128), F32),
                        pltpu.VMEM((SUBLANES, 2 * D_STATE), F32)],
        compiler_params=_cparams("parallel", "arbitrary"),
        name="s5",
    )(u, h0r, h0i, a_re, a_im, wb, wc, d, wglu, bglu, g)


def _to_row_tiles(ref, val):
    rows = val.shape[0]
    for c in range(D_MODEL // 128):
        ref[pl.ds(c, rows, stride=SUBLANES), :] = val[:, 128 * c:128 * (c + 1)]


def _from_row_tiles(ref, rows, lead=()):
    return jnp.concatenate(
        [ref[(*lead, pl.ds(c, rows, stride=SUBLANES), slice(None))] for c in range(D_MODEL // 128)], axis=-1)


def _out_proj_kernel(a_ref, s_ref, x_ref, wa_ref, ws_ref, g_ref, wr_ref, br_ref, x1_ref, xt_ref, lg_ref):
    x1 = x_ref[...] + _dot(a_ref[...], wa_ref[...]) + _dot(s_ref[...], ws_ref[...])
    x1_ref[...] = x1
    xn = _rms(x1) * g_ref[...]
    _to_row_tiles(xt_ref, xn)
    lg_ref[...] = _dot(xn.astype(BF16), wr_ref[...]) + br_ref[...]


def _out_proj(attn_n, ssm_n, x2d, wa, ws, g, wr, br):
    t = x2d.shape[0]
    tm = min(ROW_TILE, t)
    row = lambda i: (i, 0)
    fix = lambda i: (0, 0)
    sd = jax.ShapeDtypeStruct
    return pl.pallas_call(
        _out_proj_kernel,
        grid=(t // tm,),
        in_specs=[pl.BlockSpec((tm, D_ATTN), row), pl.BlockSpec((tm, D_SSM), row),
                  pl.BlockSpec((tm, D_MODEL), row),
                  pl.BlockSpec((D_ATTN, D_MODEL), fix), pl.BlockSpec((D_SSM, D_MODEL), fix),
                  pl.BlockSpec((1, D_MODEL), fix), pl.BlockSpec((D_MODEL, 128), fix),
                  pl.BlockSpec((1, 128), fix)],
        out_specs=[pl.BlockSpec((tm, D_MODEL), row), pl.BlockSpec((tm * SUBLANES, 128), row),
                   pl.BlockSpec((tm, 128), row)],
        out_shape=[sd((t, D_MODEL), F32), sd((t * SUBLANES, 128), F32), sd((t, 128), F32)],
        compiler_params=_cparams("parallel"),
        name="out_proj",
    )(attn_n, ssm_n, x2d, wa, ws, g, wr, br)


def _tile_rows(ref, r):
    return ref.at[pl.ds(pl.multiple_of(r * SUBLANES, SUBLANES), SUBLANES), :]


def _dispatch_kernel(pos_ref, pend_ref, padded_ref, nu_ref, xa_ref, xb_ref, rows_ref, zbuf, sem, zsem, *,
                     tiles_a, nblk):
    i = pl.program_id(0)
    block_rows = MOE_TILE * SUBLANES

    def zero_block(start_row):
        return pltpu.make_async_copy(
            zbuf, rows_ref.at[pl.ds(pl.multiple_of(start_row * SUBLANES, SUBLANES), block_rows), :], zsem)

    @pl.when(i == 0)
    def _():
        zbuf[...] = jnp.zeros_like(zbuf)
        for e in range(N_EXPERTS):
            @pl.when(padded_ref[e] > 0)
            def _(e=e):
                zero_block(pend_ref[e] - MOE_TILE).start()

        def tail_start(j, c):
            zero_block(j * MOE_TILE).start()
            return c

        def tail_wait(j, c):
            zero_block(j * MOE_TILE).wait()
            return c

        lax.fori_loop(nu_ref[0], nblk, tail_start, 0)
        for e in range(N_EXPERTS):
            @pl.when(padded_ref[e] > 0)
            def _(e=e):
                zero_block(pend_ref[e] - MOE_TILE).wait()
        lax.fori_loop(nu_ref[0], nblk, tail_wait, 0)

    def scatter(x_ref):
        def body(t, c):
            for k in range(TOP_K):
                r = pos_ref[(i * COMBINE_TILE + t) * TOP_K + k]
                pltpu.make_async_copy(_tile_rows(x_ref, t), _tile_rows(rows_ref, r), sem).start(priority=k % 2)
            return c

        lax.fori_loop(0, COMBINE_TILE, body, 0, unroll=4)
        for _ in range(TOP_K):
            pltpu.make_async_copy(x_ref, x_ref, sem).wait()

    @pl.when(i < tiles_a)
    def _():
        scatter(xa_ref)

    @pl.when(i >= tiles_a)
    def _():
        scatter(xb_ref)


def _dispatch(pos_flat, pend, padded, n_used, xa, xb, n_rows):
    tiles_a = xa.shape[0] // (COMBINE_TILE * SUBLANES)
    tiles_b = xb.shape[0] // (COMBINE_TILE * SUBLANES)
    nblk = n_rows // MOE_TILE
    tile = (COMBINE_TILE * SUBLANES, 128)
    grid_spec = pltpu.PrefetchScalarGridSpec(
        num_scalar_prefetch=4, grid=(tiles_a + tiles_b,),
        in_specs=[pl.BlockSpec(tile, lambda i, *_: (jnp.minimum(i, tiles_a - 1), 0)),
                  pl.BlockSpec(tile, lambda i, *_: (jnp.maximum(i - tiles_a, 0), 0))],
        out_specs=pl.BlockSpec(memory_space=pl.ANY),
        scratch_shapes=[pltpu.VMEM((MOE_TILE * SUBLANES, 128), F32), pltpu.SemaphoreType.DMA(()),
                        pltpu.SemaphoreType.DMA(())])
    return pl.pallas_call(
        functools.partial(_dispatch_kernel, tiles_a=tiles_a, nblk=nblk), grid_spec=grid_spec,
        out_shape=jax.ShapeDtypeStruct((n_rows * SUBLANES, 128), F32),
        compiler_params=_cparams("arbitrary"), name="dispatch",
    )(pos_flat, pend, padded, n_used, xa, xb)


def _moe_kernel(be_ref, nu_ref, x_ref, wgu_ref, bg_ref, bl_ref, wd_ref, bd_ref, perm_ref, o_ref,
                wg_s, wl_s, wd_s):
    i = pl.program_id(0)
    used = i < nu_ref[0]
    new_expert = (i == 0) | (be_ref[i] != be_ref[jnp.maximum(i - 1, 0)])

    @pl.when(used & new_expert)
    def _():
        for c in range(2 * D_FF // 256):
            r = _dot(wgu_ref[:, 256 * c:256 * (c + 1)].astype(BF16), perm_ref[...])
            wg_s[:, 128 * c:128 * (c + 1)] = r[:, :128].astype(BF16)
            wl_s[:, 128 * c:128 * (c + 1)] = r[:, 128:].astype(BF16)
        wd_s[...] = wd_ref[...].astype(BF16)

    @pl.when(used)
    def _():
        x = _from_row_tiles(x_ref, MOE_TILE).astype(BF16)
        glu = jnp.minimum(_dot(x, wg_s[...]) + bg_ref[...], SWIGLU_LIMIT)
        lin = jnp.clip(_dot(x, wl_s[...]) + bl_ref[...], -SWIGLU_LIMIT, SWIGLU_LIMIT)
        act = glu * jax.nn.sigmoid(SWIGLU_ALPHA * glu) * (lin + 1.0)
        _to_row_tiles(o_ref, _dot(act.astype(BF16), wd_s[...]) + bd_ref[...])

    @pl.when(jnp.logical_not(used))
    def _():
        o_ref[...] = jnp.zeros_like(o_ref)


def _moe(block_e, n_used, x_rows, wgu, bg, bl, wd, bd, perm):
    nblk = x_rows.shape[0] // (MOE_TILE * SUBLANES)
    row = lambda i, be, nu: (jnp.minimum(i, nu[0] - 1), 0)
    wsel = lambda i, be, nu: (be[i], 0, 0)
    grid_spec = pltpu.PrefetchScalarGridSpec(
        num_scalar_prefetch=2,
        grid=(nblk,),
        in_specs=[pl.BlockSpec((MOE_TILE * SUBLANES, 128), row),
                  pl.BlockSpec((None, D_MODEL, 2 * D_FF), wsel),
                  pl.BlockSpec((None, 1, D_FF), wsel), pl.BlockSpec((None, 1, D_FF), wsel),
                  pl.BlockSpec((None, D_FF, D_MODEL), wsel), pl.BlockSpec((None, 1, D_MODEL), wsel),
                  pl.BlockSpec((256, 256), lambda i, be, nu: (0, 0))],
        out_specs=pl.BlockSpec((MOE_TILE * SUBLANES, 128), lambda i, be, nu: (i, 0)),
        scratch_shapes=[pltpu.VMEM((D_MODEL, D_FF), BF16), pltpu.VMEM((D_MODEL, D_FF), BF16),
                        pltpu.VMEM((D_FF, D_MODEL), BF16)],
    )
    return pl.pallas_call(
        _moe_kernel,
        grid_spec=grid_spec,
        out_shape=jax.ShapeDtypeStruct(x_rows.shape, F32),
        compiler_params=_cparams("arbitrary"),
        name="moe",
    )(block_e, n_used, x_rows, wgu, bg, bl, wd, bd, perm)


def _combine_kernel(pos_ref, rows_hbm, x1_ref, gate_ref, o_ref, buf, sem):
    i = pl.program_id(0)
    n = pl.num_programs(0)

    def row_copy(tile, slot, t, k):
        r = pos_ref[(tile * COMBINE_TILE + t) * TOP_K + k]
        return pltpu.make_async_copy(_tile_rows(rows_hbm, r), _tile_rows(buf.at[slot, k], t), sem.at[slot])

    def fetch(tile, slot):
        def body(t, c):
            for k in range(TOP_K):
                row_copy(tile, slot, t, k).start(priority=k % 2)
            return c
        lax.fori_loop(0, COMBINE_TILE, body, 0, unroll=4)

    @pl.when(i == 0)
    def _():
        fetch(0, 0)

    @pl.when(i + 1 < n)
    def _():
        fetch(i + 1, (i + 1) % 2)

    slot = i % 2
    pltpu.make_async_copy(buf.at[slot], buf.at[slot], sem.at[slot]).wait()
    g = gate_ref[...]
    y = x1_ref[...]
    for k in range(TOP_K):
        y = y + g[:, k:k + 1] * _from_row_tiles(buf, COMBINE_TILE, lead=(slot, k))
    o_ref[...] = y


def _combine(pos_flat, rows, x1, gates):
    t = x1.shape[0]
    grid_spec = pltpu.PrefetchScalarGridSpec(
        num_scalar_prefetch=1,
        grid=(t // COMBINE_TILE,),
        in_specs=[pl.BlockSpec(memory_space=pl.ANY),
                  pl.BlockSpec((COMBINE_TILE, D_MODEL), lambda i, p: (i, 0)),
                  pl.BlockSpec((COMBINE_TILE, TOP_K), lambda i, p: (i, 0))],
        out_specs=pl.BlockSpec((COMBINE_TILE, D_MODEL), lambda i, p: (i, 0)),
        scratch_shapes=[pltpu.VMEM((2, TOP_K, COMBINE_TILE * SUBLANES, 128), F32),
                        pltpu.SemaphoreType.DMA((2,))],
    )
    return pl.pallas_call(
        _combine_kernel,
        grid_spec=grid_spec,
        out_shape=jax.ShapeDtypeStruct((t, D_MODEL), F32),
        compiler_params=_cparams("arbitrary"),
        name="combine",
    )(pos_flat, rows, x1, gates)


def _lane_pick(cols):
    lane = lax.broadcasted_iota(jnp.int32, (cols[0].shape[0], len(cols)), 1)
    out = cols[-1]
    for k in range(len(cols) - 2, -1, -1):
        out = jnp.where(lane == k, cols[k], out)
    return out


def _topk_kernel(lg_ref, tri_ref, idx_ref, gate_ref, rank_ref, cnt_ref, carry):
    @pl.when(pl.program_id(0) == 0)
    def _():
        carry[...] = jnp.zeros_like(carry)

    l = lg_ref[...]
    lane = lax.broadcasted_iota(jnp.int32, l.shape, 1)
    vals, idxs, sels = [], [], []
    for _ in range(TOP_K):
        m = jnp.max(l, axis=-1, keepdims=True)
        idx = jnp.min(jnp.where(l == m, lane, l.shape[1]), axis=-1, keepdims=True)
        sel = lane == idx
        l = jnp.where(sel, -jnp.inf, l)
        vals.append(m)
        idxs.append(idx)
        sels.append(sel)
    exps = [jnp.exp(v - vals[0]) for v in vals]
    den = exps[0] + exps[1] + exps[2] + exps[3]
    onehot = jnp.where(sels[0] | sels[1] | sels[2] | sels[3], 1.0, 0.0)
    before = _dot(tri_ref[...], onehot.astype(BF16)) + carry[...]
    ranks = [jnp.sum(jnp.where(s, before, 0.0), axis=-1, keepdims=True) for s in sels]
    carry[...] = carry[...] + jnp.sum(onehot, axis=0, keepdims=True)
    idx_ref[...] = _lane_pick(idxs)
    gate_ref[...] = _lane_pick([e / den for e in exps])
    rank_ref[...] = _lane_pick(ranks).astype(jnp.int32)
    cnt_ref[...] = carry[...]


def _dest_kernel(idx_ref, rank_ref, start_ref, pos_ref):
    lane = lax.broadcasted_iota(jnp.int32, (idx_ref.shape[0], 128), 1)
    start = start_ref[...]
    idx = idx_ref[...]
    cols = [jnp.sum(jnp.where(lane == idx[:, k:k + 1], start, 0.0), axis=-1, keepdims=True)
            for k in range(TOP_K)]
    pos_ref[...] = _lane_pick(cols).astype(jnp.int32) + rank_ref[...]


def _route(logits):
    t = logits.shape[0]
    n_assign = t * TOP_K
    tr = ROW_TILE
    row = lambda i: (i, 0)
    fix = lambda i: (0, 0)
    sd = jax.ShapeDtypeStruct
    k_spec = pl.BlockSpec((tr, TOP_K), row)
    tri = (jnp.arange(tr)[:, None] > jnp.arange(tr)[None, :]).astype(BF16)
    top_i, gates, rank, counts = pl.pallas_call(
        _topk_kernel,
        grid=(t // tr,),
        in_specs=[pl.BlockSpec((tr, 128), row), pl.BlockSpec((tr, tr), fix)],
        out_specs=[k_spec, k_spec, k_spec, pl.BlockSpec((1, 128), fix)],
        out_shape=[sd((t, TOP_K), jnp.int32), sd((t, TOP_K), F32), sd((t, TOP_K), jnp.int32),
                   sd((1, 128), F32)],
        scratch_shapes=[pltpu.VMEM((1, 128), F32)],
        compiler_params=_cparams("arbitrary"),
        name="topk",
    )(logits, tri)
    counts = counts[0, :N_EXPERTS].astype(jnp.int32)
    padded = ((counts + MOE_TILE - 1) // MOE_TILE) * MOE_TILE
    pend = jnp.cumsum(padded)
    pstart = jnp.pad(pend - padded, (0, 128 - N_EXPERTS)).astype(F32)[None]
    dest = pl.pallas_call(
        _dest_kernel,
        grid=(t // tr,),
        in_specs=[k_spec, k_spec, pl.BlockSpec((1, 128), fix)],
        out_specs=k_spec,
        out_shape=sd((t, TOP_K), jnp.int32),
        compiler_params=_cparams("parallel"),
        name="dest",
    )(top_i, rank, pstart).reshape(-1)
    nblk = (n_assign + MOE_TILE - 1) // MOE_TILE + N_EXPERTS
    n_rows = nblk * MOE_TILE
    n_used = (pend[-1] // MOE_TILE).astype(jnp.int32)
    block_start = jnp.arange(nblk, dtype=jnp.int32) * MOE_TILE
    block_e = jnp.sum(pend[None, :] <= jnp.minimum(block_start, pend[-1] - 1)[:, None], axis=1)
    block_e = jnp.minimum(block_e, N_EXPERTS - 1).astype(jnp.int32)
    return gates, dest, block_e, n_used.reshape(1), pend, padded, n_rows


def kernel(x_prompt, x_sample, cache_k, cache_v, state_ssm_re, state_ssm_im, norm_mix_g, w_in, q_norm_g, k_norm_g, attn_sinks, ssm_lambda_re, ssm_lambda_im, ssm_b_re, ssm_b_im, ssm_c_re, ssm_c_im, ssm_d, ssm_log_dt, w_glu, b_glu, attn_out_norm_g, ssm_out_norm_g, w_out, norm_ffn_g, w_router, b_router, w_gate_up, b_gate_up, w_down, b_down):
    depth = w_in.shape[0]
    assert depth == 1
    bp, sp, _ = x_prompt.shape
    bs, ss, _ = x_sample.shape
    tp, ts = bp * sp, bs * ss
    assert bp == SUBLANES and sp % S5_CHUNK == 0 and bs % SUBLANES == 0

    perm = jnp.asarray(_Q_PERM)
    w_in0 = w_in[0]
    wq = w_in0[:, :D_ATTN].reshape(D_MODEL, N_HEADS, HEAD_DIM)[:, perm].reshape(D_MODEL, D_ATTN)
    w_in_b = jnp.concatenate([wq, w_in0[:, D_ATTN:]], axis=1).astype(BF16)
    qkg = jnp.concatenate([jnp.tile(q_norm_g[0], N_HEADS), jnp.tile(k_norm_g[0], N_KV_HEADS)])[None]
    pmat = jnp.kron(jnp.eye(256 // HEAD_DIM, dtype=F32),
                    jnp.full((HEAD_DIM, HEAD_DIM), 1.0 / HEAD_DIM, F32)).astype(BF16)
    g_mix = norm_mix_g[0][None]
    sinks = attn_sinks[0]
    g_attn = attn_out_norm_g[0].reshape(N_HEADS, HEAD_DIM)[perm].reshape(1, D_ATTN)
    w_out0 = w_out[0]
    w_out_a = w_out0[:D_ATTN].reshape(N_HEADS, HEAD_DIM, D_MODEL)[perm].reshape(D_ATTN, D_MODEL).astype(BF16)
    w_out_s = w_out0[D_ATTN:].astype(BF16)
    g_ssm = ssm_out_norm_g[0][None]
    g_ffn = norm_ffn_g[0][None]
    w_r = jnp.pad(w_router[0], ((0, 0), (0, 128 - N_EXPERTS))).astype(BF16)
    b_r = jnp.pad(b_router[0], (0, 128 - N_EXPERTS), constant_values=NEG_INF)[None]

    a_re, a_im, bb_re, bb_im = _s5_prep(
        ssm_lambda_re[0], ssm_lambda_im[0], ssm_log_dt[0][:, None],
        jnp.swapaxes(ssm_b_re[0], 1, 2), jnp.swapaxes(ssm_b_im[0], 1, 2))
    eye_g = jnp.eye(N_SSM_GROUPS, dtype=F32)
    bd_b = lambda bb: jnp.einsum("ghp,gk->ghkp", bb, eye_g).reshape(D_SSM, D_STATE)
    wb = jnp.concatenate([bd_b(bb_re), bd_b(bb_im)], axis=1).astype(BF16)
    bd_c = lambda c: jnp.einsum("ghp,gk->gpkh", c, eye_g).reshape(D_STATE, D_SSM)
    wc = jnp.concatenate([bd_c(ssm_c_re[0]), -bd_c(ssm_c_im[0])], axis=0).astype(BF16)
    a_re, a_im = a_re.reshape(1, D_STATE), a_im.reshape(1, D_STATE)
    d_skip = ssm_d[0].reshape(1, D_SSM)
    w_glu_b = w_glu[0].astype(BF16)
    b_glu0 = b_glu[0][None]

    b_g = b_gate_up[0][:, None, 0::2]
    b_l = b_gate_up[0][:, None, 1::2]
    b_d = b_down[0][:, None, :]
    idx = jnp.arange(256)
    deint = (idx[None, :] == jnp.where(idx % 2 == 0, idx // 2, 128 + idx // 2)[:, None]).astype(BF16)

    xp2 = x_prompt.reshape(tp, D_MODEL)
    xs2 = x_sample.reshape(ts, D_MODEL)
    qp, kp, vp, up = _in_proj(xp2, g_mix, w_in_b, qkg, pmat)
    qs, ks, vs, us = _in_proj(xs2, g_mix, w_in_b, qkg, pmat)

    kp3, vp3 = kp.reshape(bp, sp, D_KV), vp.reshape(bp, sp, D_KV)
    attn_p = _band_attention(sinks, qp.reshape(bp, sp, D_ATTN), kp3, vp3, g_attn).reshape(tp, D_ATTN)
    ck = cache_k[0].reshape(bs, -1, D_KV)
    cv = cache_v[0].reshape(bs, -1, D_KV)
    attn_s = _cache_attention(sinks, qs, ks, vs, ck, cv, g_attn, ss)

    zeros_p = jnp.zeros((bp, D_STATE), F32)
    s5_args = (a_re, a_im, wb, wc, d_skip, w_glu_b, b_glu0, g_ssm)
    ssm_p, hr_p, hi_p = _s5(up.reshape(bp, sp, D_SSM), zeros_p, zeros_p, *s5_args,
                            tt=S5_CHUNK, stride=S5_CHUNK + S5_PAD, time_chunked=True)
    ssm_s, hr_s, hi_s = _s5(us.reshape(bs // SUBLANES, SUBLANES * ss, D_SSM),
                            state_ssm_re[0].reshape(bs, D_STATE), state_ssm_im[0].reshape(bs, D_STATE),
                            *s5_args, tt=ss, stride=ss, time_chunked=False)

    x1p, xtp, lgp = _out_proj(attn_p, ssm_p.reshape(tp, D_SSM), xp2, w_out_a, w_out_s, g_ffn, w_r, b_r)
    x1s, xts, lgs = _out_proj(attn_s, ssm_s.reshape(ts, D_SSM), xs2, w_out_a, w_out_s, g_ffn, w_r, b_r)

    logits = jnp.concatenate([lgp, lgs], axis=0)
    gates, pos, block_e, n_used, pend, padded, n_rows = _route(logits)
    pos_p, pos_s = pos[:tp * TOP_K], pos[tp * TOP_K:]
    x_rows = _dispatch(pos, pend, padded, n_used, xtp, xts, n_rows)
    out_rows = _moe(block_e, n_used, x_rows, w_gate_up[0], b_g, b_l, w_down[0], b_d, deint)
    yp = _combine(pos_p, out_rows, x1p, gates[:tp]).reshape(bp, sp, D_MODEL)
    ys = _combine(pos_s, out_rows, x1s, gates[tp:]).reshape(bs, ss, D_MODEL)

    kv5 = lambda a, b_: a.reshape(b_, -1, N_KV_HEADS, HEAD_DIM)
    new_kp = kv5(kp3[:, -WINDOW:], bp)[None]
    new_vp = kv5(vp3[:, -WINDOW:], bp)[None]
    ks3, vs3 = ks.reshape(bs, ss, D_KV), vs.reshape(bs, ss, D_KV)
    new_ks = kv5(jnp.concatenate([ck, ks3], axis=1)[:, ss:], bs)[None]
    new_vs = kv5(jnp.concatenate([cv, vs3], axis=1)[:, ss:], bs)[None]
    st = lambda h, b_: h.reshape(1, b_, N_SSM_GROUPS, SSM_STATE)
    return (yp, ys, new_kp, new_vp, st(hr_p, bp), st(hi_p, bp),
            new_ks, new_vs, st(hr_s, bs), st(hi_s, bs))
```

```python
import functools
import math

import jax
import jax.numpy as jnp
from jax import lax
from jax.experimental import pallas as pl
from jax.experimental.pallas import tpu as pltpu

F32 = jnp.float32
BF16 = jnp.bfloat16

D_MODEL = 1024
D_ATTN = 512
D_SSM = 512
HEAD_DIM = 64
N_HEADS = 8
N_KV_HEADS = 2
D_KV = N_KV_HEADS * HEAD_DIM
WINDOW = 128
SSM_GROUP = 16
N_SSM_GROUPS = 32
SSM_STATE = 64
D_STATE = N_SSM_GROUPS * SSM_STATE
N_EXPERTS = 32
TOP_K = 4
D_FF = 1024
SWIGLU_LIMIT = 7.0
SWIGLU_ALPHA = 1.702
RMS_EPS = 1e-6
NEG_INF = -1e30
D_IN_PROJ = D_ATTN + 2 * D_KV + D_SSM
D_QK = D_ATTN + D_KV

SUBLANES = 8
VMEM_LIMIT = 56 * 1024 * 1024

ROW_TILE = 512
MOE_TILE = 512
ROUTE_TILE = 512
_RUN_SIZES = tuple(1 << b for b in range(ROUTE_TILE.bit_length() - 1, -1, -1))
S5_CHUNK = 128
S5_PAD = 8
S5_COLS = 512

_Q_PERM = (0, 4, 1, 5, 2, 6, 3, 7)


def _cparams(*sem):
    return pltpu.CompilerParams(dimension_semantics=sem, vmem_limit_bytes=VMEM_LIMIT)


def _dot(a, b):
    return jnp.dot(a, b, preferred_element_type=F32)


def _rms(x):
    return x * lax.rsqrt(jnp.mean(x * x, axis=-1, keepdims=True) + RMS_EPS)


def _in_proj_kernel(x_ref, g_ref, w_ref, qkg_ref, p_ref, q_ref, k_ref, v_ref, u_ref):
    xn = _rms(x_ref[...]) * g_ref[...]
    h = _dot(xn.astype(BF16), w_ref[...])
    qk = h[:, :D_QK]
    sq = (qk * qk).astype(BF16)
    p = p_ref[...]
    ms = jnp.concatenate(
        [_dot(sq[:, 0:256], p), _dot(sq[:, 256:512], p), _dot(sq[:, 512:640], p[:128, :128])],
        axis=-1)
    qkn = qk * lax.rsqrt(ms + RMS_EPS) * qkg_ref[...]
    q_ref[...] = qkn[:, :D_ATTN].astype(BF16)
    k_ref[...] = qkn[:, D_ATTN:]
    v_ref[...] = h[:, D_QK:D_QK + D_KV]
    u_ref[...] = h[:, D_QK + D_KV:]


def _in_proj(x2d, g, w, qkg, pmat):
    t = x2d.shape[0]
    tm = min(ROW_TILE, t)
    row = lambda i: (i, 0)
    fix = lambda i: (0, 0)
    return pl.pallas_call(
        _in_proj_kernel,
        grid=(t // tm,),
        in_specs=[pl.BlockSpec((tm, D_MODEL), row), pl.BlockSpec((1, D_MODEL), fix),
                  pl.BlockSpec((D_MODEL, D_IN_PROJ), fix), pl.BlockSpec((1, D_QK), fix),
                  pl.BlockSpec((256, 256), fix)],
        out_specs=[pl.BlockSpec((tm, D_ATTN), row), pl.BlockSpec((tm, D_KV), row),
                   pl.BlockSpec((tm, D_KV), row), pl.BlockSpec((tm, D_SSM), row)],
        out_shape=[jax.ShapeDtypeStruct((t, D_ATTN), BF16), jax.ShapeDtypeStruct((t, D_KV), F32),
                   jax.ShapeDtypeStruct((t, D_KV), F32), jax.ShapeDtypeStruct((t, D_SSM), F32)],
        compiler_params=_cparams("parallel"),
        name="in_proj",
    )(x2d, g, w, qkg, pmat)


def _softmax_pv(s_blocks, v_blocks, sink):
    m = sink
    for s in s_blocks:
        m = jnp.maximum(m, jnp.max(s, axis=-1, keepdims=True))
    den = jnp.exp(sink - m)
    acc = None
    for s, v in zip(s_blocks, v_blocks):
        p = jnp.exp(s - m)
        den = den + jnp.sum(p, axis=-1, keepdims=True)
        pv = _dot(p.astype(BF16), v)
        acc = pv if acc is None else acc + pv
    return acc / den


def _band_attn_kernel(sink_ref, q_ref, kp_ref, kc_ref, vp_ref, vc_ref, g_ref, o_ref):
    i = pl.program_id(1)
    q = q_ref[...]
    kb = jnp.concatenate([kp_ref[...], kc_ref[...]], axis=0).astype(BF16)
    vb = jnp.concatenate([vp_ref[...], vc_ref[...]], axis=0).astype(BF16)
    row = lax.broadcasted_iota(jnp.int32, (WINDOW, 2 * WINDOW), 0)
    col = lax.broadcasted_iota(jnp.int32, (WINDOW, 2 * WINDOW), 1)
    mask = (col > row) & (col <= row + WINDOW) & ((col >= WINDOW) | (i > 0))
    lane = lax.broadcasted_iota(jnp.int32, (WINDOW, 128), 1)
    low = lane < HEAD_DIM
    zero = jnp.zeros((), BF16)
    outs = []
    for pair in range(N_HEADS // 2):
        qp = q[:, 128 * pair:128 * (pair + 1)]
        halves = []
        for par in range(2):
            qm = jnp.where(low if par == 0 else ~low, qp, zero)
            s = lax.dot_general(qm, kb, (((1,), (1,)), ((), ())), preferred_element_type=F32)
            s = jnp.where(mask, s * (HEAD_DIM ** -0.5), NEG_INF)
            halves.append(_softmax_pv([s], [vb], sink_ref[_Q_PERM[2 * pair + par]]))
        outs.append(jnp.where(low, halves[0], halves[1]))
    o = jnp.concatenate(outs, axis=-1)
    o_ref[...] = (_rms(o) * g_ref[...]).astype(BF16)


def _band_attention(sinks, q, k, v, g):
    b, s, _ = q.shape
    nb = s // WINDOW
    cur = lambda bi, i: (bi, i, 0)
    prev = lambda bi, i: (bi, jnp.maximum(i - 1, 0), 0)
    kv_spec = lambda im: pl.BlockSpec((None, WINDOW, D_KV), im)
    return pl.pallas_call(
        _band_attn_kernel,
        grid=(b, nb),
        in_specs=[pl.BlockSpec(memory_space=pltpu.SMEM),
                  pl.BlockSpec((None, WINDOW, D_ATTN), cur),
                  kv_spec(prev), kv_spec(cur), kv_spec(prev), kv_spec(cur),
                  pl.BlockSpec((1, D_ATTN), lambda bi, i: (0, 0))],
        out_specs=pl.BlockSpec((None, WINDOW, D_ATTN), cur),
        out_shape=jax.ShapeDtypeStruct((b, s, D_ATTN), BF16),
        compiler_params=_cparams("parallel", "parallel"),
        name="band_attn",
    )(sinks, q, k, k, v, v, g)


_PAIR_ROWS = 8
_CACHE_BB = 8


def _cache_attn_kernel(sink_ref, q_ref, kn_ref, vn_ref, ck_ref, cv_ref, g_ref, o_ref, *, n_new):
    n_buf = ck_ref.shape[1]
    rows_blk = _CACHE_BB * n_new
    knew = kn_ref[...].astype(BF16)
    vnew = vn_ref[...].astype(BF16)
    lane = lax.broadcasted_iota(jnp.int32, (_PAIR_ROWS, 128), 1)
    low = lane < HEAD_DIM
    zero = jnp.zeros((), BF16)
    n_stack = N_HEADS * _PAIR_ROWS
    r = lax.broadcasted_iota(jnp.int32, (n_stack, 1), 0) % _PAIR_ROWS
    r_seq, r_tok = r // n_new, r % n_new
    colc = lax.broadcasted_iota(jnp.int32, (n_stack, 2 * n_buf), 1)
    c_seq, c_pos = colc // n_buf, colc % n_buf
    mask_c = (c_seq == r_seq) & (c_pos + WINDOW > r_tok + n_buf)
    coln = lax.broadcasted_iota(jnp.int32, (n_stack, rows_blk), 1)
    sink_col = jnp.concatenate(
        [jnp.full((_PAIR_ROWS, 1), sink_ref[_Q_PERM[h]], F32) for h in range(N_HEADS)], axis=0)
    for sp in range(_CACHE_BB // 2):
        q = q_ref[_PAIR_ROWS * sp:_PAIR_ROWS * (sp + 1), :]
        pieces = []
        for pair in range(N_HEADS // 2):
            qp = q[:, 128 * pair:128 * (pair + 1)]
            pieces.append(jnp.where(low, qp, zero))
            pieces.append(jnp.where(low, zero, qp))
        qs = jnp.concatenate(pieces, axis=0)
        kc = jnp.concatenate([ck_ref[2 * sp], ck_ref[2 * sp + 1]], axis=0).astype(BF16)
        vc = jnp.concatenate([cv_ref[2 * sp], cv_ref[2 * sp + 1]], axis=0).astype(BF16)
        nt = (((1,), (1,)), ((), ()))
        s_c = lax.dot_general(qs, kc, nt, preferred_element_type=F32) * (HEAD_DIM ** -0.5)
        s_n = lax.dot_general(qs, knew, nt, preferred_element_type=F32) * (HEAD_DIM ** -0.5)
        s_c = jnp.where(mask_c, s_c, NEG_INF)
        n_seq, n_tok = coln // n_new - 2 * sp, coln % n_new
        mask_n = (n_seq == r_seq) & (n_tok <= r_tok)
        s_n = jnp.where(mask_n, s_n, NEG_INF)
        o = _softmax_pv([s_c, s_n], [vc, vnew], sink_col)
        outs = [jnp.where(low, o[16 * pair:16 * pair + 8], o[16 * pair + 8:16 * pair + 16])
                for pair in range(N_HEADS // 2)]
        oo = jnp.concatenate(outs, axis=-1)
        o_ref[_PAIR_ROWS * sp:_PAIR_ROWS * (sp + 1), :] = (_rms(oo) * g_ref[...]).astype(BF16)


def _cache_attention(sinks, q, k, v, cache_k, cache_v, g, n_new):
    t = q.shape[0]
    nb, n_buf, _ = cache_k.shape
    assert n_new * 2 == _PAIR_ROWS and nb % _CACHE_BB == 0 and n_buf == WINDOW
    rows = _CACHE_BB * n_new
    row = lambda i: (i, 0)
    cache_spec = pl.BlockSpec((_CACHE_BB, n_buf, D_KV), lambda i: (i, 0, 0))
    return pl.pallas_call(
        functools.partial(_cache_attn_kernel, n_new=n_new),
        grid=(nb // _CACHE_BB,),
        in_specs=[pl.BlockSpec(memory_space=pltpu.SMEM),
                  pl.BlockSpec((rows, D_ATTN), row), pl.BlockSpec((rows, D_KV), row),
                  pl.BlockSpec((rows, D_KV), row), cache_spec, cache_spec,
                  pl.BlockSpec((1, D_ATTN), lambda i: (0, 0))],
        out_specs=pl.BlockSpec((rows, D_ATTN), row),
        out_shape=jax.ShapeDtypeStruct((t, D_ATTN), BF16),
        compiler_params=_cparams("parallel"),
        name="cache_attn",
    )(sinks, q, k, v, cache_k, cache_v, g)


def _s5_prep_kernel(lre_ref, lim_ref, ldt_ref, bre_ref, bim_ref, are_ref, aim_ref, bbre_ref, bbim_ref):
    dt = jnp.exp(ldt_ref[...])
    l_re = jnp.minimum(lre_ref[...], -1e-4)
    l_im = lim_ref[...]
    mag = jnp.exp(l_re * dt)
    a_re = mag * jnp.cos(l_im * dt)
    a_im = mag * jnp.sin(l_im * dt)
    den = l_re * l_re + l_im * l_im
    n_re = a_re - 1.0
    z_re = (n_re * l_re + a_im * l_im) / den
    z_im = (a_im * l_re - n_re * l_im) / den
    are_ref[...] = a_re
    aim_ref[...] = a_im
    br, bi = bre_ref[...], bim_ref[...]
    zr, zi = z_re[:, None, :], z_im[:, None, :]
    bbre_ref[...] = zr * br - zi * bi
    bbim_ref[...] = zr * bi + zi * br


def _s5_prep(lam_re, lam_im, log_dt, b_re_t, b_im_t):
    g, p = lam_re.shape
    sd = jax.ShapeDtypeStruct
    return pl.pallas_call(
        _s5_prep_kernel,
        out_shape=[sd((g, p), F32), sd((g, p), F32), sd(b_re_t.shape, F32), sd(b_re_t.shape, F32)],
        name="s5_prep",
    )(lam_re, lam_im, log_dt, b_re_t, b_im_t)


def _s5_kernel(u_ref, h0r_ref, h0i_ref, ar_ref, ai_ref, wb_ref, wc_ref, d_ref, wglu_ref, bglu_ref,
               g_ref, o_ref, hr_ref, hi_ref, bu_ref, hs_ref, *, tt, stride):
    j = pl.program_id(1)
    rows = SUBLANES * tt
    n_tiles = D_STATE // 128

    @pl.when(j == 0)
    def _():
        hs_ref[:, :D_STATE] = h0r_ref[...]
        hs_ref[:, D_STATE:] = h0i_ref[...]

    u = u_ref[...].reshape(rows, D_SSM)
    ub = u.astype(BF16)

    def put(c, val):
        if stride == tt:
            bu_ref[c, 0:rows, :] = val
        else:
            for b in range(SUBLANES):
                bu_ref[c, b * stride:b * stride + tt, :] = val[b * tt:(b + 1) * tt]

    def get(c):
        if stride == tt:
            return bu_ref[c, 0:rows, :]
        return jnp.concatenate([bu_ref[c, b * stride:b * stride + tt, :] for b in range(SUBLANES)], axis=0)

    for n in range(2 * D_STATE // 256):
        band = (n % (D_STATE // 256)) // 2
        res = _dot(ub[:, 128 * band:128 * (band + 1)],
                   wb_ref[128 * band:128 * (band + 1), 256 * n:256 * (n + 1)])
        put(2 * n, res[:, :128])
        put(2 * n + 1, res[:, 128:])

    tiles_per_pass = S5_COLS // 128
    for c0 in range(0, n_tiles, tiles_per_pass):
        tiles = range(c0, c0 + tiles_per_pass)
        a_r = [jnp.broadcast_to(ar_ref[:, 128 * c:128 * (c + 1)], (SUBLANES, 128)) for c in tiles]
        a_i = [jnp.broadcast_to(ai_ref[:, 128 * c:128 * (c + 1)], (SUBLANES, 128)) for c in tiles]

        def step(t, carry, tiles=tiles, a_r=a_r, a_i=a_i):
            at_t = pl.ds(t, SUBLANES, stride=stride)
            out = []
            for k, c in enumerate(tiles):
                h_r, h_i = carry[2 * k], carry[2 * k + 1]
                n_r = a_r[k] * h_r - a_i[k] * h_i + bu_ref[c, at_t, :]
                n_i = a_r[k] * h_i + a_i[k] * h_r + bu_ref[n_tiles + c, at_t, :]
                bu_ref[c, at_t, :] = n_r
                bu_ref[n_tiles + c, at_t, :] = n_i
                out += [n_r, n_i]
            return tuple(out)

        init = []
        for c in tiles:
            init += [hs_ref[:, 128 * c:128 * (c + 1)], hs_ref[:, D_STATE + 128 * c:D_STATE + 128 * (c + 1)]]
        fin = lax.fori_loop(0, tt, step, tuple(init), unroll=min(tt, 8))
        for k, c in enumerate(tiles):
            hs_ref[:, 128 * c:128 * (c + 1)] = fin[2 * k]
            hs_ref[:, D_STATE + 128 * c:D_STATE + 128 * (c + 1)] = fin[2 * k + 1]

    def h_cols(first_tile):
        return jnp.concatenate([get(first_tile + k) for k in range(4)], axis=-1).astype(BF16)

    ys = []
    for m in range(D_SSM // 128):
        y = _dot(h_cols(4 * m), wc_ref[512 * m:512 * (m + 1), 128 * m:128 * (m + 1)])
        y = y + _dot(h_cols(n_tiles + 4 * m),
                     wc_ref[D_STATE + 512 * m:D_STATE + 512 * (m + 1), 128 * m:128 * (m + 1)])
        ys.append(y)
    y = jnp.concatenate(ys, axis=-1) + d_ref[...] * u
    z = _dot(jax.nn.gelu(y).astype(BF16), wglu_ref[...]) + bglu_ref[...]
    s = z[:, :D_SSM] * jax.nn.sigmoid(z[:, D_SSM:])
    o_ref[...] = (_rms(s) * g_ref[...]).astype(BF16).reshape(o_ref.shape)

    @pl.when(j == pl.num_programs(1) - 1)
    def _():
        hr_ref[...] = hs_ref[:, :D_STATE]
        hi_ref[...] = hs_ref[:, D_STATE:]


def _s5(u, h0r, h0i, a_re, a_im, wb, wc, d, wglu, bglu, g, *, tt, stride, time_chunked):
    nbg = h0r.shape[0] // SUBLANES
    if time_chunked:
        nchunks = u.shape[1] // tt
        u_spec = pl.BlockSpec((SUBLANES, tt, D_SSM), lambda gi, j: (gi, j, 0))
    else:
        nchunks = 1
        u_spec = pl.BlockSpec((None, SUBLANES * tt, D_SSM), lambda gi, j: (gi, 0, 0))
    fix = lambda gi, j: (0, 0)
    st_spec = pl.BlockSpec((SUBLANES, D_STATE), lambda gi, j: (gi, 0))
    sd = jax.ShapeDtypeStruct
    return pl.pallas_call(
        functools.partial(_s5_kernel, tt=tt, stride=stride),
        grid=(nbg, nchunks),
        in_specs=[u_spec, st_spec, st_spec,
                  pl.BlockSpec((1, D_STATE), fix), pl.BlockSpec((1, D_STATE), fix),
                  pl.BlockSpec((D_SSM, 2 * D_STATE), fix), pl.BlockSpec((2 * D_STATE, D_SSM), fix),
                  pl.BlockSpec((1, D_SSM), fix), pl.BlockSpec((D_SSM, 2 * D_SSM), fix),
                  pl.BlockSpec((1, 2 * D_SSM), fix), pl.BlockSpec((1, D_SSM), fix)],
        out_specs=[u_spec, st_spec, st_spec],
        out_shape=[sd(u.shape, BF16), sd(h0r.shape, F32), sd(h0r.shape, F32)],
        scratch_shapes=[pltpu.VMEM((2 * D_STATE // 128, SUBLANES * stride, 128), F32),
                        pltpu.VMEM((SUBLANES, 2 * D_STATE), F32)],
        compiler_params=_cparams("parallel", "arbitrary"),
        name="s5",
    )(u, h0r, h0i, a_re, a_im, wb, wc, d, wglu, bglu, g)


def _to_row_tiles(ref, val):
    rows = val.shape[0]
    for c in range(D_MODEL // 128):
        ref[pl.ds(c, rows, stride=SUBLANES), :] = val[:, 128 * c:128 * (c + 1)]


def _from_row_tiles(ref, rows, lead=()):
    return jnp.concatenate(
        [ref[(*lead, pl.ds(c, rows, stride=SUBLANES), slice(None))] for c in range(D_MODEL // 128)], axis=-1)


def _out_proj_kernel(a_ref, s_ref, x_ref, wa_ref, ws_ref, g_ref, wr_ref, br_ref, x1_ref, xt_ref, lg_ref):
    x1 = x_ref[...] + _dot(a_ref[...], wa_ref[...]) + _dot(s_ref[...], ws_ref[...])
    x1_ref[...] = x1
    xn = _rms(x1) * g_ref[...]
    _to_row_tiles(xt_ref, xn)
    lg_ref[...] = _dot(xn.astype(BF16), wr_ref[...]) + br_ref[...]


def _out_proj(attn_n, ssm_n, x2d, wa, ws, g, wr, br):
    t = x2d.shape[0]
    tm = min(ROW_TILE, t)
    row = lambda i: (i, 0)
    fix = lambda i: (0, 0)
    sd = jax.ShapeDtypeStruct
    return pl.pallas_call(
        _out_proj_kernel,
        grid=(t // tm,),
        in_specs=[pl.BlockSpec((tm, D_ATTN), row), pl.BlockSpec((tm, D_SSM), row),
                  pl.BlockSpec((tm, D_MODEL), row),
                  pl.BlockSpec((D_ATTN, D_MODEL), fix), pl.BlockSpec((D_SSM, D_MODEL), fix),
                  pl.BlockSpec((1, D_MODEL), fix), pl.BlockSpec((D_MODEL, 128), fix),
                  pl.BlockSpec((1, 128), fix)],
        out_specs=[pl.BlockSpec((tm, D_MODEL), row), pl.BlockSpec((tm * SUBLANES, 128), row),
                   pl.BlockSpec((tm, 128), row)],
        out_shape=[sd((t, D_MODEL), F32), sd((t * SUBLANES, 128), F32), sd((t, 128), F32)],
        compiler_params=_cparams("parallel"),
        name="out_proj",
    )(attn_n, ssm_n, x2d, wa, ws, g, wr, br)


def _tile_span(ref, first_row, n_rows, lead=()):
    start = pl.multiple_of(first_row * SUBLANES, SUBLANES)
    return ref.at[(*lead, pl.ds(start, n_rows * SUBLANES), slice(None))]


def _run_copies(tile, src_tbl, cnt_tbl, loc_tbl, make_copy):
    def body(e, c):
        j = tile * N_EXPERTS + e
        cnt, src, loc = cnt_tbl[j], src_tbl[j], loc_tbl[j]
        off = 0
        for size in _RUN_SIZES:
            @pl.when((cnt & size) != 0)
            def _(off=off, size=size):
                make_copy(src + off, loc + off, size).start()
            off = off + (cnt & size)
        return c

    lax.fori_loop(0, N_EXPERTS, body, 0)


def _dispatch_kernel(src_tbl, cnt_tbl, loc_tbl, pend_ref, padded_ref, nu_ref, lp_ref, xa_ref, xb_ref, rows_ref,
                     buf, zbuf, sem, zsem, *, tiles_a, nblk):
    i = pl.program_id(0)
    n = pl.num_programs(0)
    slot = i % 2

    def zero_block(start_row):
        return pltpu.make_async_copy(zbuf, _tile_span(rows_ref, start_row, MOE_TILE), zsem)

    @pl.when(i == 0)
    def _():
        zbuf[...] = jnp.zeros_like(zbuf)
        for e in range(N_EXPERTS):
            @pl.when(padded_ref[e] > 0)
            def _(e=e):
                zero_block(pend_ref[e] - MOE_TILE).start()

        def tail_start(j, c):
            zero_block(j * MOE_TILE).start()
            return c

        def tail_wait(j, c):
            zero_block(j * MOE_TILE).wait()
            return c

        lax.fori_loop(nu_ref[0], nblk, tail_start, 0)
        for e in range(N_EXPERTS):
            @pl.when(padded_ref[e] > 0)
            def _(e=e):
                zero_block(pend_ref[e] - MOE_TILE).wait()
        lax.fori_loop(nu_ref[0], nblk, tail_wait, 0)

    def slot_done(s):
        return pltpu.make_async_copy(buf.at[s], buf.at[s], sem.at[s])

    @pl.when(i >= 2)
    def _():
        slot_done(slot).wait()

    def fill(x_ref):
        def body(t, c):
            v = x_ref[pl.ds(pl.multiple_of(t * SUBLANES, SUBLANES), SUBLANES), :]
            for k in range(TOP_K):
                row = lp_ref[0, t * TOP_K + k]
                buf[slot, pl.ds(pl.multiple_of(row * SUBLANES, SUBLANES), SUBLANES), :] = v
            return c

        lax.fori_loop(0, ROUTE_TILE, body, 0, unroll=8)

    @pl.when(i < tiles_a)
    def _():
        fill(xa_ref)

    @pl.when(i >= tiles_a)
    def _():
        fill(xb_ref)

    _run_copies(i, src_tbl, cnt_tbl, loc_tbl,
                lambda g, l, size: pltpu.make_async_copy(_tile_span(buf, l, size, lead=(slot,)),
                                                         _tile_span(rows_ref, g, size), sem.at[slot]))

    @pl.when(i == n - 1)
    def _():
        slot_done(1 - slot).wait()
        slot_done(slot).wait()


def _dispatch(runs, pend, padded, n_used, lp, xa, xb, n_rows):
    tile_rows = ROUTE_TILE * SUBLANES
    tiles_a, tiles_b = xa.shape[0] // tile_rows, xb.shape[0] // tile_rows
    assert tiles_a + tiles_b >= 2
    grid_spec = pltpu.PrefetchScalarGridSpec(
        num_scalar_prefetch=6, grid=(tiles_a + tiles_b,),
        in_specs=[pl.BlockSpec((None, 1, ROUTE_TILE * TOP_K), lambda i, *_: (i, 0, 0), memory_space=pltpu.SMEM),
                  pl.BlockSpec((tile_rows, 128), lambda i, *_: (jnp.minimum(i, tiles_a - 1), 0)),
                  pl.BlockSpec((tile_rows, 128), lambda i, *_: (jnp.maximum(i - tiles_a, 0), 0))],
        out_specs=pl.BlockSpec(memory_space=pl.ANY),
        scratch_shapes=[pltpu.VMEM((2, tile_rows * TOP_K, 128), F32),
                        pltpu.VMEM((MOE_TILE * SUBLANES, 128), F32),
                        pltpu.SemaphoreType.DMA((2,)), pltpu.SemaphoreType.DMA(())])
    return pl.pallas_call(
        functools.partial(_dispatch_kernel, tiles_a=tiles_a, nblk=n_rows // MOE_TILE), grid_spec=grid_spec,
        out_shape=jax.ShapeDtypeStruct((n_rows * SUBLANES, 128), F32),
        compiler_params=_cparams("arbitrary"), name="dispatch",
    )(*runs, pend, padded, n_used, lp, xa, xb)


def _moe_kernel(be_ref, nu_ref, x_ref, wgu_ref, bg_ref, bl_ref, wd_ref, bd_ref, perm_ref, o_ref,
                wg_s, wl_s, wd_s):
    i = pl.program_id(0)
    used = i < nu_ref[0]
    new_expert = (i == 0) | (be_ref[i] != be_ref[jnp.maximum(i - 1, 0)])

    @pl.when(used & new_expert)
    def _():
        for c in range(2 * D_FF // 256):
            r = _dot(wgu_ref[:, 256 * c:256 * (c + 1)].astype(BF16), perm_ref[...])
            wg_s[:, 128 * c:128 * (c + 1)] = r[:, :128].astype(BF16)
            wl_s[:, 128 * c:128 * (c + 1)] = r[:, 128:].astype(BF16)
        wd_s[...] = wd_ref[...].astype(BF16)

    @pl.when(used)
    def _():
        x = _from_row_tiles(x_ref, MOE_TILE).astype(BF16)
        glu = jnp.minimum(_dot(x, wg_s[...]) + bg_ref[...], SWIGLU_LIMIT)
        lin = jnp.clip(_dot(x, wl_s[...]) + bl_ref[...], -SWIGLU_LIMIT, SWIGLU_LIMIT)
        act = glu * jax.nn.sigmoid(SWIGLU_ALPHA * glu) * (lin + 1.0)
        _to_row_tiles(o_ref, _dot(act.astype(BF16), wd_s[...]) + bd_ref[...])

    @pl.when(jnp.logical_not(used))
    def _():
        o_ref[...] = jnp.zeros_like(o_ref)


def _moe(block_e, n_used, x_rows, wgu, bg, bl, wd, bd, perm):
    nblk = x_rows.shape[0] // (MOE_TILE * SUBLANES)
    row = lambda i, be, nu: (jnp.minimum(i, nu[0] - 1), 0)
    wsel = lambda i, be, nu: (be[i], 0, 0)
    grid_spec = pltpu.PrefetchScalarGridSpec(
        num_scalar_prefetch=2,
        grid=(nblk,),
        in_specs=[pl.BlockSpec((MOE_TILE * SUBLANES, 128), row),
                  pl.BlockSpec((None, D_MODEL, 2 * D_FF), wsel),
                  pl.BlockSpec((None, 1, D_FF), wsel), pl.BlockSpec((None, 1, D_FF), wsel),
                  pl.BlockSpec((None, D_FF, D_MODEL), wsel), pl.BlockSpec((None, 1, D_MODEL), wsel),
                  pl.BlockSpec((256, 256), lambda i, be, nu: (0, 0))],
        out_specs=pl.BlockSpec((MOE_TILE * SUBLANES, 128), lambda i, be, nu: (i, 0)),
        scratch_shapes=[pltpu.VMEM((D_MODEL, D_FF), BF16), pltpu.VMEM((D_MODEL, D_FF), BF16),
                        pltpu.VMEM((D_FF, D_MODEL), BF16)],
    )
    return pl.pallas_call(
        _moe_kernel,
        grid_spec=grid_spec,
        out_shape=jax.ShapeDtypeStruct(x_rows.shape, F32),
        compiler_params=_cparams("arbitrary"),
        name="moe",
    )(block_e, n_used, x_rows, wgu, bg, bl, wd, bd, perm)


def _combine_kernel(src_tbl, cnt_tbl, loc_tbl, lp_ref, gate_ref, rows_hbm, x1_ref, o_ref, buf, ybuf, sem, *,
                    tile_base):
    i = pl.program_id(0)
    n = pl.num_programs(0)

    def fetch(tile, s):
        _run_copies(tile_base + tile, src_tbl, cnt_tbl, loc_tbl,
                    lambda g, l, size: pltpu.make_async_copy(_tile_span(rows_hbm, g, size),
                                                             _tile_span(buf, l, size, lead=(s,)), sem.at[s]))

    @pl.when(i == 0)
    def _():
        fetch(0, 0)

    @pl.when(i + 1 < n)
    def _():
        fetch(i + 1, (i + 1) % 2)

    slot = i % 2
    pltpu.make_async_copy(buf.at[slot], buf.at[slot], sem.at[slot]).wait()

    def body(t, c):
        acc = None
        for k in range(TOP_K):
            row = lp_ref[0, t * TOP_K + k]
            v = gate_ref[0, t * TOP_K + k] * buf[slot, pl.ds(pl.multiple_of(row * SUBLANES, SUBLANES), SUBLANES), :]
            acc = v if acc is None else acc + v
        ybuf[pl.ds(pl.multiple_of(t * SUBLANES, SUBLANES), SUBLANES), :] = acc
        return c

    lax.fori_loop(0, ROUTE_TILE, body, 0, unroll=8)
    o_ref[...] = x1_ref[...] + _from_row_tiles(ybuf, ROUTE_TILE)


def _combine(runs, lp, gates, rows, x1, tile_base):
    t = x1.shape[0]
    tile_rows = ROUTE_TILE * SUBLANES
    smem_tile = pl.BlockSpec((None, 1, ROUTE_TILE * TOP_K), lambda i, *_: (tile_base + i, 0, 0),
                             memory_space=pltpu.SMEM)
    grid_spec = pltpu.PrefetchScalarGridSpec(
        num_scalar_prefetch=3,
        grid=(t // ROUTE_TILE,),
        in_specs=[smem_tile, smem_tile, pl.BlockSpec(memory_space=pl.ANY),
                  pl.BlockSpec((ROUTE_TILE, D_MODEL), lambda i, *_: (i, 0))],
        out_specs=pl.BlockSpec((ROUTE_TILE, D_MODEL), lambda i, *_: (i, 0)),
        scratch_shapes=[pltpu.VMEM((2, tile_rows * TOP_K, 128), F32), pltpu.VMEM((tile_rows, 128), F32),
                        pltpu.SemaphoreType.DMA((2,))],
    )
    return pl.pallas_call(
        functools.partial(_combine_kernel, tile_base=tile_base),
        grid_spec=grid_spec,
        out_shape=jax.ShapeDtypeStruct((t, D_MODEL), F32),
        compiler_params=_cparams("arbitrary"),
        name="combine",
    )(*runs, lp, gates, rows, x1)


def _lane_pick(cols):
    lane = lax.broadcasted_iota(jnp.int32, (cols[0].shape[0], len(cols)), 1)
    out = cols[-1]
    for k in range(len(cols) - 2, -1, -1):
        out = jnp.where(lane == k, cols[k], out)
    return out


def _topk_kernel(lg_ref, tri_ref, idx_ref, gate_ref, rank_ref, before_ref, cnt_ref, carry):
    @pl.when(pl.program_id(0) == 0)
    def _():
        carry[...] = jnp.zeros_like(carry)

    l = lg_ref[...]
    lane = lax.broadcasted_iota(jnp.int32, l.shape, 1)
    vals, idxs, sels = [], [], []
    for _ in range(TOP_K):
        m = jnp.max(l, axis=-1, keepdims=True)
        idx = jnp.min(jnp.where(l == m, lane, l.shape[1]), axis=-1, keepdims=True)
        sel = lane == idx
        l = jnp.where(sel, -jnp.inf, l)
        vals.append(m)
        idxs.append(idx)
        sels.append(sel)
    exps = [jnp.exp(v - vals[0]) for v in vals]
    den = exps[0] + exps[1] + exps[2] + exps[3]
    onehot = jnp.where(sels[0] | sels[1] | sels[2] | sels[3], 1.0, 0.0)
    within = _dot(tri_ref[...], onehot.astype(BF16))
    ranks = [jnp.sum(jnp.where(s, within, 0.0), axis=-1, keepdims=True) for s in sels]
    cnt = jnp.sum(onehot, axis=0, keepdims=True)
    idx_ref[...] = _lane_pick(idxs)
    gate_ref[...] = _lane_pick([e / den for e in exps])
    rank_ref[...] = _lane_pick(ranks).astype(jnp.int32)
    before_ref[...] = carry[...]
    cnt_ref[...] = cnt
    carry[...] = carry[...] + cnt


def _local_pos_kernel(idx_ref, rank_ref, loc_ref, lp_ref):
    lane = lax.broadcasted_iota(jnp.int32, (idx_ref.shape[0], 128), 1)
    loc = loc_ref[...]
    idx = idx_ref[...]
    cols = [jnp.sum(jnp.where(lane == idx[:, k:k + 1], loc, 0.0), axis=-1, keepdims=True)
            for k in range(TOP_K)]
    lp_ref[...] = _lane_pick(cols).astype(jnp.int32) + rank_ref[...]


def _route(logits):
    t = logits.shape[0]
    n_assign = t * TOP_K
    tr = ROUTE_TILE
    n_tiles = t // tr
    row = lambda i: (i, 0)
    fix = lambda i: (0, 0)
    per_tile = lambda i: (i, 0, 0)
    sd = jax.ShapeDtypeStruct
    k_spec = pl.BlockSpec((tr, TOP_K), row)
    t_spec = pl.BlockSpec((None, 1, 128), per_tile)
    tri = (jnp.arange(tr)[:, None] > jnp.arange(tr)[None, :]).astype(BF16)
    top_i, gates, rank, before, cnt = pl.pallas_call(
        _topk_kernel,
        grid=(n_tiles,),
        in_specs=[pl.BlockSpec((tr, 128), row), pl.BlockSpec((tr, tr), fix)],
        out_specs=[k_spec, k_spec, k_spec, t_spec, t_spec],
        out_shape=[sd((t, TOP_K), jnp.int32), sd((t, TOP_K), F32), sd((t, TOP_K), jnp.int32),
                   sd((n_tiles, 1, 128), F32), sd((n_tiles, 1, 128), F32)],
        scratch_shapes=[pltpu.VMEM((1, 128), F32)],
        compiler_params=_cparams("arbitrary"),
        name="topk",
    )(logits, tri)
    before = before[:, 0, :N_EXPERTS].astype(jnp.int32)
    cnt = cnt[:, 0, :N_EXPERTS].astype(jnp.int32)
    counts = before[-1] + cnt[-1]
    padded = ((counts + MOE_TILE - 1) // MOE_TILE) * MOE_TILE
    pend = jnp.cumsum(padded)
    run_src = (pend - padded)[None, :] + before
    run_loc = jnp.cumsum(cnt, axis=1) - cnt
    loc = jnp.pad(run_loc, ((0, 0), (0, 128 - N_EXPERTS))).astype(F32)[:, None, :]
    lp = pl.pallas_call(
        _local_pos_kernel,
        grid=(n_tiles,),
        in_specs=[k_spec, k_spec, t_spec],
        out_specs=k_spec,
        out_shape=sd((t, TOP_K), jnp.int32),
        compiler_params=_cparams("parallel"),
        name="local_pos",
    )(top_i, rank, loc)
    nblk = (n_assign + MOE_TILE - 1) // MOE_TILE + N_EXPERTS
    n_used = (pend[-1] // MOE_TILE).astype(jnp.int32)
    block_start = jnp.arange(nblk, dtype=jnp.int32) * MOE_TILE
    block_e = jnp.sum(pend[None, :] <= jnp.minimum(block_start, pend[-1] - 1)[:, None], axis=1)
    block_e = jnp.minimum(block_e, N_EXPERTS - 1).astype(jnp.int32)
    runs = (run_src.reshape(-1), cnt.reshape(-1), run_loc.reshape(-1))
    per_tile_smem = lambda a: a.reshape(n_tiles, 1, tr * TOP_K)
    return (per_tile_smem(gates), per_tile_smem(lp), runs, block_e, n_used.reshape(1), pend, padded,
            nblk * MOE_TILE)


def kernel(x_prompt, x_sample, cache_k, cache_v, state_ssm_re, state_ssm_im, norm_mix_g, w_in, q_norm_g, k_norm_g, attn_sinks, ssm_lambda_re, ssm_lambda_im, ssm_b_re, ssm_b_im, ssm_c_re, ssm_c_im, ssm_d, ssm_log_dt, w_glu, b_glu, attn_out_norm_g, ssm_out_norm_g, w_out, norm_ffn_g, w_router, b_router, w_gate_up, b_gate_up, w_down, b_down):
    depth = w_in.shape[0]
    assert depth == 1
    bp, sp, _ = x_prompt.shape
    bs, ss, _ = x_sample.shape
    tp, ts = bp * sp, bs * ss
    assert bp == SUBLANES and sp % S5_CHUNK == 0 and bs % SUBLANES == 0

    perm = jnp.asarray(_Q_PERM)
    w_in0 = w_in[0]
    wq = w_in0[:, :D_ATTN].reshape(D_MODEL, N_HEADS, HEAD_DIM)[:, perm].reshape(D_MODEL, D_ATTN)
    w_in_b = jnp.concatenate([wq, w_in0[:, D_ATTN:]], axis=1).astype(BF16)
    qkg = jnp.concatenate([jnp.tile(q_norm_g[0], N_HEADS), jnp.tile(k_norm_g[0], N_KV_HEADS)])[None]
    pmat = jnp.kron(jnp.eye(256 // HEAD_DIM, dtype=F32),
                    jnp.full((HEAD_DIM, HEAD_DIM), 1.0 / HEAD_DIM, F32)).astype(BF16)
    g_mix = norm_mix_g[0][None]
    sinks = attn_sinks[0]
    g_attn = attn_out_norm_g[0].reshape(N_HEADS, HEAD_DIM)[perm].reshape(1, D_ATTN)
    w_out0 = w_out[0]
    w_out_a = w_out0[:D_ATTN].reshape(N_HEADS, HEAD_DIM, D_MODEL)[perm].reshape(D_ATTN, D_MODEL).astype(BF16)
    w_out_s = w_out0[D_ATTN:].astype(BF16)
    g_ssm = ssm_out_norm_g[0][None]
    g_ffn = norm_ffn_g[0][None]
    w_r = jnp.pad(w_router[0], ((0, 0), (0, 128 - N_EXPERTS))).astype(BF16)
    b_r = jnp.pad(b_router[0], (0, 128 - N_EXPERTS), constant_values=NEG_INF)[None]

    a_re, a_im, bb_re, bb_im = _s5_prep(
        ssm_lambda_re[0], ssm_lambda_im[0], ssm_log_dt[0][:, None],
        jnp.swapaxes(ssm_b_re[0], 1, 2), jnp.swapaxes(ssm_b_im[0], 1, 2))
    eye_g = jnp.eye(N_SSM_GROUPS, dtype=F32)
    bd_b = lambda bb: jnp.einsum("ghp,gk->ghkp", bb, eye_g).reshape(D_SSM, D_STATE)
    wb = jnp.concatenate([bd_b(bb_re), bd_b(bb_im)], axis=1).astype(BF16)
    bd_c = lambda c: jnp.einsum("ghp,gk->gpkh", c, eye_g).reshape(D_STATE, D_SSM)
    wc = jnp.concatenate([bd_c(ssm_c_re[0]), -bd_c(ssm_c_im[0])], axis=0).astype(BF16)
    a_re, a_im = a_re.reshape(1, D_STATE), a_im.reshape(1, D_STATE)
    d_skip = ssm_d[0].reshape(1, D_SSM)
    w_glu_b = w_glu[0].astype(BF16)
    b_glu0 = b_glu[0][None]

    b_g = b_gate_up[0][:, None, 0::2]
    b_l = b_gate_up[0][:, None, 1::2]
    b_d = b_down[0][:, None, :]
    idx = jnp.arange(256)
    deint = (idx[None, :] == jnp.where(idx % 2 == 0, idx // 2, 128 + idx // 2)[:, None]).astype(BF16)

    xp2 = x_prompt.reshape(tp, D_MODEL)
    xs2 = x_sample.reshape(ts, D_MODEL)
    qp, kp, vp, up = _in_proj(xp2, g_mix, w_in_b, qkg, pmat)
    qs, ks, vs, us = _in_proj(xs2, g_mix, w_in_b, qkg, pmat)

    kp3, vp3 = kp.reshape(bp, sp, D_KV), vp.reshape(bp, sp, D_KV)
    attn_p = _band_attention(sinks, qp.reshape(bp, sp, D_ATTN), kp3, vp3, g_attn).reshape(tp, D_ATTN)
    ck = cache_k[0].reshape(bs, -1, D_KV)
    cv = cache_v[0].reshape(bs, -1, D_KV)
    attn_s = _cache_attention(sinks, qs, ks, vs, ck, cv, g_attn, ss)

    zeros_p = jnp.zeros((bp, D_STATE), F32)
    s5_args = (a_re, a_im, wb, wc, d_skip, w_glu_b, b_glu0, g_ssm)
    ssm_p, hr_p, hi_p = _s5(up.reshape(bp, sp, D_SSM), zeros_p, zeros_p, *s5_args,
                            tt=S5_CHUNK, stride=S5_CHUNK + S5_PAD, time_chunked=True)
    ssm_s, hr_s, hi_s = _s5(us.reshape(bs // SUBLANES, SUBLANES * ss, D_SSM),
                            state_ssm_re[0].reshape(bs, D_STATE), state_ssm_im[0].reshape(bs, D_STATE),
                            *s5_args, tt=ss, stride=ss, time_chunked=False)

    x1p, xtp, lgp = _out_proj(attn_p, ssm_p.reshape(tp, D_SSM), xp2, w_out_a, w_out_s, g_ffn, w_r, b_r)
    x1s, xts, lgs = _out_proj(attn_s, ssm_s.reshape(ts, D_SSM), xs2, w_out_a, w_out_s, g_ffn, w_r, b_r)

    logits = jnp.concatenate([lgp, lgs], axis=0)
    gates, lp, runs, block_e, n_used, pend, padded, n_rows = _route(logits)
    x_rows = _dispatch(runs, pend, padded, n_used, lp, xtp, xts, n_rows)
    out_rows = _moe(block_e, n_used, x_rows, w_gate_up[0], b_g, b_l, w_down[0], b_d, deint)
    yp = _combine(runs, lp, gates, out_rows, x1p, 0).reshape(bp, sp, D_MODEL)
    ys = _combine(runs, lp, gates, out_rows, x1s, tp // ROUTE_TILE).reshape(bs, ss, D_MODEL)

    kv5 = lambda a, b_: a.reshape(b_, -1, N_KV_HEADS, HEAD_DIM)
    new_kp = kv5(kp3[:, -WINDOW:], bp)[None]
    new_vp = kv5(vp3[:, -WINDOW:], bp)[None]
    ks3, vs3 = ks.reshape(bs, ss, D_KV), vs.reshape(bs, ss, D_KV)
    new_ks = kv5(jnp.concatenate([ck, ks3], axis=1)[:, ss:], bs)[None]
    new_vs = kv5(jnp.concatenate([cv, vs3], axis=1)[:, ss:], bs)[None]
    st = lambda h, b_: h.reshape(1, b_, N_SSM_GROUPS, SSM_STATE)
    return (yp, ys, new_kp, new_vp, st(hr_p, bp), st(hi_p, bp),
            new_ks, new_vs, st(hr_s, bs), st(hi_s, bs))
```

```python
import functools
import math

import jax
import jax.numpy as jnp
from jax import lax
from jax.experimental import pallas as pl
from jax.experimental.pallas import tpu as pltpu

F32 = jnp.float32
BF16 = jnp.bfloat16

D_MODEL = 1024
D_ATTN = 512
D_SSM = 512
HEAD_DIM = 64
N_HEADS = 8
N_KV_HEADS = 2
D_KV = N_KV_HEADS * HEAD_DIM
WINDOW = 128
SSM_GROUP = 16
N_SSM_GROUPS = 32
SSM_STATE = 64
D_STATE = N_SSM_GROUPS * SSM_STATE
N_EXPERTS = 32
TOP_K = 4
D_FF = 1024
SWIGLU_LIMIT = 7.0
SWIGLU_ALPHA = 1.702
RMS_EPS = 1e-6
NEG_INF = -1e30
D_IN_PROJ = D_ATTN + 2 * D_KV + D_SSM
D_QK = D_ATTN + D_KV

SUBLANES = 8
VMEM_LIMIT = 56 * 1024 * 1024

ROW_TILE = 512
MOE_TILE = 512
ROUTE_TILE = 512
_SLOT_ROWS = ROUTE_TILE * TOP_K * SUBLANES
_RUN_SIZES = tuple(1 << b for b in range(ROUTE_TILE.bit_length() - 1, -1, -1))
S5_CHUNK = 128
S5_PAD = 8
S5_COLS = 512

_Q_PERM = (0, 4, 1, 5, 2, 6, 3, 7)


def _cparams(*sem):
    return pltpu.CompilerParams(dimension_semantics=sem, vmem_limit_bytes=VMEM_LIMIT)


def _dot(a, b):
    return jnp.dot(a, b, preferred_element_type=F32)


def _rms(x):
    return x * lax.rsqrt(jnp.mean(x * x, axis=-1, keepdims=True) + RMS_EPS)


def _in_proj_kernel(x_ref, g_ref, w_ref, qkg_ref, p_ref, q_ref, k_ref, v_ref, u_ref):
    xn = _rms(x_ref[...]) * g_ref[...]
    h = _dot(xn.astype(BF16), w_ref[...])
    qk = h[:, :D_QK]
    sq = (qk * qk).astype(BF16)
    p = p_ref[...]
    ms = jnp.concatenate(
        [_dot(sq[:, 0:256], p), _dot(sq[:, 256:512], p), _dot(sq[:, 512:640], p[:128, :128])],
        axis=-1)
    qkn = qk * lax.rsqrt(ms + RMS_EPS) * qkg_ref[...]
    q_ref[...] = qkn[:, :D_ATTN].astype(BF16)
    k_ref[...] = qkn[:, D_ATTN:]
    v_ref[...] = h[:, D_QK:D_QK + D_KV]
    u_ref[...] = h[:, D_QK + D_KV:]


def _in_proj(x2d, g, w, qkg, pmat):
    t = x2d.shape[0]
    tm = min(ROW_TILE, t)
    row = lambda i: (i, 0)
    fix = lambda i: (0, 0)
    return pl.pallas_call(
        _in_proj_kernel,
        grid=(t // tm,),
        in_specs=[pl.BlockSpec((tm, D_MODEL), row), pl.BlockSpec((1, D_MODEL), fix),
                  pl.BlockSpec((D_MODEL, D_IN_PROJ), fix), pl.BlockSpec((1, D_QK), fix),
                  pl.BlockSpec((256, 256), fix)],
        out_specs=[pl.BlockSpec((tm, D_ATTN), row), pl.BlockSpec((tm, D_KV), row),
                   pl.BlockSpec((tm, D_KV), row), pl.BlockSpec((tm, D_SSM), row)],
        out_shape=[jax.ShapeDtypeStruct((t, D_ATTN), BF16), jax.ShapeDtypeStruct((t, D_KV), F32),
                   jax.ShapeDtypeStruct((t, D_KV), F32), jax.ShapeDtypeStruct((t, D_SSM), F32)],
        compiler_params=_cparams("parallel"),
        name="in_proj",
    )(x2d, g, w, qkg, pmat)


def _softmax_pv(s_blocks, v_blocks, sink):
    m = sink
    for s in s_blocks:
        m = jnp.maximum(m, jnp.max(s, axis=-1, keepdims=True))
    den = jnp.exp(sink - m)
    acc = None
    for s, v in zip(s_blocks, v_blocks):
        p = jnp.exp(s - m)
        den = den + jnp.sum(p, axis=-1, keepdims=True)
        pv = _dot(p.astype(BF16), v)
        acc = pv if acc is None else acc + pv
    return acc / den


def _band_attn_kernel(sink_ref, q_ref, kp_ref, kc_ref, vp_ref, vc_ref, g_ref, o_ref):
    i = pl.program_id(1)
    q = q_ref[...]
    kb = jnp.concatenate([kp_ref[...], kc_ref[...]], axis=0).astype(BF16)
    vb = jnp.concatenate([vp_ref[...], vc_ref[...]], axis=0).astype(BF16)
    row = lax.broadcasted_iota(jnp.int32, (WINDOW, 2 * WINDOW), 0)
    col = lax.broadcasted_iota(jnp.int32, (WINDOW, 2 * WINDOW), 1)
    mask = (col > row) & (col <= row + WINDOW) & ((col >= WINDOW) | (i > 0))
    lane = lax.broadcasted_iota(jnp.int32, (WINDOW, 128), 1)
    low = lane < HEAD_DIM
    zero = jnp.zeros((), BF16)
    outs = []
    for pair in range(N_HEADS // 2):
        qp = q[:, 128 * pair:128 * (pair + 1)]
        halves = []
        for par in range(2):
            qm = jnp.where(low if par == 0 else ~low, qp, zero)
            s = lax.dot_general(qm, kb, (((1,), (1,)), ((), ())), preferred_element_type=F32)
            s = jnp.where(mask, s * (HEAD_DIM ** -0.5), NEG_INF)
            halves.append(_softmax_pv([s], [vb], sink_ref[_Q_PERM[2 * pair + par]]))
        outs.append(jnp.where(low, halves[0], halves[1]))
    o = jnp.concatenate(outs, axis=-1)
    o_ref[...] = (_rms(o) * g_ref[...]).astype(BF16)


def _band_attention(sinks, q, k, v, g):
    b, s, _ = q.shape
    nb = s // WINDOW
    cur = lambda bi, i: (bi, i, 0)
    prev = lambda bi, i: (bi, jnp.maximum(i - 1, 0), 0)
    kv_spec = lambda im: pl.BlockSpec((None, WINDOW, D_KV), im)
    return pl.pallas_call(
        _band_attn_kernel,
        grid=(b, nb),
        in_specs=[pl.BlockSpec(memory_space=pltpu.SMEM),
                  pl.BlockSpec((None, WINDOW, D_ATTN), cur),
                  kv_spec(prev), kv_spec(cur), kv_spec(prev), kv_spec(cur),
                  pl.BlockSpec((1, D_ATTN), lambda bi, i: (0, 0))],
        out_specs=pl.BlockSpec((None, WINDOW, D_ATTN), cur),
        out_shape=jax.ShapeDtypeStruct((b, s, D_ATTN), BF16),
        compiler_params=_cparams("parallel", "parallel"),
        name="band_attn",
    )(sinks, q, k, k, v, v, g)


_PAIR_ROWS = 8
_CACHE_BB = 8


def _cache_attn_kernel(sink_ref, q_ref, kn_ref, vn_ref, ck_ref, cv_ref, g_ref, o_ref, *, n_new):
    n_buf = ck_ref.shape[1]
    rows_blk = _CACHE_BB * n_new
    knew = kn_ref[...].astype(BF16)
    vnew = vn_ref[...].astype(BF16)
    lane = lax.broadcasted_iota(jnp.int32, (_PAIR_ROWS, 128), 1)
    low = lane < HEAD_DIM
    zero = jnp.zeros((), BF16)
    n_stack = N_HEADS * _PAIR_ROWS
    r = lax.broadcasted_iota(jnp.int32, (n_stack, 1), 0) % _PAIR_ROWS
    r_seq, r_tok = r // n_new, r % n_new
    colc = lax.broadcasted_iota(jnp.int32, (n_stack, 2 * n_buf), 1)
    c_seq, c_pos = colc // n_buf, colc % n_buf
    mask_c = (c_seq == r_seq) & (c_pos + WINDOW > r_tok + n_buf)
    coln = lax.broadcasted_iota(jnp.int32, (n_stack, rows_blk), 1)
    sink_col = jnp.concatenate(
        [jnp.full((_PAIR_ROWS, 1), sink_ref[_Q_PERM[h]], F32) for h in range(N_HEADS)], axis=0)
    for sp in range(_CACHE_BB // 2):
        q = q_ref[_PAIR_ROWS * sp:_PAIR_ROWS * (sp + 1), :]
        pieces = []
        for pair in range(N_HEADS // 2):
            qp = q[:, 128 * pair:128 * (pair + 1)]
            pieces.append(jnp.where(low, qp, zero))
            pieces.append(jnp.where(low, zero, qp))
        qs = jnp.concatenate(pieces, axis=0)
        kc = jnp.concatenate([ck_ref[2 * sp], ck_ref[2 * sp + 1]], axis=0).astype(BF16)
        vc = jnp.concatenate([cv_ref[2 * sp], cv_ref[2 * sp + 1]], axis=0).astype(BF16)
        nt = (((1,), (1,)), ((), ()))
        s_c = lax.dot_general(qs, kc, nt, preferred_element_type=F32) * (HEAD_DIM ** -0.5)
        s_n = lax.dot_general(qs, knew, nt, preferred_element_type=F32) * (HEAD_DIM ** -0.5)
        s_c = jnp.where(mask_c, s_c, NEG_INF)
        n_seq, n_tok = coln // n_new - 2 * sp, coln % n_new
        mask_n = (n_seq == r_seq) & (n_tok <= r_tok)
        s_n = jnp.where(mask_n, s_n, NEG_INF)
        o = _softmax_pv([s_c, s_n], [vc, vnew], sink_col)
        outs = [jnp.where(low, o[16 * pair:16 * pair + 8], o[16 * pair + 8:16 * pair + 16])
                for pair in range(N_HEADS // 2)]
        oo = jnp.concatenate(outs, axis=-1)
        o_ref[_PAIR_ROWS * sp:_PAIR_ROWS * (sp + 1), :] = (_rms(oo) * g_ref[...]).astype(BF16)


def _cache_attention(sinks, q, k, v, cache_k, cache_v, g, n_new):
    t = q.shape[0]
    nb, n_buf, _ = cache_k.shape
    assert n_new * 2 == _PAIR_ROWS and nb % _CACHE_BB == 0 and n_buf == WINDOW
    rows = _CACHE_BB * n_new
    row = lambda i: (i, 0)
    cache_spec = pl.BlockSpec((_CACHE_BB, n_buf, D_KV), lambda i: (i, 0, 0))
    return pl.pallas_call(
        functools.partial(_cache_attn_kernel, n_new=n_new),
        grid=(nb // _CACHE_BB,),
        in_specs=[pl.BlockSpec(memory_space=pltpu.SMEM),
                  pl.BlockSpec((rows, D_ATTN), row), pl.BlockSpec((rows, D_KV), row),
                  pl.BlockSpec((rows, D_KV), row), cache_spec, cache_spec,
                  pl.BlockSpec((1, D_ATTN), lambda i: (0, 0))],
        out_specs=pl.BlockSpec((rows, D_ATTN), row),
        out_shape=jax.ShapeDtypeStruct((t, D_ATTN), BF16),
        compiler_params=_cparams("parallel"),
        name="cache_attn",
    )(sinks, q, k, v, cache_k, cache_v, g)


def _s5_prep_kernel(lre_ref, lim_ref, ldt_ref, bre_ref, bim_ref, are_ref, aim_ref, bbre_ref, bbim_ref):
    dt = jnp.exp(ldt_ref[...])
    l_re = jnp.minimum(lre_ref[...], -1e-4)
    l_im = lim_ref[...]
    mag = jnp.exp(l_re * dt)
    a_re = mag * jnp.cos(l_im * dt)
    a_im = mag * jnp.sin(l_im * dt)
    den = l_re * l_re + l_im * l_im
    n_re = a_re - 1.0
    z_re = (n_re * l_re + a_im * l_im) / den
    z_im = (a_im * l_re - n_re * l_im) / den
    are_ref[...] = a_re
    aim_ref[...] = a_im
    br, bi = bre_ref[...], bim_ref[...]
    zr, zi = z_re[:, None, :], z_im[:, None, :]
    bbre_ref[...] = zr * br - zi * bi
    bbim_ref[...] = zr * bi + zi * br


def _s5_prep(lam_re, lam_im, log_dt, b_re_t, b_im_t):
    g, p = lam_re.shape
    sd = jax.ShapeDtypeStruct
    return pl.pallas_call(
        _s5_prep_kernel,
        out_shape=[sd((g, p), F32), sd((g, p), F32), sd(b_re_t.shape, F32), sd(b_re_t.shape, F32)],
        name="s5_prep",
    )(lam_re, lam_im, log_dt, b_re_t, b_im_t)


def _s5_kernel(u_ref, h0r_ref, h0i_ref, ar_ref, ai_ref, wb_ref, wc_ref, d_ref, wglu_ref, bglu_ref,
               g_ref, o_ref, hr_ref, hi_ref, bu_ref, hs_ref, *, tt, stride):
    j = pl.program_id(1)
    rows = SUBLANES * tt
    n_tiles = D_STATE // 128

    @pl.when(j == 0)
    def _():
        hs_ref[:, :D_STATE] = h0r_ref[...]
        hs_ref[:, D_STATE:] = h0i_ref[...]

    u = u_ref[...].reshape(rows, D_SSM)
    ub = u.astype(BF16)

    def put(c, val):
        if stride == tt:
            bu_ref[c, 0:rows, :] = val
        else:
            for b in range(SUBLANES):
                bu_ref[c, b * stride:b * stride + tt, :] = val[b * tt:(b + 1) * tt]

    def get(c):
        if stride == tt:
            return bu_ref[c, 0:rows, :]
        return jnp.concatenate([bu_ref[c, b * stride:b * stride + tt, :] for b in range(SUBLANES)], axis=0)

    for n in range(2 * D_STATE // 256):
        band = (n % (D_STATE // 256)) // 2
        res = _dot(ub[:, 128 * band:128 * (band + 1)],
                   wb_ref[128 * band:128 * (band + 1), 256 * n:256 * (n + 1)])
        put(2 * n, res[:, :128])
        put(2 * n + 1, res[:, 128:])

    tiles_per_pass = S5_COLS // 128
    for c0 in range(0, n_tiles, tiles_per_pass):
        tiles = range(c0, c0 + tiles_per_pass)
        a_r = [jnp.broadcast_to(ar_ref[:, 128 * c:128 * (c + 1)], (SUBLANES, 128)) for c in tiles]
        a_i = [jnp.broadcast_to(ai_ref[:, 128 * c:128 * (c + 1)], (SUBLANES, 128)) for c in tiles]

        def step(t, carry, tiles=tiles, a_r=a_r, a_i=a_i):
            at_t = pl.ds(t, SUBLANES, stride=stride)
            out = []
            for k, c in enumerate(tiles):
                h_r, h_i = carry[2 * k], carry[2 * k + 1]
                n_r = a_r[k] * h_r - a_i[k] * h_i + bu_ref[c, at_t, :]
                n_i = a_r[k] * h_i + a_i[k] * h_r + bu_ref[n_tiles + c, at_t, :]
                bu_ref[c, at_t, :] = n_r
                bu_ref[n_tiles + c, at_t, :] = n_i
                out += [n_r, n_i]
            return tuple(out)

        init = []
        for c in tiles:
            init += [hs_ref[:, 128 * c:128 * (c + 1)], hs_ref[:, D_STATE + 128 * c:D_STATE + 128 * (c + 1)]]
        fin = lax.fori_loop(0, tt, step, tuple(init), unroll=min(tt, 8))
        for k, c in enumerate(tiles):
            hs_ref[:, 128 * c:128 * (c + 1)] = fin[2 * k]
            hs_ref[:, D_STATE + 128 * c:D_STATE + 128 * (c + 1)] = fin[2 * k + 1]

    def h_cols(first_tile):
        return jnp.concatenate([get(first_tile + k) for k in range(4)], axis=-1).astype(BF16)

    ys = []
    for m in range(D_SSM // 128):
        y = _dot(h_cols(4 * m), wc_ref[512 * m:512 * (m + 1), 128 * m:128 * (m + 1)])
        y = y + _dot(h_cols(n_tiles + 4 * m),
                     wc_ref[D_STATE + 512 * m:D_STATE + 512 * (m + 1), 128 * m:128 * (m + 1)])
        ys.append(y)
    y = jnp.concatenate(ys, axis=-1) + d_ref[...] * u
    z = _dot(jax.nn.gelu(y).astype(BF16), wglu_ref[...]) + bglu_ref[...]
    s = z[:, :D_SSM] * jax.nn.sigmoid(z[:, D_SSM:])
    o_ref[...] = (_rms(s) * g_ref[...]).astype(BF16).reshape(o_ref.shape)

    @pl.when(j == pl.num_programs(1) - 1)
    def _():
        hr_ref[...] = hs_ref[:, :D_STATE]
        hi_ref[...] = hs_ref[:, D_STATE:]


def _s5(u, h0r, h0i, a_re, a_im, wb, wc, d, wglu, bglu, g, *, tt, stride, time_chunked):
    nbg = h0r.shape[0] // SUBLANES
    if time_chunked:
        nchunks = u.shape[1] // tt
        u_spec = pl.BlockSpec((SUBLANES, tt, D_SSM), lambda gi, j: (gi, j, 0))
    else:
        nchunks = 1
        u_spec = pl.BlockSpec((None, SUBLANES * tt, D_SSM), lambda gi, j: (gi, 0, 0))
    fix = lambda gi, j: (0, 0)
    st_spec = pl.BlockSpec((SUBLANES, D_STATE), lambda gi, j: (gi, 0))
    sd = jax.ShapeDtypeStruct
    return pl.pallas_call(
        functools.partial(_s5_kernel, tt=tt, stride=stride),
        grid=(nbg, nchunks),
        in_specs=[u_spec, st_spec, st_spec,
                  pl.BlockSpec((1, D_STATE), fix), pl.BlockSpec((1, D_STATE), fix),
                  pl.BlockSpec((D_SSM, 2 * D_STATE), fix), pl.BlockSpec((2 * D_STATE, D_SSM), fix),
                  pl.BlockSpec((1, D_SSM), fix), pl.BlockSpec((D_SSM, 2 * D_SSM), fix),
                  pl.BlockSpec((1, 2 * D_SSM), fix), pl.BlockSpec((1, D_SSM), fix)],
        out_specs=[u_spec, st_spec, st_spec],
        out_shape=[sd(u.shape, BF16), sd(h0r.shape, F32), sd(h0r.shape, F32)],
        scratch_shapes=[pltpu.VMEM((2 * D_STATE // 128, SUBLANES * stride, 128), F32),
                        pltpu.VMEM((SUBLANES, 2 * D_STATE), F32)],
        compiler_params=_cparams("parallel", "arbitrary"),
        name="s5",
    )(u, h0r, h0i, a_re, a_im, wb, wc, d, wglu, bglu, g)


def _to_row_tiles(ref, val, first=0):
    rows = val.shape[0]
    for c in range(D_MODEL // 128):
        ref[pl.ds(first * SUBLANES + c, rows, stride=SUBLANES), :] = val[:, 128 * c:128 * (c + 1)]


def _from_row_tiles(ref, rows, lead=(), first=0):
    return jnp.concatenate(
        [ref[(*lead, pl.ds(first * SUBLANES + c, rows, stride=SUBLANES), slice(None))]
         for c in range(D_MODEL // 128)], axis=-1)


def _out_proj_kernel(a_ref, s_ref, x_ref, wa_ref, ws_ref, g_ref, wr_ref, br_ref, x1_ref, xt_ref, lg_ref):
    x1 = x_ref[...] + _dot(a_ref[...], wa_ref[...]) + _dot(s_ref[...], ws_ref[...])
    x1_ref[...] = x1
    xn = _rms(x1) * g_ref[...]
    _to_row_tiles(xt_ref, xn)
    lg_ref[...] = _dot(xn.astype(BF16), wr_ref[...]) + br_ref[...]


def _out_proj(attn_n, ssm_n, x2d, wa, ws, g, wr, br):
    t = x2d.shape[0]
    tm = min(ROW_TILE, t)
    row = lambda i: (i, 0)
    fix = lambda i: (0, 0)
    sd = jax.ShapeDtypeStruct
    return pl.pallas_call(
        _out_proj_kernel,
        grid=(t // tm,),
        in_specs=[pl.BlockSpec((tm, D_ATTN), row), pl.BlockSpec((tm, D_SSM), row),
                  pl.BlockSpec((tm, D_MODEL), row),
                  pl.BlockSpec((D_ATTN, D_MODEL), fix), pl.BlockSpec((D_SSM, D_MODEL), fix),
                  pl.BlockSpec((1, D_MODEL), fix), pl.BlockSpec((D_MODEL, 128), fix),
                  pl.BlockSpec((1, 128), fix)],
        out_specs=[pl.BlockSpec((tm, D_MODEL), row), pl.BlockSpec((tm * SUBLANES, 128), row),
                   pl.BlockSpec((tm, 128), row)],
        out_shape=[sd((t, D_MODEL), F32), sd((t * SUBLANES, 128), F32), sd((t, 128), F32)],
        compiler_params=_cparams("parallel"),
        name="out_proj",
    )(attn_n, ssm_n, x2d, wa, ws, g, wr, br)


def _tile_span(ref, first_row, n_rows, lead=()):
    start = pl.multiple_of(first_row * SUBLANES, SUBLANES)
    return ref.at[(*lead, pl.ds(start, n_rows * SUBLANES), slice(None))]


def _run_copies(tile, src_tbl, cnt_tbl, loc_tbl, make_copy):
    def body(e, c):
        j = tile * N_EXPERTS + e
        cnt, src, loc = cnt_tbl[j], src_tbl[j], loc_tbl[j]
        off = 0
        for size in _RUN_SIZES:
            @pl.when((cnt & size) != 0)
            def _(off=off, size=size):
                make_copy(src + off, loc + off, size).start()
            off = off + (cnt & size)
        return c

    lax.fori_loop(0, N_EXPERTS, body, 0)


def _dispatch_kernel(src_tbl, cnt_tbl, loc_tbl, pend_ref, padded_ref, nu_ref, off_ref, xa_ref, xb_ref, rows_ref,
                     buf, zbuf, sem, zsem, *, tiles_a, nblk):
    i = pl.program_id(0)
    n = pl.num_programs(0)
    slot = i % 2

    def zero_block(start_row):
        return pltpu.make_async_copy(zbuf, _tile_span(rows_ref, start_row, MOE_TILE), zsem)

    @pl.when(i == 0)
    def _():
        zbuf[...] = jnp.zeros_like(zbuf)
        for e in range(N_EXPERTS):
            @pl.when(padded_ref[e] > 0)
            def _(e=e):
                zero_block(pend_ref[e] - MOE_TILE).start()

        def tail_start(j, c):
            zero_block(j * MOE_TILE).start()
            return c

        def tail_wait(j, c):
            zero_block(j * MOE_TILE).wait()
            return c

        lax.fori_loop(nu_ref[0], nblk, tail_start, 0)
        for e in range(N_EXPERTS):
            @pl.when(padded_ref[e] > 0)
            def _(e=e):
                zero_block(pend_ref[e] - MOE_TILE).wait()
        lax.fori_loop(nu_ref[0], nblk, tail_wait, 0)

    def slot_rows(s):
        return buf.at[pl.ds(pl.multiple_of(s * _SLOT_ROWS, _SLOT_ROWS), _SLOT_ROWS), :]

    def slot_done(s):
        return pltpu.make_async_copy(slot_rows(s), slot_rows(s), sem.at[s])

    @pl.when(i >= 2)
    def _():
        slot_done(slot).wait()

    def fill(x_ref):
        def body(t, c):
            v = x_ref[pl.ds(pl.multiple_of(t * SUBLANES, SUBLANES), SUBLANES), :]
            for k in range(TOP_K):
                off = pl.multiple_of(off_ref[0, t * TOP_K + k], SUBLANES)
                buf[pl.ds(off, SUBLANES), :] = v
            return c

        lax.fori_loop(0, ROUTE_TILE, body, 0, unroll=8)

    @pl.when(i < tiles_a)
    def _():
        fill(xa_ref)

    @pl.when(i >= tiles_a)
    def _():
        fill(xb_ref)

    _run_copies(i, src_tbl, cnt_tbl, loc_tbl,
                lambda g, l, size: pltpu.make_async_copy(_tile_span(slot_rows(slot), l, size),
                                                         _tile_span(rows_ref, g, size), sem.at[slot]))

    @pl.when(i == n - 1)
    def _():
        slot_done(1 - slot).wait()
        slot_done(slot).wait()


def _dispatch(runs, pend, padded, n_used, off, xa, xb, n_rows):
    tile_rows = ROUTE_TILE * SUBLANES
    tiles_a, tiles_b = xa.shape[0] // tile_rows, xb.shape[0] // tile_rows
    assert tiles_a + tiles_b >= 2
    grid_spec = pltpu.PrefetchScalarGridSpec(
        num_scalar_prefetch=6, grid=(tiles_a + tiles_b,),
        in_specs=[pl.BlockSpec((None, 1, ROUTE_TILE * TOP_K), lambda i, *_: (i, 0, 0), memory_space=pltpu.SMEM),
                  pl.BlockSpec((tile_rows, 128), lambda i, *_: (jnp.minimum(i, tiles_a - 1), 0)),
                  pl.BlockSpec((tile_rows, 128), lambda i, *_: (jnp.maximum(i - tiles_a, 0), 0))],
        out_specs=pl.BlockSpec(memory_space=pl.ANY),
        scratch_shapes=[pltpu.VMEM((2 * _SLOT_ROWS, 128), F32),
                        pltpu.VMEM((MOE_TILE * SUBLANES, 128), F32),
                        pltpu.SemaphoreType.DMA((2,)), pltpu.SemaphoreType.DMA(())])
    return pl.pallas_call(
        functools.partial(_dispatch_kernel, tiles_a=tiles_a, nblk=n_rows // MOE_TILE), grid_spec=grid_spec,
        out_shape=jax.ShapeDtypeStruct((n_rows * SUBLANES, 128), F32),
        compiler_params=_cparams("arbitrary"), name="dispatch",
    )(*runs, pend, padded, n_used, off, xa, xb)


def _moe_kernel(be_ref, nu_ref, nv_ref, x_ref, wgu_ref, bg_ref, bl_ref, wd_ref, bd_ref, perm_ref, o_ref,
                wg_s, wl_s, wd_s):
    i = pl.program_id(0)
    used = i < nu_ref[0]
    new_expert = (i == 0) | (be_ref[i] != be_ref[jnp.maximum(i - 1, 0)])

    @pl.when(used & new_expert)
    def _():
        for c in range(2 * D_FF // 256):
            r = _dot(wgu_ref[:, 256 * c:256 * (c + 1)].astype(BF16), perm_ref[...])
            wg_s[:, 128 * c:128 * (c + 1)] = r[:, :128].astype(BF16)
            wl_s[:, 128 * c:128 * (c + 1)] = r[:, 128:].astype(BF16)
        wd_s[...] = wd_ref[...].astype(BF16)

    half = MOE_TILE // 2
    for h in range(2):
        live = used & (nv_ref[i] > h * half)

        @pl.when(live)
        def _(h=h):
            x = _from_row_tiles(x_ref, half, first=h * half).astype(BF16)
            glu = jnp.minimum(_dot(x, wg_s[...]) + bg_ref[...], SWIGLU_LIMIT)
            lin = jnp.clip(_dot(x, wl_s[...]) + bl_ref[...], -SWIGLU_LIMIT, SWIGLU_LIMIT)
            act = glu * jax.nn.sigmoid(SWIGLU_ALPHA * glu) * (lin + 1.0)
            _to_row_tiles(o_ref, _dot(act.astype(BF16), wd_s[...]) + bd_ref[...], first=h * half)

        @pl.when(jnp.logical_not(live))
        def _(h=h):
            o_ref[pl.ds(h * half * SUBLANES, half * SUBLANES), :] = jnp.zeros((half * SUBLANES, 128), F32)


def _moe(block_e, n_used, n_valid, x_rows, wgu, bg, bl, wd, bd, perm):
    nblk = x_rows.shape[0] // (MOE_TILE * SUBLANES)
    row = lambda i, be, nu, nv: (jnp.minimum(i, nu[0] - 1), 0)
    wsel = lambda i, be, nu, nv: (be[i], 0, 0)
    grid_spec = pltpu.PrefetchScalarGridSpec(
        num_scalar_prefetch=3,
        grid=(nblk,),
        in_specs=[pl.BlockSpec((MOE_TILE * SUBLANES, 128), row),
                  pl.BlockSpec((None, D_MODEL, 2 * D_FF), wsel),
                  pl.BlockSpec((None, 1, D_FF), wsel), pl.BlockSpec((None, 1, D_FF), wsel),
                  pl.BlockSpec((None, D_FF, D_MODEL), wsel), pl.BlockSpec((None, 1, D_MODEL), wsel),
                  pl.BlockSpec((256, 256), lambda i, *_: (0, 0))],
        out_specs=pl.BlockSpec((MOE_TILE * SUBLANES, 128), lambda i, *_: (i, 0)),
        scratch_shapes=[pltpu.VMEM((D_MODEL, D_FF), BF16), pltpu.VMEM((D_MODEL, D_FF), BF16),
                        pltpu.VMEM((D_FF, D_MODEL), BF16)],
    )
    return pl.pallas_call(
        _moe_kernel,
        grid_spec=grid_spec,
        out_shape=jax.ShapeDtypeStruct(x_rows.shape, F32),
        compiler_params=_cparams("arbitrary"),
        name="moe",
    )(block_e, n_used, n_valid, x_rows, wgu, bg, bl, wd, bd, perm)


def _combine_kernel(src_tbl, cnt_tbl, loc_tbl, off_ref, gate_ref, rows_hbm, x1_ref, o_ref, buf, ybuf, sem, *,
                    tile_base):
    i = pl.program_id(0)
    n = pl.num_programs(0)

    def slot_rows(s):
        return buf.at[pl.ds(pl.multiple_of(s * _SLOT_ROWS, _SLOT_ROWS), _SLOT_ROWS), :]

    def fetch(tile, s):
        _run_copies(tile_base + tile, src_tbl, cnt_tbl, loc_tbl,
                    lambda g, l, size: pltpu.make_async_copy(_tile_span(rows_hbm, g, size),
                                                             _tile_span(slot_rows(s), l, size), sem.at[s]))

    @pl.when(i == 0)
    def _():
        fetch(0, 0)

    @pl.when(i + 1 < n)
    def _():
        fetch(i + 1, (i + 1) % 2)

    slot = i % 2
    pltpu.make_async_copy(slot_rows(slot), slot_rows(slot), sem.at[slot]).wait()

    def body(t, c):
        acc = None
        for k in range(TOP_K):
            off = pl.multiple_of(off_ref[0, t * TOP_K + k], SUBLANES)
            v = gate_ref[0, t * TOP_K + k] * buf[pl.ds(off, SUBLANES), :]
            acc = v if acc is None else acc + v
        ybuf[pl.ds(pl.multiple_of(t * SUBLANES, SUBLANES), SUBLANES), :] = acc
        return c

    lax.fori_loop(0, ROUTE_TILE, body, 0, unroll=8)
    o_ref[...] = x1_ref[...] + _from_row_tiles(ybuf, ROUTE_TILE)


def _combine(runs, off, gates, rows, x1, tile_base):
    assert tile_base % 2 == 0
    t = x1.shape[0]
    tile_rows = ROUTE_TILE * SUBLANES
    smem_tile = pl.BlockSpec((None, 1, ROUTE_TILE * TOP_K), lambda i, *_: (tile_base + i, 0, 0),
                             memory_space=pltpu.SMEM)
    grid_spec = pltpu.PrefetchScalarGridSpec(
        num_scalar_prefetch=3,
        grid=(t // ROUTE_TILE,),
        in_specs=[smem_tile, smem_tile, pl.BlockSpec(memory_space=pl.ANY),
                  pl.BlockSpec((ROUTE_TILE, D_MODEL), lambda i, *_: (i, 0))],
        out_specs=pl.BlockSpec((ROUTE_TILE, D_MODEL), lambda i, *_: (i, 0)),
        scratch_shapes=[pltpu.VMEM((2 * _SLOT_ROWS, 128), F32), pltpu.VMEM((tile_rows, 128), F32),
                        pltpu.SemaphoreType.DMA((2,))],
    )
    return pl.pallas_call(
        functools.partial(_combine_kernel, tile_base=tile_base),
        grid_spec=grid_spec,
        out_shape=jax.ShapeDtypeStruct((t, D_MODEL), F32),
        compiler_params=_cparams("arbitrary"),
        name="combine",
    )(*runs, off, gates, rows, x1)


def _lane_pick(cols):
    lane = lax.broadcasted_iota(jnp.int32, (cols[0].shape[0], len(cols)), 1)
    out = cols[-1]
    for k in range(len(cols) - 2, -1, -1):
        out = jnp.where(lane == k, cols[k], out)
    return out


def _topk_kernel(lga_ref, lgb_ref, tri_ref, upper_ref, gate_ref, off_ref, before_ref, cnt_ref, loc_ref, carry, *,
                 tiles_a):
    i = pl.program_id(0)

    @pl.when(i == 0)
    def _():
        carry[...] = jnp.zeros_like(carry)

    l = jnp.where(i < tiles_a, lga_ref[...], lgb_ref[...])
    lane = lax.broadcasted_iota(jnp.int32, l.shape, 1)
    vals, sels = [], []
    for _ in range(TOP_K):
        m = jnp.max(l, axis=-1, keepdims=True)
        idx = jnp.min(jnp.where(l == m, lane, l.shape[1]), axis=-1, keepdims=True)
        sel = lane == idx
        l = jnp.where(sel, -jnp.inf, l)
        vals.append(m)
        sels.append(sel)
    exps = [jnp.exp(v - vals[0]) for v in vals]
    den = exps[0] + exps[1] + exps[2] + exps[3]
    onehot = jnp.where(sels[0] | sels[1] | sels[2] | sels[3], 1.0, 0.0)
    within = _dot(tri_ref[...], onehot.astype(BF16))
    cnt = jnp.sum(onehot, axis=0, keepdims=True)
    cnt_hi = jnp.floor(cnt * (1.0 / 16.0))
    cnt_lo = cnt - 16.0 * cnt_hi
    loc = 16.0 * _dot(cnt_hi.astype(BF16), upper_ref[...]) + _dot(cnt_lo.astype(BF16), upper_ref[...])
    slot_base = ((i % 2) * (ROUTE_TILE * TOP_K)).astype(F32)
    rows = [jnp.sum(jnp.where(s, within + loc, 0.0), axis=-1, keepdims=True) for s in sels]
    gate_ref[...] = _lane_pick([e / den for e in exps])
    off_ref[...] = ((_lane_pick(rows) + slot_base) * SUBLANES).astype(jnp.int32)
    before_ref[...] = carry[...]
    cnt_ref[...] = cnt
    loc_ref[...] = loc
    carry[...] = carry[...] + cnt


def _route(lga, lgb):
    tr = ROUTE_TILE
    tiles_a, tiles_b = lga.shape[0] // tr, lgb.shape[0] // tr
    n_tiles = tiles_a + tiles_b
    t = n_tiles * tr
    n_assign = t * TOP_K
    row = lambda i: (i, 0)
    fix = lambda i: (0, 0)
    sd = jax.ShapeDtypeStruct
    k_spec = pl.BlockSpec((tr, TOP_K), row)
    t_spec = pl.BlockSpec((None, 1, 128), lambda i: (i, 0, 0))
    tri = (jnp.arange(tr)[:, None] > jnp.arange(tr)[None, :]).astype(BF16)
    upper = (jnp.arange(128)[:, None] < jnp.arange(128)[None, :]).astype(BF16)
    gates, off, before, cnt, loc = pl.pallas_call(
        functools.partial(_topk_kernel, tiles_a=tiles_a),
        grid=(n_tiles,),
        in_specs=[pl.BlockSpec((tr, 128), lambda i: (jnp.minimum(i, tiles_a - 1), 0)),
                  pl.BlockSpec((tr, 128), lambda i: (jnp.maximum(i - tiles_a, 0), 0)),
                  pl.BlockSpec((tr, tr), fix), pl.BlockSpec((128, 128), fix)],
        out_specs=[k_spec, k_spec, t_spec, t_spec, t_spec],
        out_shape=[sd((t, TOP_K), F32), sd((t, TOP_K), jnp.int32),
                   sd((n_tiles, 1, 128), F32), sd((n_tiles, 1, 128), F32), sd((n_tiles, 1, 128), F32)],
        scratch_shapes=[pltpu.VMEM((1, 128), F32)],
        compiler_params=_cparams("arbitrary"),
        name="topk",
    )(lga, lgb, tri, upper)
    table = lambda a: a[:, 0, :N_EXPERTS].astype(jnp.int32)
    before, cnt, run_loc = table(before), table(cnt), table(loc)
    counts = before[-1] + cnt[-1]
    padded = ((counts + MOE_TILE - 1) // MOE_TILE) * MOE_TILE
    pend = jnp.cumsum(padded)
    pstart = pend - padded
    run_src = pstart[None, :] + before
    nblk = (n_assign + MOE_TILE - 1) // MOE_TILE + N_EXPERTS
    n_used = (pend[-1] // MOE_TILE).astype(jnp.int32)
    block_start = jnp.arange(nblk, dtype=jnp.int32) * MOE_TILE
    block_e = jnp.sum(pend[None, :] <= jnp.minimum(block_start, pend[-1] - 1)[:, None], axis=1)
    block_e = jnp.minimum(block_e, N_EXPERTS - 1).astype(jnp.int32)
    n_valid = jnp.clip((pstart + counts)[block_e] - block_start, 0, MOE_TILE).astype(jnp.int32)
    runs = (run_src.reshape(-1), cnt.reshape(-1), run_loc.reshape(-1))
    per_tile_smem = lambda a: a.reshape(n_tiles, 1, tr * TOP_K)
    return (per_tile_smem(gates), per_tile_smem(off), runs, block_e, n_used.reshape(1), n_valid, pend, padded,
            nblk * MOE_TILE)


def kernel(x_prompt, x_sample, cache_k, cache_v, state_ssm_re, state_ssm_im, norm_mix_g, w_in, q_norm_g, k_norm_g, attn_sinks, ssm_lambda_re, ssm_lambda_im, ssm_b_re, ssm_b_im, ssm_c_re, ssm_c_im, ssm_d, ssm_log_dt, w_glu, b_glu, attn_out_norm_g, ssm_out_norm_g, w_out, norm_ffn_g, w_router, b_router, w_gate_up, b_gate_up, w_down, b_down):
    depth = w_in.shape[0]
    assert depth == 1
    bp, sp, _ = x_prompt.shape
    bs, ss, _ = x_sample.shape
    tp, ts = bp * sp, bs * ss
    assert bp == SUBLANES and sp % S5_CHUNK == 0 and bs % SUBLANES == 0

    perm = jnp.asarray(_Q_PERM)
    w_in0 = w_in[0]
    wq = w_in0[:, :D_ATTN].reshape(D_MODEL, N_HEADS, HEAD_DIM)[:, perm].reshape(D_MODEL, D_ATTN)
    w_in_b = jnp.concatenate([wq, w_in0[:, D_ATTN:]], axis=1).astype(BF16)
    qkg = jnp.concatenate([jnp.tile(q_norm_g[0], N_HEADS), jnp.tile(k_norm_g[0], N_KV_HEADS)])[None]
    pmat = jnp.kron(jnp.eye(256 // HEAD_DIM, dtype=F32),
                    jnp.full((HEAD_DIM, HEAD_DIM), 1.0 / HEAD_DIM, F32)).astype(BF16)
    g_mix = norm_mix_g[0][None]
    sinks = attn_sinks[0]
    g_attn = attn_out_norm_g[0].reshape(N_HEADS, HEAD_DIM)[perm].reshape(1, D_ATTN)
    w_out0 = w_out[0]
    w_out_a = w_out0[:D_ATTN].reshape(N_HEADS, HEAD_DIM, D_MODEL)[perm].reshape(D_ATTN, D_MODEL).astype(BF16)
    w_out_s = w_out0[D_ATTN:].astype(BF16)
    g_ssm = ssm_out_norm_g[0][None]
    g_ffn = norm_ffn_g[0][None]
    w_r = jnp.pad(w_router[0], ((0, 0), (0, 128 - N_EXPERTS))).astype(BF16)
    b_r = jnp.pad(b_router[0], (0, 128 - N_EXPERTS), constant_values=NEG_INF)[None]

    a_re, a_im, bb_re, bb_im = _s5_prep(
        ssm_lambda_re[0], ssm_lambda_im[0], ssm_log_dt[0][:, None],
        jnp.swapaxes(ssm_b_re[0], 1, 2), jnp.swapaxes(ssm_b_im[0], 1, 2))
    eye_g = jnp.eye(N_SSM_GROUPS, dtype=F32)
    bd_b = lambda bb: jnp.einsum("ghp,gk->ghkp", bb, eye_g).reshape(D_SSM, D_STATE)
    wb = jnp.concatenate([bd_b(bb_re), bd_b(bb_im)], axis=1).astype(BF16)
    bd_c = lambda c: jnp.einsum("ghp,gk->gpkh", c, eye_g).reshape(D_STATE, D_SSM)
    wc = jnp.concatenate([bd_c(ssm_c_re[0]), -bd_c(ssm_c_im[0])], axis=0).astype(BF16)
    a_re, a_im = a_re.reshape(1, D_STATE), a_im.reshape(1, D_STATE)
    d_skip = ssm_d[0].reshape(1, D_SSM)
    w_glu_b = w_glu[0].astype(BF16)
    b_glu0 = b_glu[0][None]

    b_g = b_gate_up[0][:, None, 0::2]
    b_l = b_gate_up[0][:, None, 1::2]
    b_d = b_down[0][:, None, :]
    idx = jnp.arange(256)
    deint = (idx[None, :] == jnp.where(idx % 2 == 0, idx // 2, 128 + idx // 2)[:, None]).astype(BF16)

    xp2 = x_prompt.reshape(tp, D_MODEL)
    xs2 = x_sample.reshape(ts, D_MODEL)
    qp, kp, vp, up = _in_proj(xp2, g_mix, w_in_b, qkg, pmat)
    qs, ks, vs, us = _in_proj(xs2, g_mix, w_in_b, qkg, pmat)

    kp3, vp3 = kp.reshape(bp, sp, D_KV), vp.reshape(bp, sp, D_KV)
    attn_p = _band_attention(sinks, qp.reshape(bp, sp, D_ATTN), kp3, vp3, g_attn).reshape(tp, D_ATTN)
    ck = cache_k[0].reshape(bs, -1, D_KV)
    cv = cache_v[0].reshape(bs, -1, D_KV)
    attn_s = _cache_attention(sinks, qs, ks, vs, ck, cv, g_attn, ss)

    zeros_p = jnp.zeros((bp, D_STATE), F32)
    s5_args = (a_re, a_im, wb, wc, d_skip, w_glu_b, b_glu0, g_ssm)
    ssm_p, hr_p, hi_p = _s5(up.reshape(bp, sp, D_SSM), zeros_p, zeros_p, *s5_args,
                            tt=S5_CHUNK, stride=S5_CHUNK + S5_PAD, time_chunked=True)
    ssm_s, hr_s, hi_s = _s5(us.reshape(bs // SUBLANES, SUBLANES * ss, D_SSM),
                            state_ssm_re[0].reshape(bs, D_STATE), state_ssm_im[0].reshape(bs, D_STATE),
                            *s5_args, tt=ss, stride=ss, time_chunked=False)

    x1p, xtp, lgp = _out_proj(attn_p, ssm_p.reshape(tp, D_SSM), xp2, w_out_a, w_out_s, g_ffn, w_r, b_r)
    x1s, xts, lgs = _out_proj(attn_s, ssm_s.reshape(ts, D_SSM), xs2, w_out_a, w_out_s, g_ffn, w_r, b_r)

    gates, off, runs, block_e, n_used, n_valid, pend, padded, n_rows = _route(lgp, lgs)
    x_rows = _dispatch(runs, pend, padded, n_used, off, xtp, xts, n_rows)
    out_rows = _moe(block_e, n_used, n_valid, x_rows, w_gate_up[0], b_g, b_l, w_down[0], b_d, deint)
    yp = _combine(runs, off, gates, out_rows, x1p, 0).reshape(bp, sp, D_MODEL)
    ys = _combine(runs, off, gates, out_rows, x1s, tp // ROUTE_TILE).reshape(bs, ss, D_MODEL)

    kv5 = lambda a, b_: a.reshape(b_, -1, N_KV_HEADS, HEAD_DIM)
    new_kp = kv5(kp3[:, -WINDOW:], bp)[None]
    new_vp = kv5(vp3[:, -WINDOW:], bp)[None]
    ks3, vs3 = ks.reshape(bs, ss, D_KV), vs.reshape(bs, ss, D_KV)
    new_ks = kv5(jnp.concatenate([ck, ks3], axis=1)[:, ss:], bs)[None]
    new_vs = kv5(jnp.concatenate([cv, vs3], axis=1)[:, ss:], bs)[None]
    st = lambda h, b_: h.reshape(1, b_, N_SSM_GROUPS, SSM_STATE)
    return (yp, ys, new_kp, new_vp, st(hr_p, bp), st(hi_p, bp),
            new_ks, new_vs, st(hr_s, bs), st(hi_s, bs))
```

```python
import functools
import math

import jax
import jax.numpy as jnp
from jax import lax
from jax.experimental import pallas as pl
from jax.experimental.pallas import tpu as pltpu

F32 = jnp.float32
BF16 = jnp.bfloat16

D_MODEL = 1024
D_ATTN = 512
D_SSM = 512
HEAD_DIM = 64
N_HEADS = 8
N_KV_HEADS = 2
D_KV = N_KV_HEADS * HEAD_DIM
WINDOW = 128
SSM_GROUP = 16
N_SSM_GROUPS = 32
SSM_STATE = 64
D_STATE = N_SSM_GROUPS * SSM_STATE
N_EXPERTS = 32
TOP_K = 4
D_FF = 1024
SWIGLU_LIMIT = 7.0
SWIGLU_ALPHA = 1.702
RMS_EPS = 1e-6
NEG_INF = -1e30
D_IN_PROJ = D_ATTN + 2 * D_KV + D_SSM
D_QK = D_ATTN + D_KV

SUBLANES = 8
VMEM_LIMIT = 56 * 1024 * 1024

ROW_TILE = 512
MOE_TILE = 512
ROUTE_TILE = 512
_SLOT_ROWS = ROUTE_TILE * TOP_K * SUBLANES
_RUN_SIZES = tuple(1 << b for b in range(ROUTE_TILE.bit_length() - 1, -1, -1))
S5_CHUNK = 128
S5_PAD = 8
S5_COLS = 512

_Q_PERM = (0, 4, 1, 5, 2, 6, 3, 7)


def _cparams(*sem):
    return pltpu.CompilerParams(dimension_semantics=sem, vmem_limit_bytes=VMEM_LIMIT)


def _dot(a, b):
    return jnp.dot(a, b, preferred_element_type=F32)


def _rms(x):
    return x * lax.rsqrt(jnp.mean(x * x, axis=-1, keepdims=True) + RMS_EPS)


def _in_proj_kernel(x_ref, g_ref, w_ref, qkg_ref, p_ref, q_ref, k_ref, v_ref, u_ref):
    xn = _rms(x_ref[...]) * g_ref[...]
    h = _dot(xn.astype(BF16), w_ref[...])
    qk = h[:, :D_QK]
    sq = (qk * qk).astype(BF16)
    p = p_ref[...]
    ms = jnp.concatenate(
        [_dot(sq[:, 0:256], p), _dot(sq[:, 256:512], p), _dot(sq[:, 512:640], p[:128, :128])],
        axis=-1)
    qkn = qk * lax.rsqrt(ms + RMS_EPS) * qkg_ref[...]
    q_ref[...] = qkn[:, :D_ATTN].astype(BF16)
    k_ref[...] = qkn[:, D_ATTN:]
    v_ref[...] = h[:, D_QK:D_QK + D_KV]
    u_ref[...] = h[:, D_QK + D_KV:]


def _in_proj(x2d, g, w, qkg, pmat):
    t = x2d.shape[0]
    tm = min(ROW_TILE, t)
    row = lambda i: (i, 0)
    fix = lambda i: (0, 0)
    return pl.pallas_call(
        _in_proj_kernel,
        grid=(t // tm,),
        in_specs=[pl.BlockSpec((tm, D_MODEL), row), pl.BlockSpec((1, D_MODEL), fix),
                  pl.BlockSpec((D_MODEL, D_IN_PROJ), fix), pl.BlockSpec((1, D_QK), fix),
                  pl.BlockSpec((256, 256), fix)],
        out_specs=[pl.BlockSpec((tm, D_ATTN), row), pl.BlockSpec((tm, D_KV), row),
                   pl.BlockSpec((tm, D_KV), row), pl.BlockSpec((tm, D_SSM), row)],
        out_shape=[jax.ShapeDtypeStruct((t, D_ATTN), BF16), jax.ShapeDtypeStruct((t, D_KV), F32),
                   jax.ShapeDtypeStruct((t, D_KV), F32), jax.ShapeDtypeStruct((t, D_SSM), F32)],
        compiler_params=_cparams("parallel"),
        name="in_proj",
    )(x2d, g, w, qkg, pmat)


def _softmax_pv(s_blocks, v_blocks, sink):
    m = sink
    for s in s_blocks:
        m = jnp.maximum(m, jnp.max(s, axis=-1, keepdims=True))
    den = jnp.exp(sink - m)
    acc = None
    for s, v in zip(s_blocks, v_blocks):
        p = jnp.exp(s - m)
        den = den + jnp.sum(p, axis=-1, keepdims=True)
        pv = _dot(p.astype(BF16), v)
        acc = pv if acc is None else acc + pv
    return acc / den


def _band_attn_kernel(sink_ref, q_ref, kp_ref, kc_ref, vp_ref, vc_ref, g_ref, o_ref):
    i = pl.program_id(1)
    q = q_ref[...]
    kb = jnp.concatenate([kp_ref[...], kc_ref[...]], axis=0).astype(BF16)
    vb = jnp.concatenate([vp_ref[...], vc_ref[...]], axis=0).astype(BF16)
    row = lax.broadcasted_iota(jnp.int32, (WINDOW, 2 * WINDOW), 0)
    col = lax.broadcasted_iota(jnp.int32, (WINDOW, 2 * WINDOW), 1)
    mask = (col > row) & (col <= row + WINDOW) & ((col >= WINDOW) | (i > 0))
    lane = lax.broadcasted_iota(jnp.int32, (WINDOW, 128), 1)
    low = lane < HEAD_DIM
    zero = jnp.zeros((), BF16)
    outs = []
    for pair in range(N_HEADS // 2):
        qp = q[:, 128 * pair:128 * (pair + 1)]
        halves = []
        for par in range(2):
            qm = jnp.where(low if par == 0 else ~low, qp, zero)
            s = lax.dot_general(qm, kb, (((1,), (1,)), ((), ())), preferred_element_type=F32)
            s = jnp.where(mask, s * (HEAD_DIM ** -0.5), NEG_INF)
            halves.append(_softmax_pv([s], [vb], sink_ref[_Q_PERM[2 * pair + par]]))
        outs.append(jnp.where(low, halves[0], halves[1]))
    o = jnp.concatenate(outs, axis=-1)
    o_ref[...] = (_rms(o) * g_ref[...]).astype(BF16)


def _band_attention(sinks, q, k, v, g):
    b, s, _ = q.shape
    nb = s // WINDOW
    cur = lambda bi, i: (bi, i, 0)
    prev = lambda bi, i: (bi, jnp.maximum(i - 1, 0), 0)
    kv_spec = lambda im: pl.BlockSpec((None, WINDOW, D_KV), im)
    return pl.pallas_call(
        _band_attn_kernel,
        grid=(b, nb),
        in_specs=[pl.BlockSpec(memory_space=pltpu.SMEM),
                  pl.BlockSpec((None, WINDOW, D_ATTN), cur),
                  kv_spec(prev), kv_spec(cur), kv_spec(prev), kv_spec(cur),
                  pl.BlockSpec((1, D_ATTN), lambda bi, i: (0, 0))],
        out_specs=pl.BlockSpec((None, WINDOW, D_ATTN), cur),
        out_shape=jax.ShapeDtypeStruct((b, s, D_ATTN), BF16),
        compiler_params=_cparams("parallel", "parallel"),
        name="band_attn",
    )(sinks, q, k, k, v, v, g)


_PAIR_ROWS = 8
_CACHE_BB = 8


def _cache_attn_kernel(sink_ref, q_ref, kn_ref, vn_ref, ck_ref, cv_ref, g_ref, o_ref, *, n_new):
    n_buf = ck_ref.shape[1]
    rows_blk = _CACHE_BB * n_new
    knew = kn_ref[...].astype(BF16)
    vnew = vn_ref[...].astype(BF16)
    lane = lax.broadcasted_iota(jnp.int32, (_PAIR_ROWS, 128), 1)
    low = lane < HEAD_DIM
    zero = jnp.zeros((), BF16)
    n_stack = N_HEADS * _PAIR_ROWS
    r = lax.broadcasted_iota(jnp.int32, (n_stack, 1), 0) % _PAIR_ROWS
    r_seq, r_tok = r // n_new, r % n_new
    colc = lax.broadcasted_iota(jnp.int32, (n_stack, 2 * n_buf), 1)
    c_seq, c_pos = colc // n_buf, colc % n_buf
    mask_c = (c_seq == r_seq) & (c_pos + WINDOW > r_tok + n_buf)
    coln = lax.broadcasted_iota(jnp.int32, (n_stack, rows_blk), 1)
    sink_col = jnp.concatenate(
        [jnp.full((_PAIR_ROWS, 1), sink_ref[_Q_PERM[h]], F32) for h in range(N_HEADS)], axis=0)
    for sp in range(_CACHE_BB // 2):
        q = q_ref[_PAIR_ROWS * sp:_PAIR_ROWS * (sp + 1), :]
        pieces = []
        for pair in range(N_HEADS // 2):
            qp = q[:, 128 * pair:128 * (pair + 1)]
            pieces.append(jnp.where(low, qp, zero))
            pieces.append(jnp.where(low, zero, qp))
        qs = jnp.concatenate(pieces, axis=0)
        kc = jnp.concatenate([ck_ref[2 * sp], ck_ref[2 * sp + 1]], axis=0).astype(BF16)
        vc = jnp.concatenate([cv_ref[2 * sp], cv_ref[2 * sp + 1]], axis=0).astype(BF16)
        nt = (((1,), (1,)), ((), ()))
        s_c = lax.dot_general(qs, kc, nt, preferred_element_type=F32) * (HEAD_DIM ** -0.5)
        s_n = lax.dot_general(qs, knew, nt, preferred_element_type=F32) * (HEAD_DIM ** -0.5)
        s_c = jnp.where(mask_c, s_c, NEG_INF)
        n_seq, n_tok = coln // n_new - 2 * sp, coln % n_new
        mask_n = (n_seq == r_seq) & (n_tok <= r_tok)
        s_n = jnp.where(mask_n, s_n, NEG_INF)
        o = _softmax_pv([s_c, s_n], [vc, vnew], sink_col)
        outs = [jnp.where(low, o[16 * pair:16 * pair + 8], o[16 * pair + 8:16 * pair + 16])
                for pair in range(N_HEADS // 2)]
        oo = jnp.concatenate(outs, axis=-1)
        o_ref[_PAIR_ROWS * sp:_PAIR_ROWS * (sp + 1), :] = (_rms(oo) * g_ref[...]).astype(BF16)


def _cache_attention(sinks, q, k, v, cache_k, cache_v, g, n_new):
    t = q.shape[0]
    nb, n_buf, _ = cache_k.shape
    assert n_new * 2 == _PAIR_ROWS and nb % _CACHE_BB == 0 and n_buf == WINDOW
    rows = _CACHE_BB * n_new
    row = lambda i: (i, 0)
    cache_spec = pl.BlockSpec((_CACHE_BB, n_buf, D_KV), lambda i: (i, 0, 0))
    return pl.pallas_call(
        functools.partial(_cache_attn_kernel, n_new=n_new),
        grid=(nb // _CACHE_BB,),
        in_specs=[pl.BlockSpec(memory_space=pltpu.SMEM),
                  pl.BlockSpec((rows, D_ATTN), row), pl.BlockSpec((rows, D_KV), row),
                  pl.BlockSpec((rows, D_KV), row), cache_spec, cache_spec,
                  pl.BlockSpec((1, D_ATTN), lambda i: (0, 0))],
        out_specs=pl.BlockSpec((rows, D_ATTN), row),
        out_shape=jax.ShapeDtypeStruct((t, D_ATTN), BF16),
        compiler_params=_cparams("parallel"),
        name="cache_attn",
    )(sinks, q, k, v, cache_k, cache_v, g)


def _s5_prep_kernel(lre_ref, lim_ref, ldt_ref, bre_ref, bim_ref, are_ref, aim_ref, bbre_ref, bbim_ref):
    dt = jnp.exp(ldt_ref[...])
    l_re = jnp.minimum(lre_ref[...], -1e-4)
    l_im = lim_ref[...]
    mag = jnp.exp(l_re * dt)
    a_re = mag * jnp.cos(l_im * dt)
    a_im = mag * jnp.sin(l_im * dt)
    den = l_re * l_re + l_im * l_im
    n_re = a_re - 1.0
    z_re = (n_re * l_re + a_im * l_im) / den
    z_im = (a_im * l_re - n_re * l_im) / den
    are_ref[...] = a_re
    aim_ref[...] = a_im
    br, bi = bre_ref[...], bim_ref[...]
    zr, zi = z_re[:, None, :], z_im[:, None, :]
    bbre_ref[...] = zr * br - zi * bi
    bbim_ref[...] = zr * bi + zi * br


def _s5_prep(lam_re, lam_im, log_dt, b_re_t, b_im_t):
    g, p = lam_re.shape
    sd = jax.ShapeDtypeStruct
    return pl.pallas_call(
        _s5_prep_kernel,
        out_shape=[sd((g, p), F32), sd((g, p), F32), sd(b_re_t.shape, F32), sd(b_re_t.shape, F32)],
        name="s5_prep",
    )(lam_re, lam_im, log_dt, b_re_t, b_im_t)


def _s5_kernel(u_ref, h0r_ref, h0i_ref, ar_ref, ai_ref, wb_ref, wc_ref, d_ref, wglu_ref, bglu_ref,
               g_ref, o_ref, hr_ref, hi_ref, bu_ref, hs_ref, *, tt, stride):
    j = pl.program_id(1)
    rows = SUBLANES * tt
    n_tiles = D_STATE // 128

    @pl.when(j == 0)
    def _():
        hs_ref[:, :D_STATE] = h0r_ref[...]
        hs_ref[:, D_STATE:] = h0i_ref[...]

    u = u_ref[...].reshape(rows, D_SSM)
    ub = u.astype(BF16)

    def put(c, val):
        if stride == tt:
            bu_ref[c, 0:rows, :] = val
        else:
            for b in range(SUBLANES):
                bu_ref[c, b * stride:b * stride + tt, :] = val[b * tt:(b + 1) * tt]

    def get(c):
        if stride == tt:
            return bu_ref[c, 0:rows, :]
        return jnp.concatenate([bu_ref[c, b * stride:b * stride + tt, :] for b in range(SUBLANES)], axis=0)

    for n in range(2 * D_STATE // 256):
        band = (n % (D_STATE // 256)) // 2
        res = _dot(ub[:, 128 * band:128 * (band + 1)],
                   wb_ref[128 * band:128 * (band + 1), 256 * n:256 * (n + 1)])
        put(2 * n, res[:, :128])
        put(2 * n + 1, res[:, 128:])

    tiles_per_pass = S5_COLS // 128
    for c0 in range(0, n_tiles, tiles_per_pass):
        tiles = range(c0, c0 + tiles_per_pass)
        a_r = [jnp.broadcast_to(ar_ref[:, 128 * c:128 * (c + 1)], (SUBLANES, 128)) for c in tiles]
        a_i = [jnp.broadcast_to(ai_ref[:, 128 * c:128 * (c + 1)], (SUBLANES, 128)) for c in tiles]

        def step(t, carry, tiles=tiles, a_r=a_r, a_i=a_i):
            at_t = pl.ds(t, SUBLANES, stride=stride)
            out = []
            for k, c in enumerate(tiles):
                h_r, h_i = carry[2 * k], carry[2 * k + 1]
                n_r = a_r[k] * h_r - a_i[k] * h_i + bu_ref[c, at_t, :]
                n_i = a_r[k] * h_i + a_i[k] * h_r + bu_ref[n_tiles + c, at_t, :]
                bu_ref[c, at_t, :] = n_r
                bu_ref[n_tiles + c, at_t, :] = n_i
                out += [n_r, n_i]
            return tuple(out)

        init = []
        for c in tiles:
            init += [hs_ref[:, 128 * c:128 * (c + 1)], hs_ref[:, D_STATE + 128 * c:D_STATE + 128 * (c + 1)]]
        fin = lax.fori_loop(0, tt, step, tuple(init), unroll=min(tt, 8))
        for k, c in enumerate(tiles):
            hs_ref[:, 128 * c:128 * (c + 1)] = fin[2 * k]
            hs_ref[:, D_STATE + 128 * c:D_STATE + 128 * (c + 1)] = fin[2 * k + 1]

    def h_cols(first_tile):
        return jnp.concatenate([get(first_tile + k) for k in range(4)], axis=-1).astype(BF16)

    ys = []
    for m in range(D_SSM // 128):
        y = _dot(h_cols(4 * m), wc_ref[512 * m:512 * (m + 1), 128 * m:128 * (m + 1)])
        y = y + _dot(h_cols(n_tiles + 4 * m),
                     wc_ref[D_STATE + 512 * m:D_STATE + 512 * (m + 1), 128 * m:128 * (m + 1)])
        ys.append(y)
    y = jnp.concatenate(ys, axis=-1) + d_ref[...] * u
    z = _dot(jax.nn.gelu(y).astype(BF16), wglu_ref[...]) + bglu_ref[...]
    s = z[:, :D_SSM] * jax.nn.sigmoid(z[:, D_SSM:])
    o_ref[...] = (_rms(s) * g_ref[...]).astype(BF16).reshape(o_ref.shape)

    @pl.when(j == pl.num_programs(1) - 1)
    def _():
        hr_ref[...] = hs_ref[:, :D_STATE]
        hi_ref[...] = hs_ref[:, D_STATE:]


def _s5(u, h0r, h0i, a_re, a_im, wb, wc, d, wglu, bglu, g, *, tt, stride, time_chunked):
    nbg = h0r.shape[0] // SUBLANES
    if time_chunked:
        nchunks = u.shape[1] // tt
        u_spec = pl.BlockSpec((SUBLANES, tt, D_SSM), lambda gi, j: (gi, j, 0))
    else:
        nchunks = 1
        u_spec = pl.BlockSpec((None, SUBLANES * tt, D_SSM), lambda gi, j: (gi, 0, 0))
    fix = lambda gi, j: (0, 0)
    st_spec = pl.BlockSpec((SUBLANES, D_STATE), lambda gi, j: (gi, 0))
    sd = jax.ShapeDtypeStruct
    return pl.pallas_call(
        functools.partial(_s5_kernel, tt=tt, stride=stride),
        grid=(nbg, nchunks),
        in_specs=[u_spec, st_spec, st_spec,
                  pl.BlockSpec((1, D_STATE), fix), pl.BlockSpec((1, D_STATE), fix),
                  pl.BlockSpec((D_SSM, 2 * D_STATE), fix), pl.BlockSpec((2 * D_STATE, D_SSM), fix),
                  pl.BlockSpec((1, D_SSM), fix), pl.BlockSpec((D_SSM, 2 * D_SSM), fix),
                  pl.BlockSpec((1, 2 * D_SSM), fix), pl.BlockSpec((1, D_SSM), fix)],
        out_specs=[u_spec, st_spec, st_spec],
        out_shape=[sd(u.shape, BF16), sd(h0r.shape, F32), sd(h0r.shape, F32)],
        scratch_shapes=[pltpu.VMEM((2 * D_STATE // 128, SUBLANES * stride, 128), F32),
                        pltpu.VMEM((SUBLANES, 2 * D_STATE), F32)],
        compiler_params=_cparams("parallel", "arbitrary"),
        name="s5",
    )(u, h0r, h0i, a_re, a_im, wb, wc, d, wglu, bglu, g)


def _to_row_tiles(ref, val, first=0):
    rows = val.shape[0]
    for c in range(D_MODEL // 128):
        ref[pl.ds(first * SUBLANES + c, rows, stride=SUBLANES), :] = val[:, 128 * c:128 * (c + 1)]


def _from_row_tiles(ref, rows, lead=(), first=0):
    return jnp.concatenate(
        [ref[(*lead, pl.ds(first * SUBLANES + c, rows, stride=SUBLANES), slice(None))]
         for c in range(D_MODEL // 128)], axis=-1)


def _out_proj_kernel(a_ref, s_ref, x_ref, wa_ref, ws_ref, g_ref, wr_ref, br_ref, x1_ref, xt_ref, lg_ref):
    x1 = x_ref[...] + _dot(a_ref[...], wa_ref[...]) + _dot(s_ref[...], ws_ref[...])
    x1_ref[...] = x1
    xn = _rms(x1) * g_ref[...]
    _to_row_tiles(xt_ref, xn)
    lg_ref[...] = _dot(xn.astype(BF16), wr_ref[...]) + br_ref[...]


def _out_proj(attn_n, ssm_n, x2d, wa, ws, g, wr, br):
    t = x2d.shape[0]
    tm = min(ROW_TILE, t)
    row = lambda i: (i, 0)
    fix = lambda i: (0, 0)
    sd = jax.ShapeDtypeStruct
    return pl.pallas_call(
        _out_proj_kernel,
        grid=(t // tm,),
        in_specs=[pl.BlockSpec((tm, D_ATTN), row), pl.BlockSpec((tm, D_SSM), row),
                  pl.BlockSpec((tm, D_MODEL), row),
                  pl.BlockSpec((D_ATTN, D_MODEL), fix), pl.BlockSpec((D_SSM, D_MODEL), fix),
                  pl.BlockSpec((1, D_MODEL), fix), pl.BlockSpec((D_MODEL, 128), fix),
                  pl.BlockSpec((1, 128), fix)],
        out_specs=[pl.BlockSpec((tm, D_MODEL), row), pl.BlockSpec((tm * SUBLANES, 128), row),
                   pl.BlockSpec((tm, 128), row)],
        out_shape=[sd((t, D_MODEL), F32), sd((t * SUBLANES, 128), F32), sd((t, 128), F32)],
        compiler_params=_cparams("parallel"),
        name="out_proj",
    )(attn_n, ssm_n, x2d, wa, ws, g, wr, br)


def _tile_span(ref, first_row, n_rows, lead=()):
    start = pl.multiple_of(first_row * SUBLANES, SUBLANES)
    return ref.at[(*lead, pl.ds(start, n_rows * SUBLANES), slice(None))]


def _run_copies(tile, src_tbl, cnt_tbl, loc_tbl, make_copy):
    def body(e, c):
        j = tile * N_EXPERTS + e
        cnt, src, loc = cnt_tbl[j], src_tbl[j], loc_tbl[j]
        off = 0
        for size in _RUN_SIZES:
            @pl.when((cnt & size) != 0)
            def _(off=off, size=size):
                make_copy(src + off, loc + off, size).start()
            off = off + (cnt & size)
        return c

    lax.fori_loop(0, N_EXPERTS, body, 0)


def _dispatch_kernel(src_tbl, cnt_tbl, loc_tbl, pend_ref, padded_ref, nu_ref, off_ref, xa_ref, xb_ref, rows_ref,
                     buf, zbuf, sem, zsem, *, tiles_a, nblk):
    i = pl.program_id(0)
    n = pl.num_programs(0)
    slot = i % 2

    def zero_block(start_row):
        return pltpu.make_async_copy(zbuf, _tile_span(rows_ref, start_row, MOE_TILE), zsem)

    @pl.when(i == 0)
    def _():
        zbuf[...] = jnp.zeros_like(zbuf)
        for e in range(N_EXPERTS):
            @pl.when(padded_ref[e] > 0)
            def _(e=e):
                zero_block(pend_ref[e] - MOE_TILE).start()

        def tail_start(j, c):
            zero_block(j * MOE_TILE).start()
            return c

        def tail_wait(j, c):
            zero_block(j * MOE_TILE).wait()
            return c

        lax.fori_loop(nu_ref[0], nblk, tail_start, 0)
        for e in range(N_EXPERTS):
            @pl.when(padded_ref[e] > 0)
            def _(e=e):
                zero_block(pend_ref[e] - MOE_TILE).wait()
        lax.fori_loop(nu_ref[0], nblk, tail_wait, 0)

    def slot_rows(s):
        return buf.at[pl.ds(pl.multiple_of(s * _SLOT_ROWS, _SLOT_ROWS), _SLOT_ROWS), :]

    def slot_done(s):
        return pltpu.make_async_copy(slot_rows(s), slot_rows(s), sem.at[s])

    @pl.when(i >= 2)
    def _():
        slot_done(slot).wait()

    def fill(x_ref):
        def body(t, c):
            v = x_ref[pl.ds(pl.multiple_of(t * SUBLANES, SUBLANES), SUBLANES), :]
            for k in range(TOP_K):
                off = pl.multiple_of(off_ref[0, t * TOP_K + k], SUBLANES)
                buf[pl.ds(off, SUBLANES), :] = v
            return c

        lax.fori_loop(0, ROUTE_TILE, body, 0, unroll=8)

    @pl.when(i < tiles_a)
    def _():
        fill(xa_ref)

    @pl.when(i >= tiles_a)
    def _():
        fill(xb_ref)

    _run_copies(i, src_tbl, cnt_tbl, loc_tbl,
                lambda g, l, size: pltpu.make_async_copy(_tile_span(slot_rows(slot), l, size),
                                                         _tile_span(rows_ref, g, size), sem.at[slot]))

    @pl.when(i == n - 1)
    def _():
        slot_done(1 - slot).wait()
        slot_done(slot).wait()


def _dispatch(runs, pend, padded, n_used, off, xa, xb, n_rows):
    tile_rows = ROUTE_TILE * SUBLANES
    tiles_a, tiles_b = xa.shape[0] // tile_rows, xb.shape[0] // tile_rows
    assert tiles_a + tiles_b >= 2
    grid_spec = pltpu.PrefetchScalarGridSpec(
        num_scalar_prefetch=6, grid=(tiles_a + tiles_b,),
        in_specs=[pl.BlockSpec((None, 1, ROUTE_TILE * TOP_K), lambda i, *_: (i, 0, 0), memory_space=pltpu.SMEM),
                  pl.BlockSpec((tile_rows, 128), lambda i, *_: (jnp.minimum(i, tiles_a - 1), 0)),
                  pl.BlockSpec((tile_rows, 128), lambda i, *_: (jnp.maximum(i - tiles_a, 0), 0))],
        out_specs=pl.BlockSpec(memory_space=pl.ANY),
        scratch_shapes=[pltpu.VMEM((2 * _SLOT_ROWS, 128), F32),
                        pltpu.VMEM((MOE_TILE * SUBLANES, 128), F32),
                        pltpu.SemaphoreType.DMA((2,)), pltpu.SemaphoreType.DMA(())])
    return pl.pallas_call(
        functools.partial(_dispatch_kernel, tiles_a=tiles_a, nblk=n_rows // MOE_TILE), grid_spec=grid_spec,
        out_shape=jax.ShapeDtypeStruct((n_rows * SUBLANES, 128), F32),
        compiler_params=_cparams("arbitrary"), name="dispatch",
    )(*runs, pend, padded, n_used, off, xa, xb)


def _moe_kernel(be_ref, nu_ref, nv_ref, x_ref, wgu_ref, bg_ref, bl_ref, wd_ref, bd_ref, perm_ref, o_ref,
                wg_s, wl_s, wd_s):
    i = pl.program_id(0)
    used = i < nu_ref[0]
    new_expert = (i == 0) | (be_ref[i] != be_ref[jnp.maximum(i - 1, 0)])

    @pl.when(used & new_expert)
    def _():
        for c in range(2 * D_FF // 256):
            r = _dot(wgu_ref[:, 256 * c:256 * (c + 1)].astype(BF16), perm_ref[...])
            wg_s[:, 128 * c:128 * (c + 1)] = r[:, :128].astype(BF16)
            wl_s[:, 128 * c:128 * (c + 1)] = r[:, 128:].astype(BF16)
        wd_s[...] = wd_ref[...].astype(BF16)

    def expert_rows(rows):
        x = _from_row_tiles(x_ref, rows).astype(BF16)
        glu = jnp.minimum(_dot(x, wg_s[...]) + bg_ref[...], SWIGLU_LIMIT)
        lin = jnp.clip(_dot(x, wl_s[...]) + bl_ref[...], -SWIGLU_LIMIT, SWIGLU_LIMIT)
        act = glu * jax.nn.sigmoid(SWIGLU_ALPHA * glu) * (lin + 1.0)
        _to_row_tiles(o_ref, _dot(act.astype(BF16), wd_s[...]) + bd_ref[...])

    def zero_rows(first, rows):
        o_ref[pl.ds(first * SUBLANES, rows * SUBLANES), :] = jnp.zeros((rows * SUBLANES, 128), F32)

    half = MOE_TILE // 2

    @pl.when(used & (nv_ref[i] > half))
    def _():
        expert_rows(MOE_TILE)

    @pl.when(used & (nv_ref[i] <= half))
    def _():
        expert_rows(half)
        zero_rows(half, half)

    @pl.when(jnp.logical_not(used))
    def _():
        zero_rows(0, MOE_TILE)


def _moe(block_e, n_used, n_valid, x_rows, wgu, bg, bl, wd, bd, perm):
    nblk = x_rows.shape[0] // (MOE_TILE * SUBLANES)
    row = lambda i, be, nu, nv: (jnp.minimum(i, nu[0] - 1), 0)
    wsel = lambda i, be, nu, nv: (be[i], 0, 0)
    grid_spec = pltpu.PrefetchScalarGridSpec(
        num_scalar_prefetch=3,
        grid=(nblk,),
        in_specs=[pl.BlockSpec((MOE_TILE * SUBLANES, 128), row),
                  pl.BlockSpec((None, D_MODEL, 2 * D_FF), wsel),
                  pl.BlockSpec((None, 1, D_FF), wsel), pl.BlockSpec((None, 1, D_FF), wsel),
                  pl.BlockSpec((None, D_FF, D_MODEL), wsel), pl.BlockSpec((None, 1, D_MODEL), wsel),
                  pl.BlockSpec((256, 256), lambda i, *_: (0, 0))],
        out_specs=pl.BlockSpec((MOE_TILE * SUBLANES, 128), lambda i, *_: (i, 0)),
        scratch_shapes=[pltpu.VMEM((D_MODEL, D_FF), BF16), pltpu.VMEM((D_MODEL, D_FF), BF16),
                        pltpu.VMEM((D_FF, D_MODEL), BF16)],
    )
    return pl.pallas_call(
        _moe_kernel,
        grid_spec=grid_spec,
        out_shape=jax.ShapeDtypeStruct(x_rows.shape, F32),
        compiler_params=_cparams("arbitrary"),
        name="moe",
    )(block_e, n_used, n_valid, x_rows, wgu, bg, bl, wd, bd, perm)


def _combine_kernel(src_tbl, cnt_tbl, loc_tbl, off_ref, gate_ref, rows_hbm, x1_ref, o_ref, buf, ybuf, sem, *,
                    tile_base):
    i = pl.program_id(0)
    n = pl.num_programs(0)

    def slot_rows(s):
        return buf.at[pl.ds(pl.multiple_of(s * _SLOT_ROWS, _SLOT_ROWS), _SLOT_ROWS), :]

    def fetch(tile, s):
        _run_copies(tile_base + tile, src_tbl, cnt_tbl, loc_tbl,
                    lambda g, l, size: pltpu.make_async_copy(_tile_span(rows_hbm, g, size),
                                                             _tile_span(slot_rows(s), l, size), sem.at[s]))

    @pl.when(i == 0)
    def _():
        fetch(0, 0)

    @pl.when(i + 1 < n)
    def _():
        fetch(i + 1, (i + 1) % 2)

    slot = i % 2
    pltpu.make_async_copy(slot_rows(slot), slot_rows(slot), sem.at[slot]).wait()

    def body(t, c):
        acc = None
        for k in range(TOP_K):
            off = pl.multiple_of(off_ref[0, t * TOP_K + k], SUBLANES)
            v = gate_ref[0, t * TOP_K + k] * buf[pl.ds(off, SUBLANES), :]
            acc = v if acc is None else acc + v
        ybuf[pl.ds(pl.multiple_of(t * SUBLANES, SUBLANES), SUBLANES), :] = acc
        return c

    lax.fori_loop(0, ROUTE_TILE, body, 0, unroll=8)
    o_ref[...] = x1_ref[...] + _from_row_tiles(ybuf, ROUTE_TILE)


def _combine(runs, off, gates, rows, x1, tile_base):
    assert tile_base % 2 == 0
    t = x1.shape[0]
    tile_rows = ROUTE_TILE * SUBLANES
    smem_tile = pl.BlockSpec((None, 1, ROUTE_TILE * TOP_K), lambda i, *_: (tile_base + i, 0, 0),
                             memory_space=pltpu.SMEM)
    grid_spec = pltpu.PrefetchScalarGridSpec(
        num_scalar_prefetch=3,
        grid=(t // ROUTE_TILE,),
        in_specs=[smem_tile, smem_tile, pl.BlockSpec(memory_space=pl.ANY),
                  pl.BlockSpec((ROUTE_TILE, D_MODEL), lambda i, *_: (i, 0))],
        out_specs=pl.BlockSpec((ROUTE_TILE, D_MODEL), lambda i, *_: (i, 0)),
        scratch_shapes=[pltpu.VMEM((2 * _SLOT_ROWS, 128), F32), pltpu.VMEM((tile_rows, 128), F32),
                        pltpu.SemaphoreType.DMA((2,))],
    )
    return pl.pallas_call(
        functools.partial(_combine_kernel, tile_base=tile_base),
        grid_spec=grid_spec,
        out_shape=jax.ShapeDtypeStruct((t, D_MODEL), F32),
        compiler_params=_cparams("arbitrary"),
        name="combine",
    )(*runs, off, gates, rows, x1)


def _lane_pick(cols):
    lane = lax.broadcasted_iota(jnp.int32, (cols[0].shape[0], len(cols)), 1)
    out = cols[-1]
    for k in range(len(cols) - 2, -1, -1):
        out = jnp.where(lane == k, cols[k], out)
    return out


def _topk_kernel(lga_ref, lgb_ref, tri_ref, upper_ref, gate_ref, off_ref, before_ref, cnt_ref, loc_ref, carry, *,
                 tiles_a):
    i = pl.program_id(0)

    @pl.when(i == 0)
    def _():
        carry[...] = jnp.zeros_like(carry)

    l = jnp.where(i < tiles_a, lga_ref[...], lgb_ref[...])
    lane = lax.broadcasted_iota(jnp.int32, l.shape, 1)
    vals, sels = [], []
    for _ in range(TOP_K):
        m = jnp.max(l, axis=-1, keepdims=True)
        idx = jnp.min(jnp.where(l == m, lane, l.shape[1]), axis=-1, keepdims=True)
        sel = lane == idx
        l = jnp.where(sel, -jnp.inf, l)
        vals.append(m)
        sels.append(sel)
    exps = [jnp.exp(v - vals[0]) for v in vals]
    den = exps[0] + exps[1] + exps[2] + exps[3]
    onehot = jnp.where(sels[0] | sels[1] | sels[2] | sels[3], 1.0, 0.0)
    within = _dot(tri_ref[...], onehot.astype(BF16))
    cnt = jnp.sum(onehot, axis=0, keepdims=True)
    cnt_hi = jnp.floor(cnt * (1.0 / 16.0))
    cnt_lo = cnt - 16.0 * cnt_hi
    loc = 16.0 * _dot(cnt_hi.astype(BF16), upper_ref[...]) + _dot(cnt_lo.astype(BF16), upper_ref[...])
    slot_base = ((i % 2) * (ROUTE_TILE * TOP_K)).astype(F32)
    rows = [jnp.sum(jnp.where(s, within + loc, 0.0), axis=-1, keepdims=True) for s in sels]
    gate_ref[...] = _lane_pick([e / den for e in exps])
    off_ref[...] = ((_lane_pick(rows) + slot_base) * SUBLANES).astype(jnp.int32)
    before_ref[...] = carry[...]
    cnt_ref[...] = cnt
    loc_ref[...] = loc
    carry[...] = carry[...] + cnt


def _route(lga, lgb):
    tr = ROUTE_TILE
    tiles_a, tiles_b = lga.shape[0] // tr, lgb.shape[0] // tr
    n_tiles = tiles_a + tiles_b
    t = n_tiles * tr
    n_assign = t * TOP_K
    row = lambda i: (i, 0)
    fix = lambda i: (0, 0)
    sd = jax.ShapeDtypeStruct
    k_spec = pl.BlockSpec((tr, TOP_K), row)
    t_spec = pl.BlockSpec((None, 1, 128), lambda i: (i, 0, 0))
    tri = (jnp.arange(tr)[:, None] > jnp.arange(tr)[None, :]).astype(BF16)
    upper = (jnp.arange(128)[:, None] < jnp.arange(128)[None, :]).astype(BF16)
    gates, off, before, cnt, loc = pl.pallas_call(
        functools.partial(_topk_kernel, tiles_a=tiles_a),
        grid=(n_tiles,),
        in_specs=[pl.BlockSpec((tr, 128), lambda i: (jnp.minimum(i, tiles_a - 1), 0)),
                  pl.BlockSpec((tr, 128), lambda i: (jnp.maximum(i - tiles_a, 0), 0)),
                  pl.BlockSpec((tr, tr), fix), pl.BlockSpec((128, 128), fix)],
        out_specs=[k_spec, k_spec, t_spec, t_spec, t_spec],
        out_shape=[sd((t, TOP_K), F32), sd((t, TOP_K), jnp.int32),
                   sd((n_tiles, 1, 128), F32), sd((n_tiles, 1, 128), F32), sd((n_tiles, 1, 128), F32)],
        scratch_shapes=[pltpu.VMEM((1, 128), F32)],
        compiler_params=_cparams("arbitrary"),
        name="topk",
    )(lga, lgb, tri, upper)
    table = lambda a: a[:, 0, :N_EXPERTS].astype(jnp.int32)
    before, cnt, run_loc = table(before), table(cnt), table(loc)
    counts = before[-1] + cnt[-1]
    padded = ((counts + MOE_TILE - 1) // MOE_TILE) * MOE_TILE
    pend = jnp.cumsum(padded)
    pstart = pend - padded
    run_src = pstart[None, :] + before
    nblk = (n_assign + MOE_TILE - 1) // MOE_TILE + N_EXPERTS
    n_used = (pend[-1] // MOE_TILE).astype(jnp.int32)
    block_start = jnp.arange(nblk, dtype=jnp.int32) * MOE_TILE
    block_e = jnp.sum(pend[None, :] <= jnp.minimum(block_start, pend[-1] - 1)[:, None], axis=1)
    block_e = jnp.minimum(block_e, N_EXPERTS - 1).astype(jnp.int32)
    region_end = jnp.sum(jnp.where(block_e[:, None] == jnp.arange(N_EXPERTS)[None, :], (pstart + counts)[None, :], 0),
                         axis=1)
    n_valid = jnp.clip(region_end - block_start, 0, MOE_TILE).astype(jnp.int32)
    runs = (run_src.reshape(-1), cnt.reshape(-1), run_loc.reshape(-1))
    per_tile_smem = lambda a: a.reshape(n_tiles, 1, tr * TOP_K)
    return (per_tile_smem(gates), per_tile_smem(off), runs, block_e, n_used.reshape(1), n_valid, pend, padded,
            nblk * MOE_TILE)


def kernel(x_prompt, x_sample, cache_k, cache_v, state_ssm_re, state_ssm_im, norm_mix_g, w_in, q_norm_g, k_norm_g, attn_sinks, ssm_lambda_re, ssm_lambda_im, ssm_b_re, ssm_b_im, ssm_c_re, ssm_c_im, ssm_d, ssm_log_dt, w_glu, b_glu, attn_out_norm_g, ssm_out_norm_g, w_out, norm_ffn_g, w_router, b_router, w_gate_up, b_gate_up, w_down, b_down):
    depth = w_in.shape[0]
    assert depth == 1
    bp, sp, _ = x_prompt.shape
    bs, ss, _ = x_sample.shape
    tp, ts = bp * sp, bs * ss
    assert bp == SUBLANES and sp % S5_CHUNK == 0 and bs % SUBLANES == 0

    perm = jnp.asarray(_Q_PERM)
    w_in0 = w_in[0]
    wq = w_in0[:, :D_ATTN].reshape(D_MODEL, N_HEADS, HEAD_DIM)[:, perm].reshape(D_MODEL, D_ATTN)
    w_in_b = jnp.concatenate([wq, w_in0[:, D_ATTN:]], axis=1).astype(BF16)
    qkg = jnp.concatenate([jnp.tile(q_norm_g[0], N_HEADS), jnp.tile(k_norm_g[0], N_KV_HEADS)])[None]
    pmat = jnp.kron(jnp.eye(256 // HEAD_DIM, dtype=F32),
                    jnp.full((HEAD_DIM, HEAD_DIM), 1.0 / HEAD_DIM, F32)).astype(BF16)
    g_mix = norm_mix_g[0][None]
    sinks = attn_sinks[0]
    g_attn = attn_out_norm_g[0].reshape(N_HEADS, HEAD_DIM)[perm].reshape(1, D_ATTN)
    w_out0 = w_out[0]
    w_out_a = w_out0[:D_ATTN].reshape(N_HEADS, HEAD_DIM, D_MODEL)[perm].reshape(D_ATTN, D_MODEL).astype(BF16)
    w_out_s = w_out0[D_ATTN:].astype(BF16)
    g_ssm = ssm_out_norm_g[0][None]
    g_ffn = norm_ffn_g[0][None]
    w_r = jnp.pad(w_router[0], ((0, 0), (0, 128 - N_EXPERTS))).astype(BF16)
    b_r = jnp.pad(b_router[0], (0, 128 - N_EXPERTS), constant_values=NEG_INF)[None]

    a_re, a_im, bb_re, bb_im = _s5_prep(
        ssm_lambda_re[0], ssm_lambda_im[0], ssm_log_dt[0][:, None],
        jnp.swapaxes(ssm_b_re[0], 1, 2), jnp.swapaxes(ssm_b_im[0], 1, 2))
    eye_g = jnp.eye(N_SSM_GROUPS, dtype=F32)
    bd_b = lambda bb: jnp.einsum("ghp,gk->ghkp", bb, eye_g).reshape(D_SSM, D_STATE)
    wb = jnp.concatenate([bd_b(bb_re), bd_b(bb_im)], axis=1).astype(BF16)
    bd_c = lambda c: jnp.einsum("ghp,gk->gpkh", c, eye_g).reshape(D_STATE, D_SSM)
    wc = jnp.concatenate([bd_c(ssm_c_re[0]), -bd_c(ssm_c_im[0])], axis=0).astype(BF16)
    a_re, a_im = a_re.reshape(1, D_STATE), a_im.reshape(1, D_STATE)
    d_skip = ssm_d[0].reshape(1, D_SSM)
    w_glu_b = w_glu[0].astype(BF16)
    b_glu0 = b_glu[0][None]

    b_g = b_gate_up[0][:, None, 0::2]
    b_l = b_gate_up[0][:, None, 1::2]
    b_d = b_down[0][:, None, :]
    idx = jnp.arange(256)
    deint = (idx[None, :] == jnp.where(idx % 2 == 0, idx // 2, 128 + idx // 2)[:, None]).astype(BF16)

    xp2 = x_prompt.reshape(tp, D_MODEL)
    xs2 = x_sample.reshape(ts, D_MODEL)
    qp, kp, vp, up = _in_proj(xp2, g_mix, w_in_b, qkg, pmat)
    qs, ks, vs, us = _in_proj(xs2, g_mix, w_in_b, qkg, pmat)

    kp3, vp3 = kp.reshape(bp, sp, D_KV), vp.reshape(bp, sp, D_KV)
    attn_p = _band_attention(sinks, qp.reshape(bp, sp, D_ATTN), kp3, vp3, g_attn).reshape(tp, D_ATTN)
    ck = cache_k[0].reshape(bs, -1, D_KV)
    cv = cache_v[0].reshape(bs, -1, D_KV)
    attn_s = _cache_attention(sinks, qs, ks, vs, ck, cv, g_attn, ss)

    zeros_p = jnp.zeros((bp, D_STATE), F32)
    s5_args = (a_re, a_im, wb, wc, d_skip, w_glu_b, b_glu0, g_ssm)
    ssm_p, hr_p, hi_p = _s5(up.reshape(bp, sp, D_SSM), zeros_p, zeros_p, *s5_args,
                            tt=S5_CHUNK, stride=S5_CHUNK + S5_PAD, time_chunked=True)
    ssm_s, hr_s, hi_s = _s5(us.reshape(bs // SUBLANES, SUBLANES * ss, D_SSM),
                            state_ssm_re[0].reshape(bs, D_STATE), state_ssm_im[0].reshape(bs, D_STATE),
                            *s5_args, tt=ss, stride=ss, time_chunked=False)

    x1p, xtp, lgp = _out_proj(attn_p, ssm_p.reshape(tp, D_SSM), xp2, w_out_a, w_out_s, g_ffn, w_r, b_r)
    x1s, xts, lgs = _out_proj(attn_s, ssm_s.reshape(ts, D_SSM), xs2, w_out_a, w_out_s, g_ffn, w_r, b_r)

    gates, off, runs, block_e, n_used, n_valid, pend, padded, n_rows = _route(lgp, lgs)
    x_rows = _dispatch(runs, pend, padded, n_used, off, xtp, xts, n_rows)
    out_rows = _moe(block_e, n_used, n_valid, x_rows, w_gate_up[0], b_g, b_l, w_down[0], b_d, deint)
    yp = _combine(runs, off, gates, out_rows, x1p, 0).reshape(bp, sp, D_MODEL)
    ys = _combine(runs, off, gates, out_rows, x1s, tp // ROUTE_TILE).reshape(bs, ss, D_MODEL)

    kv5 = lambda a, b_: a.reshape(b_, -1, N_KV_HEADS, HEAD_DIM)
    new_kp = kv5(kp3[:, -WINDOW:], bp)[None]
    new_vp = kv5(vp3[:, -WINDOW:], bp)[None]
    ks3, vs3 = ks.reshape(bs, ss, D_KV), vs.reshape(bs, ss, D_KV)
    new_ks = kv5(jnp.concatenate([ck, ks3], axis=1)[:, ss:], bs)[None]
    new_vs = kv5(jnp.concatenate([cv, vs3], axis=1)[:, ss:], bs)[None]
    st = lambda h, b_: h.reshape(1, b_, N_SSM_GROUPS, SSM_STATE)
    return (yp, ys, new_kp, new_vp, st(hr_p, bp), st(hi_p, bp),
            new_ks, new_vs, st(hr_s, bs), st(hi_s, bs))
```

```python
import functools
import math

import jax
import jax.numpy as jnp
from jax import lax
from jax.experimental import pallas as pl
from jax.experimental.pallas import tpu as pltpu

F32 = jnp.float32
BF16 = jnp.bfloat16

D_MODEL = 1024
D_ATTN = 512
D_SSM = 512
HEAD_DIM = 64
N_HEADS = 8
N_KV_HEADS = 2
D_KV = N_KV_HEADS * HEAD_DIM
WINDOW = 128
SSM_GROUP = 16
N_SSM_GROUPS = 32
SSM_STATE = 64
D_STATE = N_SSM_GROUPS * SSM_STATE
N_EXPERTS = 32
TOP_K = 4
D_FF = 1024
SWIGLU_LIMIT = 7.0
SWIGLU_ALPHA = 1.702
RMS_EPS = 1e-6
NEG_INF = -1e30
D_IN_PROJ = D_ATTN + 2 * D_KV + D_SSM
D_QK = D_ATTN + D_KV

SUBLANES = 8
VMEM_LIMIT = 56 * 1024 * 1024

ROW_TILE = 512
MOE_TILE = 512
ROUTE_TILE = 512
_SLOT_ROWS = ROUTE_TILE * TOP_K * SUBLANES
_RUN_SIZES = tuple(1 << b for b in range(ROUTE_TILE.bit_length() - 1, -1, -1))
S5_CHUNK = 128
S5_PAD = 8
S5_COLS = 512

_Q_PERM = (0, 4, 1, 5, 2, 6, 3, 7)


def _cparams(*sem):
    return pltpu.CompilerParams(dimension_semantics=sem, vmem_limit_bytes=VMEM_LIMIT)


def _dot(a, b):
    return jnp.dot(a, b, preferred_element_type=F32)


def _rms(x):
    return x * lax.rsqrt(jnp.mean(x * x, axis=-1, keepdims=True) + RMS_EPS)


def _in_proj_kernel(x_ref, g_ref, w_ref, qkg_ref, p_ref, q_ref, k_ref, v_ref, u_ref):
    xn = _rms(x_ref[...]) * g_ref[...]
    h = _dot(xn.astype(BF16), w_ref[...])
    qk = h[:, :D_QK]
    sq = (qk * qk).astype(BF16)
    p = p_ref[...]
    ms = jnp.concatenate(
        [_dot(sq[:, 0:256], p), _dot(sq[:, 256:512], p), _dot(sq[:, 512:640], p[:128, :128])],
        axis=-1)
    qkn = qk * lax.rsqrt(ms + RMS_EPS) * qkg_ref[...]
    q_ref[...] = qkn[:, :D_ATTN].astype(BF16)
    k_ref[...] = qkn[:, D_ATTN:]
    v_ref[...] = h[:, D_QK:D_QK + D_KV]
    u_ref[...] = h[:, D_QK + D_KV:]


def _in_proj(x2d, g, w, qkg, pmat):
    t = x2d.shape[0]
    tm = min(ROW_TILE, t)
    row = lambda i: (i, 0)
    fix = lambda i: (0, 0)
    return pl.pallas_call(
        _in_proj_kernel,
        grid=(t // tm,),
        in_specs=[pl.BlockSpec((tm, D_MODEL), row), pl.BlockSpec((1, D_MODEL), fix),
                  pl.BlockSpec((D_MODEL, D_IN_PROJ), fix), pl.BlockSpec((1, D_QK), fix),
                  pl.BlockSpec((256, 256), fix)],
        out_specs=[pl.BlockSpec((tm, D_ATTN), row), pl.BlockSpec((tm, D_KV), row),
                   pl.BlockSpec((tm, D_KV), row), pl.BlockSpec((tm, D_SSM), row)],
        out_shape=[jax.ShapeDtypeStruct((t, D_ATTN), BF16), jax.ShapeDtypeStruct((t, D_KV), F32),
                   jax.ShapeDtypeStruct((t, D_KV), F32), jax.ShapeDtypeStruct((t, D_SSM), F32)],
        compiler_params=_cparams("parallel"),
        name="in_proj",
    )(x2d, g, w, qkg, pmat)


def _softmax_pv(s_blocks, v_blocks, sink):
    m = sink
    for s in s_blocks:
        m = jnp.maximum(m, jnp.max(s, axis=-1, keepdims=True))
    den = jnp.exp(sink - m)
    acc = None
    for s, v in zip(s_blocks, v_blocks):
        p = jnp.exp(s - m)
        den = den + jnp.sum(p, axis=-1, keepdims=True)
        pv = _dot(p.astype(BF16), v)
        acc = pv if acc is None else acc + pv
    return acc / den


def _band_attn_kernel(sink_ref, q_ref, kp_ref, kc_ref, vp_ref, vc_ref, g_ref, o_ref):
    i = pl.program_id(1)
    q = q_ref[...]
    kb = jnp.concatenate([kp_ref[...], kc_ref[...]], axis=0).astype(BF16)
    vb = jnp.concatenate([vp_ref[...], vc_ref[...]], axis=0).astype(BF16)
    row = lax.broadcasted_iota(jnp.int32, (WINDOW, 2 * WINDOW), 0)
    col = lax.broadcasted_iota(jnp.int32, (WINDOW, 2 * WINDOW), 1)
    mask = (col > row) & (col <= row + WINDOW) & ((col >= WINDOW) | (i > 0))
    lane = lax.broadcasted_iota(jnp.int32, (WINDOW, 128), 1)
    low = lane < HEAD_DIM
    zero = jnp.zeros((), BF16)
    outs = []
    for pair in range(N_HEADS // 2):
        qp = q[:, 128 * pair:128 * (pair + 1)]
        halves = []
        for par in range(2):
            qm = jnp.where(low if par == 0 else ~low, qp, zero)
            s = lax.dot_general(qm, kb, (((1,), (1,)), ((), ())), preferred_element_type=F32)
            s = jnp.where(mask, s * (HEAD_DIM ** -0.5), NEG_INF)
            halves.append(_softmax_pv([s], [vb], sink_ref[_Q_PERM[2 * pair + par]]))
        outs.append(jnp.where(low, halves[0], halves[1]))
    o = jnp.concatenate(outs, axis=-1)
    o_ref[...] = (_rms(o) * g_ref[...]).astype(BF16)


def _band_attention(sinks, q, k, v, g):
    b, s, _ = q.shape
    nb = s // WINDOW
    cur = lambda bi, i: (bi, i, 0)
    prev = lambda bi, i: (bi, jnp.maximum(i - 1, 0), 0)
    kv_spec = lambda im: pl.BlockSpec((None, WINDOW, D_KV), im)
    return pl.pallas_call(
        _band_attn_kernel,
        grid=(b, nb),
        in_specs=[pl.BlockSpec(memory_space=pltpu.SMEM),
                  pl.BlockSpec((None, WINDOW, D_ATTN), cur),
                  kv_spec(prev), kv_spec(cur), kv_spec(prev), kv_spec(cur),
                  pl.BlockSpec((1, D_ATTN), lambda bi, i: (0, 0))],
        out_specs=pl.BlockSpec((None, WINDOW, D_ATTN), cur),
        out_shape=jax.ShapeDtypeStruct((b, s, D_ATTN), BF16),
        compiler_params=_cparams("parallel", "parallel"),
        name="band_attn",
    )(sinks, q, k, k, v, v, g)


_PAIR_ROWS = 8
_CACHE_BB = 8


def _cache_attn_kernel(sink_ref, q_ref, kn_ref, vn_ref, ck_ref, cv_ref, g_ref, o_ref, *, n_new):
    n_buf = ck_ref.shape[1]
    rows_blk = _CACHE_BB * n_new
    knew = kn_ref[...].astype(BF16)
    vnew = vn_ref[...].astype(BF16)
    lane = lax.broadcasted_iota(jnp.int32, (_PAIR_ROWS, 128), 1)
    low = lane < HEAD_DIM
    zero = jnp.zeros((), BF16)
    n_stack = N_HEADS * _PAIR_ROWS
    r = lax.broadcasted_iota(jnp.int32, (n_stack, 1), 0) % _PAIR_ROWS
    r_seq, r_tok = r // n_new, r % n_new
    colc = lax.broadcasted_iota(jnp.int32, (n_stack, 2 * n_buf), 1)
    c_seq, c_pos = colc // n_buf, colc % n_buf
    mask_c = (c_seq == r_seq) & (c_pos + WINDOW > r_tok + n_buf)
    coln = lax.broadcasted_iota(jnp.int32, (n_stack, rows_blk), 1)
    sink_col = jnp.concatenate(
        [jnp.full((_PAIR_ROWS, 1), sink_ref[_Q_PERM[h]], F32) for h in range(N_HEADS)], axis=0)
    for sp in range(_CACHE_BB // 2):
        q = q_ref[_PAIR_ROWS * sp:_PAIR_ROWS * (sp + 1), :]
        pieces = []
        for pair in range(N_HEADS // 2):
            qp = q[:, 128 * pair:128 * (pair + 1)]
            pieces.append(jnp.where(low, qp, zero))
            pieces.append(jnp.where(low, zero, qp))
        qs = jnp.concatenate(pieces, axis=0)
        kc = jnp.concatenate([ck_ref[2 * sp], ck_ref[2 * sp + 1]], axis=0).astype(BF16)
        vc = jnp.concatenate([cv_ref[2 * sp], cv_ref[2 * sp + 1]], axis=0).astype(BF16)
        nt = (((1,), (1,)), ((), ()))
        s_c = lax.dot_general(qs, kc, nt, preferred_element_type=F32) * (HEAD_DIM ** -0.5)
        s_n = lax.dot_general(qs, knew, nt, preferred_element_type=F32) * (HEAD_DIM ** -0.5)
        s_c = jnp.where(mask_c, s_c, NEG_INF)
        n_seq, n_tok = coln // n_new - 2 * sp, coln % n_new
        mask_n = (n_seq == r_seq) & (n_tok <= r_tok)
        s_n = jnp.where(mask_n, s_n, NEG_INF)
        o = _softmax_pv([s_c, s_n], [vc, vnew], sink_col)
        outs = [jnp.where(low, o[16 * pair:16 * pair + 8], o[16 * pair + 8:16 * pair + 16])
                for pair in range(N_HEADS // 2)]
        oo = jnp.concatenate(outs, axis=-1)
        o_ref[_PAIR_ROWS * sp:_PAIR_ROWS * (sp + 1), :] = (_rms(oo) * g_ref[...]).astype(BF16)


def _cache_attention(sinks, q, k, v, cache_k, cache_v, g, n_new):
    t = q.shape[0]
    nb, n_buf, _ = cache_k.shape
    assert n_new * 2 == _PAIR_ROWS and nb % _CACHE_BB == 0 and n_buf == WINDOW
    rows = _CACHE_BB * n_new
    row = lambda i: (i, 0)
    cache_spec = pl.BlockSpec((_CACHE_BB, n_buf, D_KV), lambda i: (i, 0, 0))
    return pl.pallas_call(
        functools.partial(_cache_attn_kernel, n_new=n_new),
        grid=(nb // _CACHE_BB,),
        in_specs=[pl.BlockSpec(memory_space=pltpu.SMEM),
                  pl.BlockSpec((rows, D_ATTN), row), pl.BlockSpec((rows, D_KV), row),
                  pl.BlockSpec((rows, D_KV), row), cache_spec, cache_spec,
                  pl.BlockSpec((1, D_ATTN), lambda i: (0, 0))],
        out_specs=pl.BlockSpec((rows, D_ATTN), row),
        out_shape=jax.ShapeDtypeStruct((t, D_ATTN), BF16),
        compiler_params=_cparams("parallel"),
        name="cache_attn",
    )(sinks, q, k, v, cache_k, cache_v, g)


def _s5_prep_kernel(lre_ref, lim_ref, ldt_ref, bre_ref, bim_ref, are_ref, aim_ref, bbre_ref, bbim_ref):
    dt = jnp.exp(ldt_ref[...])
    l_re = jnp.minimum(lre_ref[...], -1e-4)
    l_im = lim_ref[...]
    mag = jnp.exp(l_re * dt)
    a_re = mag * jnp.cos(l_im * dt)
    a_im = mag * jnp.sin(l_im * dt)
    den = l_re * l_re + l_im * l_im
    n_re = a_re - 1.0
    z_re = (n_re * l_re + a_im * l_im) / den
    z_im = (a_im * l_re - n_re * l_im) / den
    are_ref[...] = a_re
    aim_ref[...] = a_im
    br, bi = bre_ref[...], bim_ref[...]
    zr, zi = z_re[:, None, :], z_im[:, None, :]
    bbre_ref[...] = zr * br - zi * bi
    bbim_ref[...] = zr * bi + zi * br


def _s5_prep(lam_re, lam_im, log_dt, b_re_t, b_im_t):
    g, p = lam_re.shape
    sd = jax.ShapeDtypeStruct
    return pl.pallas_call(
        _s5_prep_kernel,
        out_shape=[sd((g, p), F32), sd((g, p), F32), sd(b_re_t.shape, F32), sd(b_re_t.shape, F32)],
        name="s5_prep",
    )(lam_re, lam_im, log_dt, b_re_t, b_im_t)


def _s5_kernel(u_ref, h0r_ref, h0i_ref, ar_ref, ai_ref, wb_ref, wc_ref, d_ref, wglu_ref, bglu_ref,
               g_ref, o_ref, hr_ref, hi_ref, bu_ref, hs_ref, *, tt, stride):
    j = pl.program_id(1)
    rows = SUBLANES * tt
    n_tiles = D_STATE // 128

    @pl.when(j == 0)
    def _():
        hs_ref[:, :D_STATE] = h0r_ref[...]
        hs_ref[:, D_STATE:] = h0i_ref[...]

    u = u_ref[...].reshape(rows, D_SSM)
    ub = u.astype(BF16)

    def put(c, val):
        if stride == tt:
            bu_ref[c, 0:rows, :] = val
        else:
            for b in range(SUBLANES):
                bu_ref[c, b * stride:b * stride + tt, :] = val[b * tt:(b + 1) * tt]

    def get(c):
        if stride == tt:
            return bu_ref[c, 0:rows, :]
        return jnp.concatenate([bu_ref[c, b * stride:b * stride + tt, :] for b in range(SUBLANES)], axis=0)

    for n in range(2 * D_STATE // 256):
        band = (n % (D_STATE // 256)) // 2
        res = _dot(ub[:, 128 * band:128 * (band + 1)],
                   wb_ref[128 * band:128 * (band + 1), 256 * n:256 * (n + 1)])
        put(2 * n, res[:, :128])
        put(2 * n + 1, res[:, 128:])

    tiles_per_pass = S5_COLS // 128
    for c0 in range(0, n_tiles, tiles_per_pass):
        tiles = range(c0, c0 + tiles_per_pass)
        a_r = [jnp.broadcast_to(ar_ref[:, 128 * c:128 * (c + 1)], (SUBLANES, 128)) for c in tiles]
        a_i = [jnp.broadcast_to(ai_ref[:, 128 * c:128 * (c + 1)], (SUBLANES, 128)) for c in tiles]

        def step(t, carry, tiles=tiles, a_r=a_r, a_i=a_i):
            at_t = pl.ds(t, SUBLANES, stride=stride)
            out = []
            for k, c in enumerate(tiles):
                h_r, h_i = carry[2 * k], carry[2 * k + 1]
                n_r = a_r[k] * h_r - a_i[k] * h_i + bu_ref[c, at_t, :]
                n_i = a_r[k] * h_i + a_i[k] * h_r + bu_ref[n_tiles + c, at_t, :]
                bu_ref[c, at_t, :] = n_r
                bu_ref[n_tiles + c, at_t, :] = n_i
                out += [n_r, n_i]
            return tuple(out)

        init = []
        for c in tiles:
            init += [hs_ref[:, 128 * c:128 * (c + 1)], hs_ref[:, D_STATE + 128 * c:D_STATE + 128 * (c + 1)]]
        fin = lax.fori_loop(0, tt, step, tuple(init), unroll=min(tt, 8))
        for k, c in enumerate(tiles):
            hs_ref[:, 128 * c:128 * (c + 1)] = fin[2 * k]
            hs_ref[:, D_STATE + 128 * c:D_STATE + 128 * (c + 1)] = fin[2 * k + 1]

    def h_cols(first_tile):
        return jnp.concatenate([get(first_tile + k) for k in range(4)], axis=-1).astype(BF16)

    ys = []
    for m in range(D_SSM // 128):
        y = _dot(h_cols(4 * m), wc_ref[512 * m:512 * (m + 1), 128 * m:128 * (m + 1)])
        y = y + _dot(h_cols(n_tiles + 4 * m),
                     wc_ref[D_STATE + 512 * m:D_STATE + 512 * (m + 1), 128 * m:128 * (m + 1)])
        ys.append(y)
    y = jnp.concatenate(ys, axis=-1) + d_ref[...] * u
    z = _dot(jax.nn.gelu(y).astype(BF16), wglu_ref[...]) + bglu_ref[...]
    s = z[:, :D_SSM] * jax.nn.sigmoid(z[:, D_SSM:])
    o_ref[...] = (_rms(s) * g_ref[...]).astype(BF16).reshape(o_ref.shape)

    @pl.when(j == pl.num_programs(1) - 1)
    def _():
        hr_ref[...] = hs_ref[:, :D_STATE]
        hi_ref[...] = hs_ref[:, D_STATE:]


def _s5(u, h0r, h0i, a_re, a_im, wb, wc, d, wglu, bglu, g, *, tt, stride, time_chunked):
    nbg = h0r.shape[0] // SUBLANES
    if time_chunked:
        nchunks = u.shape[1] // tt
        u_spec = pl.BlockSpec((SUBLANES, tt, D_SSM), lambda gi, j: (gi, j, 0))
    else:
        nchunks = 1
        u_spec = pl.BlockSpec((None, SUBLANES * tt, D_SSM), lambda gi, j: (gi, 0, 0))
    fix = lambda gi, j: (0, 0)
    st_spec = pl.BlockSpec((SUBLANES, D_STATE), lambda gi, j: (gi, 0))
    sd = jax.ShapeDtypeStruct
    return pl.pallas_call(
        functools.partial(_s5_kernel, tt=tt, stride=stride),
        grid=(nbg, nchunks),
        in_specs=[u_spec, st_spec, st_spec,
                  pl.BlockSpec((1, D_STATE), fix), pl.BlockSpec((1, D_STATE), fix),
                  pl.BlockSpec((D_SSM, 2 * D_STATE), fix), pl.BlockSpec((2 * D_STATE, D_SSM), fix),
                  pl.BlockSpec((1, D_SSM), fix), pl.BlockSpec((D_SSM, 2 * D_SSM), fix),
                  pl.BlockSpec((1, 2 * D_SSM), fix), pl.BlockSpec((1, D_SSM), fix)],
        out_specs=[u_spec, st_spec, st_spec],
        out_shape=[sd(u.shape, BF16), sd(h0r.shape, F32), sd(h0r.shape, F32)],
        scratch_shapes=[pltpu.VMEM((2 * D_STATE // 128, SUBLANES * stride, 128), F32),
                        pltpu.VMEM((SUBLANES, 2 * D_STATE), F32)],
        compiler_params=_cparams("parallel", "arbitrary"),
        name="s5",
    )(u, h0r, h0i, a_re, a_im, wb, wc, d, wglu, bglu, g)


def _to_row_tiles(ref, val, first=0):
    rows = val.shape[0]
    for c in range(D_MODEL // 128):
        ref[pl.ds(first * SUBLANES + c, rows, stride=SUBLANES), :] = val[:, 128 * c:128 * (c + 1)]


def _from_row_tiles(ref, rows, lead=(), first=0):
    return jnp.concatenate(
        [ref[(*lead, pl.ds(first * SUBLANES + c, rows, stride=SUBLANES), slice(None))]
         for c in range(D_MODEL // 128)], axis=-1)


def _out_proj_kernel(a_ref, s_ref, x_ref, wa_ref, ws_ref, g_ref, wr_ref, br_ref, x1_ref, xt_ref, lg_ref):
    x1 = x_ref[...] + _dot(a_ref[...], wa_ref[...]) + _dot(s_ref[...], ws_ref[...])
    x1_ref[...] = x1
    xn = _rms(x1) * g_ref[...]
    _to_row_tiles(xt_ref, xn)
    lg_ref[...] = _dot(xn.astype(BF16), wr_ref[...]) + br_ref[...]


def _out_proj(attn_n, ssm_n, x2d, wa, ws, g, wr, br):
    t = x2d.shape[0]
    tm = min(ROW_TILE, t)
    row = lambda i: (i, 0)
    fix = lambda i: (0, 0)
    sd = jax.ShapeDtypeStruct
    return pl.pallas_call(
        _out_proj_kernel,
        grid=(t // tm,),
        in_specs=[pl.BlockSpec((tm, D_ATTN), row), pl.BlockSpec((tm, D_SSM), row),
                  pl.BlockSpec((tm, D_MODEL), row),
                  pl.BlockSpec((D_ATTN, D_MODEL), fix), pl.BlockSpec((D_SSM, D_MODEL), fix),
                  pl.BlockSpec((1, D_MODEL), fix), pl.BlockSpec((D_MODEL, 128), fix),
                  pl.BlockSpec((1, 128), fix)],
        out_specs=[pl.BlockSpec((tm, D_MODEL), row), pl.BlockSpec((tm * SUBLANES, 128), row),
                   pl.BlockSpec((tm, 128), row)],
        out_shape=[sd((t, D_MODEL), F32), sd((t * SUBLANES, 128), F32), sd((t, 128), F32)],
        compiler_params=_cparams("parallel"),
        name="out_proj",
    )(attn_n, ssm_n, x2d, wa, ws, g, wr, br)


def _tile_span(ref, first_row, n_rows, lead=()):
    start = pl.multiple_of(first_row * SUBLANES, SUBLANES)
    return ref.at[(*lead, pl.ds(start, n_rows * SUBLANES), slice(None))]


def _run_copies(tile, src_tbl, cnt_tbl, loc_tbl, make_copy):
    def body(e, c):
        j = tile * N_EXPERTS + e
        cnt, src, loc = cnt_tbl[j], src_tbl[j], loc_tbl[j]
        off = 0
        for size in _RUN_SIZES:
            @pl.when((cnt & size) != 0)
            def _(off=off, size=size):
                make_copy(src + off, loc + off, size).start()
            off = off + (cnt & size)
        return c

    lax.fori_loop(0, N_EXPERTS, body, 0)


def _dispatch_kernel(src_tbl, cnt_tbl, loc_tbl, pend_ref, padded_ref, nu_ref, off_ref, xa_ref, xb_ref, rows_ref,
                     buf, zbuf, sem, zsem, *, tiles_a, nblk):
    i = pl.program_id(0)
    n = pl.num_programs(0)
    slot = i % 2

    def zero_block(start_row):
        return pltpu.make_async_copy(zbuf, _tile_span(rows_ref, start_row, MOE_TILE), zsem)

    @pl.when(i == 0)
    def _():
        zbuf[...] = jnp.zeros_like(zbuf)
        for e in range(N_EXPERTS):
            @pl.when(padded_ref[e] > 0)
            def _(e=e):
                zero_block(pend_ref[e] - MOE_TILE).start()

        def tail_start(j, c):
            zero_block(j * MOE_TILE).start()
            return c

        def tail_wait(j, c):
            zero_block(j * MOE_TILE).wait()
            return c

        lax.fori_loop(nu_ref[0], nblk, tail_start, 0)
        for e in range(N_EXPERTS):
            @pl.when(padded_ref[e] > 0)
            def _(e=e):
                zero_block(pend_ref[e] - MOE_TILE).wait()
        lax.fori_loop(nu_ref[0], nblk, tail_wait, 0)

    def slot_rows(s):
        return buf.at[pl.ds(pl.multiple_of(s * _SLOT_ROWS, _SLOT_ROWS), _SLOT_ROWS), :]

    def slot_done(s):
        return pltpu.make_async_copy(slot_rows(s), slot_rows(s), sem.at[s])

    @pl.when(i >= 2)
    def _():
        slot_done(slot).wait()

    def fill(x_ref):
        def body(t, c):
            v = x_ref[pl.ds(pl.multiple_of(t * SUBLANES, SUBLANES), SUBLANES), :]
            for k in range(TOP_K):
                off = pl.multiple_of(off_ref[0, t * TOP_K + k], SUBLANES)
                buf[pl.ds(off, SUBLANES), :] = v
            return c

        lax.fori_loop(0, ROUTE_TILE, body, 0, unroll=8)

    @pl.when(i < tiles_a)
    def _():
        fill(xa_ref)

    @pl.when(i >= tiles_a)
    def _():
        fill(xb_ref)

    _run_copies(i, src_tbl, cnt_tbl, loc_tbl,
                lambda g, l, size: pltpu.make_async_copy(_tile_span(slot_rows(slot), l, size),
                                                         _tile_span(rows_ref, g, size), sem.at[slot]))

    @pl.when(i == n - 1)
    def _():
        slot_done(1 - slot).wait()
        slot_done(slot).wait()


def _dispatch(runs, pend, padded, n_used, off, xa, xb, n_rows):
    tile_rows = ROUTE_TILE * SUBLANES
    tiles_a, tiles_b = xa.shape[0] // tile_rows, xb.shape[0] // tile_rows
    assert tiles_a + tiles_b >= 2
    grid_spec = pltpu.PrefetchScalarGridSpec(
        num_scalar_prefetch=6, grid=(tiles_a + tiles_b,),
        in_specs=[pl.BlockSpec((None, 1, ROUTE_TILE * TOP_K), lambda i, *_: (i, 0, 0), memory_space=pltpu.SMEM),
                  pl.BlockSpec((tile_rows, 128), lambda i, *_: (jnp.minimum(i, tiles_a - 1), 0)),
                  pl.BlockSpec((tile_rows, 128), lambda i, *_: (jnp.maximum(i - tiles_a, 0), 0))],
        out_specs=pl.BlockSpec(memory_space=pl.ANY),
        scratch_shapes=[pltpu.VMEM((2 * _SLOT_ROWS, 128), F32),
                        pltpu.VMEM((MOE_TILE * SUBLANES, 128), F32),
                        pltpu.SemaphoreType.DMA((2,)), pltpu.SemaphoreType.DMA(())])
    return pl.pallas_call(
        functools.partial(_dispatch_kernel, tiles_a=tiles_a, nblk=n_rows // MOE_TILE), grid_spec=grid_spec,
        out_shape=jax.ShapeDtypeStruct((n_rows * SUBLANES, 128), F32),
        compiler_params=_cparams("arbitrary"), name="dispatch",
    )(*runs, pend, padded, n_used, off, xa, xb)


def _moe_kernel(be_ref, nu_ref, nv_ref, next_ref, slot_ref, x_ref, wgu_hbm, bg_ref, bl_ref, wd_hbm, bd_ref,
                perm_ref, o_ref, wgu_buf, wd_buf, wg_s, wl_s, wd_s, sem_gu, sem_d):
    i = pl.program_id(0)
    used = i < nu_ref[0]
    e = be_ref[i]
    new_expert = (i == 0) | (e != be_ref[jnp.maximum(i - 1, 0)])

    def fetch(expert, s):
        return (pltpu.make_async_copy(wgu_hbm.at[expert], wgu_buf.at[s], sem_gu.at[s]),
                pltpu.make_async_copy(wd_hbm.at[expert], wd_buf.at[s], sem_d.at[s]))

    @pl.when(used & new_expert)
    def _():
        s = slot_ref[e]

        @pl.when(i == 0)
        def _():
            for cp in fetch(e, s):
                cp.start()

        @pl.when(next_ref[e] >= 0)
        def _():
            for cp in fetch(next_ref[e], 1 - s):
                cp.start()

        for cp in fetch(e, s):
            cp.wait()
        for c in range(2 * D_FF // 256):
            r = _dot(wgu_buf[s, :, 256 * c:256 * (c + 1)].astype(BF16), perm_ref[...])
            wg_s[:, 128 * c:128 * (c + 1)] = r[:, :128].astype(BF16)
            wl_s[:, 128 * c:128 * (c + 1)] = r[:, 128:].astype(BF16)
        wd_s[...] = wd_buf[s].astype(BF16)

    def expert_rows(rows):
        x = _from_row_tiles(x_ref, rows).astype(BF16)
        glu = jnp.minimum(_dot(x, wg_s[...]) + bg_ref[...], SWIGLU_LIMIT)
        lin = jnp.clip(_dot(x, wl_s[...]) + bl_ref[...], -SWIGLU_LIMIT, SWIGLU_LIMIT)
        act = glu * jax.nn.sigmoid(SWIGLU_ALPHA * glu) * (lin + 1.0)
        _to_row_tiles(o_ref, _dot(act.astype(BF16), wd_s[...]) + bd_ref[...])

    def zero_rows(first, rows):
        o_ref[pl.ds(first * SUBLANES, rows * SUBLANES), :] = jnp.zeros((rows * SUBLANES, 128), F32)

    half = MOE_TILE // 2

    @pl.when(used & (nv_ref[i] > half))
    def _():
        expert_rows(MOE_TILE)

    @pl.when(used & (nv_ref[i] <= half))
    def _():
        expert_rows(half)
        zero_rows(half, half)

    @pl.when(jnp.logical_not(used))
    def _():
        zero_rows(0, MOE_TILE)


def _moe(block_e, n_used, n_valid, next_e, e_slot, x_rows, wgu, bg, bl, wd, bd, perm):
    nblk = x_rows.shape[0] // (MOE_TILE * SUBLANES)
    row = lambda i, be, nu, *_: (jnp.minimum(i, nu[0] - 1), 0)
    wsel = lambda i, be, *_: (be[i], 0, 0)
    grid_spec = pltpu.PrefetchScalarGridSpec(
        num_scalar_prefetch=5,
        grid=(nblk,),
        in_specs=[pl.BlockSpec((MOE_TILE * SUBLANES, 128), row),
                  pl.BlockSpec(memory_space=pl.ANY),
                  pl.BlockSpec((None, 1, D_FF), wsel), pl.BlockSpec((None, 1, D_FF), wsel),
                  pl.BlockSpec(memory_space=pl.ANY), pl.BlockSpec((None, 1, D_MODEL), wsel),
                  pl.BlockSpec((256, 256), lambda i, *_: (0, 0))],
        out_specs=pl.BlockSpec((MOE_TILE * SUBLANES, 128), lambda i, *_: (i, 0)),
        scratch_shapes=[pltpu.VMEM((2, D_MODEL, 2 * D_FF), F32), pltpu.VMEM((2, D_FF, D_MODEL), F32),
                        pltpu.VMEM((D_MODEL, D_FF), BF16), pltpu.VMEM((D_MODEL, D_FF), BF16),
                        pltpu.VMEM((D_FF, D_MODEL), BF16),
                        pltpu.SemaphoreType.DMA((2,)), pltpu.SemaphoreType.DMA((2,))],
    )
    return pl.pallas_call(
        _moe_kernel,
        grid_spec=grid_spec,
        out_shape=jax.ShapeDtypeStruct(x_rows.shape, F32),
        compiler_params=_cparams("arbitrary"),
        name="moe",
    )(block_e, n_used, n_valid, next_e, e_slot, x_rows, wgu, bg, bl, wd, bd, perm)


def _combine_kernel(src_tbl, cnt_tbl, loc_tbl, off_ref, gate_ref, rows_hbm, x1_ref, o_ref, buf, ybuf, sem, *,
                    tile_base):
    i = pl.program_id(0)
    n = pl.num_programs(0)

    def slot_rows(s):
        return buf.at[pl.ds(pl.multiple_of(s * _SLOT_ROWS, _SLOT_ROWS), _SLOT_ROWS), :]

    def fetch(tile, s):
        _run_copies(tile_base + tile, src_tbl, cnt_tbl, loc_tbl,
                    lambda g, l, size: pltpu.make_async_copy(_tile_span(rows_hbm, g, size),
                                                             _tile_span(slot_rows(s), l, size), sem.at[s]))

    @pl.when(i == 0)
    def _():
        fetch(0, 0)

    @pl.when(i + 1 < n)
    def _():
        fetch(i + 1, (i + 1) % 2)

    slot = i % 2
    pltpu.make_async_copy(slot_rows(slot), slot_rows(slot), sem.at[slot]).wait()

    def body(t, c):
        acc = None
        for k in range(TOP_K):
            off = pl.multiple_of(off_ref[0, t * TOP_K + k], SUBLANES)
            v = gate_ref[0, t * TOP_K + k] * buf[pl.ds(off, SUBLANES), :]
            acc = v if acc is None else acc + v
        ybuf[pl.ds(pl.multiple_of(t * SUBLANES, SUBLANES), SUBLANES), :] = acc
        return c

    lax.fori_loop(0, ROUTE_TILE, body, 0, unroll=8)
    o_ref[...] = x1_ref[...] + _from_row_tiles(ybuf, ROUTE_TILE)


def _combine(runs, off, gates, rows, x1, tile_base):
    assert tile_base % 2 == 0
    t = x1.shape[0]
    tile_rows = ROUTE_TILE * SUBLANES
    smem_tile = pl.BlockSpec((None, 1, ROUTE_TILE * TOP_K), lambda i, *_: (tile_base + i, 0, 0),
                             memory_space=pltpu.SMEM)
    grid_spec = pltpu.PrefetchScalarGridSpec(
        num_scalar_prefetch=3,
        grid=(t // ROUTE_TILE,),
        in_specs=[smem_tile, smem_tile, pl.BlockSpec(memory_space=pl.ANY),
                  pl.BlockSpec((ROUTE_TILE, D_MODEL), lambda i, *_: (i, 0))],
        out_specs=pl.BlockSpec((ROUTE_TILE, D_MODEL), lambda i, *_: (i, 0)),
        scratch_shapes=[pltpu.VMEM((2 * _SLOT_ROWS, 128), F32), pltpu.VMEM((tile_rows, 128), F32),
                        pltpu.SemaphoreType.DMA((2,))],
    )
    return pl.pallas_call(
        functools.partial(_combine_kernel, tile_base=tile_base),
        grid_spec=grid_spec,
        out_shape=jax.ShapeDtypeStruct((t, D_MODEL), F32),
        compiler_params=_cparams("arbitrary"),
        name="combine",
    )(*runs, off, gates, rows, x1)


def _lane_pick(cols):
    lane = lax.broadcasted_iota(jnp.int32, (cols[0].shape[0], len(cols)), 1)
    out = cols[-1]
    for k in range(len(cols) - 2, -1, -1):
        out = jnp.where(lane == k, cols[k], out)
    return out


def _topk_kernel(lga_ref, lgb_ref, tri_ref, upper_ref, gate_ref, off_ref, before_ref, cnt_ref, loc_ref, carry, *,
                 tiles_a):
    i = pl.program_id(0)

    @pl.when(i == 0)
    def _():
        carry[...] = jnp.zeros_like(carry)

    l = jnp.where(i < tiles_a, lga_ref[...], lgb_ref[...])
    lane = lax.broadcasted_iota(jnp.int32, l.shape, 1)
    vals, sels = [], []
    for _ in range(TOP_K):
        m = jnp.max(l, axis=-1, keepdims=True)
        idx = jnp.min(jnp.where(l == m, lane, l.shape[1]), axis=-1, keepdims=True)
        sel = lane == idx
        l = jnp.where(sel, -jnp.inf, l)
        vals.append(m)
        sels.append(sel)
    exps = [jnp.exp(v - vals[0]) for v in vals]
    den = exps[0] + exps[1] + exps[2] + exps[3]
    onehot = jnp.where(sels[0] | sels[1] | sels[2] | sels[3], 1.0, 0.0)
    within = _dot(tri_ref[...], onehot.astype(BF16))
    cnt = jnp.sum(onehot, axis=0, keepdims=True)
    cnt_hi = jnp.floor(cnt * (1.0 / 16.0))
    cnt_lo = cnt - 16.0 * cnt_hi
    loc = 16.0 * _dot(cnt_hi.astype(BF16), upper_ref[...]) + _dot(cnt_lo.astype(BF16), upper_ref[...])
    slot_base = ((i % 2) * (ROUTE_TILE * TOP_K)).astype(F32)
    rows = [jnp.sum(jnp.where(s, within + loc, 0.0), axis=-1, keepdims=True) for s in sels]
    gate_ref[...] = _lane_pick([e / den for e in exps])
    off_ref[...] = ((_lane_pick(rows) + slot_base) * SUBLANES).astype(jnp.int32)
    before_ref[...] = carry[...]
    cnt_ref[...] = cnt
    loc_ref[...] = loc
    carry[...] = carry[...] + cnt


def _route(lga, lgb):
    tr = ROUTE_TILE
    tiles_a, tiles_b = lga.shape[0] // tr, lgb.shape[0] // tr
    n_tiles = tiles_a + tiles_b
    t = n_tiles * tr
    n_assign = t * TOP_K
    row = lambda i: (i, 0)
    fix = lambda i: (0, 0)
    sd = jax.ShapeDtypeStruct
    k_spec = pl.BlockSpec((tr, TOP_K), row)
    t_spec = pl.BlockSpec((None, 1, 128), lambda i: (i, 0, 0))
    tri = (jnp.arange(tr)[:, None] > jnp.arange(tr)[None, :]).astype(BF16)
    upper = (jnp.arange(128)[:, None] < jnp.arange(128)[None, :]).astype(BF16)
    gates, off, before, cnt, loc = pl.pallas_call(
        functools.partial(_topk_kernel, tiles_a=tiles_a),
        grid=(n_tiles,),
        in_specs=[pl.BlockSpec((tr, 128), lambda i: (jnp.minimum(i, tiles_a - 1), 0)),
                  pl.BlockSpec((tr, 128), lambda i: (jnp.maximum(i - tiles_a, 0), 0)),
                  pl.BlockSpec((tr, tr), fix), pl.BlockSpec((128, 128), fix)],
        out_specs=[k_spec, k_spec, t_spec, t_spec, t_spec],
        out_shape=[sd((t, TOP_K), F32), sd((t, TOP_K), jnp.int32),
                   sd((n_tiles, 1, 128), F32), sd((n_tiles, 1, 128), F32), sd((n_tiles, 1, 128), F32)],
        scratch_shapes=[pltpu.VMEM((1, 128), F32)],
        compiler_params=_cparams("arbitrary"),
        name="topk",
    )(lga, lgb, tri, upper)
    table = lambda a: a[:, 0, :N_EXPERTS].astype(jnp.int32)
    before, cnt, run_loc = table(before), table(cnt), table(loc)
    counts = before[-1] + cnt[-1]
    padded = ((counts + MOE_TILE - 1) // MOE_TILE) * MOE_TILE
    pend = jnp.cumsum(padded)
    pstart = pend - padded
    run_src = pstart[None, :] + before
    nblk = (n_assign + MOE_TILE - 1) // MOE_TILE + N_EXPERTS
    n_used = (pend[-1] // MOE_TILE).astype(jnp.int32)
    block_start = jnp.arange(nblk, dtype=jnp.int32) * MOE_TILE
    block_e = jnp.sum(pend[None, :] <= jnp.minimum(block_start, pend[-1] - 1)[:, None], axis=1)
    block_e = jnp.minimum(block_e, N_EXPERTS - 1).astype(jnp.int32)
    region_end = jnp.sum(jnp.where(block_e[:, None] == jnp.arange(N_EXPERTS)[None, :], (pstart + counts)[None, :], 0),
                         axis=1)
    n_valid = jnp.clip(region_end - block_start, 0, MOE_TILE).astype(jnp.int32)
    ids = jnp.arange(N_EXPERTS, dtype=jnp.int32)
    nonempty = counts > 0
    next_e = jnp.min(jnp.where((ids[None, :] > ids[:, None]) & nonempty[None, :], ids[None, :], N_EXPERTS), axis=1)
    next_e = jnp.where(next_e == N_EXPERTS, -1, next_e).astype(jnp.int32)
    e_slot = ((jnp.cumsum(nonempty.astype(jnp.int32)) - nonempty.astype(jnp.int32)) % 2).astype(jnp.int32)
    blocks = (block_e, n_used.reshape(1), n_valid, next_e, e_slot)
    runs = (run_src.reshape(-1), cnt.reshape(-1), run_loc.reshape(-1))
    per_tile_smem = lambda a: a.reshape(n_tiles, 1, tr * TOP_K)
    return per_tile_smem(gates), per_tile_smem(off), runs, blocks, pend, padded, nblk * MOE_TILE


def kernel(x_prompt, x_sample, cache_k, cache_v, state_ssm_re, state_ssm_im, norm_mix_g, w_in, q_norm_g, k_norm_g, attn_sinks, ssm_lambda_re, ssm_lambda_im, ssm_b_re, ssm_b_im, ssm_c_re, ssm_c_im, ssm_d, ssm_log_dt, w_glu, b_glu, attn_out_norm_g, ssm_out_norm_g, w_out, norm_ffn_g, w_router, b_router, w_gate_up, b_gate_up, w_down, b_down):
    depth = w_in.shape[0]
    assert depth == 1
    bp, sp, _ = x_prompt.shape
    bs, ss, _ = x_sample.shape
    tp, ts = bp * sp, bs * ss
    assert bp == SUBLANES and sp % S5_CHUNK == 0 and bs % SUBLANES == 0

    perm = jnp.asarray(_Q_PERM)
    w_in0 = w_in[0]
    wq = w_in0[:, :D_ATTN].reshape(D_MODEL, N_HEADS, HEAD_DIM)[:, perm].reshape(D_MODEL, D_ATTN)
    w_in_b = jnp.concatenate([wq, w_in0[:, D_ATTN:]], axis=1).astype(BF16)
    qkg = jnp.concatenate([jnp.tile(q_norm_g[0], N_HEADS), jnp.tile(k_norm_g[0], N_KV_HEADS)])[None]
    pmat = jnp.kron(jnp.eye(256 // HEAD_DIM, dtype=F32),
                    jnp.full((HEAD_DIM, HEAD_DIM), 1.0 / HEAD_DIM, F32)).astype(BF16)
    g_mix = norm_mix_g[0][None]
    sinks = attn_sinks[0]
    g_attn = attn_out_norm_g[0].reshape(N_HEADS, HEAD_DIM)[perm].reshape(1, D_ATTN)
    w_out0 = w_out[0]
    w_out_a = w_out0[:D_ATTN].reshape(N_HEADS, HEAD_DIM, D_MODEL)[perm].reshape(D_ATTN, D_MODEL).astype(BF16)
    w_out_s = w_out0[D_ATTN:].astype(BF16)
    g_ssm = ssm_out_norm_g[0][None]
    g_ffn = norm_ffn_g[0][None]
    w_r = jnp.pad(w_router[0], ((0, 0), (0, 128 - N_EXPERTS))).astype(BF16)
    b_r = jnp.pad(b_router[0], (0, 128 - N_EXPERTS), constant_values=NEG_INF)[None]

    a_re, a_im, bb_re, bb_im = _s5_prep(
        ssm_lambda_re[0], ssm_lambda_im[0], ssm_log_dt[0][:, None],
        jnp.swapaxes(ssm_b_re[0], 1, 2), jnp.swapaxes(ssm_b_im[0], 1, 2))
    eye_g = jnp.eye(N_SSM_GROUPS, dtype=F32)
    bd_b = lambda bb: jnp.einsum("ghp,gk->ghkp", bb, eye_g).reshape(D_SSM, D_STATE)
    wb = jnp.concatenate([bd_b(bb_re), bd_b(bb_im)], axis=1).astype(BF16)
    bd_c = lambda c: jnp.einsum("ghp,gk->gpkh", c, eye_g).reshape(D_STATE, D_SSM)
    wc = jnp.concatenate([bd_c(ssm_c_re[0]), -bd_c(ssm_c_im[0])], axis=0).astype(BF16)
    a_re, a_im = a_re.reshape(1, D_STATE), a_im.reshape(1, D_STATE)
    d_skip = ssm_d[0].reshape(1, D_SSM)
    w_glu_b = w_glu[0].astype(BF16)
    b_glu0 = b_glu[0][None]

    b_g = b_gate_up[0][:, None, 0::2]
    b_l = b_gate_up[0][:, None, 1::2]
    b_d = b_down[0][:, None, :]
    idx = jnp.arange(256)
    deint = (idx[None, :] == jnp.where(idx % 2 == 0, idx // 2, 128 + idx // 2)[:, None]).astype(BF16)

    xp2 = x_prompt.reshape(tp, D_MODEL)
    xs2 = x_sample.reshape(ts, D_MODEL)
    qp, kp, vp, up = _in_proj(xp2, g_mix, w_in_b, qkg, pmat)
    qs, ks, vs, us = _in_proj(xs2, g_mix, w_in_b, qkg, pmat)

    kp3, vp3 = kp.reshape(bp, sp, D_KV), vp.reshape(bp, sp, D_KV)
    attn_p = _band_attention(sinks, qp.reshape(bp, sp, D_ATTN), kp3, vp3, g_attn).reshape(tp, D_ATTN)
    ck = cache_k[0].reshape(bs, -1, D_KV)
    cv = cache_v[0].reshape(bs, -1, D_KV)
    attn_s = _cache_attention(sinks, qs, ks, vs, ck, cv, g_attn, ss)

    zeros_p = jnp.zeros((bp, D_STATE), F32)
    s5_args = (a_re, a_im, wb, wc, d_skip, w_glu_b, b_glu0, g_ssm)
    ssm_p, hr_p, hi_p = _s5(up.reshape(bp, sp, D_SSM), zeros_p, zeros_p, *s5_args,
                            tt=S5_CHUNK, stride=S5_CHUNK + S5_PAD, time_chunked=True)
    ssm_s, hr_s, hi_s = _s5(us.reshape(bs // SUBLANES, SUBLANES * ss, D_SSM),
                            state_ssm_re[0].reshape(bs, D_STATE), state_ssm_im[0].reshape(bs, D_STATE),
                            *s5_args, tt=ss, stride=ss, time_chunked=False)

    x1p, xtp, lgp = _out_proj(attn_p, ssm_p.reshape(tp, D_SSM), xp2, w_out_a, w_out_s, g_ffn, w_r, b_r)
    x1s, xts, lgs = _out_proj(attn_s, ssm_s.reshape(ts, D_SSM), xs2, w_out_a, w_out_s, g_ffn, w_r, b_r)

    gates, off, runs, blocks, pend, padded, n_rows = _route(lgp, lgs)
    x_rows = _dispatch(runs, pend, padded, blocks[1], off, xtp, xts, n_rows)
    out_rows = _moe(*blocks, x_rows, w_gate_up[0], b_g, b_l, w_down[0], b_d, deint)
    yp = _combine(runs, off, gates, out_rows, x1p, 0).reshape(bp, sp, D_MODEL)
    ys = _combine(runs, off, gates, out_rows, x1s, tp // ROUTE_TILE).reshape(bs, ss, D_MODEL)

    kv5 = lambda a, b_: a.reshape(b_, -1, N_KV_HEADS, HEAD_DIM)
    new_kp = kv5(kp3[:, -WINDOW:], bp)[None]
    new_vp = kv5(vp3[:, -WINDOW:], bp)[None]
    ks3, vs3 = ks.reshape(bs, ss, D_KV), vs.reshape(bs, ss, D_KV)
    new_ks = kv5(jnp.concatenate([ck, ks3], axis=1)[:, ss:], bs)[None]
    new_vs = kv5(jnp.concatenate([cv, vs3], axis=1)[:, ss:], bs)[None]
    st = lambda h, b_: h.reshape(1, b_, N_SSM_GROUPS, SSM_STATE)
    return (yp, ys, new_kp, new_vp, st(hr_p, bp), st(hi_p, bp),
            new_ks, new_vs, st(hr_s, bs), st(hi_s, bs))
```

```python
import functools
import math

import jax
import jax.numpy as jnp
from jax import lax
from jax.experimental import pallas as pl
from jax.experimental.pallas import tpu as pltpu

F32 = jnp.float32
BF16 = jnp.bfloat16

D_MODEL = 1024
D_ATTN = 512
D_SSM = 512
HEAD_DIM = 64
N_HEADS = 8
N_KV_HEADS = 2
D_KV = N_KV_HEADS * HEAD_DIM
WINDOW = 128
SSM_GROUP = 16
N_SSM_GROUPS = 32
SSM_STATE = 64
D_STATE = N_SSM_GROUPS * SSM_STATE
N_EXPERTS = 32
TOP_K = 4
D_FF = 1024
SWIGLU_LIMIT = 7.0
SWIGLU_ALPHA = 1.702
RMS_EPS = 1e-6
NEG_INF = -1e30
D_IN_PROJ = D_ATTN + 2 * D_KV + D_SSM
D_QK = D_ATTN + D_KV

SUBLANES = 8
VMEM_LIMIT = 56 * 1024 * 1024

ROW_TILE = 512
MOE_TILE = 512
ROUTE_TILE = 512
_SLOT_ROWS = ROUTE_TILE * TOP_K * SUBLANES
_RUN_SIZES = tuple(1 << b for b in range(ROUTE_TILE.bit_length() - 1, -1, -1))
S5_CHUNK = 128
S5_COLS = 512

_Q_PERM = (0, 4, 1, 5, 2, 6, 3, 7)


def _cparams(*sem):
    return pltpu.CompilerParams(dimension_semantics=sem, vmem_limit_bytes=VMEM_LIMIT)


def _dot(a, b):
    return jnp.dot(a, b, preferred_element_type=F32)


def _rms(x):
    return x * lax.rsqrt(jnp.mean(x * x, axis=-1, keepdims=True) + RMS_EPS)


def _in_proj_kernel(x_ref, g_ref, w_ref, qkg_ref, p_ref, q_ref, k_ref, v_ref, u_ref):
    xn = _rms(x_ref[...]) * g_ref[...]
    h = _dot(xn.astype(BF16), w_ref[...])
    qk = h[:, :D_QK]
    sq = (qk * qk).astype(BF16)
    p = p_ref[...]
    ms = jnp.concatenate(
        [_dot(sq[:, 0:256], p), _dot(sq[:, 256:512], p), _dot(sq[:, 512:640], p[:128, :128])],
        axis=-1)
    qkn = qk * lax.rsqrt(ms + RMS_EPS) * qkg_ref[...]
    q_ref[...] = qkn[:, :D_ATTN].astype(BF16)
    k_ref[...] = qkn[:, D_ATTN:]
    v_ref[...] = h[:, D_QK:D_QK + D_KV]
    u_ref[...] = h[:, D_QK + D_KV:]


def _in_proj(x2d, g, w, qkg, pmat):
    t = x2d.shape[0]
    tm = min(ROW_TILE, t)
    row = lambda i: (i, 0)
    fix = lambda i: (0, 0)
    return pl.pallas_call(
        _in_proj_kernel,
        grid=(t // tm,),
        in_specs=[pl.BlockSpec((tm, D_MODEL), row), pl.BlockSpec((1, D_MODEL), fix),
                  pl.BlockSpec((D_MODEL, D_IN_PROJ), fix), pl.BlockSpec((1, D_QK), fix),
                  pl.BlockSpec((256, 256), fix)],
        out_specs=[pl.BlockSpec((tm, D_ATTN), row), pl.BlockSpec((tm, D_KV), row),
                   pl.BlockSpec((tm, D_KV), row), pl.BlockSpec((tm, D_SSM), row)],
        out_shape=[jax.ShapeDtypeStruct((t, D_ATTN), BF16), jax.ShapeDtypeStruct((t, D_KV), F32),
                   jax.ShapeDtypeStruct((t, D_KV), F32), jax.ShapeDtypeStruct((t, D_SSM), F32)],
        compiler_params=_cparams("parallel"),
        name="in_proj",
    )(x2d, g, w, qkg, pmat)


def _softmax_pv(s_blocks, v_blocks, sink):
    m = sink
    for s in s_blocks:
        m = jnp.maximum(m, jnp.max(s, axis=-1, keepdims=True))
    den = jnp.exp(sink - m)
    acc = None
    for s, v in zip(s_blocks, v_blocks):
        p = jnp.exp(s - m)
        den = den + jnp.sum(p, axis=-1, keepdims=True)
        pv = _dot(p.astype(BF16), v)
        acc = pv if acc is None else acc + pv
    return acc / den


def _band_attn_kernel(sink_ref, q_ref, kp_ref, kc_ref, vp_ref, vc_ref, g_ref, o_ref):
    i = pl.program_id(1)
    q = q_ref[...]
    kb = jnp.concatenate([kp_ref[...], kc_ref[...]], axis=0).astype(BF16)
    vb = jnp.concatenate([vp_ref[...], vc_ref[...]], axis=0).astype(BF16)
    row = lax.broadcasted_iota(jnp.int32, (WINDOW, 2 * WINDOW), 0)
    col = lax.broadcasted_iota(jnp.int32, (WINDOW, 2 * WINDOW), 1)
    mask = (col > row) & (col <= row + WINDOW) & ((col >= WINDOW) | (i > 0))
    lane = lax.broadcasted_iota(jnp.int32, (WINDOW, 128), 1)
    low = lane < HEAD_DIM
    zero = jnp.zeros((), BF16)
    outs = []
    for pair in range(N_HEADS // 2):
        qp = q[:, 128 * pair:128 * (pair + 1)]
        halves = []
        for par in range(2):
            qm = jnp.where(low if par == 0 else ~low, qp, zero)
            s = lax.dot_general(qm, kb, (((1,), (1,)), ((), ())), preferred_element_type=F32)
            s = jnp.where(mask, s * (HEAD_DIM ** -0.5), NEG_INF)
            halves.append(_softmax_pv([s], [vb], sink_ref[_Q_PERM[2 * pair + par]]))
        outs.append(jnp.where(low, halves[0], halves[1]))
    o = jnp.concatenate(outs, axis=-1)
    o_ref[...] = (_rms(o) * g_ref[...]).astype(BF16)


def _band_attention(sinks, q, k, v, g):
    b, s, _ = q.shape
    nb = s // WINDOW
    cur = lambda bi, i: (bi, i, 0)
    prev = lambda bi, i: (bi, jnp.maximum(i - 1, 0), 0)
    kv_spec = lambda im: pl.BlockSpec((None, WINDOW, D_KV), im)
    return pl.pallas_call(
        _band_attn_kernel,
        grid=(b, nb),
        in_specs=[pl.BlockSpec(memory_space=pltpu.SMEM),
                  pl.BlockSpec((None, WINDOW, D_ATTN), cur),
                  kv_spec(prev), kv_spec(cur), kv_spec(prev), kv_spec(cur),
                  pl.BlockSpec((1, D_ATTN), lambda bi, i: (0, 0))],
        out_specs=pl.BlockSpec((None, WINDOW, D_ATTN), cur),
        out_shape=jax.ShapeDtypeStruct((b, s, D_ATTN), BF16),
        compiler_params=_cparams("parallel", "parallel"),
        name="band_attn",
    )(sinks, q, k, k, v, v, g)


_PAIR_ROWS = 8
_CACHE_BB = 8


def _cache_attn_kernel(sink_ref, q_ref, kn_ref, vn_ref, ck_ref, cv_ref, g_ref, o_ref, *, n_new):
    n_buf = ck_ref.shape[1]
    rows_blk = _CACHE_BB * n_new
    knew = kn_ref[...].astype(BF16)
    vnew = vn_ref[...].astype(BF16)
    lane = lax.broadcasted_iota(jnp.int32, (_PAIR_ROWS, 128), 1)
    low = lane < HEAD_DIM
    zero = jnp.zeros((), BF16)
    n_stack = N_HEADS * _PAIR_ROWS
    r = lax.broadcasted_iota(jnp.int32, (n_stack, 1), 0) % _PAIR_ROWS
    r_seq, r_tok = r // n_new, r % n_new
    colc = lax.broadcasted_iota(jnp.int32, (n_stack, 2 * n_buf), 1)
    c_seq, c_pos = colc // n_buf, colc % n_buf
    mask_c = (c_seq == r_seq) & (c_pos + WINDOW > r_tok + n_buf)
    coln = lax.broadcasted_iota(jnp.int32, (n_stack, rows_blk), 1)
    sink_col = jnp.concatenate(
        [jnp.full((_PAIR_ROWS, 1), sink_ref[_Q_PERM[h]], F32) for h in range(N_HEADS)], axis=0)
    for sp in range(_CACHE_BB // 2):
        q = q_ref[_PAIR_ROWS * sp:_PAIR_ROWS * (sp + 1), :]
        pieces = []
        for pair in range(N_HEADS // 2):
            qp = q[:, 128 * pair:128 * (pair + 1)]
            pieces.append(jnp.where(low, qp, zero))
            pieces.append(jnp.where(low, zero, qp))
        qs = jnp.concatenate(pieces, axis=0)
        kc = jnp.concatenate([ck_ref[2 * sp], ck_ref[2 * sp + 1]], axis=0).astype(BF16)
        vc = jnp.concatenate([cv_ref[2 * sp], cv_ref[2 * sp + 1]], axis=0).astype(BF16)
        nt = (((1,), (1,)), ((), ()))
        s_c = lax.dot_general(qs, kc, nt, preferred_element_type=F32) * (HEAD_DIM ** -0.5)
        s_n = lax.dot_general(qs, knew, nt, preferred_element_type=F32) * (HEAD_DIM ** -0.5)
        s_c = jnp.where(mask_c, s_c, NEG_INF)
        n_seq, n_tok = coln // n_new - 2 * sp, coln % n_new
        mask_n = (n_seq == r_seq) & (n_tok <= r_tok)
        s_n = jnp.where(mask_n, s_n, NEG_INF)
        o = _softmax_pv([s_c, s_n], [vc, vnew], sink_col)
        outs = [jnp.where(low, o[16 * pair:16 * pair + 8], o[16 * pair + 8:16 * pair + 16])
                for pair in range(N_HEADS // 2)]
        oo = jnp.concatenate(outs, axis=-1)
        o_ref[_PAIR_ROWS * sp:_PAIR_ROWS * (sp + 1), :] = (_rms(oo) * g_ref[...]).astype(BF16)


def _cache_attention(sinks, q, k, v, cache_k, cache_v, g, n_new):
    t = q.shape[0]
    nb, n_buf, _ = cache_k.shape
    assert n_new * 2 == _PAIR_ROWS and nb % _CACHE_BB == 0 and n_buf == WINDOW
    rows = _CACHE_BB * n_new
    row = lambda i: (i, 0)
    cache_spec = pl.BlockSpec((_CACHE_BB, n_buf, D_KV), lambda i: (i, 0, 0))
    return pl.pallas_call(
        functools.partial(_cache_attn_kernel, n_new=n_new),
        grid=(nb // _CACHE_BB,),
        in_specs=[pl.BlockSpec(memory_space=pltpu.SMEM),
                  pl.BlockSpec((rows, D_ATTN), row), pl.BlockSpec((rows, D_KV), row),
                  pl.BlockSpec((rows, D_KV), row), cache_spec, cache_spec,
                  pl.BlockSpec((1, D_ATTN), lambda i: (0, 0))],
        out_specs=pl.BlockSpec((rows, D_ATTN), row),
        out_shape=jax.ShapeDtypeStruct((t, D_ATTN), BF16),
        compiler_params=_cparams("parallel"),
        name="cache_attn",
    )(sinks, q, k, v, cache_k, cache_v, g)


def _s5_prep_kernel(lre_ref, lim_ref, ldt_ref, bre_ref, bim_ref, are_ref, aim_ref, bbre_ref, bbim_ref):
    dt = jnp.exp(ldt_ref[...])
    l_re = jnp.minimum(lre_ref[...], -1e-4)
    l_im = lim_ref[...]
    mag = jnp.exp(l_re * dt)
    a_re = mag * jnp.cos(l_im * dt)
    a_im = mag * jnp.sin(l_im * dt)
    den = l_re * l_re + l_im * l_im
    n_re = a_re - 1.0
    z_re = (n_re * l_re + a_im * l_im) / den
    z_im = (a_im * l_re - n_re * l_im) / den
    are_ref[...] = a_re
    aim_ref[...] = a_im
    br, bi = bre_ref[...], bim_ref[...]
    zr, zi = z_re[:, None, :], z_im[:, None, :]
    bbre_ref[...] = zr * br - zi * bi
    bbim_ref[...] = zr * bi + zi * br


def _s5_prep(lam_re, lam_im, log_dt, b_re_t, b_im_t):
    g, p = lam_re.shape
    sd = jax.ShapeDtypeStruct
    return pl.pallas_call(
        _s5_prep_kernel,
        out_shape=[sd((g, p), F32), sd((g, p), F32), sd(b_re_t.shape, F32), sd(b_re_t.shape, F32)],
        name="s5_prep",
    )(lam_re, lam_im, log_dt, b_re_t, b_im_t)


def _s5_kernel(u_ref, h0r_ref, h0i_ref, ar_ref, ai_ref, wb_ref, wc_ref, d_ref, wglu_ref, bglu_ref,
               g_ref, o_ref, hr_ref, hi_ref, bu_ref, hs_ref, usc_ref, ysc_ref, *, tt):
    j = pl.program_id(1)
    rows = SUBLANES * tt
    n_tiles = D_STATE // 128
    time_major = tt % SUBLANES == 0

    @pl.when(j == 0)
    def _():
        hs_ref[:, :D_STATE] = h0r_ref[...]
        hs_ref[:, D_STATE:] = h0i_ref[...]

    u = u_ref[...].reshape(rows, D_SSM)
    if time_major:
        for c in range(D_SSM // 128):
            for b in range(SUBLANES):
                usc_ref[c, pl.ds(b, tt, stride=SUBLANES), :] = u[b * tt:(b + 1) * tt, 128 * c:128 * (c + 1)]
        ub = jnp.concatenate([usc_ref[c] for c in range(D_SSM // 128)], axis=-1).astype(BF16)
    else:
        ub = u.astype(BF16)

    def step_rows(t):
        if time_major:
            return pl.ds(pl.multiple_of(t * SUBLANES, SUBLANES), SUBLANES)
        return pl.ds(t, SUBLANES, stride=tt)

    for n in range(2 * D_STATE // 256):
        band = (n % (D_STATE // 256)) // 2
        res = _dot(ub[:, 128 * band:128 * (band + 1)],
                   wb_ref[128 * band:128 * (band + 1), 256 * n:256 * (n + 1)])
        bu_ref[2 * n] = res[:, :128]
        bu_ref[2 * n + 1] = res[:, 128:]

    tiles_per_pass = S5_COLS // 128
    for c0 in range(0, n_tiles, tiles_per_pass):
        tiles = range(c0, c0 + tiles_per_pass)
        a_r = [jnp.broadcast_to(ar_ref[:, 128 * c:128 * (c + 1)], (SUBLANES, 128)) for c in tiles]
        a_i = [jnp.broadcast_to(ai_ref[:, 128 * c:128 * (c + 1)], (SUBLANES, 128)) for c in tiles]

        def step(t, carry, tiles=tiles, a_r=a_r, a_i=a_i):
            at_t = step_rows(t)
            out = []
            for k, c in enumerate(tiles):
                h_r, h_i = carry[2 * k], carry[2 * k + 1]
                n_r = a_r[k] * h_r - a_i[k] * h_i + bu_ref[c, at_t, :]
                n_i = a_r[k] * h_i + a_i[k] * h_r + bu_ref[n_tiles + c, at_t, :]
                bu_ref[c, at_t, :] = n_r
                bu_ref[n_tiles + c, at_t, :] = n_i
                out += [n_r, n_i]
            return tuple(out)

        init = []
        for c in tiles:
            init += [hs_ref[:, 128 * c:128 * (c + 1)], hs_ref[:, D_STATE + 128 * c:D_STATE + 128 * (c + 1)]]
        fin = lax.fori_loop(0, tt, step, tuple(init), unroll=min(tt, 8))
        for k, c in enumerate(tiles):
            hs_ref[:, 128 * c:128 * (c + 1)] = fin[2 * k]
            hs_ref[:, D_STATE + 128 * c:D_STATE + 128 * (c + 1)] = fin[2 * k + 1]

    def h_cols(first_tile):
        return jnp.concatenate([bu_ref[first_tile + k] for k in range(4)], axis=-1).astype(BF16)

    ys = []
    for m in range(D_SSM // 128):
        y = _dot(h_cols(4 * m), wc_ref[512 * m:512 * (m + 1), 128 * m:128 * (m + 1)])
        y = y + _dot(h_cols(n_tiles + 4 * m),
                     wc_ref[D_STATE + 512 * m:D_STATE + 512 * (m + 1), 128 * m:128 * (m + 1)])
        if time_major:
            ysc_ref[m] = y
            y = jnp.concatenate([ysc_ref[m, pl.ds(b, tt, stride=SUBLANES), :] for b in range(SUBLANES)], axis=0)
        ys.append(y)
    y = jnp.concatenate(ys, axis=-1) + d_ref[...] * u
    z = _dot(jax.nn.gelu(y).astype(BF16), wglu_ref[...]) + bglu_ref[...]
    s = z[:, :D_SSM] * jax.nn.sigmoid(z[:, D_SSM:])
    o_ref[...] = (_rms(s) * g_ref[...]).astype(BF16).reshape(o_ref.shape)

    @pl.when(j == pl.num_programs(1) - 1)
    def _():
        hr_ref[...] = hs_ref[:, :D_STATE]
        hi_ref[...] = hs_ref[:, D_STATE:]


def _s5(u, h0r, h0i, a_re, a_im, wb, wc, d, wglu, bglu, g, *, tt, time_chunked):
    nbg = h0r.shape[0] // SUBLANES
    if time_chunked:
        nchunks = u.shape[1] // tt
        u_spec = pl.BlockSpec((SUBLANES, tt, D_SSM), lambda gi, j: (gi, j, 0))
    else:
        nchunks = 1
        u_spec = pl.BlockSpec((None, SUBLANES * tt, D_SSM), lambda gi, j: (gi, 0, 0))
    fix = lambda gi, j: (0, 0)
    st_spec = pl.BlockSpec((SUBLANES, D_STATE), lambda gi, j: (gi, 0))
    sd = jax.ShapeDtypeStruct
    return pl.pallas_call(
        functools.partial(_s5_kernel, tt=tt),
        grid=(nbg, nchunks),
        in_specs=[u_spec, st_spec, st_spec,
                  pl.BlockSpec((1, D_STATE), fix), pl.BlockSpec((1, D_STATE), fix),
                  pl.BlockSpec((D_SSM, 2 * D_STATE), fix), pl.BlockSpec((2 * D_STATE, D_SSM), fix),
                  pl.BlockSpec((1, D_SSM), fix), pl.BlockSpec((D_SSM, 2 * D_SSM), fix),
                  pl.BlockSpec((1, 2 * D_SSM), fix), pl.BlockSpec((1, D_SSM), fix)],
        out_specs=[u_spec, st_spec, st_spec],
        out_shape=[sd(u.shape, BF16), sd(h0r.shape, F32), sd(h0r.shape, F32)],
        scratch_shapes=[pltpu.VMEM((2 * D_STATE // 128, SUBLANES * tt, 128), F32),
                        pltpu.VMEM((SUBLANES, 2 * D_STATE), F32),
                        pltpu.VMEM((D_SSM // 128, SUBLANES * tt, 128), F32),
                        pltpu.VMEM((D_SSM // 128, SUBLANES * tt, 128), F32)],
        compiler_params=_cparams("parallel", "arbitrary"),
        name="s5",
    )(u, h0r, h0i, a_re, a_im, wb, wc, d, wglu, bglu, g)


def _to_row_tiles(ref, val, first=0):
    rows = val.shape[0]
    for c in range(D_MODEL // 128):
        ref[pl.ds(first * SUBLANES + c, rows, stride=SUBLANES), :] = val[:, 128 * c:128 * (c + 1)]


def _from_row_tiles(ref, rows, lead=(), first=0):
    return jnp.concatenate(
        [ref[(*lead, pl.ds(first * SUBLANES + c, rows, stride=SUBLANES), slice(None))]
         for c in range(D_MODEL // 128)], axis=-1)


def _out_proj_kernel(a_ref, s_ref, x_ref, wa_ref, ws_ref, g_ref, wr_ref, br_ref, x1_ref, xt_ref, lg_ref):
    x1 = x_ref[...] + _dot(a_ref[...], wa_ref[...]) + _dot(s_ref[...], ws_ref[...])
    x1_ref[...] = x1
    xn = _rms(x1) * g_ref[...]
    _to_row_tiles(xt_ref, xn)
    lg_ref[...] = _dot(xn.astype(BF16), wr_ref[...]) + br_ref[...]


def _out_proj(attn_n, ssm_n, x2d, wa, ws, g, wr, br):
    t = x2d.shape[0]
    tm = min(ROW_TILE, t)
    row = lambda i: (i, 0)
    fix = lambda i: (0, 0)
    sd = jax.ShapeDtypeStruct
    return pl.pallas_call(
        _out_proj_kernel,
        grid=(t // tm,),
        in_specs=[pl.BlockSpec((tm, D_ATTN), row), pl.BlockSpec((tm, D_SSM), row),
                  pl.BlockSpec((tm, D_MODEL), row),
                  pl.BlockSpec((D_ATTN, D_MODEL), fix), pl.BlockSpec((D_SSM, D_MODEL), fix),
                  pl.BlockSpec((1, D_MODEL), fix), pl.BlockSpec((D_MODEL, 128), fix),
                  pl.BlockSpec((1, 128), fix)],
        out_specs=[pl.BlockSpec((tm, D_MODEL), row), pl.BlockSpec((tm * SUBLANES, 128), row),
                   pl.BlockSpec((tm, 128), row)],
        out_shape=[sd((t, D_MODEL), F32), sd((t * SUBLANES, 128), F32), sd((t, 128), F32)],
        compiler_params=_cparams("parallel"),
        name="out_proj",
    )(attn_n, ssm_n, x2d, wa, ws, g, wr, br)


def _tile_span(ref, first_row, n_rows, lead=()):
    start = pl.multiple_of(first_row * SUBLANES, SUBLANES)
    return ref.at[(*lead, pl.ds(start, n_rows * SUBLANES), slice(None))]


def _run_copies(tile, src_tbl, cnt_tbl, loc_tbl, make_copy):
    def body(e, c):
        j = tile * N_EXPERTS + e
        cnt, src, loc = cnt_tbl[j], src_tbl[j], loc_tbl[j]
        off = 0
        for size in _RUN_SIZES:
            @pl.when((cnt & size) != 0)
            def _(off=off, size=size):
                make_copy(src + off, loc + off, size).start()
            off = off + (cnt & size)
        return c

    lax.fori_loop(0, N_EXPERTS, body, 0)


def _dispatch_kernel(src_tbl, cnt_tbl, loc_tbl, pend_ref, padded_ref, nu_ref, off_ref, xa_ref, xb_ref, rows_ref,
                     buf, zbuf, sem, zsem, *, tiles_a, nblk):
    i = pl.program_id(0)
    n = pl.num_programs(0)
    slot = i % 2

    def zero_block(start_row):
        return pltpu.make_async_copy(zbuf, _tile_span(rows_ref, start_row, MOE_TILE), zsem)

    @pl.when(i == 0)
    def _():
        zbuf[...] = jnp.zeros_like(zbuf)
        for e in range(N_EXPERTS):
            @pl.when(padded_ref[e] > 0)
            def _(e=e):
                zero_block(pend_ref[e] - MOE_TILE).start()

        def tail_start(j, c):
            zero_block(j * MOE_TILE).start()
            return c

        def tail_wait(j, c):
            zero_block(j * MOE_TILE).wait()
            return c

        lax.fori_loop(nu_ref[0], nblk, tail_start, 0)
        for e in range(N_EXPERTS):
            @pl.when(padded_ref[e] > 0)
            def _(e=e):
                zero_block(pend_ref[e] - MOE_TILE).wait()
        lax.fori_loop(nu_ref[0], nblk, tail_wait, 0)

    def slot_rows(s):
        return buf.at[pl.ds(pl.multiple_of(s * _SLOT_ROWS, _SLOT_ROWS), _SLOT_ROWS), :]

    def slot_done(s):
        return pltpu.make_async_copy(slot_rows(s), slot_rows(s), sem.at[s])

    @pl.when(i >= 2)
    def _():
        slot_done(slot).wait()

    def fill(x_ref):
        def body(t, c):
            v = x_ref[pl.ds(pl.multiple_of(t * SUBLANES, SUBLANES), SUBLANES), :]
            for k in range(TOP_K):
                off = pl.multiple_of(off_ref[0, t * TOP_K + k], SUBLANES)
                buf[pl.ds(off, SUBLANES), :] = v
            return c

        lax.fori_loop(0, ROUTE_TILE, body, 0, unroll=8)

    @pl.when(i < tiles_a)
    def _():
        fill(xa_ref)

    @pl.when(i >= tiles_a)
    def _():
        fill(xb_ref)

    _run_copies(i, src_tbl, cnt_tbl, loc_tbl,
                lambda g, l, size: pltpu.make_async_copy(_tile_span(slot_rows(slot), l, size),
                                                         _tile_span(rows_ref, g, size), sem.at[slot]))

    @pl.when(i == n - 1)
    def _():
        slot_done(1 - slot).wait()
        slot_done(slot).wait()


def _dispatch(runs, pend, padded, n_used, off, xa, xb, n_rows):
    tile_rows = ROUTE_TILE * SUBLANES
    tiles_a, tiles_b = xa.shape[0] // tile_rows, xb.shape[0] // tile_rows
    assert tiles_a + tiles_b >= 2
    grid_spec = pltpu.PrefetchScalarGridSpec(
        num_scalar_prefetch=6, grid=(tiles_a + tiles_b,),
        in_specs=[pl.BlockSpec((None, 1, ROUTE_TILE * TOP_K), lambda i, *_: (i, 0, 0), memory_space=pltpu.SMEM),
                  pl.BlockSpec((tile_rows, 128), lambda i, *_: (jnp.minimum(i, tiles_a - 1), 0)),
                  pl.BlockSpec((tile_rows, 128), lambda i, *_: (jnp.maximum(i - tiles_a, 0), 0))],
        out_specs=pl.BlockSpec(memory_space=pl.ANY),
        scratch_shapes=[pltpu.VMEM((2 * _SLOT_ROWS, 128), F32),
                        pltpu.VMEM((MOE_TILE * SUBLANES, 128), F32),
                        pltpu.SemaphoreType.DMA((2,)), pltpu.SemaphoreType.DMA(())])
    return pl.pallas_call(
        functools.partial(_dispatch_kernel, tiles_a=tiles_a, nblk=n_rows // MOE_TILE), grid_spec=grid_spec,
        out_shape=jax.ShapeDtypeStruct((n_rows * SUBLANES, 128), F32),
        compiler_params=_cparams("arbitrary"), name="dispatch",
    )(*runs, pend, padded, n_used, off, xa, xb)


def _moe_kernel(be_ref, nu_ref, nv_ref, next_ref, slot_ref, x_ref, wgu_hbm, bg_ref, bl_ref, wd_hbm, bd_ref,
                perm_ref, o_ref, wgu_buf, wd_buf, wg_s, wl_s, wd_s, sem_gu, sem_d):
    i = pl.program_id(0)
    used = i < nu_ref[0]
    e = be_ref[i]
    new_expert = (i == 0) | (e != be_ref[jnp.maximum(i - 1, 0)])

    def fetch(expert, s):
        return (pltpu.make_async_copy(wgu_hbm.at[expert], wgu_buf.at[s], sem_gu.at[s]),
                pltpu.make_async_copy(wd_hbm.at[expert], wd_buf.at[s], sem_d.at[s]))

    @pl.when(used & new_expert)
    def _():
        s = slot_ref[e]

        @pl.when(i == 0)
        def _():
            for cp in fetch(e, s):
                cp.start()

        @pl.when(next_ref[e] >= 0)
        def _():
            for cp in fetch(next_ref[e], 1 - s):
                cp.start()

        for cp in fetch(e, s):
            cp.wait()
        for c in range(2 * D_FF // 256):
            r = _dot(wgu_buf[s, :, 256 * c:256 * (c + 1)].astype(BF16), perm_ref[...])
            wg_s[:, 128 * c:128 * (c + 1)] = r[:, :128].astype(BF16)
            wl_s[:, 128 * c:128 * (c + 1)] = r[:, 128:].astype(BF16)
        wd_s[...] = wd_buf[s].astype(BF16)

    def expert_rows(rows):
        x = _from_row_tiles(x_ref, rows).astype(BF16)
        glu = jnp.minimum(_dot(x, wg_s[...]) + bg_ref[...], SWIGLU_LIMIT)
        lin = jnp.clip(_dot(x, wl_s[...]) + bl_ref[...], -SWIGLU_LIMIT, SWIGLU_LIMIT)
        act = glu * jax.nn.sigmoid(SWIGLU_ALPHA * glu) * (lin + 1.0)
        _to_row_tiles(o_ref, _dot(act.astype(BF16), wd_s[...]) + bd_ref[...])

    def zero_rows(first, rows):
        o_ref[pl.ds(first * SUBLANES, rows * SUBLANES), :] = jnp.zeros((rows * SUBLANES, 128), F32)

    half = MOE_TILE // 2

    @pl.when(used & (nv_ref[i] > half))
    def _():
        expert_rows(MOE_TILE)

    @pl.when(used & (nv_ref[i] <= half))
    def _():
        expert_rows(half)
        zero_rows(half, half)

    @pl.when(jnp.logical_not(used))
    def _():
        zero_rows(0, MOE_TILE)


def _moe(block_e, n_used, n_valid, next_e, e_slot, x_rows, wgu, bg, bl, wd, bd, perm):
    nblk = x_rows.shape[0] // (MOE_TILE * SUBLANES)
    row = lambda i, be, nu, *_: (jnp.minimum(i, nu[0] - 1), 0)
    wsel = lambda i, be, *_: (be[i], 0, 0)
    grid_spec = pltpu.PrefetchScalarGridSpec(
        num_scalar_prefetch=5,
        grid=(nblk,),
        in_specs=[pl.BlockSpec((MOE_TILE * SUBLANES, 128), row),
                  pl.BlockSpec(memory_space=pl.ANY),
                  pl.BlockSpec((None, 1, D_FF), wsel), pl.BlockSpec((None, 1, D_FF), wsel),
                  pl.BlockSpec(memory_space=pl.ANY), pl.BlockSpec((None, 1, D_MODEL), wsel),
                  pl.BlockSpec((256, 256), lambda i, *_: (0, 0))],
        out_specs=pl.BlockSpec((MOE_TILE * SUBLANES, 128), lambda i, *_: (i, 0)),
        scratch_shapes=[pltpu.VMEM((2, D_MODEL, 2 * D_FF), F32), pltpu.VMEM((2, D_FF, D_MODEL), F32),
                        pltpu.VMEM((D_MODEL, D_FF), BF16), pltpu.VMEM((D_MODEL, D_FF), BF16),
                        pltpu.VMEM((D_FF, D_MODEL), BF16),
                        pltpu.SemaphoreType.DMA((2,)), pltpu.SemaphoreType.DMA((2,))],
    )
    return pl.pallas_call(
        _moe_kernel,
        grid_spec=grid_spec,
        out_shape=jax.ShapeDtypeStruct(x_rows.shape, F32),
        compiler_params=_cparams("arbitrary"),
        name="moe",
    )(block_e, n_used, n_valid, next_e, e_slot, x_rows, wgu, bg, bl, wd, bd, perm)


def _combine_kernel(src_tbl, cnt_tbl, loc_tbl, off_ref, gate_ref, rows_hbm, x1_ref, o_ref, buf, ybuf, sem, *,
                    tile_base):
    i = pl.program_id(0)
    n = pl.num_programs(0)

    def slot_rows(s):
        return buf.at[pl.ds(pl.multiple_of(s * _SLOT_ROWS, _SLOT_ROWS), _SLOT_ROWS), :]

    def fetch(tile, s):
        _run_copies(tile_base + tile, src_tbl, cnt_tbl, loc_tbl,
                    lambda g, l, size: pltpu.make_async_copy(_tile_span(rows_hbm, g, size),
                                                             _tile_span(slot_rows(s), l, size), sem.at[s]))

    @pl.when(i == 0)
    def _():
        fetch(0, 0)

    @pl.when(i + 1 < n)
    def _():
        fetch(i + 1, (i + 1) % 2)

    slot = i % 2
    pltpu.make_async_copy(slot_rows(slot), slot_rows(slot), sem.at[slot]).wait()

    def body(t, c):
        acc = None
        for k in range(TOP_K):
            off = pl.multiple_of(off_ref[0, t * TOP_K + k], SUBLANES)
            v = gate_ref[0, t * TOP_K + k] * buf[pl.ds(off, SUBLANES), :]
            acc = v if acc is None else acc + v
        ybuf[pl.ds(pl.multiple_of(t * SUBLANES, SUBLANES), SUBLANES), :] = acc
        return c

    lax.fori_loop(0, ROUTE_TILE, body, 0, unroll=8)
    o_ref[...] = x1_ref[...] + _from_row_tiles(ybuf, ROUTE_TILE)


def _combine(runs, off, gates, rows, x1, tile_base):
    assert tile_base % 2 == 0
    t = x1.shape[0]
    tile_rows = ROUTE_TILE * SUBLANES
    smem_tile = pl.BlockSpec((None, 1, ROUTE_TILE * TOP_K), lambda i, *_: (tile_base + i, 0, 0),
                             memory_space=pltpu.SMEM)
    grid_spec = pltpu.PrefetchScalarGridSpec(
        num_scalar_prefetch=3,
        grid=(t // ROUTE_TILE,),
        in_specs=[smem_tile, smem_tile, pl.BlockSpec(memory_space=pl.ANY),
                  pl.BlockSpec((ROUTE_TILE, D_MODEL), lambda i, *_: (i, 0))],
        out_specs=pl.BlockSpec((ROUTE_TILE, D_MODEL), lambda i, *_: (i, 0)),
        scratch_shapes=[pltpu.VMEM((2 * _SLOT_ROWS, 128), F32), pltpu.VMEM((tile_rows, 128), F32),
                        pltpu.SemaphoreType.DMA((2,))],
    )
    return pl.pallas_call(
        functools.partial(_combine_kernel, tile_base=tile_base),
        grid_spec=grid_spec,
        out_shape=jax.ShapeDtypeStruct((t, D_MODEL), F32),
        compiler_params=_cparams("arbitrary"),
        name="combine",
    )(*runs, off, gates, rows, x1)


def _lane_pick(cols):
    lane = lax.broadcasted_iota(jnp.int32, (cols[0].shape[0], len(cols)), 1)
    out = cols[-1]
    for k in range(len(cols) - 2, -1, -1):
        out = jnp.where(lane == k, cols[k], out)
    return out


def _topk_kernel(lga_ref, lgb_ref, tri_ref, upper_ref, gate_ref, off_ref, before_ref, cnt_ref, loc_ref, carry, *,
                 tiles_a):
    i = pl.program_id(0)

    @pl.when(i == 0)
    def _():
        carry[...] = jnp.zeros_like(carry)

    l = jnp.where(i < tiles_a, lga_ref[...], lgb_ref[...])
    lane = lax.broadcasted_iota(jnp.int32, l.shape, 1)
    vals, sels = [], []
    for _ in range(TOP_K):
        m = jnp.max(l, axis=-1, keepdims=True)
        idx = jnp.min(jnp.where(l == m, lane, l.shape[1]), axis=-1, keepdims=True)
        sel = lane == idx
        l = jnp.where(sel, -jnp.inf, l)
        vals.append(m)
        sels.append(sel)
    exps = [jnp.exp(v - vals[0]) for v in vals]
    den = exps[0] + exps[1] + exps[2] + exps[3]
    onehot = jnp.where(sels[0] | sels[1] | sels[2] | sels[3], 1.0, 0.0)
    within = _dot(tri_ref[...], onehot.astype(BF16))
    cnt = jnp.sum(onehot, axis=0, keepdims=True)
    cnt_hi = jnp.floor(cnt * (1.0 / 16.0))
    cnt_lo = cnt - 16.0 * cnt_hi
    loc = 16.0 * _dot(cnt_hi.astype(BF16), upper_ref[...]) + _dot(cnt_lo.astype(BF16), upper_ref[...])
    slot_base = ((i % 2) * (ROUTE_TILE * TOP_K)).astype(F32)
    rows = [jnp.sum(jnp.where(s, within + loc, 0.0), axis=-1, keepdims=True) for s in sels]
    gate_ref[...] = _lane_pick([e / den for e in exps])
    off_ref[...] = ((_lane_pick(rows) + slot_base) * SUBLANES).astype(jnp.int32)
    before_ref[...] = carry[...]
    cnt_ref[...] = cnt
    loc_ref[...] = loc
    carry[...] = carry[...] + cnt


def _route(lga, lgb):
    tr = ROUTE_TILE
    tiles_a, tiles_b = lga.shape[0] // tr, lgb.shape[0] // tr
    n_tiles = tiles_a + tiles_b
    t = n_tiles * tr
    n_assign = t * TOP_K
    row = lambda i: (i, 0)
    fix = lambda i: (0, 0)
    sd = jax.ShapeDtypeStruct
    k_spec = pl.BlockSpec((tr, TOP_K), row)
    t_spec = pl.BlockSpec((None, 1, 128), lambda i: (i, 0, 0))
    tri = (jnp.arange(tr)[:, None] > jnp.arange(tr)[None, :]).astype(BF16)
    upper = (jnp.arange(128)[:, None] < jnp.arange(128)[None, :]).astype(BF16)
    gates, off, before, cnt, loc = pl.pallas_call(
        functools.partial(_topk_kernel, tiles_a=tiles_a),
        grid=(n_tiles,),
        in_specs=[pl.BlockSpec((tr, 128), lambda i: (jnp.minimum(i, tiles_a - 1), 0)),
                  pl.BlockSpec((tr, 128), lambda i: (jnp.maximum(i - tiles_a, 0), 0)),
                  pl.BlockSpec((tr, tr), fix), pl.BlockSpec((128, 128), fix)],
        out_specs=[k_spec, k_spec, t_spec, t_spec, t_spec],
        out_shape=[sd((t, TOP_K), F32), sd((t, TOP_K), jnp.int32),
                   sd((n_tiles, 1, 128), F32), sd((n_tiles, 1, 128), F32), sd((n_tiles, 1, 128), F32)],
        scratch_shapes=[pltpu.VMEM((1, 128), F32)],
        compiler_params=_cparams("arbitrary"),
        name="topk",
    )(lga, lgb, tri, upper)
    table = lambda a: a[:, 0, :N_EXPERTS].astype(jnp.int32)
    before, cnt, run_loc = table(before), table(cnt), table(loc)
    counts = before[-1] + cnt[-1]
    padded = ((counts + MOE_TILE - 1) // MOE_TILE) * MOE_TILE
    pend = jnp.cumsum(padded)
    pstart = pend - padded
    run_src = pstart[None, :] + before
    nblk = (n_assign + MOE_TILE - 1) // MOE_TILE + N_EXPERTS
    n_used = (pend[-1] // MOE_TILE).astype(jnp.int32)
    block_start = jnp.arange(nblk, dtype=jnp.int32) * MOE_TILE
    block_e = jnp.sum(pend[None, :] <= jnp.minimum(block_start, pend[-1] - 1)[:, None], axis=1)
    block_e = jnp.minimum(block_e, N_EXPERTS - 1).astype(jnp.int32)
    region_end = jnp.sum(jnp.where(block_e[:, None] == jnp.arange(N_EXPERTS)[None, :], (pstart + counts)[None, :], 0),
                         axis=1)
    n_valid = jnp.clip(region_end - block_start, 0, MOE_TILE).astype(jnp.int32)
    ids = jnp.arange(N_EXPERTS, dtype=jnp.int32)
    nonempty = counts > 0
    next_e = jnp.min(jnp.where((ids[None, :] > ids[:, None]) & nonempty[None, :], ids[None, :], N_EXPERTS), axis=1)
    next_e = jnp.where(next_e == N_EXPERTS, -1, next_e).astype(jnp.int32)
    e_slot = ((jnp.cumsum(nonempty.astype(jnp.int32)) - nonempty.astype(jnp.int32)) % 2).astype(jnp.int32)
    blocks = (block_e, n_used.reshape(1), n_valid, next_e, e_slot)
    runs = (run_src.reshape(-1), cnt.reshape(-1), run_loc.reshape(-1))
    per_tile_smem = lambda a: a.reshape(n_tiles, 1, tr * TOP_K)
    return per_tile_smem(gates), per_tile_smem(off), runs, blocks, pend, padded, nblk * MOE_TILE


def kernel(x_prompt, x_sample, cache_k, cache_v, state_ssm_re, state_ssm_im, norm_mix_g, w_in, q_norm_g, k_norm_g, attn_sinks, ssm_lambda_re, ssm_lambda_im, ssm_b_re, ssm_b_im, ssm_c_re, ssm_c_im, ssm_d, ssm_log_dt, w_glu, b_glu, attn_out_norm_g, ssm_out_norm_g, w_out, norm_ffn_g, w_router, b_router, w_gate_up, b_gate_up, w_down, b_down):
    depth = w_in.shape[0]
    assert depth == 1
    bp, sp, _ = x_prompt.shape
    bs, ss, _ = x_sample.shape
    tp, ts = bp * sp, bs * ss
    assert bp == SUBLANES and sp % S5_CHUNK == 0 and bs % SUBLANES == 0

    perm = jnp.asarray(_Q_PERM)
    w_in0 = w_in[0]
    wq = w_in0[:, :D_ATTN].reshape(D_MODEL, N_HEADS, HEAD_DIM)[:, perm].reshape(D_MODEL, D_ATTN)
    w_in_b = jnp.concatenate([wq, w_in0[:, D_ATTN:]], axis=1).astype(BF16)
    qkg = jnp.concatenate([jnp.tile(q_norm_g[0], N_HEADS), jnp.tile(k_norm_g[0], N_KV_HEADS)])[None]
    pmat = jnp.kron(jnp.eye(256 // HEAD_DIM, dtype=F32),
                    jnp.full((HEAD_DIM, HEAD_DIM), 1.0 / HEAD_DIM, F32)).astype(BF16)
    g_mix = norm_mix_g[0][None]
    sinks = attn_sinks[0]
    g_attn = attn_out_norm_g[0].reshape(N_HEADS, HEAD_DIM)[perm].reshape(1, D_ATTN)
    w_out0 = w_out[0]
    w_out_a = w_out0[:D_ATTN].reshape(N_HEADS, HEAD_DIM, D_MODEL)[perm].reshape(D_ATTN, D_MODEL).astype(BF16)
    w_out_s = w_out0[D_ATTN:].astype(BF16)
    g_ssm = ssm_out_norm_g[0][None]
    g_ffn = norm_ffn_g[0][None]
    w_r = jnp.pad(w_router[0], ((0, 0), (0, 128 - N_EXPERTS))).astype(BF16)
    b_r = jnp.pad(b_router[0], (0, 128 - N_EXPERTS), constant_values=NEG_INF)[None]

    a_re, a_im, bb_re, bb_im = _s5_prep(
        ssm_lambda_re[0], ssm_lambda_im[0], ssm_log_dt[0][:, None],
        jnp.swapaxes(ssm_b_re[0], 1, 2), jnp.swapaxes(ssm_b_im[0], 1, 2))
    eye_g = jnp.eye(N_SSM_GROUPS, dtype=F32)
    bd_b = lambda bb: jnp.einsum("ghp,gk->ghkp", bb, eye_g).reshape(D_SSM, D_STATE)
    wb = jnp.concatenate([bd_b(bb_re), bd_b(bb_im)], axis=1).astype(BF16)
    bd_c = lambda c: jnp.einsum("ghp,gk->gpkh", c, eye_g).reshape(D_STATE, D_SSM)
    wc = jnp.concatenate([bd_c(ssm_c_re[0]), -bd_c(ssm_c_im[0])], axis=0).astype(BF16)
    a_re, a_im = a_re.reshape(1, D_STATE), a_im.reshape(1, D_STATE)
    d_skip = ssm_d[0].reshape(1, D_SSM)
    w_glu_b = w_glu[0].astype(BF16)
    b_glu0 = b_glu[0][None]

    b_g = b_gate_up[0][:, None, 0::2]
    b_l = b_gate_up[0][:, None, 1::2]
    b_d = b_down[0][:, None, :]
    idx = jnp.arange(256)
    deint = (idx[None, :] == jnp.where(idx % 2 == 0, idx // 2, 128 + idx // 2)[:, None]).astype(BF16)

    xp2 = x_prompt.reshape(tp, D_MODEL)
    xs2 = x_sample.reshape(ts, D_MODEL)
    qp, kp, vp, up = _in_proj(xp2, g_mix, w_in_b, qkg, pmat)
    qs, ks, vs, us = _in_proj(xs2, g_mix, w_in_b, qkg, pmat)

    kp3, vp3 = kp.reshape(bp, sp, D_KV), vp.reshape(bp, sp, D_KV)
    attn_p = _band_attention(sinks, qp.reshape(bp, sp, D_ATTN), kp3, vp3, g_attn).reshape(tp, D_ATTN)
    ck = cache_k[0].reshape(bs, -1, D_KV)
    cv = cache_v[0].reshape(bs, -1, D_KV)
    attn_s = _cache_attention(sinks, qs, ks, vs, ck, cv, g_attn, ss)

    zeros_p = jnp.zeros((bp, D_STATE), F32)
    s5_args = (a_re, a_im, wb, wc, d_skip, w_glu_b, b_glu0, g_ssm)
    ssm_p, hr_p, hi_p = _s5(up.reshape(bp, sp, D_SSM), zeros_p, zeros_p, *s5_args,
                            tt=S5_CHUNK, time_chunked=True)
    ssm_s, hr_s, hi_s = _s5(us.reshape(bs // SUBLANES, SUBLANES * ss, D_SSM),
                            state_ssm_re[0].reshape(bs, D_STATE), state_ssm_im[0].reshape(bs, D_STATE),
                            *s5_args, tt=ss, time_chunked=False)

    x1p, xtp, lgp = _out_proj(attn_p, ssm_p.reshape(tp, D_SSM), xp2, w_out_a, w_out_s, g_ffn, w_r, b_r)
    x1s, xts, lgs = _out_proj(attn_s, ssm_s.reshape(ts, D_SSM), xs2, w_out_a, w_out_s, g_ffn, w_r, b_r)

    gates, off, runs, blocks, pend, padded, n_rows = _route(lgp, lgs)
    x_rows = _dispatch(runs, pend, padded, blocks[1], off, xtp, xts, n_rows)
    out_rows = _moe(*blocks, x_rows, w_gate_up[0], b_g, b_l, w_down[0], b_d, deint)
    yp = _combine(runs, off, gates, out_rows, x1p, 0).reshape(bp, sp, D_MODEL)
    ys = _combine(runs, off, gates, out_rows, x1s, tp // ROUTE_TILE).reshape(bs, ss, D_MODEL)

    kv5 = lambda a, b_: a.reshape(b_, -1, N_KV_HEADS, HEAD_DIM)
    new_kp = kv5(kp3[:, -WINDOW:], bp)[None]
    new_vp = kv5(vp3[:, -WINDOW:], bp)[None]
    ks3, vs3 = ks.reshape(bs, ss, D_KV), vs.reshape(bs, ss, D_KV)
    new_ks = kv5(jnp.concatenate([ck, ks3], axis=1)[:, ss:], bs)[None]
    new_vs = kv5(jnp.concatenate([cv, vs3], axis=1)[:, ss:], bs)[None]
    st = lambda h, b_: h.reshape(1, b_, N_SSM_GROUPS, SSM_STATE)
    return (yp, ys, new_kp, new_vp, st(hr_p, bp), st(hi_p, bp),
            new_ks, new_vs, st(hr_s, bs), st(hi_s, bs))
```

```python
import functools
import math

import jax
import jax.numpy as jnp
from jax import lax
from jax.experimental import pallas as pl
from jax.experimental.pallas import tpu as pltpu

F32 = jnp.float32
BF16 = jnp.bfloat16

D_MODEL = 1024
D_ATTN = 512
D_SSM = 512
HEAD_DIM = 64
N_HEADS = 8
N_KV_HEADS = 2
D_KV = N_KV_HEADS * HEAD_DIM
WINDOW = 128
SSM_GROUP = 16
N_SSM_GROUPS = 32
SSM_STATE = 64
D_STATE = N_SSM_GROUPS * SSM_STATE
N_EXPERTS = 32
TOP_K = 4
D_FF = 1024
SWIGLU_LIMIT = 7.0
SWIGLU_ALPHA = 1.702
RMS_EPS = 1e-6
NEG_INF = -1e30
D_IN_PROJ = D_ATTN + 2 * D_KV + D_SSM
D_QK = D_ATTN + D_KV

SUBLANES = 8
VMEM_LIMIT = 56 * 1024 * 1024

ROW_TILE = 512
MOE_TILE = 512
ROUTE_TILE = 512
_SLOT_ROWS = ROUTE_TILE * TOP_K * SUBLANES
_RUN_SIZES = tuple(1 << b for b in range(ROUTE_TILE.bit_length() - 1, -1, -1))
S5_CHUNK = 128
S5_COLS = 512

_Q_PERM = (0, 4, 1, 5, 2, 6, 3, 7)


def _cparams(*sem):
    return pltpu.CompilerParams(dimension_semantics=sem, vmem_limit_bytes=VMEM_LIMIT)


def _dot(a, b):
    return jnp.dot(a, b, preferred_element_type=F32)


def _rms(x):
    return x * lax.rsqrt(jnp.mean(x * x, axis=-1, keepdims=True) + RMS_EPS)


def _in_proj_kernel(x_ref, g_ref, w_ref, qkg_ref, p_ref, q_ref, k_ref, v_ref, u_ref):
    xn = _rms(x_ref[...]) * g_ref[...]
    h = _dot(xn.astype(BF16), w_ref[...])
    qk = h[:, :D_QK]
    sq = (qk * qk).astype(BF16)
    p = p_ref[...]
    ms = jnp.concatenate(
        [_dot(sq[:, 0:256], p), _dot(sq[:, 256:512], p), _dot(sq[:, 512:640], p[:128, :128])],
        axis=-1)
    qkn = qk * lax.rsqrt(ms + RMS_EPS) * qkg_ref[...]
    q_ref[...] = qkn[:, :D_ATTN].astype(BF16)
    k_ref[...] = qkn[:, D_ATTN:]
    v_ref[...] = h[:, D_QK:D_QK + D_KV]
    u_ref[...] = h[:, D_QK + D_KV:]


def _in_proj(x2d, g, w, qkg, pmat):
    t = x2d.shape[0]
    tm = min(ROW_TILE, t)
    row = lambda i: (i, 0)
    fix = lambda i: (0, 0)
    return pl.pallas_call(
        _in_proj_kernel,
        grid=(t // tm,),
        in_specs=[pl.BlockSpec((tm, D_MODEL), row), pl.BlockSpec((1, D_MODEL), fix),
                  pl.BlockSpec((D_MODEL, D_IN_PROJ), fix), pl.BlockSpec((1, D_QK), fix),
                  pl.BlockSpec((256, 256), fix)],
        out_specs=[pl.BlockSpec((tm, D_ATTN), row), pl.BlockSpec((tm, D_KV), row),
                   pl.BlockSpec((tm, D_KV), row), pl.BlockSpec((tm, D_SSM), row)],
        out_shape=[jax.ShapeDtypeStruct((t, D_ATTN), BF16), jax.ShapeDtypeStruct((t, D_KV), F32),
                   jax.ShapeDtypeStruct((t, D_KV), F32), jax.ShapeDtypeStruct((t, D_SSM), F32)],
        compiler_params=_cparams("parallel"),
        name="in_proj",
    )(x2d, g, w, qkg, pmat)


def _softmax_pv(s_blocks, v_blocks, sink):
    m = sink
    for s in s_blocks:
        m = jnp.maximum(m, jnp.max(s, axis=-1, keepdims=True))
    den = jnp.exp(sink - m)
    acc = None
    for s, v in zip(s_blocks, v_blocks):
        p = jnp.exp(s - m)
        den = den + jnp.sum(p, axis=-1, keepdims=True)
        pv = _dot(p.astype(BF16), v)
        acc = pv if acc is None else acc + pv
    return acc / den


def _band_attn_kernel(sink_ref, q_ref, kp_ref, kc_ref, vp_ref, vc_ref, g_ref, o_ref):
    i = pl.program_id(1)
    q = q_ref[...]
    kb = jnp.concatenate([kp_ref[...], kc_ref[...]], axis=0).astype(BF16)
    vb = jnp.concatenate([vp_ref[...], vc_ref[...]], axis=0).astype(BF16)
    row = lax.broadcasted_iota(jnp.int32, (WINDOW, 2 * WINDOW), 0)
    col = lax.broadcasted_iota(jnp.int32, (WINDOW, 2 * WINDOW), 1)
    mask = (col > row) & (col <= row + WINDOW) & ((col >= WINDOW) | (i > 0))
    lane = lax.broadcasted_iota(jnp.int32, (WINDOW, 128), 1)
    low = lane < HEAD_DIM
    zero = jnp.zeros((), BF16)
    outs = []
    for pair in range(N_HEADS // 2):
        qp = q[:, 128 * pair:128 * (pair + 1)]
        halves = []
        for par in range(2):
            qm = jnp.where(low if par == 0 else ~low, qp, zero)
            s = lax.dot_general(qm, kb, (((1,), (1,)), ((), ())), preferred_element_type=F32)
            s = jnp.where(mask, s * (HEAD_DIM ** -0.5), NEG_INF)
            halves.append(_softmax_pv([s], [vb], sink_ref[_Q_PERM[2 * pair + par]]))
        outs.append(jnp.where(low, halves[0], halves[1]))
    o = jnp.concatenate(outs, axis=-1)
    o_ref[...] = (_rms(o) * g_ref[...]).astype(BF16)


def _band_attention(sinks, q, k, v, g):
    b, s, _ = q.shape
    nb = s // WINDOW
    cur = lambda bi, i: (bi, i, 0)
    prev = lambda bi, i: (bi, jnp.maximum(i - 1, 0), 0)
    kv_spec = lambda im: pl.BlockSpec((None, WINDOW, D_KV), im)
    return pl.pallas_call(
        _band_attn_kernel,
        grid=(b, nb),
        in_specs=[pl.BlockSpec(memory_space=pltpu.SMEM),
                  pl.BlockSpec((None, WINDOW, D_ATTN), cur),
                  kv_spec(prev), kv_spec(cur), kv_spec(prev), kv_spec(cur),
                  pl.BlockSpec((1, D_ATTN), lambda bi, i: (0, 0))],
        out_specs=pl.BlockSpec((None, WINDOW, D_ATTN), cur),
        out_shape=jax.ShapeDtypeStruct((b, s, D_ATTN), BF16),
        compiler_params=_cparams("parallel", "parallel"),
        name="band_attn",
    )(sinks, q, k, k, v, v, g)


_PAIR_ROWS = 8
_CACHE_BB = 8


def _cache_attn_kernel(sink_ref, q_ref, kn_ref, vn_ref, ck_ref, cv_ref, g_ref, o_ref, nk_ref, nv_ref, *, n_new):
    n_buf = ck_ref.shape[1]
    rows_blk = _CACHE_BB * n_new
    for b in range(_CACHE_BB):
        for new_ref, cache_ref, fresh_ref in ((nk_ref, ck_ref, kn_ref), (nv_ref, cv_ref, vn_ref)):
            new_ref[b, 0:n_buf - n_new, :] = cache_ref[b, n_new:n_buf, :]
            new_ref[b, n_buf - n_new:n_buf, :] = fresh_ref[n_new * b:n_new * (b + 1), :]
    knew = kn_ref[...].astype(BF16)
    vnew = vn_ref[...].astype(BF16)
    lane = lax.broadcasted_iota(jnp.int32, (_PAIR_ROWS, 128), 1)
    low = lane < HEAD_DIM
    zero = jnp.zeros((), BF16)
    n_stack = N_HEADS * _PAIR_ROWS
    r = lax.broadcasted_iota(jnp.int32, (n_stack, 1), 0) % _PAIR_ROWS
    r_seq, r_tok = r // n_new, r % n_new
    colc = lax.broadcasted_iota(jnp.int32, (n_stack, 2 * n_buf), 1)
    c_seq, c_pos = colc // n_buf, colc % n_buf
    mask_c = (c_seq == r_seq) & (c_pos + WINDOW > r_tok + n_buf)
    coln = lax.broadcasted_iota(jnp.int32, (n_stack, rows_blk), 1)
    sink_col = jnp.concatenate(
        [jnp.full((_PAIR_ROWS, 1), sink_ref[_Q_PERM[h]], F32) for h in range(N_HEADS)], axis=0)
    for sp in range(_CACHE_BB // 2):
        q = q_ref[_PAIR_ROWS * sp:_PAIR_ROWS * (sp + 1), :]
        pieces = []
        for pair in range(N_HEADS // 2):
            qp = q[:, 128 * pair:128 * (pair + 1)]
            pieces.append(jnp.where(low, qp, zero))
            pieces.append(jnp.where(low, zero, qp))
        qs = jnp.concatenate(pieces, axis=0)
        kc = jnp.concatenate([ck_ref[2 * sp], ck_ref[2 * sp + 1]], axis=0).astype(BF16)
        vc = jnp.concatenate([cv_ref[2 * sp], cv_ref[2 * sp + 1]], axis=0).astype(BF16)
        nt = (((1,), (1,)), ((), ()))
        s_c = lax.dot_general(qs, kc, nt, preferred_element_type=F32) * (HEAD_DIM ** -0.5)
        s_n = lax.dot_general(qs, knew, nt, preferred_element_type=F32) * (HEAD_DIM ** -0.5)
        s_c = jnp.where(mask_c, s_c, NEG_INF)
        n_seq, n_tok = coln // n_new - 2 * sp, coln % n_new
        mask_n = (n_seq == r_seq) & (n_tok <= r_tok)
        s_n = jnp.where(mask_n, s_n, NEG_INF)
        o = _softmax_pv([s_c, s_n], [vc, vnew], sink_col)
        outs = [jnp.where(low, o[16 * pair:16 * pair + 8], o[16 * pair + 8:16 * pair + 16])
                for pair in range(N_HEADS // 2)]
        oo = jnp.concatenate(outs, axis=-1)
        o_ref[_PAIR_ROWS * sp:_PAIR_ROWS * (sp + 1), :] = (_rms(oo) * g_ref[...]).astype(BF16)


def _cache_attention(sinks, q, k, v, cache_k, cache_v, g, n_new):
    t = q.shape[0]
    nb, n_buf, _ = cache_k.shape
    assert n_new * 2 == _PAIR_ROWS and nb % _CACHE_BB == 0 and n_buf == WINDOW
    rows = _CACHE_BB * n_new
    row = lambda i: (i, 0)
    cache_spec = pl.BlockSpec((_CACHE_BB, n_buf, D_KV), lambda i: (i, 0, 0))
    return pl.pallas_call(
        functools.partial(_cache_attn_kernel, n_new=n_new),
        grid=(nb // _CACHE_BB,),
        in_specs=[pl.BlockSpec(memory_space=pltpu.SMEM),
                  pl.BlockSpec((rows, D_ATTN), row), pl.BlockSpec((rows, D_KV), row),
                  pl.BlockSpec((rows, D_KV), row), cache_spec, cache_spec,
                  pl.BlockSpec((1, D_ATTN), lambda i: (0, 0))],
        out_specs=[pl.BlockSpec((rows, D_ATTN), row), cache_spec, cache_spec],
        out_shape=[jax.ShapeDtypeStruct((t, D_ATTN), BF16), jax.ShapeDtypeStruct(cache_k.shape, F32),
                   jax.ShapeDtypeStruct(cache_v.shape, F32)],
        compiler_params=_cparams("parallel"),
        name="cache_attn",
    )(sinks, q, k, v, cache_k, cache_v, g)


def _s5_prep_kernel(lre_ref, lim_ref, ldt_ref, bre_ref, bim_ref, are_ref, aim_ref, bbre_ref, bbim_ref):
    dt = jnp.exp(ldt_ref[...])
    l_re = jnp.minimum(lre_ref[...], -1e-4)
    l_im = lim_ref[...]
    mag = jnp.exp(l_re * dt)
    a_re = mag * jnp.cos(l_im * dt)
    a_im = mag * jnp.sin(l_im * dt)
    den = l_re * l_re + l_im * l_im
    n_re = a_re - 1.0
    z_re = (n_re * l_re + a_im * l_im) / den
    z_im = (a_im * l_re - n_re * l_im) / den
    are_ref[...] = a_re
    aim_ref[...] = a_im
    br, bi = bre_ref[...], bim_ref[...]
    zr, zi = z_re[:, None, :], z_im[:, None, :]
    bbre_ref[...] = zr * br - zi * bi
    bbim_ref[...] = zr * bi + zi * br


def _s5_prep(lam_re, lam_im, log_dt, b_re_t, b_im_t):
    g, p = lam_re.shape
    sd = jax.ShapeDtypeStruct
    return pl.pallas_call(
        _s5_prep_kernel,
        out_shape=[sd((g, p), F32), sd((g, p), F32), sd(b_re_t.shape, F32), sd(b_re_t.shape, F32)],
        name="s5_prep",
    )(lam_re, lam_im, log_dt, b_re_t, b_im_t)


def _s5_kernel(u_ref, h0r_ref, h0i_ref, ar_ref, ai_ref, wb_ref, wc_ref, d_ref, wglu_ref, bglu_ref,
               g_ref, o_ref, hr_ref, hi_ref, bu_ref, hs_ref, usc_ref, ysc_ref, *, tt):
    j = pl.program_id(1)
    rows = SUBLANES * tt
    n_tiles = D_STATE // 128
    time_major = tt % SUBLANES == 0

    @pl.when(j == 0)
    def _():
        hs_ref[:, :D_STATE] = h0r_ref[...]
        hs_ref[:, D_STATE:] = h0i_ref[...]

    u = u_ref[...].reshape(rows, D_SSM)
    if time_major:
        for c in range(D_SSM // 128):
            for b in range(SUBLANES):
                usc_ref[c, pl.ds(b, tt, stride=SUBLANES), :] = u[b * tt:(b + 1) * tt, 128 * c:128 * (c + 1)]
        ub = jnp.concatenate([usc_ref[c] for c in range(D_SSM // 128)], axis=-1).astype(BF16)
    else:
        ub = u.astype(BF16)

    def step_rows(t):
        if time_major:
            return pl.ds(pl.multiple_of(t * SUBLANES, SUBLANES), SUBLANES)
        return pl.ds(t, SUBLANES, stride=tt)

    for n in range(2 * D_STATE // 256):
        band = (n % (D_STATE // 256)) // 2
        res = _dot(ub[:, 128 * band:128 * (band + 1)],
                   wb_ref[128 * band:128 * (band + 1), 256 * n:256 * (n + 1)])
        bu_ref[2 * n] = res[:, :128]
        bu_ref[2 * n + 1] = res[:, 128:]

    tiles_per_pass = S5_COLS // 128
    for c0 in range(0, n_tiles, tiles_per_pass):
        tiles = range(c0, c0 + tiles_per_pass)
        a_r = [jnp.broadcast_to(ar_ref[:, 128 * c:128 * (c + 1)], (SUBLANES, 128)) for c in tiles]
        a_i = [jnp.broadcast_to(ai_ref[:, 128 * c:128 * (c + 1)], (SUBLANES, 128)) for c in tiles]

        def step(t, carry, tiles=tiles, a_r=a_r, a_i=a_i):
            at_t = step_rows(t)
            out = []
            for k, c in enumerate(tiles):
                h_r, h_i = carry[2 * k], carry[2 * k + 1]
                n_r = a_r[k] * h_r - a_i[k] * h_i + bu_ref[c, at_t, :]
                n_i = a_r[k] * h_i + a_i[k] * h_r + bu_ref[n_tiles + c, at_t, :]
                bu_ref[c, at_t, :] = n_r
                bu_ref[n_tiles + c, at_t, :] = n_i
                out += [n_r, n_i]
            return tuple(out)

        init = []
        for c in tiles:
            init += [hs_ref[:, 128 * c:128 * (c + 1)], hs_ref[:, D_STATE + 128 * c:D_STATE + 128 * (c + 1)]]
        fin = lax.fori_loop(0, tt, step, tuple(init), unroll=min(tt, 8))
        for k, c in enumerate(tiles):
            hs_ref[:, 128 * c:128 * (c + 1)] = fin[2 * k]
            hs_ref[:, D_STATE + 128 * c:D_STATE + 128 * (c + 1)] = fin[2 * k + 1]

    def h_cols(first_tile):
        return jnp.concatenate([bu_ref[first_tile + k] for k in range(4)], axis=-1).astype(BF16)

    ys = []
    for m in range(D_SSM // 128):
        y = _dot(h_cols(4 * m), wc_ref[512 * m:512 * (m + 1), 128 * m:128 * (m + 1)])
        y = y + _dot(h_cols(n_tiles + 4 * m),
                     wc_ref[D_STATE + 512 * m:D_STATE + 512 * (m + 1), 128 * m:128 * (m + 1)])
        if time_major:
            ysc_ref[m] = y
            y = jnp.concatenate([ysc_ref[m, pl.ds(b, tt, stride=SUBLANES), :] for b in range(SUBLANES)], axis=0)
        ys.append(y)
    y = jnp.concatenate(ys, axis=-1) + d_ref[...] * u
    z = _dot(jax.nn.gelu(y).astype(BF16), wglu_ref[...]) + bglu_ref[...]
    s = z[:, :D_SSM] * jax.nn.sigmoid(z[:, D_SSM:])
    o_ref[...] = (_rms(s) * g_ref[...]).astype(BF16).reshape(o_ref.shape)

    @pl.when(j == pl.num_programs(1) - 1)
    def _():
        hr_ref[...] = hs_ref[:, :D_STATE]
        hi_ref[...] = hs_ref[:, D_STATE:]


def _s5(u, h0r, h0i, a_re, a_im, wb, wc, d, wglu, bglu, g, *, tt, time_chunked):
    nbg = h0r.shape[0] // SUBLANES
    if time_chunked:
        nchunks = u.shape[1] // tt
        u_spec = pl.BlockSpec((SUBLANES, tt, D_SSM), lambda gi, j: (gi, j, 0))
    else:
        nchunks = 1
        u_spec = pl.BlockSpec((None, SUBLANES * tt, D_SSM), lambda gi, j: (gi, 0, 0))
    fix = lambda gi, j: (0, 0)
    st_spec = pl.BlockSpec((SUBLANES, D_STATE), lambda gi, j: (gi, 0))
    sd = jax.ShapeDtypeStruct
    return pl.pallas_call(
        functools.partial(_s5_kernel, tt=tt),
        grid=(nbg, nchunks),
        in_specs=[u_spec, st_spec, st_spec,
                  pl.BlockSpec((1, D_STATE), fix), pl.BlockSpec((1, D_STATE), fix),
                  pl.BlockSpec((D_SSM, 2 * D_STATE), fix), pl.BlockSpec((2 * D_STATE, D_SSM), fix),
                  pl.BlockSpec((1, D_SSM), fix), pl.BlockSpec((D_SSM, 2 * D_SSM), fix),
                  pl.BlockSpec((1, 2 * D_SSM), fix), pl.BlockSpec((1, D_SSM), fix)],
        out_specs=[u_spec, st_spec, st_spec],
        out_shape=[sd(u.shape, BF16), sd(h0r.shape, F32), sd(h0r.shape, F32)],
        scratch_shapes=[pltpu.VMEM((2 * D_STATE // 128, SUBLANES * tt, 128), F32),
                        pltpu.VMEM((SUBLANES, 2 * D_STATE), F32),
                        pltpu.VMEM((D_SSM // 128, SUBLANES * tt, 128), F32),
                        pltpu.VMEM((D_SSM // 128, SUBLANES * tt, 128), F32)],
        compiler_params=_cparams("parallel", "arbitrary"),
        name="s5",
    )(u, h0r, h0i, a_re, a_im, wb, wc, d, wglu, bglu, g)


def _to_row_tiles(ref, val, first=0):
    rows = val.shape[0]
    for c in range(D_MODEL // 128):
        ref[pl.ds(first * SUBLANES + c, rows, stride=SUBLANES), :] = val[:, 128 * c:128 * (c + 1)]


def _from_row_tiles(ref, rows, lead=(), first=0):
    return jnp.concatenate(
        [ref[(*lead, pl.ds(first * SUBLANES + c, rows, stride=SUBLANES), slice(None))]
         for c in range(D_MODEL // 128)], axis=-1)


def _out_proj_kernel(a_ref, s_ref, x_ref, wa_ref, ws_ref, g_ref, wr_ref, br_ref, x1_ref, xt_ref, lg_ref):
    x1 = x_ref[...] + _dot(a_ref[...], wa_ref[...]) + _dot(s_ref[...], ws_ref[...])
    x1_ref[...] = x1
    xn = _rms(x1) * g_ref[...]
    _to_row_tiles(xt_ref, xn)
    lg_ref[...] = _dot(xn.astype(BF16), wr_ref[...]) + br_ref[...]


def _out_proj(attn_n, ssm_n, x2d, wa, ws, g, wr, br):
    t = x2d.shape[0]
    tm = min(ROW_TILE, t)
    row = lambda i: (i, 0)
    fix = lambda i: (0, 0)
    sd = jax.ShapeDtypeStruct
    return pl.pallas_call(
        _out_proj_kernel,
        grid=(t // tm,),
        in_specs=[pl.BlockSpec((tm, D_ATTN), row), pl.BlockSpec((tm, D_SSM), row),
                  pl.BlockSpec((tm, D_MODEL), row),
                  pl.BlockSpec((D_ATTN, D_MODEL), fix), pl.BlockSpec((D_SSM, D_MODEL), fix),
                  pl.BlockSpec((1, D_MODEL), fix), pl.BlockSpec((D_MODEL, 128), fix),
                  pl.BlockSpec((1, 128), fix)],
        out_specs=[pl.BlockSpec((tm, D_MODEL), row), pl.BlockSpec((tm * SUBLANES, 128), row),
                   pl.BlockSpec((tm, 128), row)],
        out_shape=[sd((t, D_MODEL), F32), sd((t * SUBLANES, 128), F32), sd((t, 128), F32)],
        compiler_params=_cparams("parallel"),
        name="out_proj",
    )(attn_n, ssm_n, x2d, wa, ws, g, wr, br)


def _tile_span(ref, first_row, n_rows, lead=()):
    start = pl.multiple_of(first_row * SUBLANES, SUBLANES)
    return ref.at[(*lead, pl.ds(start, n_rows * SUBLANES), slice(None))]


def _run_copies(tile, src_tbl, cnt_tbl, loc_tbl, make_copy):
    def body(e, c):
        j = tile * N_EXPERTS + e
        cnt, src, loc = cnt_tbl[j], src_tbl[j], loc_tbl[j]
        off = 0
        for size in _RUN_SIZES:
            @pl.when((cnt & size) != 0)
            def _(off=off, size=size):
                make_copy(src + off, loc + off, size).start()
            off = off + (cnt & size)
        return c

    lax.fori_loop(0, N_EXPERTS, body, 0)


def _dispatch_kernel(src_tbl, cnt_tbl, loc_tbl, pend_ref, padded_ref, nu_ref, off_ref, xa_ref, xb_ref, rows_ref,
                     buf, zbuf, sem, zsem, *, tiles_a, nblk):
    i = pl.program_id(0)
    n = pl.num_programs(0)
    slot = i % 2

    def zero_block(start_row):
        return pltpu.make_async_copy(zbuf, _tile_span(rows_ref, start_row, MOE_TILE), zsem)

    @pl.when(i == 0)
    def _():
        zbuf[...] = jnp.zeros_like(zbuf)
        for e in range(N_EXPERTS):
            @pl.when(padded_ref[e] > 0)
            def _(e=e):
                zero_block(pend_ref[e] - MOE_TILE).start()

        def tail_start(j, c):
            zero_block(j * MOE_TILE).start()
            return c

        def tail_wait(j, c):
            zero_block(j * MOE_TILE).wait()
            return c

        lax.fori_loop(nu_ref[0], nblk, tail_start, 0)
        for e in range(N_EXPERTS):
            @pl.when(padded_ref[e] > 0)
            def _(e=e):
                zero_block(pend_ref[e] - MOE_TILE).wait()
        lax.fori_loop(nu_ref[0], nblk, tail_wait, 0)

    def slot_rows(s):
        return buf.at[pl.ds(pl.multiple_of(s * _SLOT_ROWS, _SLOT_ROWS), _SLOT_ROWS), :]

    def slot_done(s):
        return pltpu.make_async_copy(slot_rows(s), slot_rows(s), sem.at[s])

    @pl.when(i >= 2)
    def _():
        slot_done(slot).wait()

    def fill(x_ref):
        def body(t, c):
            v = x_ref[pl.ds(pl.multiple_of(t * SUBLANES, SUBLANES), SUBLANES), :]
            for k in range(TOP_K):
                off = pl.multiple_of(off_ref[0, k * ROUTE_TILE + t], SUBLANES)
                buf[pl.ds(off, SUBLANES), :] = v
            return c

        lax.fori_loop(0, ROUTE_TILE, body, 0, unroll=8)

    @pl.when(i < tiles_a)
    def _():
        fill(xa_ref)

    @pl.when(i >= tiles_a)
    def _():
        fill(xb_ref)

    _run_copies(i, src_tbl, cnt_tbl, loc_tbl,
                lambda g, l, size: pltpu.make_async_copy(_tile_span(slot_rows(slot), l, size),
                                                         _tile_span(rows_ref, g, size), sem.at[slot]))

    @pl.when(i == n - 1)
    def _():
        slot_done(1 - slot).wait()
        slot_done(slot).wait()


def _dispatch(runs, pend, padded, n_used, off, xa, xb, n_rows):
    tile_rows = ROUTE_TILE * SUBLANES
    tiles_a, tiles_b = xa.shape[0] // tile_rows, xb.shape[0] // tile_rows
    assert tiles_a + tiles_b >= 2
    grid_spec = pltpu.PrefetchScalarGridSpec(
        num_scalar_prefetch=6, grid=(tiles_a + tiles_b,),
        in_specs=[pl.BlockSpec((None, 1, TOP_K * ROUTE_TILE), lambda i, *_: (i, 0, 0), memory_space=pltpu.SMEM),
                  pl.BlockSpec((tile_rows, 128), lambda i, *_: (jnp.minimum(i, tiles_a - 1), 0)),
                  pl.BlockSpec((tile_rows, 128), lambda i, *_: (jnp.maximum(i - tiles_a, 0), 0))],
        out_specs=pl.BlockSpec(memory_space=pl.ANY),
        scratch_shapes=[pltpu.VMEM((2 * _SLOT_ROWS, 128), F32),
                        pltpu.VMEM((MOE_TILE * SUBLANES, 128), F32),
                        pltpu.SemaphoreType.DMA((2,)), pltpu.SemaphoreType.DMA(())])
    return pl.pallas_call(
        functools.partial(_dispatch_kernel, tiles_a=tiles_a, nblk=n_rows // MOE_TILE), grid_spec=grid_spec,
        out_shape=jax.ShapeDtypeStruct((n_rows * SUBLANES, 128), F32),
        compiler_params=_cparams("arbitrary"), name="dispatch",
    )(*runs, pend, padded, n_used, off, xa, xb)


def _moe_kernel(be_ref, nu_ref, nv_ref, next_ref, slot_ref, x_ref, wgu_hbm, bg_ref, bl_ref, wd_hbm, bd_ref,
                perm_ref, o_ref, wgu_buf, wd_buf, wg_s, wl_s, wd_s, sem_gu, sem_d):
    i = pl.program_id(0)
    used = i < nu_ref[0]
    e = be_ref[i]
    new_expert = (i == 0) | (e != be_ref[jnp.maximum(i - 1, 0)])

    def fetch(expert, s):
        return (pltpu.make_async_copy(wgu_hbm.at[expert], wgu_buf.at[s], sem_gu.at[s]),
                pltpu.make_async_copy(wd_hbm.at[expert], wd_buf.at[s], sem_d.at[s]))

    @pl.when(used & new_expert)
    def _():
        s = slot_ref[e]

        @pl.when(i == 0)
        def _():
            for cp in fetch(e, s):
                cp.start()

        @pl.when(next_ref[e] >= 0)
        def _():
            for cp in fetch(next_ref[e], 1 - s):
                cp.start()

        for cp in fetch(e, s):
            cp.wait()
        for c in range(2 * D_FF // 256):
            r = _dot(wgu_buf[s, :, 256 * c:256 * (c + 1)].astype(BF16), perm_ref[...])
            wg_s[:, 128 * c:128 * (c + 1)] = r[:, :128].astype(BF16)
            wl_s[:, 128 * c:128 * (c + 1)] = r[:, 128:].astype(BF16)
        wd_s[...] = wd_buf[s].astype(BF16)

    def expert_rows(rows):
        x = _from_row_tiles(x_ref, rows).astype(BF16)
        glu = jnp.minimum(_dot(x, wg_s[...]) + bg_ref[...], SWIGLU_LIMIT)
        lin = jnp.clip(_dot(x, wl_s[...]) + bl_ref[...], -SWIGLU_LIMIT, SWIGLU_LIMIT)
        act = glu * jax.nn.sigmoid(SWIGLU_ALPHA * glu) * (lin + 1.0)
        _to_row_tiles(o_ref, _dot(act.astype(BF16), wd_s[...]) + bd_ref[...])

    def zero_rows(first, rows):
        o_ref[pl.ds(first * SUBLANES, rows * SUBLANES), :] = jnp.zeros((rows * SUBLANES, 128), F32)

    half = MOE_TILE // 2

    @pl.when(used & (nv_ref[i] > half))
    def _():
        expert_rows(MOE_TILE)

    @pl.when(used & (nv_ref[i] <= half))
    def _():
        expert_rows(half)
        zero_rows(half, half)

    @pl.when(jnp.logical_not(used))
    def _():
        zero_rows(0, MOE_TILE)


def _moe(block_e, n_used, n_valid, next_e, e_slot, x_rows, wgu, bg, bl, wd, bd, perm):
    nblk = x_rows.shape[0] // (MOE_TILE * SUBLANES)
    row = lambda i, be, nu, *_: (jnp.minimum(i, nu[0] - 1), 0)
    wsel = lambda i, be, *_: (be[i], 0, 0)
    grid_spec = pltpu.PrefetchScalarGridSpec(
        num_scalar_prefetch=5,
        grid=(nblk,),
        in_specs=[pl.BlockSpec((MOE_TILE * SUBLANES, 128), row),
                  pl.BlockSpec(memory_space=pl.ANY),
                  pl.BlockSpec((None, 1, D_FF), wsel), pl.BlockSpec((None, 1, D_FF), wsel),
                  pl.BlockSpec(memory_space=pl.ANY), pl.BlockSpec((None, 1, D_MODEL), wsel),
                  pl.BlockSpec((256, 256), lambda i, *_: (0, 0))],
        out_specs=pl.BlockSpec((MOE_TILE * SUBLANES, 128), lambda i, *_: (i, 0)),
        scratch_shapes=[pltpu.VMEM((2, D_MODEL, 2 * D_FF), F32), pltpu.VMEM((2, D_FF, D_MODEL), F32),
                        pltpu.VMEM((D_MODEL, D_FF), BF16), pltpu.VMEM((D_MODEL, D_FF), BF16),
                        pltpu.VMEM((D_FF, D_MODEL), BF16),
                        pltpu.SemaphoreType.DMA((2,)), pltpu.SemaphoreType.DMA((2,))],
    )
    return pl.pallas_call(
        _moe_kernel,
        grid_spec=grid_spec,
        out_shape=jax.ShapeDtypeStruct(x_rows.shape, F32),
        compiler_params=_cparams("arbitrary"),
        name="moe",
    )(block_e, n_used, n_valid, next_e, e_slot, x_rows, wgu, bg, bl, wd, bd, perm)


def _combine_kernel(src_tbl, cnt_tbl, loc_tbl, off_ref, gate_ref, rows_hbm, x1_ref, o_ref, buf, ybuf, sem, *,
                    tile_base):
    i = pl.program_id(0)
    n = pl.num_programs(0)

    def slot_rows(s):
        return buf.at[pl.ds(pl.multiple_of(s * _SLOT_ROWS, _SLOT_ROWS), _SLOT_ROWS), :]

    def fetch(tile, s):
        _run_copies(tile_base + tile, src_tbl, cnt_tbl, loc_tbl,
                    lambda g, l, size: pltpu.make_async_copy(_tile_span(rows_hbm, g, size),
                                                             _tile_span(slot_rows(s), l, size), sem.at[s]))

    @pl.when(i == 0)
    def _():
        fetch(0, 0)

    @pl.when(i + 1 < n)
    def _():
        fetch(i + 1, (i + 1) % 2)

    slot = i % 2
    pltpu.make_async_copy(slot_rows(slot), slot_rows(slot), sem.at[slot]).wait()

    def body(t, c):
        acc = None
        for k in range(TOP_K):
            off = pl.multiple_of(off_ref[0, k * ROUTE_TILE + t], SUBLANES)
            v = gate_ref[0, k * ROUTE_TILE + t] * buf[pl.ds(off, SUBLANES), :]
            acc = v if acc is None else acc + v
        ybuf[pl.ds(pl.multiple_of(t * SUBLANES, SUBLANES), SUBLANES), :] = acc
        return c

    lax.fori_loop(0, ROUTE_TILE, body, 0, unroll=8)
    o_ref[...] = x1_ref[...] + _from_row_tiles(ybuf, ROUTE_TILE)


def _combine(runs, off, gates, rows, x1, tile_base):
    assert tile_base % 2 == 0
    t = x1.shape[0]
    tile_rows = ROUTE_TILE * SUBLANES
    smem_tile = pl.BlockSpec((None, 1, TOP_K * ROUTE_TILE), lambda i, *_: (tile_base + i, 0, 0),
                             memory_space=pltpu.SMEM)
    grid_spec = pltpu.PrefetchScalarGridSpec(
        num_scalar_prefetch=3,
        grid=(t // ROUTE_TILE,),
        in_specs=[smem_tile, smem_tile, pl.BlockSpec(memory_space=pl.ANY),
                  pl.BlockSpec((ROUTE_TILE, D_MODEL), lambda i, *_: (i, 0))],
        out_specs=pl.BlockSpec((ROUTE_TILE, D_MODEL), lambda i, *_: (i, 0)),
        scratch_shapes=[pltpu.VMEM((2 * _SLOT_ROWS, 128), F32), pltpu.VMEM((tile_rows, 128), F32),
                        pltpu.SemaphoreType.DMA((2,))],
    )
    return pl.pallas_call(
        functools.partial(_combine_kernel, tile_base=tile_base),
        grid_spec=grid_spec,
        out_shape=jax.ShapeDtypeStruct((t, D_MODEL), F32),
        compiler_params=_cparams("arbitrary"),
        name="combine",
    )(*runs, off, gates, rows, x1)


def _columns_to_row(cols):
    lane = lax.broadcasted_iota(jnp.int32, (cols[0].shape[0], 128), 1)
    staged = jnp.zeros((cols[0].shape[0], 128), cols[0].dtype)
    for k, col in enumerate(cols):
        staged = jnp.where(lane == k, col, staged)
    rows = staged.T
    return jnp.concatenate([rows[k:k + 1] for k in range(len(cols))], axis=1)


def _topk_kernel(lga_ref, lgb_ref, tri_ref, upper_ref, gate_ref, off_ref, before_ref, cnt_ref, loc_ref, carry, *,
                 tiles_a):
    i = pl.program_id(0)

    @pl.when(i == 0)
    def _():
        carry[...] = jnp.zeros_like(carry)

    l = jnp.where(i < tiles_a, lga_ref[...], lgb_ref[...])
    lane = lax.broadcasted_iota(jnp.int32, l.shape, 1)
    vals, sels = [], []
    for _ in range(TOP_K):
        m = jnp.max(l, axis=-1, keepdims=True)
        idx = jnp.min(jnp.where(l == m, lane, l.shape[1]), axis=-1, keepdims=True)
        sel = lane == idx
        l = jnp.where(sel, -jnp.inf, l)
        vals.append(m)
        sels.append(sel)
    exps = [jnp.exp(v - vals[0]) for v in vals]
    den = exps[0] + exps[1] + exps[2] + exps[3]
    onehot = jnp.where(sels[0] | sels[1] | sels[2] | sels[3], 1.0, 0.0)
    within = _dot(tri_ref[...], onehot.astype(BF16))
    cnt = jnp.sum(onehot, axis=0, keepdims=True)
    cnt_hi = jnp.floor(cnt * (1.0 / 16.0))
    cnt_lo = cnt - 16.0 * cnt_hi
    loc = 16.0 * _dot(cnt_hi.astype(BF16), upper_ref[...]) + _dot(cnt_lo.astype(BF16), upper_ref[...])
    slot_base = ((i % 2) * (ROUTE_TILE * TOP_K)).astype(F32)
    rows = [jnp.sum(jnp.where(s, within + loc, 0.0), axis=-1, keepdims=True) for s in sels]
    gate_ref[...] = _columns_to_row([e / den for e in exps])
    off_ref[...] = ((_columns_to_row(rows) + slot_base) * SUBLANES).astype(jnp.int32)
    before_ref[...] = carry[...]
    cnt_ref[...] = cnt
    loc_ref[...] = loc
    carry[...] = carry[...] + cnt


def _route(lga, lgb):
    tr = ROUTE_TILE
    tiles_a, tiles_b = lga.shape[0] // tr, lgb.shape[0] // tr
    n_tiles = tiles_a + tiles_b
    t = n_tiles * tr
    n_assign = t * TOP_K
    row = lambda i: (i, 0)
    fix = lambda i: (0, 0)
    sd = jax.ShapeDtypeStruct
    k_spec = pl.BlockSpec((None, 1, TOP_K * tr), lambda i: (i, 0, 0))
    t_spec = pl.BlockSpec((None, 1, 128), lambda i: (i, 0, 0))
    tri = (jnp.arange(tr)[:, None] > jnp.arange(tr)[None, :]).astype(BF16)
    upper = (jnp.arange(128)[:, None] < jnp.arange(128)[None, :]).astype(BF16)
    gates, off, before, cnt, loc = pl.pallas_call(
        functools.partial(_topk_kernel, tiles_a=tiles_a),
        grid=(n_tiles,),
        in_specs=[pl.BlockSpec((tr, 128), lambda i: (jnp.minimum(i, tiles_a - 1), 0)),
                  pl.BlockSpec((tr, 128), lambda i: (jnp.maximum(i - tiles_a, 0), 0)),
                  pl.BlockSpec((tr, tr), fix), pl.BlockSpec((128, 128), fix)],
        out_specs=[k_spec, k_spec, t_spec, t_spec, t_spec],
        out_shape=[sd((n_tiles, 1, TOP_K * tr), F32), sd((n_tiles, 1, TOP_K * tr), jnp.int32),
                   sd((n_tiles, 1, 128), F32), sd((n_tiles, 1, 128), F32), sd((n_tiles, 1, 128), F32)],
        scratch_shapes=[pltpu.VMEM((1, 128), F32)],
        compiler_params=_cparams("arbitrary"),
        name="topk",
    )(lga, lgb, tri, upper)
    table = lambda a: a[:, 0, :N_EXPERTS].astype(jnp.int32)
    before, cnt, run_loc = table(before), table(cnt), table(loc)
    counts = before[-1] + cnt[-1]
    padded = ((counts + MOE_TILE - 1) // MOE_TILE) * MOE_TILE
    pend = jnp.cumsum(padded)
    pstart = pend - padded
    run_src = pstart[None, :] + before
    nblk = (n_assign + MOE_TILE - 1) // MOE_TILE + N_EXPERTS
    n_used = (pend[-1] // MOE_TILE).astype(jnp.int32)
    block_start = jnp.arange(nblk, dtype=jnp.int32) * MOE_TILE
    block_e = jnp.sum(pend[None, :] <= jnp.minimum(block_start, pend[-1] - 1)[:, None], axis=1)
    block_e = jnp.minimum(block_e, N_EXPERTS - 1).astype(jnp.int32)
    region_end = jnp.sum(jnp.where(block_e[:, None] == jnp.arange(N_EXPERTS)[None, :], (pstart + counts)[None, :], 0),
                         axis=1)
    n_valid = jnp.clip(region_end - block_start, 0, MOE_TILE).astype(jnp.int32)
    ids = jnp.arange(N_EXPERTS, dtype=jnp.int32)
    nonempty = counts > 0
    next_e = jnp.min(jnp.where((ids[None, :] > ids[:, None]) & nonempty[None, :], ids[None, :], N_EXPERTS), axis=1)
    next_e = jnp.where(next_e == N_EXPERTS, -1, next_e).astype(jnp.int32)
    e_slot = ((jnp.cumsum(nonempty.astype(jnp.int32)) - nonempty.astype(jnp.int32)) % 2).astype(jnp.int32)
    blocks = (block_e, n_used.reshape(1), n_valid, next_e, e_slot)
    runs = (run_src.reshape(-1), cnt.reshape(-1), run_loc.reshape(-1))
    return gates, off, runs, blocks, pend, padded, nblk * MOE_TILE


def kernel(x_prompt, x_sample, cache_k, cache_v, state_ssm_re, state_ssm_im, norm_mix_g, w_in, q_norm_g, k_norm_g, attn_sinks, ssm_lambda_re, ssm_lambda_im, ssm_b_re, ssm_b_im, ssm_c_re, ssm_c_im, ssm_d, ssm_log_dt, w_glu, b_glu, attn_out_norm_g, ssm_out_norm_g, w_out, norm_ffn_g, w_router, b_router, w_gate_up, b_gate_up, w_down, b_down):
    depth = w_in.shape[0]
    assert depth == 1
    bp, sp, _ = x_prompt.shape
    bs, ss, _ = x_sample.shape
    tp, ts = bp * sp, bs * ss
    assert bp == SUBLANES and sp % S5_CHUNK == 0 and bs % SUBLANES == 0

    perm = jnp.asarray(_Q_PERM)
    w_in0 = w_in[0]
    wq = w_in0[:, :D_ATTN].reshape(D_MODEL, N_HEADS, HEAD_DIM)[:, perm].reshape(D_MODEL, D_ATTN)
    w_in_b = jnp.concatenate([wq, w_in0[:, D_ATTN:]], axis=1).astype(BF16)
    qkg = jnp.concatenate([jnp.tile(q_norm_g[0], N_HEADS), jnp.tile(k_norm_g[0], N_KV_HEADS)])[None]
    pmat = jnp.kron(jnp.eye(256 // HEAD_DIM, dtype=F32),
                    jnp.full((HEAD_DIM, HEAD_DIM), 1.0 / HEAD_DIM, F32)).astype(BF16)
    g_mix = norm_mix_g[0][None]
    sinks = attn_sinks[0]
    g_attn = attn_out_norm_g[0].reshape(N_HEADS, HEAD_DIM)[perm].reshape(1, D_ATTN)
    w_out0 = w_out[0]
    w_out_a = w_out0[:D_ATTN].reshape(N_HEADS, HEAD_DIM, D_MODEL)[perm].reshape(D_ATTN, D_MODEL).astype(BF16)
    w_out_s = w_out0[D_ATTN:].astype(BF16)
    g_ssm = ssm_out_norm_g[0][None]
    g_ffn = norm_ffn_g[0][None]
    w_r = jnp.pad(w_router[0], ((0, 0), (0, 128 - N_EXPERTS))).astype(BF16)
    b_r = jnp.pad(b_router[0], (0, 128 - N_EXPERTS), constant_values=NEG_INF)[None]

    a_re, a_im, bb_re, bb_im = _s5_prep(
        ssm_lambda_re[0], ssm_lambda_im[0], ssm_log_dt[0][:, None],
        jnp.swapaxes(ssm_b_re[0], 1, 2), jnp.swapaxes(ssm_b_im[0], 1, 2))
    chan_g = jnp.arange(D_SSM)[:, None] // SSM_GROUP
    state_g = jnp.arange(D_STATE)[None, :] // SSM_STATE
    bd_b = lambda bb: jnp.where(chan_g == state_g, jnp.tile(bb.reshape(D_SSM, SSM_STATE), (1, N_SSM_GROUPS)), 0.0)
    wb = jnp.concatenate([bd_b(bb_re), bd_b(bb_im)], axis=1).astype(BF16)
    bd_c = lambda c: jnp.where(state_g.T == chan_g.T,
                               jnp.tile(jnp.swapaxes(c, 1, 2).reshape(D_STATE, SSM_GROUP), (1, N_SSM_GROUPS)), 0.0)
    wc = jnp.concatenate([bd_c(ssm_c_re[0]), -bd_c(ssm_c_im[0])], axis=0).astype(BF16)
    a_re, a_im = a_re.reshape(1, D_STATE), a_im.reshape(1, D_STATE)
    d_skip = ssm_d[0].reshape(1, D_SSM)
    w_glu_b = w_glu[0].astype(BF16)
    b_glu0 = b_glu[0][None]

    b_g = b_gate_up[0][:, None, 0::2]
    b_l = b_gate_up[0][:, None, 1::2]
    b_d = b_down[0][:, None, :]
    idx = jnp.arange(256)
    deint = (idx[None, :] == jnp.where(idx % 2 == 0, idx // 2, 128 + idx // 2)[:, None]).astype(BF16)

    xp2 = x_prompt.reshape(tp, D_MODEL)
    xs2 = x_sample.reshape(ts, D_MODEL)
    qp, kp, vp, up = _in_proj(xp2, g_mix, w_in_b, qkg, pmat)
    qs, ks, vs, us = _in_proj(xs2, g_mix, w_in_b, qkg, pmat)

    kp3, vp3 = kp.reshape(bp, sp, D_KV), vp.reshape(bp, sp, D_KV)
    attn_p = _band_attention(sinks, qp.reshape(bp, sp, D_ATTN), kp3, vp3, g_attn).reshape(tp, D_ATTN)
    ck = cache_k[0].reshape(bs, -1, D_KV)
    cv = cache_v[0].reshape(bs, -1, D_KV)
    attn_s, new_ck, new_cv = _cache_attention(sinks, qs, ks, vs, ck, cv, g_attn, ss)

    zeros_p = jnp.zeros((bp, D_STATE), F32)
    s5_args = (a_re, a_im, wb, wc, d_skip, w_glu_b, b_glu0, g_ssm)
    ssm_p, hr_p, hi_p = _s5(up.reshape(bp, sp, D_SSM), zeros_p, zeros_p, *s5_args,
                            tt=S5_CHUNK, time_chunked=True)
    ssm_s, hr_s, hi_s = _s5(us.reshape(bs // SUBLANES, SUBLANES * ss, D_SSM),
                            state_ssm_re[0].reshape(bs, D_STATE), state_ssm_im[0].reshape(bs, D_STATE),
                            *s5_args, tt=ss, time_chunked=False)

    x1p, xtp, lgp = _out_proj(attn_p, ssm_p.reshape(tp, D_SSM), xp2, w_out_a, w_out_s, g_ffn, w_r, b_r)
    x1s, xts, lgs = _out_proj(attn_s, ssm_s.reshape(ts, D_SSM), xs2, w_out_a, w_out_s, g_ffn, w_r, b_r)

    gates, off, runs, blocks, pend, padded, n_rows = _route(lgp, lgs)
    x_rows = _dispatch(runs, pend, padded, blocks[1], off, xtp, xts, n_rows)
    out_rows = _moe(*blocks, x_rows, w_gate_up[0], b_g, b_l, w_down[0], b_d, deint)
    yp = _combine(runs, off, gates, out_rows, x1p, 0).reshape(bp, sp, D_MODEL)
    ys = _combine(runs, off, gates, out_rows, x1s, tp // ROUTE_TILE).reshape(bs, ss, D_MODEL)

    kv5 = lambda a, b_: a.reshape(b_, -1, N_KV_HEADS, HEAD_DIM)
    new_kp = kv5(kp3[:, -WINDOW:], bp)[None]
    new_vp = kv5(vp3[:, -WINDOW:], bp)[None]
    new_ks = kv5(new_ck, bs)[None]
    new_vs = kv5(new_cv, bs)[None]
    st = lambda h, b_: h.reshape(1, b_, N_SSM_GROUPS, SSM_STATE)
    return (yp, ys, new_kp, new_vp, st(hr_p, bp), st(hi_p, bp),
            new_ks, new_vs, st(hr_s, bs), st(hi_s, bs))
```

```python
import functools
import math

import jax
import jax.numpy as jnp
from jax import lax
from jax.experimental import pallas as pl
from jax.experimental.pallas import tpu as pltpu

F32 = jnp.float32
BF16 = jnp.bfloat16

D_MODEL = 1024
D_ATTN = 512
D_SSM = 512
HEAD_DIM = 64
N_HEADS = 8
N_KV_HEADS = 2
D_KV = N_KV_HEADS * HEAD_DIM
WINDOW = 128
SSM_GROUP = 16
N_SSM_GROUPS = 32
SSM_STATE = 64
D_STATE = N_SSM_GROUPS * SSM_STATE
N_EXPERTS = 32
TOP_K = 4
D_FF = 1024
SWIGLU_LIMIT = 7.0
SWIGLU_ALPHA = 1.702
RMS_EPS = 1e-6
NEG_INF = -1e30
D_IN_PROJ = D_ATTN + 2 * D_KV + D_SSM
D_QK = D_ATTN + D_KV

SUBLANES = 8
VMEM_LIMIT = 56 * 1024 * 1024

ROW_TILE = 512
MOE_TILE = 512
ATTN_BLOCKS = 4
ROUTE_TILE = 512
_SLOT_ROWS = ROUTE_TILE * TOP_K * SUBLANES
_RUN_SIZES = tuple(1 << b for b in range(ROUTE_TILE.bit_length() - 1, -1, -1))
S5_CHUNK = 128
S5_COLS = 512

_Q_PERM = (0, 4, 1, 5, 2, 6, 3, 7)


def _cparams(*sem):
    return pltpu.CompilerParams(dimension_semantics=sem, vmem_limit_bytes=VMEM_LIMIT)


def _dot(a, b):
    return jnp.dot(a, b, preferred_element_type=F32)


def _rms(x):
    return x * lax.rsqrt(jnp.mean(x * x, axis=-1, keepdims=True) + RMS_EPS)


def _in_proj_kernel(x_ref, g_ref, w_ref, qkg_ref, p_ref, q_ref, k_ref, v_ref, u_ref):
    xn = _rms(x_ref[...]) * g_ref[...]
    h = _dot(xn.astype(BF16), w_ref[...])
    qk = h[:, :D_QK]
    sq = (qk * qk).astype(BF16)
    p = p_ref[...]
    ms = jnp.concatenate(
        [_dot(sq[:, 0:256], p), _dot(sq[:, 256:512], p), _dot(sq[:, 512:640], p[:128, :128])],
        axis=-1)
    qkn = qk * lax.rsqrt(ms + RMS_EPS) * qkg_ref[...]
    q_ref[...] = qkn[:, :D_ATTN].astype(BF16)
    k_ref[...] = qkn[:, D_ATTN:]
    v_ref[...] = h[:, D_QK:D_QK + D_KV]
    u_ref[...] = h[:, D_QK + D_KV:]


def _in_proj(x2d, g, w, qkg, pmat):
    t = x2d.shape[0]
    tm = min(ROW_TILE, t)
    row = lambda i: (i, 0)
    fix = lambda i: (0, 0)
    return pl.pallas_call(
        _in_proj_kernel,
        grid=(t // tm,),
        in_specs=[pl.BlockSpec((tm, D_MODEL), row), pl.BlockSpec((1, D_MODEL), fix),
                  pl.BlockSpec((D_MODEL, D_IN_PROJ), fix), pl.BlockSpec((1, D_QK), fix),
                  pl.BlockSpec((256, 256), fix)],
        out_specs=[pl.BlockSpec((tm, D_ATTN), row), pl.BlockSpec((tm, D_KV), row),
                   pl.BlockSpec((tm, D_KV), row), pl.BlockSpec((tm, D_SSM), row)],
        out_shape=[jax.ShapeDtypeStruct((t, D_ATTN), BF16), jax.ShapeDtypeStruct((t, D_KV), F32),
                   jax.ShapeDtypeStruct((t, D_KV), F32), jax.ShapeDtypeStruct((t, D_SSM), F32)],
        compiler_params=_cparams("parallel"),
        name="in_proj",
    )(x2d, g, w, qkg, pmat)


def _softmax_pv(s_blocks, v_blocks, sink):
    m = sink
    for s in s_blocks:
        m = jnp.maximum(m, jnp.max(s, axis=-1, keepdims=True))
    den = jnp.exp(sink - m)
    acc = None
    for s, v in zip(s_blocks, v_blocks):
        p = jnp.exp(s - m)
        den = den + jnp.sum(p, axis=-1, keepdims=True)
        pv = _dot(p.astype(BF16), v)
        acc = pv if acc is None else acc + pv
    return acc / den


def _band_attn_kernel(sink_ref, q_ref, kp_ref, kc_ref, vp_ref, vc_ref, g_ref, o_ref):
    i = pl.program_id(1)
    row = lax.broadcasted_iota(jnp.int32, (WINDOW, 2 * WINDOW), 0)
    col = lax.broadcasted_iota(jnp.int32, (WINDOW, 2 * WINDOW), 1)
    band = (col > row) & (col <= row + WINDOW)
    lane = lax.broadcasted_iota(jnp.int32, (WINDOW, 128), 1)
    low = lane < HEAD_DIM
    zero = jnp.zeros((), BF16)
    for j in range(ATTN_BLOCKS):
        cur = slice(WINDOW * j, WINDOW * (j + 1))
        q = q_ref[cur, :]
        k_prev = kp_ref[...] if j == 0 else kc_ref[WINDOW * (j - 1):WINDOW * j, :]
        v_prev = vp_ref[...] if j == 0 else vc_ref[WINDOW * (j - 1):WINDOW * j, :]
        kb = jnp.concatenate([k_prev, kc_ref[cur, :]], axis=0).astype(BF16)
        vb = jnp.concatenate([v_prev, vc_ref[cur, :]], axis=0).astype(BF16)
        mask = band & ((col >= WINDOW) | (i > 0)) if j == 0 else band
        outs = []
        for pair in range(N_HEADS // 2):
            qp = q[:, 128 * pair:128 * (pair + 1)]
            halves = []
            for par in range(2):
                qm = jnp.where(low if par == 0 else ~low, qp, zero)
                s = lax.dot_general(qm, kb, (((1,), (1,)), ((), ())), preferred_element_type=F32)
                s = jnp.where(mask, s * (HEAD_DIM ** -0.5), NEG_INF)
                halves.append(_softmax_pv([s], [vb], sink_ref[_Q_PERM[2 * pair + par]]))
            outs.append(jnp.where(low, halves[0], halves[1]))
        o = jnp.concatenate(outs, axis=-1)
        o_ref[cur, :] = (_rms(o) * g_ref[...]).astype(BF16)


def _band_attention(sinks, q, k, v, g):
    b, s, _ = q.shape
    qb = WINDOW * ATTN_BLOCKS
    cur = lambda bi, i: (bi, i, 0)
    prev = lambda bi, i: (bi, jnp.maximum(i * ATTN_BLOCKS - 1, 0), 0)
    return pl.pallas_call(
        _band_attn_kernel,
        grid=(b, s // qb),
        in_specs=[pl.BlockSpec(memory_space=pltpu.SMEM),
                  pl.BlockSpec((None, qb, D_ATTN), cur),
                  pl.BlockSpec((None, WINDOW, D_KV), prev), pl.BlockSpec((None, qb, D_KV), cur),
                  pl.BlockSpec((None, WINDOW, D_KV), prev), pl.BlockSpec((None, qb, D_KV), cur),
                  pl.BlockSpec((1, D_ATTN), lambda bi, i: (0, 0))],
        out_specs=pl.BlockSpec((None, qb, D_ATTN), cur),
        out_shape=jax.ShapeDtypeStruct((b, s, D_ATTN), BF16),
        compiler_params=_cparams("parallel", "parallel"),
        name="band_attn",
    )(sinks, q, k, k, v, v, g)


_PAIR_ROWS = 8
_CACHE_BB = 8


def _cache_attn_kernel(sink_ref, q_ref, kn_ref, vn_ref, ck_ref, cv_ref, g_ref, o_ref, nk_ref, nv_ref, *, n_new):
    n_buf = ck_ref.shape[1]
    rows_blk = _CACHE_BB * n_new
    for b in range(_CACHE_BB):
        for new_ref, cache_ref, fresh_ref in ((nk_ref, ck_ref, kn_ref), (nv_ref, cv_ref, vn_ref)):
            new_ref[b, 0:n_buf - n_new, :] = cache_ref[b, n_new:n_buf, :]
            new_ref[b, n_buf - n_new:n_buf, :] = fresh_ref[n_new * b:n_new * (b + 1), :]
    knew = kn_ref[...].astype(BF16)
    vnew = vn_ref[...].astype(BF16)
    lane = lax.broadcasted_iota(jnp.int32, (_PAIR_ROWS, 128), 1)
    low = lane < HEAD_DIM
    zero = jnp.zeros((), BF16)
    n_stack = N_HEADS * _PAIR_ROWS
    r = lax.broadcasted_iota(jnp.int32, (n_stack, 1), 0) % _PAIR_ROWS
    r_seq, r_tok = r // n_new, r % n_new
    colc = lax.broadcasted_iota(jnp.int32, (n_stack, 2 * n_buf), 1)
    c_seq, c_pos = colc // n_buf, colc % n_buf
    mask_c = (c_seq == r_seq) & (c_pos + WINDOW > r_tok + n_buf)
    coln = lax.broadcasted_iota(jnp.int32, (n_stack, rows_blk), 1)
    sink_col = jnp.concatenate(
        [jnp.full((_PAIR_ROWS, 1), sink_ref[_Q_PERM[h]], F32) for h in range(N_HEADS)], axis=0)
    for sp in range(_CACHE_BB // 2):
        q = q_ref[_PAIR_ROWS * sp:_PAIR_ROWS * (sp + 1), :]
        pieces = []
        for pair in range(N_HEADS // 2):
            qp = q[:, 128 * pair:128 * (pair + 1)]
            pieces.append(jnp.where(low, qp, zero))
            pieces.append(jnp.where(low, zero, qp))
        qs = jnp.concatenate(pieces, axis=0)
        kc = jnp.concatenate([ck_ref[2 * sp], ck_ref[2 * sp + 1]], axis=0).astype(BF16)
        vc = jnp.concatenate([cv_ref[2 * sp], cv_ref[2 * sp + 1]], axis=0).astype(BF16)
        nt = (((1,), (1,)), ((), ()))
        s_c = lax.dot_general(qs, kc, nt, preferred_element_type=F32) * (HEAD_DIM ** -0.5)
        s_n = lax.dot_general(qs, knew, nt, preferred_element_type=F32) * (HEAD_DIM ** -0.5)
        s_c = jnp.where(mask_c, s_c, NEG_INF)
        n_seq, n_tok = coln // n_new - 2 * sp, coln % n_new
        mask_n = (n_seq == r_seq) & (n_tok <= r_tok)
        s_n = jnp.where(mask_n, s_n, NEG_INF)
        o = _softmax_pv([s_c, s_n], [vc, vnew], sink_col)
        outs = [jnp.where(low, o[16 * pair:16 * pair + 8], o[16 * pair + 8:16 * pair + 16])
                for pair in range(N_HEADS // 2)]
        oo = jnp.concatenate(outs, axis=-1)
        o_ref[_PAIR_ROWS * sp:_PAIR_ROWS * (sp + 1), :] = (_rms(oo) * g_ref[...]).astype(BF16)


def _cache_attention(sinks, q, k, v, cache_k, cache_v, g, n_new):
    t = q.shape[0]
    nb, n_buf, _ = cache_k.shape
    assert n_new * 2 == _PAIR_ROWS and nb % _CACHE_BB == 0 and n_buf == WINDOW
    rows = _CACHE_BB * n_new
    row = lambda i: (i, 0)
    cache_spec = pl.BlockSpec((_CACHE_BB, n_buf, D_KV), lambda i: (i, 0, 0))
    return pl.pallas_call(
        functools.partial(_cache_attn_kernel, n_new=n_new),
        grid=(nb // _CACHE_BB,),
        in_specs=[pl.BlockSpec(memory_space=pltpu.SMEM),
                  pl.BlockSpec((rows, D_ATTN), row), pl.BlockSpec((rows, D_KV), row),
                  pl.BlockSpec((rows, D_KV), row), cache_spec, cache_spec,
                  pl.BlockSpec((1, D_ATTN), lambda i: (0, 0))],
        out_specs=[pl.BlockSpec((rows, D_ATTN), row), cache_spec, cache_spec],
        out_shape=[jax.ShapeDtypeStruct((t, D_ATTN), BF16), jax.ShapeDtypeStruct(cache_k.shape, F32),
                   jax.ShapeDtypeStruct(cache_v.shape, F32)],
        compiler_params=_cparams("parallel"),
        name="cache_attn",
    )(sinks, q, k, v, cache_k, cache_v, g)


def _s5_prep_kernel(lre_ref, lim_ref, ldt_ref, bre_ref, bim_ref, are_ref, aim_ref, bbre_ref, bbim_ref):
    dt = jnp.exp(ldt_ref[...])
    l_re = jnp.minimum(lre_ref[...], -1e-4)
    l_im = lim_ref[...]
    mag = jnp.exp(l_re * dt)
    a_re = mag * jnp.cos(l_im * dt)
    a_im = mag * jnp.sin(l_im * dt)
    den = l_re * l_re + l_im * l_im
    n_re = a_re - 1.0
    z_re = (n_re * l_re + a_im * l_im) / den
    z_im = (a_im * l_re - n_re * l_im) / den
    are_ref[...] = a_re
    aim_ref[...] = a_im
    br, bi = bre_ref[...], bim_ref[...]
    zr, zi = z_re[:, None, :], z_im[:, None, :]
    bbre_ref[...] = zr * br - zi * bi
    bbim_ref[...] = zr * bi + zi * br


def _s5_prep(lam_re, lam_im, log_dt, b_re_t, b_im_t):
    g, p = lam_re.shape
    sd = jax.ShapeDtypeStruct
    return pl.pallas_call(
        _s5_prep_kernel,
        out_shape=[sd((g, p), F32), sd((g, p), F32), sd(b_re_t.shape, F32), sd(b_re_t.shape, F32)],
        name="s5_prep",
    )(lam_re, lam_im, log_dt, b_re_t, b_im_t)


def _s5_kernel(u_ref, h0r_ref, h0i_ref, ar_ref, ai_ref, wb_ref, wc_ref, d_ref, wglu_ref, bglu_ref,
               g_ref, o_ref, hr_ref, hi_ref, bu_ref, hs_ref, usc_ref, ysc_ref, *, tt):
    j = pl.program_id(1)
    rows = SUBLANES * tt
    n_tiles = D_STATE // 128
    time_major = tt % SUBLANES == 0

    @pl.when(j == 0)
    def _():
        hs_ref[:, :D_STATE] = h0r_ref[...]
        hs_ref[:, D_STATE:] = h0i_ref[...]

    u = u_ref[...].reshape(rows, D_SSM)
    if time_major:
        for c in range(D_SSM // 128):
            for b in range(SUBLANES):
                usc_ref[c, pl.ds(b, tt, stride=SUBLANES), :] = u[b * tt:(b + 1) * tt, 128 * c:128 * (c + 1)]
        ub = jnp.concatenate([usc_ref[c] for c in range(D_SSM // 128)], axis=-1).astype(BF16)
    else:
        ub = u.astype(BF16)

    def step_rows(t):
        if time_major:
            return pl.ds(pl.multiple_of(t * SUBLANES, SUBLANES), SUBLANES)
        return pl.ds(t, SUBLANES, stride=tt)

    for n in range(2 * D_STATE // 256):
        band = (n % (D_STATE // 256)) // 2
        res = _dot(ub[:, 128 * band:128 * (band + 1)],
                   wb_ref[128 * band:128 * (band + 1), 256 * n:256 * (n + 1)])
        bu_ref[2 * n] = res[:, :128]
        bu_ref[2 * n + 1] = res[:, 128:]

    tiles_per_pass = S5_COLS // 128
    for c0 in range(0, n_tiles, tiles_per_pass):
        tiles = range(c0, c0 + tiles_per_pass)
        a_r = [jnp.broadcast_to(ar_ref[:, 128 * c:128 * (c + 1)], (SUBLANES, 128)) for c in tiles]
        a_i = [jnp.broadcast_to(ai_ref[:, 128 * c:128 * (c + 1)], (SUBLANES, 128)) for c in tiles]

        def step(t, carry, tiles=tiles, a_r=a_r, a_i=a_i):
            at_t = step_rows(t)
            out = []
            for k, c in enumerate(tiles):
                h_r, h_i = carry[2 * k], carry[2 * k + 1]
                n_r = a_r[k] * h_r - a_i[k] * h_i + bu_ref[c, at_t, :]
                n_i = a_r[k] * h_i + a_i[k] * h_r + bu_ref[n_tiles + c, at_t, :]
                bu_ref[c, at_t, :] = n_r
                bu_ref[n_tiles + c, at_t, :] = n_i
                out += [n_r, n_i]
            return tuple(out)

        init = []
        for c in tiles:
            init += [hs_ref[:, 128 * c:128 * (c + 1)], hs_ref[:, D_STATE + 128 * c:D_STATE + 128 * (c + 1)]]
        fin = lax.fori_loop(0, tt, step, tuple(init), unroll=min(tt, 8))
        for k, c in enumerate(tiles):
            hs_ref[:, 128 * c:128 * (c + 1)] = fin[2 * k]
            hs_ref[:, D_STATE + 128 * c:D_STATE + 128 * (c + 1)] = fin[2 * k + 1]

    def h_cols(first_tile):
        return jnp.concatenate([bu_ref[first_tile + k] for k in range(4)], axis=-1).astype(BF16)

    ys = []
    for m in range(D_SSM // 128):
        y = _dot(h_cols(4 * m), wc_ref[512 * m:512 * (m + 1), 128 * m:128 * (m + 1)])
        y = y + _dot(h_cols(n_tiles + 4 * m),
                     wc_ref[D_STATE + 512 * m:D_STATE + 512 * (m + 1), 128 * m:128 * (m + 1)])
        if time_major:
            ysc_ref[m] = y
            y = jnp.concatenate([ysc_ref[m, pl.ds(b, tt, stride=SUBLANES), :] for b in range(SUBLANES)], axis=0)
        ys.append(y)
    y = jnp.concatenate(ys, axis=-1) + d_ref[...] * u
    z = _dot(jax.nn.gelu(y).astype(BF16), wglu_ref[...]) + bglu_ref[...]
    s = z[:, :D_SSM] * jax.nn.sigmoid(z[:, D_SSM:])
    o_ref[...] = (_rms(s) * g_ref[...]).astype(BF16).reshape(o_ref.shape)

    @pl.when(j == pl.num_programs(1) - 1)
    def _():
        hr_ref[...] = hs_ref[:, :D_STATE]
        hi_ref[...] = hs_ref[:, D_STATE:]


def _s5(u, h0r, h0i, a_re, a_im, wb, wc, d, wglu, bglu, g, *, tt, time_chunked):
    nbg = h0r.shape[0] // SUBLANES
    if time_chunked:
        nchunks = u.shape[1] // tt
        u_spec = pl.BlockSpec((SUBLANES, tt, D_SSM), lambda gi, j: (gi, j, 0))
    else:
        nchunks = 1
        u_spec = pl.BlockSpec((None, SUBLANES * tt, D_SSM), lambda gi, j: (gi, 0, 0))
    fix = lambda gi, j: (0, 0)
    st_spec = pl.BlockSpec((SUBLANES, D_STATE), lambda gi, j: (gi, 0))
    sd = jax.ShapeDtypeStruct
    return pl.pallas_call(
        functools.partial(_s5_kernel, tt=tt),
        grid=(nbg, nchunks),
        in_specs=[u_spec, st_spec, st_spec,
                  pl.BlockSpec((1, D_STATE), fix), pl.BlockSpec((1, D_STATE), fix),
                  pl.BlockSpec((D_SSM, 2 * D_STATE), fix), pl.BlockSpec((2 * D_STATE, D_SSM), fix),
                  pl.BlockSpec((1, D_SSM), fix), pl.BlockSpec((D_SSM, 2 * D_SSM), fix),
                  pl.BlockSpec((1, 2 * D_SSM), fix), pl.BlockSpec((1, D_SSM), fix)],
        out_specs=[u_spec, st_spec, st_spec],
        out_shape=[sd(u.shape, BF16), sd(h0r.shape, F32), sd(h0r.shape, F32)],
        scratch_shapes=[pltpu.VMEM((2 * D_STATE // 128, SUBLANES * tt, 128), F32),
                        pltpu.VMEM((SUBLANES, 2 * D_STATE), F32),
                        pltpu.VMEM((D_SSM // 128, SUBLANES * tt, 128), F32),
                        pltpu.VMEM((D_SSM // 128, SUBLANES * tt, 128), F32)],
        compiler_params=_cparams("parallel", "arbitrary"),
        name="s5",
    )(u, h0r, h0i, a_re, a_im, wb, wc, d, wglu, bglu, g)


def _to_row_tiles(ref, val, first=0):
    rows = val.shape[0]
    for c in range(D_MODEL // 128):
        ref[pl.ds(first * SUBLANES + c, rows, stride=SUBLANES), :] = val[:, 128 * c:128 * (c + 1)]


def _from_row_tiles(ref, rows, lead=(), first=0):
    return jnp.concatenate(
        [ref[(*lead, pl.ds(first * SUBLANES + c, rows, stride=SUBLANES), slice(None))]
         for c in range(D_MODEL // 128)], axis=-1)


def _out_proj_kernel(a_ref, s_ref, x_ref, wa_ref, ws_ref, g_ref, wr_ref, br_ref, x1_ref, lg_ref):
    x1 = x_ref[...] + _dot(a_ref[...], wa_ref[...]) + _dot(s_ref[...], ws_ref[...])
    x1_ref[...] = x1
    xn = _rms(x1) * g_ref[...]
    lg_ref[...] = _dot(xn.astype(BF16), wr_ref[...]) + br_ref[...]


def _out_proj(attn_n, ssm_n, x2d, wa, ws, g, wr, br):
    t = x2d.shape[0]
    tm = min(ROW_TILE, t)
    row = lambda i: (i, 0)
    fix = lambda i: (0, 0)
    sd = jax.ShapeDtypeStruct
    return pl.pallas_call(
        _out_proj_kernel,
        grid=(t // tm,),
        in_specs=[pl.BlockSpec((tm, D_ATTN), row), pl.BlockSpec((tm, D_SSM), row),
                  pl.BlockSpec((tm, D_MODEL), row),
                  pl.BlockSpec((D_ATTN, D_MODEL), fix), pl.BlockSpec((D_SSM, D_MODEL), fix),
                  pl.BlockSpec((1, D_MODEL), fix), pl.BlockSpec((D_MODEL, 128), fix),
                  pl.BlockSpec((1, 128), fix)],
        out_specs=[pl.BlockSpec((tm, D_MODEL), row), pl.BlockSpec((tm, 128), row)],
        out_shape=[sd((t, D_MODEL), F32), sd((t, 128), F32)],
        compiler_params=_cparams("parallel"),
        name="out_proj",
    )(attn_n, ssm_n, x2d, wa, ws, g, wr, br)


def _tile_span(ref, first_row, n_rows, lead=()):
    start = pl.multiple_of(first_row * SUBLANES, SUBLANES)
    return ref.at[(*lead, pl.ds(start, n_rows * SUBLANES), slice(None))]


def _run_copies(tile, src_tbl, cnt_tbl, loc_tbl, make_copy):
    def body(e, c):
        j = tile * N_EXPERTS + e
        cnt, src, loc = cnt_tbl[j], src_tbl[j], loc_tbl[j]
        off = 0
        for size in _RUN_SIZES:
            @pl.when((cnt & size) != 0)
            def _(off=off, size=size):
                make_copy(src + off, loc + off, size).start()
            off = off + (cnt & size)
        return c

    lax.fori_loop(0, N_EXPERTS, body, 0)


def _dispatch_kernel(src_tbl, cnt_tbl, loc_tbl, pend_ref, padded_ref, nu_ref, off_ref, xa_ref, xb_ref, g_ref,
                     rows_ref, buf, zbuf, xt, sem, zsem, *, tiles_a, nblk):
    i = pl.program_id(0)
    n = pl.num_programs(0)
    slot = i % 2

    def zero_block(start_row):
        return pltpu.make_async_copy(zbuf, _tile_span(rows_ref, start_row, MOE_TILE), zsem)

    @pl.when(i == 0)
    def _():
        zbuf[...] = jnp.zeros_like(zbuf)
        for e in range(N_EXPERTS):
            @pl.when(padded_ref[e] > 0)
            def _(e=e):
                zero_block(pend_ref[e] - MOE_TILE).start()

        def tail_start(j, c):
            zero_block(j * MOE_TILE).start()
            return c

        def tail_wait(j, c):
            zero_block(j * MOE_TILE).wait()
            return c

        lax.fori_loop(nu_ref[0], nblk, tail_start, 0)
        for e in range(N_EXPERTS):
            @pl.when(padded_ref[e] > 0)
            def _(e=e):
                zero_block(pend_ref[e] - MOE_TILE).wait()
        lax.fori_loop(nu_ref[0], nblk, tail_wait, 0)

    def slot_rows(s):
        return buf.at[pl.ds(pl.multiple_of(s * _SLOT_ROWS, _SLOT_ROWS), _SLOT_ROWS), :]

    def slot_done(s):
        return pltpu.make_async_copy(slot_rows(s), slot_rows(s), sem.at[s])

    @pl.when(i >= 2)
    def _():
        slot_done(slot).wait()

    def fill(x_ref):
        _to_row_tiles(xt, _rms(x_ref[...]) * g_ref[...])

        def body(t, c):
            v = xt[pl.ds(pl.multiple_of(t * SUBLANES, SUBLANES), SUBLANES), :]
            for k in range(TOP_K):
                off = pl.multiple_of(off_ref[0, k * ROUTE_TILE + t], SUBLANES)
                buf[pl.ds(off, SUBLANES), :] = v
            return c

        lax.fori_loop(0, ROUTE_TILE, body, 0, unroll=8)

    @pl.when(i < tiles_a)
    def _():
        fill(xa_ref)

    @pl.when(i >= tiles_a)
    def _():
        fill(xb_ref)

    _run_copies(i, src_tbl, cnt_tbl, loc_tbl,
                lambda g, l, size: pltpu.make_async_copy(_tile_span(slot_rows(slot), l, size),
                                                         _tile_span(rows_ref, g, size), sem.at[slot]))

    @pl.when(i == n - 1)
    def _():
        slot_done(1 - slot).wait()
        slot_done(slot).wait()


def _dispatch(runs, pend, padded, n_used, off, xa, xb, g, n_rows):
    tile_rows = ROUTE_TILE * SUBLANES
    tiles_a, tiles_b = xa.shape[0] // ROUTE_TILE, xb.shape[0] // ROUTE_TILE
    assert tiles_a + tiles_b >= 2
    grid_spec = pltpu.PrefetchScalarGridSpec(
        num_scalar_prefetch=6, grid=(tiles_a + tiles_b,),
        in_specs=[pl.BlockSpec((None, 1, TOP_K * ROUTE_TILE), lambda i, *_: (i, 0, 0), memory_space=pltpu.SMEM),
                  pl.BlockSpec((ROUTE_TILE, D_MODEL), lambda i, *_: (jnp.minimum(i, tiles_a - 1), 0)),
                  pl.BlockSpec((ROUTE_TILE, D_MODEL), lambda i, *_: (jnp.maximum(i - tiles_a, 0), 0)),
                  pl.BlockSpec((1, D_MODEL), lambda i, *_: (0, 0))],
        out_specs=pl.BlockSpec(memory_space=pl.ANY),
        scratch_shapes=[pltpu.VMEM((2 * _SLOT_ROWS, 128), F32),
                        pltpu.VMEM((MOE_TILE * SUBLANES, 128), F32), pltpu.VMEM((tile_rows, 128), F32),
                        pltpu.SemaphoreType.DMA((2,)), pltpu.SemaphoreType.DMA(())])
    return pl.pallas_call(
        functools.partial(_dispatch_kernel, tiles_a=tiles_a, nblk=n_rows // MOE_TILE), grid_spec=grid_spec,
        out_shape=jax.ShapeDtypeStruct((n_rows * SUBLANES, 128), F32),
        compiler_params=_cparams("arbitrary"), name="dispatch",
    )(*runs, pend, padded, n_used, off, xa, xb, g)


def _moe_kernel(be_ref, nu_ref, nv_ref, next_ref, slot_ref, x_ref, wgu_hbm, bg_ref, bl_ref, wd_hbm, bd_ref,
                perm_ref, o_ref, wgu_buf, wd_buf, wg_s, wl_s, wd_s, sem_gu, sem_d):
    i = pl.program_id(0)
    used = i < nu_ref[0]
    e = be_ref[i]
    new_expert = (i == 0) | (e != be_ref[jnp.maximum(i - 1, 0)])

    def fetch(expert, s):
        return (pltpu.make_async_copy(wgu_hbm.at[expert], wgu_buf.at[s], sem_gu.at[s]),
                pltpu.make_async_copy(wd_hbm.at[expert], wd_buf.at[s], sem_d.at[s]))

    @pl.when(used & new_expert)
    def _():
        s = slot_ref[e]

        @pl.when(i == 0)
        def _():
            for cp in fetch(e, s):
                cp.start()

        @pl.when(next_ref[e] >= 0)
        def _():
            for cp in fetch(next_ref[e], 1 - s):
                cp.start()

        for cp in fetch(e, s):
            cp.wait()
        for c in range(2 * D_FF // 256):
            r = _dot(wgu_buf[s, :, 256 * c:256 * (c + 1)].astype(BF16), perm_ref[...])
            wg_s[:, 128 * c:128 * (c + 1)] = r[:, :128].astype(BF16)
            wl_s[:, 128 * c:128 * (c + 1)] = r[:, 128:].astype(BF16)
        wd_s[...] = wd_buf[s].astype(BF16)

    def expert_rows(rows):
        x = _from_row_tiles(x_ref, rows).astype(BF16)
        glu = jnp.minimum(_dot(x, wg_s[...]) + bg_ref[...], SWIGLU_LIMIT)
        lin = jnp.clip(_dot(x, wl_s[...]) + bl_ref[...], -SWIGLU_LIMIT, SWIGLU_LIMIT)
        act = glu * jax.nn.sigmoid(SWIGLU_ALPHA * glu) * (lin + 1.0)
        _to_row_tiles(o_ref, _dot(act.astype(BF16), wd_s[...]) + bd_ref[...])

    def zero_rows(first, rows):
        o_ref[pl.ds(first * SUBLANES, rows * SUBLANES), :] = jnp.zeros((rows * SUBLANES, 128), F32)

    half = MOE_TILE // 2

    @pl.when(used & (nv_ref[i] > half))
    def _():
        expert_rows(MOE_TILE)

    @pl.when(used & (nv_ref[i] <= half))
    def _():
        expert_rows(half)
        zero_rows(half, half)

    @pl.when(jnp.logical_not(used))
    def _():
        zero_rows(0, MOE_TILE)


def _moe(block_e, n_used, n_valid, next_e, e_slot, x_rows, wgu, bg, bl, wd, bd, perm):
    nblk = x_rows.shape[0] // (MOE_TILE * SUBLANES)
    row = lambda i, be, nu, *_: (jnp.minimum(i, nu[0] - 1), 0)
    wsel = lambda i, be, *_: (be[i], 0, 0)
    grid_spec = pltpu.PrefetchScalarGridSpec(
        num_scalar_prefetch=5,
        grid=(nblk,),
        in_specs=[pl.BlockSpec((MOE_TILE * SUBLANES, 128), row),
                  pl.BlockSpec(memory_space=pl.ANY),
                  pl.BlockSpec((None, 1, D_FF), wsel), pl.BlockSpec((None, 1, D_FF), wsel),
                  pl.BlockSpec(memory_space=pl.ANY), pl.BlockSpec((None, 1, D_MODEL), wsel),
                  pl.BlockSpec((256, 256), lambda i, *_: (0, 0))],
        out_specs=pl.BlockSpec((MOE_TILE * SUBLANES, 128), lambda i, *_: (i, 0)),
        scratch_shapes=[pltpu.VMEM((2, D_MODEL, 2 * D_FF), F32), pltpu.VMEM((2, D_FF, D_MODEL), F32),
                        pltpu.VMEM((D_MODEL, D_FF), BF16), pltpu.VMEM((D_MODEL, D_FF), BF16),
                        pltpu.VMEM((D_FF, D_MODEL), BF16),
                        pltpu.SemaphoreType.DMA((2,)), pltpu.SemaphoreType.DMA((2,))],
    )
    return pl.pallas_call(
        _moe_kernel,
        grid_spec=grid_spec,
        out_shape=jax.ShapeDtypeStruct(x_rows.shape, F32),
        compiler_params=_cparams("arbitrary"),
        name="moe",
    )(block_e, n_used, n_valid, next_e, e_slot, x_rows, wgu, bg, bl, wd, bd, perm)


def _combine_kernel(src_tbl, cnt_tbl, loc_tbl, off_ref, gate_ref, rows_hbm, x1_ref, o_ref, buf, ybuf, sem, *,
                    tile_base):
    i = pl.program_id(0)
    n = pl.num_programs(0)

    def slot_rows(s):
        return buf.at[pl.ds(pl.multiple_of(s * _SLOT_ROWS, _SLOT_ROWS), _SLOT_ROWS), :]

    def fetch(tile, s):
        _run_copies(tile_base + tile, src_tbl, cnt_tbl, loc_tbl,
                    lambda g, l, size: pltpu.make_async_copy(_tile_span(rows_hbm, g, size),
                                                             _tile_span(slot_rows(s), l, size), sem.at[s]))

    @pl.when(i == 0)
    def _():
        fetch(0, 0)

    @pl.when(i + 1 < n)
    def _():
        fetch(i + 1, (i + 1) % 2)

    slot = i % 2
    pltpu.make_async_copy(slot_rows(slot), slot_rows(slot), sem.at[slot]).wait()

    def body(t, c):
        acc = None
        for k in range(TOP_K):
            off = pl.multiple_of(off_ref[0, k * ROUTE_TILE + t], SUBLANES)
            v = gate_ref[0, k * ROUTE_TILE + t] * buf[pl.ds(off, SUBLANES), :]
            acc = v if acc is None else acc + v
        ybuf[pl.ds(pl.multiple_of(t * SUBLANES, SUBLANES), SUBLANES), :] = acc
        return c

    lax.fori_loop(0, ROUTE_TILE, body, 0, unroll=8)
    o_ref[...] = x1_ref[...] + _from_row_tiles(ybuf, ROUTE_TILE)


def _combine(runs, off, gates, rows, x1, tile_base):
    assert tile_base % 2 == 0
    t = x1.shape[0]
    tile_rows = ROUTE_TILE * SUBLANES
    smem_tile = pl.BlockSpec((None, 1, TOP_K * ROUTE_TILE), lambda i, *_: (tile_base + i, 0, 0),
                             memory_space=pltpu.SMEM)
    grid_spec = pltpu.PrefetchScalarGridSpec(
        num_scalar_prefetch=3,
        grid=(t // ROUTE_TILE,),
        in_specs=[smem_tile, smem_tile, pl.BlockSpec(memory_space=pl.ANY),
                  pl.BlockSpec((ROUTE_TILE, D_MODEL), lambda i, *_: (i, 0))],
        out_specs=pl.BlockSpec((ROUTE_TILE, D_MODEL), lambda i, *_: (i, 0)),
        scratch_shapes=[pltpu.VMEM((2 * _SLOT_ROWS, 128), F32), pltpu.VMEM((tile_rows, 128), F32),
                        pltpu.SemaphoreType.DMA((2,))],
    )
    return pl.pallas_call(
        functools.partial(_combine_kernel, tile_base=tile_base),
        grid_spec=grid_spec,
        out_shape=jax.ShapeDtypeStruct((t, D_MODEL), F32),
        compiler_params=_cparams("arbitrary"),
        name="combine",
    )(*runs, off, gates, rows, x1)


def _columns_to_row(cols):
    lane = lax.broadcasted_iota(jnp.int32, (cols[0].shape[0], 128), 1)
    staged = jnp.zeros((cols[0].shape[0], 128), cols[0].dtype)
    for k, col in enumerate(cols):
        staged = jnp.where(lane == k, col, staged)
    rows = staged.T
    return jnp.concatenate([rows[k:k + 1] for k in range(len(cols))], axis=1)


def _topk_kernel(lga_ref, lgb_ref, tri_ref, upper_ref, gate_ref, off_ref, before_ref, cnt_ref, loc_ref, carry, *,
                 tiles_a):
    i = pl.program_id(0)

    @pl.when(i == 0)
    def _():
        carry[...] = jnp.zeros_like(carry)

    l = jnp.where(i < tiles_a, lga_ref[...], lgb_ref[...])
    lane = lax.broadcasted_iota(jnp.int32, l.shape, 1)
    vals, sels = [], []
    for _ in range(TOP_K):
        m = jnp.max(l, axis=-1, keepdims=True)
        idx = jnp.min(jnp.where(l == m, lane, l.shape[1]), axis=-1, keepdims=True)
        sel = lane == idx
        l = jnp.where(sel, -jnp.inf, l)
        vals.append(m)
        sels.append(sel)
    exps = [jnp.exp(v - vals[0]) for v in vals]
    den = exps[0] + exps[1] + exps[2] + exps[3]
    onehot = jnp.where(sels[0] | sels[1] | sels[2] | sels[3], 1.0, 0.0)
    within = _dot(tri_ref[...], onehot.astype(BF16))
    cnt = jnp.sum(onehot, axis=0, keepdims=True)
    cnt_hi = jnp.floor(cnt * (1.0 / 16.0))
    cnt_lo = cnt - 16.0 * cnt_hi
    loc = 16.0 * _dot(cnt_hi.astype(BF16), upper_ref[...]) + _dot(cnt_lo.astype(BF16), upper_ref[...])
    slot_base = ((i % 2) * (ROUTE_TILE * TOP_K)).astype(F32)
    rows = [jnp.sum(jnp.where(s, within + loc, 0.0), axis=-1, keepdims=True) for s in sels]
    gate_ref[...] = _columns_to_row([e / den for e in exps])
    off_ref[...] = ((_columns_to_row(rows) + slot_base) * SUBLANES).astype(jnp.int32)
    before_ref[...] = carry[...]
    cnt_ref[...] = cnt
    loc_ref[...] = loc
    carry[...] = carry[...] + cnt


def _route(lga, lgb):
    tr = ROUTE_TILE
    tiles_a, tiles_b = lga.shape[0] // tr, lgb.shape[0] // tr
    n_tiles = tiles_a + tiles_b
    t = n_tiles * tr
    n_assign = t * TOP_K
    row = lambda i: (i, 0)
    fix = lambda i: (0, 0)
    sd = jax.ShapeDtypeStruct
    k_spec = pl.BlockSpec((None, 1, TOP_K * tr), lambda i: (i, 0, 0))
    t_spec = pl.BlockSpec((None, 1, 128), lambda i: (i, 0, 0))
    tri = (jnp.arange(tr)[:, None] > jnp.arange(tr)[None, :]).astype(BF16)
    upper = (jnp.arange(128)[:, None] < jnp.arange(128)[None, :]).astype(BF16)
    gates, off, before, cnt, loc = pl.pallas_call(
        functools.partial(_topk_kernel, tiles_a=tiles_a),
        grid=(n_tiles,),
        in_specs=[pl.BlockSpec((tr, 128), lambda i: (jnp.minimum(i, tiles_a - 1), 0)),
                  pl.BlockSpec((tr, 128), lambda i: (jnp.maximum(i - tiles_a, 0), 0)),
                  pl.BlockSpec((tr, tr), fix), pl.BlockSpec((128, 128), fix)],
        out_specs=[k_spec, k_spec, t_spec, t_spec, t_spec],
        out_shape=[sd((n_tiles, 1, TOP_K * tr), F32), sd((n_tiles, 1, TOP_K * tr), jnp.int32),
                   sd((n_tiles, 1, 128), F32), sd((n_tiles, 1, 128), F32), sd((n_tiles, 1, 128), F32)],
        scratch_shapes=[pltpu.VMEM((1, 128), F32)],
        compiler_params=_cparams("arbitrary"),
        name="topk",
    )(lga, lgb, tri, upper)
    table = lambda a: a[:, 0, :N_EXPERTS].astype(jnp.int32)
    before, cnt, run_loc = table(before), table(cnt), table(loc)
    counts = before[-1] + cnt[-1]
    padded = ((counts + MOE_TILE - 1) // MOE_TILE) * MOE_TILE
    pend = jnp.cumsum(padded)
    pstart = pend - padded
    run_src = pstart[None, :] + before
    nblk = (n_assign + MOE_TILE - 1) // MOE_TILE + N_EXPERTS
    n_used = (pend[-1] // MOE_TILE).astype(jnp.int32)
    block_start = jnp.arange(nblk, dtype=jnp.int32) * MOE_TILE
    block_e = jnp.sum(pend[None, :] <= jnp.minimum(block_start, pend[-1] - 1)[:, None], axis=1)
    block_e = jnp.minimum(block_e, N_EXPERTS - 1).astype(jnp.int32)
    region_end = jnp.sum(jnp.where(block_e[:, None] == jnp.arange(N_EXPERTS)[None, :], (pstart + counts)[None, :], 0),
                         axis=1)
    n_valid = jnp.clip(region_end - block_start, 0, MOE_TILE).astype(jnp.int32)
    ids = jnp.arange(N_EXPERTS, dtype=jnp.int32)
    nonempty = counts > 0
    next_e = jnp.min(jnp.where((ids[None, :] > ids[:, None]) & nonempty[None, :], ids[None, :], N_EXPERTS), axis=1)
    next_e = jnp.where(next_e == N_EXPERTS, -1, next_e).astype(jnp.int32)
    e_slot = ((jnp.cumsum(nonempty.astype(jnp.int32)) - nonempty.astype(jnp.int32)) % 2).astype(jnp.int32)
    blocks = (block_e, n_used.reshape(1), n_valid, next_e, e_slot)
    runs = (run_src.reshape(-1), cnt.reshape(-1), run_loc.reshape(-1))
    return gates, off, runs, blocks, pend, padded, nblk * MOE_TILE


def kernel(x_prompt, x_sample, cache_k, cache_v, state_ssm_re, state_ssm_im, norm_mix_g, w_in, q_norm_g, k_norm_g, attn_sinks, ssm_lambda_re, ssm_lambda_im, ssm_b_re, ssm_b_im, ssm_c_re, ssm_c_im, ssm_d, ssm_log_dt, w_glu, b_glu, attn_out_norm_g, ssm_out_norm_g, w_out, norm_ffn_g, w_router, b_router, w_gate_up, b_gate_up, w_down, b_down):
    depth = w_in.shape[0]
    assert depth == 1
    bp, sp, _ = x_prompt.shape
    bs, ss, _ = x_sample.shape
    tp, ts = bp * sp, bs * ss
    assert bp == SUBLANES and sp % S5_CHUNK == 0 and bs % SUBLANES == 0

    perm = jnp.asarray(_Q_PERM)
    w_in0 = w_in[0]
    wq = w_in0[:, :D_ATTN].reshape(D_MODEL, N_HEADS, HEAD_DIM)[:, perm].reshape(D_MODEL, D_ATTN)
    w_in_b = jnp.concatenate([wq, w_in0[:, D_ATTN:]], axis=1).astype(BF16)
    qkg = jnp.concatenate([jnp.tile(q_norm_g[0], N_HEADS), jnp.tile(k_norm_g[0], N_KV_HEADS)])[None]
    pmat = jnp.kron(jnp.eye(256 // HEAD_DIM, dtype=F32),
                    jnp.full((HEAD_DIM, HEAD_DIM), 1.0 / HEAD_DIM, F32)).astype(BF16)
    g_mix = norm_mix_g[0][None]
    sinks = attn_sinks[0]
    g_attn = attn_out_norm_g[0].reshape(N_HEADS, HEAD_DIM)[perm].reshape(1, D_ATTN)
    w_out0 = w_out[0]
    w_out_a = w_out0[:D_ATTN].reshape(N_HEADS, HEAD_DIM, D_MODEL)[perm].reshape(D_ATTN, D_MODEL).astype(BF16)
    w_out_s = w_out0[D_ATTN:].astype(BF16)
    g_ssm = ssm_out_norm_g[0][None]
    g_ffn = norm_ffn_g[0][None]
    w_r = jnp.pad(w_router[0], ((0, 0), (0, 128 - N_EXPERTS))).astype(BF16)
    b_r = jnp.pad(b_router[0], (0, 128 - N_EXPERTS), constant_values=NEG_INF)[None]

    a_re, a_im, bb_re, bb_im = _s5_prep(
        ssm_lambda_re[0], ssm_lambda_im[0], ssm_log_dt[0][:, None],
        jnp.swapaxes(ssm_b_re[0], 1, 2), jnp.swapaxes(ssm_b_im[0], 1, 2))
    chan_g = jnp.arange(D_SSM)[:, None] // SSM_GROUP
    state_g = jnp.arange(D_STATE)[None, :] // SSM_STATE
    bd_b = lambda bb: jnp.where(chan_g == state_g, jnp.tile(bb.reshape(D_SSM, SSM_STATE), (1, N_SSM_GROUPS)), 0.0)
    wb = jnp.concatenate([bd_b(bb_re), bd_b(bb_im)], axis=1).astype(BF16)
    bd_c = lambda c: jnp.where(state_g.T == chan_g.T,
                               jnp.tile(jnp.swapaxes(c, 1, 2).reshape(D_STATE, SSM_GROUP), (1, N_SSM_GROUPS)), 0.0)
    wc = jnp.concatenate([bd_c(ssm_c_re[0]), -bd_c(ssm_c_im[0])], axis=0).astype(BF16)
    a_re, a_im = a_re.reshape(1, D_STATE), a_im.reshape(1, D_STATE)
    d_skip = ssm_d[0].reshape(1, D_SSM)
    w_glu_b = w_glu[0].astype(BF16)
    b_glu0 = b_glu[0][None]

    b_g = b_gate_up[0][:, None, 0::2]
    b_l = b_gate_up[0][:, None, 1::2]
    b_d = b_down[0][:, None, :]
    idx = jnp.arange(256)
    deint = (idx[None, :] == jnp.where(idx % 2 == 0, idx // 2, 128 + idx // 2)[:, None]).astype(BF16)

    xp2 = x_prompt.reshape(tp, D_MODEL)
    xs2 = x_sample.reshape(ts, D_MODEL)
    qp, kp, vp, up = _in_proj(xp2, g_mix, w_in_b, qkg, pmat)
    qs, ks, vs, us = _in_proj(xs2, g_mix, w_in_b, qkg, pmat)

    kp3, vp3 = kp.reshape(bp, sp, D_KV), vp.reshape(bp, sp, D_KV)
    attn_p = _band_attention(sinks, qp.reshape(bp, sp, D_ATTN), kp3, vp3, g_attn).reshape(tp, D_ATTN)
    ck = cache_k[0].reshape(bs, -1, D_KV)
    cv = cache_v[0].reshape(bs, -1, D_KV)
    attn_s, new_ck, new_cv = _cache_attention(sinks, qs, ks, vs, ck, cv, g_attn, ss)

    zeros_p = jnp.zeros((bp, D_STATE), F32)
    s5_args = (a_re, a_im, wb, wc, d_skip, w_glu_b, b_glu0, g_ssm)
    ssm_p, hr_p, hi_p = _s5(up.reshape(bp, sp, D_SSM), zeros_p, zeros_p, *s5_args,
                            tt=S5_CHUNK, time_chunked=True)
    ssm_s, hr_s, hi_s = _s5(us.reshape(bs // SUBLANES, SUBLANES * ss, D_SSM),
                            state_ssm_re[0].reshape(bs, D_STATE), state_ssm_im[0].reshape(bs, D_STATE),
                            *s5_args, tt=ss, time_chunked=False)

    x1p, lgp = _out_proj(attn_p, ssm_p.reshape(tp, D_SSM), xp2, w_out_a, w_out_s, g_ffn, w_r, b_r)
    x1s, lgs = _out_proj(attn_s, ssm_s.reshape(ts, D_SSM), xs2, w_out_a, w_out_s, g_ffn, w_r, b_r)

    gates, off, runs, blocks, pend, padded, n_rows = _route(lgp, lgs)
    x_rows = _dispatch(runs, pend, padded, blocks[1], off, x1p, x1s, g_ffn, n_rows)
    out_rows = _moe(*blocks, x_rows, w_gate_up[0], b_g, b_l, w_down[0], b_d, deint)
    yp = _combine(runs, off, gates, out_rows, x1p, 0).reshape(bp, sp, D_MODEL)
    ys = _combine(runs, off, gates, out_rows, x1s, tp // ROUTE_TILE).reshape(bs, ss, D_MODEL)

    kv5 = lambda a, b_: a.reshape(b_, -1, N_KV_HEADS, HEAD_DIM)
    new_kp = kv5(kp3[:, -WINDOW:], bp)[None]
    new_vp = kv5(vp3[:, -WINDOW:], bp)[None]
    new_ks = kv5(new_ck, bs)[None]
    new_vs = kv5(new_cv, bs)[None]
    st = lambda h, b_: h.reshape(1, b_, N_SSM_GROUPS, SSM_STATE)
    return (yp, ys, new_kp, new_vp, st(hr_p, bp), st(hi_p, bp),
            new_ks, new_vs, st(hr_s, bs), st(hi_s, bs))
```

```python
import functools
import math

import jax
import jax.numpy as jnp
from jax import lax
from jax.experimental import pallas as pl
from jax.experimental.pallas import tpu as pltpu

F32 = jnp.float32
BF16 = jnp.bfloat16

D_MODEL = 1024
D_ATTN = 512
D_SSM = 512
HEAD_DIM = 64
N_HEADS = 8
N_KV_HEADS = 2
D_KV = N_KV_HEADS * HEAD_DIM
WINDOW = 128
SSM_GROUP = 16
N_SSM_GROUPS = 32
SSM_STATE = 64
D_STATE = N_SSM_GROUPS * SSM_STATE
N_EXPERTS = 32
TOP_K = 4
D_FF = 1024
SWIGLU_LIMIT = 7.0
SWIGLU_ALPHA = 1.702
RMS_EPS = 1e-6
NEG_INF = -1e30
D_IN_PROJ = D_ATTN + 2 * D_KV + D_SSM
D_QK = D_ATTN + D_KV

SUBLANES = 8
VMEM_LIMIT = 56 * 1024 * 1024

ROW_TILE = 512
MOE_TILE = 512
ATTN_BLOCKS = 4
ROUTE_TILE = 512
_SLOT_ROWS = ROUTE_TILE * TOP_K * SUBLANES
_RUN_SIZES = tuple(1 << b for b in range(ROUTE_TILE.bit_length() - 1, -1, -1))
S5_CHUNK = 128
S5_COLS = 512

_Q_PERM = (0, 4, 1, 5, 2, 6, 3, 7)


def _cparams(*sem):
    return pltpu.CompilerParams(dimension_semantics=sem, vmem_limit_bytes=VMEM_LIMIT)


def _dot(a, b):
    return jnp.dot(a, b, preferred_element_type=F32)


def _rms(x):
    return x * lax.rsqrt(jnp.mean(x * x, axis=-1, keepdims=True) + RMS_EPS)


def _in_proj_kernel(x_ref, g_ref, w_ref, qkg_ref, p_ref, q_ref, k_ref, v_ref, u_ref):
    xn = _rms(x_ref[...]) * g_ref[...]
    h = _dot(xn.astype(BF16), w_ref[...])
    qk = h[:, :D_QK]
    sq = (qk * qk).astype(BF16)
    p = p_ref[...]
    ms = jnp.concatenate(
        [_dot(sq[:, 0:256], p), _dot(sq[:, 256:512], p), _dot(sq[:, 512:640], p[:128, :128])],
        axis=-1)
    qkn = qk * lax.rsqrt(ms + RMS_EPS) * qkg_ref[...]
    q_ref[...] = qkn[:, :D_ATTN].astype(BF16)
    k_ref[...] = qkn[:, D_ATTN:]
    v_ref[...] = h[:, D_QK:D_QK + D_KV]
    u_ref[...] = h[:, D_QK + D_KV:]


def _in_proj(x2d, g, w, qkg, pmat):
    t = x2d.shape[0]
    tm = min(ROW_TILE, t)
    row = lambda i: (i, 0)
    fix = lambda i: (0, 0)
    return pl.pallas_call(
        _in_proj_kernel,
        grid=(t // tm,),
        in_specs=[pl.BlockSpec((tm, D_MODEL), row), pl.BlockSpec((1, D_MODEL), fix),
                  pl.BlockSpec((D_MODEL, D_IN_PROJ), fix), pl.BlockSpec((1, D_QK), fix),
                  pl.BlockSpec((256, 256), fix)],
        out_specs=[pl.BlockSpec((tm, D_ATTN), row), pl.BlockSpec((tm, D_KV), row),
                   pl.BlockSpec((tm, D_KV), row), pl.BlockSpec((tm, D_SSM), row)],
        out_shape=[jax.ShapeDtypeStruct((t, D_ATTN), BF16), jax.ShapeDtypeStruct((t, D_KV), F32),
                   jax.ShapeDtypeStruct((t, D_KV), F32), jax.ShapeDtypeStruct((t, D_SSM), F32)],
        compiler_params=_cparams("parallel"),
        name="in_proj",
    )(x2d, g, w, qkg, pmat)


def _softmax_pv(s_blocks, v_blocks, sink):
    m = sink
    for s in s_blocks:
        m = jnp.maximum(m, jnp.max(s, axis=-1, keepdims=True))
    den = jnp.exp(sink - m)
    acc = None
    for s, v in zip(s_blocks, v_blocks):
        p = jnp.exp(s - m)
        den = den + jnp.sum(p, axis=-1, keepdims=True)
        pv = _dot(p.astype(BF16), v)
        acc = pv if acc is None else acc + pv
    return acc / den


def _band_attn_kernel(sink_ref, q_ref, kp_ref, kc_ref, vp_ref, vc_ref, g_ref, o_ref):
    i = pl.program_id(1)
    row = lax.broadcasted_iota(jnp.int32, (WINDOW, 2 * WINDOW), 0)
    col = lax.broadcasted_iota(jnp.int32, (WINDOW, 2 * WINDOW), 1)
    band = (col > row) & (col <= row + WINDOW)
    lane = lax.broadcasted_iota(jnp.int32, (WINDOW, 128), 1)
    low = lane < HEAD_DIM
    zero = jnp.zeros((), BF16)
    for j in range(ATTN_BLOCKS):
        cur = slice(WINDOW * j, WINDOW * (j + 1))
        q = q_ref[cur, :]
        k_prev = kp_ref[...] if j == 0 else kc_ref[WINDOW * (j - 1):WINDOW * j, :]
        v_prev = vp_ref[...] if j == 0 else vc_ref[WINDOW * (j - 1):WINDOW * j, :]
        kb = jnp.concatenate([k_prev, kc_ref[cur, :]], axis=0).astype(BF16)
        vb = jnp.concatenate([v_prev, vc_ref[cur, :]], axis=0).astype(BF16)
        mask = band & ((col >= WINDOW) | (i > 0)) if j == 0 else band
        outs = []
        for pair in range(N_HEADS // 2):
            qp = q[:, 128 * pair:128 * (pair + 1)]
            halves = []
            for par in range(2):
                qm = jnp.where(low if par == 0 else ~low, qp, zero)
                s = lax.dot_general(qm, kb, (((1,), (1,)), ((), ())), preferred_element_type=F32)
                s = jnp.where(mask, s * (HEAD_DIM ** -0.5), NEG_INF)
                halves.append(_softmax_pv([s], [vb], sink_ref[_Q_PERM[2 * pair + par]]))
            outs.append(jnp.where(low, halves[0], halves[1]))
        o = jnp.concatenate(outs, axis=-1)
        o_ref[cur, :] = (_rms(o) * g_ref[...]).astype(BF16)


def _band_attention(sinks, q, k, v, g):
    b, s, _ = q.shape
    qb = WINDOW * ATTN_BLOCKS
    cur = lambda bi, i: (bi, i, 0)
    prev = lambda bi, i: (bi, jnp.maximum(i * ATTN_BLOCKS - 1, 0), 0)
    return pl.pallas_call(
        _band_attn_kernel,
        grid=(b, s // qb),
        in_specs=[pl.BlockSpec(memory_space=pltpu.SMEM),
                  pl.BlockSpec((None, qb, D_ATTN), cur),
                  pl.BlockSpec((None, WINDOW, D_KV), prev), pl.BlockSpec((None, qb, D_KV), cur),
                  pl.BlockSpec((None, WINDOW, D_KV), prev), pl.BlockSpec((None, qb, D_KV), cur),
                  pl.BlockSpec((1, D_ATTN), lambda bi, i: (0, 0))],
        out_specs=pl.BlockSpec((None, qb, D_ATTN), cur),
        out_shape=jax.ShapeDtypeStruct((b, s, D_ATTN), BF16),
        compiler_params=_cparams("parallel", "parallel"),
        name="band_attn",
    )(sinks, q, k, k, v, v, g)


_PAIR_ROWS = 8
_CACHE_BB = 8


def _cache_attn_kernel(sink_ref, q_ref, kn_ref, vn_ref, ck_ref, cv_ref, g_ref, o_ref, nk_ref, nv_ref, *, n_new):
    n_buf = ck_ref.shape[1]
    rows_blk = _CACHE_BB * n_new
    for b in range(_CACHE_BB):
        for new_ref, cache_ref, fresh_ref in ((nk_ref, ck_ref, kn_ref), (nv_ref, cv_ref, vn_ref)):
            new_ref[b, 0:n_buf - n_new, :] = cache_ref[b, n_new:n_buf, :]
            new_ref[b, n_buf - n_new:n_buf, :] = fresh_ref[n_new * b:n_new * (b + 1), :]
    knew = kn_ref[...].astype(BF16)
    vnew = vn_ref[...].astype(BF16)
    lane = lax.broadcasted_iota(jnp.int32, (_PAIR_ROWS, 128), 1)
    low = lane < HEAD_DIM
    zero = jnp.zeros((), BF16)
    n_stack = N_HEADS * _PAIR_ROWS
    r = lax.broadcasted_iota(jnp.int32, (n_stack, 1), 0) % _PAIR_ROWS
    r_seq, r_tok = r // n_new, r % n_new
    colc = lax.broadcasted_iota(jnp.int32, (n_stack, 2 * n_buf), 1)
    c_seq, c_pos = colc // n_buf, colc % n_buf
    mask_c = (c_seq == r_seq) & (c_pos + WINDOW > r_tok + n_buf)
    coln = lax.broadcasted_iota(jnp.int32, (n_stack, rows_blk), 1)
    sink_col = jnp.concatenate(
        [jnp.full((_PAIR_ROWS, 1), sink_ref[_Q_PERM[h]], F32) for h in range(N_HEADS)], axis=0)
    for sp in range(_CACHE_BB // 2):
        q = q_ref[_PAIR_ROWS * sp:_PAIR_ROWS * (sp + 1), :]
        pieces = []
        for pair in range(N_HEADS // 2):
            qp = q[:, 128 * pair:128 * (pair + 1)]
            pieces.append(jnp.where(low, qp, zero))
            pieces.append(jnp.where(low, zero, qp))
        qs = jnp.concatenate(pieces, axis=0)
        kc = jnp.concatenate([ck_ref[2 * sp], ck_ref[2 * sp + 1]], axis=0).astype(BF16)
        vc = jnp.concatenate([cv_ref[2 * sp], cv_ref[2 * sp + 1]], axis=0).astype(BF16)
        nt = (((1,), (1,)), ((), ()))
        s_c = lax.dot_general(qs, kc, nt, preferred_element_type=F32) * (HEAD_DIM ** -0.5)
        s_n = lax.dot_general(qs, knew, nt, preferred_element_type=F32) * (HEAD_DIM ** -0.5)
        s_c = jnp.where(mask_c, s_c, NEG_INF)
        n_seq, n_tok = coln // n_new - 2 * sp, coln % n_new
        mask_n = (n_seq == r_seq) & (n_tok <= r_tok)
        s_n = jnp.where(mask_n, s_n, NEG_INF)
        o = _softmax_pv([s_c, s_n], [vc, vnew], sink_col)
        outs = [jnp.where(low, o[16 * pair:16 * pair + 8], o[16 * pair + 8:16 * pair + 16])
                for pair in range(N_HEADS // 2)]
        oo = jnp.concatenate(outs, axis=-1)
        o_ref[_PAIR_ROWS * sp:_PAIR_ROWS * (sp + 1), :] = (_rms(oo) * g_ref[...]).astype(BF16)


def _cache_attention(sinks, q, k, v, cache_k, cache_v, g, n_new):
    t = q.shape[0]
    nb, n_buf, _ = cache_k.shape
    assert n_new * 2 == _PAIR_ROWS and nb % _CACHE_BB == 0 and n_buf == WINDOW
    rows = _CACHE_BB * n_new
    row = lambda i: (i, 0)
    cache_spec = pl.BlockSpec((_CACHE_BB, n_buf, D_KV), lambda i: (i, 0, 0))
    return pl.pallas_call(
        functools.partial(_cache_attn_kernel, n_new=n_new),
        grid=(nb // _CACHE_BB,),
        in_specs=[pl.BlockSpec(memory_space=pltpu.SMEM),
                  pl.BlockSpec((rows, D_ATTN), row), pl.BlockSpec((rows, D_KV), row),
                  pl.BlockSpec((rows, D_KV), row), cache_spec, cache_spec,
                  pl.BlockSpec((1, D_ATTN), lambda i: (0, 0))],
        out_specs=[pl.BlockSpec((rows, D_ATTN), row), cache_spec, cache_spec],
        out_shape=[jax.ShapeDtypeStruct((t, D_ATTN), BF16), jax.ShapeDtypeStruct(cache_k.shape, F32),
                   jax.ShapeDtypeStruct(cache_v.shape, F32)],
        compiler_params=_cparams("parallel"),
        name="cache_attn",
    )(sinks, q, k, v, cache_k, cache_v, g)


def _s5_prep_kernel(lre_ref, lim_ref, ldt_ref, bre_ref, bim_ref, are_ref, aim_ref, bbre_ref, bbim_ref):
    dt = jnp.exp(ldt_ref[...])
    l_re = jnp.minimum(lre_ref[...], -1e-4)
    l_im = lim_ref[...]
    mag = jnp.exp(l_re * dt)
    a_re = mag * jnp.cos(l_im * dt)
    a_im = mag * jnp.sin(l_im * dt)
    den = l_re * l_re + l_im * l_im
    n_re = a_re - 1.0
    z_re = (n_re * l_re + a_im * l_im) / den
    z_im = (a_im * l_re - n_re * l_im) / den
    are_ref[...] = a_re
    aim_ref[...] = a_im
    br, bi = bre_ref[...], bim_ref[...]
    zr, zi = z_re[:, None, :], z_im[:, None, :]
    bbre_ref[...] = zr * br - zi * bi
    bbim_ref[...] = zr * bi + zi * br


def _s5_prep(lam_re, lam_im, log_dt, b_re_t, b_im_t):
    g, p = lam_re.shape
    sd = jax.ShapeDtypeStruct
    return pl.pallas_call(
        _s5_prep_kernel,
        out_shape=[sd((g, p), F32), sd((g, p), F32), sd(b_re_t.shape, F32), sd(b_re_t.shape, F32)],
        name="s5_prep",
    )(lam_re, lam_im, log_dt, b_re_t, b_im_t)


def _s5_kernel(u_ref, h0r_ref, h0i_ref, ar_ref, ai_ref, wb_ref, wc_ref, d_ref, wglu_ref, bglu_ref,
               g_ref, o_ref, hr_ref, hi_ref, bu_ref, hs_ref, usc_ref, ysc_ref, *, tt):
    j = pl.program_id(1)
    rows = SUBLANES * tt
    n_tiles = D_STATE // 128
    time_major = tt % SUBLANES == 0

    @pl.when(j == 0)
    def _():
        hs_ref[:, :D_STATE] = h0r_ref[...]
        hs_ref[:, D_STATE:] = h0i_ref[...]

    u = u_ref[...].reshape(rows, D_SSM)
    if time_major:
        for c in range(D_SSM // 128):
            for b in range(SUBLANES):
                usc_ref[c, pl.ds(b, tt, stride=SUBLANES), :] = u[b * tt:(b + 1) * tt, 128 * c:128 * (c + 1)]
        ub = jnp.concatenate([usc_ref[c] for c in range(D_SSM // 128)], axis=-1).astype(BF16)
    else:
        ub = u.astype(BF16)

    def step_rows(t):
        if time_major:
            return pl.ds(pl.multiple_of(t * SUBLANES, SUBLANES), SUBLANES)
        return pl.ds(t, SUBLANES, stride=tt)

    for n in range(2 * D_STATE // 256):
        band = (n % (D_STATE // 256)) // 2
        res = _dot(ub[:, 128 * band:128 * (band + 1)],
                   wb_ref[128 * band:128 * (band + 1), 256 * n:256 * (n + 1)])
        bu_ref[2 * n] = res[:, :128]
        bu_ref[2 * n + 1] = res[:, 128:]

    tiles_per_pass = S5_COLS // 128
    for c0 in range(0, n_tiles, tiles_per_pass):
        tiles = range(c0, c0 + tiles_per_pass)
        a_r = [jnp.broadcast_to(ar_ref[:, 128 * c:128 * (c + 1)], (SUBLANES, 128)) for c in tiles]
        a_i = [jnp.broadcast_to(ai_ref[:, 128 * c:128 * (c + 1)], (SUBLANES, 128)) for c in tiles]

        def step(t, carry, tiles=tiles, a_r=a_r, a_i=a_i):
            at_t = step_rows(t)
            out = []
            for k, c in enumerate(tiles):
                h_r, h_i = carry[2 * k], carry[2 * k + 1]
                n_r = a_r[k] * h_r - a_i[k] * h_i + bu_ref[c, at_t, :]
                n_i = a_r[k] * h_i + a_i[k] * h_r + bu_ref[n_tiles + c, at_t, :]
                bu_ref[c, at_t, :] = n_r
                bu_ref[n_tiles + c, at_t, :] = n_i
                out += [n_r, n_i]
            return tuple(out)

        init = []
        for c in tiles:
            init += [hs_ref[:, 128 * c:128 * (c + 1)], hs_ref[:, D_STATE + 128 * c:D_STATE + 128 * (c + 1)]]
        fin = lax.fori_loop(0, tt, step, tuple(init), unroll=min(tt, 8))
        for k, c in enumerate(tiles):
            hs_ref[:, 128 * c:128 * (c + 1)] = fin[2 * k]
            hs_ref[:, D_STATE + 128 * c:D_STATE + 128 * (c + 1)] = fin[2 * k + 1]

    def h_cols(first_tile):
        return jnp.concatenate([bu_ref[first_tile + k] for k in range(4)], axis=-1).astype(BF16)

    ys = []
    for m in range(D_SSM // 128):
        y = _dot(h_cols(4 * m), wc_ref[512 * m:512 * (m + 1), 128 * m:128 * (m + 1)])
        y = y + _dot(h_cols(n_tiles + 4 * m),
                     wc_ref[D_STATE + 512 * m:D_STATE + 512 * (m + 1), 128 * m:128 * (m + 1)])
        if time_major:
            ysc_ref[m] = y
            y = jnp.concatenate([ysc_ref[m, pl.ds(b, tt, stride=SUBLANES), :] for b in range(SUBLANES)], axis=0)
        ys.append(y)
    y = jnp.concatenate(ys, axis=-1) + d_ref[...] * u
    z = _dot(jax.nn.gelu(y).astype(BF16), wglu_ref[...]) + bglu_ref[...]
    s = z[:, :D_SSM] * jax.nn.sigmoid(z[:, D_SSM:])
    o_ref[...] = (_rms(s) * g_ref[...]).astype(BF16).reshape(o_ref.shape)

    @pl.when(j == pl.num_programs(1) - 1)
    def _():
        hr_ref[...] = hs_ref[:, :D_STATE]
        hi_ref[...] = hs_ref[:, D_STATE:]


def _s5(u, h0r, h0i, a_re, a_im, wb, wc, d, wglu, bglu, g, *, tt, time_chunked):
    nbg = h0r.shape[0] // SUBLANES
    if time_chunked:
        nchunks = u.shape[1] // tt
        u_spec = pl.BlockSpec((SUBLANES, tt, D_SSM), lambda gi, j: (gi, j, 0))
    else:
        nchunks = 1
        u_spec = pl.BlockSpec((None, SUBLANES * tt, D_SSM), lambda gi, j: (gi, 0, 0))
    fix = lambda gi, j: (0, 0)
    st_spec = pl.BlockSpec((SUBLANES, D_STATE), lambda gi, j: (gi, 0))
    sd = jax.ShapeDtypeStruct
    return pl.pallas_call(
        functools.partial(_s5_kernel, tt=tt),
        grid=(nbg, nchunks),
        in_specs=[u_spec, st_spec, st_spec,
                  pl.BlockSpec((1, D_STATE), fix), pl.BlockSpec((1, D_STATE), fix),
                  pl.BlockSpec((D_SSM, 2 * D_STATE), fix), pl.BlockSpec((2 * D_STATE, D_SSM), fix),
                  pl.BlockSpec((1, D_SSM), fix), pl.BlockSpec((D_SSM, 2 * D_SSM), fix),
                  pl.BlockSpec((1, 2 * D_SSM), fix), pl.BlockSpec((1, D_SSM), fix)],
        out_specs=[u_spec, st_spec, st_spec],
        out_shape=[sd(u.shape, BF16), sd(h0r.shape, F32), sd(h0r.shape, F32)],
        scratch_shapes=[pltpu.VMEM((2 * D_STATE // 128, SUBLANES * tt, 128), F32),
                        pltpu.VMEM((SUBLANES, 2 * D_STATE), F32),
                        pltpu.VMEM((D_SSM // 128, SUBLANES * tt, 128), F32),
                        pltpu.VMEM((D_SSM // 128, SUBLANES * tt, 128), F32)],
        compiler_params=_cparams("parallel", "arbitrary"),
        name="s5",
    )(u, h0r, h0i, a_re, a_im, wb, wc, d, wglu, bglu, g)


def _to_row_tiles(ref, val, first=0):
    rows = val.shape[0]
    for c in range(D_MODEL // 128):
        ref[pl.ds(first * SUBLANES + c, rows, stride=SUBLANES), :] = val[:, 128 * c:128 * (c + 1)]


def _from_row_tiles(ref, rows, lead=(), first=0):
    return jnp.concatenate(
        [ref[(*lead, pl.ds(first * SUBLANES + c, rows, stride=SUBLANES), slice(None))]
         for c in range(D_MODEL // 128)], axis=-1)


def _route_tile(logits, tile, tri_ref, upper_ref, carry, gate_ref, off_ref, before_ref, cnt_ref, loc_ref):
    l = logits
    lane = lax.broadcasted_iota(jnp.int32, l.shape, 1)
    vals, sels = [], []
    for _ in range(TOP_K):
        m = jnp.max(l, axis=-1, keepdims=True)
        idx = jnp.min(jnp.where(l == m, lane, l.shape[1]), axis=-1, keepdims=True)
        sel = lane == idx
        l = jnp.where(sel, -jnp.inf, l)
        vals.append(m)
        sels.append(sel)
    exps = [jnp.exp(v - vals[0]) for v in vals]
    den = exps[0] + exps[1] + exps[2] + exps[3]
    onehot = jnp.where(sels[0] | sels[1] | sels[2] | sels[3], 1.0, 0.0)
    within = _dot(tri_ref[...], onehot.astype(BF16))
    cnt = jnp.sum(onehot, axis=0, keepdims=True)
    cnt_hi = jnp.floor(cnt * (1.0 / 16.0))
    cnt_lo = cnt - 16.0 * cnt_hi
    loc = 16.0 * _dot(cnt_hi.astype(BF16), upper_ref[...]) + _dot(cnt_lo.astype(BF16), upper_ref[...])
    slot_base = ((tile % 2) * (ROUTE_TILE * TOP_K)).astype(F32)
    rows = [jnp.sum(jnp.where(s, within + loc, 0.0), axis=-1, keepdims=True) for s in sels]
    gate_ref[...] = _columns_to_row([e / den for e in exps])
    off_ref[...] = ((_columns_to_row(rows) + slot_base) * SUBLANES).astype(jnp.int32)
    before_ref[...] = carry[...]
    cnt_ref[...] = cnt
    loc_ref[...] = loc
    carry[...] = carry[...] + cnt


def _out_proj_kernel(a_ref, s_ref, x_ref, wa_ref, ws_ref, g_ref, wr_ref, br_ref, tri_ref, upper_ref, cin_ref,
                     x1_ref, gate_ref, off_ref, before_ref, cnt_ref, loc_ref, cout_ref, carry, *, tile_base):
    i = pl.program_id(0)

    @pl.when(i == 0)
    def _():
        carry[...] = cin_ref[...]

    x1 = x_ref[...] + _dot(a_ref[...], wa_ref[...]) + _dot(s_ref[...], ws_ref[...])
    x1_ref[...] = x1
    xn = _rms(x1) * g_ref[...]
    logits = _dot(xn.astype(BF16), wr_ref[...]) + br_ref[...]
    _route_tile(logits, tile_base + i, tri_ref, upper_ref, carry, gate_ref, off_ref, before_ref, cnt_ref, loc_ref)
    cout_ref[...] = carry[...]


def _out_proj(attn_n, ssm_n, x2d, wa, ws, g, wr, br, tri, upper, counts_in, tile_base):
    t = x2d.shape[0]
    tm = ROUTE_TILE
    n_tiles = t // tm
    row = lambda i: (i, 0)
    fix = lambda i: (0, 0)
    sd = jax.ShapeDtypeStruct
    k_spec = pl.BlockSpec((None, 1, TOP_K * tm), lambda i: (i, 0, 0))
    t_spec = pl.BlockSpec((None, 1, 128), lambda i: (i, 0, 0))
    return pl.pallas_call(
        functools.partial(_out_proj_kernel, tile_base=tile_base),
        grid=(n_tiles,),
        in_specs=[pl.BlockSpec((tm, D_ATTN), row), pl.BlockSpec((tm, D_SSM), row),
                  pl.BlockSpec((tm, D_MODEL), row),
                  pl.BlockSpec((D_ATTN, D_MODEL), fix), pl.BlockSpec((D_SSM, D_MODEL), fix),
                  pl.BlockSpec((1, D_MODEL), fix), pl.BlockSpec((D_MODEL, 128), fix),
                  pl.BlockSpec((1, 128), fix), pl.BlockSpec((tm, tm), fix), pl.BlockSpec((128, 128), fix),
                  pl.BlockSpec((1, 128), fix)],
        out_specs=[pl.BlockSpec((tm, D_MODEL), row), k_spec, k_spec, t_spec, t_spec, t_spec,
                   pl.BlockSpec((1, 128), fix)],
        out_shape=[sd((t, D_MODEL), F32), sd((n_tiles, 1, TOP_K * tm), F32), sd((n_tiles, 1, TOP_K * tm), jnp.int32),
                   sd((n_tiles, 1, 128), F32), sd((n_tiles, 1, 128), F32), sd((n_tiles, 1, 128), F32),
                   sd((1, 128), F32)],
        scratch_shapes=[pltpu.VMEM((1, 128), F32)],
        compiler_params=_cparams("arbitrary"),
        name="out_proj",
    )(attn_n, ssm_n, x2d, wa, ws, g, wr, br, tri, upper, counts_in)


def _tile_span(ref, first_row, n_rows, lead=()):
    start = pl.multiple_of(first_row * SUBLANES, SUBLANES)
    return ref.at[(*lead, pl.ds(start, n_rows * SUBLANES), slice(None))]


def _run_copies(tile, src_tbl, cnt_tbl, loc_tbl, make_copy):
    def body(e, c):
        j = tile * N_EXPERTS + e
        cnt, src, loc = cnt_tbl[j], src_tbl[j], loc_tbl[j]
        off = 0
        for size in _RUN_SIZES:
            @pl.when((cnt & size) != 0)
            def _(off=off, size=size):
                make_copy(src + off, loc + off, size).start()
            off = off + (cnt & size)
        return c

    lax.fori_loop(0, N_EXPERTS, body, 0)


def _dispatch_kernel(src_tbl, cnt_tbl, loc_tbl, pend_ref, padded_ref, nu_ref, off_ref, xa_ref, xb_ref, g_ref,
                     rows_ref, buf, zbuf, xt, sem, zsem, *, tiles_a, nblk):
    i = pl.program_id(0)
    n = pl.num_programs(0)
    slot = i % 2

    def zero_block(start_row):
        return pltpu.make_async_copy(zbuf, _tile_span(rows_ref, start_row, MOE_TILE), zsem)

    @pl.when(i == 0)
    def _():
        zbuf[...] = jnp.zeros_like(zbuf)
        for e in range(N_EXPERTS):
            @pl.when(padded_ref[e] > 0)
            def _(e=e):
                zero_block(pend_ref[e] - MOE_TILE).start()

        def tail_start(j, c):
            zero_block(j * MOE_TILE).start()
            return c

        def tail_wait(j, c):
            zero_block(j * MOE_TILE).wait()
            return c

        lax.fori_loop(nu_ref[0], nblk, tail_start, 0)
        for e in range(N_EXPERTS):
            @pl.when(padded_ref[e] > 0)
            def _(e=e):
                zero_block(pend_ref[e] - MOE_TILE).wait()
        lax.fori_loop(nu_ref[0], nblk, tail_wait, 0)

    def slot_rows(s):
        return buf.at[pl.ds(pl.multiple_of(s * _SLOT_ROWS, _SLOT_ROWS), _SLOT_ROWS), :]

    def slot_done(s):
        return pltpu.make_async_copy(slot_rows(s), slot_rows(s), sem.at[s])

    @pl.when(i >= 2)
    def _():
        slot_done(slot).wait()

    def fill(x_ref):
        _to_row_tiles(xt, _rms(x_ref[...]) * g_ref[...])

        def body(t, c):
            v = xt[pl.ds(pl.multiple_of(t * SUBLANES, SUBLANES), SUBLANES), :]
            for k in range(TOP_K):
                off = pl.multiple_of(off_ref[0, k * ROUTE_TILE + t], SUBLANES)
                buf[pl.ds(off, SUBLANES), :] = v
            return c

        lax.fori_loop(0, ROUTE_TILE, body, 0, unroll=8)

    @pl.when(i < tiles_a)
    def _():
        fill(xa_ref)

    @pl.when(i >= tiles_a)
    def _():
        fill(xb_ref)

    _run_copies(i, src_tbl, cnt_tbl, loc_tbl,
                lambda g, l, size: pltpu.make_async_copy(_tile_span(slot_rows(slot), l, size),
                                                         _tile_span(rows_ref, g, size), sem.at[slot]))

    @pl.when(i == n - 1)
    def _():
        slot_done(1 - slot).wait()
        slot_done(slot).wait()


def _dispatch(runs, pend, padded, n_used, off, xa, xb, g, n_rows):
    tile_rows = ROUTE_TILE * SUBLANES
    tiles_a, tiles_b = xa.shape[0] // ROUTE_TILE, xb.shape[0] // ROUTE_TILE
    assert tiles_a + tiles_b >= 2
    grid_spec = pltpu.PrefetchScalarGridSpec(
        num_scalar_prefetch=6, grid=(tiles_a + tiles_b,),
        in_specs=[pl.BlockSpec((None, 1, TOP_K * ROUTE_TILE), lambda i, *_: (i, 0, 0), memory_space=pltpu.SMEM),
                  pl.BlockSpec((ROUTE_TILE, D_MODEL), lambda i, *_: (jnp.minimum(i, tiles_a - 1), 0)),
                  pl.BlockSpec((ROUTE_TILE, D_MODEL), lambda i, *_: (jnp.maximum(i - tiles_a, 0), 0)),
                  pl.BlockSpec((1, D_MODEL), lambda i, *_: (0, 0))],
        out_specs=pl.BlockSpec(memory_space=pl.ANY),
        scratch_shapes=[pltpu.VMEM((2 * _SLOT_ROWS, 128), F32),
                        pltpu.VMEM((MOE_TILE * SUBLANES, 128), F32), pltpu.VMEM((tile_rows, 128), F32),
                        pltpu.SemaphoreType.DMA((2,)), pltpu.SemaphoreType.DMA(())])
    return pl.pallas_call(
        functools.partial(_dispatch_kernel, tiles_a=tiles_a, nblk=n_rows // MOE_TILE), grid_spec=grid_spec,
        out_shape=jax.ShapeDtypeStruct((n_rows * SUBLANES, 128), F32),
        compiler_params=_cparams("arbitrary"), name="dispatch",
    )(*runs, pend, padded, n_used, off, xa, xb, g)


def _moe_kernel(be_ref, nu_ref, nv_ref, next_ref, slot_ref, x_ref, wgu_hbm, bg_ref, bl_ref, wd_hbm, bd_ref,
                perm_ref, o_ref, wgu_buf, wd_buf, wg_s, wl_s, wd_s, sem_gu, sem_d):
    i = pl.program_id(0)
    used = i < nu_ref[0]
    e = be_ref[i]
    new_expert = (i == 0) | (e != be_ref[jnp.maximum(i - 1, 0)])

    def fetch(expert, s):
        return (pltpu.make_async_copy(wgu_hbm.at[expert], wgu_buf.at[s], sem_gu.at[s]),
                pltpu.make_async_copy(wd_hbm.at[expert], wd_buf.at[s], sem_d.at[s]))

    @pl.when(used & new_expert)
    def _():
        s = slot_ref[e]

        @pl.when(i == 0)
        def _():
            for cp in fetch(e, s):
                cp.start()

        @pl.when(next_ref[e] >= 0)
        def _():
            for cp in fetch(next_ref[e], 1 - s):
                cp.start()

        for cp in fetch(e, s):
            cp.wait()
        for c in range(2 * D_FF // 256):
            r = _dot(wgu_buf[s, :, 256 * c:256 * (c + 1)].astype(BF16), perm_ref[...])
            wg_s[:, 128 * c:128 * (c + 1)] = r[:, :128].astype(BF16)
            wl_s[:, 128 * c:128 * (c + 1)] = r[:, 128:].astype(BF16)
        wd_s[...] = wd_buf[s].astype(BF16)

    def expert_rows(rows):
        x = _from_row_tiles(x_ref, rows).astype(BF16)
        glu = jnp.minimum(_dot(x, wg_s[...]) + bg_ref[...], SWIGLU_LIMIT)
        lin = jnp.clip(_dot(x, wl_s[...]) + bl_ref[...], -SWIGLU_LIMIT, SWIGLU_LIMIT)
        act = glu * jax.nn.sigmoid(SWIGLU_ALPHA * glu) * (lin + 1.0)
        _to_row_tiles(o_ref, _dot(act.astype(BF16), wd_s[...]) + bd_ref[...])

    def zero_rows(first, rows):
        o_ref[pl.ds(first * SUBLANES, rows * SUBLANES), :] = jnp.zeros((rows * SUBLANES, 128), F32)

    half = MOE_TILE // 2

    @pl.when(used & (nv_ref[i] > half))
    def _():
        expert_rows(MOE_TILE)

    @pl.when(used & (nv_ref[i] <= half))
    def _():
        expert_rows(half)
        zero_rows(half, half)

    @pl.when(jnp.logical_not(used))
    def _():
        zero_rows(0, MOE_TILE)


def _moe(block_e, n_used, n_valid, next_e, e_slot, x_rows, wgu, bg, bl, wd, bd, perm):
    nblk = x_rows.shape[0] // (MOE_TILE * SUBLANES)
    row = lambda i, be, nu, *_: (jnp.minimum(i, nu[0] - 1), 0)
    wsel = lambda i, be, *_: (be[i], 0, 0)
    grid_spec = pltpu.PrefetchScalarGridSpec(
        num_scalar_prefetch=5,
        grid=(nblk,),
        in_specs=[pl.BlockSpec((MOE_TILE * SUBLANES, 128), row),
                  pl.BlockSpec(memory_space=pl.ANY),
                  pl.BlockSpec((None, 1, D_FF), wsel), pl.BlockSpec((None, 1, D_FF), wsel),
                  pl.BlockSpec(memory_space=pl.ANY), pl.BlockSpec((None, 1, D_MODEL), wsel),
                  pl.BlockSpec((256, 256), lambda i, *_: (0, 0))],
        out_specs=pl.BlockSpec((MOE_TILE * SUBLANES, 128), lambda i, *_: (i, 0)),
        scratch_shapes=[pltpu.VMEM((2, D_MODEL, 2 * D_FF), F32), pltpu.VMEM((2, D_FF, D_MODEL), F32),
                        pltpu.VMEM((D_MODEL, D_FF), BF16), pltpu.VMEM((D_MODEL, D_FF), BF16),
                        pltpu.VMEM((D_FF, D_MODEL), BF16),
                        pltpu.SemaphoreType.DMA((2,)), pltpu.SemaphoreType.DMA((2,))],
    )
    return pl.pallas_call(
        _moe_kernel,
        grid_spec=grid_spec,
        out_shape=jax.ShapeDtypeStruct(x_rows.shape, F32),
        compiler_params=_cparams("arbitrary"),
        name="moe",
    )(block_e, n_used, n_valid, next_e, e_slot, x_rows, wgu, bg, bl, wd, bd, perm)


def _combine_kernel(src_tbl, cnt_tbl, loc_tbl, off_ref, gate_ref, rows_hbm, x1_ref, o_ref, buf, ybuf, sem, *,
                    tile_base):
    i = pl.program_id(0)
    n = pl.num_programs(0)

    def slot_rows(s):
        return buf.at[pl.ds(pl.multiple_of(s * _SLOT_ROWS, _SLOT_ROWS), _SLOT_ROWS), :]

    def fetch(tile, s):
        _run_copies(tile_base + tile, src_tbl, cnt_tbl, loc_tbl,
                    lambda g, l, size: pltpu.make_async_copy(_tile_span(rows_hbm, g, size),
                                                             _tile_span(slot_rows(s), l, size), sem.at[s]))

    @pl.when(i == 0)
    def _():
        fetch(0, 0)

    @pl.when(i + 1 < n)
    def _():
        fetch(i + 1, (i + 1) % 2)

    slot = i % 2
    pltpu.make_async_copy(slot_rows(slot), slot_rows(slot), sem.at[slot]).wait()

    def body(t, c):
        acc = None
        for k in range(TOP_K):
            off = pl.multiple_of(off_ref[0, k * ROUTE_TILE + t], SUBLANES)
            v = gate_ref[0, k * ROUTE_TILE + t] * buf[pl.ds(off, SUBLANES), :]
            acc = v if acc is None else acc + v
        ybuf[pl.ds(pl.multiple_of(t * SUBLANES, SUBLANES), SUBLANES), :] = acc
        return c

    lax.fori_loop(0, ROUTE_TILE, body, 0, unroll=8)
    o_ref[...] = x1_ref[...] + _from_row_tiles(ybuf, ROUTE_TILE)


def _combine(runs, off, gates, rows, x1, tile_base):
    assert tile_base % 2 == 0
    t = x1.shape[0]
    tile_rows = ROUTE_TILE * SUBLANES
    smem_tile = pl.BlockSpec((None, 1, TOP_K * ROUTE_TILE), lambda i, *_: (tile_base + i, 0, 0),
                             memory_space=pltpu.SMEM)
    grid_spec = pltpu.PrefetchScalarGridSpec(
        num_scalar_prefetch=3,
        grid=(t // ROUTE_TILE,),
        in_specs=[smem_tile, smem_tile, pl.BlockSpec(memory_space=pl.ANY),
                  pl.BlockSpec((ROUTE_TILE, D_MODEL), lambda i, *_: (i, 0))],
        out_specs=pl.BlockSpec((ROUTE_TILE, D_MODEL), lambda i, *_: (i, 0)),
        scratch_shapes=[pltpu.VMEM((2 * _SLOT_ROWS, 128), F32), pltpu.VMEM((tile_rows, 128), F32),
                        pltpu.SemaphoreType.DMA((2,))],
    )
    return pl.pallas_call(
        functools.partial(_combine_kernel, tile_base=tile_base),
        grid_spec=grid_spec,
        out_shape=jax.ShapeDtypeStruct((t, D_MODEL), F32),
        compiler_params=_cparams("arbitrary"),
        name="combine",
    )(*runs, off, gates, rows, x1)


def _columns_to_row(cols):
    lane = lax.broadcasted_iota(jnp.int32, (cols[0].shape[0], 128), 1)
    staged = jnp.zeros((cols[0].shape[0], 128), cols[0].dtype)
    for k, col in enumerate(cols):
        staged = jnp.where(lane == k, col, staged)
    rows = staged.T
    return jnp.concatenate([rows[k:k + 1] for k in range(len(cols))], axis=1)


def _route(before, cnt, loc, n_tokens):
    n_assign = n_tokens * TOP_K
    table = lambda a: a[:, 0, :N_EXPERTS].astype(jnp.int32)
    before, cnt, run_loc = table(before), table(cnt), table(loc)
    counts = before[-1] + cnt[-1]
    padded = ((counts + MOE_TILE - 1) // MOE_TILE) * MOE_TILE
    pend = jnp.cumsum(padded)
    pstart = pend - padded
    run_src = pstart[None, :] + before
    nblk = (n_assign + MOE_TILE - 1) // MOE_TILE + N_EXPERTS
    n_used = (pend[-1] // MOE_TILE).astype(jnp.int32)
    block_start = jnp.arange(nblk, dtype=jnp.int32) * MOE_TILE
    block_e = jnp.sum(pend[None, :] <= jnp.minimum(block_start, pend[-1] - 1)[:, None], axis=1)
    block_e = jnp.minimum(block_e, N_EXPERTS - 1).astype(jnp.int32)
    region_end = jnp.sum(jnp.where(block_e[:, None] == jnp.arange(N_EXPERTS)[None, :], (pstart + counts)[None, :], 0),
                         axis=1)
    n_valid = jnp.clip(region_end - block_start, 0, MOE_TILE).astype(jnp.int32)
    ids = jnp.arange(N_EXPERTS, dtype=jnp.int32)
    nonempty = counts > 0
    next_e = jnp.min(jnp.where((ids[None, :] > ids[:, None]) & nonempty[None, :], ids[None, :], N_EXPERTS), axis=1)
    next_e = jnp.where(next_e == N_EXPERTS, -1, next_e).astype(jnp.int32)
    e_slot = ((jnp.cumsum(nonempty.astype(jnp.int32)) - nonempty.astype(jnp.int32)) % 2).astype(jnp.int32)
    blocks = (block_e, n_used.reshape(1), n_valid, next_e, e_slot)
    runs = (run_src.reshape(-1), cnt.reshape(-1), run_loc.reshape(-1))
    return runs, blocks, pend, padded, nblk * MOE_TILE


def kernel(x_prompt, x_sample, cache_k, cache_v, state_ssm_re, state_ssm_im, norm_mix_g, w_in, q_norm_g, k_norm_g, attn_sinks, ssm_lambda_re, ssm_lambda_im, ssm_b_re, ssm_b_im, ssm_c_re, ssm_c_im, ssm_d, ssm_log_dt, w_glu, b_glu, attn_out_norm_g, ssm_out_norm_g, w_out, norm_ffn_g, w_router, b_router, w_gate_up, b_gate_up, w_down, b_down):
    depth = w_in.shape[0]
    assert depth == 1
    bp, sp, _ = x_prompt.shape
    bs, ss, _ = x_sample.shape
    tp, ts = bp * sp, bs * ss
    assert bp == SUBLANES and sp % S5_CHUNK == 0 and bs % SUBLANES == 0

    perm = jnp.asarray(_Q_PERM)
    w_in0 = w_in[0]
    wq = w_in0[:, :D_ATTN].reshape(D_MODEL, N_HEADS, HEAD_DIM)[:, perm].reshape(D_MODEL, D_ATTN)
    w_in_b = jnp.concatenate([wq, w_in0[:, D_ATTN:]], axis=1).astype(BF16)
    qkg = jnp.concatenate([jnp.tile(q_norm_g[0], N_HEADS), jnp.tile(k_norm_g[0], N_KV_HEADS)])[None]
    pmat = jnp.kron(jnp.eye(256 // HEAD_DIM, dtype=F32),
                    jnp.full((HEAD_DIM, HEAD_DIM), 1.0 / HEAD_DIM, F32)).astype(BF16)
    g_mix = norm_mix_g[0][None]
    sinks = attn_sinks[0]
    g_attn = attn_out_norm_g[0].reshape(N_HEADS, HEAD_DIM)[perm].reshape(1, D_ATTN)
    w_out0 = w_out[0]
    w_out_a = w_out0[:D_ATTN].reshape(N_HEADS, HEAD_DIM, D_MODEL)[perm].reshape(D_ATTN, D_MODEL).astype(BF16)
    w_out_s = w_out0[D_ATTN:].astype(BF16)
    g_ssm = ssm_out_norm_g[0][None]
    g_ffn = norm_ffn_g[0][None]
    w_r = jnp.pad(w_router[0], ((0, 0), (0, 128 - N_EXPERTS))).astype(BF16)
    b_r = jnp.pad(b_router[0], (0, 128 - N_EXPERTS), constant_values=NEG_INF)[None]

    a_re, a_im, bb_re, bb_im = _s5_prep(
        ssm_lambda_re[0], ssm_lambda_im[0], ssm_log_dt[0][:, None],
        jnp.swapaxes(ssm_b_re[0], 1, 2), jnp.swapaxes(ssm_b_im[0], 1, 2))
    chan_g = jnp.arange(D_SSM)[:, None] // SSM_GROUP
    state_g = jnp.arange(D_STATE)[None, :] // SSM_STATE
    bd_b = lambda bb: jnp.where(chan_g == state_g, jnp.tile(bb.reshape(D_SSM, SSM_STATE), (1, N_SSM_GROUPS)), 0.0)
    wb = jnp.concatenate([bd_b(bb_re), bd_b(bb_im)], axis=1).astype(BF16)
    bd_c = lambda c: jnp.where(state_g.T == chan_g.T,
                               jnp.tile(jnp.swapaxes(c, 1, 2).reshape(D_STATE, SSM_GROUP), (1, N_SSM_GROUPS)), 0.0)
    wc = jnp.concatenate([bd_c(ssm_c_re[0]), -bd_c(ssm_c_im[0])], axis=0).astype(BF16)
    a_re, a_im = a_re.reshape(1, D_STATE), a_im.reshape(1, D_STATE)
    d_skip = ssm_d[0].reshape(1, D_SSM)
    w_glu_b = w_glu[0].astype(BF16)
    b_glu0 = b_glu[0][None]

    b_g = b_gate_up[0][:, None, 0::2]
    b_l = b_gate_up[0][:, None, 1::2]
    b_d = b_down[0][:, None, :]
    idx = jnp.arange(256)
    deint = (idx[None, :] == jnp.where(idx % 2 == 0, idx // 2, 128 + idx // 2)[:, None]).astype(BF16)

    xp2 = x_prompt.reshape(tp, D_MODEL)
    xs2 = x_sample.reshape(ts, D_MODEL)
    qp, kp, vp, up = _in_proj(xp2, g_mix, w_in_b, qkg, pmat)
    qs, ks, vs, us = _in_proj(xs2, g_mix, w_in_b, qkg, pmat)

    kp3, vp3 = kp.reshape(bp, sp, D_KV), vp.reshape(bp, sp, D_KV)
    attn_p = _band_attention(sinks, qp.reshape(bp, sp, D_ATTN), kp3, vp3, g_attn).reshape(tp, D_ATTN)
    ck = cache_k[0].reshape(bs, -1, D_KV)
    cv = cache_v[0].reshape(bs, -1, D_KV)
    attn_s, new_ck, new_cv = _cache_attention(sinks, qs, ks, vs, ck, cv, g_attn, ss)

    zeros_p = jnp.zeros((bp, D_STATE), F32)
    s5_args = (a_re, a_im, wb, wc, d_skip, w_glu_b, b_glu0, g_ssm)
    ssm_p, hr_p, hi_p = _s5(up.reshape(bp, sp, D_SSM), zeros_p, zeros_p, *s5_args,
                            tt=S5_CHUNK, time_chunked=True)
    ssm_s, hr_s, hi_s = _s5(us.reshape(bs // SUBLANES, SUBLANES * ss, D_SSM),
                            state_ssm_re[0].reshape(bs, D_STATE), state_ssm_im[0].reshape(bs, D_STATE),
                            *s5_args, tt=ss, time_chunked=False)

    tri = (jnp.arange(ROUTE_TILE)[:, None] > jnp.arange(ROUTE_TILE)[None, :]).astype(BF16)
    upper = (jnp.arange(128)[:, None] < jnp.arange(128)[None, :]).astype(BF16)
    proj_args = (w_out_a, w_out_s, g_ffn, w_r, b_r, tri, upper)
    x1p, *tab_p, counts_p = _out_proj(attn_p, ssm_p.reshape(tp, D_SSM), xp2, *proj_args,
                                      jnp.zeros((1, 128), F32), 0)
    x1s, *tab_s, _ = _out_proj(attn_s, ssm_s.reshape(ts, D_SSM), xs2, *proj_args, counts_p, tp // ROUTE_TILE)

    gates, off, before, cnt, loc = (jnp.concatenate([p, s_], axis=0) for p, s_ in zip(tab_p, tab_s))
    runs, blocks, pend, padded, n_rows = _route(before, cnt, loc, tp + ts)
    x_rows = _dispatch(runs, pend, padded, blocks[1], off, x1p, x1s, g_ffn, n_rows)
    out_rows = _moe(*blocks, x_rows, w_gate_up[0], b_g, b_l, w_down[0], b_d, deint)
    yp = _combine(runs, off, gates, out_rows, x1p, 0).reshape(bp, sp, D_MODEL)
    ys = _combine(runs, off, gates, out_rows, x1s, tp // ROUTE_TILE).reshape(bs, ss, D_MODEL)

    kv5 = lambda a, b_: a.reshape(b_, -1, N_KV_HEADS, HEAD_DIM)
    new_kp = kv5(kp3[:, -WINDOW:], bp)[None]
    new_vp = kv5(vp3[:, -WINDOW:], bp)[None]
    new_ks = kv5(new_ck, bs)[None]
    new_vs = kv5(new_cv, bs)[None]
    st = lambda h, b_: h.reshape(1, b_, N_SSM_GROUPS, SSM_STATE)
    return (yp, ys, new_kp, new_vp, st(hr_p, bp), st(hi_p, bp),
            new_ks, new_vs, st(hr_s, bs), st(hi_s, bs))
```

```python
import functools
import math

import jax
import jax.numpy as jnp
from jax import lax
from jax.experimental import pallas as pl
from jax.experimental.pallas import tpu as pltpu

F32 = jnp.float32
BF16 = jnp.bfloat16

D_MODEL = 1024
D_ATTN = 512
D_SSM = 512
HEAD_DIM = 64
N_HEADS = 8
N_KV_HEADS = 2
D_KV = N_KV_HEADS * HEAD_DIM
WINDOW = 128
SSM_GROUP = 16
N_SSM_GROUPS = 32
SSM_STATE = 64
D_STATE = N_SSM_GROUPS * SSM_STATE
N_EXPERTS = 32
TOP_K = 4
D_FF = 1024
SWIGLU_LIMIT = 7.0
SWIGLU_ALPHA = 1.702
RMS_EPS = 1e-6
NEG_INF = -1e30
D_IN_PROJ = D_ATTN + 2 * D_KV + D_SSM
D_QK = D_ATTN + D_KV

SUBLANES = 8
VMEM_LIMIT = 56 * 1024 * 1024

ROW_TILE = 1024
MOE_TILE = 512
ATTN_BLOCKS = 4
ROUTE_TILE = 512
_SLOT_ROWS = ROUTE_TILE * TOP_K * SUBLANES
_RUN_SIZES = tuple(1 << b for b in range(ROUTE_TILE.bit_length() - 1, -1, -1))
S5_CHUNK = 128
S5_COLS = 512

_Q_PERM = (0, 4, 1, 5, 2, 6, 3, 7)


def _cparams(*sem):
    return pltpu.CompilerParams(dimension_semantics=sem, vmem_limit_bytes=VMEM_LIMIT)


def _dot(a, b):
    return jnp.dot(a, b, preferred_element_type=F32)


def _rms(x):
    return x * lax.rsqrt(jnp.mean(x * x, axis=-1, keepdims=True) + RMS_EPS)


def _in_proj_kernel(x_ref, g_ref, w_ref, qkg_ref, p_ref, q_ref, k_ref, v_ref, u_ref):
    xn = _rms(x_ref[...]) * g_ref[...]
    h = _dot(xn.astype(BF16), w_ref[...])
    qk = h[:, :D_QK]
    sq = (qk * qk).astype(BF16)
    p = p_ref[...]
    ms = jnp.concatenate(
        [_dot(sq[:, 0:256], p), _dot(sq[:, 256:512], p), _dot(sq[:, 512:640], p[:128, :128])],
        axis=-1)
    qkn = qk * lax.rsqrt(ms + RMS_EPS) * qkg_ref[...]
    q_ref[...] = qkn[:, :D_ATTN].astype(BF16)
    k_ref[...] = qkn[:, D_ATTN:]
    v_ref[...] = h[:, D_QK:D_QK + D_KV]
    u_ref[...] = h[:, D_QK + D_KV:]


def _in_proj(x2d, g, w, qkg, pmat):
    t = x2d.shape[0]
    tm = min(ROW_TILE, t)
    row = lambda i: (i, 0)
    fix = lambda i: (0, 0)
    return pl.pallas_call(
        _in_proj_kernel,
        grid=(t // tm,),
        in_specs=[pl.BlockSpec((tm, D_MODEL), row), pl.BlockSpec((1, D_MODEL), fix),
                  pl.BlockSpec((D_MODEL, D_IN_PROJ), fix), pl.BlockSpec((1, D_QK), fix),
                  pl.BlockSpec((256, 256), fix)],
        out_specs=[pl.BlockSpec((tm, D_ATTN), row), pl.BlockSpec((tm, D_KV), row),
                   pl.BlockSpec((tm, D_KV), row), pl.BlockSpec((tm, D_SSM), row)],
        out_shape=[jax.ShapeDtypeStruct((t, D_ATTN), BF16), jax.ShapeDtypeStruct((t, D_KV), F32),
                   jax.ShapeDtypeStruct((t, D_KV), F32), jax.ShapeDtypeStruct((t, D_SSM), F32)],
        compiler_params=_cparams("parallel"),
        name="in_proj",
    )(x2d, g, w, qkg, pmat)


def _softmax_pv(s_blocks, v_blocks, sink):
    m = sink
    for s in s_blocks:
        m = jnp.maximum(m, jnp.max(s, axis=-1, keepdims=True))
    den = jnp.exp(sink - m)
    acc = None
    for s, v in zip(s_blocks, v_blocks):
        p = jnp.exp(s - m)
        den = den + jnp.sum(p, axis=-1, keepdims=True)
        pv = _dot(p.astype(BF16), v)
        acc = pv if acc is None else acc + pv
    return acc / den


def _band_attn_kernel(sink_ref, q_ref, kp_ref, kc_ref, vp_ref, vc_ref, g_ref, o_ref):
    i = pl.program_id(1)
    row = lax.broadcasted_iota(jnp.int32, (WINDOW, 2 * WINDOW), 0)
    col = lax.broadcasted_iota(jnp.int32, (WINDOW, 2 * WINDOW), 1)
    band = (col > row) & (col <= row + WINDOW)
    lane = lax.broadcasted_iota(jnp.int32, (WINDOW, 128), 1)
    low = lane < HEAD_DIM
    zero = jnp.zeros((), BF16)
    for j in range(ATTN_BLOCKS):
        cur = slice(WINDOW * j, WINDOW * (j + 1))
        q = q_ref[cur, :]
        k_prev = kp_ref[...] if j == 0 else kc_ref[WINDOW * (j - 1):WINDOW * j, :]
        v_prev = vp_ref[...] if j == 0 else vc_ref[WINDOW * (j - 1):WINDOW * j, :]
        kb = jnp.concatenate([k_prev, kc_ref[cur, :]], axis=0).astype(BF16)
        vb = jnp.concatenate([v_prev, vc_ref[cur, :]], axis=0).astype(BF16)
        mask = band & ((col >= WINDOW) | (i > 0)) if j == 0 else band
        outs = []
        for pair in range(N_HEADS // 2):
            qp = q[:, 128 * pair:128 * (pair + 1)]
            halves = []
            for par in range(2):
                qm = jnp.where(low if par == 0 else ~low, qp, zero)
                s = lax.dot_general(qm, kb, (((1,), (1,)), ((), ())), preferred_element_type=F32)
                s = jnp.where(mask, s * (HEAD_DIM ** -0.5), NEG_INF)
                halves.append(_softmax_pv([s], [vb], sink_ref[_Q_PERM[2 * pair + par]]))
            outs.append(jnp.where(low, halves[0], halves[1]))
        o = jnp.concatenate(outs, axis=-1)
        o_ref[cur, :] = (_rms(o) * g_ref[...]).astype(BF16)


def _band_attention(sinks, q, k, v, g):
    b, s, _ = q.shape
    qb = WINDOW * ATTN_BLOCKS
    cur = lambda bi, i: (bi, i, 0)
    prev = lambda bi, i: (bi, jnp.maximum(i * ATTN_BLOCKS - 1, 0), 0)
    return pl.pallas_call(
        _band_attn_kernel,
        grid=(b, s // qb),
        in_specs=[pl.BlockSpec(memory_space=pltpu.SMEM),
                  pl.BlockSpec((None, qb, D_ATTN), cur),
                  pl.BlockSpec((None, WINDOW, D_KV), prev), pl.BlockSpec((None, qb, D_KV), cur),
                  pl.BlockSpec((None, WINDOW, D_KV), prev), pl.BlockSpec((None, qb, D_KV), cur),
                  pl.BlockSpec((1, D_ATTN), lambda bi, i: (0, 0))],
        out_specs=pl.BlockSpec((None, qb, D_ATTN), cur),
        out_shape=jax.ShapeDtypeStruct((b, s, D_ATTN), BF16),
        compiler_params=_cparams("parallel", "parallel"),
        name="band_attn",
    )(sinks, q, k, k, v, v, g)


_PAIR_ROWS = 8
_CACHE_BB = 16


def _cache_attn_kernel(sink_ref, q_ref, kn_ref, vn_ref, ck_ref, cv_ref, g_ref, o_ref, nk_ref, nv_ref, *, n_new):
    n_buf = ck_ref.shape[1]
    rows_blk = _CACHE_BB * n_new
    for b in range(_CACHE_BB):
        for new_ref, cache_ref, fresh_ref in ((nk_ref, ck_ref, kn_ref), (nv_ref, cv_ref, vn_ref)):
            new_ref[b, 0:n_buf - n_new, :] = cache_ref[b, n_new:n_buf, :]
            new_ref[b, n_buf - n_new:n_buf, :] = fresh_ref[n_new * b:n_new * (b + 1), :]
    knew = kn_ref[...].astype(BF16)
    vnew = vn_ref[...].astype(BF16)
    lane = lax.broadcasted_iota(jnp.int32, (_PAIR_ROWS, 128), 1)
    low = lane < HEAD_DIM
    zero = jnp.zeros((), BF16)
    n_stack = N_HEADS * _PAIR_ROWS
    r = lax.broadcasted_iota(jnp.int32, (n_stack, 1), 0) % _PAIR_ROWS
    r_seq, r_tok = r // n_new, r % n_new
    colc = lax.broadcasted_iota(jnp.int32, (n_stack, 2 * n_buf), 1)
    c_seq, c_pos = colc // n_buf, colc % n_buf
    mask_c = (c_seq == r_seq) & (c_pos + WINDOW > r_tok + n_buf)
    coln = lax.broadcasted_iota(jnp.int32, (n_stack, rows_blk), 1)
    sink_col = jnp.concatenate(
        [jnp.full((_PAIR_ROWS, 1), sink_ref[_Q_PERM[h]], F32) for h in range(N_HEADS)], axis=0)
    for sp in range(_CACHE_BB // 2):
        q = q_ref[_PAIR_ROWS * sp:_PAIR_ROWS * (sp + 1), :]
        pieces = []
        for pair in range(N_HEADS // 2):
            qp = q[:, 128 * pair:128 * (pair + 1)]
            pieces.append(jnp.where(low, qp, zero))
            pieces.append(jnp.where(low, zero, qp))
        qs = jnp.concatenate(pieces, axis=0)
        kc = jnp.concatenate([ck_ref[2 * sp], ck_ref[2 * sp + 1]], axis=0).astype(BF16)
        vc = jnp.concatenate([cv_ref[2 * sp], cv_ref[2 * sp + 1]], axis=0).astype(BF16)
        nt = (((1,), (1,)), ((), ()))
        s_c = lax.dot_general(qs, kc, nt, preferred_element_type=F32) * (HEAD_DIM ** -0.5)
        s_n = lax.dot_general(qs, knew, nt, preferred_element_type=F32) * (HEAD_DIM ** -0.5)
        s_c = jnp.where(mask_c, s_c, NEG_INF)
        n_seq, n_tok = coln // n_new - 2 * sp, coln % n_new
        mask_n = (n_seq == r_seq) & (n_tok <= r_tok)
        s_n = jnp.where(mask_n, s_n, NEG_INF)
        o = _softmax_pv([s_c, s_n], [vc, vnew], sink_col)
        outs = [jnp.where(low, o[16 * pair:16 * pair + 8], o[16 * pair + 8:16 * pair + 16])
                for pair in range(N_HEADS // 2)]
        oo = jnp.concatenate(outs, axis=-1)
        o_ref[_PAIR_ROWS * sp:_PAIR_ROWS * (sp + 1), :] = (_rms(oo) * g_ref[...]).astype(BF16)


def _cache_attention(sinks, q, k, v, cache_k, cache_v, g, n_new):
    t = q.shape[0]
    nb, n_buf, _ = cache_k.shape
    assert n_new * 2 == _PAIR_ROWS and nb % _CACHE_BB == 0 and n_buf == WINDOW
    rows = _CACHE_BB * n_new
    row = lambda i: (i, 0)
    cache_spec = pl.BlockSpec((_CACHE_BB, n_buf, D_KV), lambda i: (i, 0, 0))
    return pl.pallas_call(
        functools.partial(_cache_attn_kernel, n_new=n_new),
        grid=(nb // _CACHE_BB,),
        in_specs=[pl.BlockSpec(memory_space=pltpu.SMEM),
                  pl.BlockSpec((rows, D_ATTN), row), pl.BlockSpec((rows, D_KV), row),
                  pl.BlockSpec((rows, D_KV), row), cache_spec, cache_spec,
                  pl.BlockSpec((1, D_ATTN), lambda i: (0, 0))],
        out_specs=[pl.BlockSpec((rows, D_ATTN), row), cache_spec, cache_spec],
        out_shape=[jax.ShapeDtypeStruct((t, D_ATTN), BF16), jax.ShapeDtypeStruct(cache_k.shape, F32),
                   jax.ShapeDtypeStruct(cache_v.shape, F32)],
        compiler_params=_cparams("parallel"),
        name="cache_attn",
    )(sinks, q, k, v, cache_k, cache_v, g)


def _s5_prep_kernel(lre_ref, lim_ref, ldt_ref, bre_ref, bim_ref, are_ref, aim_ref, bbre_ref, bbim_ref):
    dt = jnp.exp(ldt_ref[...])
    l_re = jnp.minimum(lre_ref[...], -1e-4)
    l_im = lim_ref[...]
    mag = jnp.exp(l_re * dt)
    a_re = mag * jnp.cos(l_im * dt)
    a_im = mag * jnp.sin(l_im * dt)
    den = l_re * l_re + l_im * l_im
    n_re = a_re - 1.0
    z_re = (n_re * l_re + a_im * l_im) / den
    z_im = (a_im * l_re - n_re * l_im) / den
    are_ref[...] = a_re
    aim_ref[...] = a_im
    br, bi = bre_ref[...], bim_ref[...]
    zr, zi = z_re[:, None, :], z_im[:, None, :]
    bbre_ref[...] = zr * br - zi * bi
    bbim_ref[...] = zr * bi + zi * br


def _s5_prep(lam_re, lam_im, log_dt, b_re_t, b_im_t):
    g, p = lam_re.shape
    sd = jax.ShapeDtypeStruct
    return pl.pallas_call(
        _s5_prep_kernel,
        out_shape=[sd((g, p), F32), sd((g, p), F32), sd(b_re_t.shape, F32), sd(b_re_t.shape, F32)],
        name="s5_prep",
    )(lam_re, lam_im, log_dt, b_re_t, b_im_t)


def _s5_kernel(u_ref, h0r_ref, h0i_ref, ar_ref, ai_ref, wb_ref, wc_ref, d_ref, wglu_ref, bglu_ref,
               g_ref, o_ref, hr_ref, hi_ref, bu_ref, hs_ref, usc_ref, ysc_ref, *, tt):
    j = pl.program_id(1)
    rows = SUBLANES * tt
    n_tiles = D_STATE // 128
    time_major = tt % SUBLANES == 0

    @pl.when(j == 0)
    def _():
        hs_ref[:, :D_STATE] = h0r_ref[...]
        hs_ref[:, D_STATE:] = h0i_ref[...]

    u = u_ref[...].reshape(rows, D_SSM)
    if time_major:
        for c in range(D_SSM // 128):
            for b in range(SUBLANES):
                usc_ref[c, pl.ds(b, tt, stride=SUBLANES), :] = u[b * tt:(b + 1) * tt, 128 * c:128 * (c + 1)]
        ub = jnp.concatenate([usc_ref[c] for c in range(D_SSM // 128)], axis=-1).astype(BF16)
    else:
        ub = u.astype(BF16)

    def step_rows(t):
        if time_major:
            return pl.ds(pl.multiple_of(t * SUBLANES, SUBLANES), SUBLANES)
        return pl.ds(t, SUBLANES, stride=tt)

    for n in range(2 * D_STATE // 256):
        band = (n % (D_STATE // 256)) // 2
        res = _dot(ub[:, 128 * band:128 * (band + 1)],
                   wb_ref[128 * band:128 * (band + 1), 256 * n:256 * (n + 1)])
        bu_ref[2 * n] = res[:, :128]
        bu_ref[2 * n + 1] = res[:, 128:]

    tiles_per_pass = S5_COLS // 128
    for c0 in range(0, n_tiles, tiles_per_pass):
        tiles = range(c0, c0 + tiles_per_pass)
        a_r = [jnp.broadcast_to(ar_ref[:, 128 * c:128 * (c + 1)], (SUBLANES, 128)) for c in tiles]
        a_i = [jnp.broadcast_to(ai_ref[:, 128 * c:128 * (c + 1)], (SUBLANES, 128)) for c in tiles]

        def step(t, carry, tiles=tiles, a_r=a_r, a_i=a_i):
            at_t = step_rows(t)
            out = []
            for k, c in enumerate(tiles):
                h_r, h_i = carry[2 * k], carry[2 * k + 1]
                n_r = a_r[k] * h_r - a_i[k] * h_i + bu_ref[c, at_t, :]
                n_i = a_r[k] * h_i + a_i[k] * h_r + bu_ref[n_tiles + c, at_t, :]
                bu_ref[c, at_t, :] = n_r
                bu_ref[n_tiles + c, at_t, :] = n_i
                out += [n_r, n_i]
            return tuple(out)

        init = []
        for c in tiles:
            init += [hs_ref[:, 128 * c:128 * (c + 1)], hs_ref[:, D_STATE + 128 * c:D_STATE + 128 * (c + 1)]]
        fin = lax.fori_loop(0, tt, step, tuple(init), unroll=min(tt, 8))
        for k, c in enumerate(tiles):
            hs_ref[:, 128 * c:128 * (c + 1)] = fin[2 * k]
            hs_ref[:, D_STATE + 128 * c:D_STATE + 128 * (c + 1)] = fin[2 * k + 1]

    def h_cols(first_tile):
        return jnp.concatenate([bu_ref[first_tile + k] for k in range(4)], axis=-1).astype(BF16)

    ys = []
    for m in range(D_SSM // 128):
        y = _dot(h_cols(4 * m), wc_ref[512 * m:512 * (m + 1), 128 * m:128 * (m + 1)])
        y = y + _dot(h_cols(n_tiles + 4 * m),
                     wc_ref[D_STATE + 512 * m:D_STATE + 512 * (m + 1), 128 * m:128 * (m + 1)])
        if time_major:
            ysc_ref[m] = y
            y = jnp.concatenate([ysc_ref[m, pl.ds(b, tt, stride=SUBLANES), :] for b in range(SUBLANES)], axis=0)
        ys.append(y)
    y = jnp.concatenate(ys, axis=-1) + d_ref[...] * u
    z = _dot(jax.nn.gelu(y).astype(BF16), wglu_ref[...]) + bglu_ref[...]
    s = z[:, :D_SSM] * jax.nn.sigmoid(z[:, D_SSM:])
    o_ref[...] = (_rms(s) * g_ref[...]).astype(BF16).reshape(o_ref.shape)

    @pl.when(j == pl.num_programs(1) - 1)
    def _():
        hr_ref[...] = hs_ref[:, :D_STATE]
        hi_ref[...] = hs_ref[:, D_STATE:]


def _s5(u, h0r, h0i, a_re, a_im, wb, wc, d, wglu, bglu, g, *, tt, time_chunked):
    nbg = h0r.shape[0] // SUBLANES
    if time_chunked:
        nchunks = u.shape[1] // tt
        u_spec = pl.BlockSpec((SUBLANES, tt, D_SSM), lambda gi, j: (gi, j, 0))
    else:
        nchunks = 1
        u_spec = pl.BlockSpec((None, SUBLANES * tt, D_SSM), lambda gi, j: (gi, 0, 0))
    fix = lambda gi, j: (0, 0)
    st_spec = pl.BlockSpec((SUBLANES, D_STATE), lambda gi, j: (gi, 0))
    sd = jax.ShapeDtypeStruct
    return pl.pallas_call(
        functools.partial(_s5_kernel, tt=tt),
        grid=(nbg, nchunks),
        in_specs=[u_spec, st_spec, st_spec,
                  pl.BlockSpec((1, D_STATE), fix), pl.BlockSpec((1, D_STATE), fix),
                  pl.BlockSpec((D_SSM, 2 * D_STATE), fix), pl.BlockSpec((2 * D_STATE, D_SSM), fix),
                  pl.BlockSpec((1, D_SSM), fix), pl.BlockSpec((D_SSM, 2 * D_SSM), fix),
                  pl.BlockSpec((1, 2 * D_SSM), fix), pl.BlockSpec((1, D_SSM), fix)],
        out_specs=[u_spec, st_spec, st_spec],
        out_shape=[sd(u.shape, BF16), sd(h0r.shape, F32), sd(h0r.shape, F32)],
        scratch_shapes=[pltpu.VMEM((2 * D_STATE // 128, SUBLANES * tt, 128), F32),
                        pltpu.VMEM((SUBLANES, 2 * D_STATE), F32),
                        pltpu.VMEM((D_SSM // 128, SUBLANES * tt, 128), F32),
                        pltpu.VMEM((D_SSM // 128, SUBLANES * tt, 128), F32)],
        compiler_params=_cparams("parallel", "arbitrary"),
        name="s5",
    )(u, h0r, h0i, a_re, a_im, wb, wc, d, wglu, bglu, g)


def _to_row_tiles(ref, val, first=0):
    rows = val.shape[0]
    for c in range(D_MODEL // 128):
        ref[pl.ds(first * SUBLANES + c, rows, stride=SUBLANES), :] = val[:, 128 * c:128 * (c + 1)]


def _from_row_tiles(ref, rows, lead=(), first=0):
    return jnp.concatenate(
        [ref[(*lead, pl.ds(first * SUBLANES + c, rows, stride=SUBLANES), slice(None))]
         for c in range(D_MODEL // 128)], axis=-1)


def _route_tile(logits, tile, tri_ref, upper_ref, carry, gate_ref, off_ref, before_ref, cnt_ref, loc_ref):
    l = logits
    lane = lax.broadcasted_iota(jnp.int32, l.shape, 1)
    vals, sels = [], []
    for _ in range(TOP_K):
        m = jnp.max(l, axis=-1, keepdims=True)
        idx = jnp.min(jnp.where(l == m, lane, l.shape[1]), axis=-1, keepdims=True)
        sel = lane == idx
        l = jnp.where(sel, -jnp.inf, l)
        vals.append(m)
        sels.append(sel)
    exps = [jnp.exp(v - vals[0]) for v in vals]
    den = exps[0] + exps[1] + exps[2] + exps[3]
    onehot = jnp.where(sels[0] | sels[1] | sels[2] | sels[3], 1.0, 0.0)
    within = _dot(tri_ref[...], onehot.astype(BF16))
    cnt = jnp.sum(onehot, axis=0, keepdims=True)
    cnt_hi = jnp.floor(cnt * (1.0 / 16.0))
    cnt_lo = cnt - 16.0 * cnt_hi
    loc = 16.0 * _dot(cnt_hi.astype(BF16), upper_ref[...]) + _dot(cnt_lo.astype(BF16), upper_ref[...])
    slot_base = ((tile % 2) * (ROUTE_TILE * TOP_K)).astype(F32)
    rows = [jnp.sum(jnp.where(s, within + loc, 0.0), axis=-1, keepdims=True) for s in sels]
    gate_ref[...] = _columns_to_row([e / den for e in exps])
    off_ref[...] = ((_columns_to_row(rows) + slot_base) * SUBLANES).astype(jnp.int32)
    before_ref[...] = carry[...]
    cnt_ref[...] = cnt
    loc_ref[...] = loc
    carry[...] = carry[...] + cnt


def _out_proj_kernel(a_ref, s_ref, x_ref, wa_ref, ws_ref, g_ref, wr_ref, br_ref, tri_ref, upper_ref, cin_ref,
                     x1_ref, gate_ref, off_ref, before_ref, cnt_ref, loc_ref, cout_ref, carry, *, tile_base):
    i = pl.program_id(0)

    @pl.when(i == 0)
    def _():
        carry[...] = cin_ref[...]

    x1 = x_ref[...] + _dot(a_ref[...], wa_ref[...]) + _dot(s_ref[...], ws_ref[...])
    x1_ref[...] = x1
    xn = _rms(x1) * g_ref[...]
    logits = _dot(xn.astype(BF16), wr_ref[...]) + br_ref[...]
    _route_tile(logits, tile_base + i, tri_ref, upper_ref, carry, gate_ref, off_ref, before_ref, cnt_ref, loc_ref)
    cout_ref[...] = carry[...]


def _out_proj(attn_n, ssm_n, x2d, wa, ws, g, wr, br, tri, upper, counts_in, tile_base):
    t = x2d.shape[0]
    tm = ROUTE_TILE
    n_tiles = t // tm
    row = lambda i: (i, 0)
    fix = lambda i: (0, 0)
    sd = jax.ShapeDtypeStruct
    k_spec = pl.BlockSpec((None, 1, TOP_K * tm), lambda i: (i, 0, 0))
    t_spec = pl.BlockSpec((None, 1, 128), lambda i: (i, 0, 0))
    return pl.pallas_call(
        functools.partial(_out_proj_kernel, tile_base=tile_base),
        grid=(n_tiles,),
        in_specs=[pl.BlockSpec((tm, D_ATTN), row), pl.BlockSpec((tm, D_SSM), row),
                  pl.BlockSpec((tm, D_MODEL), row),
                  pl.BlockSpec((D_ATTN, D_MODEL), fix), pl.BlockSpec((D_SSM, D_MODEL), fix),
                  pl.BlockSpec((1, D_MODEL), fix), pl.BlockSpec((D_MODEL, 128), fix),
                  pl.BlockSpec((1, 128), fix), pl.BlockSpec((tm, tm), fix), pl.BlockSpec((128, 128), fix),
                  pl.BlockSpec((1, 128), fix)],
        out_specs=[pl.BlockSpec((tm, D_MODEL), row), k_spec, k_spec, t_spec, t_spec, t_spec,
                   pl.BlockSpec((1, 128), fix)],
        out_shape=[sd((t, D_MODEL), F32), sd((n_tiles, 1, TOP_K * tm), F32), sd((n_tiles, 1, TOP_K * tm), jnp.int32),
                   sd((n_tiles, 1, 128), F32), sd((n_tiles, 1, 128), F32), sd((n_tiles, 1, 128), F32),
                   sd((1, 128), F32)],
        scratch_shapes=[pltpu.VMEM((1, 128), F32)],
        compiler_params=_cparams("arbitrary"),
        name="out_proj",
    )(attn_n, ssm_n, x2d, wa, ws, g, wr, br, tri, upper, counts_in)


def _tile_span(ref, first_row, n_rows, lead=()):
    start = pl.multiple_of(first_row * SUBLANES, SUBLANES)
    return ref.at[(*lead, pl.ds(start, n_rows * SUBLANES), slice(None))]


def _run_copies(tile, src_tbl, cnt_tbl, loc_tbl, make_copy):
    def body(e, c):
        j = tile * N_EXPERTS + e
        cnt, src, loc = cnt_tbl[j], src_tbl[j], loc_tbl[j]
        off = 0
        for size in _RUN_SIZES:
            @pl.when((cnt & size) != 0)
            def _(off=off, size=size):
                make_copy(src + off, loc + off, size).start()
            off = off + (cnt & size)
        return c

    lax.fori_loop(0, N_EXPERTS, body, 0)


def _dispatch_kernel(src_tbl, cnt_tbl, loc_tbl, pend_ref, padded_ref, nu_ref, off_ref, xa_ref, xb_ref, g_ref,
                     rows_ref, buf, zbuf, xt, sem, zsem, *, tiles_a, nblk):
    i = pl.program_id(0)
    n = pl.num_programs(0)
    slot = i % 2

    def zero_block(start_row):
        return pltpu.make_async_copy(zbuf, _tile_span(rows_ref, start_row, MOE_TILE), zsem)

    @pl.when(i == 0)
    def _():
        zbuf[...] = jnp.zeros_like(zbuf)
        for e in range(N_EXPERTS):
            @pl.when(padded_ref[e] > 0)
            def _(e=e):
                zero_block(pend_ref[e] - MOE_TILE).start()

        def tail_start(j, c):
            zero_block(j * MOE_TILE).start()
            return c

        def tail_wait(j, c):
            zero_block(j * MOE_TILE).wait()
            return c

        lax.fori_loop(nu_ref[0], nblk, tail_start, 0)
        for e in range(N_EXPERTS):
            @pl.when(padded_ref[e] > 0)
            def _(e=e):
                zero_block(pend_ref[e] - MOE_TILE).wait()
        lax.fori_loop(nu_ref[0], nblk, tail_wait, 0)

    def slot_rows(s):
        return buf.at[pl.ds(pl.multiple_of(s * _SLOT_ROWS, _SLOT_ROWS), _SLOT_ROWS), :]

    def slot_done(s):
        return pltpu.make_async_copy(slot_rows(s), slot_rows(s), sem.at[s])

    @pl.when(i >= 2)
    def _():
        slot_done(slot).wait()

    def fill(x_ref):
        _to_row_tiles(xt, _rms(x_ref[...]) * g_ref[...])

        def body(t, c):
            v = xt[pl.ds(pl.multiple_of(t * SUBLANES, SUBLANES), SUBLANES), :]
            for k in range(TOP_K):
                off = pl.multiple_of(off_ref[0, k * ROUTE_TILE + t], SUBLANES)
                buf[pl.ds(off, SUBLANES), :] = v
            return c

        lax.fori_loop(0, ROUTE_TILE, body, 0, unroll=8)

    @pl.when(i < tiles_a)
    def _():
        fill(xa_ref)

    @pl.when(i >= tiles_a)
    def _():
        fill(xb_ref)

    _run_copies(i, src_tbl, cnt_tbl, loc_tbl,
                lambda g, l, size: pltpu.make_async_copy(_tile_span(slot_rows(slot), l, size),
                                                         _tile_span(rows_ref, g, size), sem.at[slot]))

    @pl.when(i == n - 1)
    def _():
        slot_done(1 - slot).wait()
        slot_done(slot).wait()


def _dispatch(runs, pend, padded, n_used, off, xa, xb, g, n_rows):
    tile_rows = ROUTE_TILE * SUBLANES
    tiles_a, tiles_b = xa.shape[0] // ROUTE_TILE, xb.shape[0] // ROUTE_TILE
    assert tiles_a + tiles_b >= 2
    grid_spec = pltpu.PrefetchScalarGridSpec(
        num_scalar_prefetch=6, grid=(tiles_a + tiles_b,),
        in_specs=[pl.BlockSpec((None, 1, TOP_K * ROUTE_TILE), lambda i, *_: (i, 0, 0), memory_space=pltpu.SMEM),
                  pl.BlockSpec((ROUTE_TILE, D_MODEL), lambda i, *_: (jnp.minimum(i, tiles_a - 1), 0)),
                  pl.BlockSpec((ROUTE_TILE, D_MODEL), lambda i, *_: (jnp.maximum(i - tiles_a, 0), 0)),
                  pl.BlockSpec((1, D_MODEL), lambda i, *_: (0, 0))],
        out_specs=pl.BlockSpec(memory_space=pl.ANY),
        scratch_shapes=[pltpu.VMEM((2 * _SLOT_ROWS, 128), F32),
                        pltpu.VMEM((MOE_TILE * SUBLANES, 128), F32), pltpu.VMEM((tile_rows, 128), F32),
                        pltpu.SemaphoreType.DMA((2,)), pltpu.SemaphoreType.DMA(())])
    return pl.pallas_call(
        functools.partial(_dispatch_kernel, tiles_a=tiles_a, nblk=n_rows // MOE_TILE), grid_spec=grid_spec,
        out_shape=jax.ShapeDtypeStruct((n_rows * SUBLANES, 128), F32),
        compiler_params=_cparams("arbitrary"), name="dispatch",
    )(*runs, pend, padded, n_used, off, xa, xb, g)


def _moe_kernel(be_ref, nu_ref, nv_ref, next_ref, slot_ref, x_ref, wgu_hbm, bg_ref, bl_ref, wd_hbm, bd_ref,
                perm_ref, o_ref, wgu_buf, wd_buf, wg_s, wl_s, wd_s, sem_gu, sem_d):
    i = pl.program_id(0)
    used = i < nu_ref[0]
    e = be_ref[i]
    new_expert = (i == 0) | (e != be_ref[jnp.maximum(i - 1, 0)])

    def fetch(expert, s):
        return (pltpu.make_async_copy(wgu_hbm.at[expert], wgu_buf.at[s], sem_gu.at[s]),
                pltpu.make_async_copy(wd_hbm.at[expert], wd_buf.at[s], sem_d.at[s]))

    @pl.when(used & new_expert)
    def _():
        s = slot_ref[e]

        @pl.when(i == 0)
        def _():
            for cp in fetch(e, s):
                cp.start()

        @pl.when(next_ref[e] >= 0)
        def _():
            for cp in fetch(next_ref[e], 1 - s):
                cp.start()

        for cp in fetch(e, s):
            cp.wait()
        for c in range(2 * D_FF // 256):
            r = _dot(wgu_buf[s, :, 256 * c:256 * (c + 1)].astype(BF16), perm_ref[...])
            wg_s[:, 128 * c:128 * (c + 1)] = r[:, :128].astype(BF16)
            wl_s[:, 128 * c:128 * (c + 1)] = r[:, 128:].astype(BF16)
        wd_s[...] = wd_buf[s].astype(BF16)

    def expert_rows(rows):
        x = _from_row_tiles(x_ref, rows).astype(BF16)
        glu = jnp.minimum(_dot(x, wg_s[...]) + bg_ref[...], SWIGLU_LIMIT)
        lin = jnp.clip(_dot(x, wl_s[...]) + bl_ref[...], -SWIGLU_LIMIT, SWIGLU_LIMIT)
        act = glu * jax.nn.sigmoid(SWIGLU_ALPHA * glu) * (lin + 1.0)
        _to_row_tiles(o_ref, _dot(act.astype(BF16), wd_s[...]) + bd_ref[...])

    def zero_rows(first, rows):
        o_ref[pl.ds(first * SUBLANES, rows * SUBLANES), :] = jnp.zeros((rows * SUBLANES, 128), F32)

    half = MOE_TILE // 2

    @pl.when(used & (nv_ref[i] > half))
    def _():
        expert_rows(MOE_TILE)

    @pl.when(used & (nv_ref[i] <= half))
    def _():
        expert_rows(half)
        zero_rows(half, half)


def _moe(block_e, n_used, n_valid, next_e, e_slot, x_rows, wgu, bg, bl, wd, bd, perm):
    nblk = x_rows.shape[0] // (MOE_TILE * SUBLANES)
    row = lambda i, be, nu, *_: (jnp.minimum(i, nu[0] - 1), 0)
    wsel = lambda i, be, *_: (be[i], 0, 0)
    grid_spec = pltpu.PrefetchScalarGridSpec(
        num_scalar_prefetch=5,
        grid=(nblk,),
        in_specs=[pl.BlockSpec((MOE_TILE * SUBLANES, 128), row),
                  pl.BlockSpec(memory_space=pl.ANY),
                  pl.BlockSpec((None, 1, D_FF), wsel), pl.BlockSpec((None, 1, D_FF), wsel),
                  pl.BlockSpec(memory_space=pl.ANY), pl.BlockSpec((None, 1, D_MODEL), wsel),
                  pl.BlockSpec((256, 256), lambda i, *_: (0, 0))],
        out_specs=pl.BlockSpec((MOE_TILE * SUBLANES, 128), row),
        scratch_shapes=[pltpu.VMEM((2, D_MODEL, 2 * D_FF), F32), pltpu.VMEM((2, D_FF, D_MODEL), F32),
                        pltpu.VMEM((D_MODEL, D_FF), BF16), pltpu.VMEM((D_MODEL, D_FF), BF16),
                        pltpu.VMEM((D_FF, D_MODEL), BF16),
                        pltpu.SemaphoreType.DMA((2,)), pltpu.SemaphoreType.DMA((2,))],
    )
    return pl.pallas_call(
        _moe_kernel,
        grid_spec=grid_spec,
        out_shape=jax.ShapeDtypeStruct(x_rows.shape, F32),
        input_output_aliases={5: 0},
        compiler_params=_cparams("arbitrary"),
        name="moe",
    )(block_e, n_used, n_valid, next_e, e_slot, x_rows, wgu, bg, bl, wd, bd, perm)


def _combine_kernel(src_tbl, cnt_tbl, loc_tbl, off_ref, gate_ref, rows_hbm, x1_ref, o_ref, buf, ybuf, sem, *,
                    tile_base):
    i = pl.program_id(0)
    n = pl.num_programs(0)

    def slot_rows(s):
        return buf.at[pl.ds(pl.multiple_of(s * _SLOT_ROWS, _SLOT_ROWS), _SLOT_ROWS), :]

    def fetch(tile, s):
        _run_copies(tile_base + tile, src_tbl, cnt_tbl, loc_tbl,
                    lambda g, l, size: pltpu.make_async_copy(_tile_span(rows_hbm, g, size),
                                                             _tile_span(slot_rows(s), l, size), sem.at[s]))

    @pl.when(i == 0)
    def _():
        fetch(0, 0)

    @pl.when(i + 1 < n)
    def _():
        fetch(i + 1, (i + 1) % 2)

    slot = i % 2
    pltpu.make_async_copy(slot_rows(slot), slot_rows(slot), sem.at[slot]).wait()

    def body(t, c):
        acc = None
        for k in range(TOP_K):
            off = pl.multiple_of(off_ref[0, k * ROUTE_TILE + t], SUBLANES)
            v = gate_ref[0, k * ROUTE_TILE + t] * buf[pl.ds(off, SUBLANES), :]
            acc = v if acc is None else acc + v
        ybuf[pl.ds(pl.multiple_of(t * SUBLANES, SUBLANES), SUBLANES), :] = acc
        return c

    lax.fori_loop(0, ROUTE_TILE, body, 0, unroll=8)
    o_ref[...] = x1_ref[...] + _from_row_tiles(ybuf, ROUTE_TILE)


def _combine(runs, off, gates, rows, x1, tile_base):
    assert tile_base % 2 == 0
    t = x1.shape[0]
    tile_rows = ROUTE_TILE * SUBLANES
    smem_tile = pl.BlockSpec((None, 1, TOP_K * ROUTE_TILE), lambda i, *_: (tile_base + i, 0, 0),
                             memory_space=pltpu.SMEM)
    grid_spec = pltpu.PrefetchScalarGridSpec(
        num_scalar_prefetch=3,
        grid=(t // ROUTE_TILE,),
        in_specs=[smem_tile, smem_tile, pl.BlockSpec(memory_space=pl.ANY),
                  pl.BlockSpec((ROUTE_TILE, D_MODEL), lambda i, *_: (i, 0))],
        out_specs=pl.BlockSpec((ROUTE_TILE, D_MODEL), lambda i, *_: (i, 0)),
        scratch_shapes=[pltpu.VMEM((2 * _SLOT_ROWS, 128), F32), pltpu.VMEM((tile_rows, 128), F32),
                        pltpu.SemaphoreType.DMA((2,))],
    )
    return pl.pallas_call(
        functools.partial(_combine_kernel, tile_base=tile_base),
        grid_spec=grid_spec,
        out_shape=jax.ShapeDtypeStruct((t, D_MODEL), F32),
        compiler_params=_cparams("arbitrary"),
        name="combine",
    )(*runs, off, gates, rows, x1)


def _columns_to_row(cols):
    lane = lax.broadcasted_iota(jnp.int32, (cols[0].shape[0], 128), 1)
    staged = jnp.zeros((cols[0].shape[0], 128), cols[0].dtype)
    for k, col in enumerate(cols):
        staged = jnp.where(lane == k, col, staged)
    rows = staged.T
    return jnp.concatenate([rows[k:k + 1] for k in range(len(cols))], axis=1)


def _route(before, cnt, loc, n_tokens):
    n_assign = n_tokens * TOP_K
    table = lambda a: a[:, 0, :N_EXPERTS].astype(jnp.int32)
    before, cnt, run_loc = table(before), table(cnt), table(loc)
    counts = before[-1] + cnt[-1]
    padded = ((counts + MOE_TILE - 1) // MOE_TILE) * MOE_TILE
    pend = jnp.cumsum(padded)
    pstart = pend - padded
    run_src = pstart[None, :] + before
    nblk = (n_assign + MOE_TILE - 1) // MOE_TILE + N_EXPERTS
    n_used = (pend[-1] // MOE_TILE).astype(jnp.int32)
    block_start = jnp.arange(nblk, dtype=jnp.int32) * MOE_TILE
    block_e = jnp.sum(pend[None, :] <= jnp.minimum(block_start, pend[-1] - 1)[:, None], axis=1)
    block_e = jnp.minimum(block_e, N_EXPERTS - 1).astype(jnp.int32)
    region_end = jnp.sum(jnp.where(block_e[:, None] == jnp.arange(N_EXPERTS)[None, :], (pstart + counts)[None, :], 0),
                         axis=1)
    n_valid = jnp.clip(region_end - block_start, 0, MOE_TILE).astype(jnp.int32)
    ids = jnp.arange(N_EXPERTS, dtype=jnp.int32)
    nonempty = counts > 0
    next_e = jnp.min(jnp.where((ids[None, :] > ids[:, None]) & nonempty[None, :], ids[None, :], N_EXPERTS), axis=1)
    next_e = jnp.where(next_e == N_EXPERTS, -1, next_e).astype(jnp.int32)
    e_slot = ((jnp.cumsum(nonempty.astype(jnp.int32)) - nonempty.astype(jnp.int32)) % 2).astype(jnp.int32)
    blocks = (block_e, n_used.reshape(1), n_valid, next_e, e_slot)
    runs = (run_src.reshape(-1), cnt.reshape(-1), run_loc.reshape(-1))
    return runs, blocks, pend, padded, nblk * MOE_TILE


def kernel(x_prompt, x_sample, cache_k, cache_v, state_ssm_re, state_ssm_im, norm_mix_g, w_in, q_norm_g, k_norm_g, attn_sinks, ssm_lambda_re, ssm_lambda_im, ssm_b_re, ssm_b_im, ssm_c_re, ssm_c_im, ssm_d, ssm_log_dt, w_glu, b_glu, attn_out_norm_g, ssm_out_norm_g, w_out, norm_ffn_g, w_router, b_router, w_gate_up, b_gate_up, w_down, b_down):
    depth = w_in.shape[0]
    assert depth == 1
    bp, sp, _ = x_prompt.shape
    bs, ss, _ = x_sample.shape
    tp, ts = bp * sp, bs * ss
    assert bp == SUBLANES and sp % S5_CHUNK == 0 and bs % SUBLANES == 0

    perm = jnp.asarray(_Q_PERM)
    w_in0 = w_in[0]
    wq = w_in0[:, :D_ATTN].reshape(D_MODEL, N_HEADS, HEAD_DIM)[:, perm].reshape(D_MODEL, D_ATTN)
    w_in_b = jnp.concatenate([wq, w_in0[:, D_ATTN:]], axis=1).astype(BF16)
    qkg = jnp.concatenate([jnp.tile(q_norm_g[0], N_HEADS), jnp.tile(k_norm_g[0], N_KV_HEADS)])[None]
    pmat = jnp.kron(jnp.eye(256 // HEAD_DIM, dtype=F32),
                    jnp.full((HEAD_DIM, HEAD_DIM), 1.0 / HEAD_DIM, F32)).astype(BF16)
    g_mix = norm_mix_g[0][None]
    sinks = attn_sinks[0]
    g_attn = attn_out_norm_g[0].reshape(N_HEADS, HEAD_DIM)[perm].reshape(1, D_ATTN)
    w_out0 = w_out[0]
    w_out_a = w_out0[:D_ATTN].reshape(N_HEADS, HEAD_DIM, D_MODEL)[perm].reshape(D_ATTN, D_MODEL).astype(BF16)
    w_out_s = w_out0[D_ATTN:].astype(BF16)
    g_ssm = ssm_out_norm_g[0][None]
    g_ffn = norm_ffn_g[0][None]
    w_r = jnp.pad(w_router[0], ((0, 0), (0, 128 - N_EXPERTS))).astype(BF16)
    b_r = jnp.pad(b_router[0], (0, 128 - N_EXPERTS), constant_values=NEG_INF)[None]

    a_re, a_im, bb_re, bb_im = _s5_prep(
        ssm_lambda_re[0], ssm_lambda_im[0], ssm_log_dt[0][:, None],
        jnp.swapaxes(ssm_b_re[0], 1, 2), jnp.swapaxes(ssm_b_im[0], 1, 2))
    chan_g = jnp.arange(D_SSM)[:, None] // SSM_GROUP
    state_g = jnp.arange(D_STATE)[None, :] // SSM_STATE
    bd_b = lambda bb: jnp.where(chan_g == state_g, jnp.tile(bb.reshape(D_SSM, SSM_STATE), (1, N_SSM_GROUPS)), 0.0)
    wb = jnp.concatenate([bd_b(bb_re), bd_b(bb_im)], axis=1).astype(BF16)
    bd_c = lambda c: jnp.where(state_g.T == chan_g.T,
                               jnp.tile(jnp.swapaxes(c, 1, 2).reshape(D_STATE, SSM_GROUP), (1, N_SSM_GROUPS)), 0.0)
    wc = jnp.concatenate([bd_c(ssm_c_re[0]), -bd_c(ssm_c_im[0])], axis=0).astype(BF16)
    a_re, a_im = a_re.reshape(1, D_STATE), a_im.reshape(1, D_STATE)
    d_skip = ssm_d[0].reshape(1, D_SSM)
    w_glu_b = w_glu[0].astype(BF16)
    b_glu0 = b_glu[0][None]

    b_g = b_gate_up[0][:, None, 0::2]
    b_l = b_gate_up[0][:, None, 1::2]
    b_d = b_down[0][:, None, :]
    idx = jnp.arange(256)
    deint = (idx[None, :] == jnp.where(idx % 2 == 0, idx // 2, 128 + idx // 2)[:, None]).astype(BF16)

    xp2 = x_prompt.reshape(tp, D_MODEL)
    xs2 = x_sample.reshape(ts, D_MODEL)
    qp, kp, vp, up = _in_proj(xp2, g_mix, w_in_b, qkg, pmat)
    qs, ks, vs, us = _in_proj(xs2, g_mix, w_in_b, qkg, pmat)

    kp3, vp3 = kp.reshape(bp, sp, D_KV), vp.reshape(bp, sp, D_KV)
    attn_p = _band_attention(sinks, qp.reshape(bp, sp, D_ATTN), kp3, vp3, g_attn).reshape(tp, D_ATTN)
    ck = cache_k[0].reshape(bs, -1, D_KV)
    cv = cache_v[0].reshape(bs, -1, D_KV)
    attn_s, new_ck, new_cv = _cache_attention(sinks, qs, ks, vs, ck, cv, g_attn, ss)

    zeros_p = jnp.zeros((bp, D_STATE), F32)
    s5_args = (a_re, a_im, wb, wc, d_skip, w_glu_b, b_glu0, g_ssm)
    ssm_p, hr_p, hi_p = _s5(up.reshape(bp, sp, D_SSM), zeros_p, zeros_p, *s5_args,
                            tt=S5_CHUNK, time_chunked=True)
    ssm_s, hr_s, hi_s = _s5(us.reshape(bs // SUBLANES, SUBLANES * ss, D_SSM),
                            state_ssm_re[0].reshape(bs, D_STATE), state_ssm_im[0].reshape(bs, D_STATE),
                            *s5_args, tt=ss, time_chunked=False)

    tri = (jnp.arange(ROUTE_TILE)[:, None] > jnp.arange(ROUTE_TILE)[None, :]).astype(BF16)
    upper = (jnp.arange(128)[:, None] < jnp.arange(128)[None, :]).astype(BF16)
    proj_args = (w_out_a, w_out_s, g_ffn, w_r, b_r, tri, upper)
    x1p, *tab_p, counts_p = _out_proj(attn_p, ssm_p.reshape(tp, D_SSM), xp2, *proj_args,
                                      jnp.zeros((1, 128), F32), 0)
    x1s, *tab_s, _ = _out_proj(attn_s, ssm_s.reshape(ts, D_SSM), xs2, *proj_args, counts_p, tp // ROUTE_TILE)

    gates, off, before, cnt, loc = (jnp.concatenate([p, s_], axis=0) for p, s_ in zip(tab_p, tab_s))
    runs, blocks, pend, padded, n_rows = _route(before, cnt, loc, tp + ts)
    x_rows = _dispatch(runs, pend, padded, blocks[1], off, x1p, x1s, g_ffn, n_rows)
    out_rows = _moe(*blocks, x_rows, w_gate_up[0], b_g, b_l, w_down[0], b_d, deint)
    yp = _combine(runs, off, gates, out_rows, x1p, 0).reshape(bp, sp, D_MODEL)
    ys = _combine(runs, off, gates, out_rows, x1s, tp // ROUTE_TILE).reshape(bs, ss, D_MODEL)

    kv5 = lambda a, b_: a.reshape(b_, -1, N_KV_HEADS, HEAD_DIM)
    new_kp = kv5(kp3[:, -WINDOW:], bp)[None]
    new_vp = kv5(vp3[:, -WINDOW:], bp)[None]
    new_ks = kv5(new_ck, bs)[None]
    new_vs = kv5(new_cv, bs)[None]
    st = lambda h, b_: h.reshape(1, b_, N_SSM_GROUPS, SSM_STATE)
    return (yp, ys, new_kp, new_vp, st(hr_p, bp), st(hi_p, bp),
            new_ks, new_vs, st(hr_s, bs), st(hi_s, bs))
```

```python
import functools
import math

import jax
import jax.numpy as jnp
from jax import lax
from jax.experimental import pallas as pl
from jax.experimental.pallas import tpu as pltpu

F32 = jnp.float32
BF16 = jnp.bfloat16

D_MODEL = 1024
D_ATTN = 512
D_SSM = 512
HEAD_DIM = 64
N_HEADS = 8
N_KV_HEADS = 2
D_KV = N_KV_HEADS * HEAD_DIM
WINDOW = 128
SSM_GROUP = 16
N_SSM_GROUPS = 32
SSM_STATE = 64
D_STATE = N_SSM_GROUPS * SSM_STATE
N_EXPERTS = 32
TOP_K = 4
D_FF = 1024
SWIGLU_LIMIT = 7.0
SWIGLU_ALPHA = 1.702
RMS_EPS = 1e-6
NEG_INF = -1e30
D_IN_PROJ = D_ATTN + 2 * D_KV + D_SSM
D_QK = D_ATTN + D_KV

SUBLANES = 8
VMEM_LIMIT = 56 * 1024 * 1024

ROW_TILE = 1024
MOE_TILE = 512
ATTN_BLOCKS = 4
ROUTE_TILE = 512
_SLOT_ROWS = ROUTE_TILE * TOP_K * SUBLANES
_RUN_SIZES = tuple(1 << b for b in range(ROUTE_TILE.bit_length() - 1, -1, -1))
S5_CHUNK = 128
S5_COLS = 512

_Q_PERM = (0, 4, 1, 5, 2, 6, 3, 7)


def _cparams(*sem):
    return pltpu.CompilerParams(dimension_semantics=sem, vmem_limit_bytes=VMEM_LIMIT)


def _dot(a, b):
    return jnp.dot(a, b, preferred_element_type=F32)


def _rms(x):
    return x * lax.rsqrt(jnp.mean(x * x, axis=-1, keepdims=True) + RMS_EPS)


def _in_proj_kernel(x_ref, g_ref, w_ref, qkg_ref, p_ref, q_ref, k_ref, v_ref, u_ref):
    xn = _rms(x_ref[...]) * g_ref[...]
    h = _dot(xn.astype(BF16), w_ref[...])
    qk = h[:, :D_QK]
    sq = (qk * qk).astype(BF16)
    p = p_ref[...]
    ms = jnp.concatenate(
        [_dot(sq[:, 0:256], p), _dot(sq[:, 256:512], p), _dot(sq[:, 512:640], p[:128, :128])],
        axis=-1)
    qkn = qk * lax.rsqrt(ms + RMS_EPS) * qkg_ref[...]
    q_ref[...] = qkn[:, :D_ATTN].astype(BF16)
    k_ref[...] = qkn[:, D_ATTN:]
    v_ref[...] = h[:, D_QK:D_QK + D_KV]
    u_ref[...] = h[:, D_QK + D_KV:]


def _in_proj(x2d, g, w, qkg, pmat):
    t = x2d.shape[0]
    tm = min(ROW_TILE, t)
    row = lambda i: (i, 0)
    fix = lambda i: (0, 0)
    return pl.pallas_call(
        _in_proj_kernel,
        grid=(t // tm,),
        in_specs=[pl.BlockSpec((tm, D_MODEL), row), pl.BlockSpec((1, D_MODEL), fix),
                  pl.BlockSpec((D_MODEL, D_IN_PROJ), fix), pl.BlockSpec((1, D_QK), fix),
                  pl.BlockSpec((256, 256), fix)],
        out_specs=[pl.BlockSpec((tm, D_ATTN), row), pl.BlockSpec((tm, D_KV), row),
                   pl.BlockSpec((tm, D_KV), row), pl.BlockSpec((tm, D_SSM), row)],
        out_shape=[jax.ShapeDtypeStruct((t, D_ATTN), BF16), jax.ShapeDtypeStruct((t, D_KV), F32),
                   jax.ShapeDtypeStruct((t, D_KV), F32), jax.ShapeDtypeStruct((t, D_SSM), F32)],
        compiler_params=_cparams("parallel"),
        name="in_proj",
    )(x2d, g, w, qkg, pmat)


def _softmax_pv(s_blocks, v_blocks, sink):
    m = sink
    for s in s_blocks:
        m = jnp.maximum(m, jnp.max(s, axis=-1, keepdims=True))
    den = jnp.exp(sink - m)
    acc = None
    for s, v in zip(s_blocks, v_blocks):
        p = jnp.exp(s - m)
        den = den + jnp.sum(p, axis=-1, keepdims=True)
        pv = _dot(p.astype(BF16), v)
        acc = pv if acc is None else acc + pv
    return acc / den


def _band_attn_kernel(sink_ref, q_ref, kp_ref, kc_ref, vp_ref, vc_ref, g_ref, o_ref):
    i = pl.program_id(1)
    row = lax.broadcasted_iota(jnp.int32, (WINDOW, 2 * WINDOW), 0)
    col = lax.broadcasted_iota(jnp.int32, (WINDOW, 2 * WINDOW), 1)
    band = (col > row) & (col <= row + WINDOW)
    lane = lax.broadcasted_iota(jnp.int32, (WINDOW, 128), 1)
    low = lane < HEAD_DIM
    zero = jnp.zeros((), BF16)
    for j in range(ATTN_BLOCKS):
        cur = slice(WINDOW * j, WINDOW * (j + 1))
        q = q_ref[cur, :]
        k_prev = kp_ref[...] if j == 0 else kc_ref[WINDOW * (j - 1):WINDOW * j, :]
        v_prev = vp_ref[...] if j == 0 else vc_ref[WINDOW * (j - 1):WINDOW * j, :]
        kb = jnp.concatenate([k_prev, kc_ref[cur, :]], axis=0).astype(BF16)
        vb = jnp.concatenate([v_prev, vc_ref[cur, :]], axis=0).astype(BF16)
        mask = band & ((col >= WINDOW) | (i > 0)) if j == 0 else band
        outs = []
        for pair in range(N_HEADS // 2):
            qp = q[:, 128 * pair:128 * (pair + 1)]
            halves = []
            for par in range(2):
                qm = jnp.where(low if par == 0 else ~low, qp, zero)
                s = lax.dot_general(qm, kb, (((1,), (1,)), ((), ())), preferred_element_type=F32)
                s = jnp.where(mask, s * (HEAD_DIM ** -0.5), NEG_INF)
                halves.append(_softmax_pv([s], [vb], sink_ref[_Q_PERM[2 * pair + par]]))
            outs.append(jnp.where(low, halves[0], halves[1]))
        o = jnp.concatenate(outs, axis=-1)
        o_ref[cur, :] = (_rms(o) * g_ref[...]).astype(BF16)


def _band_attention(sinks, q, k, v, g):
    b, s, _ = q.shape
    qb = WINDOW * ATTN_BLOCKS
    cur = lambda bi, i: (bi, i, 0)
    prev = lambda bi, i: (bi, jnp.maximum(i * ATTN_BLOCKS - 1, 0), 0)
    return pl.pallas_call(
        _band_attn_kernel,
        grid=(b, s // qb),
        in_specs=[pl.BlockSpec(memory_space=pltpu.SMEM),
                  pl.BlockSpec((None, qb, D_ATTN), cur),
                  pl.BlockSpec((None, WINDOW, D_KV), prev), pl.BlockSpec((None, qb, D_KV), cur),
                  pl.BlockSpec((None, WINDOW, D_KV), prev), pl.BlockSpec((None, qb, D_KV), cur),
                  pl.BlockSpec((1, D_ATTN), lambda bi, i: (0, 0))],
        out_specs=pl.BlockSpec((None, qb, D_ATTN), cur),
        out_shape=jax.ShapeDtypeStruct((b, s, D_ATTN), BF16),
        compiler_params=_cparams("parallel", "parallel"),
        name="band_attn",
    )(sinks, q, k, k, v, v, g)


_PAIR_ROWS = 8
_CACHE_BB = 16


def _cache_attn_kernel(sink_ref, q_ref, kn_ref, vn_ref, ck_ref, cv_ref, g_ref, o_ref, nk_ref, nv_ref, *, n_new):
    n_buf = ck_ref.shape[1]
    rows_blk = _CACHE_BB * n_new
    for b in range(_CACHE_BB):
        for new_ref, cache_ref, fresh_ref in ((nk_ref, ck_ref, kn_ref), (nv_ref, cv_ref, vn_ref)):
            new_ref[b, 0:n_buf - n_new, :] = cache_ref[b, n_new:n_buf, :]
            new_ref[b, n_buf - n_new:n_buf, :] = fresh_ref[n_new * b:n_new * (b + 1), :]
    knew = kn_ref[...].astype(BF16)
    vnew = vn_ref[...].astype(BF16)
    lane = lax.broadcasted_iota(jnp.int32, (_PAIR_ROWS, 128), 1)
    low = lane < HEAD_DIM
    zero = jnp.zeros((), BF16)
    n_stack = N_HEADS * _PAIR_ROWS
    r = lax.broadcasted_iota(jnp.int32, (n_stack, 1), 0) % _PAIR_ROWS
    r_seq, r_tok = r // n_new, r % n_new
    colc = lax.broadcasted_iota(jnp.int32, (n_stack, 2 * n_buf), 1)
    c_seq, c_pos = colc // n_buf, colc % n_buf
    mask_c = (c_seq == r_seq) & (c_pos + WINDOW > r_tok + n_buf)
    coln = lax.broadcasted_iota(jnp.int32, (n_stack, rows_blk), 1)
    sink_col = jnp.concatenate(
        [jnp.full((_PAIR_ROWS, 1), sink_ref[_Q_PERM[h]], F32) for h in range(N_HEADS)], axis=0)
    for sp in range(_CACHE_BB // 2):
        q = q_ref[_PAIR_ROWS * sp:_PAIR_ROWS * (sp + 1), :]
        pieces = []
        for pair in range(N_HEADS // 2):
            qp = q[:, 128 * pair:128 * (pair + 1)]
            pieces.append(jnp.where(low, qp, zero))
            pieces.append(jnp.where(low, zero, qp))
        qs = jnp.concatenate(pieces, axis=0)
        kc = jnp.concatenate([ck_ref[2 * sp], ck_ref[2 * sp + 1]], axis=0).astype(BF16)
        vc = jnp.concatenate([cv_ref[2 * sp], cv_ref[2 * sp + 1]], axis=0).astype(BF16)
        nt = (((1,), (1,)), ((), ()))
        s_c = lax.dot_general(qs, kc, nt, preferred_element_type=F32) * (HEAD_DIM ** -0.5)
        s_n = lax.dot_general(qs, knew, nt, preferred_element_type=F32) * (HEAD_DIM ** -0.5)
        s_c = jnp.where(mask_c, s_c, NEG_INF)
        n_seq, n_tok = coln // n_new - 2 * sp, coln % n_new
        mask_n = (n_seq == r_seq) & (n_tok <= r_tok)
        s_n = jnp.where(mask_n, s_n, NEG_INF)
        o = _softmax_pv([s_c, s_n], [vc, vnew], sink_col)
        outs = [jnp.where(low, o[16 * pair:16 * pair + 8], o[16 * pair + 8:16 * pair + 16])
                for pair in range(N_HEADS // 2)]
        oo = jnp.concatenate(outs, axis=-1)
        o_ref[_PAIR_ROWS * sp:_PAIR_ROWS * (sp + 1), :] = (_rms(oo) * g_ref[...]).astype(BF16)


def _cache_attention(sinks, q, k, v, cache_k, cache_v, g, n_new):
    t = q.shape[0]
    nb, n_buf, _ = cache_k.shape
    assert n_new * 2 == _PAIR_ROWS and nb % _CACHE_BB == 0 and n_buf == WINDOW
    rows = _CACHE_BB * n_new
    row = lambda i: (i, 0)
    cache_spec = pl.BlockSpec((_CACHE_BB, n_buf, D_KV), lambda i: (i, 0, 0))
    return pl.pallas_call(
        functools.partial(_cache_attn_kernel, n_new=n_new),
        grid=(nb // _CACHE_BB,),
        in_specs=[pl.BlockSpec(memory_space=pltpu.SMEM),
                  pl.BlockSpec((rows, D_ATTN), row), pl.BlockSpec((rows, D_KV), row),
                  pl.BlockSpec((rows, D_KV), row), cache_spec, cache_spec,
                  pl.BlockSpec((1, D_ATTN), lambda i: (0, 0))],
        out_specs=[pl.BlockSpec((rows, D_ATTN), row), cache_spec, cache_spec],
        out_shape=[jax.ShapeDtypeStruct((t, D_ATTN), BF16), jax.ShapeDtypeStruct(cache_k.shape, F32),
                   jax.ShapeDtypeStruct(cache_v.shape, F32)],
        compiler_params=_cparams("parallel"),
        name="cache_attn",
    )(sinks, q, k, v, cache_k, cache_v, g)


def _s5_prep_kernel(lre_ref, lim_ref, ldt_ref, bre_ref, bim_ref, are_ref, aim_ref, bbre_ref, bbim_ref):
    dt = jnp.exp(ldt_ref[...])
    l_re = jnp.minimum(lre_ref[...], -1e-4)
    l_im = lim_ref[...]
    mag = jnp.exp(l_re * dt)
    a_re = mag * jnp.cos(l_im * dt)
    a_im = mag * jnp.sin(l_im * dt)
    den = l_re * l_re + l_im * l_im
    n_re = a_re - 1.0
    z_re = (n_re * l_re + a_im * l_im) / den
    z_im = (a_im * l_re - n_re * l_im) / den
    are_ref[...] = a_re
    aim_ref[...] = a_im
    br, bi = bre_ref[...], bim_ref[...]
    zr, zi = z_re[:, None, :], z_im[:, None, :]
    bbre_ref[...] = zr * br - zi * bi
    bbim_ref[...] = zr * bi + zi * br


def _s5_prep(lam_re, lam_im, log_dt, b_re_t, b_im_t):
    g, p = lam_re.shape
    sd = jax.ShapeDtypeStruct
    return pl.pallas_call(
        _s5_prep_kernel,
        out_shape=[sd((g, p), F32), sd((g, p), F32), sd(b_re_t.shape, F32), sd(b_re_t.shape, F32)],
        name="s5_prep",
    )(lam_re, lam_im, log_dt, b_re_t, b_im_t)


def _s5_kernel(u_ref, h0r_ref, h0i_ref, ar_ref, ai_ref, wb_ref, wc_ref, d_ref, wglu_ref, bglu_ref,
               g_ref, o_ref, hr_ref, hi_ref, bu_ref, hs_ref, usc_ref, ysc_ref, *, tt):
    j = pl.program_id(1)
    rows = SUBLANES * tt
    n_tiles = D_STATE // 128
    time_major = tt % SUBLANES == 0

    @pl.when(j == 0)
    def _():
        hs_ref[:, :D_STATE] = h0r_ref[...]
        hs_ref[:, D_STATE:] = h0i_ref[...]

    u = u_ref[...].reshape(rows, D_SSM)
    if time_major:
        for c in range(D_SSM // 128):
            for b in range(SUBLANES):
                usc_ref[c, pl.ds(b, tt, stride=SUBLANES), :] = u[b * tt:(b + 1) * tt, 128 * c:128 * (c + 1)]
        ub = jnp.concatenate([usc_ref[c] for c in range(D_SSM // 128)], axis=-1).astype(BF16)
    else:
        ub = u.astype(BF16)

    def step_rows(t):
        if time_major:
            return pl.ds(pl.multiple_of(t * SUBLANES, SUBLANES), SUBLANES)
        return pl.ds(t, SUBLANES, stride=tt)

    for n in range(2 * D_STATE // 256):
        band = (n % (D_STATE // 256)) // 2
        res = _dot(ub[:, 128 * band:128 * (band + 1)],
                   wb_ref[128 * band:128 * (band + 1), 256 * n:256 * (n + 1)])
        bu_ref[2 * n] = res[:, :128]
        bu_ref[2 * n + 1] = res[:, 128:]

    tiles_per_pass = S5_COLS // 128
    for c0 in range(0, n_tiles, tiles_per_pass):
        tiles = range(c0, c0 + tiles_per_pass)
        a_r = [jnp.broadcast_to(ar_ref[:, 128 * c:128 * (c + 1)], (SUBLANES, 128)) for c in tiles]
        a_i = [jnp.broadcast_to(ai_ref[:, 128 * c:128 * (c + 1)], (SUBLANES, 128)) for c in tiles]

        def step(t, carry, tiles=tiles, a_r=a_r, a_i=a_i):
            at_t = step_rows(t)
            out = []
            for k, c in enumerate(tiles):
                h_r, h_i = carry[2 * k], carry[2 * k + 1]
                n_r = a_r[k] * h_r - a_i[k] * h_i + bu_ref[c, at_t, :]
                n_i = a_r[k] * h_i + a_i[k] * h_r + bu_ref[n_tiles + c, at_t, :]
                bu_ref[c, at_t, :] = n_r
                bu_ref[n_tiles + c, at_t, :] = n_i
                out += [n_r, n_i]
            return tuple(out)

        init = []
        for c in tiles:
            init += [hs_ref[:, 128 * c:128 * (c + 1)], hs_ref[:, D_STATE + 128 * c:D_STATE + 128 * (c + 1)]]
        fin = lax.fori_loop(0, tt, step, tuple(init), unroll=min(tt, 8))
        for k, c in enumerate(tiles):
            hs_ref[:, 128 * c:128 * (c + 1)] = fin[2 * k]
            hs_ref[:, D_STATE + 128 * c:D_STATE + 128 * (c + 1)] = fin[2 * k + 1]

    def h_cols(first_tile):
        return jnp.concatenate([bu_ref[first_tile + k] for k in range(4)], axis=-1).astype(BF16)

    ys = []
    for m in range(D_SSM // 128):
        y = _dot(h_cols(4 * m), wc_ref[512 * m:512 * (m + 1), 128 * m:128 * (m + 1)])
        y = y + _dot(h_cols(n_tiles + 4 * m),
                     wc_ref[D_STATE + 512 * m:D_STATE + 512 * (m + 1), 128 * m:128 * (m + 1)])
        if time_major:
            ysc_ref[m] = y
            y = jnp.concatenate([ysc_ref[m, pl.ds(b, tt, stride=SUBLANES), :] for b in range(SUBLANES)], axis=0)
        ys.append(y)
    y = jnp.concatenate(ys, axis=-1) + d_ref[...] * u
    z = _dot(jax.nn.gelu(y).astype(BF16), wglu_ref[...]) + bglu_ref[...]
    s = z[:, :D_SSM] * jax.nn.sigmoid(z[:, D_SSM:])
    o_ref[...] = (_rms(s) * g_ref[...]).astype(BF16).reshape(o_ref.shape)

    @pl.when(j == pl.num_programs(1) - 1)
    def _():
        hr_ref[...] = hs_ref[:, :D_STATE]
        hi_ref[...] = hs_ref[:, D_STATE:]


def _s5(u, h0r, h0i, a_re, a_im, wb, wc, d, wglu, bglu, g, *, tt, time_chunked):
    nbg = h0r.shape[0] // SUBLANES
    if time_chunked:
        nchunks = u.shape[1] // tt
        u_spec = pl.BlockSpec((SUBLANES, tt, D_SSM), lambda gi, j: (gi, j, 0))
    else:
        nchunks = 1
        u_spec = pl.BlockSpec((None, SUBLANES * tt, D_SSM), lambda gi, j: (gi, 0, 0))
    fix = lambda gi, j: (0, 0)
    st_spec = pl.BlockSpec((SUBLANES, D_STATE), lambda gi, j: (gi, 0))
    sd = jax.ShapeDtypeStruct
    return pl.pallas_call(
        functools.partial(_s5_kernel, tt=tt),
        grid=(nbg, nchunks),
        in_specs=[u_spec, st_spec, st_spec,
                  pl.BlockSpec((1, D_STATE), fix), pl.BlockSpec((1, D_STATE), fix),
                  pl.BlockSpec((D_SSM, 2 * D_STATE), fix), pl.BlockSpec((2 * D_STATE, D_SSM), fix),
                  pl.BlockSpec((1, D_SSM), fix), pl.BlockSpec((D_SSM, 2 * D_SSM), fix),
                  pl.BlockSpec((1, 2 * D_SSM), fix), pl.BlockSpec((1, D_SSM), fix)],
        out_specs=[u_spec, st_spec, st_spec],
        out_shape=[sd(u.shape, BF16), sd(h0r.shape, F32), sd(h0r.shape, F32)],
        scratch_shapes=[pltpu.VMEM((2 * D_STATE // 128, SUBLANES * tt, 128), F32),
                        pltpu.VMEM((SUBLANES, 2 * D_STATE), F32),
                        pltpu.VMEM((D_SSM // 128, SUBLANES * tt, 128), F32),
                        pltpu.VMEM((D_SSM // 128, SUBLANES * tt, 128), F32)],
        compiler_params=_cparams("parallel", "arbitrary"),
        name="s5",
    )(u, h0r, h0i, a_re, a_im, wb, wc, d, wglu, bglu, g)


def _to_row_tiles(ref, val, first=0):
    rows = val.shape[0]
    for c in range(D_MODEL // 128):
        ref[pl.ds(first * SUBLANES + c, rows, stride=SUBLANES), :] = val[:, 128 * c:128 * (c + 1)]


def _from_row_tiles(ref, rows, lead=(), first=0):
    return jnp.concatenate(
        [ref[(*lead, pl.ds(first * SUBLANES + c, rows, stride=SUBLANES), slice(None))]
         for c in range(D_MODEL // 128)], axis=-1)


def _route_tile(logits_t, tile, triu_ref, low_ref, carry, gate_ref, off_ref, before_ref, cnt_ref, loc_ref):
    l = logits_t
    expert = lax.broadcasted_iota(jnp.int32, l.shape, 0).astype(F32)
    vals, sels = [], []
    for _ in range(TOP_K):
        m = jnp.max(l, axis=0, keepdims=True)
        idx = jnp.min(jnp.where(l == m, expert, float(N_EXPERTS)), axis=0, keepdims=True)
        sel = expert == idx
        l = jnp.where(sel, -jnp.inf, l)
        vals.append(m)
        sels.append(sel)
    exps = [jnp.exp(v - vals[0]) for v in vals]
    den = exps[0] + exps[1] + exps[2] + exps[3]
    onehot = jnp.where(sels[0] | sels[1] | sels[2] | sels[3], 1.0, 0.0)
    within = _dot(onehot.astype(BF16), triu_ref[...])
    cnt = jnp.broadcast_to(jnp.sum(onehot, axis=1, keepdims=True), onehot.shape)
    cnt_hi = jnp.floor(cnt * (1.0 / 16.0))
    cnt_lo = cnt - 16.0 * cnt_hi
    loc = 16.0 * _dot(low_ref[...], cnt_hi.astype(BF16)) + _dot(low_ref[...], cnt_lo.astype(BF16))
    slot_base = ((tile % 2) * (ROUTE_TILE * TOP_K)).astype(F32)
    rows = [jnp.sum(jnp.where(s, within + loc, 0.0), axis=0, keepdims=True) for s in sels]
    gate_ref[...] = jnp.concatenate([e / den for e in exps], axis=1)
    off_ref[...] = ((jnp.concatenate(rows, axis=1) + slot_base) * SUBLANES).astype(jnp.int32)
    before_ref[...] = carry[...]
    cnt_ref[...] = cnt[:, :128]
    loc_ref[...] = loc[:, :128]
    carry[...] = carry[...] + cnt[:, :128]


def _out_proj_kernel(a_ref, s_ref, x_ref, wa_ref, ws_ref, g_ref, wrt_ref, brt_ref, triu_ref, low_ref, cin_ref,
                     x1_ref, gate_ref, off_ref, before_ref, cnt_ref, loc_ref, cout_ref, carry, *, tile_base):
    i = pl.program_id(0)

    @pl.when(i == 0)
    def _():
        carry[...] = cin_ref[...]

    x1 = x_ref[...] + _dot(a_ref[...], wa_ref[...]) + _dot(s_ref[...], ws_ref[...])
    x1_ref[...] = x1
    xn = _rms(x1) * g_ref[...]
    logits_t = lax.dot_general(wrt_ref[...], xn.astype(BF16), (((1,), (1,)), ((), ())),
                               preferred_element_type=F32) + brt_ref[...]
    _route_tile(logits_t, tile_base + i, triu_ref, low_ref, carry, gate_ref, off_ref, before_ref, cnt_ref, loc_ref)
    cout_ref[...] = carry[...]


def _out_proj(attn_n, ssm_n, x2d, wa, ws, g, wrt, brt, triu, low, counts_in, tile_base):
    t = x2d.shape[0]
    tm = ROUTE_TILE
    n_tiles = t // tm
    row = lambda i: (i, 0)
    fix = lambda i: (0, 0)
    sd = jax.ShapeDtypeStruct
    k_spec = pl.BlockSpec((None, 1, TOP_K * tm), lambda i: (i, 0, 0))
    t_spec = pl.BlockSpec((None, N_EXPERTS, 128), lambda i: (i, 0, 0))
    return pl.pallas_call(
        functools.partial(_out_proj_kernel, tile_base=tile_base),
        grid=(n_tiles,),
        in_specs=[pl.BlockSpec((tm, D_ATTN), row), pl.BlockSpec((tm, D_SSM), row),
                  pl.BlockSpec((tm, D_MODEL), row),
                  pl.BlockSpec((D_ATTN, D_MODEL), fix), pl.BlockSpec((D_SSM, D_MODEL), fix),
                  pl.BlockSpec((1, D_MODEL), fix), pl.BlockSpec((N_EXPERTS, D_MODEL), fix),
                  pl.BlockSpec((N_EXPERTS, tm), fix), pl.BlockSpec((tm, tm), fix),
                  pl.BlockSpec((N_EXPERTS, N_EXPERTS), fix), pl.BlockSpec((N_EXPERTS, 128), fix)],
        out_specs=[pl.BlockSpec((tm, D_MODEL), row), k_spec, k_spec, t_spec, t_spec, t_spec,
                   pl.BlockSpec((N_EXPERTS, 128), fix)],
        out_shape=[sd((t, D_MODEL), F32), sd((n_tiles, 1, TOP_K * tm), F32), sd((n_tiles, 1, TOP_K * tm), jnp.int32),
                   sd((n_tiles, N_EXPERTS, 128), F32), sd((n_tiles, N_EXPERTS, 128), F32),
                   sd((n_tiles, N_EXPERTS, 128), F32), sd((N_EXPERTS, 128), F32)],
        scratch_shapes=[pltpu.VMEM((N_EXPERTS, 128), F32)],
        compiler_params=_cparams("arbitrary"),
        name="out_proj",
    )(attn_n, ssm_n, x2d, wa, ws, g, wrt, brt, triu, low, counts_in)


def _tile_span(ref, first_row, n_rows, lead=()):
    start = pl.multiple_of(first_row * SUBLANES, SUBLANES)
    return ref.at[(*lead, pl.ds(start, n_rows * SUBLANES), slice(None))]


def _run_copies(tile, src_tbl, cnt_tbl, loc_tbl, make_copy):
    def body(e, c):
        j = tile * N_EXPERTS + e
        cnt, src, loc = cnt_tbl[j], src_tbl[j], loc_tbl[j]
        off = 0
        for size in _RUN_SIZES:
            @pl.when((cnt & size) != 0)
            def _(off=off, size=size):
                make_copy(src + off, loc + off, size).start()
            off = off + (cnt & size)
        return c

    lax.fori_loop(0, N_EXPERTS, body, 0)


def _dispatch_kernel(src_tbl, cnt_tbl, loc_tbl, pend_ref, padded_ref, nu_ref, off_ref, xa_ref, xb_ref, g_ref,
                     rows_ref, buf, zbuf, xt, sem, zsem, *, tiles_a, nblk):
    i = pl.program_id(0)
    n = pl.num_programs(0)
    slot = i % 2

    def zero_block(start_row):
        return pltpu.make_async_copy(zbuf, _tile_span(rows_ref, start_row, MOE_TILE), zsem)

    @pl.when(i == 0)
    def _():
        zbuf[...] = jnp.zeros_like(zbuf)
        for e in range(N_EXPERTS):
            @pl.when(padded_ref[e] > 0)
            def _(e=e):
                zero_block(pend_ref[e] - MOE_TILE).start()

        def tail_start(j, c):
            zero_block(j * MOE_TILE).start()
            return c

        def tail_wait(j, c):
            zero_block(j * MOE_TILE).wait()
            return c

        lax.fori_loop(nu_ref[0], nblk, tail_start, 0)
        for e in range(N_EXPERTS):
            @pl.when(padded_ref[e] > 0)
            def _(e=e):
                zero_block(pend_ref[e] - MOE_TILE).wait()
        lax.fori_loop(nu_ref[0], nblk, tail_wait, 0)

    def slot_rows(s):
        return buf.at[pl.ds(pl.multiple_of(s * _SLOT_ROWS, _SLOT_ROWS), _SLOT_ROWS), :]

    def slot_done(s):
        return pltpu.make_async_copy(slot_rows(s), slot_rows(s), sem.at[s])

    @pl.when(i >= 2)
    def _():
        slot_done(slot).wait()

    def fill(x_ref):
        _to_row_tiles(xt, _rms(x_ref[...]) * g_ref[...])

        def body(t, c):
            v = xt[pl.ds(pl.multiple_of(t * SUBLANES, SUBLANES), SUBLANES), :]
            for k in range(TOP_K):
                off = pl.multiple_of(off_ref[0, k * ROUTE_TILE + t], SUBLANES)
                buf[pl.ds(off, SUBLANES), :] = v
            return c

        lax.fori_loop(0, ROUTE_TILE, body, 0, unroll=8)

    @pl.when(i < tiles_a)
    def _():
        fill(xa_ref)

    @pl.when(i >= tiles_a)
    def _():
        fill(xb_ref)

    _run_copies(i, src_tbl, cnt_tbl, loc_tbl,
                lambda g, l, size: pltpu.make_async_copy(_tile_span(slot_rows(slot), l, size),
                                                         _tile_span(rows_ref, g, size), sem.at[slot]))

    @pl.when(i == n - 1)
    def _():
        slot_done(1 - slot).wait()
        slot_done(slot).wait()


def _dispatch(runs, pend, padded, n_used, off, xa, xb, g, n_rows):
    tile_rows = ROUTE_TILE * SUBLANES
    tiles_a, tiles_b = xa.shape[0] // ROUTE_TILE, xb.shape[0] // ROUTE_TILE
    assert tiles_a + tiles_b >= 2
    grid_spec = pltpu.PrefetchScalarGridSpec(
        num_scalar_prefetch=6, grid=(tiles_a + tiles_b,),
        in_specs=[pl.BlockSpec((None, 1, TOP_K * ROUTE_TILE), lambda i, *_: (i, 0, 0), memory_space=pltpu.SMEM),
                  pl.BlockSpec((ROUTE_TILE, D_MODEL), lambda i, *_: (jnp.minimum(i, tiles_a - 1), 0)),
                  pl.BlockSpec((ROUTE_TILE, D_MODEL), lambda i, *_: (jnp.maximum(i - tiles_a, 0), 0)),
                  pl.BlockSpec((1, D_MODEL), lambda i, *_: (0, 0))],
        out_specs=pl.BlockSpec(memory_space=pl.ANY),
        scratch_shapes=[pltpu.VMEM((2 * _SLOT_ROWS, 128), F32),
                        pltpu.VMEM((MOE_TILE * SUBLANES, 128), F32), pltpu.VMEM((tile_rows, 128), F32),
                        pltpu.SemaphoreType.DMA((2,)), pltpu.SemaphoreType.DMA(())])
    return pl.pallas_call(
        functools.partial(_dispatch_kernel, tiles_a=tiles_a, nblk=n_rows // MOE_TILE), grid_spec=grid_spec,
        out_shape=jax.ShapeDtypeStruct((n_rows * SUBLANES, 128), F32),
        compiler_params=_cparams("arbitrary"), name="dispatch",
    )(*runs, pend, padded, n_used, off, xa, xb, g)


def _moe_kernel(be_ref, nu_ref, nv_ref, next_ref, slot_ref, x_ref, wgu_hbm, bg_ref, bl_ref, wd_hbm, bd_ref,
                perm_ref, o_ref, wgu_buf, wd_buf, wg_s, wl_s, wd_s, sem_gu, sem_d):
    i = pl.program_id(0)
    used = i < nu_ref[0]
    e = be_ref[i]
    new_expert = (i == 0) | (e != be_ref[jnp.maximum(i - 1, 0)])

    def fetch(expert, s):
        return (pltpu.make_async_copy(wgu_hbm.at[expert], wgu_buf.at[s], sem_gu.at[s]),
                pltpu.make_async_copy(wd_hbm.at[expert], wd_buf.at[s], sem_d.at[s]))

    @pl.when(used & new_expert)
    def _():
        s = slot_ref[e]

        @pl.when(i == 0)
        def _():
            for cp in fetch(e, s):
                cp.start()

        @pl.when(next_ref[e] >= 0)
        def _():
            for cp in fetch(next_ref[e], 1 - s):
                cp.start()

        for cp in fetch(e, s):
            cp.wait()
        for c in range(2 * D_FF // 256):
            r = _dot(wgu_buf[s, :, 256 * c:256 * (c + 1)].astype(BF16), perm_ref[...])
            wg_s[:, 128 * c:128 * (c + 1)] = r[:, :128].astype(BF16)
            wl_s[:, 128 * c:128 * (c + 1)] = r[:, 128:].astype(BF16)
        wd_s[...] = wd_buf[s].astype(BF16)

    def expert_rows(rows):
        x = _from_row_tiles(x_ref, rows).astype(BF16)
        glu = jnp.minimum(_dot(x, wg_s[...]) + bg_ref[...], SWIGLU_LIMIT)
        lin = jnp.clip(_dot(x, wl_s[...]) + bl_ref[...], -SWIGLU_LIMIT, SWIGLU_LIMIT)
        act = glu * jax.nn.sigmoid(SWIGLU_ALPHA * glu) * (lin + 1.0)
        _to_row_tiles(o_ref, _dot(act.astype(BF16), wd_s[...]) + bd_ref[...])

    def zero_rows(first, rows):
        o_ref[pl.ds(first * SUBLANES, rows * SUBLANES), :] = jnp.zeros((rows * SUBLANES, 128), F32)

    half = MOE_TILE // 2

    @pl.when(used & (nv_ref[i] > half))
    def _():
        expert_rows(MOE_TILE)

    @pl.when(used & (nv_ref[i] <= half))
    def _():
        expert_rows(half)
        zero_rows(half, half)


def _moe(block_e, n_used, n_valid, next_e, e_slot, x_rows, wgu, bg, bl, wd, bd, perm):
    nblk = x_rows.shape[0] // (MOE_TILE * SUBLANES)
    row = lambda i, be, nu, *_: (jnp.minimum(i, nu[0] - 1), 0)
    wsel = lambda i, be, *_: (be[i], 0, 0)
    grid_spec = pltpu.PrefetchScalarGridSpec(
        num_scalar_prefetch=5,
        grid=(nblk,),
        in_specs=[pl.BlockSpec((MOE_TILE * SUBLANES, 128), row),
                  pl.BlockSpec(memory_space=pl.ANY),
                  pl.BlockSpec((None, 1, D_FF), wsel), pl.BlockSpec((None, 1, D_FF), wsel),
                  pl.BlockSpec(memory_space=pl.ANY), pl.BlockSpec((None, 1, D_MODEL), wsel),
                  pl.BlockSpec((256, 256), lambda i, *_: (0, 0))],
        out_specs=pl.BlockSpec((MOE_TILE * SUBLANES, 128), row),
        scratch_shapes=[pltpu.VMEM((2, D_MODEL, 2 * D_FF), F32), pltpu.VMEM((2, D_FF, D_MODEL), F32),
                        pltpu.VMEM((D_MODEL, D_FF), BF16), pltpu.VMEM((D_MODEL, D_FF), BF16),
                        pltpu.VMEM((D_FF, D_MODEL), BF16),
                        pltpu.SemaphoreType.DMA((2,)), pltpu.SemaphoreType.DMA((2,))],
    )
    return pl.pallas_call(
        _moe_kernel,
        grid_spec=grid_spec,
        out_shape=jax.ShapeDtypeStruct(x_rows.shape, F32),
        input_output_aliases={5: 0},
        compiler_params=_cparams("arbitrary"),
        name="moe",
    )(block_e, n_used, n_valid, next_e, e_slot, x_rows, wgu, bg, bl, wd, bd, perm)


def _combine_kernel(src_tbl, cnt_tbl, loc_tbl, off_ref, gate_ref, rows_hbm, x1_ref, o_ref, buf, ybuf, sem, *,
                    tile_base):
    i = pl.program_id(0)
    n = pl.num_programs(0)

    def slot_rows(s):
        return buf.at[pl.ds(pl.multiple_of(s * _SLOT_ROWS, _SLOT_ROWS), _SLOT_ROWS), :]

    def fetch(tile, s):
        _run_copies(tile_base + tile, src_tbl, cnt_tbl, loc_tbl,
                    lambda g, l, size: pltpu.make_async_copy(_tile_span(rows_hbm, g, size),
                                                             _tile_span(slot_rows(s), l, size), sem.at[s]))

    @pl.when(i == 0)
    def _():
        fetch(0, 0)

    @pl.when(i + 1 < n)
    def _():
        fetch(i + 1, (i + 1) % 2)

    slot = i % 2
    pltpu.make_async_copy(slot_rows(slot), slot_rows(slot), sem.at[slot]).wait()

    def body(t, c):
        acc = None
        for k in range(TOP_K):
            off = pl.multiple_of(off_ref[0, k * ROUTE_TILE + t], SUBLANES)
            v = gate_ref[0, k * ROUTE_TILE + t] * buf[pl.ds(off, SUBLANES), :]
            acc = v if acc is None else acc + v
        ybuf[pl.ds(pl.multiple_of(t * SUBLANES, SUBLANES), SUBLANES), :] = acc
        return c

    lax.fori_loop(0, ROUTE_TILE, body, 0, unroll=8)
    o_ref[...] = x1_ref[...] + _from_row_tiles(ybuf, ROUTE_TILE)


def _combine(runs, off, gates, rows, x1, tile_base):
    assert tile_base % 2 == 0
    t = x1.shape[0]
    tile_rows = ROUTE_TILE * SUBLANES
    smem_tile = pl.BlockSpec((None, 1, TOP_K * ROUTE_TILE), lambda i, *_: (tile_base + i, 0, 0),
                             memory_space=pltpu.SMEM)
    grid_spec = pltpu.PrefetchScalarGridSpec(
        num_scalar_prefetch=3,
        grid=(t // ROUTE_TILE,),
        in_specs=[smem_tile, smem_tile, pl.BlockSpec(memory_space=pl.ANY),
                  pl.BlockSpec((ROUTE_TILE, D_MODEL), lambda i, *_: (i, 0))],
        out_specs=pl.BlockSpec((ROUTE_TILE, D_MODEL), lambda i, *_: (i, 0)),
        scratch_shapes=[pltpu.VMEM((2 * _SLOT_ROWS, 128), F32), pltpu.VMEM((tile_rows, 128), F32),
                        pltpu.SemaphoreType.DMA((2,))],
    )
    return pl.pallas_call(
        functools.partial(_combine_kernel, tile_base=tile_base),
        grid_spec=grid_spec,
        out_shape=jax.ShapeDtypeStruct((t, D_MODEL), F32),
        compiler_params=_cparams("arbitrary"),
        name="combine",
    )(*runs, off, gates, rows, x1)


def _route(before, cnt, loc, n_tokens):
    n_assign = n_tokens * TOP_K
    table = lambda a: a[:, :, 0].astype(jnp.int32)
    before, cnt, run_loc = table(before), table(cnt), table(loc)
    counts = before[-1] + cnt[-1]
    padded = ((counts + MOE_TILE - 1) // MOE_TILE) * MOE_TILE
    pend = jnp.cumsum(padded)
    pstart = pend - padded
    run_src = pstart[None, :] + before
    nblk = (n_assign + MOE_TILE - 1) // MOE_TILE + N_EXPERTS
    n_used = (pend[-1] // MOE_TILE).astype(jnp.int32)
    block_start = jnp.arange(nblk, dtype=jnp.int32) * MOE_TILE
    block_e = jnp.sum(pend[None, :] <= jnp.minimum(block_start, pend[-1] - 1)[:, None], axis=1)
    block_e = jnp.minimum(block_e, N_EXPERTS - 1).astype(jnp.int32)
    region_end = jnp.sum(jnp.where(block_e[:, None] == jnp.arange(N_EXPERTS)[None, :], (pstart + counts)[None, :], 0),
                         axis=1)
    n_valid = jnp.clip(region_end - block_start, 0, MOE_TILE).astype(jnp.int32)
    ids = jnp.arange(N_EXPERTS, dtype=jnp.int32)
    nonempty = counts > 0
    next_e = jnp.min(jnp.where((ids[None, :] > ids[:, None]) & nonempty[None, :], ids[None, :], N_EXPERTS), axis=1)
    next_e = jnp.where(next_e == N_EXPERTS, -1, next_e).astype(jnp.int32)
    e_slot = ((jnp.cumsum(nonempty.astype(jnp.int32)) - nonempty.astype(jnp.int32)) % 2).astype(jnp.int32)
    blocks = (block_e, n_used.reshape(1), n_valid, next_e, e_slot)
    runs = (run_src.reshape(-1), cnt.reshape(-1), run_loc.reshape(-1))
    return runs, blocks, pend, padded, nblk * MOE_TILE


def kernel(x_prompt, x_sample, cache_k, cache_v, state_ssm_re, state_ssm_im, norm_mix_g, w_in, q_norm_g, k_norm_g, attn_sinks, ssm_lambda_re, ssm_lambda_im, ssm_b_re, ssm_b_im, ssm_c_re, ssm_c_im, ssm_d, ssm_log_dt, w_glu, b_glu, attn_out_norm_g, ssm_out_norm_g, w_out, norm_ffn_g, w_router, b_router, w_gate_up, b_gate_up, w_down, b_down):
    depth = w_in.shape[0]
    assert depth == 1
    bp, sp, _ = x_prompt.shape
    bs, ss, _ = x_sample.shape
    tp, ts = bp * sp, bs * ss
    assert bp == SUBLANES and sp % S5_CHUNK == 0 and bs % SUBLANES == 0

    perm = jnp.asarray(_Q_PERM)
    w_in0 = w_in[0]
    wq = w_in0[:, :D_ATTN].reshape(D_MODEL, N_HEADS, HEAD_DIM)[:, perm].reshape(D_MODEL, D_ATTN)
    w_in_b = jnp.concatenate([wq, w_in0[:, D_ATTN:]], axis=1).astype(BF16)
    qkg = jnp.concatenate([jnp.tile(q_norm_g[0], N_HEADS), jnp.tile(k_norm_g[0], N_KV_HEADS)])[None]
    pmat = jnp.kron(jnp.eye(256 // HEAD_DIM, dtype=F32),
                    jnp.full((HEAD_DIM, HEAD_DIM), 1.0 / HEAD_DIM, F32)).astype(BF16)
    g_mix = norm_mix_g[0][None]
    sinks = attn_sinks[0]
    g_attn = attn_out_norm_g[0].reshape(N_HEADS, HEAD_DIM)[perm].reshape(1, D_ATTN)
    w_out0 = w_out[0]
    w_out_a = w_out0[:D_ATTN].reshape(N_HEADS, HEAD_DIM, D_MODEL)[perm].reshape(D_ATTN, D_MODEL).astype(BF16)
    w_out_s = w_out0[D_ATTN:].astype(BF16)
    g_ssm = ssm_out_norm_g[0][None]
    g_ffn = norm_ffn_g[0][None]
    w_rt = w_router[0].T.astype(BF16)
    b_rt = jnp.broadcast_to(b_router[0][:, None], (N_EXPERTS, ROUTE_TILE))

    a_re, a_im, bb_re, bb_im = _s5_prep(
        ssm_lambda_re[0], ssm_lambda_im[0], ssm_log_dt[0][:, None],
        jnp.swapaxes(ssm_b_re[0], 1, 2), jnp.swapaxes(ssm_b_im[0], 1, 2))
    chan_g = jnp.arange(D_SSM)[:, None] // SSM_GROUP
    state_g = jnp.arange(D_STATE)[None, :] // SSM_STATE
    bd_b = lambda bb: jnp.where(chan_g == state_g, jnp.tile(bb.reshape(D_SSM, SSM_STATE), (1, N_SSM_GROUPS)), 0.0)
    wb = jnp.concatenate([bd_b(bb_re), bd_b(bb_im)], axis=1).astype(BF16)
    bd_c = lambda c: jnp.where(state_g.T == chan_g.T,
                               jnp.tile(jnp.swapaxes(c, 1, 2).reshape(D_STATE, SSM_GROUP), (1, N_SSM_GROUPS)), 0.0)
    wc = jnp.concatenate([bd_c(ssm_c_re[0]), -bd_c(ssm_c_im[0])], axis=0).astype(BF16)
    a_re, a_im = a_re.reshape(1, D_STATE), a_im.reshape(1, D_STATE)
    d_skip = ssm_d[0].reshape(1, D_SSM)
    w_glu_b = w_glu[0].astype(BF16)
    b_glu0 = b_glu[0][None]

    b_g = b_gate_up[0][:, None, 0::2]
    b_l = b_gate_up[0][:, None, 1::2]
    b_d = b_down[0][:, None, :]
    idx = jnp.arange(256)
    deint = (idx[None, :] == jnp.where(idx % 2 == 0, idx // 2, 128 + idx // 2)[:, None]).astype(BF16)

    xp2 = x_prompt.reshape(tp, D_MODEL)
    xs2 = x_sample.reshape(ts, D_MODEL)
    qp, kp, vp, up = _in_proj(xp2, g_mix, w_in_b, qkg, pmat)
    qs, ks, vs, us = _in_proj(xs2, g_mix, w_in_b, qkg, pmat)

    kp3, vp3 = kp.reshape(bp, sp, D_KV), vp.reshape(bp, sp, D_KV)
    attn_p = _band_attention(sinks, qp.reshape(bp, sp, D_ATTN), kp3, vp3, g_attn).reshape(tp, D_ATTN)
    ck = cache_k[0].reshape(bs, -1, D_KV)
    cv = cache_v[0].reshape(bs, -1, D_KV)
    attn_s, new_ck, new_cv = _cache_attention(sinks, qs, ks, vs, ck, cv, g_attn, ss)

    zeros_p = jnp.zeros((bp, D_STATE), F32)
    s5_args = (a_re, a_im, wb, wc, d_skip, w_glu_b, b_glu0, g_ssm)
    ssm_p, hr_p, hi_p = _s5(up.reshape(bp, sp, D_SSM), zeros_p, zeros_p, *s5_args,
                            tt=S5_CHUNK, time_chunked=True)
    ssm_s, hr_s, hi_s = _s5(us.reshape(bs // SUBLANES, SUBLANES * ss, D_SSM),
                            state_ssm_re[0].reshape(bs, D_STATE), state_ssm_im[0].reshape(bs, D_STATE),
                            *s5_args, tt=ss, time_chunked=False)

    triu = (jnp.arange(ROUTE_TILE)[:, None] < jnp.arange(ROUTE_TILE)[None, :]).astype(BF16)
    low = (jnp.arange(N_EXPERTS)[:, None] > jnp.arange(N_EXPERTS)[None, :]).astype(BF16)
    proj_args = (w_out_a, w_out_s, g_ffn, w_rt, b_rt, triu, low)
    x1p, *tab_p, counts_p = _out_proj(attn_p, ssm_p.reshape(tp, D_SSM), xp2, *proj_args,
                                      jnp.zeros((N_EXPERTS, 128), F32), 0)
    x1s, *tab_s, _ = _out_proj(attn_s, ssm_s.reshape(ts, D_SSM), xs2, *proj_args, counts_p, tp // ROUTE_TILE)

    gates, off, before, cnt, loc = (jnp.concatenate([p, s_], axis=0) for p, s_ in zip(tab_p, tab_s))
    runs, blocks, pend, padded, n_rows = _route(before, cnt, loc, tp + ts)
    x_rows = _dispatch(runs, pend, padded, blocks[1], off, x1p, x1s, g_ffn, n_rows)
    out_rows = _moe(*blocks, x_rows, w_gate_up[0], b_g, b_l, w_down[0], b_d, deint)
    yp = _combine(runs, off, gates, out_rows, x1p, 0).reshape(bp, sp, D_MODEL)
    ys = _combine(runs, off, gates, out_rows, x1s, tp // ROUTE_TILE).reshape(bs, ss, D_MODEL)

    kv5 = lambda a, b_: a.reshape(b_, -1, N_KV_HEADS, HEAD_DIM)
    new_kp = kv5(kp3[:, -WINDOW:], bp)[None]
    new_vp = kv5(vp3[:, -WINDOW:], bp)[None]
    new_ks = kv5(new_ck, bs)[None]
    new_vs = kv5(new_cv, bs)[None]
    st = lambda h, b_: h.reshape(1, b_, N_SSM_GROUPS, SSM_STATE)
    return (yp, ys, new_kp, new_vp, st(hr_p, bp), st(hi_p, bp),
            new_ks, new_vs, st(hr_s, bs), st(hi_s, bs))
```

```python
import functools
import math

import jax
import jax.numpy as jnp
from jax import lax
from jax.experimental import pallas as pl
from jax.experimental.pallas import tpu as pltpu

F32 = jnp.float32
BF16 = jnp.bfloat16

D_MODEL = 1024
D_ATTN = 512
D_SSM = 512
HEAD_DIM = 64
N_HEADS = 8
N_KV_HEADS = 2
D_KV = N_KV_HEADS * HEAD_DIM
WINDOW = 128
SSM_GROUP = 16
N_SSM_GROUPS = 32
SSM_STATE = 64
D_STATE = N_SSM_GROUPS * SSM_STATE
N_EXPERTS = 32
TOP_K = 4
D_FF = 1024
SWIGLU_LIMIT = 7.0
SWIGLU_ALPHA = 1.702
RMS_EPS = 1e-6
NEG_INF = -1e30
D_IN_PROJ = D_ATTN + 2 * D_KV + D_SSM
D_QK = D_ATTN + D_KV

SUBLANES = 8
VMEM_LIMIT = 56 * 1024 * 1024

ROW_TILE = 2048
MOE_TILE = 512
ATTN_BLOCKS = 8
ROUTE_TILE = 512
_SLOT_ROWS = ROUTE_TILE * TOP_K * SUBLANES
_RUN_SIZES = tuple(1 << b for b in range(ROUTE_TILE.bit_length() - 1, -1, -1))
S5_CHUNK = 128
S5_COLS = 512

_Q_PERM = (0, 4, 1, 5, 2, 6, 3, 7)


def _cparams(*sem):
    return pltpu.CompilerParams(dimension_semantics=sem, vmem_limit_bytes=VMEM_LIMIT)


def _dot(a, b):
    return jnp.dot(a, b, preferred_element_type=F32)


def _rms(x):
    return x * lax.rsqrt(jnp.mean(x * x, axis=-1, keepdims=True) + RMS_EPS)


def _in_proj_kernel(x_ref, g_ref, w_ref, qkg_ref, p_ref, q_ref, k_ref, v_ref, u_ref):
    xn = _rms(x_ref[...]) * g_ref[...]
    h = _dot(xn.astype(BF16), w_ref[...])
    qk = h[:, :D_QK]
    sq = (qk * qk).astype(BF16)
    p = p_ref[...]
    ms = jnp.concatenate(
        [_dot(sq[:, 0:256], p), _dot(sq[:, 256:512], p), _dot(sq[:, 512:640], p[:128, :128])],
        axis=-1)
    qkn = qk * lax.rsqrt(ms + RMS_EPS) * qkg_ref[...]
    q_ref[...] = qkn[:, :D_ATTN].astype(BF16)
    k_ref[...] = qkn[:, D_ATTN:]
    v_ref[...] = h[:, D_QK:D_QK + D_KV]
    u_ref[...] = h[:, D_QK + D_KV:]


def _in_proj(x2d, g, w, qkg, pmat):
    t = x2d.shape[0]
    tm = min(ROW_TILE, t)
    row = lambda i: (i, 0)
    fix = lambda i: (0, 0)
    return pl.pallas_call(
        _in_proj_kernel,
        grid=(t // tm,),
        in_specs=[pl.BlockSpec((tm, D_MODEL), row), pl.BlockSpec((1, D_MODEL), fix),
                  pl.BlockSpec((D_MODEL, D_IN_PROJ), fix), pl.BlockSpec((1, D_QK), fix),
                  pl.BlockSpec((256, 256), fix)],
        out_specs=[pl.BlockSpec((tm, D_ATTN), row), pl.BlockSpec((tm, D_KV), row),
                   pl.BlockSpec((tm, D_KV), row), pl.BlockSpec((tm, D_SSM), row)],
        out_shape=[jax.ShapeDtypeStruct((t, D_ATTN), BF16), jax.ShapeDtypeStruct((t, D_KV), F32),
                   jax.ShapeDtypeStruct((t, D_KV), F32), jax.ShapeDtypeStruct((t, D_SSM), F32)],
        compiler_params=_cparams("parallel"),
        name="in_proj",
    )(x2d, g, w, qkg, pmat)


def _softmax_pv(s_blocks, v_blocks, sink):
    m = sink
    for s in s_blocks:
        m = jnp.maximum(m, jnp.max(s, axis=-1, keepdims=True))
    den = jnp.exp(sink - m)
    acc = None
    for s, v in zip(s_blocks, v_blocks):
        p = jnp.exp(s - m)
        den = den + jnp.sum(p, axis=-1, keepdims=True)
        pv = _dot(p.astype(BF16), v)
        acc = pv if acc is None else acc + pv
    return acc / den


def _band_attn_kernel(sink_ref, q_ref, kp_ref, kc_ref, vp_ref, vc_ref, g_ref, o_ref):
    i = pl.program_id(1)
    row = lax.broadcasted_iota(jnp.int32, (WINDOW, 2 * WINDOW), 0)
    col = lax.broadcasted_iota(jnp.int32, (WINDOW, 2 * WINDOW), 1)
    band = (col > row) & (col <= row + WINDOW)
    lane = lax.broadcasted_iota(jnp.int32, (WINDOW, 128), 1)
    low = lane < HEAD_DIM
    zero = jnp.zeros((), BF16)
    for j in range(ATTN_BLOCKS):
        cur = slice(WINDOW * j, WINDOW * (j + 1))
        q = q_ref[cur, :]
        k_prev = kp_ref[...] if j == 0 else kc_ref[WINDOW * (j - 1):WINDOW * j, :]
        v_prev = vp_ref[...] if j == 0 else vc_ref[WINDOW * (j - 1):WINDOW * j, :]
        kb = jnp.concatenate([k_prev, kc_ref[cur, :]], axis=0).astype(BF16)
        vb = jnp.concatenate([v_prev, vc_ref[cur, :]], axis=0).astype(BF16)
        mask = band & ((col >= WINDOW) | (i > 0)) if j == 0 else band
        outs = []
        for pair in range(N_HEADS // 2):
            qp = q[:, 128 * pair:128 * (pair + 1)]
            halves = []
            for par in range(2):
                qm = jnp.where(low if par == 0 else ~low, qp, zero)
                s = lax.dot_general(qm, kb, (((1,), (1,)), ((), ())), preferred_element_type=F32)
                s = jnp.where(mask, s * (HEAD_DIM ** -0.5), NEG_INF)
                halves.append(_softmax_pv([s], [vb], sink_ref[_Q_PERM[2 * pair + par]]))
            outs.append(jnp.where(low, halves[0], halves[1]))
        o = jnp.concatenate(outs, axis=-1)
        o_ref[cur, :] = (_rms(o) * g_ref[...]).astype(BF16)


def _band_attention(sinks, q, k, v, g):
    b, s, _ = q.shape
    qb = WINDOW * ATTN_BLOCKS
    cur = lambda bi, i: (bi, i, 0)
    prev = lambda bi, i: (bi, jnp.maximum(i * ATTN_BLOCKS - 1, 0), 0)
    return pl.pallas_call(
        _band_attn_kernel,
        grid=(b, s // qb),
        in_specs=[pl.BlockSpec(memory_space=pltpu.SMEM),
                  pl.BlockSpec((None, qb, D_ATTN), cur),
                  pl.BlockSpec((None, WINDOW, D_KV), prev), pl.BlockSpec((None, qb, D_KV), cur),
                  pl.BlockSpec((None, WINDOW, D_KV), prev), pl.BlockSpec((None, qb, D_KV), cur),
                  pl.BlockSpec((1, D_ATTN), lambda bi, i: (0, 0))],
        out_specs=pl.BlockSpec((None, qb, D_ATTN), cur),
        out_shape=jax.ShapeDtypeStruct((b, s, D_ATTN), BF16),
        compiler_params=_cparams("parallel", "parallel"),
        name="band_attn",
    )(sinks, q, k, k, v, v, g)


_PAIR_ROWS = 8
_CACHE_BB = 16


def _cache_attn_kernel(sink_ref, q_ref, kn_ref, vn_ref, ck_ref, cv_ref, g_ref, o_ref, nk_ref, nv_ref, *, n_new):
    n_buf = ck_ref.shape[1]
    rows_blk = _CACHE_BB * n_new
    for b in range(_CACHE_BB):
        for new_ref, cache_ref, fresh_ref in ((nk_ref, ck_ref, kn_ref), (nv_ref, cv_ref, vn_ref)):
            new_ref[b, 0:n_buf - n_new, :] = cache_ref[b, n_new:n_buf, :]
            new_ref[b, n_buf - n_new:n_buf, :] = fresh_ref[n_new * b:n_new * (b + 1), :]
    knew = kn_ref[...].astype(BF16)
    vnew = vn_ref[...].astype(BF16)
    lane = lax.broadcasted_iota(jnp.int32, (_PAIR_ROWS, 128), 1)
    low = lane < HEAD_DIM
    zero = jnp.zeros((), BF16)
    n_stack = N_HEADS * _PAIR_ROWS
    r = lax.broadcasted_iota(jnp.int32, (n_stack, 1), 0) % _PAIR_ROWS
    r_seq, r_tok = r // n_new, r % n_new
    colc = lax.broadcasted_iota(jnp.int32, (n_stack, 2 * n_buf), 1)
    c_seq, c_pos = colc // n_buf, colc % n_buf
    mask_c = (c_seq == r_seq) & (c_pos + WINDOW > r_tok + n_buf)
    coln = lax.broadcasted_iota(jnp.int32, (n_stack, rows_blk), 1)
    sink_col = jnp.concatenate(
        [jnp.full((_PAIR_ROWS, 1), sink_ref[_Q_PERM[h]], F32) for h in range(N_HEADS)], axis=0)
    for sp in range(_CACHE_BB // 2):
        q = q_ref[_PAIR_ROWS * sp:_PAIR_ROWS * (sp + 1), :]
        pieces = []
        for pair in range(N_HEADS // 2):
            qp = q[:, 128 * pair:128 * (pair + 1)]
            pieces.append(jnp.where(low, qp, zero))
            pieces.append(jnp.where(low, zero, qp))
        qs = jnp.concatenate(pieces, axis=0)
        kc = jnp.concatenate([ck_ref[2 * sp], ck_ref[2 * sp + 1]], axis=0).astype(BF16)
        vc = jnp.concatenate([cv_ref[2 * sp], cv_ref[2 * sp + 1]], axis=0).astype(BF16)
        nt = (((1,), (1,)), ((), ()))
        s_c = lax.dot_general(qs, kc, nt, preferred_element_type=F32) * (HEAD_DIM ** -0.5)
        s_n = lax.dot_general(qs, knew, nt, preferred_element_type=F32) * (HEAD_DIM ** -0.5)
        s_c = jnp.where(mask_c, s_c, NEG_INF)
        n_seq, n_tok = coln // n_new - 2 * sp, coln % n_new
        mask_n = (n_seq == r_seq) & (n_tok <= r_tok)
        s_n = jnp.where(mask_n, s_n, NEG_INF)
        o = _softmax_pv([s_c, s_n], [vc, vnew], sink_col)
        outs = [jnp.where(low, o[16 * pair:16 * pair + 8], o[16 * pair + 8:16 * pair + 16])
                for pair in range(N_HEADS // 2)]
        oo = jnp.concatenate(outs, axis=-1)
        o_ref[_PAIR_ROWS * sp:_PAIR_ROWS * (sp + 1), :] = (_rms(oo) * g_ref[...]).astype(BF16)


def _cache_attention(sinks, q, k, v, cache_k, cache_v, g, n_new):
    t = q.shape[0]
    nb, n_buf, _ = cache_k.shape
    assert n_new * 2 == _PAIR_ROWS and nb % _CACHE_BB == 0 and n_buf == WINDOW
    rows = _CACHE_BB * n_new
    row = lambda i: (i, 0)
    cache_spec = pl.BlockSpec((_CACHE_BB, n_buf, D_KV), lambda i: (i, 0, 0))
    return pl.pallas_call(
        functools.partial(_cache_attn_kernel, n_new=n_new),
        grid=(nb // _CACHE_BB,),
        in_specs=[pl.BlockSpec(memory_space=pltpu.SMEM),
                  pl.BlockSpec((rows, D_ATTN), row), pl.BlockSpec((rows, D_KV), row),
                  pl.BlockSpec((rows, D_KV), row), cache_spec, cache_spec,
                  pl.BlockSpec((1, D_ATTN), lambda i: (0, 0))],
        out_specs=[pl.BlockSpec((rows, D_ATTN), row), cache_spec, cache_spec],
        out_shape=[jax.ShapeDtypeStruct((t, D_ATTN), BF16), jax.ShapeDtypeStruct(cache_k.shape, F32),
                   jax.ShapeDtypeStruct(cache_v.shape, F32)],
        compiler_params=_cparams("parallel"),
        name="cache_attn",
    )(sinks, q, k, v, cache_k, cache_v, g)


def _s5_prep_kernel(lre_ref, lim_ref, ldt_ref, bre_ref, bim_ref, are_ref, aim_ref, bbre_ref, bbim_ref):
    dt = jnp.exp(ldt_ref[...])
    l_re = jnp.minimum(lre_ref[...], -1e-4)
    l_im = lim_ref[...]
    mag = jnp.exp(l_re * dt)
    a_re = mag * jnp.cos(l_im * dt)
    a_im = mag * jnp.sin(l_im * dt)
    den = l_re * l_re + l_im * l_im
    n_re = a_re - 1.0
    z_re = (n_re * l_re + a_im * l_im) / den
    z_im = (a_im * l_re - n_re * l_im) / den
    are_ref[...] = a_re
    aim_ref[...] = a_im
    br, bi = bre_ref[...], bim_ref[...]
    zr, zi = z_re[:, None, :], z_im[:, None, :]
    bbre_ref[...] = zr * br - zi * bi
    bbim_ref[...] = zr * bi + zi * br


def _s5_prep(lam_re, lam_im, log_dt, b_re_t, b_im_t):
    g, p = lam_re.shape
    sd = jax.ShapeDtypeStruct
    return pl.pallas_call(
        _s5_prep_kernel,
        out_shape=[sd((g, p), F32), sd((g, p), F32), sd(b_re_t.shape, F32), sd(b_re_t.shape, F32)],
        name="s5_prep",
    )(lam_re, lam_im, log_dt, b_re_t, b_im_t)


def _s5_kernel(u_ref, h0r_ref, h0i_ref, ar_ref, ai_ref, wb_ref, wc_ref, d_ref, wglu_ref, bglu_ref,
               g_ref, o_ref, hr_ref, hi_ref, bu_ref, hs_ref, usc_ref, ysc_ref, *, tt):
    j = pl.program_id(1)
    rows = SUBLANES * tt
    n_tiles = D_STATE // 128
    time_major = tt % SUBLANES == 0

    @pl.when(j == 0)
    def _():
        hs_ref[:, :D_STATE] = h0r_ref[...]
        hs_ref[:, D_STATE:] = h0i_ref[...]

    u = u_ref[...].reshape(rows, D_SSM)
    if time_major:
        for c in range(D_SSM // 128):
            for b in range(SUBLANES):
                usc_ref[c, pl.ds(b, tt, stride=SUBLANES), :] = u[b * tt:(b + 1) * tt, 128 * c:128 * (c + 1)]
        ub = jnp.concatenate([usc_ref[c] for c in range(D_SSM // 128)], axis=-1).astype(BF16)
    else:
        ub = u.astype(BF16)

    def step_rows(t):
        if time_major:
            return pl.ds(pl.multiple_of(t * SUBLANES, SUBLANES), SUBLANES)
        return pl.ds(t, SUBLANES, stride=tt)

    for n in range(2 * D_STATE // 256):
        band = (n % (D_STATE // 256)) // 2
        res = _dot(ub[:, 128 * band:128 * (band + 1)],
                   wb_ref[128 * band:128 * (band + 1), 256 * n:256 * (n + 1)])
        bu_ref[2 * n] = res[:, :128]
        bu_ref[2 * n + 1] = res[:, 128:]

    tiles_per_pass = S5_COLS // 128
    for c0 in range(0, n_tiles, tiles_per_pass):
        tiles = range(c0, c0 + tiles_per_pass)
        a_r = [jnp.broadcast_to(ar_ref[:, 128 * c:128 * (c + 1)], (SUBLANES, 128)) for c in tiles]
        a_i = [jnp.broadcast_to(ai_ref[:, 128 * c:128 * (c + 1)], (SUBLANES, 128)) for c in tiles]

        def step(t, carry, tiles=tiles, a_r=a_r, a_i=a_i):
            at_t = step_rows(t)
            out = []
            for k, c in enumerate(tiles):
                h_r, h_i = carry[2 * k], carry[2 * k + 1]
                n_r = a_r[k] * h_r - a_i[k] * h_i + bu_ref[c, at_t, :]
                n_i = a_r[k] * h_i + a_i[k] * h_r + bu_ref[n_tiles + c, at_t, :]
                bu_ref[c, at_t, :] = n_r
                bu_ref[n_tiles + c, at_t, :] = n_i
                out += [n_r, n_i]
            return tuple(out)

        init = []
        for c in tiles:
            init += [hs_ref[:, 128 * c:128 * (c + 1)], hs_ref[:, D_STATE + 128 * c:D_STATE + 128 * (c + 1)]]
        fin = lax.fori_loop(0, tt, step, tuple(init), unroll=min(tt, 8))
        for k, c in enumerate(tiles):
            hs_ref[:, 128 * c:128 * (c + 1)] = fin[2 * k]
            hs_ref[:, D_STATE + 128 * c:D_STATE + 128 * (c + 1)] = fin[2 * k + 1]

    def h_cols(first_tile):
        return jnp.concatenate([bu_ref[first_tile + k] for k in range(4)], axis=-1).astype(BF16)

    ys = []
    for m in range(D_SSM // 128):
        y = _dot(h_cols(4 * m), wc_ref[512 * m:512 * (m + 1), 128 * m:128 * (m + 1)])
        y = y + _dot(h_cols(n_tiles + 4 * m),
                     wc_ref[D_STATE + 512 * m:D_STATE + 512 * (m + 1), 128 * m:128 * (m + 1)])
        if time_major:
            ysc_ref[m] = y
            y = jnp.concatenate([ysc_ref[m, pl.ds(b, tt, stride=SUBLANES), :] for b in range(SUBLANES)], axis=0)
        ys.append(y)
    y = jnp.concatenate(ys, axis=-1) + d_ref[...] * u
    z = _dot(jax.nn.gelu(y).astype(BF16), wglu_ref[...]) + bglu_ref[...]
    s = z[:, :D_SSM] * jax.nn.sigmoid(z[:, D_SSM:])
    o_ref[...] = (_rms(s) * g_ref[...]).astype(BF16).reshape(o_ref.shape)

    @pl.when(j == pl.num_programs(1) - 1)
    def _():
        hr_ref[...] = hs_ref[:, :D_STATE]
        hi_ref[...] = hs_ref[:, D_STATE:]


def _s5(u, h0r, h0i, a_re, a_im, wb, wc, d, wglu, bglu, g, *, tt, time_chunked):
    nbg = h0r.shape[0] // SUBLANES
    if time_chunked:
        nchunks = u.shape[1] // tt
        u_spec = pl.BlockSpec((SUBLANES, tt, D_SSM), lambda gi, j: (gi, j, 0))
    else:
        nchunks = 1
        u_spec = pl.BlockSpec((None, SUBLANES * tt, D_SSM), lambda gi, j: (gi, 0, 0))
    fix = lambda gi, j: (0, 0)
    st_spec = pl.BlockSpec((SUBLANES, D_STATE), lambda gi, j: (gi, 0))
    sd = jax.ShapeDtypeStruct
    return pl.pallas_call(
        functools.partial(_s5_kernel, tt=tt),
        grid=(nbg, nchunks),
        in_specs=[u_spec, st_spec, st_spec,
                  pl.BlockSpec((1, D_STATE), fix), pl.BlockSpec((1, D_STATE), fix),
                  pl.BlockSpec((D_SSM, 2 * D_STATE), fix), pl.BlockSpec((2 * D_STATE, D_SSM), fix),
                  pl.BlockSpec((1, D_SSM), fix), pl.BlockSpec((D_SSM, 2 * D_SSM), fix),
                  pl.BlockSpec((1, 2 * D_SSM), fix), pl.BlockSpec((1, D_SSM), fix)],
        out_specs=[u_spec, st_spec, st_spec],
        out_shape=[sd(u.shape, BF16), sd(h0r.shape, F32), sd(h0r.shape, F32)],
        scratch_shapes=[pltpu.VMEM((2 * D_STATE // 128, SUBLANES * tt, 128), F32),
                        pltpu.VMEM((SUBLANES, 2 * D_STATE), F32),
                        pltpu.VMEM((D_SSM // 128, SUBLANES * tt, 128), F32),
                        pltpu.VMEM((D_SSM // 128, SUBLANES * tt, 128), F32)],
        compiler_params=_cparams("parallel", "arbitrary"),
        name="s5",
    )(u, h0r, h0i, a_re, a_im, wb, wc, d, wglu, bglu, g)


def _to_row_tiles(ref, val, first=0):
    rows = val.shape[0]
    for c in range(D_MODEL // 128):
        ref[pl.ds(first * SUBLANES + c, rows, stride=SUBLANES), :] = val[:, 128 * c:128 * (c + 1)]


def _from_row_tiles(ref, rows, lead=(), first=0):
    return jnp.concatenate(
        [ref[(*lead, pl.ds(first * SUBLANES + c, rows, stride=SUBLANES), slice(None))]
         for c in range(D_MODEL // 128)], axis=-1)


def _route_tile(logits_t, tile, triu_ref, low_ref, carry, gate_ref, off_ref, before_ref, cnt_ref, loc_ref):
    l = logits_t
    expert = lax.broadcasted_iota(jnp.int32, l.shape, 0).astype(F32)
    vals, sels = [], []
    for _ in range(TOP_K):
        m = jnp.max(l, axis=0, keepdims=True)
        idx = jnp.min(jnp.where(l == m, expert, float(N_EXPERTS)), axis=0, keepdims=True)
        sel = expert == idx
        l = jnp.where(sel, -jnp.inf, l)
        vals.append(m)
        sels.append(sel)
    exps = [jnp.exp(v - vals[0]) for v in vals]
    den = exps[0] + exps[1] + exps[2] + exps[3]
    onehot = jnp.where(sels[0] | sels[1] | sels[2] | sels[3], 1.0, 0.0)
    within = _dot(onehot.astype(BF16), triu_ref[...])
    cnt = jnp.broadcast_to(jnp.sum(onehot, axis=1, keepdims=True), onehot.shape)
    cnt_hi = jnp.floor(cnt * (1.0 / 16.0))
    cnt_lo = cnt - 16.0 * cnt_hi
    loc = 16.0 * _dot(low_ref[...], cnt_hi.astype(BF16)) + _dot(low_ref[...], cnt_lo.astype(BF16))
    slot_base = ((tile % 2) * (ROUTE_TILE * TOP_K)).astype(F32)
    rows = [jnp.sum(jnp.where(s, within + loc, 0.0), axis=0, keepdims=True) for s in sels]
    gate_ref[...] = jnp.concatenate([e / den for e in exps], axis=1)
    off_ref[...] = ((jnp.concatenate(rows, axis=1) + slot_base) * SUBLANES).astype(jnp.int32)
    before_ref[...] = carry[...]
    cnt_ref[...] = cnt[:, :128]
    loc_ref[...] = loc[:, :128]
    carry[...] = carry[...] + cnt[:, :128]


def _out_proj_kernel(a_ref, s_ref, x_ref, wa_ref, ws_ref, g_ref, wrt_ref, brt_ref, triu_ref, low_ref, cin_ref,
                     x1_ref, gate_ref, off_ref, before_ref, cnt_ref, loc_ref, cout_ref, carry, *, tile_base):
    i = pl.program_id(0)

    @pl.when(i == 0)
    def _():
        carry[...] = cin_ref[...]

    x1 = x_ref[...] + _dot(a_ref[...], wa_ref[...]) + _dot(s_ref[...], ws_ref[...])
    x1_ref[...] = x1
    xn = _rms(x1) * g_ref[...]
    logits_t = lax.dot_general(wrt_ref[...], xn.astype(BF16), (((1,), (1,)), ((), ())),
                               preferred_element_type=F32) + brt_ref[...]
    _route_tile(logits_t, tile_base + i, triu_ref, low_ref, carry, gate_ref, off_ref, before_ref, cnt_ref, loc_ref)
    cout_ref[...] = carry[...]


def _out_proj(attn_n, ssm_n, x2d, wa, ws, g, wrt, brt, triu, low, counts_in, tile_base):
    t = x2d.shape[0]
    tm = ROUTE_TILE
    n_tiles = t // tm
    row = lambda i: (i, 0)
    fix = lambda i: (0, 0)
    sd = jax.ShapeDtypeStruct
    k_spec = pl.BlockSpec((None, 1, TOP_K * tm), lambda i: (i, 0, 0))
    t_spec = pl.BlockSpec((None, N_EXPERTS, 128), lambda i: (i, 0, 0))
    return pl.pallas_call(
        functools.partial(_out_proj_kernel, tile_base=tile_base),
        grid=(n_tiles,),
        in_specs=[pl.BlockSpec((tm, D_ATTN), row), pl.BlockSpec((tm, D_SSM), row),
                  pl.BlockSpec((tm, D_MODEL), row),
                  pl.BlockSpec((D_ATTN, D_MODEL), fix), pl.BlockSpec((D_SSM, D_MODEL), fix),
                  pl.BlockSpec((1, D_MODEL), fix), pl.BlockSpec((N_EXPERTS, D_MODEL), fix),
                  pl.BlockSpec((N_EXPERTS, tm), fix), pl.BlockSpec((tm, tm), fix),
                  pl.BlockSpec((N_EXPERTS, N_EXPERTS), fix), pl.BlockSpec((N_EXPERTS, 128), fix)],
        out_specs=[pl.BlockSpec((tm, D_MODEL), row), k_spec, k_spec, t_spec, t_spec, t_spec,
                   pl.BlockSpec((N_EXPERTS, 128), fix)],
        out_shape=[sd((t, D_MODEL), F32), sd((n_tiles, 1, TOP_K * tm), F32), sd((n_tiles, 1, TOP_K * tm), jnp.int32),
                   sd((n_tiles, N_EXPERTS, 128), F32), sd((n_tiles, N_EXPERTS, 128), F32),
                   sd((n_tiles, N_EXPERTS, 128), F32), sd((N_EXPERTS, 128), F32)],
        scratch_shapes=[pltpu.VMEM((N_EXPERTS, 128), F32)],
        compiler_params=_cparams("arbitrary"),
        name="out_proj",
    )(attn_n, ssm_n, x2d, wa, ws, g, wrt, brt, triu, low, counts_in)


def _tile_span(ref, first_row, n_rows, lead=()):
    start = pl.multiple_of(first_row * SUBLANES, SUBLANES)
    return ref.at[(*lead, pl.ds(start, n_rows * SUBLANES), slice(None))]


def _run_copies(tile, src_tbl, cnt_tbl, loc_tbl, make_copy):
    def body(e, c):
        j = tile * N_EXPERTS + e
        cnt, src, loc = cnt_tbl[j], src_tbl[j], loc_tbl[j]
        off = 0
        for size in _RUN_SIZES:
            @pl.when((cnt & size) != 0)
            def _(off=off, size=size):
                make_copy(src + off, loc + off, size).start()
            off = off + (cnt & size)
        return c

    lax.fori_loop(0, N_EXPERTS, body, 0)


def _dispatch_kernel(src_tbl, cnt_tbl, loc_tbl, pend_ref, pad_ref, nu_ref, off_ref, xa_ref, xb_ref, g_ref,
                     rows_ref, buf, zbuf, xt, sem, zsem, *, tiles_a, nblk):
    i = pl.program_id(0)
    n = pl.num_programs(0)
    slot = i % 2

    def zero_block(start_row):
        return pltpu.make_async_copy(zbuf, _tile_span(rows_ref, start_row, MOE_TILE), zsem)

    def pad_copies(e, issue):
        n_pad = pad_ref[e]
        first = pend_ref[e] - n_pad
        off = 0
        for size in _RUN_SIZES:
            @pl.when((n_pad & size) != 0)
            def _(off=off, size=size):
                issue(pltpu.make_async_copy(_tile_span(zbuf, 0, size), _tile_span(rows_ref, first + off, size), zsem))
            off = off + (n_pad & size)

    @pl.when(i == 0)
    def _():
        zbuf[...] = jnp.zeros_like(zbuf)

        def pads_start(e, c):
            pad_copies(e, lambda cp: cp.start())
            return c

        def pads_wait(e, c):
            pad_copies(e, lambda cp: cp.wait())
            return c

        def tail_start(j, c):
            zero_block(j * MOE_TILE).start()
            return c

        def tail_wait(j, c):
            zero_block(j * MOE_TILE).wait()
            return c

        lax.fori_loop(0, N_EXPERTS, pads_start, 0)
        lax.fori_loop(nu_ref[0], nblk, tail_start, 0)
        lax.fori_loop(0, N_EXPERTS, pads_wait, 0)
        lax.fori_loop(nu_ref[0], nblk, tail_wait, 0)

    def slot_rows(s):
        return buf.at[pl.ds(pl.multiple_of(s * _SLOT_ROWS, _SLOT_ROWS), _SLOT_ROWS), :]

    def slot_done(s):
        return pltpu.make_async_copy(slot_rows(s), slot_rows(s), sem.at[s])

    @pl.when(i >= 2)
    def _():
        slot_done(slot).wait()

    def fill(x_ref):
        _to_row_tiles(xt, _rms(x_ref[...]) * g_ref[...])

        def body(t, c):
            v = xt[pl.ds(pl.multiple_of(t * SUBLANES, SUBLANES), SUBLANES), :]
            for k in range(TOP_K):
                off = pl.multiple_of(off_ref[0, k * ROUTE_TILE + t], SUBLANES)
                buf[pl.ds(off, SUBLANES), :] = v
            return c

        lax.fori_loop(0, ROUTE_TILE, body, 0, unroll=8)

    @pl.when(i < tiles_a)
    def _():
        fill(xa_ref)

    @pl.when(i >= tiles_a)
    def _():
        fill(xb_ref)

    _run_copies(i, src_tbl, cnt_tbl, loc_tbl,
                lambda g, l, size: pltpu.make_async_copy(_tile_span(slot_rows(slot), l, size),
                                                         _tile_span(rows_ref, g, size), sem.at[slot]))

    @pl.when(i == n - 1)
    def _():
        slot_done(1 - slot).wait()
        slot_done(slot).wait()


def _dispatch(runs, pend, pad, n_used, off, xa, xb, g, n_rows):
    assert MOE_TILE <= ROUTE_TILE
    tile_rows = ROUTE_TILE * SUBLANES
    tiles_a, tiles_b = xa.shape[0] // ROUTE_TILE, xb.shape[0] // ROUTE_TILE
    assert tiles_a + tiles_b >= 2
    grid_spec = pltpu.PrefetchScalarGridSpec(
        num_scalar_prefetch=6, grid=(tiles_a + tiles_b,),
        in_specs=[pl.BlockSpec((None, 1, TOP_K * ROUTE_TILE), lambda i, *_: (i, 0, 0), memory_space=pltpu.SMEM),
                  pl.BlockSpec((ROUTE_TILE, D_MODEL), lambda i, *_: (jnp.minimum(i, tiles_a - 1), 0)),
                  pl.BlockSpec((ROUTE_TILE, D_MODEL), lambda i, *_: (jnp.maximum(i - tiles_a, 0), 0)),
                  pl.BlockSpec((1, D_MODEL), lambda i, *_: (0, 0))],
        out_specs=pl.BlockSpec(memory_space=pl.ANY),
        scratch_shapes=[pltpu.VMEM((2 * _SLOT_ROWS, 128), F32),
                        pltpu.VMEM((MOE_TILE * SUBLANES, 128), F32), pltpu.VMEM((tile_rows, 128), F32),
                        pltpu.SemaphoreType.DMA((2,)), pltpu.SemaphoreType.DMA(())])
    return pl.pallas_call(
        functools.partial(_dispatch_kernel, tiles_a=tiles_a, nblk=n_rows // MOE_TILE), grid_spec=grid_spec,
        out_shape=jax.ShapeDtypeStruct((n_rows * SUBLANES, 128), F32),
        compiler_params=_cparams("arbitrary"), name="dispatch",
    )(*runs, pend, pad, n_used, off, xa, xb, g)


def _moe_kernel(be_ref, nu_ref, nv_ref, next_ref, slot_ref, x_ref, wgu_hbm, bg_ref, bl_ref, wd_hbm, bd_ref,
                perm_ref, o_ref, wgu_buf, wd_buf, wg_s, wl_s, wd_s, sem_gu, sem_d):
    i = pl.program_id(0)
    used = i < nu_ref[0]
    e = be_ref[i]
    new_expert = (i == 0) | (e != be_ref[jnp.maximum(i - 1, 0)])

    def fetch(expert, s):
        return (pltpu.make_async_copy(wgu_hbm.at[expert], wgu_buf.at[s], sem_gu.at[s]),
                pltpu.make_async_copy(wd_hbm.at[expert], wd_buf.at[s], sem_d.at[s]))

    @pl.when(used & new_expert)
    def _():
        s = slot_ref[e]

        @pl.when(i == 0)
        def _():
            for cp in fetch(e, s):
                cp.start()

        @pl.when(next_ref[e] >= 0)
        def _():
            for cp in fetch(next_ref[e], 1 - s):
                cp.start()

        for cp in fetch(e, s):
            cp.wait()
        for c in range(2 * D_FF // 256):
            r = _dot(wgu_buf[s, :, 256 * c:256 * (c + 1)].astype(BF16), perm_ref[...])
            wg_s[:, 128 * c:128 * (c + 1)] = r[:, :128].astype(BF16)
            wl_s[:, 128 * c:128 * (c + 1)] = r[:, 128:].astype(BF16)
        wd_s[...] = wd_buf[s].astype(BF16)

    def expert_rows(rows):
        x = _from_row_tiles(x_ref, rows).astype(BF16)
        glu = jnp.minimum(_dot(x, wg_s[...]) + bg_ref[...], SWIGLU_LIMIT)
        lin = jnp.clip(_dot(x, wl_s[...]) + bl_ref[...], -SWIGLU_LIMIT, SWIGLU_LIMIT)
        act = glu * jax.nn.sigmoid(SWIGLU_ALPHA * glu) * (lin + 1.0)
        _to_row_tiles(o_ref, _dot(act.astype(BF16), wd_s[...]) + bd_ref[...])

    def zero_rows(first, rows):
        o_ref[pl.ds(first * SUBLANES, rows * SUBLANES), :] = jnp.zeros((rows * SUBLANES, 128), F32)

    half = MOE_TILE // 2

    @pl.when(used & (nv_ref[i] > half))
    def _():
        expert_rows(MOE_TILE)

    @pl.when(used & (nv_ref[i] <= half))
    def _():
        expert_rows(half)
        zero_rows(half, half)


def _moe(block_e, n_used, n_valid, next_e, e_slot, x_rows, wgu, bg, bl, wd, bd, perm):
    nblk = x_rows.shape[0] // (MOE_TILE * SUBLANES)
    row = lambda i, be, nu, *_: (jnp.minimum(i, nu[0] - 1), 0)
    wsel = lambda i, be, *_: (be[i], 0, 0)
    grid_spec = pltpu.PrefetchScalarGridSpec(
        num_scalar_prefetch=5,
        grid=(nblk,),
        in_specs=[pl.BlockSpec((MOE_TILE * SUBLANES, 128), row),
                  pl.BlockSpec(memory_space=pl.ANY),
                  pl.BlockSpec((None, 1, D_FF), wsel), pl.BlockSpec((None, 1, D_FF), wsel),
                  pl.BlockSpec(memory_space=pl.ANY), pl.BlockSpec((None, 1, D_MODEL), wsel),
                  pl.BlockSpec((256, 256), lambda i, *_: (0, 0))],
        out_specs=pl.BlockSpec((MOE_TILE * SUBLANES, 128), row),
        scratch_shapes=[pltpu.VMEM((2, D_MODEL, 2 * D_FF), F32), pltpu.VMEM((2, D_FF, D_MODEL), F32),
                        pltpu.VMEM((D_MODEL, D_FF), BF16), pltpu.VMEM((D_MODEL, D_FF), BF16),
                        pltpu.VMEM((D_FF, D_MODEL), BF16),
                        pltpu.SemaphoreType.DMA((2,)), pltpu.SemaphoreType.DMA((2,))],
    )
    return pl.pallas_call(
        _moe_kernel,
        grid_spec=grid_spec,
        out_shape=jax.ShapeDtypeStruct(x_rows.shape, F32),
        input_output_aliases={5: 0},
        compiler_params=_cparams("arbitrary"),
        name="moe",
    )(block_e, n_used, n_valid, next_e, e_slot, x_rows, wgu, bg, bl, wd, bd, perm)


def _combine_kernel(src_tbl, cnt_tbl, loc_tbl, off_ref, gate_ref, rows_hbm, x1_ref, o_ref, buf, ybuf, sem, *,
                    tile_base):
    i = pl.program_id(0)
    n = pl.num_programs(0)

    def slot_rows(s):
        return buf.at[pl.ds(pl.multiple_of(s * _SLOT_ROWS, _SLOT_ROWS), _SLOT_ROWS), :]

    def fetch(tile, s):
        _run_copies(tile_base + tile, src_tbl, cnt_tbl, loc_tbl,
                    lambda g, l, size: pltpu.make_async_copy(_tile_span(rows_hbm, g, size),
                                                             _tile_span(slot_rows(s), l, size), sem.at[s]))

    @pl.when(i == 0)
    def _():
        fetch(0, 0)

    @pl.when(i + 1 < n)
    def _():
        fetch(i + 1, (i + 1) % 2)

    slot = i % 2
    pltpu.make_async_copy(slot_rows(slot), slot_rows(slot), sem.at[slot]).wait()

    def body(t, c):
        acc = None
        for k in range(TOP_K):
            off = pl.multiple_of(off_ref[0, k * ROUTE_TILE + t], SUBLANES)
            v = gate_ref[0, k * ROUTE_TILE + t] * buf[pl.ds(off, SUBLANES), :]
            acc = v if acc is None else acc + v
        ybuf[pl.ds(pl.multiple_of(t * SUBLANES, SUBLANES), SUBLANES), :] = acc
        return c

    lax.fori_loop(0, ROUTE_TILE, body, 0, unroll=8)
    o_ref[...] = x1_ref[...] + _from_row_tiles(ybuf, ROUTE_TILE)


def _combine(runs, off, gates, rows, x1, tile_base):
    assert tile_base % 2 == 0
    t = x1.shape[0]
    tile_rows = ROUTE_TILE * SUBLANES
    smem_tile = pl.BlockSpec((None, 1, TOP_K * ROUTE_TILE), lambda i, *_: (tile_base + i, 0, 0),
                             memory_space=pltpu.SMEM)
    grid_spec = pltpu.PrefetchScalarGridSpec(
        num_scalar_prefetch=3,
        grid=(t // ROUTE_TILE,),
        in_specs=[smem_tile, smem_tile, pl.BlockSpec(memory_space=pl.ANY),
                  pl.BlockSpec((ROUTE_TILE, D_MODEL), lambda i, *_: (i, 0))],
        out_specs=pl.BlockSpec((ROUTE_TILE, D_MODEL), lambda i, *_: (i, 0)),
        scratch_shapes=[pltpu.VMEM((2 * _SLOT_ROWS, 128), F32), pltpu.VMEM((tile_rows, 128), F32),
                        pltpu.SemaphoreType.DMA((2,))],
    )
    return pl.pallas_call(
        functools.partial(_combine_kernel, tile_base=tile_base),
        grid_spec=grid_spec,
        out_shape=jax.ShapeDtypeStruct((t, D_MODEL), F32),
        compiler_params=_cparams("arbitrary"),
        name="combine",
    )(*runs, off, gates, rows, x1)


def _route(before, cnt, loc, n_tokens):
    n_assign = n_tokens * TOP_K
    table = lambda a: a[:, :, 0].astype(jnp.int32)
    before, cnt, run_loc = table(before), table(cnt), table(loc)
    counts = before[-1] + cnt[-1]
    padded = ((counts + MOE_TILE - 1) // MOE_TILE) * MOE_TILE
    pend = jnp.cumsum(padded)
    pstart = pend - padded
    run_src = pstart[None, :] + before
    nblk = (n_assign + MOE_TILE - 1) // MOE_TILE + N_EXPERTS
    n_used = (pend[-1] // MOE_TILE).astype(jnp.int32)
    block_start = jnp.arange(nblk, dtype=jnp.int32) * MOE_TILE
    block_e = jnp.sum(pend[None, :] <= jnp.minimum(block_start, pend[-1] - 1)[:, None], axis=1)
    block_e = jnp.minimum(block_e, N_EXPERTS - 1).astype(jnp.int32)
    region_end = jnp.sum(jnp.where(block_e[:, None] == jnp.arange(N_EXPERTS)[None, :], (pstart + counts)[None, :], 0),
                         axis=1)
    n_valid = jnp.clip(region_end - block_start, 0, MOE_TILE).astype(jnp.int32)
    ids = jnp.arange(N_EXPERTS, dtype=jnp.int32)
    nonempty = counts > 0
    next_e = jnp.min(jnp.where((ids[None, :] > ids[:, None]) & nonempty[None, :], ids[None, :], N_EXPERTS), axis=1)
    next_e = jnp.where(next_e == N_EXPERTS, -1, next_e).astype(jnp.int32)
    e_slot = ((jnp.cumsum(nonempty.astype(jnp.int32)) - nonempty.astype(jnp.int32)) % 2).astype(jnp.int32)
    blocks = (block_e, n_used.reshape(1), n_valid, next_e, e_slot)
    runs = (run_src.reshape(-1), cnt.reshape(-1), run_loc.reshape(-1))
    return runs, blocks, pend, padded - counts, nblk * MOE_TILE


def kernel(x_prompt, x_sample, cache_k, cache_v, state_ssm_re, state_ssm_im, norm_mix_g, w_in, q_norm_g, k_norm_g, attn_sinks, ssm_lambda_re, ssm_lambda_im, ssm_b_re, ssm_b_im, ssm_c_re, ssm_c_im, ssm_d, ssm_log_dt, w_glu, b_glu, attn_out_norm_g, ssm_out_norm_g, w_out, norm_ffn_g, w_router, b_router, w_gate_up, b_gate_up, w_down, b_down):
    depth = w_in.shape[0]
    assert depth == 1
    bp, sp, _ = x_prompt.shape
    bs, ss, _ = x_sample.shape
    tp, ts = bp * sp, bs * ss
    assert bp == SUBLANES and sp % S5_CHUNK == 0 and bs % SUBLANES == 0

    perm = jnp.asarray(_Q_PERM)
    w_in0 = w_in[0]
    wq = w_in0[:, :D_ATTN].reshape(D_MODEL, N_HEADS, HEAD_DIM)[:, perm].reshape(D_MODEL, D_ATTN)
    w_in_b = jnp.concatenate([wq, w_in0[:, D_ATTN:]], axis=1).astype(BF16)
    qkg = jnp.concatenate([jnp.tile(q_norm_g[0], N_HEADS), jnp.tile(k_norm_g[0], N_KV_HEADS)])[None]
    pmat = jnp.kron(jnp.eye(256 // HEAD_DIM, dtype=F32),
                    jnp.full((HEAD_DIM, HEAD_DIM), 1.0 / HEAD_DIM, F32)).astype(BF16)
    g_mix = norm_mix_g[0][None]
    sinks = attn_sinks[0]
    g_attn = attn_out_norm_g[0].reshape(N_HEADS, HEAD_DIM)[perm].reshape(1, D_ATTN)
    w_out0 = w_out[0]
    w_out_a = w_out0[:D_ATTN].reshape(N_HEADS, HEAD_DIM, D_MODEL)[perm].reshape(D_ATTN, D_MODEL).astype(BF16)
    w_out_s = w_out0[D_ATTN:].astype(BF16)
    g_ssm = ssm_out_norm_g[0][None]
    g_ffn = norm_ffn_g[0][None]
    w_rt = w_router[0].T.astype(BF16)
    b_rt = jnp.broadcast_to(b_router[0][:, None], (N_EXPERTS, ROUTE_TILE))

    a_re, a_im, bb_re, bb_im = _s5_prep(
        ssm_lambda_re[0], ssm_lambda_im[0], ssm_log_dt[0][:, None],
        jnp.swapaxes(ssm_b_re[0], 1, 2), jnp.swapaxes(ssm_b_im[0], 1, 2))
    chan_g = jnp.arange(D_SSM)[:, None] // SSM_GROUP
    state_g = jnp.arange(D_STATE)[None, :] // SSM_STATE
    bd_b = lambda bb: jnp.where(chan_g == state_g, jnp.tile(bb.reshape(D_SSM, SSM_STATE), (1, N_SSM_GROUPS)), 0.0)
    wb = jnp.concatenate([bd_b(bb_re), bd_b(bb_im)], axis=1).astype(BF16)
    bd_c = lambda c: jnp.where(state_g.T == chan_g.T,
                               jnp.tile(jnp.swapaxes(c, 1, 2).reshape(D_STATE, SSM_GROUP), (1, N_SSM_GROUPS)), 0.0)
    wc = jnp.concatenate([bd_c(ssm_c_re[0]), -bd_c(ssm_c_im[0])], axis=0).astype(BF16)
    a_re, a_im = a_re.reshape(1, D_STATE), a_im.reshape(1, D_STATE)
    d_skip = ssm_d[0].reshape(1, D_SSM)
    w_glu_b = w_glu[0].astype(BF16)
    b_glu0 = b_glu[0][None]

    b_g = b_gate_up[0][:, None, 0::2]
    b_l = b_gate_up[0][:, None, 1::2]
    b_d = b_down[0][:, None, :]
    idx = jnp.arange(256)
    deint = (idx[None, :] == jnp.where(idx % 2 == 0, idx // 2, 128 + idx // 2)[:, None]).astype(BF16)

    xp2 = x_prompt.reshape(tp, D_MODEL)
    xs2 = x_sample.reshape(ts, D_MODEL)
    qp, kp, vp, up = _in_proj(xp2, g_mix, w_in_b, qkg, pmat)
    qs, ks, vs, us = _in_proj(xs2, g_mix, w_in_b, qkg, pmat)

    kp3, vp3 = kp.reshape(bp, sp, D_KV), vp.reshape(bp, sp, D_KV)
    attn_p = _band_attention(sinks, qp.reshape(bp, sp, D_ATTN), kp3, vp3, g_attn).reshape(tp, D_ATTN)
    ck = cache_k[0].reshape(bs, -1, D_KV)
    cv = cache_v[0].reshape(bs, -1, D_KV)
    attn_s, new_ck, new_cv = _cache_attention(sinks, qs, ks, vs, ck, cv, g_attn, ss)

    zeros_p = jnp.zeros((bp, D_STATE), F32)
    s5_args = (a_re, a_im, wb, wc, d_skip, w_glu_b, b_glu0, g_ssm)
    ssm_p, hr_p, hi_p = _s5(up.reshape(bp, sp, D_SSM), zeros_p, zeros_p, *s5_args,
                            tt=S5_CHUNK, time_chunked=True)
    ssm_s, hr_s, hi_s = _s5(us.reshape(bs // SUBLANES, SUBLANES * ss, D_SSM),
                            state_ssm_re[0].reshape(bs, D_STATE), state_ssm_im[0].reshape(bs, D_STATE),
                            *s5_args, tt=ss, time_chunked=False)

    triu = (jnp.arange(ROUTE_TILE)[:, None] < jnp.arange(ROUTE_TILE)[None, :]).astype(BF16)
    low = (jnp.arange(N_EXPERTS)[:, None] > jnp.arange(N_EXPERTS)[None, :]).astype(BF16)
    proj_args = (w_out_a, w_out_s, g_ffn, w_rt, b_rt, triu, low)
    x1p, *tab_p, counts_p = _out_proj(attn_p, ssm_p.reshape(tp, D_SSM), xp2, *proj_args,
                                      jnp.zeros((N_EXPERTS, 128), F32), 0)
    x1s, *tab_s, _ = _out_proj(attn_s, ssm_s.reshape(ts, D_SSM), xs2, *proj_args, counts_p, tp // ROUTE_TILE)

    gates, off, before, cnt, loc = (jnp.concatenate([p, s_], axis=0) for p, s_ in zip(tab_p, tab_s))
    runs, blocks, pend, pad, n_rows = _route(before, cnt, loc, tp + ts)
    x_rows = _dispatch(runs, pend, pad, blocks[1], off, x1p, x1s, g_ffn, n_rows)
    out_rows = _moe(*blocks, x_rows, w_gate_up[0], b_g, b_l, w_down[0], b_d, deint)
    yp = _combine(runs, off, gates, out_rows, x1p, 0).reshape(bp, sp, D_MODEL)
    ys = _combine(runs, off, gates, out_rows, x1s, tp // ROUTE_TILE).reshape(bs, ss, D_MODEL)

    kv5 = lambda a, b_: a.reshape(b_, -1, N_KV_HEADS, HEAD_DIM)
    new_kp = kv5(kp3[:, -WINDOW:], bp)[None]
    new_vp = kv5(vp3[:, -WINDOW:], bp)[None]
    new_ks = kv5(new_ck, bs)[None]
    new_vs = kv5(new_cv, bs)[None]
    st = lambda h, b_: h.reshape(1, b_, N_SSM_GROUPS, SSM_STATE)
    return (yp, ys, new_kp, new_vp, st(hr_p, bp), st(hi_p, bp),
            new_ks, new_vs, st(hr_s, bs), st(hi_s, bs))
```

```python
import functools
import math

import jax
import jax.numpy as jnp
from jax import lax
from jax.experimental import pallas as pl
from jax.experimental.pallas import tpu as pltpu

F32 = jnp.float32
BF16 = jnp.bfloat16

D_MODEL = 1024
D_ATTN = 512
D_SSM = 512
HEAD_DIM = 64
N_HEADS = 8
N_KV_HEADS = 2
D_KV = N_KV_HEADS * HEAD_DIM
WINDOW = 128
SSM_GROUP = 16
N_SSM_GROUPS = 32
SSM_STATE = 64
D_STATE = N_SSM_GROUPS * SSM_STATE
N_EXPERTS = 32
TOP_K = 4
D_FF = 1024
SWIGLU_LIMIT = 7.0
SWIGLU_ALPHA = 1.702
RMS_EPS = 1e-6
NEG_INF = -1e30
D_IN_PROJ = D_ATTN + 2 * D_KV + D_SSM
D_QK = D_ATTN + D_KV

SUBLANES = 8
VMEM_LIMIT = 56 * 1024 * 1024

ROW_TILE = 2048
MOE_TILE = 512
ATTN_BLOCKS = 8
ROUTE_TILE = 512
_SLOT_ROWS = ROUTE_TILE * TOP_K * SUBLANES
_LONG_RUN = 128
_RUN_SIZES = tuple(1 << b for b in range(ROUTE_TILE.bit_length() - 1, -1, -1))
S5_CHUNK = 128
S5_COLS = 512

_Q_PERM = (0, 4, 1, 5, 2, 6, 3, 7)


def _cparams(*sem):
    return pltpu.CompilerParams(dimension_semantics=sem, vmem_limit_bytes=VMEM_LIMIT)


def _dot(a, b):
    return jnp.dot(a, b, preferred_element_type=F32)


def _rms(x):
    return x * lax.rsqrt(jnp.mean(x * x, axis=-1, keepdims=True) + RMS_EPS)


def _in_proj_kernel(x_ref, g_ref, w_ref, qkg_ref, p_ref, q_ref, k_ref, v_ref, u_ref):
    xn = _rms(x_ref[...]) * g_ref[...]
    h = _dot(xn.astype(BF16), w_ref[...])
    qk = h[:, :D_QK]
    sq = (qk * qk).astype(BF16)
    p = p_ref[...]
    ms = jnp.concatenate(
        [_dot(sq[:, 0:256], p), _dot(sq[:, 256:512], p), _dot(sq[:, 512:640], p[:128, :128])],
        axis=-1)
    qkn = qk * lax.rsqrt(ms + RMS_EPS) * qkg_ref[...]
    q_ref[...] = qkn[:, :D_ATTN].astype(BF16)
    k_ref[...] = qkn[:, D_ATTN:]
    v_ref[...] = h[:, D_QK:D_QK + D_KV]
    u_ref[...] = h[:, D_QK + D_KV:]


def _in_proj(x2d, g, w, qkg, pmat):
    t = x2d.shape[0]
    tm = min(ROW_TILE, t)
    row = lambda i: (i, 0)
    fix = lambda i: (0, 0)
    return pl.pallas_call(
        _in_proj_kernel,
        grid=(t // tm,),
        in_specs=[pl.BlockSpec((tm, D_MODEL), row), pl.BlockSpec((1, D_MODEL), fix),
                  pl.BlockSpec((D_MODEL, D_IN_PROJ), fix), pl.BlockSpec((1, D_QK), fix),
                  pl.BlockSpec((256, 256), fix)],
        out_specs=[pl.BlockSpec((tm, D_ATTN), row), pl.BlockSpec((tm, D_KV), row),
                   pl.BlockSpec((tm, D_KV), row), pl.BlockSpec((tm, D_SSM), row)],
        out_shape=[jax.ShapeDtypeStruct((t, D_ATTN), BF16), jax.ShapeDtypeStruct((t, D_KV), F32),
                   jax.ShapeDtypeStruct((t, D_KV), F32), jax.ShapeDtypeStruct((t, D_SSM), F32)],
        compiler_params=_cparams("parallel"),
        name="in_proj",
    )(x2d, g, w, qkg, pmat)


def _softmax_pv(s_blocks, v_blocks, sink):
    m = sink
    for s in s_blocks:
        m = jnp.maximum(m, jnp.max(s, axis=-1, keepdims=True))
    den = jnp.exp(sink - m)
    acc = None
    for s, v in zip(s_blocks, v_blocks):
        p = jnp.exp(s - m)
        den = den + jnp.sum(p, axis=-1, keepdims=True)
        pv = _dot(p.astype(BF16), v)
        acc = pv if acc is None else acc + pv
    return acc / den


def _band_attn_kernel(sink_ref, q_ref, kp_ref, kc_ref, vp_ref, vc_ref, g_ref, o_ref, nk_ref, nv_ref):
    i = pl.program_id(1)
    row = lax.broadcasted_iota(jnp.int32, (WINDOW, 2 * WINDOW), 0)
    col = lax.broadcasted_iota(jnp.int32, (WINDOW, 2 * WINDOW), 1)
    band = (col > row) & (col <= row + WINDOW)
    lane = lax.broadcasted_iota(jnp.int32, (WINDOW, 128), 1)
    low = lane < HEAD_DIM
    zero = jnp.zeros((), BF16)
    for j in range(ATTN_BLOCKS):
        cur = slice(WINDOW * j, WINDOW * (j + 1))
        q = q_ref[cur, :]
        k_prev = kp_ref[...] if j == 0 else kc_ref[WINDOW * (j - 1):WINDOW * j, :]
        v_prev = vp_ref[...] if j == 0 else vc_ref[WINDOW * (j - 1):WINDOW * j, :]
        kb = jnp.concatenate([k_prev, kc_ref[cur, :]], axis=0).astype(BF16)
        vb = jnp.concatenate([v_prev, vc_ref[cur, :]], axis=0).astype(BF16)
        mask = band & ((col >= WINDOW) | (i > 0)) if j == 0 else band
        outs = []
        for pair in range(N_HEADS // 2):
            qp = q[:, 128 * pair:128 * (pair + 1)]
            halves = []
            for par in range(2):
                qm = jnp.where(low if par == 0 else ~low, qp, zero)
                s = lax.dot_general(qm, kb, (((1,), (1,)), ((), ())), preferred_element_type=F32)
                s = jnp.where(mask, s * (HEAD_DIM ** -0.5), NEG_INF)
                halves.append(_softmax_pv([s], [vb], sink_ref[_Q_PERM[2 * pair + par]]))
            outs.append(jnp.where(low, halves[0], halves[1]))
        o = jnp.concatenate(outs, axis=-1)
        o_ref[cur, :] = (_rms(o) * g_ref[...]).astype(BF16)
    nk_ref[...] = kc_ref[WINDOW * (ATTN_BLOCKS - 1):, :]
    nv_ref[...] = vc_ref[WINDOW * (ATTN_BLOCKS - 1):, :]


def _band_attention(sinks, q, k, v, g):
    b, s, _ = q.shape
    qb = WINDOW * ATTN_BLOCKS
    cur = lambda bi, i: (bi, i, 0)
    prev = lambda bi, i: (bi, jnp.maximum(i * ATTN_BLOCKS - 1, 0), 0)
    first = lambda bi, i: (bi, 0, 0)
    return pl.pallas_call(
        _band_attn_kernel,
        grid=(b, s // qb),
        in_specs=[pl.BlockSpec(memory_space=pltpu.SMEM),
                  pl.BlockSpec((None, qb, D_ATTN), cur),
                  pl.BlockSpec((None, WINDOW, D_KV), prev), pl.BlockSpec((None, qb, D_KV), cur),
                  pl.BlockSpec((None, WINDOW, D_KV), prev), pl.BlockSpec((None, qb, D_KV), cur),
                  pl.BlockSpec((1, D_ATTN), lambda bi, i: (0, 0))],
        out_specs=[pl.BlockSpec((None, qb, D_ATTN), cur), pl.BlockSpec((None, WINDOW, D_KV), first),
                   pl.BlockSpec((None, WINDOW, D_KV), first)],
        out_shape=[jax.ShapeDtypeStruct((b, s, D_ATTN), BF16), jax.ShapeDtypeStruct((b, WINDOW, D_KV), F32),
                   jax.ShapeDtypeStruct((b, WINDOW, D_KV), F32)],
        compiler_params=_cparams("parallel", "arbitrary"),
        name="band_attn",
    )(sinks, q, k, k, v, v, g)


_PAIR_ROWS = 8
_CACHE_BB = 16


def _cache_attn_kernel(sink_ref, q_ref, kn_ref, vn_ref, ck_ref, cv_ref, g_ref, o_ref, nk_ref, nv_ref, *, n_new):
    n_buf = ck_ref.shape[1]
    rows_blk = _CACHE_BB * n_new
    for b in range(_CACHE_BB):
        for new_ref, cache_ref, fresh_ref in ((nk_ref, ck_ref, kn_ref), (nv_ref, cv_ref, vn_ref)):
            new_ref[b, 0:n_buf - n_new, :] = cache_ref[b, n_new:n_buf, :]
            new_ref[b, n_buf - n_new:n_buf, :] = fresh_ref[n_new * b:n_new * (b + 1), :]
    knew = kn_ref[...].astype(BF16)
    vnew = vn_ref[...].astype(BF16)
    lane = lax.broadcasted_iota(jnp.int32, (_PAIR_ROWS, 128), 1)
    low = lane < HEAD_DIM
    zero = jnp.zeros((), BF16)
    n_stack = N_HEADS * _PAIR_ROWS
    r = lax.broadcasted_iota(jnp.int32, (n_stack, 1), 0) % _PAIR_ROWS
    r_seq, r_tok = r // n_new, r % n_new
    colc = lax.broadcasted_iota(jnp.int32, (n_stack, 2 * n_buf), 1)
    c_seq, c_pos = colc // n_buf, colc % n_buf
    mask_c = (c_seq == r_seq) & (c_pos + WINDOW > r_tok + n_buf)
    coln = lax.broadcasted_iota(jnp.int32, (n_stack, rows_blk), 1)
    sink_col = jnp.concatenate(
        [jnp.full((_PAIR_ROWS, 1), sink_ref[_Q_PERM[h]], F32) for h in range(N_HEADS)], axis=0)
    for sp in range(_CACHE_BB // 2):
        q = q_ref[_PAIR_ROWS * sp:_PAIR_ROWS * (sp + 1), :]
        pieces = []
        for pair in range(N_HEADS // 2):
            qp = q[:, 128 * pair:128 * (pair + 1)]
            pieces.append(jnp.where(low, qp, zero))
            pieces.append(jnp.where(low, zero, qp))
        qs = jnp.concatenate(pieces, axis=0)
        kc = jnp.concatenate([ck_ref[2 * sp], ck_ref[2 * sp + 1]], axis=0).astype(BF16)
        vc = jnp.concatenate([cv_ref[2 * sp], cv_ref[2 * sp + 1]], axis=0).astype(BF16)
        nt = (((1,), (1,)), ((), ()))
        s_c = lax.dot_general(qs, kc, nt, preferred_element_type=F32) * (HEAD_DIM ** -0.5)
        s_n = lax.dot_general(qs, knew, nt, preferred_element_type=F32) * (HEAD_DIM ** -0.5)
        s_c = jnp.where(mask_c, s_c, NEG_INF)
        n_seq, n_tok = coln // n_new - 2 * sp, coln % n_new
        mask_n = (n_seq == r_seq) & (n_tok <= r_tok)
        s_n = jnp.where(mask_n, s_n, NEG_INF)
        o = _softmax_pv([s_c, s_n], [vc, vnew], sink_col)
        outs = [jnp.where(low, o[16 * pair:16 * pair + 8], o[16 * pair + 8:16 * pair + 16])
                for pair in range(N_HEADS // 2)]
        oo = jnp.concatenate(outs, axis=-1)
        o_ref[_PAIR_ROWS * sp:_PAIR_ROWS * (sp + 1), :] = (_rms(oo) * g_ref[...]).astype(BF16)


def _cache_attention(sinks, q, k, v, cache_k, cache_v, g, n_new):
    t = q.shape[0]
    nb, n_buf, _ = cache_k.shape
    assert n_new * 2 == _PAIR_ROWS and nb % _CACHE_BB == 0 and n_buf == WINDOW
    rows = _CACHE_BB * n_new
    row = lambda i: (i, 0)
    cache_spec = pl.BlockSpec((_CACHE_BB, n_buf, D_KV), lambda i: (i, 0, 0))
    return pl.pallas_call(
        functools.partial(_cache_attn_kernel, n_new=n_new),
        grid=(nb // _CACHE_BB,),
        in_specs=[pl.BlockSpec(memory_space=pltpu.SMEM),
                  pl.BlockSpec((rows, D_ATTN), row), pl.BlockSpec((rows, D_KV), row),
                  pl.BlockSpec((rows, D_KV), row), cache_spec, cache_spec,
                  pl.BlockSpec((1, D_ATTN), lambda i: (0, 0))],
        out_specs=[pl.BlockSpec((rows, D_ATTN), row), cache_spec, cache_spec],
        out_shape=[jax.ShapeDtypeStruct((t, D_ATTN), BF16), jax.ShapeDtypeStruct(cache_k.shape, F32),
                   jax.ShapeDtypeStruct(cache_v.shape, F32)],
        compiler_params=_cparams("parallel"),
        name="cache_attn",
    )(sinks, q, k, v, cache_k, cache_v, g)


def _s5_prep_kernel(lre_ref, lim_ref, ldt_ref, bre_ref, bim_ref, are_ref, aim_ref, bbre_ref, bbim_ref):
    dt = jnp.exp(ldt_ref[...])
    l_re = jnp.minimum(lre_ref[...], -1e-4)
    l_im = lim_ref[...]
    mag = jnp.exp(l_re * dt)
    a_re = mag * jnp.cos(l_im * dt)
    a_im = mag * jnp.sin(l_im * dt)
    den = l_re * l_re + l_im * l_im
    n_re = a_re - 1.0
    z_re = (n_re * l_re + a_im * l_im) / den
    z_im = (a_im * l_re - n_re * l_im) / den
    are_ref[...] = a_re
    aim_ref[...] = a_im
    br, bi = bre_ref[...], bim_ref[...]
    zr, zi = z_re[:, None, :], z_im[:, None, :]
    bbre_ref[...] = zr * br - zi * bi
    bbim_ref[...] = zr * bi + zi * br


def _s5_prep(lam_re, lam_im, log_dt, b_re_t, b_im_t):
    g, p = lam_re.shape
    sd = jax.ShapeDtypeStruct
    return pl.pallas_call(
        _s5_prep_kernel,
        out_shape=[sd((g, p), F32), sd((g, p), F32), sd(b_re_t.shape, F32), sd(b_re_t.shape, F32)],
        name="s5_prep",
    )(lam_re, lam_im, log_dt, b_re_t, b_im_t)


def _s5_kernel(u_ref, h0r_ref, h0i_ref, ar_ref, ai_ref, wb_ref, wc_ref, d_ref, wglu_ref, bglu_ref,
               g_ref, o_ref, hr_ref, hi_ref, bu_ref, hs_ref, usc_ref, ysc_ref, *, tt):
    j = pl.program_id(1)
    rows = SUBLANES * tt
    n_tiles = D_STATE // 128
    time_major = tt % SUBLANES == 0

    @pl.when(j == 0)
    def _():
        hs_ref[:, :D_STATE] = h0r_ref[...]
        hs_ref[:, D_STATE:] = h0i_ref[...]

    u = u_ref[...].reshape(rows, D_SSM)
    if time_major:
        for c in range(D_SSM // 128):
            for b in range(SUBLANES):
                usc_ref[c, pl.ds(b, tt, stride=SUBLANES), :] = u[b * tt:(b + 1) * tt, 128 * c:128 * (c + 1)]
        ub = jnp.concatenate([usc_ref[c] for c in range(D_SSM // 128)], axis=-1).astype(BF16)
    else:
        ub = u.astype(BF16)

    def step_rows(t):
        if time_major:
            return pl.ds(pl.multiple_of(t * SUBLANES, SUBLANES), SUBLANES)
        return pl.ds(t, SUBLANES, stride=tt)

    for n in range(2 * D_STATE // 256):
        band = (n % (D_STATE // 256)) // 2
        res = _dot(ub[:, 128 * band:128 * (band + 1)],
                   wb_ref[128 * band:128 * (band + 1), 256 * n:256 * (n + 1)])
        bu_ref[2 * n] = res[:, :128]
        bu_ref[2 * n + 1] = res[:, 128:]

    tiles_per_pass = S5_COLS // 128
    for c0 in range(0, n_tiles, tiles_per_pass):
        tiles = range(c0, c0 + tiles_per_pass)
        a_r = [jnp.broadcast_to(ar_ref[:, 128 * c:128 * (c + 1)], (SUBLANES, 128)) for c in tiles]
        a_i = [jnp.broadcast_to(ai_ref[:, 128 * c:128 * (c + 1)], (SUBLANES, 128)) for c in tiles]

        def step(t, carry, tiles=tiles, a_r=a_r, a_i=a_i):
            at_t = step_rows(t)
            out = []
            for k, c in enumerate(tiles):
                h_r, h_i = carry[2 * k], carry[2 * k + 1]
                n_r = a_r[k] * h_r - a_i[k] * h_i + bu_ref[c, at_t, :]
                n_i = a_r[k] * h_i + a_i[k] * h_r + bu_ref[n_tiles + c, at_t, :]
                bu_ref[c, at_t, :] = n_r
                bu_ref[n_tiles + c, at_t, :] = n_i
                out += [n_r, n_i]
            return tuple(out)

        init = []
        for c in tiles:
            init += [hs_ref[:, 128 * c:128 * (c + 1)], hs_ref[:, D_STATE + 128 * c:D_STATE + 128 * (c + 1)]]
        fin = lax.fori_loop(0, tt, step, tuple(init), unroll=min(tt, 8))
        for k, c in enumerate(tiles):
            hs_ref[:, 128 * c:128 * (c + 1)] = fin[2 * k]
            hs_ref[:, D_STATE + 128 * c:D_STATE + 128 * (c + 1)] = fin[2 * k + 1]

    def h_cols(first_tile):
        return jnp.concatenate([bu_ref[first_tile + k] for k in range(4)], axis=-1).astype(BF16)

    ys = []
    for m in range(D_SSM // 128):
        y = _dot(h_cols(4 * m), wc_ref[512 * m:512 * (m + 1), 128 * m:128 * (m + 1)])
        y = y + _dot(h_cols(n_tiles + 4 * m),
                     wc_ref[D_STATE + 512 * m:D_STATE + 512 * (m + 1), 128 * m:128 * (m + 1)])
        if time_major:
            ysc_ref[m] = y
            y = jnp.concatenate([ysc_ref[m, pl.ds(b, tt, stride=SUBLANES), :] for b in range(SUBLANES)], axis=0)
        ys.append(y)
    y = jnp.concatenate(ys, axis=-1) + d_ref[...] * u
    z = _dot(jax.nn.gelu(y).astype(BF16), wglu_ref[...]) + bglu_ref[...]
    s = z[:, :D_SSM] * jax.nn.sigmoid(z[:, D_SSM:])
    o_ref[...] = (_rms(s) * g_ref[...]).astype(BF16).reshape(o_ref.shape)

    @pl.when(j == pl.num_programs(1) - 1)
    def _():
        hr_ref[...] = hs_ref[:, :D_STATE]
        hi_ref[...] = hs_ref[:, D_STATE:]


def _s5(u, h0r, h0i, a_re, a_im, wb, wc, d, wglu, bglu, g, *, tt, time_chunked):
    nbg = h0r.shape[0] // SUBLANES
    if time_chunked:
        nchunks = u.shape[1] // tt
        u_spec = pl.BlockSpec((SUBLANES, tt, D_SSM), lambda gi, j: (gi, j, 0))
    else:
        nchunks = 1
        u_spec = pl.BlockSpec((None, SUBLANES * tt, D_SSM), lambda gi, j: (gi, 0, 0))
    fix = lambda gi, j: (0, 0)
    st_spec = pl.BlockSpec((SUBLANES, D_STATE), lambda gi, j: (gi, 0))
    sd = jax.ShapeDtypeStruct
    return pl.pallas_call(
        functools.partial(_s5_kernel, tt=tt),
        grid=(nbg, nchunks),
        in_specs=[u_spec, st_spec, st_spec,
                  pl.BlockSpec((1, D_STATE), fix), pl.BlockSpec((1, D_STATE), fix),
                  pl.BlockSpec((D_SSM, 2 * D_STATE), fix), pl.BlockSpec((2 * D_STATE, D_SSM), fix),
                  pl.BlockSpec((1, D_SSM), fix), pl.BlockSpec((D_SSM, 2 * D_SSM), fix),
                  pl.BlockSpec((1, 2 * D_SSM), fix), pl.BlockSpec((1, D_SSM), fix)],
        out_specs=[u_spec, st_spec, st_spec],
        out_shape=[sd(u.shape, BF16), sd(h0r.shape, F32), sd(h0r.shape, F32)],
        scratch_shapes=[pltpu.VMEM((2 * D_STATE // 128, SUBLANES * tt, 128), F32),
                        pltpu.VMEM((SUBLANES, 2 * D_STATE), F32),
                        pltpu.VMEM((D_SSM // 128, SUBLANES * tt, 128), F32),
                        pltpu.VMEM((D_SSM // 128, SUBLANES * tt, 128), F32)],
        compiler_params=_cparams("parallel", "arbitrary"),
        name="s5",
    )(u, h0r, h0i, a_re, a_im, wb, wc, d, wglu, bglu, g)


def _to_row_tiles(ref, val, first=0):
    rows = val.shape[0]
    for c in range(D_MODEL // 128):
        ref[pl.ds(first * SUBLANES + c, rows, stride=SUBLANES), :] = val[:, 128 * c:128 * (c + 1)]


def _from_row_tiles(ref, rows, lead=(), first=0):
    return jnp.concatenate(
        [ref[(*lead, pl.ds(first * SUBLANES + c, rows, stride=SUBLANES), slice(None))]
         for c in range(D_MODEL // 128)], axis=-1)


def _route_tile(logits_t, tile, triu_ref, low_ref, carry, gate_ref, off_ref, before_ref, cnt_ref, loc_ref):
    l = logits_t
    expert = lax.broadcasted_iota(jnp.int32, l.shape, 0).astype(F32)
    vals, sels = [], []
    for _ in range(TOP_K):
        m = jnp.max(l, axis=0, keepdims=True)
        idx = jnp.min(jnp.where(l == m, expert, float(N_EXPERTS)), axis=0, keepdims=True)
        sel = expert == idx
        l = jnp.where(sel, -jnp.inf, l)
        vals.append(m)
        sels.append(sel)
    exps = [jnp.exp(v - vals[0]) for v in vals]
    den = exps[0] + exps[1] + exps[2] + exps[3]
    onehot = jnp.where(sels[0] | sels[1] | sels[2] | sels[3], 1.0, 0.0)
    within = _dot(onehot.astype(BF16), triu_ref[...])
    cnt = jnp.broadcast_to(jnp.sum(onehot, axis=1, keepdims=True), onehot.shape)
    cnt_hi = jnp.floor(cnt * (1.0 / 16.0))
    cnt_lo = cnt - 16.0 * cnt_hi
    loc = 16.0 * _dot(low_ref[...], cnt_hi.astype(BF16)) + _dot(low_ref[...], cnt_lo.astype(BF16))
    slot_base = ((tile % 2) * (ROUTE_TILE * TOP_K)).astype(F32)
    rows = [jnp.sum(jnp.where(s, within + loc, 0.0), axis=0, keepdims=True) for s in sels]
    gate_ref[...] = jnp.concatenate([e / den for e in exps], axis=1)
    off_ref[...] = ((jnp.concatenate(rows, axis=1) + slot_base) * SUBLANES).astype(jnp.int32)
    before_ref[...] = carry[...]
    cnt_ref[...] = cnt[:, :128]
    loc_ref[...] = loc[:, :128]
    carry[...] = carry[...] + cnt[:, :128]


def _out_proj_kernel(a_ref, s_ref, x_ref, wa_ref, ws_ref, g_ref, wrt_ref, brt_ref, triu_ref, low_ref, cin_ref,
                     x1_ref, gate_ref, off_ref, before_ref, cnt_ref, loc_ref, cout_ref, carry, *, tile_base):
    i = pl.program_id(0)

    @pl.when(i == 0)
    def _():
        carry[...] = cin_ref[...]

    x1 = x_ref[...] + _dot(a_ref[...], wa_ref[...]) + _dot(s_ref[...], ws_ref[...])
    x1_ref[...] = x1
    xn = _rms(x1) * g_ref[...]
    logits_t = lax.dot_general(wrt_ref[...], xn.astype(BF16), (((1,), (1,)), ((), ())),
                               preferred_element_type=F32) + brt_ref[...]
    _route_tile(logits_t, tile_base + i, triu_ref, low_ref, carry, gate_ref, off_ref, before_ref, cnt_ref, loc_ref)
    cout_ref[...] = carry[...]


def _out_proj(attn_n, ssm_n, x2d, wa, ws, g, wrt, brt, triu, low, counts_in, tile_base):
    t = x2d.shape[0]
    tm = ROUTE_TILE
    n_tiles = t // tm
    row = lambda i: (i, 0)
    fix = lambda i: (0, 0)
    sd = jax.ShapeDtypeStruct
    k_spec = pl.BlockSpec((None, 1, TOP_K * tm), lambda i: (i, 0, 0))
    t_spec = pl.BlockSpec((None, N_EXPERTS, 128), lambda i: (i, 0, 0))
    return pl.pallas_call(
        functools.partial(_out_proj_kernel, tile_base=tile_base),
        grid=(n_tiles,),
        in_specs=[pl.BlockSpec((tm, D_ATTN), row), pl.BlockSpec((tm, D_SSM), row),
                  pl.BlockSpec((tm, D_MODEL), row),
                  pl.BlockSpec((D_ATTN, D_MODEL), fix), pl.BlockSpec((D_SSM, D_MODEL), fix),
                  pl.BlockSpec((1, D_MODEL), fix), pl.BlockSpec((N_EXPERTS, D_MODEL), fix),
                  pl.BlockSpec((N_EXPERTS, tm), fix), pl.BlockSpec((tm, tm), fix),
                  pl.BlockSpec((N_EXPERTS, N_EXPERTS), fix), pl.BlockSpec((N_EXPERTS, 128), fix)],
        out_specs=[pl.BlockSpec((tm, D_MODEL), row), k_spec, k_spec, t_spec, t_spec, t_spec,
                   pl.BlockSpec((N_EXPERTS, 128), fix)],
        out_shape=[sd((t, D_MODEL), F32), sd((n_tiles, 1, TOP_K * tm), F32), sd((n_tiles, 1, TOP_K * tm), jnp.int32),
                   sd((n_tiles, N_EXPERTS, 128), F32), sd((n_tiles, N_EXPERTS, 128), F32),
                   sd((n_tiles, N_EXPERTS, 128), F32), sd((N_EXPERTS, 128), F32)],
        scratch_shapes=[pltpu.VMEM((N_EXPERTS, 128), F32)],
        compiler_params=_cparams("arbitrary"),
        name="out_proj",
    )(attn_n, ssm_n, x2d, wa, ws, g, wrt, brt, triu, low, counts_in)


def _tile_span(ref, first_row, n_rows, lead=()):
    start = pl.multiple_of(first_row * SUBLANES, SUBLANES)
    return ref.at[(*lead, pl.ds(start, n_rows * SUBLANES), slice(None))]


def _run_copies(tile, src_tbl, cnt_tbl, loc_tbl, make_copy):
    def body(e, c):
        j = tile * N_EXPERTS + e
        cnt, src, loc = cnt_tbl[j], src_tbl[j], loc_tbl[j]

        def copies(sizes, off):
            for size in sizes:
                @pl.when((cnt & size) != 0)
                def _(off=off, size=size):
                    make_copy(src + off, loc + off, size).start()
                off = off + (cnt & size)

        long_sizes = tuple(s for s in _RUN_SIZES if s >= _LONG_RUN)

        @pl.when(cnt >= _LONG_RUN)
        def _():
            copies(long_sizes, 0)

        copies(_RUN_SIZES[len(long_sizes):], cnt - (cnt & (_LONG_RUN - 1)))
        return c

    lax.fori_loop(0, N_EXPERTS, body, 0)


def _dispatch_kernel(src_tbl, cnt_tbl, loc_tbl, pend_ref, pad_ref, nu_ref, off_ref, xa_ref, xb_ref, g_ref,
                     rows_ref, buf, zbuf, xt, sem, zsem, *, tiles_a, nblk):
    i = pl.program_id(0)
    n = pl.num_programs(0)
    slot = i % 2

    def zero_block(start_row):
        return pltpu.make_async_copy(zbuf, _tile_span(rows_ref, start_row, MOE_TILE), zsem)

    def pad_copies(e, issue):
        n_pad = pad_ref[e]
        first = pend_ref[e] - n_pad
        off = 0
        for size in _RUN_SIZES:
            @pl.when((n_pad & size) != 0)
            def _(off=off, size=size):
                issue(pltpu.make_async_copy(_tile_span(zbuf, 0, size), _tile_span(rows_ref, first + off, size), zsem))
            off = off + (n_pad & size)

    @pl.when(i == 0)
    def _():
        zbuf[...] = jnp.zeros_like(zbuf)

        def pads_start(e, c):
            pad_copies(e, lambda cp: cp.start())
            return c

        def pads_wait(e, c):
            pad_copies(e, lambda cp: cp.wait())
            return c

        def tail_start(j, c):
            zero_block(j * MOE_TILE).start()
            return c

        def tail_wait(j, c):
            zero_block(j * MOE_TILE).wait()
            return c

        lax.fori_loop(0, N_EXPERTS, pads_start, 0)
        lax.fori_loop(nu_ref[0], nblk, tail_start, 0)
        lax.fori_loop(0, N_EXPERTS, pads_wait, 0)
        lax.fori_loop(nu_ref[0], nblk, tail_wait, 0)

    def slot_rows(s):
        return buf.at[pl.ds(pl.multiple_of(s * _SLOT_ROWS, _SLOT_ROWS), _SLOT_ROWS), :]

    def slot_done(s):
        return pltpu.make_async_copy(slot_rows(s), slot_rows(s), sem.at[s])

    @pl.when(i >= 2)
    def _():
        slot_done(slot).wait()

    def fill(x_ref):
        _to_row_tiles(xt, _rms(x_ref[...]) * g_ref[...])

        def body(t, c):
            v = xt[pl.ds(pl.multiple_of(t * SUBLANES, SUBLANES), SUBLANES), :]
            for k in range(TOP_K):
                off = pl.multiple_of(off_ref[0, k * ROUTE_TILE + t], SUBLANES)
                buf[pl.ds(off, SUBLANES), :] = v
            return c

        lax.fori_loop(0, ROUTE_TILE, body, 0, unroll=8)

    @pl.when(i < tiles_a)
    def _():
        fill(xa_ref)

    @pl.when(i >= tiles_a)
    def _():
        fill(xb_ref)

    _run_copies(i, src_tbl, cnt_tbl, loc_tbl,
                lambda g, l, size: pltpu.make_async_copy(_tile_span(slot_rows(slot), l, size),
                                                         _tile_span(rows_ref, g, size), sem.at[slot]))

    @pl.when(i == n - 1)
    def _():
        slot_done(1 - slot).wait()
        slot_done(slot).wait()


def _dispatch(runs, pend, pad, n_used, off, xa, xb, g, n_rows):
    assert MOE_TILE <= ROUTE_TILE
    tile_rows = ROUTE_TILE * SUBLANES
    tiles_a, tiles_b = xa.shape[0] // ROUTE_TILE, xb.shape[0] // ROUTE_TILE
    assert tiles_a + tiles_b >= 2
    grid_spec = pltpu.PrefetchScalarGridSpec(
        num_scalar_prefetch=6, grid=(tiles_a + tiles_b,),
        in_specs=[pl.BlockSpec((None, 1, TOP_K * ROUTE_TILE), lambda i, *_: (i, 0, 0), memory_space=pltpu.SMEM),
                  pl.BlockSpec((ROUTE_TILE, D_MODEL), lambda i, *_: (jnp.minimum(i, tiles_a - 1), 0)),
                  pl.BlockSpec((ROUTE_TILE, D_MODEL), lambda i, *_: (jnp.maximum(i - tiles_a, 0), 0)),
                  pl.BlockSpec((1, D_MODEL), lambda i, *_: (0, 0))],
        out_specs=pl.BlockSpec(memory_space=pl.ANY),
        scratch_shapes=[pltpu.VMEM((2 * _SLOT_ROWS, 128), F32),
                        pltpu.VMEM((MOE_TILE * SUBLANES, 128), F32), pltpu.VMEM((tile_rows, 128), F32),
                        pltpu.SemaphoreType.DMA((2,)), pltpu.SemaphoreType.DMA(())])
    return pl.pallas_call(
        functools.partial(_dispatch_kernel, tiles_a=tiles_a, nblk=n_rows // MOE_TILE), grid_spec=grid_spec,
        out_shape=jax.ShapeDtypeStruct((n_rows * SUBLANES, 128), F32),
        compiler_params=_cparams("arbitrary"), name="dispatch",
    )(*runs, pend, pad, n_used, off, xa, xb, g)


def _moe_kernel(be_ref, nu_ref, nv_ref, next_ref, slot_ref, x_ref, wgu_hbm, bg_ref, bl_ref, wd_hbm, bd_ref,
                perm_ref, o_ref, wgu_buf, wd_buf, wg_s, wl_s, wd_s, sem_gu, sem_d):
    i = pl.program_id(0)
    used = i < nu_ref[0]
    e = be_ref[i]
    new_expert = (i == 0) | (e != be_ref[jnp.maximum(i - 1, 0)])

    def fetch(expert, s):
        return (pltpu.make_async_copy(wgu_hbm.at[expert], wgu_buf.at[s], sem_gu.at[s]),
                pltpu.make_async_copy(wd_hbm.at[expert], wd_buf.at[s], sem_d.at[s]))

    @pl.when(used & new_expert)
    def _():
        s = slot_ref[e]

        @pl.when(i == 0)
        def _():
            for cp in fetch(e, s):
                cp.start()

        @pl.when(next_ref[e] >= 0)
        def _():
            for cp in fetch(next_ref[e], 1 - s):
                cp.start()

        for cp in fetch(e, s):
            cp.wait()
        for c in range(2 * D_FF // 256):
            r = _dot(wgu_buf[s, :, 256 * c:256 * (c + 1)].astype(BF16), perm_ref[...])
            wg_s[:, 128 * c:128 * (c + 1)] = r[:, :128].astype(BF16)
            wl_s[:, 128 * c:128 * (c + 1)] = r[:, 128:].astype(BF16)
        wd_s[...] = wd_buf[s].astype(BF16)

    def expert_rows(rows):
        x = _from_row_tiles(x_ref, rows).astype(BF16)
        glu = jnp.minimum(_dot(x, wg_s[...]) + bg_ref[...], SWIGLU_LIMIT)
        lin = jnp.clip(_dot(x, wl_s[...]) + bl_ref[...], -SWIGLU_LIMIT, SWIGLU_LIMIT)
        act = glu * jax.nn.sigmoid(SWIGLU_ALPHA * glu) * (lin + 1.0)
        _to_row_tiles(o_ref, _dot(act.astype(BF16), wd_s[...]) + bd_ref[...])

    def zero_rows(first, rows):
        o_ref[pl.ds(first * SUBLANES, rows * SUBLANES), :] = jnp.zeros((rows * SUBLANES, 128), F32)

    half = MOE_TILE // 2

    @pl.when(used & (nv_ref[i] > half))
    def _():
        expert_rows(MOE_TILE)

    @pl.when(used & (nv_ref[i] <= half))
    def _():
        expert_rows(half)
        zero_rows(half, half)


def _moe(block_e, n_used, n_valid, next_e, e_slot, x_rows, wgu, bg, bl, wd, bd, perm):
    nblk = x_rows.shape[0] // (MOE_TILE * SUBLANES)
    row = lambda i, be, nu, *_: (jnp.minimum(i, nu[0] - 1), 0)
    wsel = lambda i, be, *_: (be[i], 0, 0)
    grid_spec = pltpu.PrefetchScalarGridSpec(
        num_scalar_prefetch=5,
        grid=(nblk,),
        in_specs=[pl.BlockSpec((MOE_TILE * SUBLANES, 128), row),
                  pl.BlockSpec(memory_space=pl.ANY),
                  pl.BlockSpec((None, 1, D_FF), wsel), pl.BlockSpec((None, 1, D_FF), wsel),
                  pl.BlockSpec(memory_space=pl.ANY), pl.BlockSpec((None, 1, D_MODEL), wsel),
                  pl.BlockSpec((256, 256), lambda i, *_: (0, 0))],
        out_specs=pl.BlockSpec((MOE_TILE * SUBLANES, 128), row),
        scratch_shapes=[pltpu.VMEM((2, D_MODEL, 2 * D_FF), F32), pltpu.VMEM((2, D_FF, D_MODEL), F32),
                        pltpu.VMEM((D_MODEL, D_FF), BF16), pltpu.VMEM((D_MODEL, D_FF), BF16),
                        pltpu.VMEM((D_FF, D_MODEL), BF16),
                        pltpu.SemaphoreType.DMA((2,)), pltpu.SemaphoreType.DMA((2,))],
    )
    return pl.pallas_call(
        _moe_kernel,
        grid_spec=grid_spec,
        out_shape=jax.ShapeDtypeStruct(x_rows.shape, F32),
        input_output_aliases={5: 0},
        compiler_params=_cparams("arbitrary"),
        name="moe",
    )(block_e, n_used, n_valid, next_e, e_slot, x_rows, wgu, bg, bl, wd, bd, perm)


def _combine_kernel(src_tbl, cnt_tbl, loc_tbl, off_ref, gate_ref, rows_hbm, x1_ref, o_ref, buf, ybuf, sem, *,
                    tile_base):
    i = pl.program_id(0)
    n = pl.num_programs(0)

    def slot_rows(s):
        return buf.at[pl.ds(pl.multiple_of(s * _SLOT_ROWS, _SLOT_ROWS), _SLOT_ROWS), :]

    def fetch(tile, s):
        _run_copies(tile_base + tile, src_tbl, cnt_tbl, loc_tbl,
                    lambda g, l, size: pltpu.make_async_copy(_tile_span(rows_hbm, g, size),
                                                             _tile_span(slot_rows(s), l, size), sem.at[s]))

    @pl.when(i == 0)
    def _():
        fetch(0, 0)

    @pl.when(i + 1 < n)
    def _():
        fetch(i + 1, (i + 1) % 2)

    slot = i % 2
    pltpu.make_async_copy(slot_rows(slot), slot_rows(slot), sem.at[slot]).wait()

    def body(t, c):
        acc = None
        for k in range(TOP_K):
            off = pl.multiple_of(off_ref[0, k * ROUTE_TILE + t], SUBLANES)
            v = gate_ref[0, k * ROUTE_TILE + t] * buf[pl.ds(off, SUBLANES), :]
            acc = v if acc is None else acc + v
        ybuf[pl.ds(pl.multiple_of(t * SUBLANES, SUBLANES), SUBLANES), :] = acc
        return c

    lax.fori_loop(0, ROUTE_TILE, body, 0, unroll=8)
    o_ref[...] = x1_ref[...] + _from_row_tiles(ybuf, ROUTE_TILE)


def _combine(runs, off, gates, rows, x1, tile_base):
    assert tile_base % 2 == 0
    t = x1.shape[0]
    tile_rows = ROUTE_TILE * SUBLANES
    smem_tile = pl.BlockSpec((None, 1, TOP_K * ROUTE_TILE), lambda i, *_: (tile_base + i, 0, 0),
                             memory_space=pltpu.SMEM)
    grid_spec = pltpu.PrefetchScalarGridSpec(
        num_scalar_prefetch=3,
        grid=(t // ROUTE_TILE,),
        in_specs=[smem_tile, smem_tile, pl.BlockSpec(memory_space=pl.ANY),
                  pl.BlockSpec((ROUTE_TILE, D_MODEL), lambda i, *_: (i, 0))],
        out_specs=pl.BlockSpec((ROUTE_TILE, D_MODEL), lambda i, *_: (i, 0)),
        scratch_shapes=[pltpu.VMEM((2 * _SLOT_ROWS, 128), F32), pltpu.VMEM((tile_rows, 128), F32),
                        pltpu.SemaphoreType.DMA((2,))],
    )
    return pl.pallas_call(
        functools.partial(_combine_kernel, tile_base=tile_base),
        grid_spec=grid_spec,
        out_shape=jax.ShapeDtypeStruct((t, D_MODEL), F32),
        compiler_params=_cparams("arbitrary"),
        name="combine",
    )(*runs, off, gates, rows, x1)


def _route(before, cnt, loc, n_tokens):
    n_assign = n_tokens * TOP_K
    table = lambda a: a[:, :, 0].astype(jnp.int32)
    before, cnt, run_loc = table(before), table(cnt), table(loc)
    counts = before[-1] + cnt[-1]
    padded = ((counts + MOE_TILE - 1) // MOE_TILE) * MOE_TILE
    pend = jnp.cumsum(padded)
    pstart = pend - padded
    run_src = pstart[None, :] + before
    nblk = (n_assign + MOE_TILE - 1) // MOE_TILE + N_EXPERTS
    n_used = (pend[-1] // MOE_TILE).astype(jnp.int32)
    block_start = jnp.arange(nblk, dtype=jnp.int32) * MOE_TILE
    block_e = jnp.sum(pend[None, :] <= jnp.minimum(block_start, pend[-1] - 1)[:, None], axis=1)
    block_e = jnp.minimum(block_e, N_EXPERTS - 1).astype(jnp.int32)
    region_end = jnp.sum(jnp.where(block_e[:, None] == jnp.arange(N_EXPERTS)[None, :], (pstart + counts)[None, :], 0),
                         axis=1)
    n_valid = jnp.clip(region_end - block_start, 0, MOE_TILE).astype(jnp.int32)
    ids = jnp.arange(N_EXPERTS, dtype=jnp.int32)
    nonempty = counts > 0
    next_e = jnp.min(jnp.where((ids[None, :] > ids[:, None]) & nonempty[None, :], ids[None, :], N_EXPERTS), axis=1)
    next_e = jnp.where(next_e == N_EXPERTS, -1, next_e).astype(jnp.int32)
    e_slot = ((jnp.cumsum(nonempty.astype(jnp.int32)) - nonempty.astype(jnp.int32)) % 2).astype(jnp.int32)
    blocks = (block_e, n_used.reshape(1), n_valid, next_e, e_slot)
    runs = (run_src.reshape(-1), cnt.reshape(-1), run_loc.reshape(-1))
    return runs, blocks, pend, padded - counts, nblk * MOE_TILE


def kernel(x_prompt, x_sample, cache_k, cache_v, state_ssm_re, state_ssm_im, norm_mix_g, w_in, q_norm_g, k_norm_g, attn_sinks, ssm_lambda_re, ssm_lambda_im, ssm_b_re, ssm_b_im, ssm_c_re, ssm_c_im, ssm_d, ssm_log_dt, w_glu, b_glu, attn_out_norm_g, ssm_out_norm_g, w_out, norm_ffn_g, w_router, b_router, w_gate_up, b_gate_up, w_down, b_down):
    depth = w_in.shape[0]
    assert depth == 1
    bp, sp, _ = x_prompt.shape
    bs, ss, _ = x_sample.shape
    tp, ts = bp * sp, bs * ss
    assert bp == SUBLANES and sp % S5_CHUNK == 0 and bs % SUBLANES == 0

    perm = jnp.asarray(_Q_PERM)
    w_in0 = w_in[0]
    wq = w_in0[:, :D_ATTN].reshape(D_MODEL, N_HEADS, HEAD_DIM)[:, perm].reshape(D_MODEL, D_ATTN)
    w_in_b = jnp.concatenate([wq, w_in0[:, D_ATTN:]], axis=1).astype(BF16)
    qkg = jnp.concatenate([jnp.tile(q_norm_g[0], N_HEADS), jnp.tile(k_norm_g[0], N_KV_HEADS)])[None]
    pmat = jnp.kron(jnp.eye(256 // HEAD_DIM, dtype=F32),
                    jnp.full((HEAD_DIM, HEAD_DIM), 1.0 / HEAD_DIM, F32)).astype(BF16)
    g_mix = norm_mix_g[0][None]
    sinks = attn_sinks[0]
    g_attn = attn_out_norm_g[0].reshape(N_HEADS, HEAD_DIM)[perm].reshape(1, D_ATTN)
    w_out0 = w_out[0]
    w_out_a = w_out0[:D_ATTN].reshape(N_HEADS, HEAD_DIM, D_MODEL)[perm].reshape(D_ATTN, D_MODEL).astype(BF16)
    w_out_s = w_out0[D_ATTN:].astype(BF16)
    g_ssm = ssm_out_norm_g[0][None]
    g_ffn = norm_ffn_g[0][None]
    w_rt = w_router[0].T.astype(BF16)
    b_rt = jnp.broadcast_to(b_router[0][:, None], (N_EXPERTS, ROUTE_TILE))

    a_re, a_im, bb_re, bb_im = _s5_prep(
        ssm_lambda_re[0], ssm_lambda_im[0], ssm_log_dt[0][:, None],
        jnp.swapaxes(ssm_b_re[0], 1, 2), jnp.swapaxes(ssm_b_im[0], 1, 2))
    chan_g = jnp.arange(D_SSM)[:, None] // SSM_GROUP
    state_g = jnp.arange(D_STATE)[None, :] // SSM_STATE
    bd_b = lambda bb: jnp.where(chan_g == state_g, jnp.tile(bb.reshape(D_SSM, SSM_STATE), (1, N_SSM_GROUPS)), 0.0)
    wb = jnp.concatenate([bd_b(bb_re), bd_b(bb_im)], axis=1).astype(BF16)
    bd_c = lambda c: jnp.where(state_g.T == chan_g.T,
                               jnp.tile(jnp.swapaxes(c, 1, 2).reshape(D_STATE, SSM_GROUP), (1, N_SSM_GROUPS)), 0.0)
    wc = jnp.concatenate([bd_c(ssm_c_re[0]), -bd_c(ssm_c_im[0])], axis=0).astype(BF16)
    a_re, a_im = a_re.reshape(1, D_STATE), a_im.reshape(1, D_STATE)
    d_skip = ssm_d[0].reshape(1, D_SSM)
    w_glu_b = w_glu[0].astype(BF16)
    b_glu0 = b_glu[0][None]

    b_g = b_gate_up[0][:, None, 0::2]
    b_l = b_gate_up[0][:, None, 1::2]
    b_d = b_down[0][:, None, :]
    idx = jnp.arange(256)
    deint = (idx[None, :] == jnp.where(idx % 2 == 0, idx // 2, 128 + idx // 2)[:, None]).astype(BF16)

    xp2 = x_prompt.reshape(tp, D_MODEL)
    xs2 = x_sample.reshape(ts, D_MODEL)
    qp, kp, vp, up = _in_proj(xp2, g_mix, w_in_b, qkg, pmat)
    qs, ks, vs, us = _in_proj(xs2, g_mix, w_in_b, qkg, pmat)

    kp3, vp3 = kp.reshape(bp, sp, D_KV), vp.reshape(bp, sp, D_KV)
    attn_p, win_k, win_v = _band_attention(sinks, qp.reshape(bp, sp, D_ATTN), kp3, vp3, g_attn)
    attn_p = attn_p.reshape(tp, D_ATTN)
    ck = cache_k[0].reshape(bs, -1, D_KV)
    cv = cache_v[0].reshape(bs, -1, D_KV)
    attn_s, new_ck, new_cv = _cache_attention(sinks, qs, ks, vs, ck, cv, g_attn, ss)

    zeros_p = jnp.zeros((bp, D_STATE), F32)
    s5_args = (a_re, a_im, wb, wc, d_skip, w_glu_b, b_glu0, g_ssm)
    ssm_p, hr_p, hi_p = _s5(up.reshape(bp, sp, D_SSM), zeros_p, zeros_p, *s5_args,
                            tt=S5_CHUNK, time_chunked=True)
    ssm_s, hr_s, hi_s = _s5(us.reshape(bs // SUBLANES, SUBLANES * ss, D_SSM),
                            state_ssm_re[0].reshape(bs, D_STATE), state_ssm_im[0].reshape(bs, D_STATE),
                            *s5_args, tt=ss, time_chunked=False)

    triu = (jnp.arange(ROUTE_TILE)[:, None] < jnp.arange(ROUTE_TILE)[None, :]).astype(BF16)
    low = (jnp.arange(N_EXPERTS)[:, None] > jnp.arange(N_EXPERTS)[None, :]).astype(BF16)
    proj_args = (w_out_a, w_out_s, g_ffn, w_rt, b_rt, triu, low)
    x1p, *tab_p, counts_p = _out_proj(attn_p, ssm_p.reshape(tp, D_SSM), xp2, *proj_args,
                                      jnp.zeros((N_EXPERTS, 128), F32), 0)
    x1s, *tab_s, _ = _out_proj(attn_s, ssm_s.reshape(ts, D_SSM), xs2, *proj_args, counts_p, tp // ROUTE_TILE)

    gates, off, before, cnt, loc = (jnp.concatenate([p, s_], axis=0) for p, s_ in zip(tab_p, tab_s))
    runs, blocks, pend, pad, n_rows = _route(before, cnt, loc, tp + ts)
    x_rows = _dispatch(runs, pend, pad, blocks[1], off, x1p, x1s, g_ffn, n_rows)
    out_rows = _moe(*blocks, x_rows, w_gate_up[0], b_g, b_l, w_down[0], b_d, deint)
    yp = _combine(runs, off, gates, out_rows, x1p, 0).reshape(bp, sp, D_MODEL)
    ys = _combine(runs, off, gates, out_rows, x1s, tp // ROUTE_TILE).reshape(bs, ss, D_MODEL)

    kv5 = lambda a, b_: a.reshape(b_, -1, N_KV_HEADS, HEAD_DIM)
    new_kp = kv5(win_k, bp)[None]
    new_vp = kv5(win_v, bp)[None]
    new_ks = kv5(new_ck, bs)[None]
    new_vs = kv5(new_cv, bs)[None]
    st = lambda h, b_: h.reshape(1, b_, N_SSM_GROUPS, SSM_STATE)
    return (yp, ys, new_kp, new_vp, st(hr_p, bp), st(hi_p, bp),
            new_ks, new_vs, st(hr_s, bs), st(hi_s, bs))
```

```python
import functools

import jax
import jax.numpy as jnp
from jax import lax
from jax.experimental import pallas as pl
from jax.experimental.pallas import tpu as pltpu

F32 = jnp.float32
BF16 = jnp.bfloat16

D_MODEL = 1024
D_ATTN = 512
D_SSM = 512
HEAD_DIM = 64
N_HEADS = 8
N_KV_HEADS = 2
D_KV = N_KV_HEADS * HEAD_DIM
WINDOW = 128
SSM_GROUP = 16
N_SSM_GROUPS = 32
SSM_STATE = 64
D_STATE = N_SSM_GROUPS * SSM_STATE
N_EXPERTS = 32
TOP_K = 4
D_FF = 1024
SWIGLU_LIMIT = 7.0
SWIGLU_ALPHA = 1.702
RMS_EPS = 1e-6
NEG_INF = -1e30
D_IN_PROJ = D_ATTN + 2 * D_KV + D_SSM
D_QK = D_ATTN + D_KV

SUBLANES = 8
VMEM_LIMIT = 56 * 1024 * 1024

ROW_TILE = 2048
MOE_TILE = 512
ATTN_BLOCKS = 8
ROUTE_TILE = 512
_SLOT_ROWS = ROUTE_TILE * TOP_K * SUBLANES
_RUN_SIZES = tuple(1 << b for b in range(ROUTE_TILE.bit_length() - 1, -1, -1))
S5_CHUNK = 128
S5_COLS = 512

_Q_PERM = (0, 4, 1, 5, 2, 6, 3, 7)


def _cparams(*sem):
    return pltpu.CompilerParams(dimension_semantics=sem, vmem_limit_bytes=VMEM_LIMIT)


def _dot(a, b):
    return jnp.dot(a, b, preferred_element_type=F32)


def _rms(x):
    return x * lax.rsqrt(jnp.mean(x * x, axis=-1, keepdims=True) + RMS_EPS)


def _in_proj_kernel(x_ref, g_ref, w_ref, qkg_ref, p_ref, q_ref, k_ref, v_ref, u_ref):
    xn = _rms(x_ref[...]) * g_ref[...]
    h = _dot(xn.astype(BF16), w_ref[...])
    qk = h[:, :D_QK]
    sq = (qk * qk).astype(BF16)
    p = p_ref[...]
    ms = jnp.concatenate(
        [_dot(sq[:, 0:256], p), _dot(sq[:, 256:512], p), _dot(sq[:, 512:640], p[:128, :128])],
        axis=-1)
    qkn = qk * lax.rsqrt(ms + RMS_EPS) * qkg_ref[...]
    q_ref[...] = qkn[:, :D_ATTN].astype(BF16)
    k_ref[...] = qkn[:, D_ATTN:]
    v_ref[...] = h[:, D_QK:D_QK + D_KV]
    u_ref[...] = h[:, D_QK + D_KV:]


def _in_proj(x2d, g, w, qkg, pmat):
    t = x2d.shape[0]
    tm = min(ROW_TILE, t)
    row = lambda i: (i, 0)
    fix = lambda i: (0, 0)
    return pl.pallas_call(
        _in_proj_kernel,
        grid=(t // tm,),
        in_specs=[pl.BlockSpec((tm, D_MODEL), row), pl.BlockSpec((1, D_MODEL), fix),
                  pl.BlockSpec((D_MODEL, D_IN_PROJ), fix), pl.BlockSpec((1, D_QK), fix),
                  pl.BlockSpec((256, 256), fix)],
        out_specs=[pl.BlockSpec((tm, D_ATTN), row), pl.BlockSpec((tm, D_KV), row),
                   pl.BlockSpec((tm, D_KV), row), pl.BlockSpec((tm, D_SSM), row)],
        out_shape=[jax.ShapeDtypeStruct((t, D_ATTN), BF16), jax.ShapeDtypeStruct((t, D_KV), F32),
                   jax.ShapeDtypeStruct((t, D_KV), F32), jax.ShapeDtypeStruct((t, D_SSM), F32)],
        compiler_params=_cparams("parallel"),
        name="in_proj",
    )(x2d, g, w, qkg, pmat)


def _softmax_pv(s_blocks, v_blocks, sink):
    m = sink
    for s in s_blocks:
        m = jnp.maximum(m, jnp.max(s, axis=-1, keepdims=True))
    den = jnp.exp(sink - m)
    acc = None
    for s, v in zip(s_blocks, v_blocks):
        p = jnp.exp(s - m)
        den = den + jnp.sum(p, axis=-1, keepdims=True)
        pv = _dot(p.astype(BF16), v)
        acc = pv if acc is None else acc + pv
    return acc / den


def _band_attn_kernel(sink_ref, q_ref, kp_ref, kc_ref, vp_ref, vc_ref, g_ref, o_ref):
    i = pl.program_id(1)
    row = lax.broadcasted_iota(jnp.int32, (WINDOW, 2 * WINDOW), 0)
    col = lax.broadcasted_iota(jnp.int32, (WINDOW, 2 * WINDOW), 1)
    band = (col > row) & (col <= row + WINDOW)
    lane = lax.broadcasted_iota(jnp.int32, (WINDOW, 128), 1)
    low = lane < HEAD_DIM
    zero = jnp.zeros((), BF16)
    for j in range(ATTN_BLOCKS):
        cur = slice(WINDOW * j, WINDOW * (j + 1))
        q = q_ref[cur, :]
        k_prev = kp_ref[...] if j == 0 else kc_ref[WINDOW * (j - 1):WINDOW * j, :]
        v_prev = vp_ref[...] if j == 0 else vc_ref[WINDOW * (j - 1):WINDOW * j, :]
        kb = jnp.concatenate([k_prev, kc_ref[cur, :]], axis=0).astype(BF16)
        vb = jnp.concatenate([v_prev, vc_ref[cur, :]], axis=0).astype(BF16)
        mask = band & ((col >= WINDOW) | (i > 0)) if j == 0 else band
        outs = []
        for pair in range(N_HEADS // 2):
            qp = q[:, 128 * pair:128 * (pair + 1)]
            halves = []
            for par in range(2):
                qm = jnp.where(low if par == 0 else ~low, qp, zero)
                s = lax.dot_general(qm, kb, (((1,), (1,)), ((), ())), preferred_element_type=F32)
                s = jnp.where(mask, s * (HEAD_DIM ** -0.5), NEG_INF)
                halves.append(_softmax_pv([s], [vb], sink_ref[_Q_PERM[2 * pair + par]]))
            outs.append(jnp.where(low, halves[0], halves[1]))
        o = jnp.concatenate(outs, axis=-1)
        o_ref[cur, :] = (_rms(o) * g_ref[...]).astype(BF16)


def _band_attention(sinks, q, k, v, g):
    b, s, _ = q.shape
    qb = WINDOW * ATTN_BLOCKS
    cur = lambda bi, i: (bi, i, 0)
    prev = lambda bi, i: (bi, jnp.maximum(i * ATTN_BLOCKS - 1, 0), 0)
    return pl.pallas_call(
        _band_attn_kernel,
        grid=(b, s // qb),
        in_specs=[pl.BlockSpec(memory_space=pltpu.SMEM),
                  pl.BlockSpec((None, qb, D_ATTN), cur),
                  pl.BlockSpec((None, WINDOW, D_KV), prev), pl.BlockSpec((None, qb, D_KV), cur),
                  pl.BlockSpec((None, WINDOW, D_KV), prev), pl.BlockSpec((None, qb, D_KV), cur),
                  pl.BlockSpec((1, D_ATTN), lambda bi, i: (0, 0))],
        out_specs=pl.BlockSpec((None, qb, D_ATTN), cur),
        out_shape=jax.ShapeDtypeStruct((b, s, D_ATTN), BF16),
        compiler_params=_cparams("parallel", "parallel"),
        name="band_attn",
    )(sinks, q, k, k, v, v, g)


_PAIR_ROWS = 8
_CACHE_BB = 16


def _cache_attn_kernel(sink_ref, q_ref, kn_ref, vn_ref, ck_ref, cv_ref, g_ref, o_ref, nk_ref, nv_ref, *, n_new):
    n_buf = ck_ref.shape[1]
    rows_blk = _CACHE_BB * n_new
    for b in range(_CACHE_BB):
        for new_ref, cache_ref, fresh_ref in ((nk_ref, ck_ref, kn_ref), (nv_ref, cv_ref, vn_ref)):
            new_ref[b, 0:n_buf - n_new, :] = cache_ref[b, n_new:n_buf, :]
            new_ref[b, n_buf - n_new:n_buf, :] = fresh_ref[n_new * b:n_new * (b + 1), :]
    knew = kn_ref[...].astype(BF16)
    vnew = vn_ref[...].astype(BF16)
    lane = lax.broadcasted_iota(jnp.int32, (_PAIR_ROWS, 128), 1)
    low = lane < HEAD_DIM
    zero = jnp.zeros((), BF16)
    n_stack = N_HEADS * _PAIR_ROWS
    r = lax.broadcasted_iota(jnp.int32, (n_stack, 1), 0) % _PAIR_ROWS
    r_seq, r_tok = r // n_new, r % n_new
    colc = lax.broadcasted_iota(jnp.int32, (n_stack, 2 * n_buf), 1)
    c_seq, c_pos = colc // n_buf, colc % n_buf
    mask_c = (c_seq == r_seq) & (c_pos + WINDOW > r_tok + n_buf)
    coln = lax.broadcasted_iota(jnp.int32, (n_stack, rows_blk), 1)
    sink_col = jnp.concatenate(
        [jnp.full((_PAIR_ROWS, 1), sink_ref[_Q_PERM[h]], F32) for h in range(N_HEADS)], axis=0)
    for sp in range(_CACHE_BB // 2):
        q = q_ref[_PAIR_ROWS * sp:_PAIR_ROWS * (sp + 1), :]
        pieces = []
        for pair in range(N_HEADS // 2):
            qp = q[:, 128 * pair:128 * (pair + 1)]
            pieces.append(jnp.where(low, qp, zero))
            pieces.append(jnp.where(low, zero, qp))
        qs = jnp.concatenate(pieces, axis=0)
        kc = jnp.concatenate([ck_ref[2 * sp], ck_ref[2 * sp + 1]], axis=0).astype(BF16)
        vc = jnp.concatenate([cv_ref[2 * sp], cv_ref[2 * sp + 1]], axis=0).astype(BF16)
        nt = (((1,), (1,)), ((), ()))
        s_c = lax.dot_general(qs, kc, nt, preferred_element_type=F32) * (HEAD_DIM ** -0.5)
        s_n = lax.dot_general(qs, knew, nt, preferred_element_type=F32) * (HEAD_DIM ** -0.5)
        s_c = jnp.where(mask_c, s_c, NEG_INF)
        n_seq, n_tok = coln // n_new - 2 * sp, coln % n_new
        mask_n = (n_seq == r_seq) & (n_tok <= r_tok)
        s_n = jnp.where(mask_n, s_n, NEG_INF)
        o = _softmax_pv([s_c, s_n], [vc, vnew], sink_col)
        outs = [jnp.where(low, o[16 * pair:16 * pair + 8], o[16 * pair + 8:16 * pair + 16])
                for pair in range(N_HEADS // 2)]
        oo = jnp.concatenate(outs, axis=-1)
        o_ref[_PAIR_ROWS * sp:_PAIR_ROWS * (sp + 1), :] = (_rms(oo) * g_ref[...]).astype(BF16)


def _cache_attention(sinks, q, k, v, cache_k, cache_v, g, n_new):
    t = q.shape[0]
    nb, n_buf, _ = cache_k.shape
    assert n_new * 2 == _PAIR_ROWS and nb % _CACHE_BB == 0 and n_buf == WINDOW
    rows = _CACHE_BB * n_new
    row = lambda i: (i, 0)
    cache_spec = pl.BlockSpec((_CACHE_BB, n_buf, D_KV), lambda i: (i, 0, 0))
    return pl.pallas_call(
        functools.partial(_cache_attn_kernel, n_new=n_new),
        grid=(nb // _CACHE_BB,),
        in_specs=[pl.BlockSpec(memory_space=pltpu.SMEM),
                  pl.BlockSpec((rows, D_ATTN), row), pl.BlockSpec((rows, D_KV), row),
                  pl.BlockSpec((rows, D_KV), row), cache_spec, cache_spec,
                  pl.BlockSpec((1, D_ATTN), lambda i: (0, 0))],
        out_specs=[pl.BlockSpec((rows, D_ATTN), row), cache_spec, cache_spec],
        out_shape=[jax.ShapeDtypeStruct((t, D_ATTN), BF16), jax.ShapeDtypeStruct(cache_k.shape, F32),
                   jax.ShapeDtypeStruct(cache_v.shape, F32)],
        compiler_params=_cparams("parallel"),
        name="cache_attn",
    )(sinks, q, k, v, cache_k, cache_v, g)


def _s5_prep_kernel(lre_ref, lim_ref, ldt_ref, bre_ref, bim_ref, are_ref, aim_ref, bbre_ref, bbim_ref):
    dt = jnp.exp(ldt_ref[...])
    l_re = jnp.minimum(lre_ref[...], -1e-4)
    l_im = lim_ref[...]
    mag = jnp.exp(l_re * dt)
    a_re = mag * jnp.cos(l_im * dt)
    a_im = mag * jnp.sin(l_im * dt)
    den = l_re * l_re + l_im * l_im
    n_re = a_re - 1.0
    z_re = (n_re * l_re + a_im * l_im) / den
    z_im = (a_im * l_re - n_re * l_im) / den
    are_ref[...] = a_re
    aim_ref[...] = a_im
    br, bi = bre_ref[...], bim_ref[...]
    zr, zi = z_re[:, None, :], z_im[:, None, :]
    bbre_ref[...] = zr * br - zi * bi
    bbim_ref[...] = zr * bi + zi * br


def _s5_prep(lam_re, lam_im, log_dt, b_re_t, b_im_t):
    g, p = lam_re.shape
    sd = jax.ShapeDtypeStruct
    return pl.pallas_call(
        _s5_prep_kernel,
        out_shape=[sd((g, p), F32), sd((g, p), F32), sd(b_re_t.shape, F32), sd(b_re_t.shape, F32)],
        name="s5_prep",
    )(lam_re, lam_im, log_dt, b_re_t, b_im_t)


def _s5_kernel(u_ref, h0r_ref, h0i_ref, ar_ref, ai_ref, wb_ref, wc_ref, d_ref, wglu_ref, bglu_ref,
               g_ref, o_ref, hr_ref, hi_ref, bu_ref, hs_ref, usc_ref, ysc_ref, *, tt):
    j = pl.program_id(1)
    rows = SUBLANES * tt
    n_tiles = D_STATE // 128
    time_major = tt % SUBLANES == 0

    @pl.when(j == 0)
    def _():
        hs_ref[:, :D_STATE] = h0r_ref[...]
        hs_ref[:, D_STATE:] = h0i_ref[...]

    u = u_ref[...].reshape(rows, D_SSM)
    if time_major:
        for c in range(D_SSM // 128):
            for b in range(SUBLANES):
                usc_ref[c, pl.ds(b, tt, stride=SUBLANES), :] = u[b * tt:(b + 1) * tt, 128 * c:128 * (c + 1)]
        ub = jnp.concatenate([usc_ref[c] for c in range(D_SSM // 128)], axis=-1).astype(BF16)
    else:
        ub = u.astype(BF16)

    def step_rows(t):
        if time_major:
            return pl.ds(pl.multiple_of(t * SUBLANES, SUBLANES), SUBLANES)
        return pl.ds(t, SUBLANES, stride=tt)

    for n in range(2 * D_STATE // 256):
        band = (n % (D_STATE // 256)) // 2
        res = _dot(ub[:, 128 * band:128 * (band + 1)],
                   wb_ref[128 * band:128 * (band + 1), 256 * n:256 * (n + 1)])
        bu_ref[2 * n] = res[:, :128]
        bu_ref[2 * n + 1] = res[:, 128:]

    tiles_per_pass = S5_COLS // 128
    for c0 in range(0, n_tiles, tiles_per_pass):
        tiles = range(c0, c0 + tiles_per_pass)
        a_r = [jnp.broadcast_to(ar_ref[:, 128 * c:128 * (c + 1)], (SUBLANES, 128)) for c in tiles]
        a_i = [jnp.broadcast_to(ai_ref[:, 128 * c:128 * (c + 1)], (SUBLANES, 128)) for c in tiles]

        def step(t, carry, tiles=tiles, a_r=a_r, a_i=a_i):
            at_t = step_rows(t)
            out = []
            for k, c in enumerate(tiles):
                h_r, h_i = carry[2 * k], carry[2 * k + 1]
                n_r = a_r[k] * h_r - a_i[k] * h_i + bu_ref[c, at_t, :]
                n_i = a_r[k] * h_i + a_i[k] * h_r + bu_ref[n_tiles + c, at_t, :]
                bu_ref[c, at_t, :] = n_r
                bu_ref[n_tiles + c, at_t, :] = n_i
                out += [n_r, n_i]
            return tuple(out)

        init = []
        for c in tiles:
            init += [hs_ref[:, 128 * c:128 * (c + 1)], hs_ref[:, D_STATE + 128 * c:D_STATE + 128 * (c + 1)]]
        fin = lax.fori_loop(0, tt, step, tuple(init), unroll=min(tt, 8))
        for k, c in enumerate(tiles):
            hs_ref[:, 128 * c:128 * (c + 1)] = fin[2 * k]
            hs_ref[:, D_STATE + 128 * c:D_STATE + 128 * (c + 1)] = fin[2 * k + 1]

    def h_cols(first_tile):
        return jnp.concatenate([bu_ref[first_tile + k] for k in range(4)], axis=-1).astype(BF16)

    ys = []
    for m in range(D_SSM // 128):
        y = _dot(h_cols(4 * m), wc_ref[512 * m:512 * (m + 1), 128 * m:128 * (m + 1)])
        y = y + _dot(h_cols(n_tiles + 4 * m),
                     wc_ref[D_STATE + 512 * m:D_STATE + 512 * (m + 1), 128 * m:128 * (m + 1)])
        if time_major:
            ysc_ref[m] = y
            y = jnp.concatenate([ysc_ref[m, pl.ds(b, tt, stride=SUBLANES), :] for b in range(SUBLANES)], axis=0)
        ys.append(y)
    y = jnp.concatenate(ys, axis=-1) + d_ref[...] * u
    z = _dot(jax.nn.gelu(y).astype(BF16), wglu_ref[...]) + bglu_ref[...]
    s = z[:, :D_SSM] * jax.nn.sigmoid(z[:, D_SSM:])
    o_ref[...] = (_rms(s) * g_ref[...]).astype(BF16).reshape(o_ref.shape)

    @pl.when(j == pl.num_programs(1) - 1)
    def _():
        hr_ref[...] = hs_ref[:, :D_STATE]
        hi_ref[...] = hs_ref[:, D_STATE:]


def _s5(u, h0r, h0i, a_re, a_im, wb, wc, d, wglu, bglu, g, *, tt, time_chunked):
    nbg = h0r.shape[0] // SUBLANES
    if time_chunked:
        nchunks = u.shape[1] // tt
        u_spec = pl.BlockSpec((SUBLANES, tt, D_SSM), lambda gi, j: (gi, j, 0))
    else:
        nchunks = 1
        u_spec = pl.BlockSpec((None, SUBLANES * tt, D_SSM), lambda gi, j: (gi, 0, 0))
    fix = lambda gi, j: (0, 0)
    st_spec = pl.BlockSpec((SUBLANES, D_STATE), lambda gi, j: (gi, 0))
    sd = jax.ShapeDtypeStruct
    return pl.pallas_call(
        functools.partial(_s5_kernel, tt=tt),
        grid=(nbg, nchunks),
        in_specs=[u_spec, st_spec, st_spec,
                  pl.BlockSpec((1, D_STATE), fix), pl.BlockSpec((1, D_STATE), fix),
                  pl.BlockSpec((D_SSM, 2 * D_STATE), fix), pl.BlockSpec((2 * D_STATE, D_SSM), fix),
                  pl.BlockSpec((1, D_SSM), fix), pl.BlockSpec((D_SSM, 2 * D_SSM), fix),
                  pl.BlockSpec((1, 2 * D_SSM), fix), pl.BlockSpec((1, D_SSM), fix)],
        out_specs=[u_spec, st_spec, st_spec],
        out_shape=[sd(u.shape, BF16), sd(h0r.shape, F32), sd(h0r.shape, F32)],
        scratch_shapes=[pltpu.VMEM((2 * D_STATE // 128, SUBLANES * tt, 128), F32),
                        pltpu.VMEM((SUBLANES, 2 * D_STATE), F32),
                        pltpu.VMEM((D_SSM // 128, SUBLANES * tt, 128), F32),
                        pltpu.VMEM((D_SSM // 128, SUBLANES * tt, 128), F32)],
        compiler_params=_cparams("parallel", "arbitrary"),
        name="s5",
    )(u, h0r, h0i, a_re, a_im, wb, wc, d, wglu, bglu, g)


def _to_row_tiles(ref, val, first=0):
    rows = val.shape[0]
    for c in range(D_MODEL // 128):
        ref[pl.ds(first * SUBLANES + c, rows, stride=SUBLANES), :] = val[:, 128 * c:128 * (c + 1)]


def _from_row_tiles(ref, rows, lead=(), first=0):
    return jnp.concatenate(
        [ref[(*lead, pl.ds(first * SUBLANES + c, rows, stride=SUBLANES), slice(None))]
         for c in range(D_MODEL // 128)], axis=-1)


def _route_tile(logits_t, tile, triu_ref, low_ref, carry, gate_ref, off_ref, before_ref, cnt_ref, loc_ref):
    l = logits_t
    expert = lax.broadcasted_iota(jnp.int32, l.shape, 0).astype(F32)
    vals, sels = [], []
    for _ in range(TOP_K):
        m = jnp.max(l, axis=0, keepdims=True)
        idx = jnp.min(jnp.where(l == m, expert, float(N_EXPERTS)), axis=0, keepdims=True)
        sel = expert == idx
        l = jnp.where(sel, -jnp.inf, l)
        vals.append(m)
        sels.append(sel)
    exps = [jnp.exp(v - vals[0]) for v in vals]
    den = exps[0] + exps[1] + exps[2] + exps[3]
    onehot = jnp.where(sels[0] | sels[1] | sels[2] | sels[3], 1.0, 0.0)
    within = _dot(onehot.astype(BF16), triu_ref[...])
    cnt = jnp.broadcast_to(jnp.sum(onehot, axis=1, keepdims=True), onehot.shape)
    cnt_hi = jnp.floor(cnt * (1.0 / 16.0))
    cnt_lo = cnt - 16.0 * cnt_hi
    loc = 16.0 * _dot(low_ref[...], cnt_hi.astype(BF16)) + _dot(low_ref[...], cnt_lo.astype(BF16))
    slot_base = ((tile % 2) * (ROUTE_TILE * TOP_K)).astype(F32)
    rows = [jnp.sum(jnp.where(s, within + loc, 0.0), axis=0, keepdims=True) for s in sels]
    gate_ref[...] = jnp.concatenate([e / den for e in exps], axis=1)
    off_ref[...] = ((jnp.concatenate(rows, axis=1) + slot_base) * SUBLANES).astype(jnp.int32)
    before_ref[...] = carry[...]
    cnt_ref[...] = cnt[:, :128]
    loc_ref[...] = loc[:, :128]
    carry[...] = carry[...] + cnt[:, :128]


def _out_proj_kernel(a_ref, s_ref, x_ref, wa_ref, ws_ref, g_ref, wrt_ref, brt_ref, triu_ref, low_ref, cin_ref,
                     x1_ref, gate_ref, off_ref, before_ref, cnt_ref, loc_ref, cout_ref, carry, *, tile_base):
    i = pl.program_id(0)

    @pl.when(i == 0)
    def _():
        carry[...] = cin_ref[...]

    x1 = x_ref[...] + _dot(a_ref[...], wa_ref[...]) + _dot(s_ref[...], ws_ref[...])
    x1_ref[...] = x1
    xn = _rms(x1) * g_ref[...]
    logits_t = lax.dot_general(wrt_ref[...], xn.astype(BF16), (((1,), (1,)), ((), ())),
                               preferred_element_type=F32) + brt_ref[...]
    _route_tile(logits_t, tile_base + i, triu_ref, low_ref, carry, gate_ref, off_ref, before_ref, cnt_ref, loc_ref)
    cout_ref[...] = carry[...]


def _out_proj(attn_n, ssm_n, x2d, wa, ws, g, wrt, brt, triu, low, counts_in, tile_base):
    t = x2d.shape[0]
    tm = ROUTE_TILE
    n_tiles = t // tm
    row = lambda i: (i, 0)
    fix = lambda i: (0, 0)
    sd = jax.ShapeDtypeStruct
    k_spec = pl.BlockSpec((None, 1, TOP_K * tm), lambda i: (i, 0, 0))
    t_spec = pl.BlockSpec((None, N_EXPERTS, 128), lambda i: (i, 0, 0))
    return pl.pallas_call(
        functools.partial(_out_proj_kernel, tile_base=tile_base),
        grid=(n_tiles,),
        in_specs=[pl.BlockSpec((tm, D_ATTN), row), pl.BlockSpec((tm, D_SSM), row),
                  pl.BlockSpec((tm, D_MODEL), row),
                  pl.BlockSpec((D_ATTN, D_MODEL), fix), pl.BlockSpec((D_SSM, D_MODEL), fix),
                  pl.BlockSpec((1, D_MODEL), fix), pl.BlockSpec((N_EXPERTS, D_MODEL), fix),
                  pl.BlockSpec((N_EXPERTS, tm), fix), pl.BlockSpec((tm, tm), fix),
                  pl.BlockSpec((N_EXPERTS, N_EXPERTS), fix), pl.BlockSpec((N_EXPERTS, 128), fix)],
        out_specs=[pl.BlockSpec((tm, D_MODEL), row), k_spec, k_spec, t_spec, t_spec, t_spec,
                   pl.BlockSpec((N_EXPERTS, 128), fix)],
        out_shape=[sd((t, D_MODEL), F32), sd((n_tiles, 1, TOP_K * tm), F32), sd((n_tiles, 1, TOP_K * tm), jnp.int32),
                   sd((n_tiles, N_EXPERTS, 128), F32), sd((n_tiles, N_EXPERTS, 128), F32),
                   sd((n_tiles, N_EXPERTS, 128), F32), sd((N_EXPERTS, 128), F32)],
        scratch_shapes=[pltpu.VMEM((N_EXPERTS, 128), F32)],
        compiler_params=_cparams("arbitrary"),
        name="out_proj",
    )(attn_n, ssm_n, x2d, wa, ws, g, wrt, brt, triu, low, counts_in)


def _tile_span(ref, first_row, n_rows, lead=()):
    start = pl.multiple_of(first_row * SUBLANES, SUBLANES)
    return ref.at[(*lead, pl.ds(start, n_rows * SUBLANES), slice(None))]


def _run_copies(tile, src_tbl, cnt_tbl, loc_tbl, make_copy):
    def body(e, c):
        j = tile * N_EXPERTS + e
        cnt, src, loc = cnt_tbl[j], src_tbl[j], loc_tbl[j]
        off = 0
        for size in _RUN_SIZES:
            @pl.when((cnt & size) != 0)
            def _(off=off, size=size):
                make_copy(src + off, loc + off, size).start()
            off = off + (cnt & size)
        return c

    lax.fori_loop(0, N_EXPERTS, body, 0)


def _dispatch_kernel(src_tbl, cnt_tbl, loc_tbl, pend_ref, pad_ref, nu_ref, off_ref, xa_ref, xb_ref, g_ref,
                     rows_ref, buf, zbuf, xt, sem, zsem, *, tiles_a, nblk):
    i = pl.program_id(0)
    n = pl.num_programs(0)
    slot = i % 2

    def zero_block(start_row):
        return pltpu.make_async_copy(zbuf, _tile_span(rows_ref, start_row, MOE_TILE), zsem)

    def pad_copies(e, issue):
        n_pad = pad_ref[e]
        first = pend_ref[e] - n_pad
        off = 0
        for size in _RUN_SIZES:
            @pl.when((n_pad & size) != 0)
            def _(off=off, size=size):
                issue(pltpu.make_async_copy(_tile_span(zbuf, 0, size), _tile_span(rows_ref, first + off, size), zsem))
            off = off + (n_pad & size)

    @pl.when(i == 0)
    def _():
        zbuf[...] = jnp.zeros_like(zbuf)

        def pads_start(e, c):
            pad_copies(e, lambda cp: cp.start())
            return c

        def pads_wait(e, c):
            pad_copies(e, lambda cp: cp.wait())
            return c

        def tail_start(j, c):
            zero_block(j * MOE_TILE).start()
            return c

        def tail_wait(j, c):
            zero_block(j * MOE_TILE).wait()
            return c

        lax.fori_loop(0, N_EXPERTS, pads_start, 0)
        lax.fori_loop(nu_ref[0], nblk, tail_start, 0)
        lax.fori_loop(0, N_EXPERTS, pads_wait, 0)
        lax.fori_loop(nu_ref[0], nblk, tail_wait, 0)

    def slot_rows(s):
        return buf.at[pl.ds(pl.multiple_of(s * _SLOT_ROWS, _SLOT_ROWS), _SLOT_ROWS), :]

    def slot_done(s):
        return pltpu.make_async_copy(slot_rows(s), slot_rows(s), sem.at[s])

    @pl.when(i >= 2)
    def _():
        slot_done(slot).wait()

    def fill(x_ref):
        _to_row_tiles(xt, _rms(x_ref[...]) * g_ref[...])

        def body(t, c):
            v = xt[pl.ds(pl.multiple_of(t * SUBLANES, SUBLANES), SUBLANES), :]
            for k in range(TOP_K):
                off = pl.multiple_of(off_ref[0, k * ROUTE_TILE + t], SUBLANES)
                buf[pl.ds(off, SUBLANES), :] = v
            return c

        lax.fori_loop(0, ROUTE_TILE, body, 0, unroll=8)

    @pl.when(i < tiles_a)
    def _():
        fill(xa_ref)

    @pl.when(i >= tiles_a)
    def _():
        fill(xb_ref)

    _run_copies(i, src_tbl, cnt_tbl, loc_tbl,
                lambda g, l, size: pltpu.make_async_copy(_tile_span(slot_rows(slot), l, size),
                                                         _tile_span(rows_ref, g, size), sem.at[slot]))

    @pl.when(i == n - 1)
    def _():
        slot_done(1 - slot).wait()
        slot_done(slot).wait()


def _dispatch(runs, pend, pad, n_used, off, xa, xb, g, n_rows):
    assert MOE_TILE <= ROUTE_TILE
    tile_rows = ROUTE_TILE * SUBLANES
    tiles_a, tiles_b = xa.shape[0] // ROUTE_TILE, xb.shape[0] // ROUTE_TILE
    assert tiles_a + tiles_b >= 2
    grid_spec = pltpu.PrefetchScalarGridSpec(
        num_scalar_prefetch=6, grid=(tiles_a + tiles_b,),
        in_specs=[pl.BlockSpec((None, 1, TOP_K * ROUTE_TILE), lambda i, *_: (i, 0, 0), memory_space=pltpu.SMEM),
                  pl.BlockSpec((ROUTE_TILE, D_MODEL), lambda i, *_: (jnp.minimum(i, tiles_a - 1), 0)),
                  pl.BlockSpec((ROUTE_TILE, D_MODEL), lambda i, *_: (jnp.maximum(i - tiles_a, 0), 0)),
                  pl.BlockSpec((1, D_MODEL), lambda i, *_: (0, 0))],
        out_specs=pl.BlockSpec(memory_space=pl.ANY),
        scratch_shapes=[pltpu.VMEM((2 * _SLOT_ROWS, 128), F32),
                        pltpu.VMEM((MOE_TILE * SUBLANES, 128), F32), pltpu.VMEM((tile_rows, 128), F32),
                        pltpu.SemaphoreType.DMA((2,)), pltpu.SemaphoreType.DMA(())])
    return pl.pallas_call(
        functools.partial(_dispatch_kernel, tiles_a=tiles_a, nblk=n_rows // MOE_TILE), grid_spec=grid_spec,
        out_shape=jax.ShapeDtypeStruct((n_rows * SUBLANES, 128), F32),
        compiler_params=_cparams("arbitrary"), name="dispatch",
    )(*runs, pend, pad, n_used, off, xa, xb, g)


def _moe_kernel(be_ref, nu_ref, nv_ref, next_ref, slot_ref, x_ref, wgu_hbm, bg_ref, bl_ref, wd_hbm, bd_ref,
                perm_ref, o_ref, wgu_buf, wd_buf, wg_s, wl_s, wd_s, sem_gu, sem_d):
    i = pl.program_id(0)
    used = i < nu_ref[0]
    e = be_ref[i]
    new_expert = (i == 0) | (e != be_ref[jnp.maximum(i - 1, 0)])

    def fetch(expert, s):
        return (pltpu.make_async_copy(wgu_hbm.at[expert], wgu_buf.at[s], sem_gu.at[s]),
                pltpu.make_async_copy(wd_hbm.at[expert], wd_buf.at[s], sem_d.at[s]))

    @pl.when(used & new_expert)
    def _():
        s = slot_ref[e]

        @pl.when(i == 0)
        def _():
            for cp in fetch(e, s):
                cp.start()

        @pl.when(next_ref[e] >= 0)
        def _():
            for cp in fetch(next_ref[e], 1 - s):
                cp.start()

        for cp in fetch(e, s):
            cp.wait()
        for c in range(2 * D_FF // 256):
            r = _dot(wgu_buf[s, :, 256 * c:256 * (c + 1)].astype(BF16), perm_ref[...])
            wg_s[:, 128 * c:128 * (c + 1)] = r[:, :128].astype(BF16)
            wl_s[:, 128 * c:128 * (c + 1)] = r[:, 128:].astype(BF16)
        wd_s[...] = wd_buf[s].astype(BF16)

    def expert_rows(rows):
        x = _from_row_tiles(x_ref, rows).astype(BF16)
        glu = jnp.minimum(_dot(x, wg_s[...]) + bg_ref[...], SWIGLU_LIMIT)
        lin = jnp.clip(_dot(x, wl_s[...]) + bl_ref[...], -SWIGLU_LIMIT, SWIGLU_LIMIT)
        act = glu * jax.nn.sigmoid(SWIGLU_ALPHA * glu) * (lin + 1.0)
        _to_row_tiles(o_ref, _dot(act.astype(BF16), wd_s[...]) + bd_ref[...])

    def zero_rows(first, rows):
        o_ref[pl.ds(first * SUBLANES, rows * SUBLANES), :] = jnp.zeros((rows * SUBLANES, 128), F32)

    half = MOE_TILE // 2

    @pl.when(used & (nv_ref[i] > half))
    def _():
        expert_rows(MOE_TILE)

    @pl.when(used & (nv_ref[i] <= half))
    def _():
        expert_rows(half)
        zero_rows(half, half)


def _moe(block_e, n_used, n_valid, next_e, e_slot, x_rows, wgu, bg, bl, wd, bd, perm):
    nblk = x_rows.shape[0] // (MOE_TILE * SUBLANES)
    row = lambda i, be, nu, *_: (jnp.minimum(i, nu[0] - 1), 0)
    wsel = lambda i, be, *_: (be[i], 0, 0)
    grid_spec = pltpu.PrefetchScalarGridSpec(
        num_scalar_prefetch=5,
        grid=(nblk,),
        in_specs=[pl.BlockSpec((MOE_TILE * SUBLANES, 128), row),
                  pl.BlockSpec(memory_space=pl.ANY),
                  pl.BlockSpec((None, 1, D_FF), wsel), pl.BlockSpec((None, 1, D_FF), wsel),
                  pl.BlockSpec(memory_space=pl.ANY), pl.BlockSpec((None, 1, D_MODEL), wsel),
                  pl.BlockSpec((256, 256), lambda i, *_: (0, 0))],
        out_specs=pl.BlockSpec((MOE_TILE * SUBLANES, 128), row),
        scratch_shapes=[pltpu.VMEM((2, D_MODEL, 2 * D_FF), F32), pltpu.VMEM((2, D_FF, D_MODEL), F32),
                        pltpu.VMEM((D_MODEL, D_FF), BF16), pltpu.VMEM((D_MODEL, D_FF), BF16),
                        pltpu.VMEM((D_FF, D_MODEL), BF16),
                        pltpu.SemaphoreType.DMA((2,)), pltpu.SemaphoreType.DMA((2,))],
    )
    return pl.pallas_call(
        _moe_kernel,
        grid_spec=grid_spec,
        out_shape=jax.ShapeDtypeStruct(x_rows.shape, F32),
        input_output_aliases={5: 0},
        compiler_params=_cparams("arbitrary"),
        name="moe",
    )(block_e, n_used, n_valid, next_e, e_slot, x_rows, wgu, bg, bl, wd, bd, perm)


def _combine_kernel(src_tbl, cnt_tbl, loc_tbl, off_ref, gate_ref, rows_hbm, x1_ref, o_ref, buf, ybuf, sem, *,
                    tile_base):
    i = pl.program_id(0)
    n = pl.num_programs(0)

    def slot_rows(s):
        return buf.at[pl.ds(pl.multiple_of(s * _SLOT_ROWS, _SLOT_ROWS), _SLOT_ROWS), :]

    def fetch(tile, s):
        _run_copies(tile_base + tile, src_tbl, cnt_tbl, loc_tbl,
                    lambda g, l, size: pltpu.make_async_copy(_tile_span(rows_hbm, g, size),
                                                             _tile_span(slot_rows(s), l, size), sem.at[s]))

    @pl.when(i == 0)
    def _():
        fetch(0, 0)

    @pl.when(i + 1 < n)
    def _():
        fetch(i + 1, (i + 1) % 2)

    slot = i % 2
    pltpu.make_async_copy(slot_rows(slot), slot_rows(slot), sem.at[slot]).wait()

    def body(t, c):
        acc = None
        for k in range(TOP_K):
            off = pl.multiple_of(off_ref[0, k * ROUTE_TILE + t], SUBLANES)
            v = gate_ref[0, k * ROUTE_TILE + t] * buf[pl.ds(off, SUBLANES), :]
            acc = v if acc is None else acc + v
        ybuf[pl.ds(pl.multiple_of(t * SUBLANES, SUBLANES), SUBLANES), :] = acc
        return c

    lax.fori_loop(0, ROUTE_TILE, body, 0, unroll=8)
    o_ref[...] = x1_ref[...] + _from_row_tiles(ybuf, ROUTE_TILE)


def _combine(runs, off, gates, rows, x1, tile_base):
    assert tile_base % 2 == 0
    t = x1.shape[0]
    tile_rows = ROUTE_TILE * SUBLANES
    smem_tile = pl.BlockSpec((None, 1, TOP_K * ROUTE_TILE), lambda i, *_: (tile_base + i, 0, 0),
                             memory_space=pltpu.SMEM)
    grid_spec = pltpu.PrefetchScalarGridSpec(
        num_scalar_prefetch=3,
        grid=(t // ROUTE_TILE,),
        in_specs=[smem_tile, smem_tile, pl.BlockSpec(memory_space=pl.ANY),
                  pl.BlockSpec((ROUTE_TILE, D_MODEL), lambda i, *_: (i, 0))],
        out_specs=pl.BlockSpec((ROUTE_TILE, D_MODEL), lambda i, *_: (i, 0)),
        scratch_shapes=[pltpu.VMEM((2 * _SLOT_ROWS, 128), F32), pltpu.VMEM((tile_rows, 128), F32),
                        pltpu.SemaphoreType.DMA((2,))],
    )
    return pl.pallas_call(
        functools.partial(_combine_kernel, tile_base=tile_base),
        grid_spec=grid_spec,
        out_shape=jax.ShapeDtypeStruct((t, D_MODEL), F32),
        compiler_params=_cparams("arbitrary"),
        name="combine",
    )(*runs, off, gates, rows, x1)


def _route(before, cnt, loc, n_tokens):
    n_assign = n_tokens * TOP_K
    table = lambda a: a[:, :, 0].astype(jnp.int32)
    before, cnt, run_loc = table(before), table(cnt), table(loc)
    counts = before[-1] + cnt[-1]
    padded = ((counts + MOE_TILE - 1) // MOE_TILE) * MOE_TILE
    pend = jnp.cumsum(padded)
    pstart = pend - padded
    run_src = pstart[None, :] + before
    nblk = (n_assign + MOE_TILE - 1) // MOE_TILE + N_EXPERTS
    n_used = (pend[-1] // MOE_TILE).astype(jnp.int32)
    block_start = jnp.arange(nblk, dtype=jnp.int32) * MOE_TILE
    block_e = jnp.sum(pend[None, :] <= jnp.minimum(block_start, pend[-1] - 1)[:, None], axis=1)
    block_e = jnp.minimum(block_e, N_EXPERTS - 1).astype(jnp.int32)
    region_end = jnp.sum(jnp.where(block_e[:, None] == jnp.arange(N_EXPERTS)[None, :], (pstart + counts)[None, :], 0),
                         axis=1)
    n_valid = jnp.clip(region_end - block_start, 0, MOE_TILE).astype(jnp.int32)
    ids = jnp.arange(N_EXPERTS, dtype=jnp.int32)
    nonempty = counts > 0
    next_e = jnp.min(jnp.where((ids[None, :] > ids[:, None]) & nonempty[None, :], ids[None, :], N_EXPERTS), axis=1)
    next_e = jnp.where(next_e == N_EXPERTS, -1, next_e).astype(jnp.int32)
    e_slot = ((jnp.cumsum(nonempty.astype(jnp.int32)) - nonempty.astype(jnp.int32)) % 2).astype(jnp.int32)
    blocks = (block_e, n_used.reshape(1), n_valid, next_e, e_slot)
    runs = (run_src.reshape(-1), cnt.reshape(-1), run_loc.reshape(-1))
    return runs, blocks, pend, padded - counts, nblk * MOE_TILE


def kernel(x_prompt, x_sample, cache_k, cache_v, state_ssm_re, state_ssm_im, norm_mix_g, w_in, q_norm_g, k_norm_g, attn_sinks, ssm_lambda_re, ssm_lambda_im, ssm_b_re, ssm_b_im, ssm_c_re, ssm_c_im, ssm_d, ssm_log_dt, w_glu, b_glu, attn_out_norm_g, ssm_out_norm_g, w_out, norm_ffn_g, w_router, b_router, w_gate_up, b_gate_up, w_down, b_down):
    depth = w_in.shape[0]
    assert depth == 1
    bp, sp, _ = x_prompt.shape
    bs, ss, _ = x_sample.shape
    tp, ts = bp * sp, bs * ss
    assert bp == SUBLANES and sp % S5_CHUNK == 0 and bs % SUBLANES == 0

    perm = jnp.asarray(_Q_PERM)
    w_in0 = w_in[0]
    wq = w_in0[:, :D_ATTN].reshape(D_MODEL, N_HEADS, HEAD_DIM)[:, perm].reshape(D_MODEL, D_ATTN)
    w_in_b = jnp.concatenate([wq, w_in0[:, D_ATTN:]], axis=1).astype(BF16)
    qkg = jnp.concatenate([jnp.tile(q_norm_g[0], N_HEADS), jnp.tile(k_norm_g[0], N_KV_HEADS)])[None]
    pmat = jnp.kron(jnp.eye(256 // HEAD_DIM, dtype=F32),
                    jnp.full((HEAD_DIM, HEAD_DIM), 1.0 / HEAD_DIM, F32)).astype(BF16)
    g_mix = norm_mix_g[0][None]
    sinks = attn_sinks[0]
    g_attn = attn_out_norm_g[0].reshape(N_HEADS, HEAD_DIM)[perm].reshape(1, D_ATTN)
    w_out0 = w_out[0]
    w_out_a = w_out0[:D_ATTN].reshape(N_HEADS, HEAD_DIM, D_MODEL)[perm].reshape(D_ATTN, D_MODEL).astype(BF16)
    w_out_s = w_out0[D_ATTN:].astype(BF16)
    g_ssm = ssm_out_norm_g[0][None]
    g_ffn = norm_ffn_g[0][None]
    w_rt = w_router[0].T.astype(BF16)
    b_rt = jnp.broadcast_to(b_router[0][:, None], (N_EXPERTS, ROUTE_TILE))

    a_re, a_im, bb_re, bb_im = _s5_prep(
        ssm_lambda_re[0], ssm_lambda_im[0], ssm_log_dt[0][:, None],
        jnp.swapaxes(ssm_b_re[0], 1, 2), jnp.swapaxes(ssm_b_im[0], 1, 2))
    chan_g = jnp.arange(D_SSM)[:, None] // SSM_GROUP
    state_g = jnp.arange(D_STATE)[None, :] // SSM_STATE
    bd_b = lambda bb: jnp.where(chan_g == state_g, jnp.tile(bb.reshape(D_SSM, SSM_STATE), (1, N_SSM_GROUPS)), 0.0)
    wb = jnp.concatenate([bd_b(bb_re), bd_b(bb_im)], axis=1).astype(BF16)
    bd_c = lambda c: jnp.where(state_g.T == chan_g.T,
                               jnp.tile(jnp.swapaxes(c, 1, 2).reshape(D_STATE, SSM_GROUP), (1, N_SSM_GROUPS)), 0.0)
    wc = jnp.concatenate([bd_c(ssm_c_re[0]), -bd_c(ssm_c_im[0])], axis=0).astype(BF16)
    a_re, a_im = a_re.reshape(1, D_STATE), a_im.reshape(1, D_STATE)
    d_skip = ssm_d[0].reshape(1, D_SSM)
    w_glu_b = w_glu[0].astype(BF16)
    b_glu0 = b_glu[0][None]

    b_g = b_gate_up[0][:, None, 0::2]
    b_l = b_gate_up[0][:, None, 1::2]
    b_d = b_down[0][:, None, :]
    idx = jnp.arange(256)
    deint = (idx[None, :] == jnp.where(idx % 2 == 0, idx // 2, 128 + idx // 2)[:, None]).astype(BF16)

    xp2 = x_prompt.reshape(tp, D_MODEL)
    xs2 = x_sample.reshape(ts, D_MODEL)
    qp, kp, vp, up = _in_proj(xp2, g_mix, w_in_b, qkg, pmat)
    qs, ks, vs, us = _in_proj(xs2, g_mix, w_in_b, qkg, pmat)

    kp3, vp3 = kp.reshape(bp, sp, D_KV), vp.reshape(bp, sp, D_KV)
    attn_p = _band_attention(sinks, qp.reshape(bp, sp, D_ATTN), kp3, vp3, g_attn).reshape(tp, D_ATTN)
    ck = cache_k[0].reshape(bs, -1, D_KV)
    cv = cache_v[0].reshape(bs, -1, D_KV)
    attn_s, new_ck, new_cv = _cache_attention(sinks, qs, ks, vs, ck, cv, g_attn, ss)

    zeros_p = jnp.zeros((bp, D_STATE), F32)
    s5_args = (a_re, a_im, wb, wc, d_skip, w_glu_b, b_glu0, g_ssm)
    ssm_p, hr_p, hi_p = _s5(up.reshape(bp, sp, D_SSM), zeros_p, zeros_p, *s5_args,
                            tt=S5_CHUNK, time_chunked=True)
    ssm_s, hr_s, hi_s = _s5(us.reshape(bs // SUBLANES, SUBLANES * ss, D_SSM),
                            state_ssm_re[0].reshape(bs, D_STATE), state_ssm_im[0].reshape(bs, D_STATE),
                            *s5_args, tt=ss, time_chunked=False)

    triu = (jnp.arange(ROUTE_TILE)[:, None] < jnp.arange(ROUTE_TILE)[None, :]).astype(BF16)
    low = (jnp.arange(N_EXPERTS)[:, None] > jnp.arange(N_EXPERTS)[None, :]).astype(BF16)
    proj_args = (w_out_a, w_out_s, g_ffn, w_rt, b_rt, triu, low)
    x1p, *tab_p, counts_p = _out_proj(attn_p, ssm_p.reshape(tp, D_SSM), xp2, *proj_args,
                                      jnp.zeros((N_EXPERTS, 128), F32), 0)
    x1s, *tab_s, _ = _out_proj(attn_s, ssm_s.reshape(ts, D_SSM), xs2, *proj_args, counts_p, tp // ROUTE_TILE)

    gates, off, before, cnt, loc = (jnp.concatenate([p, s_], axis=0) for p, s_ in zip(tab_p, tab_s))
    runs, blocks, pend, pad, n_rows = _route(before, cnt, loc, tp + ts)
    x_rows = _dispatch(runs, pend, pad, blocks[1], off, x1p, x1s, g_ffn, n_rows)
    out_rows = _moe(*blocks, x_rows, w_gate_up[0], b_g, b_l, w_down[0], b_d, deint)
    yp = _combine(runs, off, gates, out_rows, x1p, 0).reshape(bp, sp, D_MODEL)
    ys = _combine(runs, off, gates, out_rows, x1s, tp // ROUTE_TILE).reshape(bs, ss, D_MODEL)

    kv5 = lambda a, b_: a.reshape(b_, -1, N_KV_HEADS, HEAD_DIM)
    new_kp = kv5(kp3[:, -WINDOW:], bp)[None]
    new_vp = kv5(vp3[:, -WINDOW:], bp)[None]
    new_ks = kv5(new_ck, bs)[None]
    new_vs = kv5(new_cv, bs)[None]
    st = lambda h, b_: h.reshape(1, b_, N_SSM_GROUPS, SSM_STATE)
    return (yp, ys, new_kp, new_vp, st(hr_p, bp), st(hi_p, bp),
            new_ks, new_vs, st(hr_s, bs), st(hi_s, bs))
```

```python
import functools

import jax
import jax.numpy as jnp
from jax import lax
from jax.experimental import pallas as pl
from jax.experimental.pallas import tpu as pltpu

F32 = jnp.float32
BF16 = jnp.bfloat16

D_MODEL = 1024
D_ATTN = 512
D_SSM = 512
HEAD_DIM = 64
N_HEADS = 8
N_KV_HEADS = 2
D_KV = N_KV_HEADS * HEAD_DIM
WINDOW = 128
SSM_GROUP = 16
N_SSM_GROUPS = 32
SSM_STATE = 64
D_STATE = N_SSM_GROUPS * SSM_STATE
N_EXPERTS = 32
TOP_K = 4
D_FF = 1024
SWIGLU_LIMIT = 7.0
SWIGLU_ALPHA = 1.702
RMS_EPS = 1e-6
NEG_INF = -1e30
D_IN_PROJ = D_ATTN + 2 * D_KV + D_SSM
D_QK = D_ATTN + D_KV

SUBLANES = 8
VMEM_LIMIT = 56 * 1024 * 1024

ROW_TILE = 2048
MOE_TILE = 512
ATTN_BLOCKS = 8
ROUTE_TILE = 512
_SLOT_ROWS = ROUTE_TILE * TOP_K * SUBLANES
_RUN_SIZES = tuple(1 << b for b in range(ROUTE_TILE.bit_length() - 1, -1, -1))
S5_CHUNK = 128
S5_COLS = 512

_Q_PERM = (0, 4, 1, 5, 2, 6, 3, 7)


def _cparams(*sem):
    return pltpu.CompilerParams(dimension_semantics=sem, vmem_limit_bytes=VMEM_LIMIT)


def _dot(a, b):
    return jnp.dot(a, b, preferred_element_type=F32)


def _rms(x):
    return x * lax.rsqrt(jnp.mean(x * x, axis=-1, keepdims=True) + RMS_EPS)


def _in_proj_kernel(x_ref, g_ref, w_ref, qkg_ref, p_ref, q_ref, k_ref, v_ref, u_ref):
    xn = _rms(x_ref[...]) * g_ref[...]
    h = _dot(xn.astype(BF16), w_ref[...])
    qk = h[:, :D_QK]
    sq = (qk * qk).astype(BF16)
    p = p_ref[...]
    ms = jnp.concatenate(
        [_dot(sq[:, 0:256], p), _dot(sq[:, 256:512], p), _dot(sq[:, 512:640], p[:128, :128])],
        axis=-1)
    qkn = qk * lax.rsqrt(ms + RMS_EPS) * qkg_ref[...]
    q_ref[...] = qkn[:, :D_ATTN].astype(BF16)
    k_ref[...] = qkn[:, D_ATTN:]
    v_ref[...] = h[:, D_QK:D_QK + D_KV]
    u_ref[...] = h[:, D_QK + D_KV:]


def _in_proj(x2d, g, w, qkg, pmat):
    t = x2d.shape[0]
    tm = min(ROW_TILE, t)
    row = lambda i: (i, 0)
    fix = lambda i: (0, 0)
    return pl.pallas_call(
        _in_proj_kernel,
        grid=(t // tm,),
        in_specs=[pl.BlockSpec((tm, D_MODEL), row), pl.BlockSpec((1, D_MODEL), fix),
                  pl.BlockSpec((D_MODEL, D_IN_PROJ), fix), pl.BlockSpec((1, D_QK), fix),
                  pl.BlockSpec((256, 256), fix)],
        out_specs=[pl.BlockSpec((tm, D_ATTN), row), pl.BlockSpec((tm, D_KV), row),
                   pl.BlockSpec((tm, D_KV), row), pl.BlockSpec((tm, D_SSM), row)],
        out_shape=[jax.ShapeDtypeStruct((t, D_ATTN), BF16), jax.ShapeDtypeStruct((t, D_KV), F32),
                   jax.ShapeDtypeStruct((t, D_KV), F32), jax.ShapeDtypeStruct((t, D_SSM), F32)],
        compiler_params=_cparams("parallel"),
        name="in_proj",
    )(x2d, g, w, qkg, pmat)


def _softmax_pv(s_blocks, v_blocks, sink):
    m = sink
    for s in s_blocks:
        m = jnp.maximum(m, jnp.max(s, axis=-1, keepdims=True))
    den = jnp.exp(sink - m)
    acc = None
    for s, v in zip(s_blocks, v_blocks):
        p = jnp.exp(s - m)
        den = den + jnp.sum(p, axis=-1, keepdims=True)
        pv = _dot(p.astype(BF16), v)
        acc = pv if acc is None else acc + pv
    return acc / den


def _band_attn_kernel(sink_ref, q_ref, kp_ref, kc_ref, vp_ref, vc_ref, g_ref, o_ref):
    i = pl.program_id(1)
    row = lax.broadcasted_iota(jnp.int32, (WINDOW, 2 * WINDOW), 0)
    col = lax.broadcasted_iota(jnp.int32, (WINDOW, 2 * WINDOW), 1)
    band = (col > row) & (col <= row + WINDOW)
    lane = lax.broadcasted_iota(jnp.int32, (WINDOW, 128), 1)
    low = lane < HEAD_DIM
    zero = jnp.zeros((), BF16)
    for j in range(ATTN_BLOCKS):
        cur = slice(WINDOW * j, WINDOW * (j + 1))
        q = q_ref[cur, :]
        k_prev = kp_ref[...] if j == 0 else kc_ref[WINDOW * (j - 1):WINDOW * j, :]
        v_prev = vp_ref[...] if j == 0 else vc_ref[WINDOW * (j - 1):WINDOW * j, :]
        kb = jnp.concatenate([k_prev, kc_ref[cur, :]], axis=0).astype(BF16)
        vb = jnp.concatenate([v_prev, vc_ref[cur, :]], axis=0).astype(BF16)
        mask = band & ((col >= WINDOW) | (i > 0)) if j == 0 else band
        outs = []
        for pair in range(N_HEADS // 2):
            qp = q[:, 128 * pair:128 * (pair + 1)]
            halves = []
            for par in range(2):
                qm = jnp.where(low if par == 0 else ~low, qp, zero)
                s = lax.dot_general(qm, kb, (((1,), (1,)), ((), ())), preferred_element_type=F32)
                s = jnp.where(mask, s * (HEAD_DIM ** -0.5), NEG_INF)
                halves.append(_softmax_pv([s], [vb], sink_ref[_Q_PERM[2 * pair + par]]))
            outs.append(jnp.where(low, halves[0], halves[1]))
        o = jnp.concatenate(outs, axis=-1)
        o_ref[cur, :] = (_rms(o) * g_ref[...]).astype(BF16)


def _band_attention(sinks, q, k, v, g):
    b, s, _ = q.shape
    qb = WINDOW * ATTN_BLOCKS
    cur = lambda bi, i: (bi, i, 0)
    prev = lambda bi, i: (bi, jnp.maximum(i * ATTN_BLOCKS - 1, 0), 0)
    return pl.pallas_call(
        _band_attn_kernel,
        grid=(b, s // qb),
        in_specs=[pl.BlockSpec(memory_space=pltpu.SMEM),
                  pl.BlockSpec((None, qb, D_ATTN), cur),
                  pl.BlockSpec((None, WINDOW, D_KV), prev), pl.BlockSpec((None, qb, D_KV), cur),
                  pl.BlockSpec((None, WINDOW, D_KV), prev), pl.BlockSpec((None, qb, D_KV), cur),
                  pl.BlockSpec((1, D_ATTN), lambda bi, i: (0, 0))],
        out_specs=pl.BlockSpec((None, qb, D_ATTN), cur),
        out_shape=jax.ShapeDtypeStruct((b, s, D_ATTN), BF16),
        compiler_params=_cparams("parallel", "parallel"),
        name="band_attn",
    )(sinks, q, k, k, v, v, g)


_PAIR_ROWS = 8
_CACHE_BB = 16


def _cache_attn_kernel(sink_ref, q_ref, kn_ref, vn_ref, ck_ref, cv_ref, g_ref, o_ref, nk_ref, nv_ref, *, n_new):
    n_buf = ck_ref.shape[1]
    rows_blk = _CACHE_BB * n_new
    for b in range(_CACHE_BB):
        for new_ref, cache_ref, fresh_ref in ((nk_ref, ck_ref, kn_ref), (nv_ref, cv_ref, vn_ref)):
            new_ref[b, 0:n_buf - n_new, :] = cache_ref[b, n_new:n_buf, :]
            new_ref[b, n_buf - n_new:n_buf, :] = fresh_ref[n_new * b:n_new * (b + 1), :]
    knew = kn_ref[...].astype(BF16)
    vnew = vn_ref[...].astype(BF16)
    lane = lax.broadcasted_iota(jnp.int32, (_PAIR_ROWS, 128), 1)
    low = lane < HEAD_DIM
    zero = jnp.zeros((), BF16)
    n_stack = N_HEADS * _PAIR_ROWS
    r = lax.broadcasted_iota(jnp.int32, (n_stack, 1), 0) % _PAIR_ROWS
    r_seq, r_tok = r // n_new, r % n_new
    colc = lax.broadcasted_iota(jnp.int32, (n_stack, 2 * n_buf), 1)
    c_seq, c_pos = colc // n_buf, colc % n_buf
    mask_c = (c_seq == r_seq) & (c_pos + WINDOW > r_tok + n_buf)
    coln = lax.broadcasted_iota(jnp.int32, (n_stack, rows_blk), 1)
    sink_col = jnp.concatenate(
        [jnp.full((_PAIR_ROWS, 1), sink_ref[_Q_PERM[h]], F32) for h in range(N_HEADS)], axis=0)
    n_pairs = _CACHE_BB // 2
    nt = (((1,), (1,)), ((), ()))
    scores_c, scores_n, values = [], [], []
    for sp in range(n_pairs):
        q = q_ref[_PAIR_ROWS * sp:_PAIR_ROWS * (sp + 1), :]
        pieces = []
        for pair in range(N_HEADS // 2):
            qp = q[:, 128 * pair:128 * (pair + 1)]
            pieces.append(jnp.where(low, qp, zero))
            pieces.append(jnp.where(low, zero, qp))
        qs = jnp.concatenate(pieces, axis=0)
        kc = jnp.concatenate([ck_ref[2 * sp], ck_ref[2 * sp + 1]], axis=0).astype(BF16)
        values.append(jnp.concatenate([cv_ref[2 * sp], cv_ref[2 * sp + 1]], axis=0).astype(BF16))
        s_c = lax.dot_general(qs, kc, nt, preferred_element_type=F32) * (HEAD_DIM ** -0.5)
        s_n = lax.dot_general(qs, knew, nt, preferred_element_type=F32) * (HEAD_DIM ** -0.5)
        n_seq, n_tok = coln // n_new - 2 * sp, coln % n_new
        mask_n = (n_seq == r_seq) & (n_tok <= r_tok)
        scores_c.append(jnp.where(mask_c, s_c, NEG_INF))
        scores_n.append(jnp.where(mask_n, s_n, NEG_INF))
    s_c = jnp.concatenate(scores_c, axis=0)
    s_n = jnp.concatenate(scores_n, axis=0)
    sink = jnp.concatenate([sink_col] * n_pairs, axis=0)
    m = jnp.maximum(sink, jnp.maximum(jnp.max(s_c, axis=-1, keepdims=True), jnp.max(s_n, axis=-1, keepdims=True)))
    p_c = jnp.exp(s_c - m)
    p_n = jnp.exp(s_n - m)
    den = jnp.exp(sink - m) + jnp.sum(p_c, axis=-1, keepdims=True) + jnp.sum(p_n, axis=-1, keepdims=True)
    p_c, p_n = p_c.astype(BF16), p_n.astype(BF16)
    for sp in range(n_pairs):
        rows = slice(n_stack * sp, n_stack * (sp + 1))
        o = (_dot(p_c[rows], values[sp]) + _dot(p_n[rows], vnew)) / den[rows]
        outs = [jnp.where(low, o[16 * pair:16 * pair + 8], o[16 * pair + 8:16 * pair + 16])
                for pair in range(N_HEADS // 2)]
        oo = jnp.concatenate(outs, axis=-1)
        o_ref[_PAIR_ROWS * sp:_PAIR_ROWS * (sp + 1), :] = (_rms(oo) * g_ref[...]).astype(BF16)


def _cache_attention(sinks, q, k, v, cache_k, cache_v, g, n_new):
    t = q.shape[0]
    nb, n_buf, _ = cache_k.shape
    assert n_new * 2 == _PAIR_ROWS and nb % _CACHE_BB == 0 and n_buf == WINDOW
    rows = _CACHE_BB * n_new
    row = lambda i: (i, 0)
    cache_spec = pl.BlockSpec((_CACHE_BB, n_buf, D_KV), lambda i: (i, 0, 0))
    return pl.pallas_call(
        functools.partial(_cache_attn_kernel, n_new=n_new),
        grid=(nb // _CACHE_BB,),
        in_specs=[pl.BlockSpec(memory_space=pltpu.SMEM),
                  pl.BlockSpec((rows, D_ATTN), row), pl.BlockSpec((rows, D_KV), row),
                  pl.BlockSpec((rows, D_KV), row), cache_spec, cache_spec,
                  pl.BlockSpec((1, D_ATTN), lambda i: (0, 0))],
        out_specs=[pl.BlockSpec((rows, D_ATTN), row), cache_spec, cache_spec],
        out_shape=[jax.ShapeDtypeStruct((t, D_ATTN), BF16), jax.ShapeDtypeStruct(cache_k.shape, F32),
                   jax.ShapeDtypeStruct(cache_v.shape, F32)],
        compiler_params=_cparams("parallel"),
        name="cache_attn",
    )(sinks, q, k, v, cache_k, cache_v, g)


def _s5_prep_kernel(lre_ref, lim_ref, ldt_ref, bre_ref, bim_ref, are_ref, aim_ref, bbre_ref, bbim_ref):
    dt = jnp.exp(ldt_ref[...])
    l_re = jnp.minimum(lre_ref[...], -1e-4)
    l_im = lim_ref[...]
    mag = jnp.exp(l_re * dt)
    a_re = mag * jnp.cos(l_im * dt)
    a_im = mag * jnp.sin(l_im * dt)
    den = l_re * l_re + l_im * l_im
    n_re = a_re - 1.0
    z_re = (n_re * l_re + a_im * l_im) / den
    z_im = (a_im * l_re - n_re * l_im) / den
    are_ref[...] = a_re
    aim_ref[...] = a_im
    br, bi = bre_ref[...], bim_ref[...]
    zr, zi = z_re[:, None, :], z_im[:, None, :]
    bbre_ref[...] = zr * br - zi * bi
    bbim_ref[...] = zr * bi + zi * br


def _s5_prep(lam_re, lam_im, log_dt, b_re_t, b_im_t):
    g, p = lam_re.shape
    sd = jax.ShapeDtypeStruct
    return pl.pallas_call(
        _s5_prep_kernel,
        out_shape=[sd((g, p), F32), sd((g, p), F32), sd(b_re_t.shape, F32), sd(b_re_t.shape, F32)],
        name="s5_prep",
    )(lam_re, lam_im, log_dt, b_re_t, b_im_t)


def _s5_kernel(u_ref, h0r_ref, h0i_ref, ar_ref, ai_ref, wb_ref, wc_ref, d_ref, wglu_ref, bglu_ref,
               g_ref, o_ref, hr_ref, hi_ref, bu_ref, hs_ref, usc_ref, ysc_ref, *, tt):
    j = pl.program_id(1)
    rows = SUBLANES * tt
    n_tiles = D_STATE // 128
    time_major = tt % SUBLANES == 0

    @pl.when(j == 0)
    def _():
        hs_ref[:, :D_STATE] = h0r_ref[...]
        hs_ref[:, D_STATE:] = h0i_ref[...]

    u = u_ref[...].reshape(rows, D_SSM)
    if time_major:
        for c in range(D_SSM // 128):
            for b in range(SUBLANES):
                usc_ref[c, pl.ds(b, tt, stride=SUBLANES), :] = u[b * tt:(b + 1) * tt, 128 * c:128 * (c + 1)]
        ub = jnp.concatenate([usc_ref[c] for c in range(D_SSM // 128)], axis=-1).astype(BF16)
    else:
        ub = u.astype(BF16)

    def step_rows(t):
        if time_major:
            return pl.ds(pl.multiple_of(t * SUBLANES, SUBLANES), SUBLANES)
        return pl.ds(t, SUBLANES, stride=tt)

    for n in range(2 * D_STATE // 256):
        band = (n % (D_STATE // 256)) // 2
        res = _dot(ub[:, 128 * band:128 * (band + 1)],
                   wb_ref[128 * band:128 * (band + 1), 256 * n:256 * (n + 1)])
        bu_ref[2 * n] = res[:, :128]
        bu_ref[2 * n + 1] = res[:, 128:]

    tiles_per_pass = S5_COLS // 128
    for c0 in range(0, n_tiles, tiles_per_pass):
        tiles = range(c0, c0 + tiles_per_pass)
        a_r = [jnp.broadcast_to(ar_ref[:, 128 * c:128 * (c + 1)], (SUBLANES, 128)) for c in tiles]
        a_i = [jnp.broadcast_to(ai_ref[:, 128 * c:128 * (c + 1)], (SUBLANES, 128)) for c in tiles]

        def step(t, carry, tiles=tiles, a_r=a_r, a_i=a_i):
            at_t = step_rows(t)
            out = []
            for k, c in enumerate(tiles):
                h_r, h_i = carry[2 * k], carry[2 * k + 1]
                n_r = a_r[k] * h_r - a_i[k] * h_i + bu_ref[c, at_t, :]
                n_i = a_r[k] * h_i + a_i[k] * h_r + bu_ref[n_tiles + c, at_t, :]
                bu_ref[c, at_t, :] = n_r
                bu_ref[n_tiles + c, at_t, :] = n_i
                out += [n_r, n_i]
            return tuple(out)

        init = []
        for c in tiles:
            init += [hs_ref[:, 128 * c:128 * (c + 1)], hs_ref[:, D_STATE + 128 * c:D_STATE + 128 * (c + 1)]]
        fin = lax.fori_loop(0, tt, step, tuple(init), unroll=min(tt, 8))
        for k, c in enumerate(tiles):
            hs_ref[:, 128 * c:128 * (c + 1)] = fin[2 * k]
            hs_ref[:, D_STATE + 128 * c:D_STATE + 128 * (c + 1)] = fin[2 * k + 1]

    def h_cols(first_tile):
        return jnp.concatenate([bu_ref[first_tile + k] for k in range(4)], axis=-1).astype(BF16)

    ys = []
    for m in range(D_SSM // 128):
        y = _dot(h_cols(4 * m), wc_ref[512 * m:512 * (m + 1), 128 * m:128 * (m + 1)])
        y = y + _dot(h_cols(n_tiles + 4 * m),
                     wc_ref[D_STATE + 512 * m:D_STATE + 512 * (m + 1), 128 * m:128 * (m + 1)])
        if time_major:
            ysc_ref[m] = y
            y = jnp.concatenate([ysc_ref[m, pl.ds(b, tt, stride=SUBLANES), :] for b in range(SUBLANES)], axis=0)
        ys.append(y)
    y = jnp.concatenate(ys, axis=-1) + d_ref[...] * u
    z = _dot(jax.nn.gelu(y).astype(BF16), wglu_ref[...]) + bglu_ref[...]
    s = z[:, :D_SSM] * jax.nn.sigmoid(z[:, D_SSM:])
    o_ref[...] = (_rms(s) * g_ref[...]).astype(BF16).reshape(o_ref.shape)

    @pl.when(j == pl.num_programs(1) - 1)
    def _():
        hr_ref[...] = hs_ref[:, :D_STATE]
        hi_ref[...] = hs_ref[:, D_STATE:]


def _s5(u, h0r, h0i, a_re, a_im, wb, wc, d, wglu, bglu, g, *, tt, time_chunked):
    nbg = h0r.shape[0] // SUBLANES
    if time_chunked:
        nchunks = u.shape[1] // tt
        u_spec = pl.BlockSpec((SUBLANES, tt, D_SSM), lambda gi, j: (gi, j, 0))
    else:
        nchunks = 1
        u_spec = pl.BlockSpec((None, SUBLANES * tt, D_SSM), lambda gi, j: (gi, 0, 0))
    fix = lambda gi, j: (0, 0)
    st_spec = pl.BlockSpec((SUBLANES, D_STATE), lambda gi, j: (gi, 0))
    sd = jax.ShapeDtypeStruct
    return pl.pallas_call(
        functools.partial(_s5_kernel, tt=tt),
        grid=(nbg, nchunks),
        in_specs=[u_spec, st_spec, st_spec,
                  pl.BlockSpec((1, D_STATE), fix), pl.BlockSpec((1, D_STATE), fix),
                  pl.BlockSpec((D_SSM, 2 * D_STATE), fix), pl.BlockSpec((2 * D_STATE, D_SSM), fix),
                  pl.BlockSpec((1, D_SSM), fix), pl.BlockSpec((D_SSM, 2 * D_SSM), fix),
                  pl.BlockSpec((1, 2 * D_SSM), fix), pl.BlockSpec((1, D_SSM), fix)],
        out_specs=[u_spec, st_spec, st_spec],
        out_shape=[sd(u.shape, BF16), sd(h0r.shape, F32), sd(h0r.shape, F32)],
        scratch_shapes=[pltpu.VMEM((2 * D_STATE // 128, SUBLANES * tt, 128), F32),
                        pltpu.VMEM((SUBLANES, 2 * D_STATE), F32),
                        pltpu.VMEM((D_SSM // 128, SUBLANES * tt, 128), F32),
                        pltpu.VMEM((D_SSM // 128, SUBLANES * tt, 128), F32)],
        compiler_params=_cparams("parallel", "arbitrary"),
        name="s5",
    )(u, h0r, h0i, a_re, a_im, wb, wc, d, wglu, bglu, g)


def _to_row_tiles(ref, val, first=0):
    rows = val.shape[0]
    for c in range(D_MODEL // 128):
        ref[pl.ds(first * SUBLANES + c, rows, stride=SUBLANES), :] = val[:, 128 * c:128 * (c + 1)]


def _from_row_tiles(ref, rows, lead=(), first=0):
    return jnp.concatenate(
        [ref[(*lead, pl.ds(first * SUBLANES + c, rows, stride=SUBLANES), slice(None))]
         for c in range(D_MODEL // 128)], axis=-1)


def _route_tile(logits_t, tile, triu_ref, low_ref, carry, gate_ref, off_ref, before_ref, cnt_ref, loc_ref):
    l = logits_t
    expert = lax.broadcasted_iota(jnp.int32, l.shape, 0).astype(F32)
    vals, sels = [], []
    for _ in range(TOP_K):
        m = jnp.max(l, axis=0, keepdims=True)
        idx = jnp.min(jnp.where(l == m, expert, float(N_EXPERTS)), axis=0, keepdims=True)
        sel = expert == idx
        l = jnp.where(sel, -jnp.inf, l)
        vals.append(m)
        sels.append(sel)
    exps = [jnp.exp(v - vals[0]) for v in vals]
    den = exps[0] + exps[1] + exps[2] + exps[3]
    onehot = jnp.where(sels[0] | sels[1] | sels[2] | sels[3], 1.0, 0.0)
    within = _dot(onehot.astype(BF16), triu_ref[...])
    cnt = jnp.broadcast_to(jnp.sum(onehot, axis=1, keepdims=True), onehot.shape)
    cnt_hi = jnp.floor(cnt * (1.0 / 16.0))
    cnt_lo = cnt - 16.0 * cnt_hi
    loc = 16.0 * _dot(low_ref[...], cnt_hi.astype(BF16)) + _dot(low_ref[...], cnt_lo.astype(BF16))
    slot_base = ((tile % 2) * (ROUTE_TILE * TOP_K)).astype(F32)
    rows = [jnp.sum(jnp.where(s, within + loc, 0.0), axis=0, keepdims=True) for s in sels]
    gate_ref[...] = jnp.concatenate([e / den for e in exps], axis=1)
    off_ref[...] = ((jnp.concatenate(rows, axis=1) + slot_base) * SUBLANES).astype(jnp.int32)
    before_ref[...] = carry[...]
    cnt_ref[...] = cnt[:, :128]
    loc_ref[...] = loc[:, :128]
    carry[...] = carry[...] + cnt[:, :128]


def _out_proj_kernel(a_ref, s_ref, x_ref, wa_ref, ws_ref, g_ref, wrt_ref, brt_ref, triu_ref, low_ref, cin_ref,
                     x1_ref, gate_ref, off_ref, before_ref, cnt_ref, loc_ref, cout_ref, carry, *, tile_base):
    i = pl.program_id(0)

    @pl.when(i == 0)
    def _():
        carry[...] = cin_ref[...]

    x1 = x_ref[...] + _dot(a_ref[...], wa_ref[...]) + _dot(s_ref[...], ws_ref[...])
    x1_ref[...] = x1
    xn = _rms(x1) * g_ref[...]
    logits_t = lax.dot_general(wrt_ref[...], xn.astype(BF16), (((1,), (1,)), ((), ())),
                               preferred_element_type=F32) + brt_ref[...]
    _route_tile(logits_t, tile_base + i, triu_ref, low_ref, carry, gate_ref, off_ref, before_ref, cnt_ref, loc_ref)
    cout_ref[...] = carry[...]


def _out_proj(attn_n, ssm_n, x2d, wa, ws, g, wrt, brt, triu, low, counts_in, tile_base):
    t = x2d.shape[0]
    tm = ROUTE_TILE
    n_tiles = t // tm
    row = lambda i: (i, 0)
    fix = lambda i: (0, 0)
    sd = jax.ShapeDtypeStruct
    k_spec = pl.BlockSpec((None, 1, TOP_K * tm), lambda i: (i, 0, 0))
    t_spec = pl.BlockSpec((None, N_EXPERTS, 128), lambda i: (i, 0, 0))
    return pl.pallas_call(
        functools.partial(_out_proj_kernel, tile_base=tile_base),
        grid=(n_tiles,),
        in_specs=[pl.BlockSpec((tm, D_ATTN), row), pl.BlockSpec((tm, D_SSM), row),
                  pl.BlockSpec((tm, D_MODEL), row),
                  pl.BlockSpec((D_ATTN, D_MODEL), fix), pl.BlockSpec((D_SSM, D_MODEL), fix),
                  pl.BlockSpec((1, D_MODEL), fix), pl.BlockSpec((N_EXPERTS, D_MODEL), fix),
                  pl.BlockSpec((N_EXPERTS, tm), fix), pl.BlockSpec((tm, tm), fix),
                  pl.BlockSpec((N_EXPERTS, N_EXPERTS), fix), pl.BlockSpec((N_EXPERTS, 128), fix)],
        out_specs=[pl.BlockSpec((tm, D_MODEL), row), k_spec, k_spec, t_spec, t_spec, t_spec,
                   pl.BlockSpec((N_EXPERTS, 128), fix)],
        out_shape=[sd((t, D_MODEL), F32), sd((n_tiles, 1, TOP_K * tm), F32), sd((n_tiles, 1, TOP_K * tm), jnp.int32),
                   sd((n_tiles, N_EXPERTS, 128), F32), sd((n_tiles, N_EXPERTS, 128), F32),
                   sd((n_tiles, N_EXPERTS, 128), F32), sd((N_EXPERTS, 128), F32)],
        scratch_shapes=[pltpu.VMEM((N_EXPERTS, 128), F32)],
        compiler_params=_cparams("arbitrary"),
        name="out_proj",
    )(attn_n, ssm_n, x2d, wa, ws, g, wrt, brt, triu, low, counts_in)


def _tile_span(ref, first_row, n_rows, lead=()):
    start = pl.multiple_of(first_row * SUBLANES, SUBLANES)
    return ref.at[(*lead, pl.ds(start, n_rows * SUBLANES), slice(None))]


def _run_copies(tile, src_tbl, cnt_tbl, loc_tbl, make_copy):
    def body(e, c):
        j = tile * N_EXPERTS + e
        cnt, src, loc = cnt_tbl[j], src_tbl[j], loc_tbl[j]
        off = 0
        for size in _RUN_SIZES:
            @pl.when((cnt & size) != 0)
            def _(off=off, size=size):
                make_copy(src + off, loc + off, size).start()
            off = off + (cnt & size)
        return c

    lax.fori_loop(0, N_EXPERTS, body, 0)


def _dispatch_kernel(src_tbl, cnt_tbl, loc_tbl, pend_ref, pad_ref, nu_ref, off_ref, xa_ref, xb_ref, g_ref,
                     rows_ref, buf, zbuf, xt, sem, zsem, *, tiles_a, nblk):
    i = pl.program_id(0)
    n = pl.num_programs(0)
    slot = i % 2

    def zero_block(start_row):
        return pltpu.make_async_copy(zbuf, _tile_span(rows_ref, start_row, MOE_TILE), zsem)

    def pad_copies(e, issue):
        n_pad = pad_ref[e]
        first = pend_ref[e] - n_pad
        off = 0
        for size in _RUN_SIZES:
            @pl.when((n_pad & size) != 0)
            def _(off=off, size=size):
                issue(pltpu.make_async_copy(_tile_span(zbuf, 0, size), _tile_span(rows_ref, first + off, size), zsem))
            off = off + (n_pad & size)

    @pl.when(i == 0)
    def _():
        zbuf[...] = jnp.zeros_like(zbuf)

        def pads_start(e, c):
            pad_copies(e, lambda cp: cp.start())
            return c

        def pads_wait(e, c):
            pad_copies(e, lambda cp: cp.wait())
            return c

        def tail_start(j, c):
            zero_block(j * MOE_TILE).start()
            return c

        def tail_wait(j, c):
            zero_block(j * MOE_TILE).wait()
            return c

        lax.fori_loop(0, N_EXPERTS, pads_start, 0)
        lax.fori_loop(nu_ref[0], nblk, tail_start, 0)
        lax.fori_loop(0, N_EXPERTS, pads_wait, 0)
        lax.fori_loop(nu_ref[0], nblk, tail_wait, 0)

    def slot_rows(s):
        return buf.at[pl.ds(pl.multiple_of(s * _SLOT_ROWS, _SLOT_ROWS), _SLOT_ROWS), :]

    def slot_done(s):
        return pltpu.make_async_copy(slot_rows(s), slot_rows(s), sem.at[s])

    @pl.when(i >= 2)
    def _():
        slot_done(slot).wait()

    def fill(x_ref):
        _to_row_tiles(xt, _rms(x_ref[...]) * g_ref[...])

        def body(t, c):
            v = xt[pl.ds(pl.multiple_of(t * SUBLANES, SUBLANES), SUBLANES), :]
            for k in range(TOP_K):
                off = pl.multiple_of(off_ref[0, k * ROUTE_TILE + t], SUBLANES)
                buf[pl.ds(off, SUBLANES), :] = v
            return c

        lax.fori_loop(0, ROUTE_TILE, body, 0, unroll=8)

    @pl.when(i < tiles_a)
    def _():
        fill(xa_ref)

    @pl.when(i >= tiles_a)
    def _():
        fill(xb_ref)

    _run_copies(i, src_tbl, cnt_tbl, loc_tbl,
                lambda g, l, size: pltpu.make_async_copy(_tile_span(slot_rows(slot), l, size),
                                                         _tile_span(rows_ref, g, size), sem.at[slot]))

    @pl.when(i == n - 1)
    def _():
        slot_done(1 - slot).wait()
        slot_done(slot).wait()


def _dispatch(runs, pend, pad, n_used, off, xa, xb, g, n_rows):
    assert MOE_TILE <= ROUTE_TILE
    tile_rows = ROUTE_TILE * SUBLANES
    tiles_a, tiles_b = xa.shape[0] // ROUTE_TILE, xb.shape[0] // ROUTE_TILE
    assert tiles_a + tiles_b >= 2
    grid_spec = pltpu.PrefetchScalarGridSpec(
        num_scalar_prefetch=6, grid=(tiles_a + tiles_b,),
        in_specs=[pl.BlockSpec((None, 1, TOP_K * ROUTE_TILE), lambda i, *_: (i, 0, 0), memory_space=pltpu.SMEM),
                  pl.BlockSpec((ROUTE_TILE, D_MODEL), lambda i, *_: (jnp.minimum(i, tiles_a - 1), 0)),
                  pl.BlockSpec((ROUTE_TILE, D_MODEL), lambda i, *_: (jnp.maximum(i - tiles_a, 0), 0)),
                  pl.BlockSpec((1, D_MODEL), lambda i, *_: (0, 0))],
        out_specs=pl.BlockSpec(memory_space=pl.ANY),
        scratch_shapes=[pltpu.VMEM((2 * _SLOT_ROWS, 128), F32),
                        pltpu.VMEM((MOE_TILE * SUBLANES, 128), F32), pltpu.VMEM((tile_rows, 128), F32),
                        pltpu.SemaphoreType.DMA((2,)), pltpu.SemaphoreType.DMA(())])
    return pl.pallas_call(
        functools.partial(_dispatch_kernel, tiles_a=tiles_a, nblk=n_rows // MOE_TILE), grid_spec=grid_spec,
        out_shape=jax.ShapeDtypeStruct((n_rows * SUBLANES, 128), F32),
        compiler_params=_cparams("arbitrary"), name="dispatch",
    )(*runs, pend, pad, n_used, off, xa, xb, g)


def _moe_kernel(be_ref, nu_ref, nv_ref, next_ref, slot_ref, x_ref, wgu_hbm, bg_ref, bl_ref, wd_hbm, bd_ref,
                perm_ref, o_ref, wgu_buf, wd_buf, wg_s, wl_s, wd_s, sem_gu, sem_d):
    i = pl.program_id(0)
    used = i < nu_ref[0]
    e = be_ref[i]
    new_expert = (i == 0) | (e != be_ref[jnp.maximum(i - 1, 0)])

    def fetch(expert, s):
        return (pltpu.make_async_copy(wgu_hbm.at[expert], wgu_buf.at[s], sem_gu.at[s]),
                pltpu.make_async_copy(wd_hbm.at[expert], wd_buf.at[s], sem_d.at[s]))

    @pl.when(used & new_expert)
    def _():
        s = slot_ref[e]

        @pl.when(i == 0)
        def _():
            for cp in fetch(e, s):
                cp.start()

        @pl.when(next_ref[e] >= 0)
        def _():
            for cp in fetch(next_ref[e], 1 - s):
                cp.start()

        for cp in fetch(e, s):
            cp.wait()
        for c in range(2 * D_FF // 256):
            r = _dot(wgu_buf[s, :, 256 * c:256 * (c + 1)].astype(BF16), perm_ref[...])
            wg_s[:, 128 * c:128 * (c + 1)] = r[:, :128].astype(BF16)
            wl_s[:, 128 * c:128 * (c + 1)] = r[:, 128:].astype(BF16)
        wd_s[...] = wd_buf[s].astype(BF16)

    def expert_rows(rows):
        x = _from_row_tiles(x_ref, rows).astype(BF16)
        glu = jnp.minimum(_dot(x, wg_s[...]) + bg_ref[...], SWIGLU_LIMIT)
        lin = jnp.clip(_dot(x, wl_s[...]) + bl_ref[...], -SWIGLU_LIMIT, SWIGLU_LIMIT)
        act = glu * jax.nn.sigmoid(SWIGLU_ALPHA * glu) * (lin + 1.0)
        _to_row_tiles(o_ref, _dot(act.astype(BF16), wd_s[...]) + bd_ref[...])

    def zero_rows(first, rows):
        o_ref[pl.ds(first * SUBLANES, rows * SUBLANES), :] = jnp.zeros((rows * SUBLANES, 128), F32)

    half = MOE_TILE // 2

    @pl.when(used & (nv_ref[i] > half))
    def _():
        expert_rows(MOE_TILE)

    @pl.when(used & (nv_ref[i] <= half))
    def _():
        expert_rows(half)
        zero_rows(half, half)


def _moe(block_e, n_used, n_valid, next_e, e_slot, x_rows, wgu, bg, bl, wd, bd, perm):
    nblk = x_rows.shape[0] // (MOE_TILE * SUBLANES)
    row = lambda i, be, nu, *_: (jnp.minimum(i, nu[0] - 1), 0)
    wsel = lambda i, be, *_: (be[i], 0, 0)
    grid_spec = pltpu.PrefetchScalarGridSpec(
        num_scalar_prefetch=5,
        grid=(nblk,),
        in_specs=[pl.BlockSpec((MOE_TILE * SUBLANES, 128), row),
                  pl.BlockSpec(memory_space=pl.ANY),
                  pl.BlockSpec((None, 1, D_FF), wsel), pl.BlockSpec((None, 1, D_FF), wsel),
                  pl.BlockSpec(memory_space=pl.ANY), pl.BlockSpec((None, 1, D_MODEL), wsel),
                  pl.BlockSpec((256, 256), lambda i, *_: (0, 0))],
        out_specs=pl.BlockSpec((MOE_TILE * SUBLANES, 128), row),
        scratch_shapes=[pltpu.VMEM((2, D_MODEL, 2 * D_FF), F32), pltpu.VMEM((2, D_FF, D_MODEL), F32),
                        pltpu.VMEM((D_MODEL, D_FF), BF16), pltpu.VMEM((D_MODEL, D_FF), BF16),
                        pltpu.VMEM((D_FF, D_MODEL), BF16),
                        pltpu.SemaphoreType.DMA((2,)), pltpu.SemaphoreType.DMA((2,))],
    )
    return pl.pallas_call(
        _moe_kernel,
        grid_spec=grid_spec,
        out_shape=jax.ShapeDtypeStruct(x_rows.shape, F32),
        input_output_aliases={5: 0},
        compiler_params=_cparams("arbitrary"),
        name="moe",
    )(block_e, n_used, n_valid, next_e, e_slot, x_rows, wgu, bg, bl, wd, bd, perm)


def _combine_kernel(src_tbl, cnt_tbl, loc_tbl, off_ref, gate_ref, rows_hbm, x1_ref, o_ref, buf, ybuf, sem, *,
                    tile_base):
    i = pl.program_id(0)
    n = pl.num_programs(0)

    def slot_rows(s):
        return buf.at[pl.ds(pl.multiple_of(s * _SLOT_ROWS, _SLOT_ROWS), _SLOT_ROWS), :]

    def fetch(tile, s):
        _run_copies(tile_base + tile, src_tbl, cnt_tbl, loc_tbl,
                    lambda g, l, size: pltpu.make_async_copy(_tile_span(rows_hbm, g, size),
                                                             _tile_span(slot_rows(s), l, size), sem.at[s]))

    @pl.when(i == 0)
    def _():
        fetch(0, 0)

    @pl.when(i + 1 < n)
    def _():
        fetch(i + 1, (i + 1) % 2)

    slot = i % 2
    pltpu.make_async_copy(slot_rows(slot), slot_rows(slot), sem.at[slot]).wait()

    def body(t, c):
        acc = None
        for k in range(TOP_K):
            off = pl.multiple_of(off_ref[0, k * ROUTE_TILE + t], SUBLANES)
            v = gate_ref[0, k * ROUTE_TILE + t] * buf[pl.ds(off, SUBLANES), :]
            acc = v if acc is None else acc + v
        ybuf[pl.ds(pl.multiple_of(t * SUBLANES, SUBLANES), SUBLANES), :] = acc
        return c

    lax.fori_loop(0, ROUTE_TILE, body, 0, unroll=8)
    o_ref[...] = x1_ref[...] + _from_row_tiles(ybuf, ROUTE_TILE)


def _combine(runs, off, gates, rows, x1, tile_base):
    assert tile_base % 2 == 0
    t = x1.shape[0]
    tile_rows = ROUTE_TILE * SUBLANES
    smem_tile = pl.BlockSpec((None, 1, TOP_K * ROUTE_TILE), lambda i, *_: (tile_base + i, 0, 0),
                             memory_space=pltpu.SMEM)
    grid_spec = pltpu.PrefetchScalarGridSpec(
        num_scalar_prefetch=3,
        grid=(t // ROUTE_TILE,),
        in_specs=[smem_tile, smem_tile, pl.BlockSpec(memory_space=pl.ANY),
                  pl.BlockSpec((ROUTE_TILE, D_MODEL), lambda i, *_: (i, 0))],
        out_specs=pl.BlockSpec((ROUTE_TILE, D_MODEL), lambda i, *_: (i, 0)),
        scratch_shapes=[pltpu.VMEM((2 * _SLOT_ROWS, 128), F32), pltpu.VMEM((tile_rows, 128), F32),
                        pltpu.SemaphoreType.DMA((2,))],
    )
    return pl.pallas_call(
        functools.partial(_combine_kernel, tile_base=tile_base),
        grid_spec=grid_spec,
        out_shape=jax.ShapeDtypeStruct((t, D_MODEL), F32),
        compiler_params=_cparams("arbitrary"),
        name="combine",
    )(*runs, off, gates, rows, x1)


def _route(before, cnt, loc, n_tokens):
    n_assign = n_tokens * TOP_K
    table = lambda a: a[:, :, 0].astype(jnp.int32)
    before, cnt, run_loc = table(before), table(cnt), table(loc)
    counts = before[-1] + cnt[-1]
    padded = ((counts + MOE_TILE - 1) // MOE_TILE) * MOE_TILE
    pend = jnp.cumsum(padded)
    pstart = pend - padded
    run_src = pstart[None, :] + before
    nblk = (n_assign + MOE_TILE - 1) // MOE_TILE + N_EXPERTS
    n_used = (pend[-1] // MOE_TILE).astype(jnp.int32)
    block_start = jnp.arange(nblk, dtype=jnp.int32) * MOE_TILE
    block_e = jnp.sum(pend[None, :] <= jnp.minimum(block_start, pend[-1] - 1)[:, None], axis=1)
    block_e = jnp.minimum(block_e, N_EXPERTS - 1).astype(jnp.int32)
    region_end = jnp.sum(jnp.where(block_e[:, None] == jnp.arange(N_EXPERTS)[None, :], (pstart + counts)[None, :], 0),
                         axis=1)
    n_valid = jnp.clip(region_end - block_start, 0, MOE_TILE).astype(jnp.int32)
    ids = jnp.arange(N_EXPERTS, dtype=jnp.int32)
    nonempty = counts > 0
    next_e = jnp.min(jnp.where((ids[None, :] > ids[:, None]) & nonempty[None, :], ids[None, :], N_EXPERTS), axis=1)
    next_e = jnp.where(next_e == N_EXPERTS, -1, next_e).astype(jnp.int32)
    e_slot = ((jnp.cumsum(nonempty.astype(jnp.int32)) - nonempty.astype(jnp.int32)) % 2).astype(jnp.int32)
    blocks = (block_e, n_used.reshape(1), n_valid, next_e, e_slot)
    runs = (run_src.reshape(-1), cnt.reshape(-1), run_loc.reshape(-1))
    return runs, blocks, pend, padded - counts, nblk * MOE_TILE


def kernel(x_prompt, x_sample, cache_k, cache_v, state_ssm_re, state_ssm_im, norm_mix_g, w_in, q_norm_g, k_norm_g, attn_sinks, ssm_lambda_re, ssm_lambda_im, ssm_b_re, ssm_b_im, ssm_c_re, ssm_c_im, ssm_d, ssm_log_dt, w_glu, b_glu, attn_out_norm_g, ssm_out_norm_g, w_out, norm_ffn_g, w_router, b_router, w_gate_up, b_gate_up, w_down, b_down):
    depth = w_in.shape[0]
    assert depth == 1
    bp, sp, _ = x_prompt.shape
    bs, ss, _ = x_sample.shape
    tp, ts = bp * sp, bs * ss
    assert bp == SUBLANES and sp % S5_CHUNK == 0 and bs % SUBLANES == 0

    perm = jnp.asarray(_Q_PERM)
    w_in0 = w_in[0]
    wq = w_in0[:, :D_ATTN].reshape(D_MODEL, N_HEADS, HEAD_DIM)[:, perm].reshape(D_MODEL, D_ATTN)
    w_in_b = jnp.concatenate([wq, w_in0[:, D_ATTN:]], axis=1).astype(BF16)
    qkg = jnp.concatenate([jnp.tile(q_norm_g[0], N_HEADS), jnp.tile(k_norm_g[0], N_KV_HEADS)])[None]
    pmat = jnp.kron(jnp.eye(256 // HEAD_DIM, dtype=F32),
                    jnp.full((HEAD_DIM, HEAD_DIM), 1.0 / HEAD_DIM, F32)).astype(BF16)
    g_mix = norm_mix_g[0][None]
    sinks = attn_sinks[0]
    g_attn = attn_out_norm_g[0].reshape(N_HEADS, HEAD_DIM)[perm].reshape(1, D_ATTN)
    w_out0 = w_out[0]
    w_out_a = w_out0[:D_ATTN].reshape(N_HEADS, HEAD_DIM, D_MODEL)[perm].reshape(D_ATTN, D_MODEL).astype(BF16)
    w_out_s = w_out0[D_ATTN:].astype(BF16)
    g_ssm = ssm_out_norm_g[0][None]
    g_ffn = norm_ffn_g[0][None]
    w_rt = w_router[0].T.astype(BF16)
    b_rt = jnp.broadcast_to(b_router[0][:, None], (N_EXPERTS, ROUTE_TILE))

    a_re, a_im, bb_re, bb_im = _s5_prep(
        ssm_lambda_re[0], ssm_lambda_im[0], ssm_log_dt[0][:, None],
        jnp.swapaxes(ssm_b_re[0], 1, 2), jnp.swapaxes(ssm_b_im[0], 1, 2))
    chan_g = jnp.arange(D_SSM)[:, None] // SSM_GROUP
    state_g = jnp.arange(D_STATE)[None, :] // SSM_STATE
    bd_b = lambda bb: jnp.where(chan_g == state_g, jnp.tile(bb.reshape(D_SSM, SSM_STATE), (1, N_SSM_GROUPS)), 0.0)
    wb = jnp.concatenate([bd_b(bb_re), bd_b(bb_im)], axis=1).astype(BF16)
    bd_c = lambda c: jnp.where(state_g.T == chan_g.T,
                               jnp.tile(jnp.swapaxes(c, 1, 2).reshape(D_STATE, SSM_GROUP), (1, N_SSM_GROUPS)), 0.0)
    wc = jnp.concatenate([bd_c(ssm_c_re[0]), -bd_c(ssm_c_im[0])], axis=0).astype(BF16)
    a_re, a_im = a_re.reshape(1, D_STATE), a_im.reshape(1, D_STATE)
    d_skip = ssm_d[0].reshape(1, D_SSM)
    w_glu_b = w_glu[0].astype(BF16)
    b_glu0 = b_glu[0][None]

    b_g = b_gate_up[0][:, None, 0::2]
    b_l = b_gate_up[0][:, None, 1::2]
    b_d = b_down[0][:, None, :]
    idx = jnp.arange(256)
    deint = (idx[None, :] == jnp.where(idx % 2 == 0, idx // 2, 128 + idx // 2)[:, None]).astype(BF16)

    xp2 = x_prompt.reshape(tp, D_MODEL)
    xs2 = x_sample.reshape(ts, D_MODEL)
    qp, kp, vp, up = _in_proj(xp2, g_mix, w_in_b, qkg, pmat)
    qs, ks, vs, us = _in_proj(xs2, g_mix, w_in_b, qkg, pmat)

    kp3, vp3 = kp.reshape(bp, sp, D_KV), vp.reshape(bp, sp, D_KV)
    attn_p = _band_attention(sinks, qp.reshape(bp, sp, D_ATTN), kp3, vp3, g_attn).reshape(tp, D_ATTN)
    ck = cache_k[0].reshape(bs, -1, D_KV)
    cv = cache_v[0].reshape(bs, -1, D_KV)
    attn_s, new_ck, new_cv = _cache_attention(sinks, qs, ks, vs, ck, cv, g_attn, ss)

    zeros_p = jnp.zeros((bp, D_STATE), F32)
    s5_args = (a_re, a_im, wb, wc, d_skip, w_glu_b, b_glu0, g_ssm)
    ssm_p, hr_p, hi_p = _s5(up.reshape(bp, sp, D_SSM), zeros_p, zeros_p, *s5_args,
                            tt=S5_CHUNK, time_chunked=True)
    ssm_s, hr_s, hi_s = _s5(us.reshape(bs // SUBLANES, SUBLANES * ss, D_SSM),
                            state_ssm_re[0].reshape(bs, D_STATE), state_ssm_im[0].reshape(bs, D_STATE),
                            *s5_args, tt=ss, time_chunked=False)

    triu = (jnp.arange(ROUTE_TILE)[:, None] < jnp.arange(ROUTE_TILE)[None, :]).astype(BF16)
    low = (jnp.arange(N_EXPERTS)[:, None] > jnp.arange(N_EXPERTS)[None, :]).astype(BF16)
    proj_args = (w_out_a, w_out_s, g_ffn, w_rt, b_rt, triu, low)
    x1p, *tab_p, counts_p = _out_proj(attn_p, ssm_p.reshape(tp, D_SSM), xp2, *proj_args,
                                      jnp.zeros((N_EXPERTS, 128), F32), 0)
    x1s, *tab_s, _ = _out_proj(attn_s, ssm_s.reshape(ts, D_SSM), xs2, *proj_args, counts_p, tp // ROUTE_TILE)

    gates, off, before, cnt, loc = (jnp.concatenate([p, s_], axis=0) for p, s_ in zip(tab_p, tab_s))
    runs, blocks, pend, pad, n_rows = _route(before, cnt, loc, tp + ts)
    x_rows = _dispatch(runs, pend, pad, blocks[1], off, x1p, x1s, g_ffn, n_rows)
    out_rows = _moe(*blocks, x_rows, w_gate_up[0], b_g, b_l, w_down[0], b_d, deint)
    yp = _combine(runs, off, gates, out_rows, x1p, 0).reshape(bp, sp, D_MODEL)
    ys = _combine(runs, off, gates, out_rows, x1s, tp // ROUTE_TILE).reshape(bs, ss, D_MODEL)

    kv5 = lambda a, b_: a.reshape(b_, -1, N_KV_HEADS, HEAD_DIM)
    new_kp = kv5(kp3[:, -WINDOW:], bp)[None]
    new_vp = kv5(vp3[:, -WINDOW:], bp)[None]
    new_ks = kv5(new_ck, bs)[None]
    new_vs = kv5(new_cv, bs)[None]
    st = lambda h, b_: h.reshape(1, b_, N_SSM_GROUPS, SSM_STATE)
    return (yp, ys, new_kp, new_vp, st(hr_p, bp), st(hi_p, bp),
            new_ks, new_vs, st(hr_s, bs), st(hi_s, bs))
```

```python
import functools

import jax
import jax.numpy as jnp
from jax import lax
from jax.experimental import pallas as pl
from jax.experimental.pallas import tpu as pltpu

F32 = jnp.float32
BF16 = jnp.bfloat16

D_MODEL = 1024
D_ATTN = 512
D_SSM = 512
HEAD_DIM = 64
N_HEADS = 8
N_KV_HEADS = 2
D_KV = N_KV_HEADS * HEAD_DIM
WINDOW = 128
SSM_GROUP = 16
N_SSM_GROUPS = 32
SSM_STATE = 64
D_STATE = N_SSM_GROUPS * SSM_STATE
N_EXPERTS = 32
TOP_K = 4
D_FF = 1024
SWIGLU_LIMIT = 7.0
SWIGLU_ALPHA = 1.702
RMS_EPS = 1e-6
NEG_INF = -1e30
D_IN_PROJ = D_ATTN + 2 * D_KV + D_SSM
D_QK = D_ATTN + D_KV

SUBLANES = 8
VMEM_LIMIT = 56 * 1024 * 1024

ROW_TILE = 2048
MOE_TILE = 512
ATTN_BLOCKS = 8
ROUTE_TILE = 512
_SLOT_ROWS = ROUTE_TILE * TOP_K * SUBLANES
_RUN_SIZES = tuple(1 << b for b in range(ROUTE_TILE.bit_length() - 1, -1, -1))
S5_CHUNK = 128
S5_COLS = 1024

_Q_PERM = (0, 4, 1, 5, 2, 6, 3, 7)


def _cparams(*sem):
    return pltpu.CompilerParams(dimension_semantics=sem, vmem_limit_bytes=VMEM_LIMIT)


def _dot(a, b):
    return jnp.dot(a, b, preferred_element_type=F32)


def _rms(x):
    return x * lax.rsqrt(jnp.mean(x * x, axis=-1, keepdims=True) + RMS_EPS)


def _in_proj_kernel(x_ref, g_ref, w_ref, qkg_ref, p_ref, q_ref, k_ref, v_ref, u_ref):
    xn = _rms(x_ref[...]) * g_ref[...]
    h = _dot(xn.astype(BF16), w_ref[...])
    qk = h[:, :D_QK]
    sq = (qk * qk).astype(BF16)
    p = p_ref[...]
    ms = jnp.concatenate(
        [_dot(sq[:, 0:256], p), _dot(sq[:, 256:512], p), _dot(sq[:, 512:640], p[:128, :128])],
        axis=-1)
    qkn = qk * lax.rsqrt(ms + RMS_EPS) * qkg_ref[...]
    q_ref[...] = qkn[:, :D_ATTN].astype(BF16)
    k_ref[...] = qkn[:, D_ATTN:]
    v_ref[...] = h[:, D_QK:D_QK + D_KV]
    u_ref[...] = h[:, D_QK + D_KV:]


def _in_proj(x2d, g, w, qkg, pmat):
    t = x2d.shape[0]
    tm = min(ROW_TILE, t)
    row = lambda i: (i, 0)
    fix = lambda i: (0, 0)
    return pl.pallas_call(
        _in_proj_kernel,
        grid=(t // tm,),
        in_specs=[pl.BlockSpec((tm, D_MODEL), row), pl.BlockSpec((1, D_MODEL), fix),
                  pl.BlockSpec((D_MODEL, D_IN_PROJ), fix), pl.BlockSpec((1, D_QK), fix),
                  pl.BlockSpec((256, 256), fix)],
        out_specs=[pl.BlockSpec((tm, D_ATTN), row), pl.BlockSpec((tm, D_KV), row),
                   pl.BlockSpec((tm, D_KV), row), pl.BlockSpec((tm, D_SSM), row)],
        out_shape=[jax.ShapeDtypeStruct((t, D_ATTN), BF16), jax.ShapeDtypeStruct((t, D_KV), F32),
                   jax.ShapeDtypeStruct((t, D_KV), F32), jax.ShapeDtypeStruct((t, D_SSM), F32)],
        compiler_params=_cparams("parallel"),
        name="in_proj",
    )(x2d, g, w, qkg, pmat)


def _softmax_pv(s_blocks, v_blocks, sink):
    m = sink
    for s in s_blocks:
        m = jnp.maximum(m, jnp.max(s, axis=-1, keepdims=True))
    den = jnp.exp(sink - m)
    acc = None
    for s, v in zip(s_blocks, v_blocks):
        p = jnp.exp(s - m)
        den = den + jnp.sum(p, axis=-1, keepdims=True)
        pv = _dot(p.astype(BF16), v)
        acc = pv if acc is None else acc + pv
    return acc / den


def _band_attn_kernel(sink_ref, q_ref, kp_ref, kc_ref, vp_ref, vc_ref, g_ref, o_ref):
    i = pl.program_id(1)
    row = lax.broadcasted_iota(jnp.int32, (WINDOW, 2 * WINDOW), 0)
    col = lax.broadcasted_iota(jnp.int32, (WINDOW, 2 * WINDOW), 1)
    band = (col > row) & (col <= row + WINDOW)
    lane = lax.broadcasted_iota(jnp.int32, (WINDOW, 128), 1)
    low = lane < HEAD_DIM
    zero = jnp.zeros((), BF16)
    for j in range(ATTN_BLOCKS):
        cur = slice(WINDOW * j, WINDOW * (j + 1))
        q = q_ref[cur, :]
        k_prev = kp_ref[...] if j == 0 else kc_ref[WINDOW * (j - 1):WINDOW * j, :]
        v_prev = vp_ref[...] if j == 0 else vc_ref[WINDOW * (j - 1):WINDOW * j, :]
        kb = jnp.concatenate([k_prev, kc_ref[cur, :]], axis=0).astype(BF16)
        vb = jnp.concatenate([v_prev, vc_ref[cur, :]], axis=0).astype(BF16)
        mask = band & ((col >= WINDOW) | (i > 0)) if j == 0 else band
        outs = []
        for pair in range(N_HEADS // 2):
            qp = q[:, 128 * pair:128 * (pair + 1)]
            halves = []
            for par in range(2):
                qm = jnp.where(low if par == 0 else ~low, qp, zero)
                s = lax.dot_general(qm, kb, (((1,), (1,)), ((), ())), preferred_element_type=F32)
                s = jnp.where(mask, s * (HEAD_DIM ** -0.5), NEG_INF)
                halves.append(_softmax_pv([s], [vb], sink_ref[_Q_PERM[2 * pair + par]]))
            outs.append(jnp.where(low, halves[0], halves[1]))
        o = jnp.concatenate(outs, axis=-1)
        o_ref[cur, :] = (_rms(o) * g_ref[...]).astype(BF16)


def _band_attention(sinks, q, k, v, g):
    b, s, _ = q.shape
    qb = WINDOW * ATTN_BLOCKS
    cur = lambda bi, i: (bi, i, 0)
    prev = lambda bi, i: (bi, jnp.maximum(i * ATTN_BLOCKS - 1, 0), 0)
    return pl.pallas_call(
        _band_attn_kernel,
        grid=(b, s // qb),
        in_specs=[pl.BlockSpec(memory_space=pltpu.SMEM),
                  pl.BlockSpec((None, qb, D_ATTN), cur),
                  pl.BlockSpec((None, WINDOW, D_KV), prev), pl.BlockSpec((None, qb, D_KV), cur),
                  pl.BlockSpec((None, WINDOW, D_KV), prev), pl.BlockSpec((None, qb, D_KV), cur),
                  pl.BlockSpec((1, D_ATTN), lambda bi, i: (0, 0))],
        out_specs=pl.BlockSpec((None, qb, D_ATTN), cur),
        out_shape=jax.ShapeDtypeStruct((b, s, D_ATTN), BF16),
        compiler_params=_cparams("parallel", "parallel"),
        name="band_attn",
    )(sinks, q, k, k, v, v, g)


_PAIR_ROWS = 8
_CACHE_BB = 16


def _cache_attn_kernel(sink_ref, q_ref, kn_ref, vn_ref, ck_ref, cv_ref, g_ref, o_ref, nk_ref, nv_ref, *, n_new):
    n_buf = ck_ref.shape[1]
    rows_blk = _CACHE_BB * n_new
    for b in range(_CACHE_BB):
        for new_ref, cache_ref, fresh_ref in ((nk_ref, ck_ref, kn_ref), (nv_ref, cv_ref, vn_ref)):
            new_ref[b, 0:n_buf - n_new, :] = cache_ref[b, n_new:n_buf, :]
            new_ref[b, n_buf - n_new:n_buf, :] = fresh_ref[n_new * b:n_new * (b + 1), :]
    knew = kn_ref[...].astype(BF16)
    vnew = vn_ref[...].astype(BF16)
    lane = lax.broadcasted_iota(jnp.int32, (_PAIR_ROWS, 128), 1)
    low = lane < HEAD_DIM
    zero = jnp.zeros((), BF16)
    n_stack = N_HEADS * _PAIR_ROWS
    r = lax.broadcasted_iota(jnp.int32, (n_stack, 1), 0) % _PAIR_ROWS
    r_seq, r_tok = r // n_new, r % n_new
    colc = lax.broadcasted_iota(jnp.int32, (n_stack, 2 * n_buf), 1)
    c_seq, c_pos = colc // n_buf, colc % n_buf
    mask_c = (c_seq == r_seq) & (c_pos + WINDOW > r_tok + n_buf)
    coln = lax.broadcasted_iota(jnp.int32, (n_stack, rows_blk), 1)
    sink_col = jnp.concatenate(
        [jnp.full((_PAIR_ROWS, 1), sink_ref[_Q_PERM[h]], F32) for h in range(N_HEADS)], axis=0)
    n_pairs = _CACHE_BB // 2
    nt = (((1,), (1,)), ((), ()))
    scores_c, scores_n, values = [], [], []
    for sp in range(n_pairs):
        q = q_ref[_PAIR_ROWS * sp:_PAIR_ROWS * (sp + 1), :]
        pieces = []
        for pair in range(N_HEADS // 2):
            qp = q[:, 128 * pair:128 * (pair + 1)]
            pieces.append(jnp.where(low, qp, zero))
            pieces.append(jnp.where(low, zero, qp))
        qs = jnp.concatenate(pieces, axis=0)
        kc = jnp.concatenate([ck_ref[2 * sp], ck_ref[2 * sp + 1]], axis=0).astype(BF16)
        values.append(jnp.concatenate([cv_ref[2 * sp], cv_ref[2 * sp + 1]], axis=0).astype(BF16))
        s_c = lax.dot_general(qs, kc, nt, preferred_element_type=F32) * (HEAD_DIM ** -0.5)
        s_n = lax.dot_general(qs, knew, nt, preferred_element_type=F32) * (HEAD_DIM ** -0.5)
        n_seq, n_tok = coln // n_new - 2 * sp, coln % n_new
        mask_n = (n_seq == r_seq) & (n_tok <= r_tok)
        scores_c.append(jnp.where(mask_c, s_c, NEG_INF))
        scores_n.append(jnp.where(mask_n, s_n, NEG_INF))
    s_c = jnp.concatenate(scores_c, axis=0)
    s_n = jnp.concatenate(scores_n, axis=0)
    sink = jnp.concatenate([sink_col] * n_pairs, axis=0)
    m = jnp.maximum(sink, jnp.maximum(jnp.max(s_c, axis=-1, keepdims=True), jnp.max(s_n, axis=-1, keepdims=True)))
    p_c = jnp.exp(s_c - m)
    p_n = jnp.exp(s_n - m)
    den = jnp.exp(sink - m) + jnp.sum(p_c, axis=-1, keepdims=True) + jnp.sum(p_n, axis=-1, keepdims=True)
    p_c, p_n = p_c.astype(BF16), p_n.astype(BF16)
    for sp in range(n_pairs):
        rows = slice(n_stack * sp, n_stack * (sp + 1))
        o = (_dot(p_c[rows], values[sp]) + _dot(p_n[rows], vnew)) / den[rows]
        outs = [jnp.where(low, o[16 * pair:16 * pair + 8], o[16 * pair + 8:16 * pair + 16])
                for pair in range(N_HEADS // 2)]
        oo = jnp.concatenate(outs, axis=-1)
        o_ref[_PAIR_ROWS * sp:_PAIR_ROWS * (sp + 1), :] = (_rms(oo) * g_ref[...]).astype(BF16)


def _cache_attention(sinks, q, k, v, cache_k, cache_v, g, n_new):
    t = q.shape[0]
    nb, n_buf, _ = cache_k.shape
    assert n_new * 2 == _PAIR_ROWS and nb % _CACHE_BB == 0 and n_buf == WINDOW
    rows = _CACHE_BB * n_new
    row = lambda i: (i, 0)
    cache_spec = pl.BlockSpec((_CACHE_BB, n_buf, D_KV), lambda i: (i, 0, 0))
    return pl.pallas_call(
        functools.partial(_cache_attn_kernel, n_new=n_new),
        grid=(nb // _CACHE_BB,),
        in_specs=[pl.BlockSpec(memory_space=pltpu.SMEM),
                  pl.BlockSpec((rows, D_ATTN), row), pl.BlockSpec((rows, D_KV), row),
                  pl.BlockSpec((rows, D_KV), row), cache_spec, cache_spec,
                  pl.BlockSpec((1, D_ATTN), lambda i: (0, 0))],
        out_specs=[pl.BlockSpec((rows, D_ATTN), row), cache_spec, cache_spec],
        out_shape=[jax.ShapeDtypeStruct((t, D_ATTN), BF16), jax.ShapeDtypeStruct(cache_k.shape, F32),
                   jax.ShapeDtypeStruct(cache_v.shape, F32)],
        compiler_params=_cparams("parallel"),
        name="cache_attn",
    )(sinks, q, k, v, cache_k, cache_v, g)


def _s5_prep_kernel(lre_ref, lim_ref, ldt_ref, bre_ref, bim_ref, are_ref, aim_ref, bbre_ref, bbim_ref):
    dt = jnp.exp(ldt_ref[...])
    l_re = jnp.minimum(lre_ref[...], -1e-4)
    l_im = lim_ref[...]
    mag = jnp.exp(l_re * dt)
    a_re = mag * jnp.cos(l_im * dt)
    a_im = mag * jnp.sin(l_im * dt)
    den = l_re * l_re + l_im * l_im
    n_re = a_re - 1.0
    z_re = (n_re * l_re + a_im * l_im) / den
    z_im = (a_im * l_re - n_re * l_im) / den
    are_ref[...] = a_re
    aim_ref[...] = a_im
    br, bi = bre_ref[...], bim_ref[...]
    zr, zi = z_re[:, None, :], z_im[:, None, :]
    bbre_ref[...] = zr * br - zi * bi
    bbim_ref[...] = zr * bi + zi * br


def _s5_prep(lam_re, lam_im, log_dt, b_re_t, b_im_t):
    g, p = lam_re.shape
    sd = jax.ShapeDtypeStruct
    return pl.pallas_call(
        _s5_prep_kernel,
        out_shape=[sd((g, p), F32), sd((g, p), F32), sd(b_re_t.shape, F32), sd(b_re_t.shape, F32)],
        name="s5_prep",
    )(lam_re, lam_im, log_dt, b_re_t, b_im_t)


def _s5_kernel(u_ref, h0r_ref, h0i_ref, ar_ref, ai_ref, wb_ref, wc_ref, d_ref, wglu_ref, bglu_ref,
               g_ref, o_ref, hr_ref, hi_ref, bu_ref, hs_ref, usc_ref, ysc_ref, *, tt):
    j = pl.program_id(1)
    rows = SUBLANES * tt
    n_tiles = D_STATE // 128
    time_major = tt % SUBLANES == 0

    @pl.when(j == 0)
    def _():
        hs_ref[:, :D_STATE] = h0r_ref[...]
        hs_ref[:, D_STATE:] = h0i_ref[...]

    u = u_ref[...].reshape(rows, D_SSM)
    if time_major:
        for c in range(D_SSM // 128):
            for b in range(SUBLANES):
                usc_ref[c, pl.ds(b, tt, stride=SUBLANES), :] = u[b * tt:(b + 1) * tt, 128 * c:128 * (c + 1)]
        ub = jnp.concatenate([usc_ref[c] for c in range(D_SSM // 128)], axis=-1).astype(BF16)
    else:
        ub = u.astype(BF16)

    def step_rows(t):
        if time_major:
            return pl.ds(pl.multiple_of(t * SUBLANES, SUBLANES), SUBLANES)
        return pl.ds(t, SUBLANES, stride=tt)

    for n in range(2 * D_STATE // 256):
        band = (n % (D_STATE // 256)) // 2
        res = _dot(ub[:, 128 * band:128 * (band + 1)],
                   wb_ref[128 * band:128 * (band + 1), 256 * n:256 * (n + 1)])
        bu_ref[2 * n] = res[:, :128]
        bu_ref[2 * n + 1] = res[:, 128:]

    tiles_per_pass = S5_COLS // 128
    for c0 in range(0, n_tiles, tiles_per_pass):
        tiles = range(c0, c0 + tiles_per_pass)
        a_r = [jnp.broadcast_to(ar_ref[:, 128 * c:128 * (c + 1)], (SUBLANES, 128)) for c in tiles]
        a_i = [jnp.broadcast_to(ai_ref[:, 128 * c:128 * (c + 1)], (SUBLANES, 128)) for c in tiles]

        def step(t, carry, tiles=tiles, a_r=a_r, a_i=a_i):
            at_t = step_rows(t)
            out = []
            for k, c in enumerate(tiles):
                h_r, h_i = carry[2 * k], carry[2 * k + 1]
                n_r = a_r[k] * h_r - a_i[k] * h_i + bu_ref[c, at_t, :]
                n_i = a_r[k] * h_i + a_i[k] * h_r + bu_ref[n_tiles + c, at_t, :]
                bu_ref[c, at_t, :] = n_r
                bu_ref[n_tiles + c, at_t, :] = n_i
                out += [n_r, n_i]
            return tuple(out)

        init = []
        for c in tiles:
            init += [hs_ref[:, 128 * c:128 * (c + 1)], hs_ref[:, D_STATE + 128 * c:D_STATE + 128 * (c + 1)]]
        fin = lax.fori_loop(0, tt, step, tuple(init), unroll=min(tt, 8))
        for k, c in enumerate(tiles):
            hs_ref[:, 128 * c:128 * (c + 1)] = fin[2 * k]
            hs_ref[:, D_STATE + 128 * c:D_STATE + 128 * (c + 1)] = fin[2 * k + 1]

    def h_cols(first_tile):
        return jnp.concatenate([bu_ref[first_tile + k] for k in range(4)], axis=-1).astype(BF16)

    ys = []
    for m in range(D_SSM // 128):
        y = _dot(h_cols(4 * m), wc_ref[512 * m:512 * (m + 1), 128 * m:128 * (m + 1)])
        y = y + _dot(h_cols(n_tiles + 4 * m),
                     wc_ref[D_STATE + 512 * m:D_STATE + 512 * (m + 1), 128 * m:128 * (m + 1)])
        if time_major:
            ysc_ref[m] = y
            y = jnp.concatenate([ysc_ref[m, pl.ds(b, tt, stride=SUBLANES), :] for b in range(SUBLANES)], axis=0)
        ys.append(y)
    y = jnp.concatenate(ys, axis=-1) + d_ref[...] * u
    z = _dot(jax.nn.gelu(y).astype(BF16), wglu_ref[...]) + bglu_ref[...]
    s = z[:, :D_SSM] * jax.nn.sigmoid(z[:, D_SSM:])
    o_ref[...] = (_rms(s) * g_ref[...]).astype(BF16).reshape(o_ref.shape)

    @pl.when(j == pl.num_programs(1) - 1)
    def _():
        hr_ref[...] = hs_ref[:, :D_STATE]
        hi_ref[...] = hs_ref[:, D_STATE:]


def _s5(u, h0r, h0i, a_re, a_im, wb, wc, d, wglu, bglu, g, *, tt, time_chunked):
    nbg = h0r.shape[0] // SUBLANES
    if time_chunked:
        nchunks = u.shape[1] // tt
        u_spec = pl.BlockSpec((SUBLANES, tt, D_SSM), lambda gi, j: (gi, j, 0))
    else:
        nchunks = 1
        u_spec = pl.BlockSpec((None, SUBLANES * tt, D_SSM), lambda gi, j: (gi, 0, 0))
    fix = lambda gi, j: (0, 0)
    st_spec = pl.BlockSpec((SUBLANES, D_STATE), lambda gi, j: (gi, 0))
    sd = jax.ShapeDtypeStruct
    return pl.pallas_call(
        functools.partial(_s5_kernel, tt=tt),
        grid=(nbg, nchunks),
        in_specs=[u_spec, st_spec, st_spec,
                  pl.BlockSpec((1, D_STATE), fix), pl.BlockSpec((1, D_STATE), fix),
                  pl.BlockSpec((D_SSM, 2 * D_STATE), fix), pl.BlockSpec((2 * D_STATE, D_SSM), fix),
                  pl.BlockSpec((1, D_SSM), fix), pl.BlockSpec((D_SSM, 2 * D_SSM), fix),
                  pl.BlockSpec((1, 2 * D_SSM), fix), pl.BlockSpec((1, D_SSM), fix)],
        out_specs=[u_spec, st_spec, st_spec],
        out_shape=[sd(u.shape, BF16), sd(h0r.shape, F32), sd(h0r.shape, F32)],
        scratch_shapes=[pltpu.VMEM((2 * D_STATE // 128, SUBLANES * tt, 128), F32),
                        pltpu.VMEM((SUBLANES, 2 * D_STATE), F32),
                        pltpu.VMEM((D_SSM // 128, SUBLANES * tt, 128), F32),
                        pltpu.VMEM((D_SSM // 128, SUBLANES * tt, 128), F32)],
        compiler_params=_cparams("parallel", "arbitrary"),
        name="s5",
    )(u, h0r, h0i, a_re, a_im, wb, wc, d, wglu, bglu, g)


def _to_row_tiles(ref, val, first=0):
    rows = val.shape[0]
    for c in range(D_MODEL // 128):
        ref[pl.ds(first * SUBLANES + c, rows, stride=SUBLANES), :] = val[:, 128 * c:128 * (c + 1)]


def _from_row_tiles(ref, rows, lead=(), first=0):
    return jnp.concatenate(
        [ref[(*lead, pl.ds(first * SUBLANES + c, rows, stride=SUBLANES), slice(None))]
         for c in range(D_MODEL // 128)], axis=-1)


def _route_tile(logits_t, tile, triu_ref, low_ref, carry, gate_ref, off_ref, before_ref, cnt_ref, loc_ref):
    l = logits_t
    expert = lax.broadcasted_iota(jnp.int32, l.shape, 0).astype(F32)
    vals, sels = [], []
    for _ in range(TOP_K):
        m = jnp.max(l, axis=0, keepdims=True)
        idx = jnp.min(jnp.where(l == m, expert, float(N_EXPERTS)), axis=0, keepdims=True)
        sel = expert == idx
        l = jnp.where(sel, -jnp.inf, l)
        vals.append(m)
        sels.append(sel)
    exps = [jnp.exp(v - vals[0]) for v in vals]
    den = exps[0] + exps[1] + exps[2] + exps[3]
    onehot = jnp.where(sels[0] | sels[1] | sels[2] | sels[3], 1.0, 0.0)
    within = _dot(onehot.astype(BF16), triu_ref[...])
    cnt = jnp.broadcast_to(jnp.sum(onehot, axis=1, keepdims=True), onehot.shape)
    cnt_hi = jnp.floor(cnt * (1.0 / 16.0))
    cnt_lo = cnt - 16.0 * cnt_hi
    loc = 16.0 * _dot(low_ref[...], cnt_hi.astype(BF16)) + _dot(low_ref[...], cnt_lo.astype(BF16))
    slot_base = ((tile % 2) * (ROUTE_TILE * TOP_K)).astype(F32)
    rows = [jnp.sum(jnp.where(s, within + loc, 0.0), axis=0, keepdims=True) for s in sels]
    gate_ref[...] = jnp.concatenate([e / den for e in exps], axis=1)
    off_ref[...] = ((jnp.concatenate(rows, axis=1) + slot_base) * SUBLANES).astype(jnp.int32)
    before_ref[...] = carry[...]
    cnt_ref[...] = cnt[:, :128]
    loc_ref[...] = loc[:, :128]
    carry[...] = carry[...] + cnt[:, :128]


def _out_proj_kernel(a_ref, s_ref, x_ref, wa_ref, ws_ref, g_ref, wrt_ref, brt_ref, triu_ref, low_ref, cin_ref,
                     x1_ref, gate_ref, off_ref, before_ref, cnt_ref, loc_ref, cout_ref, carry, *, tile_base):
    i = pl.program_id(0)

    @pl.when(i == 0)
    def _():
        carry[...] = cin_ref[...]

    x1 = x_ref[...] + _dot(a_ref[...], wa_ref[...]) + _dot(s_ref[...], ws_ref[...])
    x1_ref[...] = x1
    xn = _rms(x1) * g_ref[...]
    logits_t = lax.dot_general(wrt_ref[...], xn.astype(BF16), (((1,), (1,)), ((), ())),
                               preferred_element_type=F32) + brt_ref[...]
    _route_tile(logits_t, tile_base + i, triu_ref, low_ref, carry, gate_ref, off_ref, before_ref, cnt_ref, loc_ref)
    cout_ref[...] = carry[...]


def _out_proj(attn_n, ssm_n, x2d, wa, ws, g, wrt, brt, triu, low, counts_in, tile_base):
    t = x2d.shape[0]
    tm = ROUTE_TILE
    n_tiles = t // tm
    row = lambda i: (i, 0)
    fix = lambda i: (0, 0)
    sd = jax.ShapeDtypeStruct
    k_spec = pl.BlockSpec((None, 1, TOP_K * tm), lambda i: (i, 0, 0))
    t_spec = pl.BlockSpec((None, N_EXPERTS, 128), lambda i: (i, 0, 0))
    return pl.pallas_call(
        functools.partial(_out_proj_kernel, tile_base=tile_base),
        grid=(n_tiles,),
        in_specs=[pl.BlockSpec((tm, D_ATTN), row), pl.BlockSpec((tm, D_SSM), row),
                  pl.BlockSpec((tm, D_MODEL), row),
                  pl.BlockSpec((D_ATTN, D_MODEL), fix), pl.BlockSpec((D_SSM, D_MODEL), fix),
                  pl.BlockSpec((1, D_MODEL), fix), pl.BlockSpec((N_EXPERTS, D_MODEL), fix),
                  pl.BlockSpec((N_EXPERTS, tm), fix), pl.BlockSpec((tm, tm), fix),
                  pl.BlockSpec((N_EXPERTS, N_EXPERTS), fix), pl.BlockSpec((N_EXPERTS, 128), fix)],
        out_specs=[pl.BlockSpec((tm, D_MODEL), row), k_spec, k_spec, t_spec, t_spec, t_spec,
                   pl.BlockSpec((N_EXPERTS, 128), fix)],
        out_shape=[sd((t, D_MODEL), F32), sd((n_tiles, 1, TOP_K * tm), F32), sd((n_tiles, 1, TOP_K * tm), jnp.int32),
                   sd((n_tiles, N_EXPERTS, 128), F32), sd((n_tiles, N_EXPERTS, 128), F32),
                   sd((n_tiles, N_EXPERTS, 128), F32), sd((N_EXPERTS, 128), F32)],
        scratch_shapes=[pltpu.VMEM((N_EXPERTS, 128), F32)],
        compiler_params=_cparams("arbitrary"),
        name="out_proj",
    )(attn_n, ssm_n, x2d, wa, ws, g, wrt, brt, triu, low, counts_in)


def _tile_span(ref, first_row, n_rows, lead=()):
    start = pl.multiple_of(first_row * SUBLANES, SUBLANES)
    return ref.at[(*lead, pl.ds(start, n_rows * SUBLANES), slice(None))]


def _run_copies(tile, src_tbl, cnt_tbl, loc_tbl, make_copy):
    def body(e, c):
        j = tile * N_EXPERTS + e
        cnt, src, loc = cnt_tbl[j], src_tbl[j], loc_tbl[j]
        off = 0
        for size in _RUN_SIZES:
            @pl.when((cnt & size) != 0)
            def _(off=off, size=size):
                make_copy(src + off, loc + off, size).start()
            off = off + (cnt & size)
        return c

    lax.fori_loop(0, N_EXPERTS, body, 0)


def _dispatch_kernel(src_tbl, cnt_tbl, loc_tbl, pend_ref, pad_ref, nu_ref, off_ref, xa_ref, xb_ref, g_ref,
                     rows_ref, buf, zbuf, xt, sem, zsem, *, tiles_a, nblk):
    i = pl.program_id(0)
    n = pl.num_programs(0)
    slot = i % 2

    def zero_block(start_row):
        return pltpu.make_async_copy(zbuf, _tile_span(rows_ref, start_row, MOE_TILE), zsem)

    def pad_copies(e, issue):
        n_pad = pad_ref[e]
        first = pend_ref[e] - n_pad
        off = 0
        for size in _RUN_SIZES:
            @pl.when((n_pad & size) != 0)
            def _(off=off, size=size):
                issue(pltpu.make_async_copy(_tile_span(zbuf, 0, size), _tile_span(rows_ref, first + off, size), zsem))
            off = off + (n_pad & size)

    @pl.when(i == 0)
    def _():
        zbuf[...] = jnp.zeros_like(zbuf)

        def pads_start(e, c):
            pad_copies(e, lambda cp: cp.start())
            return c

        def pads_wait(e, c):
            pad_copies(e, lambda cp: cp.wait())
            return c

        def tail_start(j, c):
            zero_block(j * MOE_TILE).start()
            return c

        def tail_wait(j, c):
            zero_block(j * MOE_TILE).wait()
            return c

        lax.fori_loop(0, N_EXPERTS, pads_start, 0)
        lax.fori_loop(nu_ref[0], nblk, tail_start, 0)
        lax.fori_loop(0, N_EXPERTS, pads_wait, 0)
        lax.fori_loop(nu_ref[0], nblk, tail_wait, 0)

    def slot_rows(s):
        return buf.at[pl.ds(pl.multiple_of(s * _SLOT_ROWS, _SLOT_ROWS), _SLOT_ROWS), :]

    def slot_done(s):
        return pltpu.make_async_copy(slot_rows(s), slot_rows(s), sem.at[s])

    @pl.when(i >= 2)
    def _():
        slot_done(slot).wait()

    def fill(x_ref):
        _to_row_tiles(xt, _rms(x_ref[...]) * g_ref[...])

        def body(t, c):
            v = xt[pl.ds(pl.multiple_of(t * SUBLANES, SUBLANES), SUBLANES), :]
            for k in range(TOP_K):
                off = pl.multiple_of(off_ref[0, k * ROUTE_TILE + t], SUBLANES)
                buf[pl.ds(off, SUBLANES), :] = v
            return c

        lax.fori_loop(0, ROUTE_TILE, body, 0, unroll=16)

    @pl.when(i < tiles_a)
    def _():
        fill(xa_ref)

    @pl.when(i >= tiles_a)
    def _():
        fill(xb_ref)

    _run_copies(i, src_tbl, cnt_tbl, loc_tbl,
                lambda g, l, size: pltpu.make_async_copy(_tile_span(slot_rows(slot), l, size),
                                                         _tile_span(rows_ref, g, size), sem.at[slot]))

    @pl.when(i == n - 1)
    def _():
        slot_done(1 - slot).wait()
        slot_done(slot).wait()


def _dispatch(runs, pend, pad, n_used, off, xa, xb, g, n_rows):
    assert MOE_TILE <= ROUTE_TILE
    tile_rows = ROUTE_TILE * SUBLANES
    tiles_a, tiles_b = xa.shape[0] // ROUTE_TILE, xb.shape[0] // ROUTE_TILE
    assert tiles_a + tiles_b >= 2
    grid_spec = pltpu.PrefetchScalarGridSpec(
        num_scalar_prefetch=6, grid=(tiles_a + tiles_b,),
        in_specs=[pl.BlockSpec((None, 1, TOP_K * ROUTE_TILE), lambda i, *_: (i, 0, 0), memory_space=pltpu.SMEM),
                  pl.BlockSpec((ROUTE_TILE, D_MODEL), lambda i, *_: (jnp.minimum(i, tiles_a - 1), 0)),
                  pl.BlockSpec((ROUTE_TILE, D_MODEL), lambda i, *_: (jnp.maximum(i - tiles_a, 0), 0)),
                  pl.BlockSpec((1, D_MODEL), lambda i, *_: (0, 0))],
        out_specs=pl.BlockSpec(memory_space=pl.ANY),
        scratch_shapes=[pltpu.VMEM((2 * _SLOT_ROWS, 128), F32),
                        pltpu.VMEM((MOE_TILE * SUBLANES, 128), F32), pltpu.VMEM((tile_rows, 128), F32),
                        pltpu.SemaphoreType.DMA((2,)), pltpu.SemaphoreType.DMA(())])
    return pl.pallas_call(
        functools.partial(_dispatch_kernel, tiles_a=tiles_a, nblk=n_rows // MOE_TILE), grid_spec=grid_spec,
        out_shape=jax.ShapeDtypeStruct((n_rows * SUBLANES, 128), F32),
        compiler_params=_cparams("arbitrary"), name="dispatch",
    )(*runs, pend, pad, n_used, off, xa, xb, g)


def _moe_kernel(be_ref, nu_ref, nv_ref, next_ref, slot_ref, x_ref, wgu_hbm, bg_ref, bl_ref, wd_hbm, bd_ref,
                perm_ref, o_ref, wgu_buf, wd_buf, wg_s, wl_s, wd_s, sem_gu, sem_d):
    i = pl.program_id(0)
    used = i < nu_ref[0]
    e = be_ref[i]
    new_expert = (i == 0) | (e != be_ref[jnp.maximum(i - 1, 0)])

    def fetch(expert, s):
        return (pltpu.make_async_copy(wgu_hbm.at[expert], wgu_buf.at[s], sem_gu.at[s]),
                pltpu.make_async_copy(wd_hbm.at[expert], wd_buf.at[s], sem_d.at[s]))

    @pl.when(used & new_expert)
    def _():
        s = slot_ref[e]

        @pl.when(i == 0)
        def _():
            for cp in fetch(e, s):
                cp.start()

        @pl.when(next_ref[e] >= 0)
        def _():
            for cp in fetch(next_ref[e], 1 - s):
                cp.start()

        for cp in fetch(e, s):
            cp.wait()
        for c in range(2 * D_FF // 256):
            r = _dot(wgu_buf[s, :, 256 * c:256 * (c + 1)].astype(BF16), perm_ref[...])
            wg_s[:, 128 * c:128 * (c + 1)] = r[:, :128].astype(BF16)
            wl_s[:, 128 * c:128 * (c + 1)] = r[:, 128:].astype(BF16)
        wd_s[...] = wd_buf[s].astype(BF16)

    def expert_rows(rows):
        x = _from_row_tiles(x_ref, rows).astype(BF16)
        glu = jnp.minimum(_dot(x, wg_s[...]) + bg_ref[...], SWIGLU_LIMIT)
        lin = jnp.clip(_dot(x, wl_s[...]) + bl_ref[...], -SWIGLU_LIMIT, SWIGLU_LIMIT)
        act = glu * jax.nn.sigmoid(SWIGLU_ALPHA * glu) * (lin + 1.0)
        _to_row_tiles(o_ref, _dot(act.astype(BF16), wd_s[...]) + bd_ref[...])

    def zero_rows(first, rows):
        o_ref[pl.ds(first * SUBLANES, rows * SUBLANES), :] = jnp.zeros((rows * SUBLANES, 128), F32)

    half = MOE_TILE // 2

    @pl.when(used & (nv_ref[i] > half))
    def _():
        expert_rows(MOE_TILE)

    @pl.when(used & (nv_ref[i] <= half))
    def _():
        expert_rows(half)
        zero_rows(half, half)


def _moe(block_e, n_used, n_valid, next_e, e_slot, x_rows, wgu, bg, bl, wd, bd, perm):
    nblk = x_rows.shape[0] // (MOE_TILE * SUBLANES)
    row = lambda i, be, nu, *_: (jnp.minimum(i, nu[0] - 1), 0)
    wsel = lambda i, be, *_: (be[i], 0, 0)
    grid_spec = pltpu.PrefetchScalarGridSpec(
        num_scalar_prefetch=5,
        grid=(nblk,),
        in_specs=[pl.BlockSpec((MOE_TILE * SUBLANES, 128), row),
                  pl.BlockSpec(memory_space=pl.ANY),
                  pl.BlockSpec((None, 1, D_FF), wsel), pl.BlockSpec((None, 1, D_FF), wsel),
                  pl.BlockSpec(memory_space=pl.ANY), pl.BlockSpec((None, 1, D_MODEL), wsel),
                  pl.BlockSpec((256, 256), lambda i, *_: (0, 0))],
        out_specs=pl.BlockSpec((MOE_TILE * SUBLANES, 128), row),
        scratch_shapes=[pltpu.VMEM((2, D_MODEL, 2 * D_FF), F32), pltpu.VMEM((2, D_FF, D_MODEL), F32),
                        pltpu.VMEM((D_MODEL, D_FF), BF16), pltpu.VMEM((D_MODEL, D_FF), BF16),
                        pltpu.VMEM((D_FF, D_MODEL), BF16),
                        pltpu.SemaphoreType.DMA((2,)), pltpu.SemaphoreType.DMA((2,))],
    )
    return pl.pallas_call(
        _moe_kernel,
        grid_spec=grid_spec,
        out_shape=jax.ShapeDtypeStruct(x_rows.shape, F32),
        input_output_aliases={5: 0},
        compiler_params=_cparams("arbitrary"),
        name="moe",
    )(block_e, n_used, n_valid, next_e, e_slot, x_rows, wgu, bg, bl, wd, bd, perm)


def _combine_kernel(src_tbl, cnt_tbl, loc_tbl, off_ref, gate_ref, rows_hbm, x1_ref, o_ref, buf, ybuf, sem, *,
                    tile_base):
    i = pl.program_id(0)
    n = pl.num_programs(0)

    def slot_rows(s):
        return buf.at[pl.ds(pl.multiple_of(s * _SLOT_ROWS, _SLOT_ROWS), _SLOT_ROWS), :]

    def fetch(tile, s):
        _run_copies(tile_base + tile, src_tbl, cnt_tbl, loc_tbl,
                    lambda g, l, size: pltpu.make_async_copy(_tile_span(rows_hbm, g, size),
                                                             _tile_span(slot_rows(s), l, size), sem.at[s]))

    @pl.when(i == 0)
    def _():
        fetch(0, 0)

    @pl.when(i + 1 < n)
    def _():
        fetch(i + 1, (i + 1) % 2)

    slot = i % 2
    pltpu.make_async_copy(slot_rows(slot), slot_rows(slot), sem.at[slot]).wait()

    def body(t, c):
        acc = None
        for k in range(TOP_K):
            off = pl.multiple_of(off_ref[0, k * ROUTE_TILE + t], SUBLANES)
            v = gate_ref[0, k * ROUTE_TILE + t] * buf[pl.ds(off, SUBLANES), :]
            acc = v if acc is None else acc + v
        ybuf[pl.ds(pl.multiple_of(t * SUBLANES, SUBLANES), SUBLANES), :] = acc
        return c

    lax.fori_loop(0, ROUTE_TILE, body, 0, unroll=16)
    o_ref[...] = x1_ref[...] + _from_row_tiles(ybuf, ROUTE_TILE)


def _combine(runs, off, gates, rows, x1, tile_base):
    assert tile_base % 2 == 0
    t = x1.shape[0]
    tile_rows = ROUTE_TILE * SUBLANES
    smem_tile = pl.BlockSpec((None, 1, TOP_K * ROUTE_TILE), lambda i, *_: (tile_base + i, 0, 0),
                             memory_space=pltpu.SMEM)
    grid_spec = pltpu.PrefetchScalarGridSpec(
        num_scalar_prefetch=3,
        grid=(t // ROUTE_TILE,),
        in_specs=[smem_tile, smem_tile, pl.BlockSpec(memory_space=pl.ANY),
                  pl.BlockSpec((ROUTE_TILE, D_MODEL), lambda i, *_: (i, 0))],
        out_specs=pl.BlockSpec((ROUTE_TILE, D_MODEL), lambda i, *_: (i, 0)),
        scratch_shapes=[pltpu.VMEM((2 * _SLOT_ROWS, 128), F32), pltpu.VMEM((tile_rows, 128), F32),
                        pltpu.SemaphoreType.DMA((2,))],
    )
    return pl.pallas_call(
        functools.partial(_combine_kernel, tile_base=tile_base),
        grid_spec=grid_spec,
        out_shape=jax.ShapeDtypeStruct((t, D_MODEL), F32),
        compiler_params=_cparams("arbitrary"),
        name="combine",
    )(*runs, off, gates, rows, x1)


def _route(before, cnt, loc, n_tokens):
    n_assign = n_tokens * TOP_K
    table = lambda a: a[:, :, 0].astype(jnp.int32)
    before, cnt, run_loc = table(before), table(cnt), table(loc)
    counts = before[-1] + cnt[-1]
    padded = ((counts + MOE_TILE - 1) // MOE_TILE) * MOE_TILE
    pend = jnp.cumsum(padded)
    pstart = pend - padded
    run_src = pstart[None, :] + before
    nblk = (n_assign + MOE_TILE - 1) // MOE_TILE + N_EXPERTS
    n_used = (pend[-1] // MOE_TILE).astype(jnp.int32)
    block_start = jnp.arange(nblk, dtype=jnp.int32) * MOE_TILE
    block_e = jnp.sum(pend[None, :] <= jnp.minimum(block_start, pend[-1] - 1)[:, None], axis=1)
    block_e = jnp.minimum(block_e, N_EXPERTS - 1).astype(jnp.int32)
    region_end = jnp.sum(jnp.where(block_e[:, None] == jnp.arange(N_EXPERTS)[None, :], (pstart + counts)[None, :], 0),
                         axis=1)
    n_valid = jnp.clip(region_end - block_start, 0, MOE_TILE).astype(jnp.int32)
    ids = jnp.arange(N_EXPERTS, dtype=jnp.int32)
    nonempty = counts > 0
    next_e = jnp.min(jnp.where((ids[None, :] > ids[:, None]) & nonempty[None, :], ids[None, :], N_EXPERTS), axis=1)
    next_e = jnp.where(next_e == N_EXPERTS, -1, next_e).astype(jnp.int32)
    e_slot = ((jnp.cumsum(nonempty.astype(jnp.int32)) - nonempty.astype(jnp.int32)) % 2).astype(jnp.int32)
    blocks = (block_e, n_used.reshape(1), n_valid, next_e, e_slot)
    runs = (run_src.reshape(-1), cnt.reshape(-1), run_loc.reshape(-1))
    return runs, blocks, pend, padded - counts, nblk * MOE_TILE


def kernel(x_prompt, x_sample, cache_k, cache_v, state_ssm_re, state_ssm_im, norm_mix_g, w_in, q_norm_g, k_norm_g, attn_sinks, ssm_lambda_re, ssm_lambda_im, ssm_b_re, ssm_b_im, ssm_c_re, ssm_c_im, ssm_d, ssm_log_dt, w_glu, b_glu, attn_out_norm_g, ssm_out_norm_g, w_out, norm_ffn_g, w_router, b_router, w_gate_up, b_gate_up, w_down, b_down):
    depth = w_in.shape[0]
    assert depth == 1
    bp, sp, _ = x_prompt.shape
    bs, ss, _ = x_sample.shape
    tp, ts = bp * sp, bs * ss
    assert bp == SUBLANES and sp % S5_CHUNK == 0 and bs % SUBLANES == 0

    perm = jnp.asarray(_Q_PERM)
    w_in0 = w_in[0]
    wq = w_in0[:, :D_ATTN].reshape(D_MODEL, N_HEADS, HEAD_DIM)[:, perm].reshape(D_MODEL, D_ATTN)
    w_in_b = jnp.concatenate([wq, w_in0[:, D_ATTN:]], axis=1).astype(BF16)
    qkg = jnp.concatenate([jnp.tile(q_norm_g[0], N_HEADS), jnp.tile(k_norm_g[0], N_KV_HEADS)])[None]
    pmat = jnp.kron(jnp.eye(256 // HEAD_DIM, dtype=F32),
                    jnp.full((HEAD_DIM, HEAD_DIM), 1.0 / HEAD_DIM, F32)).astype(BF16)
    g_mix = norm_mix_g[0][None]
    sinks = attn_sinks[0]
    g_attn = attn_out_norm_g[0].reshape(N_HEADS, HEAD_DIM)[perm].reshape(1, D_ATTN)
    w_out0 = w_out[0]
    w_out_a = w_out0[:D_ATTN].reshape(N_HEADS, HEAD_DIM, D_MODEL)[perm].reshape(D_ATTN, D_MODEL).astype(BF16)
    w_out_s = w_out0[D_ATTN:].astype(BF16)
    g_ssm = ssm_out_norm_g[0][None]
    g_ffn = norm_ffn_g[0][None]
    w_rt = w_router[0].T.astype(BF16)
    b_rt = jnp.broadcast_to(b_router[0][:, None], (N_EXPERTS, ROUTE_TILE))

    a_re, a_im, bb_re, bb_im = _s5_prep(
        ssm_lambda_re[0], ssm_lambda_im[0], ssm_log_dt[0][:, None],
        jnp.swapaxes(ssm_b_re[0], 1, 2), jnp.swapaxes(ssm_b_im[0], 1, 2))
    chan_g = jnp.arange(D_SSM)[:, None] // SSM_GROUP
    state_g = jnp.arange(D_STATE)[None, :] // SSM_STATE
    bd_b = lambda bb: jnp.where(chan_g == state_g, jnp.tile(bb.reshape(D_SSM, SSM_STATE), (1, N_SSM_GROUPS)), 0.0)
    wb = jnp.concatenate([bd_b(bb_re), bd_b(bb_im)], axis=1).astype(BF16)
    bd_c = lambda c: jnp.where(state_g.T == chan_g.T,
                               jnp.tile(jnp.swapaxes(c, 1, 2).reshape(D_STATE, SSM_GROUP), (1, N_SSM_GROUPS)), 0.0)
    wc = jnp.concatenate([bd_c(ssm_c_re[0]), -bd_c(ssm_c_im[0])], axis=0).astype(BF16)
    a_re, a_im = a_re.reshape(1, D_STATE), a_im.reshape(1, D_STATE)
    d_skip = ssm_d[0].reshape(1, D_SSM)
    w_glu_b = w_glu[0].astype(BF16)
    b_glu0 = b_glu[0][None]

    b_g = b_gate_up[0][:, None, 0::2]
    b_l = b_gate_up[0][:, None, 1::2]
    b_d = b_down[0][:, None, :]
    idx = jnp.arange(256)
    deint = (idx[None, :] == jnp.where(idx % 2 == 0, idx // 2, 128 + idx // 2)[:, None]).astype(BF16)

    xp2 = x_prompt.reshape(tp, D_MODEL)
    xs2 = x_sample.reshape(ts, D_MODEL)
    qp, kp, vp, up = _in_proj(xp2, g_mix, w_in_b, qkg, pmat)
    qs, ks, vs, us = _in_proj(xs2, g_mix, w_in_b, qkg, pmat)

    kp3, vp3 = kp.reshape(bp, sp, D_KV), vp.reshape(bp, sp, D_KV)
    attn_p = _band_attention(sinks, qp.reshape(bp, sp, D_ATTN), kp3, vp3, g_attn).reshape(tp, D_ATTN)
    ck = cache_k[0].reshape(bs, -1, D_KV)
    cv = cache_v[0].reshape(bs, -1, D_KV)
    attn_s, new_ck, new_cv = _cache_attention(sinks, qs, ks, vs, ck, cv, g_attn, ss)

    zeros_p = jnp.zeros((bp, D_STATE), F32)
    s5_args = (a_re, a_im, wb, wc, d_skip, w_glu_b, b_glu0, g_ssm)
    ssm_p, hr_p, hi_p = _s5(up.reshape(bp, sp, D_SSM), zeros_p, zeros_p, *s5_args,
                            tt=S5_CHUNK, time_chunked=True)
    ssm_s, hr_s, hi_s = _s5(us.reshape(bs // SUBLANES, SUBLANES * ss, D_SSM),
                            state_ssm_re[0].reshape(bs, D_STATE), state_ssm_im[0].reshape(bs, D_STATE),
                            *s5_args, tt=ss, time_chunked=False)

    triu = (jnp.arange(ROUTE_TILE)[:, None] < jnp.arange(ROUTE_TILE)[None, :]).astype(BF16)
    low = (jnp.arange(N_EXPERTS)[:, None] > jnp.arange(N_EXPERTS)[None, :]).astype(BF16)
    proj_args = (w_out_a, w_out_s, g_ffn, w_rt, b_rt, triu, low)
    x1p, *tab_p, counts_p = _out_proj(attn_p, ssm_p.reshape(tp, D_SSM), xp2, *proj_args,
                                      jnp.zeros((N_EXPERTS, 128), F32), 0)
    x1s, *tab_s, _ = _out_proj(attn_s, ssm_s.reshape(ts, D_SSM), xs2, *proj_args, counts_p, tp // ROUTE_TILE)

    gates, off, before, cnt, loc = (jnp.concatenate([p, s_], axis=0) for p, s_ in zip(tab_p, tab_s))
    runs, blocks, pend, pad, n_rows = _route(before, cnt, loc, tp + ts)
    x_rows = _dispatch(runs, pend, pad, blocks[1], off, x1p, x1s, g_ffn, n_rows)
    out_rows = _moe(*blocks, x_rows, w_gate_up[0], b_g, b_l, w_down[0], b_d, deint)
    yp = _combine(runs, off, gates, out_rows, x1p, 0).reshape(bp, sp, D_MODEL)
    ys = _combine(runs, off, gates, out_rows, x1s, tp // ROUTE_TILE).reshape(bs, ss, D_MODEL)

    kv5 = lambda a, b_: a.reshape(b_, -1, N_KV_HEADS, HEAD_DIM)
    new_kp = kv5(kp3[:, -WINDOW:], bp)[None]
    new_vp = kv5(vp3[:, -WINDOW:], bp)[None]
    new_ks = kv5(new_ck, bs)[None]
    new_vs = kv5(new_cv, bs)[None]
    st = lambda h, b_: h.reshape(1, b_, N_SSM_GROUPS, SSM_STATE)
    return (yp, ys, new_kp, new_vp, st(hr_p, bp), st(hi_p, bp),
            new_ks, new_vs, st(hr_s, bs), st(hi_s, bs))
```
